```python
import math
import jax
import jax.numpy as jnp
from jax import lax
import numpy as np

D_MODEL = 1024
BATCH = 32
SEQ = 256
DEPTH = 2
DEC_BATCH = 4
DEC_SEQ = 1024
PAST_LEN = 512

GRID_W = 64
EPS = 1e-6
N_ADA = 6
SSD_HEADS = 16
SSD_HEAD_DIM = 64
SSD_INNER = SSD_HEADS * SSD_HEAD_DIM
SSD_GROUPS = 2
SSD_STATE = 128
SSD_CONV = 5
SSD_CHUNK = 128
SSD_CONV_CH = SSD_INNER + 2 * SSD_GROUPS * SSD_STATE
GLA_HEADS = 4
GLA_KEY_DIM = 128
GLA_VAL_DIM = 256
GLA_GATE_RANK = 16
GLA_GATE_TAU = 16.0
GLA_CHUNK = 16
GLA_QK = GLA_HEADS * GLA_KEY_DIM
GLA_V = GLA_HEADS * GLA_VAL_DIM
L0_SPLITS = (SSD_INNER, SSD_CONV_CH, 2 * SSD_HEADS, GLA_QK, GLA_QK, GLA_V, GLA_V, 2 * GLA_GATE_RANK)
L0_IN = SSD_INNER + SSD_CONV_CH + 2 * SSD_HEADS + 2 * GLA_QK + 2 * GLA_V + 2 * GLA_GATE_RANK
L0_MIX = SSD_INNER + GLA_V
ATT_HEADS = 16
ATT_KV_HEADS = 4
ATT_GROUP = ATT_HEADS // ATT_KV_HEADS
ATT_HEAD_DIM = 64
ATT_QKV = (ATT_HEADS + 2 * ATT_KV_HEADS) * ATT_HEAD_DIM
WINDOW = 128
ATT_BLOCK = 128
ROPE_THETA = 10000.0
N_EXPERTS = 32
TOP_K = 4
EXPERT_FF = 1024
SWIGLU_LIMIT = 7.0
SWIGLU_ALPHA = 1.702
MOE_BLOCK = 128

kernel_name = 'hybrid_ssd_gla_swa_moe_prefix_step'


def rms_norm(x, w):
    xf = x.astype(jnp.float32)
    y = xf * lax.rsqrt(jnp.mean(xf * xf, axis=-1, keepdims=True) + EPS)
    return (y * w.astype(jnp.float32)).astype(x.dtype)


def ada_modulation(cond, w, b):
    mod = jax.nn.silu(cond) @ w + b
    return jnp.split(mod[:, None, :], N_ADA, axis=-1)


def centred_depthwise_conv(x, w, b):
    pad = w.shape[0] // 2
    y = lax.conv_general_dilated(x, w[:, None, :], window_strides=(1,), padding=[(pad, pad)],
                                 dimension_numbers=('NWC', 'WIO', 'NWC'),
                                 feature_group_count=x.shape[-1])
    return y + b


def segsum(a):
    t = a.shape[-1]
    cs = jnp.cumsum(a, axis=-1)
    diff = cs[..., :, None] - cs[..., None, :]
    return jnp.where(jnp.tril(jnp.ones((t, t), bool)), diff, -jnp.inf)


def ssd_scan(x, dt, a, bm, cm, s0):
    b, L, h, p = x.shape
    q = SSD_CHUNK
    nc = L // q
    xdt = (x.astype(jnp.float32) * dt[..., None]).reshape(b, nc, q, h, p)
    bc = bm.reshape(b, nc, q, h, -1)
    cc = cm.reshape(b, nc, q, h, -1)
    ad = (dt * a).reshape(b, nc, q, h).transpose(0, 3, 1, 2)
    a_cs = jnp.cumsum(ad, axis=-1)
    scores = jnp.einsum('bclhn,bcshn->bhcls', cc, bc) * jnp.exp(segsum(ad))
    y_diag = jnp.einsum('bhcls,bcshp->bclhp', scores, xdt)
    decay_to_end = jnp.exp(a_cs[..., -1:] - a_cs)
    chunk_states = jnp.einsum('bclhn,bhcl,bclhp->bchpn', bc, decay_to_end, xdt)
    chunk_decay = jnp.exp(a_cs[..., -1])

    def step(s, inp):
        st, dec = inp
        return dec[..., None, None] * s + st, s

    s_final, s_in = lax.scan(step, s0.astype(jnp.float32),
                             (chunk_states.swapaxes(0, 1), chunk_decay.transpose(2, 0, 1)))
    y_off = jnp.einsum('bclhn,cbhpn,bhcl->bclhp', cc, s_in, jnp.exp(a_cs))
    y = (y_diag + y_off).reshape(b, L, h, p)
    return y.astype(x.dtype), s_final.astype(x.dtype)


def gla_scan(q, k, v, log_a, s0):
    b, L, h, dk = q.shape
    dv = v.shape[-1]
    t = GLA_CHUNK
    nc = L // t
    qc = q.astype(jnp.float32).reshape(b, nc, t, h, dk)
    kc = k.astype(jnp.float32).reshape(b, nc, t, h, dk)
    vc = v.astype(jnp.float32).reshape(b, nc, t, h, dv)
    g = jnp.cumsum(log_a.reshape(b, nc, t, h, dk), axis=2)
    mask = jnp.tril(jnp.ones((t, t), bool))[None, None, :, :, None, None]
    decay = jnp.exp(jnp.where(mask, g[:, :, :, None] - g[:, :, None, :], -jnp.inf))
    scores = jnp.einsum('bclhd,bcshd,bclshd->bchls', qc, kc, decay)
    o_intra = jnp.einsum('bchls,bcshe->bclhe', scores, vc)
    q_dec = qc * jnp.exp(g)
    k_dec = kc * jnp.exp(g[:, :, -1:] - g)
    chunk_states = jnp.einsum('bclhd,bclhe->bchde', k_dec, vc)
    chunk_decay = jnp.exp(g[:, :, -1])

    def step(s, inp):
        st, dec = inp
        return dec[..., None] * s + st, s

    s_final, s_in = lax.scan(step, s0.astype(jnp.float32),
                             (chunk_states.swapaxes(0, 1), chunk_decay.swapaxes(0, 1)))
    o_inter = jnp.einsum('bclhd,cbhde->bclhe', q_dec, s_in)
    o = (o_intra + o_inter).reshape(b, L, h, dv)
    return o.astype(q.dtype), s_final.astype(q.dtype)


def ssd_gla_mixer(h, init, w_in, conv_w, conv_b, a_log, dt_bias, d_skip, ssd_norm,
                  gate_w2, gate_b, gla_norm, w_out):
    b, L, _ = h.shape
    flip = lambda t: jnp.flip(t, axis=1)
    proj = h @ w_in
    z, xbc, dt_raw, q, k, v, og, gate_lr = jnp.split(proj, np.cumsum(L0_SPLITS)[:-1].tolist(), axis=-1)
    xbc = jax.nn.silu(centred_depthwise_conv(xbc, conv_w, conv_b))
    xs, bm, cm = jnp.split(xbc, [SSD_INNER, SSD_INNER + SSD_GROUPS * SSD_STATE], axis=-1)
    xs = xs.reshape(b, L, SSD_HEADS, SSD_HEAD_DIM)
    rep = SSD_HEADS // SSD_GROUPS
    bm = jnp.repeat(bm.reshape(b, L, SSD_GROUPS, SSD_STATE), rep, axis=2)
    cm = jnp.repeat(cm.reshape(b, L, SSD_GROUPS, SSD_STATE), rep, axis=2)
    dt = jax.nn.softplus(dt_raw.reshape(b, L, 2, SSD_HEADS).astype(jnp.float32) + dt_bias.astype(jnp.float32))
    a = -jnp.exp(a_log.astype(jnp.float32))
    y_f, s_f = ssd_scan(xs, dt[:, :, 0], a[0], bm, cm, init[0])
    y_b, s_b = ssd_scan(flip(xs), flip(dt[:, :, 1]), a[1], flip(bm), flip(cm), init[1])
    y = y_f + flip(y_b) + d_skip[:, None] * xs
    y = rms_norm((y * jax.nn.silu(z.reshape(b, L, SSD_HEADS, SSD_HEAD_DIM))).reshape(b, L, SSD_INNER), ssd_norm)
    q = q.reshape(b, L, GLA_HEADS, GLA_KEY_DIM) * (GLA_KEY_DIM ** -0.5)
    k = k.reshape(b, L, GLA_HEADS, GLA_KEY_DIM)
    v = v.reshape(b, L, GLA_HEADS, GLA_VAL_DIM)
    gate_pre = jnp.einsum('bldr,drk->bldk', gate_lr.reshape(b, L, 2, GLA_GATE_RANK), gate_w2) + gate_b
    log_a = (jax.nn.log_sigmoid(gate_pre.astype(jnp.float32)) / GLA_GATE_TAU).reshape(b, L, 2, GLA_HEADS, GLA_KEY_DIM)
    o_f, g_f = gla_scan(q, k, v, log_a[:, :, 0], init[2])
    o_b, g_b = gla_scan(flip(q), flip(k), flip(v), flip(log_a[:, :, 1]), init[3])
    o = rms_norm(o_f + flip(o_b), gla_norm) * jax.nn.silu(og.reshape(b, L, GLA_HEADS, GLA_VAL_DIM))
    mixed = jnp.concatenate([y.astype(h.dtype), o.reshape(b, L, GLA_V).astype(h.dtype)], axis=-1)
    return mixed @ w_out, (s_f, s_b, g_f, g_b)


def axial_rope(x):
    L = x.shape[1]
    n_rows = L // GRID_W
    rows = jnp.repeat(jnp.arange(n_rows, dtype=jnp.float32), GRID_W)
    cols = jnp.tile(jnp.arange(GRID_W, dtype=jnp.float32), n_rows)
    axis_dim = ATT_HEAD_DIM // 2
    n_freq = axis_dim // 2
    inv_freq = ROPE_THETA ** (-jnp.arange(n_freq, dtype=jnp.float32) / n_freq)

    def rotate(xa, pos):
        ang = pos[:, None] * inv_freq[None, :]
        cos = jnp.cos(ang)[None, :, None, :]
        sin = jnp.sin(ang)[None, :, None, :]
        x1 = xa[..., :n_freq].astype(jnp.float32)
        x2 = xa[..., n_freq:].astype(jnp.float32)
        return jnp.concatenate([x1 * cos - x2 * sin, x2 * cos + x1 * sin], axis=-1)

    out = jnp.concatenate([rotate(x[..., :axis_dim], rows), rotate(x[..., axis_dim:], cols)], axis=-1)
    return out.astype(x.dtype)


def attn_project(h, w_qkv, q_norm, k_norm):
    b, L, _ = h.shape
    q, k, v = jnp.split(h @ w_qkv, [ATT_HEADS * ATT_HEAD_DIM, (ATT_HEADS + ATT_KV_HEADS) * ATT_HEAD_DIM], axis=-1)
    q = rms_norm(q.reshape(b, L, ATT_HEADS, ATT_HEAD_DIM), q_norm)
    k = rms_norm(k.reshape(b, L, ATT_KV_HEADS, ATT_HEAD_DIM), k_norm)
    return q, k, v.reshape(b, L, ATT_KV_HEADS, ATT_HEAD_DIM)


def sink_attention(q, k, v, mask, sinks):
    b, nq, nkv, ng, _ = q.shape
    s = jnp.einsum('bqkgd,bskd->bkgqs', q, k).astype(jnp.float32) * (ATT_HEAD_DIM ** -0.5)
    if mask is not None:
        s = jnp.where(mask, s, -jnp.inf)
    sink = jnp.broadcast_to(sinks.astype(jnp.float32).reshape(1, nkv, ng, 1, 1), (b, nkv, ng, nq, 1))
    p = jax.nn.softmax(jnp.concatenate([s, sink], axis=-1), axis=-1)[..., :-1]
    return jnp.einsum('bkgqs,bskd->bqkgd', p.astype(v.dtype), v)


def attn_context(h, w_qkv, q_norm, k_norm, sinks, w_out):
    b, L, _ = h.shape
    q, k, v = attn_project(h, w_qkv, q_norm, k_norm)
    nb = L // ATT_BLOCK
    qb = q.reshape(b, nb, ATT_BLOCK, ATT_KV_HEADS, ATT_GROUP, ATT_HEAD_DIM).swapaxes(0, 1)
    o = lax.map(lambda qi: sink_attention(qi, k, v, None, sinks), qb)
    o = o.swapaxes(0, 1).reshape(b, L, ATT_HEADS * ATT_HEAD_DIM)
    return o @ w_out, (k, v)


def attn_latent(h, ctx_k, ctx_v, w_qkv, q_norm, k_norm, sinks, w_out):
    b, L, _ = h.shape
    q, k, v = attn_project(h, w_qkv, q_norm, k_norm)
    q = axial_rope(q)
    k = axial_rope(k)
    ctx_k = ctx_k.astype(k.dtype)
    ctx_v = ctx_v.astype(v.dtype)
    n_ctx = ctx_k.shape[1]
    pad = jnp.zeros((b, ATT_BLOCK, ATT_KV_HEADS, ATT_HEAD_DIM), k.dtype)
    k_pad = jnp.concatenate([pad, k, pad], axis=1)
    v_pad = jnp.concatenate([pad.astype(v.dtype), v, pad.astype(v.dtype)], axis=1)
    nb = L // ATT_BLOCK
    qb = q.reshape(b, nb, ATT_BLOCK, ATT_KV_HEADS, ATT_GROUP, ATT_HEAD_DIM).swapaxes(0, 1)
    q_off = jnp.arange(ATT_BLOCK)
    k_off = jnp.arange(3 * ATT_BLOCK) - ATT_BLOCK
    ctx_mask = jnp.ones((ATT_BLOCK, n_ctx), bool)

    def block(args):
        i, qi = args
        start = i * ATT_BLOCK
        kb = lax.dynamic_slice_in_dim(k_pad, start, 3 * ATT_BLOCK, axis=1)
        vb = lax.dynamic_slice_in_dim(v_pad, start, 3 * ATT_BLOCK, axis=1)
        qpos = start + q_off
        kpos = start + k_off
        band = (jnp.abs(qpos[:, None] - kpos[None, :]) <= WINDOW) & (kpos[None, :] >= 0) & (kpos[None, :] < L)
        mask = jnp.concatenate([band, ctx_mask], axis=1)
        return sink_attention(qi, jnp.concatenate([kb, ctx_k], axis=1), jnp.concatenate([vb, ctx_v], axis=1), mask, sinks)

    o = lax.map(block, (jnp.arange(nb), qb))
    o = o.swapaxes(0, 1).reshape(b, L, ATT_HEADS * ATT_HEAD_DIM)
    return o @ w_out


def moe_ffn(x, router_w, router_b, w_gate, b_gate, w_up, b_up, w_down, b_down):
    shape = x.shape
    xt = x.reshape(-1, shape[-1])
    n_tok = xt.shape[0]
    logits = (xt @ router_w + router_b).astype(jnp.float32)
    top_val, top_idx = lax.top_k(logits, TOP_K)
    gates = jax.nn.softmax(top_val, axis=-1)
    n_assign = n_tok * TOP_K
    flat_e = top_idx.reshape(-1)
    order = jnp.argsort(flat_e)
    e_sorted = flat_e[order]
    tok_sorted = (order // TOP_K).astype(jnp.int32)
    counts = jnp.bincount(flat_e, length=N_EXPERTS)
    padded = (counts + MOE_BLOCK - 1) // MOE_BLOCK * MOE_BLOCK
    start = jnp.cumsum(counts) - counts
    padded_end = jnp.cumsum(padded)
    padded_start = padded_end - padded
    dest = padded_start[e_sorted] + jnp.arange(n_assign) - start[e_sorted]
    n_rows = -(-n_assign // MOE_BLOCK) * MOE_BLOCK + N_EXPERTS * MOE_BLOCK
    n_blocks = n_rows // MOE_BLOCK
    row_tok = jnp.full((n_rows,), n_tok, jnp.int32).at[dest].set(tok_sorted)
    x_rows = jnp.concatenate([xt, jnp.zeros((1, xt.shape[1]), xt.dtype)], axis=0)[row_tok]
    blk_expert = jnp.minimum(jnp.searchsorted(padded_end, jnp.arange(n_blocks) * MOE_BLOCK, side='right'), N_EXPERTS - 1)

    def expert_block(args):
        xb, e = args
        gt = jnp.minimum(xb @ w_gate[e] + b_gate[e], SWIGLU_LIMIT)
        up = jnp.clip(xb @ w_up[e] + b_up[e], -SWIGLU_LIMIT, SWIGLU_LIMIT)
        return ((up + 1.0) * gt * jax.nn.sigmoid(SWIGLU_ALPHA * gt)) @ w_down[e] + b_down[e]

    y_rows = lax.map(expert_block, (x_rows.reshape(n_blocks, MOE_BLOCK, -1), blk_expert)).reshape(n_rows, -1)
    w_sorted = gates.reshape(-1)[order]
    contrib = y_rows[dest] * w_sorted[:, None].astype(y_rows.dtype)
    y = jnp.zeros((n_tok, contrib.shape[1]), contrib.dtype).at[tok_sorted].add(contrib)
    return y.reshape(shape).astype(x.dtype)


def residual_block(x, mod, g_norm1, g_norm2, mixer, moe_params):
    shift1, scale1, gate1, shift2, scale2, gate2 = mod
    mix, ctx_side = mixer(rms_norm(x, g_norm1) * (1.0 + scale1) + shift1)
    x = x + gate1 * mix
    x = x + gate2 * moe_ffn(rms_norm(x, g_norm2) * (1.0 + scale2) + shift2, *moe_params)
    return x, ctx_side


def setup_inputs(seed: int = 0) -> dict:
    key = jax.random.key(seed)
    keys = iter(jax.random.split(key, 64))

    def nrm(shape, scale):
        return jax.random.normal(next(keys), shape, jnp.float32) * scale

    def gain(shape):
        return 1.0 + nrm(shape, 0.01)

    D = D_MODEL
    dt_init = jnp.exp(jax.random.uniform(next(keys), (2, SSD_HEADS), jnp.float32, math.log(1e-3), math.log(1e-1)))
    a_init = jax.random.uniform(next(keys), (2, SSD_HEADS), jnp.float32, 1.0, 16.0)
    return {
        'x_prompt': nrm((BATCH, SEQ, D), 1.0),
        'x_sample': nrm((DEC_BATCH, DEC_SEQ, D), 1.0),
        'state_l0_ssd_fwd': nrm((DEC_BATCH, SSD_HEADS, SSD_HEAD_DIM, SSD_STATE), 0.1),
        'state_l0_ssd_bwd': nrm((DEC_BATCH, SSD_HEADS, SSD_HEAD_DIM, SSD_STATE), 0.1),
        'state_l0_gla_fwd': nrm((DEC_BATCH, GLA_HEADS, GLA_KEY_DIM, GLA_VAL_DIM), 0.1),
        'state_l0_gla_bwd': nrm((DEC_BATCH, GLA_HEADS, GLA_KEY_DIM, GLA_VAL_DIM), 0.1),
        'cache_l1_k': nrm((DEC_BATCH, PAST_LEN, ATT_KV_HEADS, ATT_HEAD_DIM), 1.0),
        'cache_l1_v': nrm((DEC_BATCH, PAST_LEN, ATT_KV_HEADS, ATT_HEAD_DIM), 1.0),
        'c': nrm((DEC_BATCH, D), 1.0),
        'c_ctx': nrm((D,), 1.0),
        'ada_w': nrm((DEPTH, D, N_ADA * D), 0.5 * D ** -0.5),
        'ada_b': nrm((DEPTH, N_ADA * D), 0.02),
        'norm1': gain((DEPTH, D)),
        'norm2': gain((DEPTH, D)),
        'l0_w_in': nrm((D, L0_IN), D ** -0.5),
        'l0_conv_w': nrm((SSD_CONV, SSD_CONV_CH), SSD_CONV ** -0.5),
        'l0_conv_b': nrm((SSD_CONV_CH,), 0.01),
        'l0_a_log': jnp.log(a_init),
        'l0_dt_bias': dt_init + jnp.log(-jnp.expm1(-dt_init)),
        'l0_d_skip': gain((SSD_HEADS,)),
        'l0_ssd_norm': gain((SSD_INNER,)),
        'l0_gate_w2': nrm((2, GLA_GATE_RANK, GLA_QK), GLA_GATE_RANK ** -0.5),
        'l0_gate_b': nrm((2, GLA_QK), 0.1),
        'l0_gla_norm': gain((GLA_VAL_DIM,)),
        'l0_w_out': nrm((L0_MIX, D), L0_MIX ** -0.5),
        'l1_w_qkv': nrm((D, ATT_QKV), D ** -0.5),
        'l1_q_norm': gain((ATT_HEAD_DIM,)),
        'l1_k_norm': gain((ATT_HEAD_DIM,)),
        'l1_sinks': nrm((ATT_HEADS,), 1.0),
        'l1_w_out': nrm((ATT_HEADS * ATT_HEAD_DIM, D), (ATT_HEADS * ATT_HEAD_DIM) ** -0.5),
        'router_w': nrm((DEPTH, D, N_EXPERTS), D ** -0.5),
        'router_b': nrm((DEPTH, N_EXPERTS), 0.01),
        'exp_w_gate': nrm((DEPTH, N_EXPERTS, D, EXPERT_FF), D ** -0.5),
        'exp_b_gate': nrm((DEPTH, N_EXPERTS, EXPERT_FF), 0.01),
        'exp_w_up': nrm((DEPTH, N_EXPERTS, D, EXPERT_FF), D ** -0.5),
        'exp_b_up': nrm((DEPTH, N_EXPERTS, EXPERT_FF), 0.01),
        'exp_w_down': nrm((DEPTH, N_EXPERTS, EXPERT_FF, D), EXPERT_FF ** -0.5),
        'exp_b_down': nrm((DEPTH, N_EXPERTS, D), 0.01),
    }


def reference(x_prompt, x_sample, state_l0_ssd_fwd, state_l0_ssd_bwd, state_l0_gla_fwd, state_l0_gla_bwd,
              cache_l1_k, cache_l1_v, c, c_ctx, ada_w, ada_b, norm1, norm2,
              l0_w_in, l0_conv_w, l0_conv_b, l0_a_log, l0_dt_bias, l0_d_skip, l0_ssd_norm,
              l0_gate_w2, l0_gate_b, l0_gla_norm, l0_w_out,
              l1_w_qkv, l1_q_norm, l1_k_norm, l1_sinks, l1_w_out,
              router_w, router_b, exp_w_gate, exp_b_gate, exp_w_up, exp_b_up, exp_w_down, exp_b_down):
    xp = x_prompt
    xs = x_sample
    bp = x_prompt.shape[0]
    l0 = (l0_w_in, l0_conv_w, l0_conv_b, l0_a_log, l0_dt_bias, l0_d_skip, l0_ssd_norm,
          l0_gate_w2, l0_gate_b, l0_gla_norm, l0_w_out)
    l1 = (l1_w_qkv, l1_q_norm, l1_k_norm, l1_sinks, l1_w_out)
    for layer in range(DEPTH):
        moe_p = (router_w[layer], router_b[layer], exp_w_gate[layer], exp_b_gate[layer],
                 exp_w_up[layer], exp_b_up[layer], exp_w_down[layer], exp_b_down[layer])
        mod_ctx = ada_modulation(c_ctx[None, :], ada_w[layer], ada_b[layer])
        mod_lat = ada_modulation(c, ada_w[layer], ada_b[layer])
        if layer % 2 == 0:
            zero_init = (jnp.zeros((bp, SSD_HEADS, SSD_HEAD_DIM, SSD_STATE), jnp.float32),
                         jnp.zeros((bp, SSD_HEADS, SSD_HEAD_DIM, SSD_STATE), jnp.float32),
                         jnp.zeros((bp, GLA_HEADS, GLA_KEY_DIM, GLA_VAL_DIM), jnp.float32),
                         jnp.zeros((bp, GLA_HEADS, GLA_KEY_DIM, GLA_VAL_DIM), jnp.float32))
            xp, (ssd_f, ssd_b, gla_f, gla_b) = residual_block(
                xp, mod_ctx, norm1[layer], norm2[layer],
                lambda h: ssd_gla_mixer(h, zero_init, *l0), moe_p)
            cached = (state_l0_ssd_fwd, state_l0_ssd_bwd, state_l0_gla_fwd, state_l0_gla_bwd)
            xs, _ = residual_block(
                xs, mod_lat, norm1[layer], norm2[layer],
                lambda h: ssd_gla_mixer(h, cached, *l0), moe_p)
        else:
            xp, (ctx_k, ctx_v) = residual_block(
                xp, mod_ctx, norm1[layer], norm2[layer],
                lambda h: attn_context(h, *l1), moe_p)
            xs, _ = residual_block(
                xs, mod_lat, norm1[layer], norm2[layer],
                lambda h: (attn_latent(h, cache_l1_k, cache_l1_v, *l1), None), moe_p)
    return (xp, xs, ssd_f, ssd_b, gla_f, gla_b, ctx_k, ctx_v)
```

```python
import functools
import math

import numpy as np
import jax
import jax.numpy as jnp
from jax import lax
from jax.experimental import pallas as pl
from jax.experimental.pallas import tpu as pltpu

F32 = jnp.float32
BF16 = jnp.bfloat16
I32 = jnp.int32
HI = lax.Precision.HIGHEST

D = 1024
EPS = 1e-6
N_ADA = 6
SSD_HEADS = 16
SSD_HEAD_DIM = 64
SSD_INNER = 1024
SSD_STATE = 128
SSD_GROUPS = 2
SSD_CONV = 5
SSD_CHUNK = 128
GLA_HEADS = 4
GLA_KEY_DIM = 128
GLA_VAL_DIM = 256
GLA_RANK = 16
GLA_TAU = 16.0
GLA_BLOCK = 64
ATT_HEADS = 16
ATT_KV = 4
ATT_HD = 64
ATT_BLOCK = 128
WINDOW = 128
GRID_W = 64
ROPE_THETA = 10000.0
N_EXPERTS = 32
TOP_K = 4
EXPERT_FF = 1024
SWIGLU_LIMIT = 7.0
SWIGLU_ALPHA = 1.702
MOE_BLOCK = 128
TOK_TILE = 256
LANES = 128
NEG = -1e30

PJ_Z, PJ_V, PJ_OG, PJ_XBC, PJ_Q, PJ_K = 0, 1024, 2048, 3072, 4608, 5120
PJ_W = 5632
VMEM_LIMIT = 48 * 1024 * 1024


def _cp(sem, vmem=VMEM_LIMIT):
    return pltpu.CompilerParams(dimension_semantics=sem, vmem_limit_bytes=vmem)


class Layout:
    def __init__(self, n_prompt, prompt_len, n_sample, sample_len):
        self.n_prompt, self.prompt_len = n_prompt, prompt_len
        self.n_sample, self.sample_len = n_sample, sample_len
        self.p_tok = n_prompt * prompt_len
        self.n_tok = self.p_tok + n_sample * sample_len
        self.seqs = [(i * prompt_len, prompt_len) for i in range(n_prompt)]
        self.seqs += [(self.p_tok + i * sample_len, sample_len) for i in range(n_sample)]
        self.n_seq = len(self.seqs)

    def mod_row(self, start):
        return jnp.where(start < self.p_tok, 0, 1 + (start - self.p_tok) // self.sample_len)

    def scan_tables(self, blk):
        rows = []
        for sid, (st, ln) in enumerate(self.seqs):
            nc = ln // blk
            b0 = st // blk
            for ph in (0, 1):
                order = range(nc) if ph == 0 else range(nc - 1, -1, -1)
                for n, c in enumerate(order):
                    yb = b0 + nc - 1 if ph == 0 else b0 + c
                    rows.append((b0 + c, sid, ph, c, int(n == 0), int(n == nc - 1), yb))
        return [jnp.asarray(np.array(col, np.int32)) for col in zip(*rows)]

    def tile_flags(self, tile):
        first, last = [], []
        for st, ln in self.seqs:
            n = ln // tile
            first += [1] + [0] * (n - 1)
            last += [0] * (n - 1) + [1]
        return jnp.asarray(np.array(first, np.int32)), jnp.asarray(np.array(last, np.int32))


def _sigmoid(x):
    return 1.0 / (1.0 + jnp.exp(-x))


def _silu(x):
    return x * _sigmoid(x)


def _softplus(x):
    return jnp.maximum(x, 0.0) + jnp.log(1.0 + jnp.exp(-jnp.abs(x)))


def _modnorm(x, g, sc, sh):
    ms = jnp.mean(x * x, axis=-1, keepdims=True)
    return (x * lax.rsqrt(ms + EPS) * g) * (1.0 + sc) + sh


def _dot(a, b, **kw):
    return jnp.dot(a, b, preferred_element_type=F32, **kw)


def _dot_nt(a, b):
    return lax.dot_general(a, b, (((1,), (1,)), ((), ())), preferred_element_type=F32)


def _dot_tn(a, b):
    return lax.dot_general(a, b, (((0,), (0,)), ((), ())), preferred_element_type=F32)


def _ada_kernel(c_ref, w_ref, b_ref, o_ref):
    o_ref[0] = _dot(_silu(c_ref[...]), w_ref[0], precision=HI) + b_ref[0]


def ada_table(cond8, ada_w, ada_b):
    depth, _, n = ada_w.shape
    tn = 1536
    return pl.pallas_call(
        _ada_kernel, grid=(depth, n // tn),
        in_specs=[pl.BlockSpec((8, D), lambda l, j: (0, 0)),
                  pl.BlockSpec((1, D, tn), lambda l, j: (l, 0, j)),
                  pl.BlockSpec((1, 1, tn), lambda l, j: (l, 0, j))],
        out_specs=pl.BlockSpec((1, 8, tn), lambda l, j: (l, 0, j)),
        out_shape=jax.ShapeDtypeStruct((depth, 8, n), F32),
        compiler_params=_cp(("arbitrary", "arbitrary")), name="ada_table",
    )(cond8, ada_w, ada_b.reshape(depth, 1, n))


def _proj_kernel(x_ref, g_ref, sc_ref, sh_ref, w_ref, ws_ref, o_ref, os_ref, h_scr):
    @pl.when(pl.program_id(1) == 0)
    def _():
        h = _modnorm(x_ref[...], g_ref[...], sc_ref[...], sh_ref[...])
        h_scr[...] = h.astype(BF16)
        os_ref[...] = _dot(h, ws_ref[...], precision=HI)

    o_ref[...] = _dot(h_scr[...], w_ref[...]).astype(o_ref.dtype)


def norm_proj(lay, x, g, sc, sh, w, w_small, tm, tn, out_dtype):
    n_tok = x.shape[0]
    n = w.shape[1]
    ns = w_small.shape[1]
    mrow = lambda i, j: (lay.mod_row(i * tm), 0, 0)
    return pl.pallas_call(
        _proj_kernel, grid=(n_tok // tm, n // tn),
        in_specs=[pl.BlockSpec((tm, D), lambda i, j: (i, 0)),
                  pl.BlockSpec((1, D), lambda i, j: (0, 0)),
                  pl.BlockSpec((None, 1, D), mrow),
                  pl.BlockSpec((None, 1, D), mrow),
                  pl.BlockSpec((D, tn), lambda i, j: (0, j)),
                  pl.BlockSpec((D, ns), lambda i, j: (0, 0))],
        out_specs=[pl.BlockSpec((tm, tn), lambda i, j: (i, j)),
                   pl.BlockSpec((tm, ns), lambda i, j: (i, 0))],
        out_shape=[jax.ShapeDtypeStruct((n_tok, n), out_dtype),
                   jax.ShapeDtypeStruct((n_tok, ns), F32)],
        scratch_shapes=[pltpu.VMEM((tm, D), BF16)],
        compiler_params=_cp(("arbitrary", "arbitrary")), name="norm_proj",
    )(x, g.reshape(1, D), sc, sh, w, w_small)


CONV_TILE = 256
CONV_HALO = 16


def _conv_kernel(first_ref, last_ref, prev_ref, cur_ref, next_ref, w_ref, b_ref, o_ref, ext):
    i = pl.program_id(0)
    keep_p = first_ref[i] == 0
    keep_n = last_ref[i] == 0
    h, t = CONV_HALO, CONV_TILE
    ext[0:h, :] = jnp.where(keep_p, prev_ref[...].astype(F32), 0.0)
    ext[h:h + t, :] = cur_ref[...].astype(F32)
    ext[h + t:h + t + h, :] = jnp.where(keep_n, next_ref[...].astype(F32), 0.0)
    pad = SSD_CONV // 2
    acc = jnp.broadcast_to(b_ref[...], (t, b_ref.shape[1]))
    for k in range(SSD_CONV):
        acc = acc + w_ref[k:k + 1, :] * ext[h - pad + k:h - pad + k + t, :]
    o_ref[...] = _silu(acc).astype(o_ref.dtype)


def conv_silu(lay, proj, conv_w, conv_b):
    n_tok = proj.shape[0]
    cw = 512
    nch = conv_w.shape[1] // cw
    c0 = PJ_XBC // cw
    t, h = CONV_TILE, CONV_HALO
    first, last = lay.tile_flags(t)
    r = t // h
    nhb = n_tok // h
    grid_spec = pltpu.PrefetchScalarGridSpec(
        num_scalar_prefetch=2, grid=(n_tok // t, nch),
        in_specs=[pl.BlockSpec((h, cw), lambda i, c, f, l: (jnp.maximum(i * r - 1, 0), c0 + c)),
                  pl.BlockSpec((t, cw), lambda i, c, f, l: (i, c0 + c)),
                  pl.BlockSpec((h, cw), lambda i, c, f, l: (jnp.minimum((i + 1) * r, nhb - 1), c0 + c)),
                  pl.BlockSpec((SSD_CONV, cw), lambda i, c, f, l: (0, c)),
                  pl.BlockSpec((1, cw), lambda i, c, f, l: (0, c))],
        out_specs=pl.BlockSpec((t, cw), lambda i, c, f, l: (i, c)),
        scratch_shapes=[pltpu.VMEM((t + 2 * h, cw), F32)])
    return pl.pallas_call(
        _conv_kernel, grid_spec=grid_spec,
        out_shape=jax.ShapeDtypeStruct((n_tok, conv_w.shape[1]), BF16),
        compiler_params=_cp(("arbitrary", "arbitrary")), name="conv_silu",
    )(first, last, proj, proj, proj, conv_w, conv_b.reshape(1, -1))


def _ssd_dir(d, cpos, first, last, xs_ref, bc_ref, z_ref, dtg_ref, dtgT_ref, s0_ref,
             alog_ref, alogT_ref, dtb_ref, dtbT_ref, dskip_ref, nrm_ref,
             y_ref, sout_ref, S, yf, yt):
    q = SSD_CHUNK
    nh = SSD_HEADS

    @pl.when(first == 1)
    def _():
        S[...] = s0_ref[0].T

    xs = xs_ref[...].astype(F32)
    dt = _softplus(dtg_ref[:, nh * d:nh * d + nh] + dtb_ref[d:d + 1, :])
    dtT = _softplus(dtgT_ref[nh * d:nh * d + nh, :] + dtbT_ref[:, d:d + 1])
    ad = dt * (-jnp.exp(alog_ref[d:d + 1, :]))
    adT = dtT * (-jnp.exp(alogT_ref[:, d:d + 1]))
    row = lax.broadcasted_iota(I32, (q, q), 0)
    col = lax.broadcasted_iota(I32, (q, q), 1)
    if d == 0:
        e = _dot((col <= row).astype(F32), ad, precision=HI)
        eT = _dot(adT, (row <= col).astype(F32), precision=HI)
        tot = e[q - 1:q, :]
        mask = row >= col
        fq = jnp.exp(e)
        fk = jnp.exp(tot - e)
    else:
        e = _dot((col < row).astype(F32), ad, precision=HI)
        eT = _dot(adT, (row < col).astype(F32), precision=HI)
        tot = jnp.sum(ad, axis=0, keepdims=True)
        mask = col >= row
        fq = jnp.exp(tot - e)
        fk = jnp.exp(e)
    dec = jnp.exp(tot)
    lo = lax.broadcasted_iota(I32, (q, LANES), 1) < SSD_HEAD_DIM
    lo1 = lax.broadcasted_iota(I32, (1, LANES), 1) < SSD_HEAD_DIM

    def colpat(arr, a):
        return jnp.where(lo, arr[:, a:a + 1], arr[:, a + 1:a + 2])

    rep = (nh // SSD_GROUPS) // 2
    for g in range(SSD_GROUPS):
        bg = bc_ref[:, SSD_STATE * g:SSD_STATE * (g + 1)]
        cg = bc_ref[:, SSD_STATE * (SSD_GROUPS + g):SSD_STATE * (SSD_GROUPS + g + 1)]
        gmat = _dot_nt(cg, bg)
        for j in range(rep * g, rep * (g + 1)):
            a = 2 * j
            sl = slice(LANES * j, LANES * (j + 1))
            parts = []
            for hh in (a, a + 1):
                if d == 0:
                    diff = e[:, hh:hh + 1] - eT[hh:hh + 1, :]
                else:
                    diff = eT[hh:hh + 1, :] - e[:, hh:hh + 1]
                parts.append((gmat * jnp.exp(jnp.where(mask, diff, NEG))).astype(BF16))
            lhs = jnp.concatenate(parts, axis=1)
            xdt = xs[:, sl] * colpat(dt, a)
            rhs = jnp.concatenate([jnp.where(lo, xdt, 0.0), jnp.where(lo, 0.0, xdt)], axis=0)
            y = _dot(lhs, rhs.astype(BF16))
            sj = S[:, sl]
            y = y + _dot(cg, sj.astype(BF16)) * colpat(fq, a)
            xk = (xdt * colpat(fk, a)).astype(BF16)
            decp = jnp.where(lo1, dec[:, a:a + 1], dec[:, a + 1:a + 2])
            S[:, sl] = sj * decp + _dot_tn(bg, xk)
            if d == 0:
                yf[pl.ds(pl.multiple_of(cpos * q, q), q), sl] = y
            else:
                yt[:, sl] = y

    if d == 1:
        ytot = yf[pl.ds(pl.multiple_of(cpos * q, q), q), :] + yt[...] + dskip_ref[...] * xs
        yg = ytot * _silu(z_ref[...].astype(F32))
        ms = jnp.mean(yg * yg, axis=-1, keepdims=True)
        y_ref[...] = (yg * lax.rsqrt(ms + EPS) * nrm_ref[...]).astype(y_ref.dtype)

    @pl.when(last == 1)
    def _():
        sout_ref[0] = S[...].T


def _ssd_kernel(blk_ref, sid_ref, ph_ref, cpos_ref, first_ref, last_ref, yblk_ref,
                xs_ref, bc_ref, z_ref, dtg_ref, dtgT_ref, s0f_ref, s0b_ref,
                alog_ref, alogT_ref, dtb_ref, dtbT_ref, dskip_ref, nrm_ref,
                y_ref, sf_ref, sb_ref, S, yf, yt):
    s = pl.program_id(0)
    common = (xs_ref, bc_ref, z_ref, dtg_ref, dtgT_ref)
    params = (alog_ref, alogT_ref, dtb_ref, dtbT_ref, dskip_ref, nrm_ref)

    @pl.when(ph_ref[s] == 0)
    def _():
        _ssd_dir(0, cpos_ref[s], first_ref[s], last_ref[s], *common, s0f_ref, *params,
                 y_ref, sf_ref, S, yf, yt)

    @pl.when(ph_ref[s] == 1)
    def _():
        _ssd_dir(1, cpos_ref[s], first_ref[s], last_ref[s], *common, s0b_ref, *params,
                 y_ref, sb_ref, S, yf, yt)


def ssd_mixer(lay, proj, xbc, small, smallT, s0f, s0b, a_log, dt_bias, d_skip, ssd_norm):
    n_tok = proj.shape[0]
    q = SSD_CHUNK
    tabs = lay.scan_tables(q)
    nsteps = int(tabs[0].shape[0])
    max_len = max(ln for _, ln in lay.seqs)
    hp = SSD_HEADS * SSD_HEAD_DIM
    blk = lambda w, cb: pl.BlockSpec((q, w), lambda s, b, *_: (b[s], cb))
    seq3 = pl.BlockSpec((1, hp, SSD_STATE), lambda s, b, sid, *_: (sid[s], 0, 0))
    full = lambda shp: pl.BlockSpec(shp, lambda s, *_: (0,) * len(shp))
    grid_spec = pltpu.PrefetchScalarGridSpec(
        num_scalar_prefetch=7, grid=(nsteps,),
        in_specs=[blk(SSD_INNER, 0), blk(512, SSD_INNER // 512), blk(SSD_INNER, PJ_Z // 1024),
                  blk(LANES, 0),
                  pl.BlockSpec((2 * SSD_HEADS, q), lambda s, b, *_: (0, b[s])),
                  seq3, seq3,
                  full((2, SSD_HEADS)), full((SSD_HEADS, 2)), full((2, SSD_HEADS)),
                  full((SSD_HEADS, 2)), full((1, hp)), full((1, hp))],
        out_specs=[pl.BlockSpec((q, hp), lambda s, b, sid, ph, cp, f, l, yb: (yb[s], 0)), seq3, seq3],
        scratch_shapes=[pltpu.VMEM((SSD_STATE, hp), F32), pltpu.VMEM((max_len, hp), F32),
                        pltpu.VMEM((q, hp), F32)])
    return pl.pallas_call(
        _ssd_kernel, grid_spec=grid_spec,
        out_shape=[jax.ShapeDtypeStruct((n_tok, hp), BF16),
                   jax.ShapeDtypeStruct((lay.n_seq, hp, SSD_STATE), F32),
                   jax.ShapeDtypeStruct((lay.n_seq, hp, SSD_STATE), F32)],
        compiler_params=_cp(("arbitrary",)), name="ssd_mixer",
    )(*tabs, xbc, xbc, proj, small, smallT, s0f, s0b,
      a_log, a_log.T, dt_bias, dt_bias.T,
      jnp.repeat(d_skip, SSD_HEAD_DIM).reshape(1, hp), ssd_norm.reshape(1, hp))


def _gla_dir(d, cpos, first, last, q_ref, k_ref, v_ref, og_ref, glr_ref, s0_ref,
             w2_ref, gb_ref, nrm_ref, o_ref, sout_ref, S, of):
    t = GLA_BLOCK
    dk, dv = GLA_KEY_DIM, GLA_VAL_DIM

    @pl.when(first == 1)
    def _():
        for h in range(GLA_HEADS):
            S[h] = s0_ref[0, h].T

    c0 = 2 * SSD_HEADS + GLA_RANK * d
    gp = _dot(glr_ref[:, c0:c0 + GLA_RANK], w2_ref[d], precision=HI) + gb_ref[d:d + 1, :]
    la = -_softplus(-gp) * (1.0 / GLA_TAU)
    row = lax.broadcasted_iota(I32, (t, t), 0)
    col = lax.broadcasted_iota(I32, (t, t), 1)
    mid = t // 2 - 1
    if d == 0:
        e = _dot((col <= row).astype(F32), la, precision=HI)
        tot = e[t - 1:t, :]
        r = e[mid:mid + 1, :]
        fqi, fki = jnp.exp(e - r), jnp.exp(r - e)
        fq, fk = jnp.exp(e), jnp.exp(tot - e)
        mask = row >= col
    else:
        e = _dot((col < row).astype(F32), la, precision=HI)
        tot = e[t - 1:t, :] + la[t - 1:t, :]
        r = e[mid:mid + 1, :]
        fqi, fki = jnp.exp(r - e), jnp.exp(e - r)
        fq, fk = jnp.exp(tot - e), jnp.exp(e)
        mask = col >= row
    dec = jnp.exp(tot)
    qf = q_ref[...].astype(F32) * (dk ** -0.5)
    kf = k_ref[...].astype(F32)
    rows = pl.ds(pl.multiple_of(cpos * t, t), t)
    for h in range(GLA_HEADS):
        sl = slice(dk * h, dk * (h + 1))
        vl = slice(dv * h, dv * (h + 1))
        qh, kh = qf[:, sl], kf[:, sl]
        sc = _dot_nt((qh * fqi[:, sl]).astype(BF16), (kh * fki[:, sl]).astype(BF16))
        sc = jnp.where(mask, sc, 0.0)
        vh = v_ref[:, vl]
        st = S[h]
        o = _dot(sc.astype(BF16), vh) + _dot_nt((qh * fq[:, sl]).astype(BF16), st.astype(BF16))
        S[h] = st * dec[:, sl] + _dot_tn(vh, (kh * fk[:, sl]).astype(BF16))
        if d == 0:
            of[rows, vl] = o
        else:
            ot = of[rows, vl] + o
            ms = jnp.mean(ot * ot, axis=-1, keepdims=True)
            on = ot * lax.rsqrt(ms + EPS) * nrm_ref[...]
            o_ref[:, vl] = (on * _silu(og_ref[:, vl].astype(F32))).astype(o_ref.dtype)

    @pl.when(last == 1)
    def _():
        for h in range(GLA_HEADS):
            sout_ref[0, h] = S[h].T


def _gla_kernel(blk_ref, sid_ref, ph_ref, cpos_ref, first_ref, last_ref, yblk_ref,
                q_ref, k_ref, v_ref, og_ref, glr_ref, s0f_ref, s0b_ref, w2_ref, gb_ref, nrm_ref,
                o_ref, sf_ref, sb_ref, S, of):
    s = pl.program_id(0)

    @pl.when(ph_ref[s] == 0)
    def _():
        _gla_dir(0, cpos_ref[s], first_ref[s], last_ref[s], q_ref, k_ref, v_ref, og_ref, glr_ref,
                 s0f_ref, w2_ref, gb_ref, nrm_ref, o_ref, sf_ref, S, of)

    @pl.when(ph_ref[s] == 1)
    def _():
        _gla_dir(1, cpos_ref[s], first_ref[s], last_ref[s], q_ref, k_ref, v_ref, og_ref, glr_ref,
                 s0b_ref, w2_ref, gb_ref, nrm_ref, o_ref, sb_ref, S, of)


def gla_mixer(lay, proj, small, s0f, s0b, gate_w2, gate_b, gla_norm):
    n_tok = proj.shape[0]
    t = GLA_BLOCK
    tabs = lay.scan_tables(t)
    nsteps = int(tabs[0].shape[0])
    max_len = max(ln for _, ln in lay.seqs)
    qk_w = GLA_HEADS * GLA_KEY_DIM
    v_w = GLA_HEADS * GLA_VAL_DIM
    blk = lambda w, cb: pl.BlockSpec((t, w), lambda s, b, *_: (b[s], cb))
    seq4 = pl.BlockSpec((1, GLA_HEADS, GLA_KEY_DIM, GLA_VAL_DIM), lambda s, b, sid, *_: (sid[s], 0, 0, 0))
    full = lambda shp: pl.BlockSpec(shp, lambda s, *_: (0,) * len(shp))
    grid_spec = pltpu.PrefetchScalarGridSpec(
        num_scalar_prefetch=7, grid=(nsteps,),
        in_specs=[blk(qk_w, PJ_Q // qk_w), blk(qk_w, PJ_K // qk_w), blk(v_w, PJ_V // v_w),
                  blk(v_w, PJ_OG // v_w), blk(LANES, 0), seq4, seq4,
                  full((2, GLA_RANK, qk_w)), full((2, qk_w)), full((1, GLA_VAL_DIM))],
        out_specs=[pl.BlockSpec((t, v_w), lambda s, b, sid, ph, cp, f, l, yb: (yb[s], 0)), seq4, seq4],
        scratch_shapes=[pltpu.VMEM((GLA_HEADS, GLA_VAL_DIM, GLA_KEY_DIM), F32),
                        pltpu.VMEM((max_len, v_w), F32)])
    st_shape = jax.ShapeDtypeStruct((lay.n_seq, GLA_HEADS, GLA_KEY_DIM, GLA_VAL_DIM), F32)
    return pl.pallas_call(
        _gla_kernel, grid_spec=grid_spec,
        out_shape=[jax.ShapeDtypeStruct((n_tok, v_w), BF16), st_shape, st_shape],
        compiler_params=_cp(("arbitrary",)), name="gla_mixer",
    )(*tabs, proj, proj, proj, proj, small, s0f, s0b, gate_w2, gate_b, gla_norm.reshape(1, -1))


def _res_kernel(*refs, ks):
    n = len(ks)
    a_refs, w_ref, x_ref, gate_ref, o_ref = refs[:n], refs[n], refs[n + 1], refs[n + 2], refs[n + 3]
    acc = None
    off = 0
    for a_ref, k in zip(a_refs, ks):
        part = _dot(a_ref[...], w_ref[off:off + k, :])
        acc = part if acc is None else acc + part
        off += k
    o_ref[...] = x_ref[...] + gate_ref[...] * acc


def proj_residual(lay, acts, w, x, gate, tm=512):
    n_tok = x.shape[0]
    ks = tuple(int(a.shape[1]) for a in acts)
    mrow = lambda i: (lay.mod_row(i * tm), 0, 0)
    return pl.pallas_call(
        functools.partial(_res_kernel, ks=ks), grid=(n_tok // tm,),
        in_specs=[pl.BlockSpec((tm, k), lambda i: (i, 0)) for k in ks]
        + [pl.BlockSpec(w.shape, lambda i: (0, 0)),
           pl.BlockSpec((tm, D), lambda i: (i, 0)),
           pl.BlockSpec((None, 1, D), mrow)],
        out_specs=pl.BlockSpec((tm, D), lambda i: (i, 0)),
        out_shape=jax.ShapeDtypeStruct((n_tok, D), F32),
        compiler_params=_cp(("arbitrary",)), name="proj_residual",
    )(*acts, w, x, gate)


def _router_kernel(x_ref, g_ref, sc_ref, sh_ref, rw_ref, rb_ref,
                   h_ref, idx_ref, gate_ref, pos_ref, posT_ref, cnt_ref):
    tm = x_ref.shape[0]
    h = _modnorm(x_ref[...], g_ref[...], sc_ref[...], sh_ref[...])
    h_ref[...] = h.astype(BF16)
    lg = _dot(h, rw_ref[...], precision=HI) + rb_ref[...]
    lane = lax.broadcasted_iota(I32, (tm, LANES), 1).astype(F32)
    vals, ids = [], []
    for _ in range(TOP_K):
        m = jnp.max(lg, axis=1, keepdims=True)
        i = jnp.min(jnp.where(lg == m, lane, float(LANES)), axis=1, keepdims=True)
        vals.append(m)
        ids.append(i)
        lg = jnp.where(lane == i, -jnp.inf, lg)
    ex = [jnp.exp(v - vals[0]) for v in vals]
    den = ex[0] + ex[1] + ex[2] + ex[3]
    sel = jnp.zeros((tm, LANES), F32)
    for i in ids:
        sel = sel + (lane == i).astype(F32)
    row = lax.broadcasted_iota(I32, (tm, tm), 0)
    col = lax.broadcasted_iota(I32, (tm, tm), 1)
    before = _dot((col < row).astype(BF16), sel.astype(BF16))
    n = jnp.sum(sel, axis=0, keepdims=True)
    er = lax.broadcasted_iota(I32, (LANES, LANES), 0)
    ec = lax.broadcasted_iota(I32, (LANES, LANES), 1)
    n_al = jnp.ceil(n * (1.0 / SEG_ALIGN)) * SEG_ALIGN
    offs = _dot(jnp.broadcast_to(n_al, (8, LANES)).astype(BF16), (er < ec).astype(BF16))[0:1, :]
    slot = before + offs
    idx_o = jnp.zeros((tm, LANES), F32)
    gate_o = jnp.zeros((tm, LANES), F32)
    pos_o = jnp.zeros((tm, LANES), F32)
    for k in range(TOP_K):
        p = jnp.sum(jnp.where(lane == ids[k], slot, 0.0), axis=1, keepdims=True)
        idx_o = jnp.where(lane == k, ids[k], idx_o)
        gate_o = jnp.where(lane == k, ex[k] / den, gate_o)
        pos_o = jnp.where(lane == k, p, pos_o)
    idx_ref[...] = idx_o.astype(I32)
    gate_ref[...] = gate_o
    pos_ref[...] = pos_o.astype(I32)
    posT_ref[...] = pos_o.T[0:8, :]
    cnt_ref[0] = jnp.broadcast_to(n, (8, LANES))


def moe_router(lay, x, g, sc, sh, rw, rb):
    n_tok = x.shape[0]
    tm = TOK_TILE
    nt = n_tok // tm
    mrow = lambda i: (lay.mod_row(i * tm), 0, 0)
    tile = lambda w, dt: (pl.BlockSpec((tm, w), lambda i: (i, 0)), jax.ShapeDtypeStruct((n_tok, w), dt))
    outs = [tile(D, BF16), tile(LANES, I32), tile(LANES, F32), tile(LANES, I32),
            (pl.BlockSpec((8, tm), lambda i: (0, i)), jax.ShapeDtypeStruct((8, n_tok), F32)),
            (pl.BlockSpec((1, 8, LANES), lambda i: (i, 0, 0)), jax.ShapeDtypeStruct((nt, 8, LANES), F32))]
    return pl.pallas_call(
        _router_kernel, grid=(nt,),
        in_specs=[pl.BlockSpec((tm, D), lambda i: (i, 0)),
                  pl.BlockSpec((1, D), lambda i: (0, 0)),
                  pl.BlockSpec((None, 1, D), mrow), pl.BlockSpec((None, 1, D), mrow),
                  pl.BlockSpec((D, LANES), lambda i: (0, 0)),
                  pl.BlockSpec((1, LANES), lambda i: (0, 0))],
        out_specs=[o[0] for o in outs], out_shape=[o[1] for o in outs],
        compiler_params=_cp(("arbitrary",)), name="moe_router",
    )(x, g.reshape(1, D), sc, sh, rw, rb)


SEG_ALIGN = 8
SEG_BITS = tuple(range(int(math.log2(TOK_TILE)), int(math.log2(SEG_ALIGN)) - 1, -1))
TILE_ROWS = TOK_TILE * TOP_K + N_EXPERTS * SEG_ALIGN


def _pow2_copies(n, src, dst, make_copy, op, bits):
    for b in bits:
        sz = 1 << b
        done = (n >> (b + 1)) << (b + 1)

        @pl.when((n & sz) != 0)
        def _():
            op(make_copy(pl.multiple_of(src + done, SEG_ALIGN), pl.multiple_of(dst + done, SEG_ALIGN), sz))


def _segment_copies(i, n_ref, off_ref, dst_ref, make_copy, op):
    def body(e, carry):
        k = i * N_EXPERTS + e
        _pow2_copies(n_ref[k], off_ref[k], dst_ref[k], make_copy, op, SEG_BITS)
        return carry
    lax.fori_loop(0, N_EXPERTS, body, 0)


TAIL_BITS = tuple(range(int(math.log2(MOE_BLOCK)) - 1, int(math.log2(SEG_ALIGN)) - 1, -1))


def _dispatch_kernel(n_ref, off_ref, dst_ref, tn_ref, td_ref, posT_ref, h_ref, xout_ref, srt, zbuf, sem):
    i = pl.program_id(0)
    tm = h_ref.shape[0]
    r = lax.broadcasted_iota(I32, (TILE_ROWS, tm), 0)
    hit = jnp.zeros((TILE_ROWS, tm), jnp.bool_)
    for k in range(TOP_K):
        hit = hit | (r == posT_ref[k:k + 1, :].astype(I32))
    sel = jnp.where(hit, 1.0, 0.0).astype(BF16)
    srt[...] = _dot(sel, h_ref[...])

    def make_copy(src, dst, sz):
        return pltpu.make_async_copy(srt.at[pl.ds(src, sz)], xout_ref.at[pl.ds(dst, sz)], sem)

    _segment_copies(i, n_ref, off_ref, dst_ref, make_copy, lambda c: c.start())
    _segment_copies(i, n_ref, off_ref, dst_ref, make_copy, lambda c: c.wait())

    @pl.when(i == pl.num_programs(0) - 1)
    def _():
        zbuf[...] = jnp.zeros_like(zbuf)

        def zero_copy(src, dst, sz):
            return pltpu.make_async_copy(zbuf.at[pl.ds(src, sz)], xout_ref.at[pl.ds(dst, sz)], sem)

        nb = xout_ref.shape[0] // MOE_BLOCK
        for op in (lambda c: c.start(), lambda c: c.wait()):
            def body(e, carry):
                _pow2_copies(tn_ref[e], 0, td_ref[e], zero_copy, op, TAIL_BITS)
                return carry
            lax.fori_loop(0, N_EXPERTS, body, 0)

            def unused(b, carry):
                op(zero_copy(0, pl.multiple_of(b * MOE_BLOCK, MOE_BLOCK), MOE_BLOCK))
                return carry
            lax.fori_loop(tn_ref[N_EXPERTS], nb, unused, 0)


def moe_dispatch(n_tab, off_tab, dst_tab, tail_n, tail_dst, posT, h2, n_rows):
    n_tok = h2.shape[0]
    tm = TOK_TILE
    grid_spec = pltpu.PrefetchScalarGridSpec(
        num_scalar_prefetch=5, grid=(n_tok // tm,),
        in_specs=[pl.BlockSpec((8, tm), lambda i, *_: (0, i)),
                  pl.BlockSpec((tm, D), lambda i, *_: (i, 0))],
        out_specs=pl.BlockSpec(memory_space=pl.ANY),
        scratch_shapes=[pltpu.VMEM((TILE_ROWS, D), F32), pltpu.VMEM((MOE_BLOCK, D), F32),
                        pltpu.SemaphoreType.DMA])
    return pl.pallas_call(
        _dispatch_kernel, grid_spec=grid_spec,
        out_shape=jax.ShapeDtypeStruct((n_rows, D), F32),
        compiler_params=_cp(("arbitrary",)), name="moe_dispatch",
    )(n_tab, off_tab, dst_tab, tail_n, tail_dst, posT, h2)


def _combine_kernel(n_ref, off_ref, dst_ref, pos_ref, gate_ref, x_ref, g2_ref, y_ref, o_ref, buf, sem):
    i = pl.program_id(0)
    tm = x_ref.shape[0]
    na = TILE_ROWS
    buf[tm * TOP_K:na, :] = jnp.zeros((na - tm * TOP_K, D), F32)

    def make_copy(src, dst, sz):
        return pltpu.make_async_copy(y_ref.at[pl.ds(dst, sz)], buf.at[pl.ds(src, sz)], sem)

    _segment_copies(i, n_ref, off_ref, dst_ref, make_copy, lambda c: c.start())
    _segment_copies(i, n_ref, off_ref, dst_ref, make_copy, lambda c: c.wait())
    lane = lax.broadcasted_iota(I32, (tm, na), 1)
    pw = jnp.zeros((tm, na), F32)
    for k in range(TOP_K):
        pw = pw + jnp.where(lane == pos_ref[:, k:k + 1], gate_ref[:, k:k + 1], 0.0)
    phi = pw.astype(BF16)
    plo = (pw - phi.astype(F32)).astype(BF16)
    yb = buf[...].astype(BF16)
    o_ref[...] = x_ref[...] + g2_ref[...] * (_dot(phi, yb) + _dot(plo, yb))


def moe_combine(lay, n_tab, off_tab, dst_tab, pos, gates, x, gate2, y_rows):
    n_tok = x.shape[0]
    tm = TOK_TILE
    mrow = lambda i, *_: (lay.mod_row(i * tm), 0, 0)
    grid_spec = pltpu.PrefetchScalarGridSpec(
        num_scalar_prefetch=3, grid=(n_tok // tm,),
        in_specs=[pl.BlockSpec((tm, LANES), lambda i, *_: (i, 0)),
                  pl.BlockSpec((tm, LANES), lambda i, *_: (i, 0)),
                  pl.BlockSpec((tm, D), lambda i, *_: (i, 0)),
                  pl.BlockSpec((None, 1, D), mrow),
                  pl.BlockSpec(memory_space=pl.ANY)],
        out_specs=pl.BlockSpec((tm, D), lambda i, *_: (i, 0)),
        scratch_shapes=[pltpu.VMEM((TILE_ROWS, D), F32), pltpu.SemaphoreType.DMA])
    return pl.pallas_call(
        _combine_kernel, grid_spec=grid_spec,
        out_shape=jax.ShapeDtypeStruct((n_tok, D), F32),
        compiler_params=_cp(("arbitrary",)), name="moe_combine",
    )(n_tab, off_tab, dst_tab, pos, gates, x, gate2, y_rows)


def _expert_kernel(be_ref, nv_ref, x_ref, wg_ref, bg_ref, wu_ref, bu_ref, wd_ref, bd_ref, y_ref,
                   wg_s, wu_s, wd_s):
    i = pl.program_id(0)
    valid = i < nv_ref[0]
    changed = jnp.logical_or(i == 0, be_ref[i] != be_ref[jnp.maximum(i - 1, 0)])

    @pl.when(jnp.logical_and(valid, changed))
    def _():
        rc = 128

        def cast(c, carry):
            rows = pl.ds(pl.multiple_of(c * rc, rc), rc)
            wg_s[rows, :] = wg_ref[0, rows, :].astype(BF16)
            wu_s[rows, :] = wu_ref[0, rows, :].astype(BF16)
            wd_s[rows, :] = wd_ref[0, rows, :].astype(BF16)
            return carry
        lax.fori_loop(0, D // rc, cast, 0)

    @pl.when(valid)
    def _():
        x = x_ref[...].astype(BF16)
        gt = jnp.minimum(_dot(x, wg_s[...]) + bg_ref[0], SWIGLU_LIMIT)
        up = jnp.clip(_dot(x, wu_s[...]) + bu_ref[0], -SWIGLU_LIMIT, SWIGLU_LIMIT)
        act = (up + 1.0) * gt * _sigmoid(SWIGLU_ALPHA * gt)
        y_ref[...] = _dot(act.astype(BF16), wd_s[...]) + bd_ref[0]

    @pl.when(jnp.logical_not(valid))
    def _():
        y_ref[...] = jnp.zeros_like(y_ref)


def moe_experts(blk_expert, n_valid, x_rows, w_gate, b_gate, w_up, b_up, w_down, b_down):
    n_rows = x_rows.shape[0]
    nb = n_rows // MOE_BLOCK
    ne, _, ff = w_gate.shape
    rowblk = lambda i, be, nv: (jnp.minimum(i, nv[0] - 1), 0)
    wsel = lambda i, be, nv: (be[i], 0, 0)
    grid_spec = pltpu.PrefetchScalarGridSpec(
        num_scalar_prefetch=2, grid=(nb,),
        in_specs=[pl.BlockSpec((MOE_BLOCK, D), rowblk),
                  pl.BlockSpec((1, D, ff), wsel), pl.BlockSpec((1, 1, ff), wsel),
                  pl.BlockSpec((1, D, ff), wsel), pl.BlockSpec((1, 1, ff), wsel),
                  pl.BlockSpec((1, ff, D), wsel), pl.BlockSpec((1, 1, D), wsel)],
        out_specs=pl.BlockSpec((MOE_BLOCK, D), lambda i, be, nv: (i, 0)),
        scratch_shapes=[pltpu.VMEM((D, ff), BF16), pltpu.VMEM((D, ff), BF16), pltpu.VMEM((ff, D), BF16)])
    return pl.pallas_call(
        _expert_kernel, grid_spec=grid_spec,
        out_shape=jax.ShapeDtypeStruct((n_rows, D), F32),
        compiler_params=_cp(("arbitrary",)), name="moe_experts",
    )(blk_expert, n_valid, x_rows, w_gate, b_gate.reshape(ne, 1, ff), w_up, b_up.reshape(ne, 1, ff),
      w_down, b_down.reshape(ne, 1, D))


def moe_layer(lay, x, g2, sc2, sh2, gate2, router_w, router_b, w_gate, b_gate, w_up, b_up, w_down, b_down):
    n_tok = x.shape[0]
    nt = n_tok // TOK_TILE
    rw = jnp.zeros((D, LANES), F32).at[:, :N_EXPERTS].set(router_w)
    rb = jnp.full((1, LANES), NEG, F32).at[0, :N_EXPERTS].set(router_b)
    h2, _, gates, pos, posT, cnt = moe_router(lay, x, g2, sc2, sh2, rw, rb)
    n_te = cnt[:, 0, :N_EXPERTS].astype(I32)
    n_te = (n_te + SEG_ALIGN - 1) // SEG_ALIGN * SEG_ALIGN
    totals = jnp.sum(n_te, axis=0)
    padded = (totals + MOE_BLOCK - 1) // MOE_BLOCK * MOE_BLOCK
    padded_end = jnp.cumsum(padded)
    pstart = padded_end - padded
    dst = pstart[None, :] + jnp.cumsum(n_te, axis=0) - n_te
    off = jnp.cumsum(n_te, axis=1) - n_te
    n_rows = nt * TILE_ROWS + N_EXPERTS * MOE_BLOCK
    nb = n_rows // MOE_BLOCK
    n_valid = (padded_end[-1] // MOE_BLOCK).astype(I32).reshape(1)
    bstart = jnp.minimum(jnp.arange(nb, dtype=I32), n_valid[0] - 1) * MOE_BLOCK
    blk_expert = jnp.minimum(jnp.sum((bstart[:, None] >= padded_end[None, :]).astype(I32), axis=1),
                             N_EXPERTS - 1).astype(I32)
    tabs = (n_te.reshape(-1).astype(I32), off.reshape(-1).astype(I32), dst.reshape(-1).astype(I32))
    tail_n = jnp.concatenate([(padded - totals).astype(I32), n_valid])
    x_rows = moe_dispatch(*tabs, tail_n, (pstart + totals).astype(I32), posT, h2, n_rows)
    y_rows = moe_experts(blk_expert, n_valid, x_rows, w_gate, b_gate, w_up, b_up, w_down, b_down)
    return moe_combine(lay, *tabs, pos, gates, x, gate2, y_rows)


QKV_TN = 256
N_QK_TILES = (ATT_HEADS + ATT_KV) * ATT_HD // QKV_TN


def _qkv_kernel(x_ref, g_ref, sc_ref, sh_ref, w_ref, nw_ref, cos_ref, sin_ref, o_ref, h_scr, *, p_tok):
    i, j = pl.program_id(0), pl.program_id(1)
    tm = x_ref.shape[0]

    @pl.when(j == 0)
    def _():
        h_scr[...] = _modnorm(x_ref[...], g_ref[...], sc_ref[...], sh_ref[...]).astype(BF16)

    acc = _dot(h_scr[...], w_ref[...])

    @pl.when(j >= N_QK_TILES)
    def _():
        o_ref[...] = acc

    @pl.when(j < N_QK_TILES)
    def _():
        r = lax.broadcasted_iota(I32, (QKV_TN, QKV_TN), 0) // ATT_HD
        c = lax.broadcasted_iota(I32, (QKV_TN, QKV_TN), 1) // ATT_HD
        ms = _dot(acc * acc, jnp.where(r == c, 1.0 / ATT_HD, 0.0).astype(F32), precision=HI)
        qn = acc * lax.rsqrt(ms + EPS) * nw_ref[...]

        @pl.when(i * tm < p_tok)
        def _():
            o_ref[...] = qn

        @pl.when(i * tm >= p_tok)
        def _():
            lane = lax.broadcasted_iota(I32, (tm, QKV_TN), 1)
            half = ATT_HD // 4
            first = (lane % (2 * half)) < half
            swapped = jnp.where(first, pltpu.roll(qn, QKV_TN - half, 1), pltpu.roll(qn, half, 1))
            o_ref[...] = qn * cos_ref[...] + swapped * sin_ref[...]


def _rope_tables(sample_len):
    pos = np.arange(sample_len)
    half = ATT_HD // 4
    inv = (ROPE_THETA ** (-np.arange(half, dtype=np.float32) / half)).astype(np.float32)
    ang_r = (pos // GRID_W).astype(np.float32)[:, None] * inv[None, :]
    ang_c = (pos % GRID_W).astype(np.float32)[:, None] * inv[None, :]
    cos = np.concatenate([np.cos(ang_r)] * 2 + [np.cos(ang_c)] * 2, axis=1)
    sin = np.concatenate([-np.sin(ang_r), np.sin(ang_r), -np.sin(ang_c), np.sin(ang_c)], axis=1)
    rep = QKV_TN // ATT_HD
    return (jnp.asarray(np.tile(cos, (1, rep)), F32), jnp.asarray(np.tile(sin, (1, rep)), F32))


def qkv_proj(lay, x, g, sc, sh, w, q_norm, k_norm, tm=256):
    n_tok = x.shape[0]
    n = w.shape[1]
    nq = ATT_HEADS * ATT_HD // QKV_TN
    rep = QKV_TN // ATT_HD
    nw = jnp.concatenate([jnp.tile(jnp.tile(q_norm, rep)[None, :], (nq, 1)),
                          jnp.tile(jnp.tile(k_norm, rep)[None, :], (n // QKV_TN - nq, 1))], axis=0)
    cos, sin = _rope_tables(lay.sample_len)
    mrow = lambda i, j: (lay.mod_row(i * tm), 0, 0)
    rrow = lambda i, j: (jnp.where(i * tm < lay.p_tok, 0, ((i * tm - lay.p_tok) % lay.sample_len) // tm), 0)
    return pl.pallas_call(
        functools.partial(_qkv_kernel, p_tok=lay.p_tok), grid=(n_tok // tm, n // QKV_TN),
        in_specs=[pl.BlockSpec((tm, D), lambda i, j: (i, 0)),
                  pl.BlockSpec((1, D), lambda i, j: (0, 0)),
                  pl.BlockSpec((None, 1, D), mrow), pl.BlockSpec((None, 1, D), mrow),
                  pl.BlockSpec((D, QKV_TN), lambda i, j: (0, j)),
                  pl.BlockSpec((None, 1, QKV_TN), lambda i, j: (j, 0, 0)),
                  pl.BlockSpec((tm, QKV_TN), rrow), pl.BlockSpec((tm, QKV_TN), rrow)],
        out_specs=pl.BlockSpec((tm, QKV_TN), lambda i, j: (i, j)),
        out_shape=jax.ShapeDtypeStruct((n_tok, n), F32),
        scratch_shapes=[pltpu.VMEM((tm, D), BF16)],
        compiler_params=_cp(("arbitrary", "arbitrary")), name="qkv_proj",
    )(x, g.reshape(1, D), sc, sh, w, nw.reshape(n // QKV_TN, 1, QKV_TN), cos, sin)


def _dup_group(x, g):
    blk = x[:, LANES * (g // 2):LANES * (g // 2 + 1)]
    if g % 2 == 1:
        blk = pltpu.roll(blk, ATT_HD, 1)
    lo = lax.broadcasted_iota(I32, blk.shape, 1) < ATT_HD
    low = jnp.where(lo, blk, 0.0)
    return low + pltpu.roll(low, ATT_HD, 1)


def _attend(q_ref, k_all, v_all, mask, sink_ref, o_ref):
    nq = q_ref.shape[0]
    lo = lax.broadcasted_iota(I32, (nq, LANES), 1) < ATT_HD
    grp = ATT_HEADS // ATT_KV
    for g in range(ATT_KV):
        k2 = _dup_group(k_all, g).astype(BF16)
        v2 = _dup_group(v_all, g).astype(BF16)
        for jp in range(grp // 2):
            j = g * (grp // 2) + jp
            qp = q_ref[:, LANES * j:LANES * (j + 1)] * (ATT_HD ** -0.5)
            outs = []
            for half in range(2):
                qh = jnp.where(lo, qp, 0.0) if half == 0 else jnp.where(lo, 0.0, qp)
                s = _dot_nt(qh.astype(BF16), k2)
                if mask is not None:
                    s = jnp.where(mask, s, NEG)
                sink = sink_ref[2 * j + half]
                m = jnp.maximum(jnp.max(s, axis=1, keepdims=True), sink)
                p = jnp.exp(s - m)
                den = jnp.sum(p, axis=1, keepdims=True) + jnp.exp(sink - m)
                outs.append(_dot(p.astype(BF16), v2) / den)
            o_ref[:, LANES * j:LANES * (j + 1)] = jnp.where(lo, outs[0], outs[1]).astype(o_ref.dtype)


def _attn_ctx_kernel(sink_ref, q_ref, k_ref, v_ref, o_ref):
    _attend(q_ref, k_ref[...], v_ref[...], None, sink_ref, o_ref)


def attn_context(lay, qkv, sinks):
    qw = ATT_HEADS * ATT_HD
    kw = ATT_KV * ATT_HD
    ln = lay.prompt_len
    grid_spec = pltpu.PrefetchScalarGridSpec(
        num_scalar_prefetch=0, grid=(lay.n_prompt,),
        in_specs=[pl.BlockSpec(memory_space=pltpu.SMEM),
                  pl.BlockSpec((ln, qw), lambda b: (b, 0)),
                  pl.BlockSpec((ln, kw), lambda b: (b, qw // kw)),
                  pl.BlockSpec((ln, kw), lambda b: (b, qw // kw + 1))],
        out_specs=pl.BlockSpec((ln, qw), lambda b: (b, 0)))
    return pl.pallas_call(
        _attn_ctx_kernel, grid_spec=grid_spec,
        out_shape=jax.ShapeDtypeStruct((lay.p_tok, qw), BF16),
        compiler_params=_cp(("arbitrary",)), name="attn_context",
    )(sinks, qkv, qkv, qkv)


def _attn_lat_kernel(sink_ref, q_ref, kp_ref, kc_ref, kn_ref, vp_ref, vc_ref, vn_ref, ck_ref, cv_ref, o_ref,
                     *, nblk):
    i = pl.program_id(1)
    bq = ATT_BLOCK
    nctx = ck_ref.shape[1]
    k_all = jnp.concatenate([kp_ref[...], kc_ref[...], kn_ref[...], ck_ref[0]], axis=0)
    v_all = jnp.concatenate([vp_ref[...], vc_ref[...], vn_ref[...], cv_ref[0]], axis=0)
    ns = 3 * bq + nctx
    r = lax.broadcasted_iota(I32, (bq, ns), 0)
    c = lax.broadcasted_iota(I32, (bq, ns), 1)
    rel = c - r
    first_key = jnp.where(i > 0, 0, bq)
    end_key = jnp.where(i < nblk - 1, 3 * bq, 2 * bq)
    band = (rel >= bq - WINDOW) & (rel <= bq + WINDOW) & (c >= first_key) & (c < end_key)
    mask = band | (c >= 3 * bq)
    _attend(q_ref, k_all, v_all, mask, sink_ref, o_ref)


def attn_latent(lay, qkv, cache_k, cache_v, sinks):
    qw = ATT_HEADS * ATT_HD
    kw = ATT_KV * ATT_HD
    bq = ATT_BLOCK
    nblk = lay.sample_len // bq
    b0 = lay.p_tok // bq
    nctx = cache_k.shape[1]
    rb = lambda b, i: b0 + b * nblk + i
    kspec = lambda cb, sh: pl.BlockSpec(
        (bq, kw), lambda b, i: (b0 + b * nblk + jnp.clip(i + sh, 0, nblk - 1), cb))
    kc, vc = qw // kw, qw // kw + 1
    grid_spec = pltpu.PrefetchScalarGridSpec(
        num_scalar_prefetch=0, grid=(lay.n_sample, nblk),
        in_specs=[pl.BlockSpec(memory_space=pltpu.SMEM),
                  pl.BlockSpec((bq, qw), lambda b, i: (rb(b, i), 0)),
                  kspec(kc, -1), kspec(kc, 0), kspec(kc, 1),
                  kspec(vc, -1), kspec(vc, 0), kspec(vc, 1),
                  pl.BlockSpec((1, nctx, kw), lambda b, i: (b, 0, 0)),
                  pl.BlockSpec((1, nctx, kw), lambda b, i: (b, 0, 0))],
        out_specs=pl.BlockSpec((bq, qw), lambda b, i: (b * nblk + i, 0)))
    return pl.pallas_call(
        functools.partial(_attn_lat_kernel, nblk=nblk), grid_spec=grid_spec,
        out_shape=jax.ShapeDtypeStruct((lay.n_sample * lay.sample_len, qw), BF16),
        compiler_params=_cp(("arbitrary", "arbitrary")), name="attn_latent",
    )(sinks, qkv, qkv, qkv, qkv, qkv, qkv, qkv,
      cache_k.reshape(lay.n_sample, nctx, kw), cache_v.reshape(lay.n_sample, nctx, kw))


def _forward(lay, x_prompt, x_sample, state_l0_ssd_fwd, state_l0_ssd_bwd, state_l0_gla_fwd, state_l0_gla_bwd,
             cache_l1_k, cache_l1_v, c, c_ctx, ada_w, ada_b, norm1, norm2,
             l0_w_in, l0_conv_w, l0_conv_b, l0_a_log, l0_dt_bias, l0_d_skip, l0_ssd_norm,
             l0_gate_w2, l0_gate_b, l0_gla_norm, l0_w_out,
             l1_w_qkv, l1_q_norm, l1_k_norm, l1_sinks, l1_w_out,
             router_w, router_b, exp_w_gate, exp_b_gate, exp_w_up, exp_b_up, exp_w_down, exp_b_down):
    np_, ns = lay.n_prompt, lay.n_sample
    x = jnp.concatenate([x_prompt.reshape(-1, D), x_sample.reshape(-1, D)], axis=0)
    cond8 = jnp.zeros((8, D), F32).at[0].set(c_ctx).at[1:1 + ns].set(c)
    mod = ada_table(cond8, ada_w, ada_b)
    mods = [[mod[l, :, p * D:(p + 1) * D].reshape(8, 1, D) for p in range(N_ADA)] for l in range(2)]

    def moe(l, xx):
        return moe_layer(lay, xx, norm2[l], mods[l][4], mods[l][3], mods[l][5], router_w[l], router_b[l],
                         exp_w_gate[l], exp_b_gate[l], exp_w_up[l], exp_b_up[l], exp_w_down[l], exp_b_down[l])

    sp = np.cumsum((SSD_INNER, SSD_INNER + 2 * SSD_GROUPS * SSD_STATE, 2 * SSD_HEADS,
                    GLA_HEADS * GLA_KEY_DIM, GLA_HEADS * GLA_KEY_DIM,
                    GLA_HEADS * GLA_VAL_DIM, GLA_HEADS * GLA_VAL_DIM, 2 * GLA_RANK))
    cols = lambda a, b: l0_w_in[:, a:b]
    w_main = jnp.concatenate([cols(0, sp[0]), cols(sp[4], sp[5]), cols(sp[5], sp[6]), cols(sp[0], sp[1]),
                              cols(sp[2], sp[3]), cols(sp[3], sp[4])], axis=1).astype(BF16)
    w_small = jnp.concatenate([cols(sp[1], sp[2]), cols(sp[6], sp[7]),
                               jnp.zeros((D, LANES - 2 * SSD_HEADS - 2 * GLA_RANK), F32)], axis=1)
    proj, small = norm_proj(lay, x, norm1[0], mods[0][1], mods[0][0], w_main, w_small, 512, PJ_W // 2, BF16)
    xbc = conv_silu(lay, proj, l0_conv_w, l0_conv_b)
    hp = SSD_HEADS * SSD_HEAD_DIM
    zs = jnp.zeros((np_, hp, SSD_STATE), F32)
    zg = jnp.zeros((np_,) + state_l0_gla_fwd.shape[1:], F32)
    y_n, ssd_f, ssd_b = ssd_mixer(
        lay, proj, xbc, small, small.T,
        jnp.concatenate([zs, state_l0_ssd_fwd.reshape(ns, hp, SSD_STATE)], axis=0),
        jnp.concatenate([zs, state_l0_ssd_bwd.reshape(ns, hp, SSD_STATE)], axis=0),
        l0_a_log, l0_dt_bias, l0_d_skip, l0_ssd_norm)
    o_n, gla_f, gla_b = gla_mixer(
        lay, proj, small,
        jnp.concatenate([zg, state_l0_gla_fwd], axis=0), jnp.concatenate([zg, state_l0_gla_bwd], axis=0),
        l0_gate_w2, l0_gate_b, l0_gla_norm)
    x = proj_residual(lay, [y_n, o_n], l0_w_out.astype(BF16), x, mods[0][2])
    x = moe(0, x)

    qkv = qkv_proj(lay, x, norm1[1], mods[1][1], mods[1][0], l1_w_qkv.astype(BF16), l1_q_norm, l1_k_norm)
    o_ctx = attn_context(lay, qkv, l1_sinks)
    o_lat = attn_latent(lay, qkv, cache_l1_k, cache_l1_v, l1_sinks)
    o = jnp.concatenate([o_ctx, o_lat], axis=0)
    x = proj_residual(lay, [o], l1_w_out.astype(BF16), x, mods[1][2])
    x = moe(1, x)

    qw = ATT_HEADS * ATT_HD
    kw = ATT_KV * ATT_HD
    return (x[:lay.p_tok].reshape(x_prompt.shape), x[lay.p_tok:].reshape(x_sample.shape),
            ssd_f[:np_].reshape(np_, SSD_HEADS, SSD_HEAD_DIM, SSD_STATE),
            ssd_b[:np_].reshape(np_, SSD_HEADS, SSD_HEAD_DIM, SSD_STATE),
            gla_f[:np_], gla_b[:np_],
            qkv[:lay.p_tok, qw:qw + kw].reshape(np_, lay.prompt_len, ATT_KV, ATT_HD),
            qkv[:lay.p_tok, qw + kw:].reshape(np_, lay.prompt_len, ATT_KV, ATT_HD))


def kernel(x_prompt, x_sample, state_l0_ssd_fwd, state_l0_ssd_bwd, state_l0_gla_fwd, state_l0_gla_bwd, cache_l1_k, cache_l1_v, c, c_ctx, ada_w, ada_b, norm1, norm2, l0_w_in, l0_conv_w, l0_conv_b, l0_a_log, l0_dt_bias, l0_d_skip, l0_ssd_norm, l0_gate_w2, l0_gate_b, l0_gla_norm, l0_w_out, l1_w_qkv, l1_q_norm, l1_k_norm, l1_sinks, l1_w_out, router_w, router_b, exp_w_gate, exp_b_gate, exp_w_up, exp_b_up, exp_w_down, exp_b_down):
    lay = Layout(x_prompt.shape[0], x_prompt.shape[1], x_sample.shape[0], x_sample.shape[1])
    return _forward(lay, x_prompt, x_sample, state_l0_ssd_fwd, state_l0_ssd_bwd, state_l0_gla_fwd,
                    state_l0_gla_bwd, cache_l1_k, cache_l1_v, c, c_ctx, ada_w, ada_b, norm1, norm2,
                    l0_w_in, l0_conv_w, l0_conv_b, l0_a_log, l0_dt_bias, l0_d_skip, l0_ssd_norm,
                    l0_gate_w2, l0_gate_b, l0_gla_norm, l0_w_out,
                    l1_w_qkv, l1_q_norm, l1_k_norm, l1_sinks, l1_w_out,
                    router_w, router_b, exp_w_gate, exp_b_gate, exp_w_up, exp_b_up, exp_w_down, exp_b_down)
```

```python
import functools
import math

import numpy as np
import jax
import jax.numpy as jnp
from jax import lax
from jax.experimental import pallas as pl
from jax.experimental.pallas import tpu as pltpu

F32 = jnp.float32
BF16 = jnp.bfloat16
I32 = jnp.int32
HI = lax.Precision.HIGHEST

D = 1024
EPS = 1e-6
N_ADA = 6
SSD_HEADS = 16
SSD_HEAD_DIM = 64
SSD_INNER = 1024
SSD_STATE = 128
SSD_GROUPS = 2
SSD_CONV = 5
SSD_CHUNK = 128
GLA_HEADS = 4
GLA_KEY_DIM = 128
GLA_VAL_DIM = 256
GLA_RANK = 16
GLA_TAU = 16.0
GLA_BLOCK = 64
ATT_HEADS = 16
ATT_KV = 4
ATT_HD = 64
ATT_BLOCK = 128
WINDOW = 128
GRID_W = 64
ROPE_THETA = 10000.0
N_EXPERTS = 32
TOP_K = 4
EXPERT_FF = 1024
SWIGLU_LIMIT = 7.0
SWIGLU_ALPHA = 1.702
MOE_BLOCK = 256
TOK_TILE = 256
LANES = 128
NEG = -1e30

PJ_Z, PJ_V, PJ_OG, PJ_XBC, PJ_Q, PJ_K = 0, 1024, 2048, 3072, 4608, 5120
PJ_W = 5632
VMEM_LIMIT = 48 * 1024 * 1024


def _cp(sem, vmem=VMEM_LIMIT):
    return pltpu.CompilerParams(dimension_semantics=sem, vmem_limit_bytes=vmem)


class Layout:
    def __init__(self, n_prompt, prompt_len, n_sample, sample_len):
        self.n_prompt, self.prompt_len = n_prompt, prompt_len
        self.n_sample, self.sample_len = n_sample, sample_len
        self.p_tok = n_prompt * prompt_len
        self.n_tok = self.p_tok + n_sample * sample_len
        self.seqs = [(i * prompt_len, prompt_len) for i in range(n_prompt)]
        self.seqs += [(self.p_tok + i * sample_len, sample_len) for i in range(n_sample)]
        self.n_seq = len(self.seqs)

    def mod_row(self, start):
        return jnp.where(start < self.p_tok, 0, 1 + (start - self.p_tok) // self.sample_len)

    def scan_tables(self, blk):
        rows = []
        for sid, (st, ln) in enumerate(self.seqs):
            nc = ln // blk
            b0 = st // blk
            for ph in (0, 1):
                order = range(nc) if ph == 0 else range(nc - 1, -1, -1)
                for n, c in enumerate(order):
                    yb = b0 + nc - 1 if ph == 0 else b0 + c
                    rows.append((b0 + c, sid, ph, c, int(n == 0), int(n == nc - 1), yb))
        return [jnp.asarray(np.array(col, np.int32)) for col in zip(*rows)]

    def tile_flags(self, tile):
        first, last = [], []
        for st, ln in self.seqs:
            n = ln // tile
            first += [1] + [0] * (n - 1)
            last += [0] * (n - 1) + [1]
        return jnp.asarray(np.array(first, np.int32)), jnp.asarray(np.array(last, np.int32))


def _sigmoid(x):
    return 1.0 / (1.0 + jnp.exp(-x))


def _silu(x):
    return x * _sigmoid(x)


def _softplus(x):
    return jnp.maximum(x, 0.0) + jnp.log(1.0 + jnp.exp(-jnp.abs(x)))


def _modnorm(x, g, sc, sh):
    ms = jnp.mean(x * x, axis=-1, keepdims=True)
    return (x * lax.rsqrt(ms + EPS) * g) * (1.0 + sc) + sh


def _dot(a, b, **kw):
    return jnp.dot(a, b, preferred_element_type=F32, **kw)


def _dot_nt(a, b):
    return lax.dot_general(a, b, (((1,), (1,)), ((), ())), preferred_element_type=F32)


def _dot_tn(a, b):
    return lax.dot_general(a, b, (((0,), (0,)), ((), ())), preferred_element_type=F32)


def _ada_kernel(c_ref, w_ref, b_ref, o_ref):
    o_ref[0] = _dot(_silu(c_ref[...]), w_ref[0], precision=HI) + b_ref[0]


def ada_table(cond8, ada_w, ada_b):
    depth, _, n = ada_w.shape
    tn = 1536
    return pl.pallas_call(
        _ada_kernel, grid=(depth, n // tn),
        in_specs=[pl.BlockSpec((8, D), lambda l, j: (0, 0)),
                  pl.BlockSpec((1, D, tn), lambda l, j: (l, 0, j)),
                  pl.BlockSpec((1, 1, tn), lambda l, j: (l, 0, j))],
        out_specs=pl.BlockSpec((1, 8, tn), lambda l, j: (l, 0, j)),
        out_shape=jax.ShapeDtypeStruct((depth, 8, n), F32),
        compiler_params=_cp(("arbitrary", "arbitrary")), name="ada_table",
    )(cond8, ada_w, ada_b.reshape(depth, 1, n))


def _proj_kernel(x_ref, g_ref, sc_ref, sh_ref, w_ref, ws_ref, o_ref, os_ref, h_scr):
    @pl.when(pl.program_id(1) == 0)
    def _():
        h = _modnorm(x_ref[...], g_ref[...], sc_ref[...], sh_ref[...])
        h_scr[...] = h.astype(BF16)
        os_ref[...] = _dot(h, ws_ref[...], precision=HI)

    o_ref[...] = _dot(h_scr[...], w_ref[...]).astype(o_ref.dtype)


def norm_proj(lay, x, g, sc, sh, w, w_small, tm, tn, out_dtype):
    n_tok = x.shape[0]
    n = w.shape[1]
    ns = w_small.shape[1]
    mrow = lambda i, j: (lay.mod_row(i * tm), 0, 0)
    return pl.pallas_call(
        _proj_kernel, grid=(n_tok // tm, n // tn),
        in_specs=[pl.BlockSpec((tm, D), lambda i, j: (i, 0)),
                  pl.BlockSpec((1, D), lambda i, j: (0, 0)),
                  pl.BlockSpec((None, 1, D), mrow),
                  pl.BlockSpec((None, 1, D), mrow),
                  pl.BlockSpec((D, tn), lambda i, j: (0, j)),
                  pl.BlockSpec((D, ns), lambda i, j: (0, 0))],
        out_specs=[pl.BlockSpec((tm, tn), lambda i, j: (i, j)),
                   pl.BlockSpec((tm, ns), lambda i, j: (i, 0))],
        out_shape=[jax.ShapeDtypeStruct((n_tok, n), out_dtype),
                   jax.ShapeDtypeStruct((n_tok, ns), F32)],
        scratch_shapes=[pltpu.VMEM((tm, D), BF16)],
        compiler_params=_cp(("arbitrary", "arbitrary")), name="norm_proj",
    )(x, g.reshape(1, D), sc, sh, w, w_small)


CONV_TILE = 256
CONV_HALO = 16


def _conv_kernel(first_ref, last_ref, prev_ref, cur_ref, next_ref, w_ref, b_ref, o_ref, ext):
    i = pl.program_id(0)
    keep_p = first_ref[i] == 0
    keep_n = last_ref[i] == 0
    h, t = CONV_HALO, CONV_TILE
    ext[0:h, :] = jnp.where(keep_p, prev_ref[...].astype(F32), 0.0)
    ext[h:h + t, :] = cur_ref[...].astype(F32)
    ext[h + t:h + t + h, :] = jnp.where(keep_n, next_ref[...].astype(F32), 0.0)
    pad = SSD_CONV // 2
    acc = jnp.broadcast_to(b_ref[...], (t, b_ref.shape[1]))
    for k in range(SSD_CONV):
        acc = acc + w_ref[k:k + 1, :] * ext[h - pad + k:h - pad + k + t, :]
    o_ref[...] = _silu(acc).astype(o_ref.dtype)


def conv_silu(lay, proj, conv_w, conv_b):
    n_tok = proj.shape[0]
    cw = 512
    nch = conv_w.shape[1] // cw
    c0 = PJ_XBC // cw
    t, h = CONV_TILE, CONV_HALO
    first, last = lay.tile_flags(t)
    r = t // h
    nhb = n_tok // h
    grid_spec = pltpu.PrefetchScalarGridSpec(
        num_scalar_prefetch=2, grid=(n_tok // t, nch),
        in_specs=[pl.BlockSpec((h, cw), lambda i, c, f, l: (jnp.maximum(i * r - 1, 0), c0 + c)),
                  pl.BlockSpec((t, cw), lambda i, c, f, l: (i, c0 + c)),
                  pl.BlockSpec((h, cw), lambda i, c, f, l: (jnp.minimum((i + 1) * r, nhb - 1), c0 + c)),
                  pl.BlockSpec((SSD_CONV, cw), lambda i, c, f, l: (0, c)),
                  pl.BlockSpec((1, cw), lambda i, c, f, l: (0, c))],
        out_specs=pl.BlockSpec((t, cw), lambda i, c, f, l: (i, c)),
        scratch_shapes=[pltpu.VMEM((t + 2 * h, cw), F32)])
    return pl.pallas_call(
        _conv_kernel, grid_spec=grid_spec,
        out_shape=jax.ShapeDtypeStruct((n_tok, conv_w.shape[1]), BF16),
        compiler_params=_cp(("arbitrary", "arbitrary")), name="conv_silu",
    )(first, last, proj, proj, proj, conv_w, conv_b.reshape(1, -1))


def _ssd_dir(d, cpos, first, last, xs_ref, bc_ref, z_ref, dtg_ref, dtgT_ref, s0_ref,
             alog_ref, alogT_ref, dtb_ref, dtbT_ref, dskip_ref, nrm_ref,
             y_ref, sout_ref, S, yf, yt):
    q = SSD_CHUNK
    nh = SSD_HEADS

    @pl.when(first == 1)
    def _():
        S[...] = s0_ref[0].T

    xs = xs_ref[...].astype(F32)
    dt = _softplus(dtg_ref[:, nh * d:nh * d + nh] + dtb_ref[d:d + 1, :])
    dtT = _softplus(dtgT_ref[nh * d:nh * d + nh, :] + dtbT_ref[:, d:d + 1])
    ad = dt * (-jnp.exp(alog_ref[d:d + 1, :]))
    adT = dtT * (-jnp.exp(alogT_ref[:, d:d + 1]))
    row = lax.broadcasted_iota(I32, (q, q), 0)
    col = lax.broadcasted_iota(I32, (q, q), 1)
    if d == 0:
        e = _dot((col <= row).astype(F32), ad, precision=HI)
        eT = _dot(adT, (row <= col).astype(F32), precision=HI)
        tot = e[q - 1:q, :]
        mask = row >= col
        fq = jnp.exp(e)
        fk = jnp.exp(tot - e)
    else:
        e = _dot((col < row).astype(F32), ad, precision=HI)
        eT = _dot(adT, (row < col).astype(F32), precision=HI)
        tot = jnp.sum(ad, axis=0, keepdims=True)
        mask = col >= row
        fq = jnp.exp(tot - e)
        fk = jnp.exp(e)
    dec = jnp.exp(tot)
    lo = lax.broadcasted_iota(I32, (q, LANES), 1) < SSD_HEAD_DIM
    lo1 = lax.broadcasted_iota(I32, (1, LANES), 1) < SSD_HEAD_DIM

    def colpat(arr, a):
        return jnp.where(lo, arr[:, a:a + 1], arr[:, a + 1:a + 2])

    rep = (nh // SSD_GROUPS) // 2
    for g in range(SSD_GROUPS):
        bg = bc_ref[:, SSD_STATE * g:SSD_STATE * (g + 1)]
        cg = bc_ref[:, SSD_STATE * (SSD_GROUPS + g):SSD_STATE * (SSD_GROUPS + g + 1)]
        gmat = _dot_nt(cg, bg)
        for j in range(rep * g, rep * (g + 1)):
            a = 2 * j
            sl = slice(LANES * j, LANES * (j + 1))
            parts = []
            for hh in (a, a + 1):
                if d == 0:
                    diff = e[:, hh:hh + 1] - eT[hh:hh + 1, :]
                else:
                    diff = eT[hh:hh + 1, :] - e[:, hh:hh + 1]
                parts.append((gmat * jnp.exp(jnp.where(mask, diff, NEG))).astype(BF16))
            lhs = jnp.concatenate(parts, axis=1)
            xdt = xs[:, sl] * colpat(dt, a)
            rhs = jnp.concatenate([jnp.where(lo, xdt, 0.0), jnp.where(lo, 0.0, xdt)], axis=0)
            y = _dot(lhs, rhs.astype(BF16))
            sj = S[:, sl]
            y = y + _dot(cg, sj.astype(BF16)) * colpat(fq, a)
            xk = (xdt * colpat(fk, a)).astype(BF16)
            decp = jnp.where(lo1, dec[:, a:a + 1], dec[:, a + 1:a + 2])
            S[:, sl] = sj * decp + _dot_tn(bg, xk)
            if d == 0:
                yf[pl.ds(pl.multiple_of(cpos * q, q), q), sl] = y
            else:
                yt[:, sl] = y

    if d == 1:
        ytot = yf[pl.ds(pl.multiple_of(cpos * q, q), q), :] + yt[...] + dskip_ref[...] * xs
        yg = ytot * _silu(z_ref[...].astype(F32))
        ms = jnp.mean(yg * yg, axis=-1, keepdims=True)
        y_ref[...] = (yg * lax.rsqrt(ms + EPS) * nrm_ref[...]).astype(y_ref.dtype)

    @pl.when(last == 1)
    def _():
        sout_ref[0] = S[...].T


def _ssd_kernel(blk_ref, sid_ref, ph_ref, cpos_ref, first_ref, last_ref, yblk_ref,
                xs_ref, bc_ref, z_ref, dtg_ref, dtgT_ref, s0f_ref, s0b_ref,
                alog_ref, alogT_ref, dtb_ref, dtbT_ref, dskip_ref, nrm_ref,
                y_ref, sf_ref, sb_ref, S, yf, yt):
    s = pl.program_id(0)
    common = (xs_ref, bc_ref, z_ref, dtg_ref, dtgT_ref)
    params = (alog_ref, alogT_ref, dtb_ref, dtbT_ref, dskip_ref, nrm_ref)

    @pl.when(ph_ref[s] == 0)
    def _():
        _ssd_dir(0, cpos_ref[s], first_ref[s], last_ref[s], *common, s0f_ref, *params,
                 y_ref, sf_ref, S, yf, yt)

    @pl.when(ph_ref[s] == 1)
    def _():
        _ssd_dir(1, cpos_ref[s], first_ref[s], last_ref[s], *common, s0b_ref, *params,
                 y_ref, sb_ref, S, yf, yt)


def ssd_mixer(lay, proj, xbc, small, smallT, s0f, s0b, a_log, dt_bias, d_skip, ssd_norm):
    n_tok = proj.shape[0]
    q = SSD_CHUNK
    tabs = lay.scan_tables(q)
    nsteps = int(tabs[0].shape[0])
    max_len = max(ln for _, ln in lay.seqs)
    hp = SSD_HEADS * SSD_HEAD_DIM
    blk = lambda w, cb: pl.BlockSpec((q, w), lambda s, b, *_: (b[s], cb))
    seq3 = pl.BlockSpec((1, hp, SSD_STATE), lambda s, b, sid, *_: (sid[s], 0, 0))
    full = lambda shp: pl.BlockSpec(shp, lambda s, *_: (0,) * len(shp))
    grid_spec = pltpu.PrefetchScalarGridSpec(
        num_scalar_prefetch=7, grid=(nsteps,),
        in_specs=[blk(SSD_INNER, 0), blk(512, SSD_INNER // 512), blk(SSD_INNER, PJ_Z // 1024),
                  blk(LANES, 0),
                  pl.BlockSpec((2 * SSD_HEADS, q), lambda s, b, *_: (0, b[s])),
                  seq3, seq3,
                  full((2, SSD_HEADS)), full((SSD_HEADS, 2)), full((2, SSD_HEADS)),
                  full((SSD_HEADS, 2)), full((1, hp)), full((1, hp))],
        out_specs=[pl.BlockSpec((q, hp), lambda s, b, sid, ph, cp, f, l, yb: (yb[s], 0)), seq3, seq3],
        scratch_shapes=[pltpu.VMEM((SSD_STATE, hp), F32), pltpu.VMEM((max_len, hp), F32),
                        pltpu.VMEM((q, hp), F32)])
    return pl.pallas_call(
        _ssd_kernel, grid_spec=grid_spec,
        out_shape=[jax.ShapeDtypeStruct((n_tok, hp), BF16),
                   jax.ShapeDtypeStruct((lay.n_seq, hp, SSD_STATE), F32),
                   jax.ShapeDtypeStruct((lay.n_seq, hp, SSD_STATE), F32)],
        compiler_params=_cp(("arbitrary",)), name="ssd_mixer",
    )(*tabs, xbc, xbc, proj, small, smallT, s0f, s0b,
      a_log, a_log.T, dt_bias, dt_bias.T,
      jnp.repeat(d_skip, SSD_HEAD_DIM).reshape(1, hp), ssd_norm.reshape(1, hp))


def _gla_dir(d, cpos, first, last, q_ref, k_ref, v_ref, og_ref, glr_ref, s0_ref,
             w2_ref, gb_ref, nrm_ref, o_ref, sout_ref, S, of):
    t = GLA_BLOCK
    dk, dv = GLA_KEY_DIM, GLA_VAL_DIM

    @pl.when(first == 1)
    def _():
        for h in range(GLA_HEADS):
            S[h] = s0_ref[0, h].T

    c0 = 2 * SSD_HEADS + GLA_RANK * d
    gp = _dot(glr_ref[:, c0:c0 + GLA_RANK], w2_ref[d], precision=HI) + gb_ref[d:d + 1, :]
    la = -_softplus(-gp) * (1.0 / GLA_TAU)
    row = lax.broadcasted_iota(I32, (t, t), 0)
    col = lax.broadcasted_iota(I32, (t, t), 1)
    mid = t // 2 - 1
    if d == 0:
        e = _dot((col <= row).astype(F32), la, precision=HI)
        tot = e[t - 1:t, :]
        r = e[mid:mid + 1, :]
        fqi, fki = jnp.exp(e - r), jnp.exp(r - e)
        fq, fk = jnp.exp(e), jnp.exp(tot - e)
        mask = row >= col
    else:
        e = _dot((col < row).astype(F32), la, precision=HI)
        tot = e[t - 1:t, :] + la[t - 1:t, :]
        r = e[mid:mid + 1, :]
        fqi, fki = jnp.exp(r - e), jnp.exp(e - r)
        fq, fk = jnp.exp(tot - e), jnp.exp(e)
        mask = col >= row
    dec = jnp.exp(tot)
    qf = q_ref[...].astype(F32) * (dk ** -0.5)
    kf = k_ref[...].astype(F32)
    rows = pl.ds(pl.multiple_of(cpos * t, t), t)
    for h in range(GLA_HEADS):
        sl = slice(dk * h, dk * (h + 1))
        vl = slice(dv * h, dv * (h + 1))
        qh, kh = qf[:, sl], kf[:, sl]
        sc = _dot_nt((qh * fqi[:, sl]).astype(BF16), (kh * fki[:, sl]).astype(BF16))
        sc = jnp.where(mask, sc, 0.0)
        vh = v_ref[:, vl]
        st = S[h]
        o = _dot(sc.astype(BF16), vh) + _dot_nt((qh * fq[:, sl]).astype(BF16), st.astype(BF16))
        S[h] = st * dec[:, sl] + _dot_tn(vh, (kh * fk[:, sl]).astype(BF16))
        if d == 0:
            of[rows, vl] = o
        else:
            ot = of[rows, vl] + o
            ms = jnp.mean(ot * ot, axis=-1, keepdims=True)
            on = ot * lax.rsqrt(ms + EPS) * nrm_ref[...]
            o_ref[:, vl] = (on * _silu(og_ref[:, vl].astype(F32))).astype(o_ref.dtype)

    @pl.when(last == 1)
    def _():
        for h in range(GLA_HEADS):
            sout_ref[0, h] = S[h].T


def _gla_kernel(blk_ref, sid_ref, ph_ref, cpos_ref, first_ref, last_ref, yblk_ref,
                q_ref, k_ref, v_ref, og_ref, glr_ref, s0f_ref, s0b_ref, w2_ref, gb_ref, nrm_ref,
                o_ref, sf_ref, sb_ref, S, of):
    s = pl.program_id(0)

    @pl.when(ph_ref[s] == 0)
    def _():
        _gla_dir(0, cpos_ref[s], first_ref[s], last_ref[s], q_ref, k_ref, v_ref, og_ref, glr_ref,
                 s0f_ref, w2_ref, gb_ref, nrm_ref, o_ref, sf_ref, S, of)

    @pl.when(ph_ref[s] == 1)
    def _():
        _gla_dir(1, cpos_ref[s], first_ref[s], last_ref[s], q_ref, k_ref, v_ref, og_ref, glr_ref,
                 s0b_ref, w2_ref, gb_ref, nrm_ref, o_ref, sb_ref, S, of)


def gla_mixer(lay, proj, small, s0f, s0b, gate_w2, gate_b, gla_norm):
    n_tok = proj.shape[0]
    t = GLA_BLOCK
    tabs = lay.scan_tables(t)
    nsteps = int(tabs[0].shape[0])
    max_len = max(ln for _, ln in lay.seqs)
    qk_w = GLA_HEADS * GLA_KEY_DIM
    v_w = GLA_HEADS * GLA_VAL_DIM
    blk = lambda w, cb: pl.BlockSpec((t, w), lambda s, b, *_: (b[s], cb))
    seq4 = pl.BlockSpec((1, GLA_HEADS, GLA_KEY_DIM, GLA_VAL_DIM), lambda s, b, sid, *_: (sid[s], 0, 0, 0))
    full = lambda shp: pl.BlockSpec(shp, lambda s, *_: (0,) * len(shp))
    grid_spec = pltpu.PrefetchScalarGridSpec(
        num_scalar_prefetch=7, grid=(nsteps,),
        in_specs=[blk(qk_w, PJ_Q // qk_w), blk(qk_w, PJ_K // qk_w), blk(v_w, PJ_V // v_w),
                  blk(v_w, PJ_OG // v_w), blk(LANES, 0), seq4, seq4,
                  full((2, GLA_RANK, qk_w)), full((2, qk_w)), full((1, GLA_VAL_DIM))],
        out_specs=[pl.BlockSpec((t, v_w), lambda s, b, sid, ph, cp, f, l, yb: (yb[s], 0)), seq4, seq4],
        scratch_shapes=[pltpu.VMEM((GLA_HEADS, GLA_VAL_DIM, GLA_KEY_DIM), F32),
                        pltpu.VMEM((max_len, v_w), F32)])
    st_shape = jax.ShapeDtypeStruct((lay.n_seq, GLA_HEADS, GLA_KEY_DIM, GLA_VAL_DIM), F32)
    return pl.pallas_call(
        _gla_kernel, grid_spec=grid_spec,
        out_shape=[jax.ShapeDtypeStruct((n_tok, v_w), BF16), st_shape, st_shape],
        compiler_params=_cp(("arbitrary",)), name="gla_mixer",
    )(*tabs, proj, proj, proj, proj, small, s0f, s0b, gate_w2, gate_b, gla_norm.reshape(1, -1))


def _res_kernel(*refs, ks):
    n = len(ks)
    a_refs, w_ref, x_ref, gate_ref, o_ref = refs[:n], refs[n], refs[n + 1], refs[n + 2], refs[n + 3]
    acc = None
    off = 0
    for a_ref, k in zip(a_refs, ks):
        part = _dot(a_ref[...], w_ref[off:off + k, :])
        acc = part if acc is None else acc + part
        off += k
    o_ref[...] = x_ref[...] + gate_ref[...] * acc


def proj_residual(lay, acts, w, x, gate, tm=512):
    n_tok = x.shape[0]
    ks = tuple(int(a.shape[1]) for a in acts)
    mrow = lambda i: (lay.mod_row(i * tm), 0, 0)
    return pl.pallas_call(
        functools.partial(_res_kernel, ks=ks), grid=(n_tok // tm,),
        in_specs=[pl.BlockSpec((tm, k), lambda i: (i, 0)) for k in ks]
        + [pl.BlockSpec(w.shape, lambda i: (0, 0)),
           pl.BlockSpec((tm, D), lambda i: (i, 0)),
           pl.BlockSpec((None, 1, D), mrow)],
        out_specs=pl.BlockSpec((tm, D), lambda i: (i, 0)),
        out_shape=jax.ShapeDtypeStruct((n_tok, D), F32),
        compiler_params=_cp(("arbitrary",)), name="proj_residual",
    )(*acts, w, x, gate)


def _router_kernel(x_ref, g_ref, sc_ref, sh_ref, rw_ref, rb_ref,
                   h_ref, idx_ref, gate_ref, pos_ref, posT_ref, cnt_ref):
    tm = x_ref.shape[0]
    h = _modnorm(x_ref[...], g_ref[...], sc_ref[...], sh_ref[...])
    h_ref[...] = h.astype(BF16)
    lg = _dot(h, rw_ref[...], precision=HI) + rb_ref[...]
    lane = lax.broadcasted_iota(I32, (tm, LANES), 1).astype(F32)
    vals, ids = [], []
    for _ in range(TOP_K):
        m = jnp.max(lg, axis=1, keepdims=True)
        i = jnp.min(jnp.where(lg == m, lane, float(LANES)), axis=1, keepdims=True)
        vals.append(m)
        ids.append(i)
        lg = jnp.where(lane == i, -jnp.inf, lg)
    ex = [jnp.exp(v - vals[0]) for v in vals]
    den = ex[0] + ex[1] + ex[2] + ex[3]
    sel = jnp.zeros((tm, LANES), F32)
    for i in ids:
        sel = sel + (lane == i).astype(F32)
    row = lax.broadcasted_iota(I32, (tm, tm), 0)
    col = lax.broadcasted_iota(I32, (tm, tm), 1)
    before = _dot((col < row).astype(BF16), sel.astype(BF16))
    n = jnp.sum(sel, axis=0, keepdims=True)
    er = lax.broadcasted_iota(I32, (LANES, LANES), 0)
    ec = lax.broadcasted_iota(I32, (LANES, LANES), 1)
    n_al = jnp.ceil(n * (1.0 / SEG_ALIGN)) * SEG_ALIGN
    offs = _dot(jnp.broadcast_to(n_al, (8, LANES)).astype(BF16), (er < ec).astype(BF16))[0:1, :]
    slot = before + offs
    idx_o = jnp.zeros((tm, LANES), F32)
    gate_o = jnp.zeros((tm, LANES), F32)
    pos_o = jnp.zeros((tm, LANES), F32)
    for k in range(TOP_K):
        p = jnp.sum(jnp.where(lane == ids[k], slot, 0.0), axis=1, keepdims=True)
        idx_o = jnp.where(lane == k, ids[k], idx_o)
        gate_o = jnp.where(lane == k, ex[k] / den, gate_o)
        pos_o = jnp.where(lane == k, p, pos_o)
    idx_ref[...] = idx_o.astype(I32)
    gate_ref[...] = gate_o
    pos_ref[...] = pos_o.astype(I32)
    posT_ref[...] = pos_o.T[0:8, :]
    cnt_ref[0] = jnp.broadcast_to(n, (8, LANES))


def moe_router(lay, x, g, sc, sh, rw, rb):
    n_tok = x.shape[0]
    tm = TOK_TILE
    nt = n_tok // tm
    mrow = lambda i: (lay.mod_row(i * tm), 0, 0)
    tile = lambda w, dt: (pl.BlockSpec((tm, w), lambda i: (i, 0)), jax.ShapeDtypeStruct((n_tok, w), dt))
    outs = [tile(D, BF16), tile(LANES, I32), tile(LANES, F32), tile(LANES, I32),
            (pl.BlockSpec((8, tm), lambda i: (0, i)), jax.ShapeDtypeStruct((8, n_tok), F32)),
            (pl.BlockSpec((1, 8, LANES), lambda i: (i, 0, 0)), jax.ShapeDtypeStruct((nt, 8, LANES), F32))]
    return pl.pallas_call(
        _router_kernel, grid=(nt,),
        in_specs=[pl.BlockSpec((tm, D), lambda i: (i, 0)),
                  pl.BlockSpec((1, D), lambda i: (0, 0)),
                  pl.BlockSpec((None, 1, D), mrow), pl.BlockSpec((None, 1, D), mrow),
                  pl.BlockSpec((D, LANES), lambda i: (0, 0)),
                  pl.BlockSpec((1, LANES), lambda i: (0, 0))],
        out_specs=[o[0] for o in outs], out_shape=[o[1] for o in outs],
        compiler_params=_cp(("arbitrary",)), name="moe_router",
    )(x, g.reshape(1, D), sc, sh, rw, rb)


SEG_ALIGN = 8
SEG_BITS = tuple(range(int(math.log2(TOK_TILE)), int(math.log2(SEG_ALIGN)) - 1, -1))
TILE_ROWS = TOK_TILE * TOP_K + N_EXPERTS * SEG_ALIGN


def _pow2_copies(n, src, dst, make_copy, op, bits):
    for b in bits:
        sz = 1 << b
        done = (n >> (b + 1)) << (b + 1)

        @pl.when((n & sz) != 0)
        def _():
            op(make_copy(pl.multiple_of(src + done, SEG_ALIGN), pl.multiple_of(dst + done, SEG_ALIGN), sz))


def _segment_copies(i, n_ref, off_ref, dst_ref, make_copy, op):
    def body(e, carry):
        k = i * N_EXPERTS + e
        _pow2_copies(n_ref[k], off_ref[k], dst_ref[k], make_copy, op, SEG_BITS)
        return carry
    lax.fori_loop(0, N_EXPERTS, body, 0)


TAIL_BITS = tuple(range(int(math.log2(MOE_BLOCK)) - 1, int(math.log2(SEG_ALIGN)) - 1, -1))


def _dispatch_kernel(n_ref, off_ref, dst_ref, tn_ref, td_ref, posT_ref, h_ref, xout_ref, srt, zbuf, sem):
    i = pl.program_id(0)
    tm = h_ref.shape[0]
    r = lax.broadcasted_iota(I32, (TILE_ROWS, tm), 0)
    hit = jnp.zeros((TILE_ROWS, tm), jnp.bool_)
    for k in range(TOP_K):
        hit = hit | (r == posT_ref[k:k + 1, :].astype(I32))
    sel = jnp.where(hit, 1.0, 0.0).astype(BF16)
    srt[...] = _dot(sel, h_ref[...])

    def make_copy(src, dst, sz):
        return pltpu.make_async_copy(srt.at[pl.ds(src, sz)], xout_ref.at[pl.ds(dst, sz)], sem)

    _segment_copies(i, n_ref, off_ref, dst_ref, make_copy, lambda c: c.start())
    _segment_copies(i, n_ref, off_ref, dst_ref, make_copy, lambda c: c.wait())

    @pl.when(i == pl.num_programs(0) - 1)
    def _():
        zbuf[...] = jnp.zeros_like(zbuf)

        def zero_copy(src, dst, sz):
            return pltpu.make_async_copy(zbuf.at[pl.ds(src, sz)], xout_ref.at[pl.ds(dst, sz)], sem)

        nb = xout_ref.shape[0] // MOE_BLOCK
        for op in (lambda c: c.start(), lambda c: c.wait()):
            def body(e, carry):
                _pow2_copies(tn_ref[e], 0, td_ref[e], zero_copy, op, TAIL_BITS)
                return carry
            lax.fori_loop(0, N_EXPERTS, body, 0)

            def unused(b, carry):
                op(zero_copy(0, pl.multiple_of(b * MOE_BLOCK, MOE_BLOCK), MOE_BLOCK))
                return carry
            lax.fori_loop(tn_ref[N_EXPERTS], nb, unused, 0)


def moe_dispatch(n_tab, off_tab, dst_tab, tail_n, tail_dst, posT, h2, n_rows):
    n_tok = h2.shape[0]
    tm = TOK_TILE
    grid_spec = pltpu.PrefetchScalarGridSpec(
        num_scalar_prefetch=5, grid=(n_tok // tm,),
        in_specs=[pl.BlockSpec((8, tm), lambda i, *_: (0, i)),
                  pl.BlockSpec((tm, D), lambda i, *_: (i, 0))],
        out_specs=pl.BlockSpec(memory_space=pl.ANY),
        scratch_shapes=[pltpu.VMEM((TILE_ROWS, D), F32), pltpu.VMEM((MOE_BLOCK, D), F32),
                        pltpu.SemaphoreType.DMA])
    return pl.pallas_call(
        _dispatch_kernel, grid_spec=grid_spec,
        out_shape=jax.ShapeDtypeStruct((n_rows, D), F32),
        compiler_params=_cp(("arbitrary",)), name="moe_dispatch",
    )(n_tab, off_tab, dst_tab, tail_n, tail_dst, posT, h2)


def _combine_kernel(n_ref, off_ref, dst_ref, pos_ref, gate_ref, x_ref, g2_ref, y_ref, o_ref, buf, sem):
    i = pl.program_id(0)
    tm = x_ref.shape[0]
    na = TILE_ROWS
    buf[tm * TOP_K:na, :] = jnp.zeros((na - tm * TOP_K, D), F32)

    def make_copy(src, dst, sz):
        return pltpu.make_async_copy(y_ref.at[pl.ds(dst, sz)], buf.at[pl.ds(src, sz)], sem)

    _segment_copies(i, n_ref, off_ref, dst_ref, make_copy, lambda c: c.start())
    _segment_copies(i, n_ref, off_ref, dst_ref, make_copy, lambda c: c.wait())
    lane = lax.broadcasted_iota(I32, (tm, na), 1)
    pw = jnp.zeros((tm, na), F32)
    for k in range(TOP_K):
        pw = pw + jnp.where(lane == pos_ref[:, k:k + 1], gate_ref[:, k:k + 1], 0.0)
    phi = pw.astype(BF16)
    plo = (pw - phi.astype(F32)).astype(BF16)
    yb = buf[...].astype(BF16)
    o_ref[...] = x_ref[...] + g2_ref[...] * (_dot(phi, yb) + _dot(plo, yb))


def moe_combine(lay, n_tab, off_tab, dst_tab, pos, gates, x, gate2, y_rows):
    n_tok = x.shape[0]
    tm = TOK_TILE
    mrow = lambda i, *_: (lay.mod_row(i * tm), 0, 0)
    grid_spec = pltpu.PrefetchScalarGridSpec(
        num_scalar_prefetch=3, grid=(n_tok // tm,),
        in_specs=[pl.BlockSpec((tm, LANES), lambda i, *_: (i, 0)),
                  pl.BlockSpec((tm, LANES), lambda i, *_: (i, 0)),
                  pl.BlockSpec((tm, D), lambda i, *_: (i, 0)),
                  pl.BlockSpec((None, 1, D), mrow),
                  pl.BlockSpec(memory_space=pl.ANY)],
        out_specs=pl.BlockSpec((tm, D), lambda i, *_: (i, 0)),
        scratch_shapes=[pltpu.VMEM((TILE_ROWS, D), F32), pltpu.SemaphoreType.DMA])
    return pl.pallas_call(
        _combine_kernel, grid_spec=grid_spec,
        out_shape=jax.ShapeDtypeStruct((n_tok, D), F32),
        compiler_params=_cp(("arbitrary",)), name="moe_combine",
    )(n_tab, off_tab, dst_tab, pos, gates, x, gate2, y_rows)


def _expert_kernel(be_ref, nv_ref, x_ref, wg_ref, bg_ref, wu_ref, bu_ref, wd_ref, bd_ref, y_ref,
                   wg_s, wu_s, wd_s):
    i = pl.program_id(0)
    valid = i < nv_ref[0]
    changed = jnp.logical_or(i == 0, be_ref[i] != be_ref[jnp.maximum(i - 1, 0)])

    @pl.when(jnp.logical_and(valid, changed))
    def _():
        rc = 128

        def cast(c, carry):
            rows = pl.ds(pl.multiple_of(c * rc, rc), rc)
            wg_s[rows, :] = wg_ref[rows, :].astype(BF16)
            wu_s[rows, :] = wu_ref[rows, :].astype(BF16)
            wd_s[rows, :] = wd_ref[rows, :].astype(BF16)
            return carry
        lax.fori_loop(0, D // rc, cast, 0)

    @pl.when(valid)
    def _():
        x = x_ref[...].astype(BF16)
        gt = jnp.minimum(_dot(x, wg_s[...]) + bg_ref[...], SWIGLU_LIMIT)
        up = jnp.clip(_dot(x, wu_s[...]) + bu_ref[...], -SWIGLU_LIMIT, SWIGLU_LIMIT)
        act = (up + 1.0) * gt * _sigmoid(SWIGLU_ALPHA * gt)
        y_ref[...] = _dot(act.astype(BF16), wd_s[...]) + bd_ref[...]

    @pl.when(jnp.logical_not(valid))
    def _():
        y_ref[...] = jnp.zeros_like(y_ref)


def moe_experts(layer, blk_expert, n_valid, x_rows, w_gate, b_gate, w_up, b_up, w_down, b_down):
    n_rows = x_rows.shape[0]
    nb = n_rows // MOE_BLOCK
    depth, ne, _, ff = w_gate.shape
    rowblk = lambda i, be, nv: (jnp.minimum(i, nv[0] - 1), 0)
    wsel = lambda i, be, nv: (layer, be[i], 0, 0)
    grid_spec = pltpu.PrefetchScalarGridSpec(
        num_scalar_prefetch=2, grid=(nb,),
        in_specs=[pl.BlockSpec((MOE_BLOCK, D), rowblk),
                  pl.BlockSpec((None, None, D, ff), wsel), pl.BlockSpec((None, None, 1, ff), wsel),
                  pl.BlockSpec((None, None, D, ff), wsel), pl.BlockSpec((None, None, 1, ff), wsel),
                  pl.BlockSpec((None, None, ff, D), wsel), pl.BlockSpec((None, None, 1, D), wsel)],
        out_specs=pl.BlockSpec((MOE_BLOCK, D), lambda i, be, nv: (i, 0)),
        scratch_shapes=[pltpu.VMEM((D, ff), BF16), pltpu.VMEM((D, ff), BF16), pltpu.VMEM((ff, D), BF16)])
    return pl.pallas_call(
        _expert_kernel, grid_spec=grid_spec,
        out_shape=jax.ShapeDtypeStruct((n_rows, D), F32),
        compiler_params=_cp(("arbitrary",)), name="moe_experts",
    )(blk_expert, n_valid, x_rows, w_gate, b_gate.reshape(depth, ne, 1, ff), w_up,
      b_up.reshape(depth, ne, 1, ff), w_down, b_down.reshape(depth, ne, 1, D))


def moe_layer(lay, layer, x, g2, sc2, sh2, gate2, router_w, router_b, w_gate, b_gate, w_up, b_up, w_down,
              b_down):
    n_tok = x.shape[0]
    nt = n_tok // TOK_TILE
    rw = jnp.zeros((D, LANES), F32).at[:, :N_EXPERTS].set(router_w)
    rb = jnp.full((1, LANES), NEG, F32).at[0, :N_EXPERTS].set(router_b)
    h2, _, gates, pos, posT, cnt = moe_router(lay, x, g2, sc2, sh2, rw, rb)
    n_te = cnt[:, 0, :N_EXPERTS].astype(I32)
    n_te = (n_te + SEG_ALIGN - 1) // SEG_ALIGN * SEG_ALIGN
    totals = jnp.sum(n_te, axis=0)
    padded = (totals + MOE_BLOCK - 1) // MOE_BLOCK * MOE_BLOCK
    padded_end = jnp.cumsum(padded)
    pstart = padded_end - padded
    dst = pstart[None, :] + jnp.cumsum(n_te, axis=0) - n_te
    off = jnp.cumsum(n_te, axis=1) - n_te
    n_rows = nt * TILE_ROWS + N_EXPERTS * MOE_BLOCK
    nb = n_rows // MOE_BLOCK
    n_valid = (padded_end[-1] // MOE_BLOCK).astype(I32).reshape(1)
    bstart = jnp.minimum(jnp.arange(nb, dtype=I32), n_valid[0] - 1) * MOE_BLOCK
    blk_expert = jnp.minimum(jnp.sum((bstart[:, None] >= padded_end[None, :]).astype(I32), axis=1),
                             N_EXPERTS - 1).astype(I32)
    tabs = (n_te.reshape(-1).astype(I32), off.reshape(-1).astype(I32), dst.reshape(-1).astype(I32))
    tail_n = jnp.concatenate([(padded - totals).astype(I32), n_valid])
    x_rows = moe_dispatch(*tabs, tail_n, (pstart + totals).astype(I32), posT, h2, n_rows)
    y_rows = moe_experts(layer, blk_expert, n_valid, x_rows, w_gate, b_gate, w_up, b_up, w_down, b_down)
    return moe_combine(lay, *tabs, pos, gates, x, gate2, y_rows)


QKV_TN = 256
N_QK_TILES = (ATT_HEADS + ATT_KV) * ATT_HD // QKV_TN


def _qkv_kernel(x_ref, g_ref, sc_ref, sh_ref, w_ref, nw_ref, cos_ref, sin_ref, o_ref, *, p_tok):
    i = pl.program_id(0)
    tm = x_ref.shape[0]
    h = _modnorm(x_ref[...], g_ref[...], sc_ref[...], sh_ref[...]).astype(BF16)
    r = lax.broadcasted_iota(I32, (QKV_TN, QKV_TN), 0) // ATT_HD
    c = lax.broadcasted_iota(I32, (QKV_TN, QKV_TN), 1) // ATT_HD
    head_mean = jnp.where(r == c, 1.0 / ATT_HD, 0.0).astype(BF16)
    lane = lax.broadcasted_iota(I32, (tm, QKV_TN), 1)
    half = ATT_HD // 4
    first = (lane % (2 * half)) < half
    for j in range(w_ref.shape[1] // QKV_TN):
        cols = slice(QKV_TN * j, QKV_TN * (j + 1))
        acc = _dot(h, w_ref[:, cols])
        if j >= N_QK_TILES:
            o_ref[:, cols] = acc
            continue
        ms = _dot((acc * acc).astype(BF16), head_mean)
        qn = acc * lax.rsqrt(ms + EPS) * nw_ref[j]

        @pl.when(i * tm < p_tok)
        def _():
            o_ref[:, cols] = qn

        @pl.when(i * tm >= p_tok)
        def _():
            swapped = jnp.where(first, pltpu.roll(qn, QKV_TN - half, 1), pltpu.roll(qn, half, 1))
            o_ref[:, cols] = qn * cos_ref[...] + swapped * sin_ref[...]


def _rope_tables(sample_len):
    pos = np.arange(sample_len)
    half = ATT_HD // 4
    inv = (ROPE_THETA ** (-np.arange(half, dtype=np.float32) / half)).astype(np.float32)
    ang_r = (pos // GRID_W).astype(np.float32)[:, None] * inv[None, :]
    ang_c = (pos % GRID_W).astype(np.float32)[:, None] * inv[None, :]
    cos = np.concatenate([np.cos(ang_r)] * 2 + [np.cos(ang_c)] * 2, axis=1)
    sin = np.concatenate([-np.sin(ang_r), np.sin(ang_r), -np.sin(ang_c), np.sin(ang_c)], axis=1)
    rep = QKV_TN // ATT_HD
    return (jnp.asarray(np.tile(cos, (1, rep)), F32), jnp.asarray(np.tile(sin, (1, rep)), F32))


def qkv_proj(lay, x, g, sc, sh, w, q_norm, k_norm, tm=256):
    n_tok = x.shape[0]
    n = w.shape[1]
    nq = ATT_HEADS * ATT_HD // QKV_TN
    rep = QKV_TN // ATT_HD
    nw = jnp.concatenate([jnp.tile(jnp.tile(q_norm, rep)[None, :], (nq, 1)),
                          jnp.tile(jnp.tile(k_norm, rep)[None, :], (n // QKV_TN - nq, 1))], axis=0)
    cos, sin = _rope_tables(lay.sample_len)
    mrow = lambda i: (lay.mod_row(i * tm), 0, 0)
    rrow = lambda i: (jnp.where(i * tm < lay.p_tok, 0, ((i * tm - lay.p_tok) % lay.sample_len) // tm), 0)
    nt = n // QKV_TN
    return pl.pallas_call(
        functools.partial(_qkv_kernel, p_tok=lay.p_tok), grid=(n_tok // tm,),
        in_specs=[pl.BlockSpec((tm, D), lambda i: (i, 0)),
                  pl.BlockSpec((1, D), lambda i: (0, 0)),
                  pl.BlockSpec((None, 1, D), mrow), pl.BlockSpec((None, 1, D), mrow),
                  pl.BlockSpec((D, n), lambda i: (0, 0)),
                  pl.BlockSpec((nt, 1, QKV_TN), lambda i: (0, 0, 0)),
                  pl.BlockSpec((tm, QKV_TN), rrow), pl.BlockSpec((tm, QKV_TN), rrow)],
        out_specs=pl.BlockSpec((tm, n), lambda i: (i, 0)),
        out_shape=jax.ShapeDtypeStruct((n_tok, n), F32),
        compiler_params=_cp(("arbitrary",)), name="qkv_proj",
    )(x, g.reshape(1, D), sc, sh, w, nw.reshape(nt, 1, QKV_TN), cos, sin)


def _dup_group(x, g):
    blk = x[:, LANES * (g // 2):LANES * (g // 2 + 1)]
    if g % 2 == 1:
        blk = pltpu.roll(blk, ATT_HD, 1)
    lo = lax.broadcasted_iota(I32, blk.shape, 1) < ATT_HD
    low = jnp.where(lo, blk, 0.0)
    return low + pltpu.roll(low, ATT_HD, 1)


def _attend(q_ref, k_all, v_all, mask, sink_ref, o_ref):
    nq = q_ref.shape[0]
    lo = lax.broadcasted_iota(I32, (nq, LANES), 1) < ATT_HD
    grp = ATT_HEADS // ATT_KV
    for g in range(ATT_KV):
        k2 = _dup_group(k_all, g).astype(BF16)
        v2 = _dup_group(v_all, g).astype(BF16)
        for jp in range(grp // 2):
            j = g * (grp // 2) + jp
            qp = q_ref[:, LANES * j:LANES * (j + 1)] * (ATT_HD ** -0.5)
            outs = []
            for half in range(2):
                qh = jnp.where(lo, qp, 0.0) if half == 0 else jnp.where(lo, 0.0, qp)
                s = _dot_nt(qh.astype(BF16), k2)
                if mask is not None:
                    s = jnp.where(mask, s, NEG)
                sink = sink_ref[2 * j + half]
                m = jnp.maximum(jnp.max(s, axis=1, keepdims=True), sink)
                p = jnp.exp(s - m)
                den = jnp.sum(p, axis=1, keepdims=True) + jnp.exp(sink - m)
                outs.append(_dot(p.astype(BF16), v2) / den)
            o_ref[:, LANES * j:LANES * (j + 1)] = jnp.where(lo, outs[0], outs[1]).astype(o_ref.dtype)


def _attn_ctx_kernel(sink_ref, q_ref, k_ref, v_ref, o_ref):
    _attend(q_ref, k_ref[...], v_ref[...], None, sink_ref, o_ref)


def attn_context(lay, qkv, sinks):
    qw = ATT_HEADS * ATT_HD
    kw = ATT_KV * ATT_HD
    ln = lay.prompt_len
    grid_spec = pltpu.PrefetchScalarGridSpec(
        num_scalar_prefetch=0, grid=(lay.n_prompt,),
        in_specs=[pl.BlockSpec(memory_space=pltpu.SMEM),
                  pl.BlockSpec((ln, qw), lambda b: (b, 0)),
                  pl.BlockSpec((ln, kw), lambda b: (b, qw // kw)),
                  pl.BlockSpec((ln, kw), lambda b: (b, qw // kw + 1))],
        out_specs=pl.BlockSpec((ln, qw), lambda b: (b, 0)))
    return pl.pallas_call(
        _attn_ctx_kernel, grid_spec=grid_spec,
        out_shape=jax.ShapeDtypeStruct((lay.p_tok, qw), BF16),
        compiler_params=_cp(("arbitrary",)), name="attn_context",
    )(sinks, qkv, qkv, qkv)


def _attn_lat_kernel(sink_ref, q_ref, kp_ref, kc_ref, kn_ref, vp_ref, vc_ref, vn_ref, ck_ref, cv_ref, o_ref,
                     *, nblk):
    i = pl.program_id(1)
    bq = ATT_BLOCK
    nctx = ck_ref.shape[1]
    k_all = jnp.concatenate([kp_ref[...], kc_ref[...], kn_ref[...], ck_ref[0]], axis=0)
    v_all = jnp.concatenate([vp_ref[...], vc_ref[...], vn_ref[...], cv_ref[0]], axis=0)
    ns = 3 * bq + nctx
    r = lax.broadcasted_iota(I32, (bq, ns), 0)
    c = lax.broadcasted_iota(I32, (bq, ns), 1)
    rel = c - r
    first_key = jnp.where(i > 0, 0, bq)
    end_key = jnp.where(i < nblk - 1, 3 * bq, 2 * bq)
    band = (rel >= bq - WINDOW) & (rel <= bq + WINDOW) & (c >= first_key) & (c < end_key)
    mask = band | (c >= 3 * bq)
    _attend(q_ref, k_all, v_all, mask, sink_ref, o_ref)


def attn_latent(lay, qkv, cache_k, cache_v, sinks):
    qw = ATT_HEADS * ATT_HD
    kw = ATT_KV * ATT_HD
    bq = ATT_BLOCK
    nblk = lay.sample_len // bq
    b0 = lay.p_tok // bq
    nctx = cache_k.shape[1]
    rb = lambda b, i: b0 + b * nblk + i
    kspec = lambda cb, sh: pl.BlockSpec(
        (bq, kw), lambda b, i: (b0 + b * nblk + jnp.clip(i + sh, 0, nblk - 1), cb))
    kc, vc = qw // kw, qw // kw + 1
    grid_spec = pltpu.PrefetchScalarGridSpec(
        num_scalar_prefetch=0, grid=(lay.n_sample, nblk),
        in_specs=[pl.BlockSpec(memory_space=pltpu.SMEM),
                  pl.BlockSpec((bq, qw), lambda b, i: (rb(b, i), 0)),
                  kspec(kc, -1), kspec(kc, 0), kspec(kc, 1),
                  kspec(vc, -1), kspec(vc, 0), kspec(vc, 1),
                  pl.BlockSpec((1, nctx, kw), lambda b, i: (b, 0, 0)),
                  pl.BlockSpec((1, nctx, kw), lambda b, i: (b, 0, 0))],
        out_specs=pl.BlockSpec((bq, qw), lambda b, i: (b * nblk + i, 0)))
    return pl.pallas_call(
        functools.partial(_attn_lat_kernel, nblk=nblk), grid_spec=grid_spec,
        out_shape=jax.ShapeDtypeStruct((lay.n_sample * lay.sample_len, qw), BF16),
        compiler_params=_cp(("arbitrary", "arbitrary")), name="attn_latent",
    )(sinks, qkv, qkv, qkv, qkv, qkv, qkv, qkv,
      cache_k.reshape(lay.n_sample, nctx, kw), cache_v.reshape(lay.n_sample, nctx, kw))


def _forward(lay, x_prompt, x_sample, state_l0_ssd_fwd, state_l0_ssd_bwd, state_l0_gla_fwd, state_l0_gla_bwd,
             cache_l1_k, cache_l1_v, c, c_ctx, ada_w, ada_b, norm1, norm2,
             l0_w_in, l0_conv_w, l0_conv_b, l0_a_log, l0_dt_bias, l0_d_skip, l0_ssd_norm,
             l0_gate_w2, l0_gate_b, l0_gla_norm, l0_w_out,
             l1_w_qkv, l1_q_norm, l1_k_norm, l1_sinks, l1_w_out,
             router_w, router_b, exp_w_gate, exp_b_gate, exp_w_up, exp_b_up, exp_w_down, exp_b_down):
    np_, ns = lay.n_prompt, lay.n_sample
    x = jnp.concatenate([x_prompt.reshape(-1, D), x_sample.reshape(-1, D)], axis=0)
    cond8 = jnp.zeros((8, D), F32).at[0].set(c_ctx).at[1:1 + ns].set(c)
    mod = ada_table(cond8, ada_w, ada_b)
    mods = [[mod[l, :, p * D:(p + 1) * D].reshape(8, 1, D) for p in range(N_ADA)] for l in range(2)]

    def moe(l, xx):
        return moe_layer(lay, l, xx, norm2[l], mods[l][4], mods[l][3], mods[l][5], router_w[l], router_b[l],
                         exp_w_gate, exp_b_gate, exp_w_up, exp_b_up, exp_w_down, exp_b_down)

    sp = np.cumsum((SSD_INNER, SSD_INNER + 2 * SSD_GROUPS * SSD_STATE, 2 * SSD_HEADS,
                    GLA_HEADS * GLA_KEY_DIM, GLA_HEADS * GLA_KEY_DIM,
                    GLA_HEADS * GLA_VAL_DIM, GLA_HEADS * GLA_VAL_DIM, 2 * GLA_RANK))
    cols = lambda a, b: l0_w_in[:, a:b]
    w_main = jnp.concatenate([cols(0, sp[0]), cols(sp[4], sp[5]), cols(sp[5], sp[6]), cols(sp[0], sp[1]),
                              cols(sp[2], sp[3]), cols(sp[3], sp[4])], axis=1).astype(BF16)
    w_small = jnp.concatenate([cols(sp[1], sp[2]), cols(sp[6], sp[7]),
                               jnp.zeros((D, LANES - 2 * SSD_HEADS - 2 * GLA_RANK), F32)], axis=1)
    proj, small = norm_proj(lay, x, norm1[0], mods[0][1], mods[0][0], w_main, w_small, 512, PJ_W // 2, BF16)
    xbc = conv_silu(lay, proj, l0_conv_w, l0_conv_b)
    hp = SSD_HEADS * SSD_HEAD_DIM
    zs = jnp.zeros((np_, hp, SSD_STATE), F32)
    zg = jnp.zeros((np_,) + state_l0_gla_fwd.shape[1:], F32)
    y_n, ssd_f, ssd_b = ssd_mixer(
        lay, proj, xbc, small, small.T,
        jnp.concatenate([zs, state_l0_ssd_fwd.reshape(ns, hp, SSD_STATE)], axis=0),
        jnp.concatenate([zs, state_l0_ssd_bwd.reshape(ns, hp, SSD_STATE)], axis=0),
        l0_a_log, l0_dt_bias, l0_d_skip, l0_ssd_norm)
    o_n, gla_f, gla_b = gla_mixer(
        lay, proj, small,
        jnp.concatenate([zg, state_l0_gla_fwd], axis=0), jnp.concatenate([zg, state_l0_gla_bwd], axis=0),
        l0_gate_w2, l0_gate_b, l0_gla_norm)
    x = proj_residual(lay, [y_n, o_n], l0_w_out.astype(BF16), x, mods[0][2])
    x = moe(0, x)

    qkv = qkv_proj(lay, x, norm1[1], mods[1][1], mods[1][0], l1_w_qkv.astype(BF16), l1_q_norm, l1_k_norm)
    o_ctx = attn_context(lay, qkv, l1_sinks)
    o_lat = attn_latent(lay, qkv, cache_l1_k, cache_l1_v, l1_sinks)
    o = jnp.concatenate([o_ctx, o_lat], axis=0)
    x = proj_residual(lay, [o], l1_w_out.astype(BF16), x, mods[1][2])
    x = moe(1, x)

    qw = ATT_HEADS * ATT_HD
    kw = ATT_KV * ATT_HD
    return (x[:lay.p_tok].reshape(x_prompt.shape), x[lay.p_tok:].reshape(x_sample.shape),
            ssd_f[:np_].reshape(np_, SSD_HEADS, SSD_HEAD_DIM, SSD_STATE),
            ssd_b[:np_].reshape(np_, SSD_HEADS, SSD_HEAD_DIM, SSD_STATE),
            gla_f[:np_], gla_b[:np_],
            qkv[:lay.p_tok, qw:qw + kw].reshape(np_, lay.prompt_len, ATT_KV, ATT_HD),
            qkv[:lay.p_tok, qw + kw:].reshape(np_, lay.prompt_len, ATT_KV, ATT_HD))


def kernel(x_prompt, x_sample, state_l0_ssd_fwd, state_l0_ssd_bwd, state_l0_gla_fwd, state_l0_gla_bwd, cache_l1_k, cache_l1_v, c, c_ctx, ada_w, ada_b, norm1, norm2, l0_w_in, l0_conv_w, l0_conv_b, l0_a_log, l0_dt_bias, l0_d_skip, l0_ssd_norm, l0_gate_w2, l0_gate_b, l0_gla_norm, l0_w_out, l1_w_qkv, l1_q_norm, l1_k_norm, l1_sinks, l1_w_out, router_w, router_b, exp_w_gate, exp_b_gate, exp_w_up, exp_b_up, exp_w_down, exp_b_down):
    lay = Layout(x_prompt.shape[0], x_prompt.shape[1], x_sample.shape[0], x_sample.shape[1])
    return _forward(lay, x_prompt, x_sample, state_l0_ssd_fwd, state_l0_ssd_bwd, state_l0_gla_fwd,
                    state_l0_gla_bwd, cache_l1_k, cache_l1_v, c, c_ctx, ada_w, ada_b, norm1, norm2,
                    l0_w_in, l0_conv_w, l0_conv_b, l0_a_log, l0_dt_bias, l0_d_skip, l0_ssd_norm,
                    l0_gate_w2, l0_gate_b, l0_gla_norm, l0_w_out,
                    l1_w_qkv, l1_q_norm, l1_k_norm, l1_sinks, l1_w_out,
                    router_w, router_b, exp_w_gate, exp_b_gate, exp_w_up, exp_b_up, exp_w_down, exp_b_down)
```

```python
import functools
import math

import numpy as np
import jax
import jax.numpy as jnp
from jax import lax
from jax.experimental import pallas as pl
from jax.experimental.pallas import tpu as pltpu

F32 = jnp.float32
BF16 = jnp.bfloat16
I32 = jnp.int32
HI = lax.Precision.HIGHEST

D = 1024
EPS = 1e-6
N_ADA = 6
SSD_HEADS = 16
SSD_HEAD_DIM = 64
SSD_INNER = 1024
SSD_STATE = 128
SSD_GROUPS = 2
SSD_CONV = 5
SSD_CHUNK = 128
GLA_HEADS = 4
GLA_KEY_DIM = 128
GLA_VAL_DIM = 256
GLA_RANK = 16
GLA_TAU = 16.0
GLA_BLOCK = 64
ATT_HEADS = 16
ATT_KV = 4
ATT_HD = 64
ATT_BLOCK = 128
WINDOW = 128
GRID_W = 64
ROPE_THETA = 10000.0
N_EXPERTS = 32
TOP_K = 4
EXPERT_FF = 1024
SWIGLU_LIMIT = 7.0
SWIGLU_ALPHA = 1.702
MOE_BLOCK = 256
TOK_TILE = 256
LANES = 128
NEG = -1e30

PJ_Z, PJ_V, PJ_OG, PJ_XBC, PJ_Q, PJ_K = 0, 1024, 2048, 3072, 4608, 5120
PJ_W = 5632
VMEM_LIMIT = 48 * 1024 * 1024


def _cp(sem, vmem=VMEM_LIMIT):
    return pltpu.CompilerParams(dimension_semantics=sem, vmem_limit_bytes=vmem)


class Layout:
    def __init__(self, n_prompt, prompt_len, n_sample, sample_len):
        self.n_prompt, self.prompt_len = n_prompt, prompt_len
        self.n_sample, self.sample_len = n_sample, sample_len
        self.p_tok = n_prompt * prompt_len
        self.n_tok = self.p_tok + n_sample * sample_len
        self.seqs = [(i * prompt_len, prompt_len) for i in range(n_prompt)]
        self.seqs += [(self.p_tok + i * sample_len, sample_len) for i in range(n_sample)]
        self.n_seq = len(self.seqs)

    def mod_row(self, start):
        return jnp.where(start < self.p_tok, 0, 1 + (start - self.p_tok) // self.sample_len)

    def scan_tables(self, blk):
        rows = []
        for sid, (st, ln) in enumerate(self.seqs):
            nc = ln // blk
            b0 = st // blk
            for ph in (0, 1):
                order = range(nc) if ph == 0 else range(nc - 1, -1, -1)
                for n, c in enumerate(order):
                    yb = b0 + nc - 1 if ph == 0 else b0 + c
                    rows.append((b0 + c, sid, ph, c, int(n == 0), int(n == nc - 1), yb))
        return [jnp.asarray(np.array(col, np.int32)) for col in zip(*rows)]

    def tile_flags(self, tile):
        first, last = [], []
        for st, ln in self.seqs:
            n = ln // tile
            first += [1] + [0] * (n - 1)
            last += [0] * (n - 1) + [1]
        return jnp.asarray(np.array(first, np.int32)), jnp.asarray(np.array(last, np.int32))


def _sigmoid(x):
    return 1.0 / (1.0 + jnp.exp(-x))


def _silu(x):
    return x * _sigmoid(x)


def _softplus(x):
    return jnp.maximum(x, 0.0) + jnp.log(1.0 + jnp.exp(-jnp.abs(x)))


def _modnorm(x, g, sc, sh):
    ms = jnp.mean(x * x, axis=-1, keepdims=True)
    return (x * lax.rsqrt(ms + EPS) * g) * (1.0 + sc) + sh


def _dot(a, b, **kw):
    return jnp.dot(a, b, preferred_element_type=F32, **kw)


def _dot_nt(a, b):
    return lax.dot_general(a, b, (((1,), (1,)), ((), ())), preferred_element_type=F32)


def _dot_tn(a, b):
    return lax.dot_general(a, b, (((0,), (0,)), ((), ())), preferred_element_type=F32)


def _ada_kernel(c_ref, w_ref, b_ref, o_ref):
    o_ref[0] = _dot(_silu(c_ref[...]), w_ref[0], precision=HI) + b_ref[0]


def ada_table(cond8, ada_w, ada_b):
    depth, _, n = ada_w.shape
    tn = 1536
    return pl.pallas_call(
        _ada_kernel, grid=(depth, n // tn),
        in_specs=[pl.BlockSpec((8, D), lambda l, j: (0, 0)),
                  pl.BlockSpec((1, D, tn), lambda l, j: (l, 0, j)),
                  pl.BlockSpec((1, 1, tn), lambda l, j: (l, 0, j))],
        out_specs=pl.BlockSpec((1, 8, tn), lambda l, j: (l, 0, j)),
        out_shape=jax.ShapeDtypeStruct((depth, 8, n), F32),
        compiler_params=_cp(("arbitrary", "arbitrary")), name="ada_table",
    )(cond8, ada_w, ada_b.reshape(depth, 1, n))


def _proj_kernel(x_ref, g_ref, sc_ref, sh_ref, w_ref, ws_ref, o_ref, os_ref, h_scr):
    @pl.when(pl.program_id(1) == 0)
    def _():
        h = _modnorm(x_ref[...], g_ref[...], sc_ref[...], sh_ref[...])
        h_scr[...] = h.astype(BF16)
        os_ref[...] = _dot(h, ws_ref[...], precision=HI)

    o_ref[...] = _dot(h_scr[...], w_ref[...]).astype(o_ref.dtype)


def norm_proj(lay, x, g, sc, sh, w, w_small, tm, tn, out_dtype):
    n_tok = x.shape[0]
    n = w.shape[1]
    ns = w_small.shape[1]
    mrow = lambda i, j: (lay.mod_row(i * tm), 0, 0)
    return pl.pallas_call(
        _proj_kernel, grid=(n_tok // tm, n // tn),
        in_specs=[pl.BlockSpec((tm, D), lambda i, j: (i, 0)),
                  pl.BlockSpec((1, D), lambda i, j: (0, 0)),
                  pl.BlockSpec((None, 1, D), mrow),
                  pl.BlockSpec((None, 1, D), mrow),
                  pl.BlockSpec((D, tn), lambda i, j: (0, j)),
                  pl.BlockSpec((D, ns), lambda i, j: (0, 0))],
        out_specs=[pl.BlockSpec((tm, tn), lambda i, j: (i, j)),
                   pl.BlockSpec((tm, ns), lambda i, j: (i, 0))],
        out_shape=[jax.ShapeDtypeStruct((n_tok, n), out_dtype),
                   jax.ShapeDtypeStruct((n_tok, ns), F32)],
        scratch_shapes=[pltpu.VMEM((tm, D), BF16)],
        compiler_params=_cp(("arbitrary", "arbitrary")), name="norm_proj",
    )(x, g.reshape(1, D), sc, sh, w, w_small)


CONV_TILE = 256
CONV_HALO = 16


def _conv_kernel(first_ref, last_ref, prev_ref, cur_ref, next_ref, w_ref, b_ref, o_ref, ext):
    i = pl.program_id(0)
    keep_p = first_ref[i] == 0
    keep_n = last_ref[i] == 0
    h, t = CONV_HALO, CONV_TILE
    ext[0:h, :] = jnp.where(keep_p, prev_ref[...].astype(F32), 0.0)
    ext[h:h + t, :] = cur_ref[...].astype(F32)
    ext[h + t:h + t + h, :] = jnp.where(keep_n, next_ref[...].astype(F32), 0.0)
    pad = SSD_CONV // 2
    acc = jnp.broadcast_to(b_ref[...], (t, b_ref.shape[1]))
    for k in range(SSD_CONV):
        acc = acc + w_ref[k:k + 1, :] * ext[h - pad + k:h - pad + k + t, :]
    o_ref[...] = _silu(acc).astype(o_ref.dtype)


def conv_silu(lay, proj, conv_w, conv_b):
    n_tok = proj.shape[0]
    cw = 512
    nch = conv_w.shape[1] // cw
    c0 = PJ_XBC // cw
    t, h = CONV_TILE, CONV_HALO
    first, last = lay.tile_flags(t)
    r = t // h
    nhb = n_tok // h
    grid_spec = pltpu.PrefetchScalarGridSpec(
        num_scalar_prefetch=2, grid=(n_tok // t, nch),
        in_specs=[pl.BlockSpec((h, cw), lambda i, c, f, l: (jnp.maximum(i * r - 1, 0), c0 + c)),
                  pl.BlockSpec((t, cw), lambda i, c, f, l: (i, c0 + c)),
                  pl.BlockSpec((h, cw), lambda i, c, f, l: (jnp.minimum((i + 1) * r, nhb - 1), c0 + c)),
                  pl.BlockSpec((SSD_CONV, cw), lambda i, c, f, l: (0, c)),
                  pl.BlockSpec((1, cw), lambda i, c, f, l: (0, c))],
        out_specs=pl.BlockSpec((t, cw), lambda i, c, f, l: (i, c)),
        scratch_shapes=[pltpu.VMEM((t + 2 * h, cw), F32)])
    return pl.pallas_call(
        _conv_kernel, grid_spec=grid_spec,
        out_shape=jax.ShapeDtypeStruct((n_tok, conv_w.shape[1]), BF16),
        compiler_params=_cp(("arbitrary", "arbitrary")), name="conv_silu",
    )(first, last, proj, proj, proj, conv_w, conv_b.reshape(1, -1))


def _ssd_dir(d, cpos, first, last, xs_ref, bc_ref, z_ref, dtg_ref, dtgT_ref, s0_ref,
             alog_ref, alogT_ref, dtb_ref, dtbT_ref, dskip_ref, nrm_ref,
             y_ref, sout_ref, S, yf, yt):
    q = SSD_CHUNK
    nh = SSD_HEADS

    @pl.when(first == 1)
    def _():
        S[...] = s0_ref[0].T

    xs = xs_ref[...].astype(F32)
    dt = _softplus(dtg_ref[:, nh * d:nh * d + nh] + dtb_ref[d:d + 1, :])
    dtT = _softplus(dtgT_ref[nh * d:nh * d + nh, :] + dtbT_ref[:, d:d + 1])
    ad = dt * (-jnp.exp(alog_ref[d:d + 1, :]))
    adT = dtT * (-jnp.exp(alogT_ref[:, d:d + 1]))
    row = lax.broadcasted_iota(I32, (q, q), 0)
    col = lax.broadcasted_iota(I32, (q, q), 1)
    if d == 0:
        e = _dot((col <= row).astype(F32), ad, precision=HI)
        eT = _dot(adT, (row <= col).astype(F32), precision=HI)
        tot = e[q - 1:q, :]
        mask = row >= col
        fq = jnp.exp(e)
        fk = jnp.exp(tot - e)
    else:
        e = _dot((col < row).astype(F32), ad, precision=HI)
        eT = _dot(adT, (row < col).astype(F32), precision=HI)
        tot = jnp.sum(ad, axis=0, keepdims=True)
        mask = col >= row
        fq = jnp.exp(tot - e)
        fk = jnp.exp(e)
    dec = jnp.exp(tot)
    lo = lax.broadcasted_iota(I32, (q, LANES), 1) < SSD_HEAD_DIM
    lo1 = lax.broadcasted_iota(I32, (1, LANES), 1) < SSD_HEAD_DIM

    def colpat(arr, a):
        return jnp.where(lo, arr[:, a:a + 1], arr[:, a + 1:a + 2])

    rep = (nh // SSD_GROUPS) // 2
    for g in range(SSD_GROUPS):
        bg = bc_ref[:, SSD_STATE * g:SSD_STATE * (g + 1)]
        cg = bc_ref[:, SSD_STATE * (SSD_GROUPS + g):SSD_STATE * (SSD_GROUPS + g + 1)]
        gmat = _dot_nt(cg, bg)
        for j in range(rep * g, rep * (g + 1)):
            a = 2 * j
            sl = slice(LANES * j, LANES * (j + 1))
            parts = []
            for hh in (a, a + 1):
                if d == 0:
                    diff = e[:, hh:hh + 1] - eT[hh:hh + 1, :]
                else:
                    diff = eT[hh:hh + 1, :] - e[:, hh:hh + 1]
                parts.append((gmat * jnp.exp(jnp.where(mask, diff, NEG))).astype(BF16))
            lhs = jnp.concatenate(parts, axis=1)
            xdt = xs[:, sl] * colpat(dt, a)
            rhs = jnp.concatenate([jnp.where(lo, xdt, 0.0), jnp.where(lo, 0.0, xdt)], axis=0)
            y = _dot(lhs, rhs.astype(BF16))
            sj = S[:, sl]
            y = y + _dot(cg, sj.astype(BF16)) * colpat(fq, a)
            xk = (xdt * colpat(fk, a)).astype(BF16)
            decp = jnp.where(lo1, dec[:, a:a + 1], dec[:, a + 1:a + 2])
            S[:, sl] = sj * decp + _dot_tn(bg, xk)
            if d == 0:
                yf[pl.ds(pl.multiple_of(cpos * q, q), q), sl] = y
            else:
                yt[:, sl] = y

    if d == 1:
        ytot = yf[pl.ds(pl.multiple_of(cpos * q, q), q), :] + yt[...] + dskip_ref[...] * xs
        yg = ytot * _silu(z_ref[...].astype(F32))
        ms = jnp.mean(yg * yg, axis=-1, keepdims=True)
        y_ref[...] = (yg * lax.rsqrt(ms + EPS) * nrm_ref[...]).astype(y_ref.dtype)

    @pl.when(last == 1)
    def _():
        sout_ref[0] = S[...].T


def _ssd_kernel(blk_ref, sid_ref, ph_ref, cpos_ref, first_ref, last_ref, yblk_ref,
                xs_ref, bc_ref, z_ref, dtg_ref, dtgT_ref, s0f_ref, s0b_ref,
                alog_ref, alogT_ref, dtb_ref, dtbT_ref, dskip_ref, nrm_ref,
                y_ref, sf_ref, sb_ref, S, yf, yt):
    s = pl.program_id(0)
    common = (xs_ref, bc_ref, z_ref, dtg_ref, dtgT_ref)
    params = (alog_ref, alogT_ref, dtb_ref, dtbT_ref, dskip_ref, nrm_ref)

    @pl.when(ph_ref[s] == 0)
    def _():
        _ssd_dir(0, cpos_ref[s], first_ref[s], last_ref[s], *common, s0f_ref, *params,
                 y_ref, sf_ref, S, yf, yt)

    @pl.when(ph_ref[s] == 1)
    def _():
        _ssd_dir(1, cpos_ref[s], first_ref[s], last_ref[s], *common, s0b_ref, *params,
                 y_ref, sb_ref, S, yf, yt)


def ssd_mixer(lay, proj, xbc, small, smallT, s0f, s0b, a_log, dt_bias, d_skip, ssd_norm):
    n_tok = proj.shape[0]
    q = SSD_CHUNK
    tabs = lay.scan_tables(q)
    nsteps = int(tabs[0].shape[0])
    max_len = max(ln for _, ln in lay.seqs)
    hp = SSD_HEADS * SSD_HEAD_DIM
    blk = lambda w, cb: pl.BlockSpec((q, w), lambda s, b, *_: (b[s], cb))
    seq3 = pl.BlockSpec((1, hp, SSD_STATE), lambda s, b, sid, *_: (sid[s], 0, 0))
    full = lambda shp: pl.BlockSpec(shp, lambda s, *_: (0,) * len(shp))
    grid_spec = pltpu.PrefetchScalarGridSpec(
        num_scalar_prefetch=7, grid=(nsteps,),
        in_specs=[blk(SSD_INNER, 0), blk(512, SSD_INNER // 512), blk(SSD_INNER, PJ_Z // 1024),
                  blk(LANES, 0),
                  pl.BlockSpec((2 * SSD_HEADS, q), lambda s, b, *_: (0, b[s])),
                  seq3, seq3,
                  full((2, SSD_HEADS)), full((SSD_HEADS, 2)), full((2, SSD_HEADS)),
                  full((SSD_HEADS, 2)), full((1, hp)), full((1, hp))],
        out_specs=[pl.BlockSpec((q, hp), lambda s, b, sid, ph, cp, f, l, yb: (yb[s], 0)), seq3, seq3],
        scratch_shapes=[pltpu.VMEM((SSD_STATE, hp), F32), pltpu.VMEM((max_len, hp), F32),
                        pltpu.VMEM((q, hp), F32)])
    return pl.pallas_call(
        _ssd_kernel, grid_spec=grid_spec,
        out_shape=[jax.ShapeDtypeStruct((n_tok, hp), BF16),
                   jax.ShapeDtypeStruct((lay.n_seq, hp, SSD_STATE), F32),
                   jax.ShapeDtypeStruct((lay.n_seq, hp, SSD_STATE), F32)],
        compiler_params=_cp(("arbitrary",)), name="ssd_mixer",
    )(*tabs, xbc, xbc, proj, small, smallT, s0f, s0b,
      a_log, a_log.T, dt_bias, dt_bias.T,
      jnp.repeat(d_skip, SSD_HEAD_DIM).reshape(1, hp), ssd_norm.reshape(1, hp))


def _gla_dir(d, cpos, first, last, q_ref, k_ref, v_ref, og_ref, glr_ref, s0_ref,
             w2_ref, gb_ref, nrm_ref, o_ref, sout_ref, S, of):
    t = GLA_BLOCK
    dk, dv = GLA_KEY_DIM, GLA_VAL_DIM

    @pl.when(first == 1)
    def _():
        for h in range(GLA_HEADS):
            S[h] = s0_ref[0, h].T

    c0 = 2 * SSD_HEADS + GLA_RANK * d
    gp = _dot(glr_ref[:, c0:c0 + GLA_RANK], w2_ref[d], precision=HI) + gb_ref[d:d + 1, :]
    la = -_softplus(-gp) * (1.0 / GLA_TAU)
    row = lax.broadcasted_iota(I32, (t, t), 0)
    col = lax.broadcasted_iota(I32, (t, t), 1)
    mid = t // 2 - 1
    if d == 0:
        e = _dot((col <= row).astype(F32), la, precision=HI)
        tot = e[t - 1:t, :]
        r = e[mid:mid + 1, :]
        fqi, fki = jnp.exp(e - r), jnp.exp(r - e)
        fq, fk = jnp.exp(e), jnp.exp(tot - e)
        mask = row >= col
    else:
        e = _dot((col < row).astype(F32), la, precision=HI)
        tot = e[t - 1:t, :] + la[t - 1:t, :]
        r = e[mid:mid + 1, :]
        fqi, fki = jnp.exp(r - e), jnp.exp(e - r)
        fq, fk = jnp.exp(tot - e), jnp.exp(e)
        mask = col >= row
    dec = jnp.exp(tot)
    qf = q_ref[...].astype(F32) * (dk ** -0.5)
    kf = k_ref[...].astype(F32)
    rows = pl.ds(pl.multiple_of(cpos * t, t), t)
    for h in range(GLA_HEADS):
        sl = slice(dk * h, dk * (h + 1))
        vl = slice(dv * h, dv * (h + 1))
        qh, kh = qf[:, sl], kf[:, sl]
        sc = _dot_nt((qh * fqi[:, sl]).astype(BF16), (kh * fki[:, sl]).astype(BF16))
        sc = jnp.where(mask, sc, 0.0)
        vh = v_ref[:, vl]
        st = S[h]
        o = _dot(sc.astype(BF16), vh) + _dot_nt((qh * fq[:, sl]).astype(BF16), st.astype(BF16))
        S[h] = st * dec[:, sl] + _dot_tn(vh, (kh * fk[:, sl]).astype(BF16))
        if d == 0:
            of[rows, vl] = o
        else:
            ot = of[rows, vl] + o
            ms = jnp.mean(ot * ot, axis=-1, keepdims=True)
            on = ot * lax.rsqrt(ms + EPS) * nrm_ref[...]
            o_ref[:, vl] = (on * _silu(og_ref[:, vl].astype(F32))).astype(o_ref.dtype)

    @pl.when(last == 1)
    def _():
        for h in range(GLA_HEADS):
            sout_ref[0, h] = S[h].T


def _gla_kernel(blk_ref, sid_ref, ph_ref, cpos_ref, first_ref, last_ref, yblk_ref,
                q_ref, k_ref, v_ref, og_ref, glr_ref, s0f_ref, s0b_ref, w2_ref, gb_ref, nrm_ref,
                o_ref, sf_ref, sb_ref, S, of):
    s = pl.program_id(0)

    @pl.when(ph_ref[s] == 0)
    def _():
        _gla_dir(0, cpos_ref[s], first_ref[s], last_ref[s], q_ref, k_ref, v_ref, og_ref, glr_ref,
                 s0f_ref, w2_ref, gb_ref, nrm_ref, o_ref, sf_ref, S, of)

    @pl.when(ph_ref[s] == 1)
    def _():
        _gla_dir(1, cpos_ref[s], first_ref[s], last_ref[s], q_ref, k_ref, v_ref, og_ref, glr_ref,
                 s0b_ref, w2_ref, gb_ref, nrm_ref, o_ref, sb_ref, S, of)


def gla_mixer(lay, proj, small, s0f, s0b, gate_w2, gate_b, gla_norm):
    n_tok = proj.shape[0]
    t = GLA_BLOCK
    tabs = lay.scan_tables(t)
    nsteps = int(tabs[0].shape[0])
    max_len = max(ln for _, ln in lay.seqs)
    qk_w = GLA_HEADS * GLA_KEY_DIM
    v_w = GLA_HEADS * GLA_VAL_DIM
    blk = lambda w, cb: pl.BlockSpec((t, w), lambda s, b, *_: (b[s], cb))
    seq4 = pl.BlockSpec((1, GLA_HEADS, GLA_KEY_DIM, GLA_VAL_DIM), lambda s, b, sid, *_: (sid[s], 0, 0, 0))
    full = lambda shp: pl.BlockSpec(shp, lambda s, *_: (0,) * len(shp))
    grid_spec = pltpu.PrefetchScalarGridSpec(
        num_scalar_prefetch=7, grid=(nsteps,),
        in_specs=[blk(qk_w, PJ_Q // qk_w), blk(qk_w, PJ_K // qk_w), blk(v_w, PJ_V // v_w),
                  blk(v_w, PJ_OG // v_w), blk(LANES, 0), seq4, seq4,
                  full((2, GLA_RANK, qk_w)), full((2, qk_w)), full((1, GLA_VAL_DIM))],
        out_specs=[pl.BlockSpec((t, v_w), lambda s, b, sid, ph, cp, f, l, yb: (yb[s], 0)), seq4, seq4],
        scratch_shapes=[pltpu.VMEM((GLA_HEADS, GLA_VAL_DIM, GLA_KEY_DIM), F32),
                        pltpu.VMEM((max_len, v_w), F32)])
    st_shape = jax.ShapeDtypeStruct((lay.n_seq, GLA_HEADS, GLA_KEY_DIM, GLA_VAL_DIM), F32)
    return pl.pallas_call(
        _gla_kernel, grid_spec=grid_spec,
        out_shape=[jax.ShapeDtypeStruct((n_tok, v_w), BF16), st_shape, st_shape],
        compiler_params=_cp(("arbitrary",)), name="gla_mixer",
    )(*tabs, proj, proj, proj, proj, small, s0f, s0b, gate_w2, gate_b, gla_norm.reshape(1, -1))


def _ssd_chunk(d, c, xc, dtg_ref, dtgT_ref, alog_ref, alogT_ref, dtb_ref, dtbT_ref, S, yout):
    q = SSD_CHUNK
    nh = SSD_HEADS
    rows = pl.ds(pl.multiple_of(c * q, q), q)
    xs = xc[rows, 0:SSD_INNER].astype(F32)
    bc = xc[rows, SSD_INNER:SSD_INNER + 2 * SSD_GROUPS * SSD_STATE]
    dt = _softplus(dtg_ref[rows, nh * d:nh * d + nh] + dtb_ref[d:d + 1, :])
    dtT = _softplus(dtgT_ref[c, nh * d:nh * d + nh, :] + dtbT_ref[:, d:d + 1])
    ad = dt * (-jnp.exp(alog_ref[d:d + 1, :]))
    adT = dtT * (-jnp.exp(alogT_ref[:, d:d + 1]))
    row = lax.broadcasted_iota(I32, (q, q), 0)
    col = lax.broadcasted_iota(I32, (q, q), 1)
    if d == 0:
        e = _dot((col <= row).astype(F32), ad, precision=HI)
        eT = _dot(adT, (row <= col).astype(F32), precision=HI)
        tot = e[q - 1:q, :]
        mask = row >= col
        fq = jnp.exp(e)
        fk = jnp.exp(tot - e)
    else:
        e = _dot((col < row).astype(F32), ad, precision=HI)
        eT = _dot(adT, (row < col).astype(F32), precision=HI)
        tot = jnp.sum(ad, axis=0, keepdims=True)
        mask = col >= row
        fq = jnp.exp(tot - e)
        fk = jnp.exp(e)
    dec = jnp.exp(tot)
    lo = lax.broadcasted_iota(I32, (q, LANES), 1) < SSD_HEAD_DIM
    lo1 = lax.broadcasted_iota(I32, (1, LANES), 1) < SSD_HEAD_DIM

    def colpat(arr, a):
        return jnp.where(lo, arr[:, a:a + 1], arr[:, a + 1:a + 2])

    rep = (nh // SSD_GROUPS) // 2
    for g in range(SSD_GROUPS):
        bg = bc[:, SSD_STATE * g:SSD_STATE * (g + 1)]
        cg = bc[:, SSD_STATE * (SSD_GROUPS + g):SSD_STATE * (SSD_GROUPS + g + 1)]
        gmat = _dot_nt(cg, bg)
        for j in range(rep * g, rep * (g + 1)):
            a = 2 * j
            sl = slice(LANES * j, LANES * (j + 1))
            parts = []
            for hh in (a, a + 1):
                if d == 0:
                    diff = e[:, hh:hh + 1] - eT[hh:hh + 1, :]
                else:
                    diff = eT[hh:hh + 1, :] - e[:, hh:hh + 1]
                parts.append((gmat * jnp.exp(jnp.where(mask, diff, NEG))).astype(BF16))
            lhs = jnp.concatenate(parts, axis=1)
            xdt = xs[:, sl] * colpat(dt, a)
            rhs = jnp.concatenate([jnp.where(lo, xdt, 0.0), jnp.where(lo, 0.0, xdt)], axis=0)
            y = _dot(lhs, rhs.astype(BF16))
            sj = S[:, sl]
            y = y + _dot(cg, sj.astype(BF16)) * colpat(fq, a)
            xk = (xdt * colpat(fk, a)).astype(BF16)
            decp = jnp.where(lo1, dec[:, a:a + 1], dec[:, a + 1:a + 2])
            S[:, sl] = sj * decp + _dot_tn(bg, xk)
            yout[rows, sl] = y


def _ssd_seq_kernel(xbc_ref, z_ref, dtg_ref, dtgT_ref, s0f_ref, s0b_ref, cw_ref, cb_ref,
                    alog_ref, alogT_ref, dtb_ref, dtbT_ref, dskip_ref, nrm_ref,
                    y_ref, sf_ref, sb_ref, xc, ext, Sf, Sb, yf, yb):
    ln = xbc_ref.shape[0]
    q = SSD_CHUNK
    nc = ln // q
    h = CONV_HALO
    pad = SSD_CONV // 2

    def conv_body(c, carry):
        r0 = pl.multiple_of(c * q, q)
        prev = xbc_ref[pl.ds(pl.multiple_of(jnp.maximum(r0 - h, 0), h), h), :].astype(F32)
        nxt = xbc_ref[pl.ds(pl.multiple_of(jnp.minimum(r0 + q, ln - h), h), h), :].astype(F32)
        ext[0:h, :] = jnp.where(c > 0, prev, 0.0)
        ext[h:h + q, :] = xbc_ref[pl.ds(r0, q), :].astype(F32)
        ext[h + q:h + q + h, :] = jnp.where(c < nc - 1, nxt, 0.0)
        acc = jnp.broadcast_to(cb_ref[...], (q, cb_ref.shape[1]))
        for k in range(SSD_CONV):
            acc = acc + cw_ref[k:k + 1, :] * ext[h - pad + k:h - pad + k + q, :]
        xc[pl.ds(r0, q), :] = _silu(acc).astype(xc.dtype)
        return carry
    lax.fori_loop(0, nc, conv_body, 0)

    Sf[...] = s0f_ref[0].T
    Sb[...] = s0b_ref[0].T
    params = (dtg_ref, dtgT_ref, alog_ref, alogT_ref, dtb_ref, dtbT_ref)

    def scan_body(c, carry):
        _ssd_chunk(0, c, xc, *params, Sf, yf)
        _ssd_chunk(1, nc - 1 - c, xc, *params, Sb, yb)
        return carry
    lax.fori_loop(0, nc, scan_body, 0)
    sf_ref[0] = Sf[...].T
    sb_ref[0] = Sb[...].T

    def out_body(c, carry):
        rows = pl.ds(pl.multiple_of(c * q, q), q)
        ytot = yf[rows, :] + yb[rows, :] + dskip_ref[...] * xc[rows, 0:SSD_INNER].astype(F32)
        yg = ytot * _silu(z_ref[rows, :].astype(F32))
        ms = jnp.mean(yg * yg, axis=-1, keepdims=True)
        y_ref[rows, :] = (yg * lax.rsqrt(ms + EPS) * nrm_ref[...]).astype(y_ref.dtype)
        return carry
    lax.fori_loop(0, nc, out_body, 0)


def _ssd_call(n_seq, ln, blk0, proj, small, smallT3, s0f, s0b, params):
    q = SSD_CHUNK
    hp = SSD_HEADS * SSD_HEAD_DIM
    cw = SSD_INNER + 2 * SSD_GROUPS * SSD_STATE
    nc = ln // q
    assert PJ_XBC % cw == 0 and PJ_Z % SSD_INNER == 0
    tok = lambda w, cb: pl.BlockSpec((ln, w), lambda b: (blk0 + b, cb))
    seq3 = pl.BlockSpec((1, hp, SSD_STATE), lambda b: (b, 0, 0))
    full = lambda a: pl.BlockSpec(a.shape, lambda b: (0,) * a.ndim)
    return pl.pallas_call(
        _ssd_seq_kernel, grid=(n_seq,),
        in_specs=[tok(cw, PJ_XBC // cw), tok(SSD_INNER, PJ_Z // SSD_INNER), tok(LANES, 0),
                  pl.BlockSpec((nc, 2 * SSD_HEADS, q), lambda b: (blk0 + b, 0, 0)), seq3, seq3]
        + [full(a) for a in params],
        out_specs=[pl.BlockSpec((ln, hp), lambda b: (b, 0)), seq3, seq3],
        out_shape=[jax.ShapeDtypeStruct((n_seq * ln, hp), BF16),
                   jax.ShapeDtypeStruct((n_seq, hp, SSD_STATE), F32),
                   jax.ShapeDtypeStruct((n_seq, hp, SSD_STATE), F32)],
        scratch_shapes=[pltpu.VMEM((ln, cw), BF16), pltpu.VMEM((q + 2 * CONV_HALO, cw), F32),
                        pltpu.VMEM((SSD_STATE, hp), F32), pltpu.VMEM((SSD_STATE, hp), F32),
                        pltpu.VMEM((ln, hp), F32), pltpu.VMEM((ln, hp), F32)],
        compiler_params=_cp(("arbitrary",)), name="ssd_seq",
    )(proj, proj, small, smallT3, s0f, s0b, *params)


def _gla_block(d, c, q_ref, k_ref, v_ref, glr_ref, w2_ref, gb_ref, S, oout):
    t = GLA_BLOCK
    dk, dv = GLA_KEY_DIM, GLA_VAL_DIM
    rows = pl.ds(pl.multiple_of(c * t, t), t)
    c0 = 2 * SSD_HEADS + GLA_RANK * d
    gp = _dot(glr_ref[rows, c0:c0 + GLA_RANK], w2_ref[d], precision=HI) + gb_ref[d:d + 1, :]
    la = -_softplus(-gp) * (1.0 / GLA_TAU)
    row = lax.broadcasted_iota(I32, (t, t), 0)
    col = lax.broadcasted_iota(I32, (t, t), 1)
    mid = t // 2 - 1
    if d == 0:
        e = _dot((col <= row).astype(F32), la, precision=HI)
        tot = e[t - 1:t, :]
        r = e[mid:mid + 1, :]
        fqi, fki = jnp.exp(e - r), jnp.exp(r - e)
        fq, fk = jnp.exp(e), jnp.exp(tot - e)
        mask = row >= col
    else:
        e = _dot((col < row).astype(F32), la, precision=HI)
        tot = e[t - 1:t, :] + la[t - 1:t, :]
        r = e[mid:mid + 1, :]
        fqi, fki = jnp.exp(r - e), jnp.exp(e - r)
        fq, fk = jnp.exp(tot - e), jnp.exp(e)
        mask = col >= row
    dec = jnp.exp(tot)
    qf = q_ref[rows, :].astype(F32) * (dk ** -0.5)
    kf = k_ref[rows, :].astype(F32)
    for h in range(GLA_HEADS):
        sl = slice(dk * h, dk * (h + 1))
        vl = slice(dv * h, dv * (h + 1))
        qh, kh = qf[:, sl], kf[:, sl]
        sc = _dot_nt((qh * fqi[:, sl]).astype(BF16), (kh * fki[:, sl]).astype(BF16))
        sc = jnp.where(mask, sc, 0.0)
        vh = v_ref[rows, vl]
        st = S[h]
        oout[rows, vl] = (_dot(sc.astype(BF16), vh)
                          + _dot_nt((qh * fq[:, sl]).astype(BF16), st.astype(BF16)))
        S[h] = st * dec[:, sl] + _dot_tn(vh, (kh * fk[:, sl]).astype(BF16))


def _gla_seq_kernel(q_ref, k_ref, v_ref, og_ref, glr_ref, s0f_ref, s0b_ref, w2_ref, gb_ref, nrm_ref,
                    o_ref, sf_ref, sb_ref, Sf, Sb, of, ob):
    ln = q_ref.shape[0]
    t = GLA_BLOCK
    nc = ln // t
    dv = GLA_VAL_DIM
    for h in range(GLA_HEADS):
        Sf[h] = s0f_ref[0, h].T
        Sb[h] = s0b_ref[0, h].T
    ins = (q_ref, k_ref, v_ref, glr_ref, w2_ref, gb_ref)

    def scan_body(c, carry):
        _gla_block(0, c, *ins, Sf, of)
        _gla_block(1, nc - 1 - c, *ins, Sb, ob)
        return carry
    lax.fori_loop(0, nc, scan_body, 0)
    for h in range(GLA_HEADS):
        sf_ref[0, h] = Sf[h].T
        sb_ref[0, h] = Sb[h].T

    def out_body(c, carry):
        rows = pl.ds(pl.multiple_of(c * t, t), t)
        for h in range(GLA_HEADS):
            vl = slice(dv * h, dv * (h + 1))
            ot = of[rows, vl] + ob[rows, vl]
            ms = jnp.mean(ot * ot, axis=-1, keepdims=True)
            on = ot * lax.rsqrt(ms + EPS) * nrm_ref[...]
            o_ref[rows, vl] = (on * _silu(og_ref[rows, vl].astype(F32))).astype(o_ref.dtype)
        return carry
    lax.fori_loop(0, nc, out_body, 0)


def _gla_call(n_seq, ln, blk0, proj, small, s0f, s0b, params):
    qk_w = GLA_HEADS * GLA_KEY_DIM
    v_w = GLA_HEADS * GLA_VAL_DIM
    tok = lambda w, cb: pl.BlockSpec((ln, w), lambda b: (blk0 + b, cb))
    seq4 = pl.BlockSpec((1, GLA_HEADS, GLA_KEY_DIM, GLA_VAL_DIM), lambda b: (b, 0, 0, 0))
    full = lambda a: pl.BlockSpec(a.shape, lambda b: (0,) * a.ndim)
    st_shape = jax.ShapeDtypeStruct((n_seq, GLA_HEADS, GLA_KEY_DIM, GLA_VAL_DIM), F32)
    return pl.pallas_call(
        _gla_seq_kernel, grid=(n_seq,),
        in_specs=[tok(qk_w, PJ_Q // qk_w), tok(qk_w, PJ_K // qk_w), tok(v_w, PJ_V // v_w),
                  tok(v_w, PJ_OG // v_w), tok(LANES, 0), seq4, seq4] + [full(a) for a in params],
        out_specs=[pl.BlockSpec((ln, v_w), lambda b: (b, 0)), seq4, seq4],
        out_shape=[jax.ShapeDtypeStruct((n_seq * ln, v_w), BF16), st_shape, st_shape],
        scratch_shapes=[pltpu.VMEM((GLA_HEADS, GLA_VAL_DIM, GLA_KEY_DIM), F32),
                        pltpu.VMEM((GLA_HEADS, GLA_VAL_DIM, GLA_KEY_DIM), F32),
                        pltpu.VMEM((ln, v_w), F32), pltpu.VMEM((ln, v_w), F32)],
        compiler_params=_cp(("arbitrary",)), name="gla_seq",
    )(proj, proj, proj, proj, small, s0f, s0b, *params)


def l0_mixers(lay, proj, small, ssd_f0, ssd_b0, gla_f0, gla_b0, conv_w, conv_b, a_log, dt_bias, d_skip,
              ssd_norm, gate_w2, gate_b, gla_norm):
    q = SSD_CHUNK
    hp = SSD_HEADS * SSD_HEAD_DIM
    n_tok = proj.shape[0]
    smallT3 = small.reshape(n_tok // q, q, LANES).swapaxes(1, 2)
    ssd_p = (conv_w, conv_b.reshape(1, -1), a_log, a_log.T, dt_bias, dt_bias.T,
             jnp.repeat(d_skip, SSD_HEAD_DIM).reshape(1, hp), ssd_norm.reshape(1, hp))
    gla_p = (gate_w2, gate_b, gla_norm.reshape(1, -1))
    np_, ns = lay.n_prompt, lay.n_sample
    assert lay.p_tok % lay.sample_len == 0
    groups = [(np_, lay.prompt_len, 0, jnp.zeros((np_, hp, SSD_STATE), F32), jnp.zeros((np_, hp, SSD_STATE), F32),
               jnp.zeros((np_,) + gla_f0.shape[1:], F32), jnp.zeros((np_,) + gla_f0.shape[1:], F32)),
              (ns, lay.sample_len, lay.p_tok // lay.sample_len, ssd_f0.reshape(ns, hp, SSD_STATE),
               ssd_b0.reshape(ns, hp, SSD_STATE), gla_f0, gla_b0)]
    ys, os_, states = [], [], None
    dtT = smallT3[:, :2 * SSD_HEADS, :]
    for n, ln, blk0, sf0, sb0, gf0, gb0 in groups:
        y, sf, sb = _ssd_call(n, ln, blk0, proj, small, dtT, sf0, sb0, ssd_p)
        o, gf, gb = _gla_call(n, ln, blk0, proj, small, gf0, gb0, gla_p)
        ys.append(y)
        os_.append(o)
        if states is None:
            states = (sf, sb, gf, gb)
    return jnp.concatenate(ys, axis=0), jnp.concatenate(os_, axis=0), states


def _res_kernel(*refs, ks):
    n = len(ks)
    a_refs, w_ref, x_ref, gate_ref, o_ref = refs[:n], refs[n], refs[n + 1], refs[n + 2], refs[n + 3]
    acc = None
    off = 0
    for a_ref, k in zip(a_refs, ks):
        part = _dot(a_ref[...], w_ref[off:off + k, :])
        acc = part if acc is None else acc + part
        off += k
    o_ref[...] = x_ref[...] + gate_ref[...] * acc


def proj_residual(lay, acts, w, x, gate, tm=512):
    n_tok = x.shape[0]
    ks = tuple(int(a.shape[1]) for a in acts)
    mrow = lambda i: (lay.mod_row(i * tm), 0, 0)
    return pl.pallas_call(
        functools.partial(_res_kernel, ks=ks), grid=(n_tok // tm,),
        in_specs=[pl.BlockSpec((tm, k), lambda i: (i, 0)) for k in ks]
        + [pl.BlockSpec(w.shape, lambda i: (0, 0)),
           pl.BlockSpec((tm, D), lambda i: (i, 0)),
           pl.BlockSpec((None, 1, D), mrow)],
        out_specs=pl.BlockSpec((tm, D), lambda i: (i, 0)),
        out_shape=jax.ShapeDtypeStruct((n_tok, D), F32),
        compiler_params=_cp(("arbitrary",)), name="proj_residual",
    )(*acts, w, x, gate)


def _router_kernel(x_ref, g_ref, sc_ref, sh_ref, rw_ref, rb_ref,
                   h_ref, idx_ref, gate_ref, pos_ref, posT_ref, cnt_ref):
    tm = x_ref.shape[0]
    h = _modnorm(x_ref[...], g_ref[...], sc_ref[...], sh_ref[...])
    h_ref[...] = h.astype(BF16)
    lg = _dot(h, rw_ref[...], precision=HI) + rb_ref[...]
    lane = lax.broadcasted_iota(I32, (tm, LANES), 1).astype(F32)
    vals, ids = [], []
    for _ in range(TOP_K):
        m = jnp.max(lg, axis=1, keepdims=True)
        i = jnp.min(jnp.where(lg == m, lane, float(LANES)), axis=1, keepdims=True)
        vals.append(m)
        ids.append(i)
        lg = jnp.where(lane == i, -jnp.inf, lg)
    ex = [jnp.exp(v - vals[0]) for v in vals]
    den = ex[0] + ex[1] + ex[2] + ex[3]
    sel = jnp.zeros((tm, LANES), F32)
    for i in ids:
        sel = sel + (lane == i).astype(F32)
    row = lax.broadcasted_iota(I32, (tm, tm), 0)
    col = lax.broadcasted_iota(I32, (tm, tm), 1)
    before = _dot((col < row).astype(BF16), sel.astype(BF16))
    n = jnp.sum(sel, axis=0, keepdims=True)
    er = lax.broadcasted_iota(I32, (LANES, LANES), 0)
    ec = lax.broadcasted_iota(I32, (LANES, LANES), 1)
    n_al = jnp.ceil(n * (1.0 / SEG_ALIGN)) * SEG_ALIGN
    offs = _dot(jnp.broadcast_to(n_al, (8, LANES)).astype(BF16), (er < ec).astype(BF16))[0:1, :]
    slot = before + offs
    idx_o = jnp.zeros((tm, LANES), F32)
    gate_o = jnp.zeros((tm, LANES), F32)
    pos_o = jnp.zeros((tm, LANES), F32)
    for k in range(TOP_K):
        p = jnp.sum(jnp.where(lane == ids[k], slot, 0.0), axis=1, keepdims=True)
        idx_o = jnp.where(lane == k, ids[k], idx_o)
        gate_o = jnp.where(lane == k, ex[k] / den, gate_o)
        pos_o = jnp.where(lane == k, p, pos_o)
    idx_ref[...] = idx_o.astype(I32)
    gate_ref[...] = gate_o
    pos_ref[...] = pos_o.astype(I32)
    posT_ref[...] = pos_o.T[0:8, :]
    cnt_ref[0] = jnp.broadcast_to(n, (8, LANES))


def moe_router(lay, x, g, sc, sh, rw, rb):
    n_tok = x.shape[0]
    tm = TOK_TILE
    nt = n_tok // tm
    mrow = lambda i: (lay.mod_row(i * tm), 0, 0)
    tile = lambda w, dt: (pl.BlockSpec((tm, w), lambda i: (i, 0)), jax.ShapeDtypeStruct((n_tok, w), dt))
    outs = [tile(D, BF16), tile(LANES, I32), tile(LANES, F32), tile(LANES, I32),
            (pl.BlockSpec((8, tm), lambda i: (0, i)), jax.ShapeDtypeStruct((8, n_tok), F32)),
            (pl.BlockSpec((1, 8, LANES), lambda i: (i, 0, 0)), jax.ShapeDtypeStruct((nt, 8, LANES), F32))]
    return pl.pallas_call(
        _router_kernel, grid=(nt,),
        in_specs=[pl.BlockSpec((tm, D), lambda i: (i, 0)),
                  pl.BlockSpec((1, D), lambda i: (0, 0)),
                  pl.BlockSpec((None, 1, D), mrow), pl.BlockSpec((None, 1, D), mrow),
                  pl.BlockSpec((D, LANES), lambda i: (0, 0)),
                  pl.BlockSpec((1, LANES), lambda i: (0, 0))],
        out_specs=[o[0] for o in outs], out_shape=[o[1] for o in outs],
        compiler_params=_cp(("arbitrary",)), name="moe_router",
    )(x, g.reshape(1, D), sc, sh, rw, rb)


SEG_ALIGN = 8
SEG_BITS = tuple(range(int(math.log2(TOK_TILE)), int(math.log2(SEG_ALIGN)) - 1, -1))
TILE_ROWS = TOK_TILE * TOP_K + N_EXPERTS * SEG_ALIGN


def _pow2_copies(n, src, dst, make_copy, op, bits):
    for b in bits:
        sz = 1 << b
        done = (n >> (b + 1)) << (b + 1)

        @pl.when((n & sz) != 0)
        def _():
            op(make_copy(pl.multiple_of(src + done, SEG_ALIGN), pl.multiple_of(dst + done, SEG_ALIGN), sz))


def _segment_copies(i, n_ref, off_ref, dst_ref, make_copy, op):
    def body(e, carry):
        k = i * N_EXPERTS + e
        _pow2_copies(n_ref[k], off_ref[k], dst_ref[k], make_copy, op, SEG_BITS)
        return carry
    lax.fori_loop(0, N_EXPERTS, body, 0)


TAIL_BITS = tuple(range(int(math.log2(MOE_BLOCK)) - 1, int(math.log2(SEG_ALIGN)) - 1, -1))


def _dispatch_kernel(n_ref, off_ref, dst_ref, tn_ref, td_ref, posT_ref, h_ref, xout_ref, srt, zbuf, sem):
    i = pl.program_id(0)
    tm = h_ref.shape[0]
    r = lax.broadcasted_iota(I32, (TILE_ROWS, tm), 0)
    hit = jnp.zeros((TILE_ROWS, tm), jnp.bool_)
    for k in range(TOP_K):
        hit = hit | (r == posT_ref[k:k + 1, :].astype(I32))
    sel = jnp.where(hit, 1.0, 0.0).astype(BF16)
    srt[...] = _dot(sel, h_ref[...])

    def make_copy(src, dst, sz):
        return pltpu.make_async_copy(srt.at[pl.ds(src, sz)], xout_ref.at[pl.ds(dst, sz)], sem)

    _segment_copies(i, n_ref, off_ref, dst_ref, make_copy, lambda c: c.start())
    _segment_copies(i, n_ref, off_ref, dst_ref, make_copy, lambda c: c.wait())

    @pl.when(i == pl.num_programs(0) - 1)
    def _():
        zbuf[...] = jnp.zeros_like(zbuf)

        def zero_copy(src, dst, sz):
            return pltpu.make_async_copy(zbuf.at[pl.ds(src, sz)], xout_ref.at[pl.ds(dst, sz)], sem)

        nb = xout_ref.shape[0] // MOE_BLOCK
        for op in (lambda c: c.start(), lambda c: c.wait()):
            def body(e, carry):
                _pow2_copies(tn_ref[e], 0, td_ref[e], zero_copy, op, TAIL_BITS)
                return carry
            lax.fori_loop(0, N_EXPERTS, body, 0)

            def unused(b, carry):
                op(zero_copy(0, pl.multiple_of(b * MOE_BLOCK, MOE_BLOCK), MOE_BLOCK))
                return carry
            lax.fori_loop(tn_ref[N_EXPERTS], nb, unused, 0)


def moe_dispatch(n_tab, off_tab, dst_tab, tail_n, tail_dst, posT, h2, n_rows):
    n_tok = h2.shape[0]
    tm = TOK_TILE
    grid_spec = pltpu.PrefetchScalarGridSpec(
        num_scalar_prefetch=5, grid=(n_tok // tm,),
        in_specs=[pl.BlockSpec((8, tm), lambda i, *_: (0, i)),
                  pl.BlockSpec((tm, D), lambda i, *_: (i, 0))],
        out_specs=pl.BlockSpec(memory_space=pl.ANY),
        scratch_shapes=[pltpu.VMEM((TILE_ROWS, D), F32), pltpu.VMEM((MOE_BLOCK, D), F32),
                        pltpu.SemaphoreType.DMA])
    return pl.pallas_call(
        _dispatch_kernel, grid_spec=grid_spec,
        out_shape=jax.ShapeDtypeStruct((n_rows, D), F32),
        compiler_params=_cp(("arbitrary",)), name="moe_dispatch",
    )(n_tab, off_tab, dst_tab, tail_n, tail_dst, posT, h2)


def _combine_kernel(n_ref, off_ref, dst_ref, pos_ref, gate_ref, x_ref, g2_ref, y_ref, o_ref, buf, sem):
    i = pl.program_id(0)
    tm = x_ref.shape[0]
    na = TILE_ROWS
    buf[tm * TOP_K:na, :] = jnp.zeros((na - tm * TOP_K, D), F32)

    def make_copy(src, dst, sz):
        return pltpu.make_async_copy(y_ref.at[pl.ds(dst, sz)], buf.at[pl.ds(src, sz)], sem)

    _segment_copies(i, n_ref, off_ref, dst_ref, make_copy, lambda c: c.start())
    _segment_copies(i, n_ref, off_ref, dst_ref, make_copy, lambda c: c.wait())
    lane = lax.broadcasted_iota(I32, (tm, na), 1)
    pw = jnp.zeros((tm, na), F32)
    for k in range(TOP_K):
        pw = pw + jnp.where(lane == pos_ref[:, k:k + 1], gate_ref[:, k:k + 1], 0.0)
    phi = pw.astype(BF16)
    plo = (pw - phi.astype(F32)).astype(BF16)
    yb = buf[...].astype(BF16)
    o_ref[...] = x_ref[...] + g2_ref[...] * (_dot(phi, yb) + _dot(plo, yb))


def moe_combine(lay, n_tab, off_tab, dst_tab, pos, gates, x, gate2, y_rows):
    n_tok = x.shape[0]
    tm = TOK_TILE
    mrow = lambda i, *_: (lay.mod_row(i * tm), 0, 0)
    grid_spec = pltpu.PrefetchScalarGridSpec(
        num_scalar_prefetch=3, grid=(n_tok // tm,),
        in_specs=[pl.BlockSpec((tm, LANES), lambda i, *_: (i, 0)),
                  pl.BlockSpec((tm, LANES), lambda i, *_: (i, 0)),
                  pl.BlockSpec((tm, D), lambda i, *_: (i, 0)),
                  pl.BlockSpec((None, 1, D), mrow),
                  pl.BlockSpec(memory_space=pl.ANY)],
        out_specs=pl.BlockSpec((tm, D), lambda i, *_: (i, 0)),
        scratch_shapes=[pltpu.VMEM((TILE_ROWS, D), F32), pltpu.SemaphoreType.DMA])
    return pl.pallas_call(
        _combine_kernel, grid_spec=grid_spec,
        out_shape=jax.ShapeDtypeStruct((n_tok, D), F32),
        compiler_params=_cp(("arbitrary",)), name="moe_combine",
    )(n_tab, off_tab, dst_tab, pos, gates, x, gate2, y_rows)


def _expert_kernel(be_ref, nv_ref, nxt_ref, x_ref, bg_ref, bu_ref, bd_ref, wg_hbm, wu_hbm, wd_hbm, y_ref,
                   wf, wg_s, wu_s, wd_s, sems, *, layer):
    i = pl.program_id(0)
    valid = i < nv_ref[0]
    e = be_ref[i]
    changed = jnp.logical_or(i == 0, e != be_ref[jnp.maximum(i - 1, 0)])

    def weight_copies(ex):
        return [pltpu.make_async_copy(w.at[layer, ex], wf.at[k], sems.at[k])
                for k, w in enumerate((wg_hbm, wu_hbm, wd_hbm))]

    @pl.when(jnp.logical_and(valid, changed))
    def _():
        @pl.when(i == 0)
        def _():
            for c in weight_copies(e):
                c.start()

        for c in weight_copies(e):
            c.wait()
        rc = 128

        def cast(c, carry):
            rows = pl.ds(pl.multiple_of(c * rc, rc), rc)
            wg_s[rows, :] = wf[0, rows, :].astype(BF16)
            wu_s[rows, :] = wf[1, rows, :].astype(BF16)
            wd_s[rows, :] = wf[2, rows, :].astype(BF16)
            return carry
        lax.fori_loop(0, D // rc, cast, 0)
        nxt = nxt_ref[e]

        @pl.when(nxt >= 0)
        def _():
            for c in weight_copies(nxt):
                c.start()

    @pl.when(valid)
    def _():
        x = x_ref[...].astype(BF16)
        gt = jnp.minimum(_dot(x, wg_s[...]) + bg_ref[...], SWIGLU_LIMIT)
        up = jnp.clip(_dot(x, wu_s[...]) + bu_ref[...], -SWIGLU_LIMIT, SWIGLU_LIMIT)
        act = (up + 1.0) * gt * _sigmoid(SWIGLU_ALPHA * gt)
        y_ref[...] = _dot(act.astype(BF16), wd_s[...]) + bd_ref[...]

    @pl.when(jnp.logical_not(valid))
    def _():
        y_ref[...] = jnp.zeros_like(y_ref)


def moe_experts(layer, blk_expert, n_valid, next_expert, x_rows, w_gate, b_gate, w_up, b_up, w_down, b_down):
    n_rows = x_rows.shape[0]
    nb = n_rows // MOE_BLOCK
    depth, ne, _, ff = w_gate.shape
    assert ff == D
    rowblk = lambda i, be, nv, nx: (jnp.maximum(jnp.minimum(i, nv[0] - 1), 0), 0)
    bsel = lambda i, be, nv, nx: (layer, be[i], 0, 0)
    hbm = pl.BlockSpec(memory_space=pl.ANY)
    grid_spec = pltpu.PrefetchScalarGridSpec(
        num_scalar_prefetch=3, grid=(nb,),
        in_specs=[pl.BlockSpec((MOE_BLOCK, D), rowblk),
                  pl.BlockSpec((None, None, 1, ff), bsel), pl.BlockSpec((None, None, 1, ff), bsel),
                  pl.BlockSpec((None, None, 1, D), bsel), hbm, hbm, hbm],
        out_specs=pl.BlockSpec((MOE_BLOCK, D), lambda i, be, nv, nx: (i, 0)),
        scratch_shapes=[pltpu.VMEM((3, D, ff), F32), pltpu.VMEM((D, ff), BF16), pltpu.VMEM((D, ff), BF16),
                        pltpu.VMEM((ff, D), BF16), pltpu.SemaphoreType.DMA((3,))])
    return pl.pallas_call(
        functools.partial(_expert_kernel, layer=layer), grid_spec=grid_spec,
        out_shape=jax.ShapeDtypeStruct((n_rows, D), F32),
        compiler_params=_cp(("arbitrary",)), name="moe_experts",
    )(blk_expert, n_valid, next_expert, x_rows, b_gate.reshape(depth, ne, 1, ff),
      b_up.reshape(depth, ne, 1, ff), b_down.reshape(depth, ne, 1, D), w_gate, w_up, w_down)


def moe_layer(lay, layer, x, g2, sc2, sh2, gate2, router_w, router_b, w_gate, b_gate, w_up, b_up, w_down,
              b_down):
    n_tok = x.shape[0]
    nt = n_tok // TOK_TILE
    rw = jnp.zeros((D, LANES), F32).at[:, :N_EXPERTS].set(router_w)
    rb = jnp.full((1, LANES), NEG, F32).at[0, :N_EXPERTS].set(router_b)
    h2, _, gates, pos, posT, cnt = moe_router(lay, x, g2, sc2, sh2, rw, rb)
    n_te = cnt[:, 0, :N_EXPERTS].astype(I32)
    n_te = (n_te + SEG_ALIGN - 1) // SEG_ALIGN * SEG_ALIGN
    totals = jnp.sum(n_te, axis=0)
    padded = (totals + MOE_BLOCK - 1) // MOE_BLOCK * MOE_BLOCK
    padded_end = jnp.cumsum(padded)
    pstart = padded_end - padded
    dst = pstart[None, :] + jnp.cumsum(n_te, axis=0) - n_te
    off = jnp.cumsum(n_te, axis=1) - n_te
    n_rows = nt * TILE_ROWS + N_EXPERTS * MOE_BLOCK
    nb = n_rows // MOE_BLOCK
    n_valid = (padded_end[-1] // MOE_BLOCK).astype(I32).reshape(1)
    bstart = jnp.minimum(jnp.arange(nb, dtype=I32), n_valid[0] - 1) * MOE_BLOCK
    blk_expert = jnp.minimum(jnp.sum((bstart[:, None] >= padded_end[None, :]).astype(I32), axis=1),
                             N_EXPERTS - 1).astype(I32)
    tabs = (n_te.reshape(-1).astype(I32), off.reshape(-1).astype(I32), dst.reshape(-1).astype(I32))
    tail_n = jnp.concatenate([(padded - totals).astype(I32), n_valid])
    x_rows = moe_dispatch(*tabs, tail_n, (pstart + totals).astype(I32), posT, h2, n_rows)
    owner = jnp.where(padded > 0, jnp.arange(N_EXPERTS, dtype=I32), N_EXPERTS)
    later = jnp.concatenate([lax.cummin(owner, axis=0, reverse=True)[1:], jnp.full((1,), N_EXPERTS, I32)])
    next_expert = jnp.where(later < N_EXPERTS, later, -1).astype(I32)
    y_rows = moe_experts(layer, blk_expert, n_valid, next_expert, x_rows, w_gate, b_gate, w_up, b_up,
                         w_down, b_down)
    return moe_combine(lay, *tabs, pos, gates, x, gate2, y_rows)


QKV_TN = 256
N_QK_TILES = (ATT_HEADS + ATT_KV) * ATT_HD // QKV_TN


def _qkv_kernel(x_ref, g_ref, sc_ref, sh_ref, w_ref, nw_ref, cos_ref, sin_ref, o_ref, *, p_tok):
    i = pl.program_id(0)
    tm = x_ref.shape[0]
    h = _modnorm(x_ref[...], g_ref[...], sc_ref[...], sh_ref[...]).astype(BF16)
    r = lax.broadcasted_iota(I32, (QKV_TN, QKV_TN), 0) // ATT_HD
    c = lax.broadcasted_iota(I32, (QKV_TN, QKV_TN), 1) // ATT_HD
    head_mean = jnp.where(r == c, 1.0 / ATT_HD, 0.0).astype(BF16)
    lane = lax.broadcasted_iota(I32, (tm, QKV_TN), 1)
    half = ATT_HD // 4
    first = (lane % (2 * half)) < half
    for j in range(w_ref.shape[1] // QKV_TN):
        cols = slice(QKV_TN * j, QKV_TN * (j + 1))
        acc = _dot(h, w_ref[:, cols])
        if j >= N_QK_TILES:
            o_ref[:, cols] = acc
            continue
        ms = _dot((acc * acc).astype(BF16), head_mean)
        qn = acc * lax.rsqrt(ms + EPS) * nw_ref[j]

        @pl.when(i * tm < p_tok)
        def _():
            o_ref[:, cols] = qn

        @pl.when(i * tm >= p_tok)
        def _():
            swapped = jnp.where(first, pltpu.roll(qn, QKV_TN - half, 1), pltpu.roll(qn, half, 1))
            o_ref[:, cols] = qn * cos_ref[...] + swapped * sin_ref[...]


def _rope_tables(sample_len):
    pos = np.arange(sample_len)
    half = ATT_HD // 4
    inv = (ROPE_THETA ** (-np.arange(half, dtype=np.float32) / half)).astype(np.float32)
    ang_r = (pos // GRID_W).astype(np.float32)[:, None] * inv[None, :]
    ang_c = (pos % GRID_W).astype(np.float32)[:, None] * inv[None, :]
    cos = np.concatenate([np.cos(ang_r)] * 2 + [np.cos(ang_c)] * 2, axis=1)
    sin = np.concatenate([-np.sin(ang_r), np.sin(ang_r), -np.sin(ang_c), np.sin(ang_c)], axis=1)
    rep = QKV_TN // ATT_HD
    return (jnp.asarray(np.tile(cos, (1, rep)), F32), jnp.asarray(np.tile(sin, (1, rep)), F32))


def qkv_proj(lay, x, g, sc, sh, w, q_norm, k_norm, tm=256):
    n_tok = x.shape[0]
    n = w.shape[1]
    nq = ATT_HEADS * ATT_HD // QKV_TN
    rep = QKV_TN // ATT_HD
    nw = jnp.concatenate([jnp.tile(jnp.tile(q_norm, rep)[None, :], (nq, 1)),
                          jnp.tile(jnp.tile(k_norm, rep)[None, :], (n // QKV_TN - nq, 1))], axis=0)
    cos, sin = _rope_tables(lay.sample_len)
    mrow = lambda i: (lay.mod_row(i * tm), 0, 0)
    rrow = lambda i: (jnp.where(i * tm < lay.p_tok, 0, ((i * tm - lay.p_tok) % lay.sample_len) // tm), 0)
    nt = n // QKV_TN
    return pl.pallas_call(
        functools.partial(_qkv_kernel, p_tok=lay.p_tok), grid=(n_tok // tm,),
        in_specs=[pl.BlockSpec((tm, D), lambda i: (i, 0)),
                  pl.BlockSpec((1, D), lambda i: (0, 0)),
                  pl.BlockSpec((None, 1, D), mrow), pl.BlockSpec((None, 1, D), mrow),
                  pl.BlockSpec((D, n), lambda i: (0, 0)),
                  pl.BlockSpec((nt, 1, QKV_TN), lambda i: (0, 0, 0)),
                  pl.BlockSpec((tm, QKV_TN), rrow), pl.BlockSpec((tm, QKV_TN), rrow)],
        out_specs=pl.BlockSpec((tm, n), lambda i: (i, 0)),
        out_shape=jax.ShapeDtypeStruct((n_tok, n), F32),
        compiler_params=_cp(("arbitrary",)), name="qkv_proj",
    )(x, g.reshape(1, D), sc, sh, w, nw.reshape(nt, 1, QKV_TN), cos, sin)


def _dup_group(x, g):
    blk = x[:, LANES * (g // 2):LANES * (g // 2 + 1)]
    if g % 2 == 1:
        blk = pltpu.roll(blk, ATT_HD, 1)
    lo = lax.broadcasted_iota(I32, blk.shape, 1) < ATT_HD
    low = jnp.where(lo, blk, 0.0)
    return low + pltpu.roll(low, ATT_HD, 1)


def _attend(q_ref, k_all, v_all, mask, sink_ref, o_ref):
    nq = q_ref.shape[0]
    lo = lax.broadcasted_iota(I32, (nq, LANES), 1) < ATT_HD
    grp = ATT_HEADS // ATT_KV
    for g in range(ATT_KV):
        k2 = _dup_group(k_all, g).astype(BF16)
        v2 = _dup_group(v_all, g).astype(BF16)
        for jp in range(grp // 2):
            j = g * (grp // 2) + jp
            qp = q_ref[:, LANES * j:LANES * (j + 1)] * (ATT_HD ** -0.5)
            outs = []
            for half in range(2):
                qh = jnp.where(lo, qp, 0.0) if half == 0 else jnp.where(lo, 0.0, qp)
                s = _dot_nt(qh.astype(BF16), k2)
                if mask is not None:
                    s = jnp.where(mask, s, NEG)
                sink = sink_ref[2 * j + half]
                m = jnp.maximum(jnp.max(s, axis=1, keepdims=True), sink)
                p = jnp.exp(s - m)
                den = jnp.sum(p, axis=1, keepdims=True) + jnp.exp(sink - m)
                outs.append(_dot(p.astype(BF16), v2) / den)
            o_ref[:, LANES * j:LANES * (j + 1)] = jnp.where(lo, outs[0], outs[1]).astype(o_ref.dtype)


def _attn_ctx_kernel(sink_ref, q_ref, k_ref, v_ref, o_ref):
    _attend(q_ref, k_ref[...], v_ref[...], None, sink_ref, o_ref)


def attn_context(lay, qkv, sinks):
    qw = ATT_HEADS * ATT_HD
    kw = ATT_KV * ATT_HD
    ln = lay.prompt_len
    grid_spec = pltpu.PrefetchScalarGridSpec(
        num_scalar_prefetch=0, grid=(lay.n_prompt,),
        in_specs=[pl.BlockSpec(memory_space=pltpu.SMEM),
                  pl.BlockSpec((ln, qw), lambda b: (b, 0)),
                  pl.BlockSpec((ln, kw), lambda b: (b, qw // kw)),
                  pl.BlockSpec((ln, kw), lambda b: (b, qw // kw + 1))],
        out_specs=pl.BlockSpec((ln, qw), lambda b: (b, 0)))
    return pl.pallas_call(
        _attn_ctx_kernel, grid_spec=grid_spec,
        out_shape=jax.ShapeDtypeStruct((lay.p_tok, qw), BF16),
        compiler_params=_cp(("arbitrary",)), name="attn_context",
    )(sinks, qkv, qkv, qkv)


def _attn_lat_kernel(sink_ref, q_ref, kp_ref, kc_ref, kn_ref, vp_ref, vc_ref, vn_ref, ck_ref, cv_ref, o_ref,
                     *, nblk):
    i = pl.program_id(1)
    bq = ATT_BLOCK
    nctx = ck_ref.shape[1]
    k_all = jnp.concatenate([kp_ref[...], kc_ref[...], kn_ref[...], ck_ref[0]], axis=0)
    v_all = jnp.concatenate([vp_ref[...], vc_ref[...], vn_ref[...], cv_ref[0]], axis=0)
    ns = 3 * bq + nctx
    r = lax.broadcasted_iota(I32, (bq, ns), 0)
    c = lax.broadcasted_iota(I32, (bq, ns), 1)
    rel = c - r
    first_key = jnp.where(i > 0, 0, bq)
    end_key = jnp.where(i < nblk - 1, 3 * bq, 2 * bq)
    band = (rel >= bq - WINDOW) & (rel <= bq + WINDOW) & (c >= first_key) & (c < end_key)
    mask = band | (c >= 3 * bq)
    _attend(q_ref, k_all, v_all, mask, sink_ref, o_ref)


def attn_latent(lay, qkv, cache_k, cache_v, sinks):
    qw = ATT_HEADS * ATT_HD
    kw = ATT_KV * ATT_HD
    bq = ATT_BLOCK
    nblk = lay.sample_len // bq
    b0 = lay.p_tok // bq
    nctx = cache_k.shape[1]
    rb = lambda b, i: b0 + b * nblk + i
    kspec = lambda cb, sh: pl.BlockSpec(
        (bq, kw), lambda b, i: (b0 + b * nblk + jnp.clip(i + sh, 0, nblk - 1), cb))
    kc, vc = qw // kw, qw // kw + 1
    grid_spec = pltpu.PrefetchScalarGridSpec(
        num_scalar_prefetch=0, grid=(lay.n_sample, nblk),
        in_specs=[pl.BlockSpec(memory_space=pltpu.SMEM),
                  pl.BlockSpec((bq, qw), lambda b, i: (rb(b, i), 0)),
                  kspec(kc, -1), kspec(kc, 0), kspec(kc, 1),
                  kspec(vc, -1), kspec(vc, 0), kspec(vc, 1),
                  pl.BlockSpec((1, nctx, kw), lambda b, i: (b, 0, 0)),
                  pl.BlockSpec((1, nctx, kw), lambda b, i: (b, 0, 0))],
        out_specs=pl.BlockSpec((bq, qw), lambda b, i: (b * nblk + i, 0)))
    return pl.pallas_call(
        functools.partial(_attn_lat_kernel, nblk=nblk), grid_spec=grid_spec,
        out_shape=jax.ShapeDtypeStruct((lay.n_sample * lay.sample_len, qw), BF16),
        compiler_params=_cp(("arbitrary", "arbitrary")), name="attn_latent",
    )(sinks, qkv, qkv, qkv, qkv, qkv, qkv, qkv,
      cache_k.reshape(lay.n_sample, nctx, kw), cache_v.reshape(lay.n_sample, nctx, kw))


def _forward(lay, x_prompt, x_sample, state_l0_ssd_fwd, state_l0_ssd_bwd, state_l0_gla_fwd, state_l0_gla_bwd,
             cache_l1_k, cache_l1_v, c, c_ctx, ada_w, ada_b, norm1, norm2,
             l0_w_in, l0_conv_w, l0_conv_b, l0_a_log, l0_dt_bias, l0_d_skip, l0_ssd_norm,
             l0_gate_w2, l0_gate_b, l0_gla_norm, l0_w_out,
             l1_w_qkv, l1_q_norm, l1_k_norm, l1_sinks, l1_w_out,
             router_w, router_b, exp_w_gate, exp_b_gate, exp_w_up, exp_b_up, exp_w_down, exp_b_down):
    np_, ns = lay.n_prompt, lay.n_sample
    x = jnp.concatenate([x_prompt.reshape(-1, D), x_sample.reshape(-1, D)], axis=0)
    cond8 = jnp.zeros((8, D), F32).at[0].set(c_ctx).at[1:1 + ns].set(c)
    mod = ada_table(cond8, ada_w, ada_b)
    mods = [[mod[l, :, p * D:(p + 1) * D].reshape(8, 1, D) for p in range(N_ADA)] for l in range(2)]

    def moe(l, xx):
        return moe_layer(lay, l, xx, norm2[l], mods[l][4], mods[l][3], mods[l][5], router_w[l], router_b[l],
                         exp_w_gate, exp_b_gate, exp_w_up, exp_b_up, exp_w_down, exp_b_down)

    sp = np.cumsum((SSD_INNER, SSD_INNER + 2 * SSD_GROUPS * SSD_STATE, 2 * SSD_HEADS,
                    GLA_HEADS * GLA_KEY_DIM, GLA_HEADS * GLA_KEY_DIM,
                    GLA_HEADS * GLA_VAL_DIM, GLA_HEADS * GLA_VAL_DIM, 2 * GLA_RANK))
    cols = lambda a, b: l0_w_in[:, a:b]
    w_main = jnp.concatenate([cols(0, sp[0]), cols(sp[4], sp[5]), cols(sp[5], sp[6]), cols(sp[0], sp[1]),
                              cols(sp[2], sp[3]), cols(sp[3], sp[4])], axis=1).astype(BF16)
    w_small = jnp.concatenate([cols(sp[1], sp[2]), cols(sp[6], sp[7]),
                               jnp.zeros((D, LANES - 2 * SSD_HEADS - 2 * GLA_RANK), F32)], axis=1)
    proj, small = norm_proj(lay, x, norm1[0], mods[0][1], mods[0][0], w_main, w_small, 512, PJ_W // 2, BF16)
    y_n, o_n, (ssd_f, ssd_b, gla_f, gla_b) = l0_mixers(
        lay, proj, small, state_l0_ssd_fwd, state_l0_ssd_bwd, state_l0_gla_fwd, state_l0_gla_bwd,
        l0_conv_w, l0_conv_b, l0_a_log, l0_dt_bias, l0_d_skip, l0_ssd_norm,
        l0_gate_w2, l0_gate_b, l0_gla_norm)
    x = proj_residual(lay, [y_n, o_n], l0_w_out.astype(BF16), x, mods[0][2])
    x = moe(0, x)

    qkv = qkv_proj(lay, x, norm1[1], mods[1][1], mods[1][0], l1_w_qkv.astype(BF16), l1_q_norm, l1_k_norm)
    o_ctx = attn_context(lay, qkv, l1_sinks)
    o_lat = attn_latent(lay, qkv, cache_l1_k, cache_l1_v, l1_sinks)
    o = jnp.concatenate([o_ctx, o_lat], axis=0)
    x = proj_residual(lay, [o], l1_w_out.astype(BF16), x, mods[1][2])
    x = moe(1, x)

    qw = ATT_HEADS * ATT_HD
    kw = ATT_KV * ATT_HD
    return (x[:lay.p_tok].reshape(x_prompt.shape), x[lay.p_tok:].reshape(x_sample.shape),
            ssd_f[:np_].reshape(np_, SSD_HEADS, SSD_HEAD_DIM, SSD_STATE),
            ssd_b[:np_].reshape(np_, SSD_HEADS, SSD_HEAD_DIM, SSD_STATE),
            gla_f[:np_], gla_b[:np_],
            qkv[:lay.p_tok, qw:qw + kw].reshape(np_, lay.prompt_len, ATT_KV, ATT_HD),
            qkv[:lay.p_tok, qw + kw:].reshape(np_, lay.prompt_len, ATT_KV, ATT_HD))


def kernel(x_prompt, x_sample, state_l0_ssd_fwd, state_l0_ssd_bwd, state_l0_gla_fwd, state_l0_gla_bwd, cache_l1_k, cache_l1_v, c, c_ctx, ada_w, ada_b, norm1, norm2, l0_w_in, l0_conv_w, l0_conv_b, l0_a_log, l0_dt_bias, l0_d_skip, l0_ssd_norm, l0_gate_w2, l0_gate_b, l0_gla_norm, l0_w_out, l1_w_qkv, l1_q_norm, l1_k_norm, l1_sinks, l1_w_out, router_w, router_b, exp_w_gate, exp_b_gate, exp_w_up, exp_b_up, exp_w_down, exp_b_down):
    lay = Layout(x_prompt.shape[0], x_prompt.shape[1], x_sample.shape[0], x_sample.shape[1])
    return _forward(lay, x_prompt, x_sample, state_l0_ssd_fwd, state_l0_ssd_bwd, state_l0_gla_fwd,
                    state_l0_gla_bwd, cache_l1_k, cache_l1_v, c, c_ctx, ada_w, ada_b, norm1, norm2,
                    l0_w_in, l0_conv_w, l0_conv_b, l0_a_log, l0_dt_bias, l0_d_skip, l0_ssd_norm,
                    l0_gate_w2, l0_gate_b, l0_gla_norm, l0_w_out,
                    l1_w_qkv, l1_q_norm, l1_k_norm, l1_sinks, l1_w_out,
                    router_w, router_b, exp_w_gate, exp_b_gate, exp_w_up, exp_b_up, exp_w_down, exp_b_down)
```

```python
import functools
import math

import numpy as np
import jax
import jax.numpy as jnp
from jax import lax
from jax.experimental import pallas as pl
from jax.experimental.pallas import tpu as pltpu

F32 = jnp.float32
BF16 = jnp.bfloat16
I32 = jnp.int32
HI = lax.Precision.HIGHEST

D = 1024
EPS = 1e-6
N_ADA = 6
SSD_HEADS = 16
SSD_HEAD_DIM = 64
SSD_INNER = 1024
SSD_STATE = 128
SSD_GROUPS = 2
SSD_CONV = 5
SSD_CHUNK = 128
GLA_HEADS = 4
GLA_KEY_DIM = 128
GLA_VAL_DIM = 256
GLA_RANK = 16
GLA_TAU = 16.0
GLA_BLOCK = 64
ATT_HEADS = 16
ATT_KV = 4
ATT_HD = 64
ATT_BLOCK = 128
WINDOW = 128
GRID_W = 64
ROPE_THETA = 10000.0
N_EXPERTS = 32
TOP_K = 4
EXPERT_FF = 1024
SWIGLU_LIMIT = 7.0
SWIGLU_ALPHA = 1.702
MOE_BLOCK = 256
TOK_TILE = 256
LANES = 128
NEG = -1e30

PJ_Z, PJ_V, PJ_OG, PJ_XBC, PJ_Q, PJ_K = 0, 1024, 2048, 3072, 4608, 5120
PJ_W = 5632
VMEM_LIMIT = 48 * 1024 * 1024


def _cp(sem, vmem=VMEM_LIMIT):
    return pltpu.CompilerParams(dimension_semantics=sem, vmem_limit_bytes=vmem)


class Layout:
    def __init__(self, n_prompt, prompt_len, n_sample, sample_len):
        self.n_prompt, self.prompt_len = n_prompt, prompt_len
        self.n_sample, self.sample_len = n_sample, sample_len
        self.p_tok = n_prompt * prompt_len
        self.n_tok = self.p_tok + n_sample * sample_len
        self.seqs = [(i * prompt_len, prompt_len) for i in range(n_prompt)]
        self.seqs += [(self.p_tok + i * sample_len, sample_len) for i in range(n_sample)]
        self.n_seq = len(self.seqs)

    def mod_row(self, start):
        return jnp.where(start < self.p_tok, 0, 1 + (start - self.p_tok) // self.sample_len)

def _sigmoid(x):
    return 1.0 / (1.0 + jnp.exp(-x))


def _silu(x):
    return x * _sigmoid(x)


def _softplus(x):
    return jnp.maximum(x, 0.0) + jnp.log(1.0 + jnp.exp(-jnp.abs(x)))


def _modnorm(x, g, sc, sh):
    ms = jnp.mean(x * x, axis=-1, keepdims=True)
    return (x * lax.rsqrt(ms + EPS) * g) * (1.0 + sc) + sh


def _dot(a, b, **kw):
    return jnp.dot(a, b, preferred_element_type=F32, **kw)


def _dot_nt(a, b):
    return lax.dot_general(a, b, (((1,), (1,)), ((), ())), preferred_element_type=F32)


def _dot_tn(a, b):
    return lax.dot_general(a, b, (((0,), (0,)), ((), ())), preferred_element_type=F32)


def _split(x, n):
    parts = []
    for _ in range(n):
        p = x.astype(BF16)
        parts.append(p)
        x = x - p.astype(F32)
    return parts


def _dot_sel(sel, x):
    sel = sel.astype(BF16)
    return sum(_dot(sel, p) for p in _split(x, 3))


def _dot_sel_r(x, sel):
    sel = sel.astype(BF16)
    return sum(_dot(p, sel) for p in _split(x, 3))


def _dot_hilo(x, w_hi, w_lo):
    x_hi, x_lo = _split(x, 2)
    return _dot(x_hi, w_hi) + _dot(x_lo, w_hi) + _dot(x_hi, w_lo)


def _hilo(w):
    hi = w.astype(BF16)
    return jnp.stack([hi, (w - hi.astype(F32)).astype(BF16)])


def _ada_kernel(c_ref, w_ref, b_ref, o_ref):
    o_ref[0] = _dot(_silu(c_ref[...]), w_ref[0], precision=HI) + b_ref[0]


def ada_table(cond8, ada_w, ada_b):
    depth, _, n = ada_w.shape
    tn = 1536
    return pl.pallas_call(
        _ada_kernel, grid=(depth, n // tn),
        in_specs=[pl.BlockSpec((8, D), lambda l, j: (0, 0)),
                  pl.BlockSpec((1, D, tn), lambda l, j: (l, 0, j)),
                  pl.BlockSpec((1, 1, tn), lambda l, j: (l, 0, j))],
        out_specs=pl.BlockSpec((1, 8, tn), lambda l, j: (l, 0, j)),
        out_shape=jax.ShapeDtypeStruct((depth, 8, n), F32),
        compiler_params=_cp(("arbitrary", "arbitrary")), name="ada_table",
    )(cond8, ada_w, ada_b.reshape(depth, 1, n))


def _proj_kernel(x_ref, g_ref, sc_ref, sh_ref, w_ref, ws_ref, o_ref, os_ref, h_scr):
    @pl.when(pl.program_id(1) == 0)
    def _():
        h = _modnorm(x_ref[...], g_ref[...], sc_ref[...], sh_ref[...])
        h_scr[...] = h.astype(BF16)
        os_ref[...] = _dot_hilo(h, ws_ref[0], ws_ref[1])

    o_ref[...] = _dot(h_scr[...], w_ref[...]).astype(o_ref.dtype)


def norm_proj(lay, x, g, sc, sh, w, w_small, tm, tn, out_dtype):
    n_tok = x.shape[0]
    n = w.shape[1]
    ns = w_small.shape[-1]
    mrow = lambda i, j: (lay.mod_row(i * tm), 0, 0)
    return pl.pallas_call(
        _proj_kernel, grid=(n_tok // tm, n // tn),
        in_specs=[pl.BlockSpec((tm, D), lambda i, j: (i, 0)),
                  pl.BlockSpec((1, D), lambda i, j: (0, 0)),
                  pl.BlockSpec((None, 1, D), mrow),
                  pl.BlockSpec((None, 1, D), mrow),
                  pl.BlockSpec((D, tn), lambda i, j: (0, j)),
                  pl.BlockSpec((2, D, ns), lambda i, j: (0, 0, 0))],
        out_specs=[pl.BlockSpec((tm, tn), lambda i, j: (i, j)),
                   pl.BlockSpec((tm, ns), lambda i, j: (i, 0))],
        out_shape=[jax.ShapeDtypeStruct((n_tok, n), out_dtype),
                   jax.ShapeDtypeStruct((n_tok, ns), F32)],
        scratch_shapes=[pltpu.VMEM((tm, D), BF16)],
        compiler_params=_cp(("arbitrary", "arbitrary")), name="norm_proj",
    )(x, g.reshape(1, D), sc, sh, w, w_small)


CONV_HALO = 16


def _ssd_chunk(d, c, xc, dtg_ref, dtgT_ref, alog_ref, alogT_ref, dtb_ref, dtbT_ref, S, yout):
    q = SSD_CHUNK
    nh = SSD_HEADS
    rows = pl.ds(pl.multiple_of(c * q, q), q)
    xs = xc[rows, 0:SSD_INNER].astype(F32)
    bc = xc[rows, SSD_INNER:SSD_INNER + 2 * SSD_GROUPS * SSD_STATE]
    dt = _softplus(dtg_ref[rows, nh * d:nh * d + nh] + dtb_ref[d:d + 1, :])
    dtT = _softplus(dtgT_ref[c, nh * d:nh * d + nh, :] + dtbT_ref[:, d:d + 1])
    ad = dt * (-jnp.exp(alog_ref[d:d + 1, :]))
    adT = dtT * (-jnp.exp(alogT_ref[:, d:d + 1]))
    row = lax.broadcasted_iota(I32, (q, q), 0)
    col = lax.broadcasted_iota(I32, (q, q), 1)
    if d == 0:
        e = _dot_sel(col <= row, ad)
        eT = _dot_sel_r(adT, row <= col)
        tot = e[q - 1:q, :]
        mask = row >= col
        fq = jnp.exp(e)
        fk = jnp.exp(tot - e)
    else:
        e = _dot_sel(col < row, ad)
        eT = _dot_sel_r(adT, row < col)
        tot = jnp.sum(ad, axis=0, keepdims=True)
        mask = col >= row
        fq = jnp.exp(tot - e)
        fk = jnp.exp(e)
    dec = jnp.exp(tot)
    lo = lax.broadcasted_iota(I32, (q, LANES), 1) < SSD_HEAD_DIM
    lo1 = lax.broadcasted_iota(I32, (1, LANES), 1) < SSD_HEAD_DIM

    def colpat(arr, a):
        return jnp.where(lo, arr[:, a:a + 1], arr[:, a + 1:a + 2])

    rep = (nh // SSD_GROUPS) // 2
    for g in range(SSD_GROUPS):
        bg = bc[:, SSD_STATE * g:SSD_STATE * (g + 1)]
        cg = bc[:, SSD_STATE * (SSD_GROUPS + g):SSD_STATE * (SSD_GROUPS + g + 1)]
        gmat = _dot_nt(cg, bg)
        for j in range(rep * g, rep * (g + 1)):
            a = 2 * j
            sl = slice(LANES * j, LANES * (j + 1))
            parts = []
            for hh in (a, a + 1):
                if d == 0:
                    diff = e[:, hh:hh + 1] - eT[hh:hh + 1, :]
                else:
                    diff = eT[hh:hh + 1, :] - e[:, hh:hh + 1]
                parts.append((gmat * jnp.exp(jnp.where(mask, diff, NEG))).astype(BF16))
            lhs = jnp.concatenate(parts, axis=1)
            xdt = xs[:, sl] * colpat(dt, a)
            rhs = jnp.concatenate([jnp.where(lo, xdt, 0.0), jnp.where(lo, 0.0, xdt)], axis=0)
            y = _dot(lhs, rhs.astype(BF16))
            sj = S[:, sl]
            y = y + _dot(cg, sj.astype(BF16)) * colpat(fq, a)
            xk = (xdt * colpat(fk, a)).astype(BF16)
            decp = jnp.where(lo1, dec[:, a:a + 1], dec[:, a + 1:a + 2])
            S[:, sl] = sj * decp + _dot_tn(bg, xk)
            yout[rows, sl] = y


def _ssd_seq_kernel(xbc_ref, z_ref, dtg_ref, dtgT_ref, s0f_ref, s0b_ref, cw_ref, cb_ref,
                    alog_ref, alogT_ref, dtb_ref, dtbT_ref, dskip_ref, nrm_ref,
                    y_ref, sf_ref, sb_ref, xc, ext, Sf, Sb, yf, yb):
    ln = xbc_ref.shape[0]
    q = SSD_CHUNK
    nc = ln // q
    h = CONV_HALO
    pad = SSD_CONV // 2

    def conv_body(c, carry):
        r0 = pl.multiple_of(c * q, q)
        prev = xbc_ref[pl.ds(pl.multiple_of(jnp.maximum(r0 - h, 0), h), h), :].astype(F32)
        nxt = xbc_ref[pl.ds(pl.multiple_of(jnp.minimum(r0 + q, ln - h), h), h), :].astype(F32)
        ext[0:h, :] = jnp.where(c > 0, prev, 0.0)
        ext[h:h + q, :] = xbc_ref[pl.ds(r0, q), :].astype(F32)
        ext[h + q:h + q + h, :] = jnp.where(c < nc - 1, nxt, 0.0)
        acc = jnp.broadcast_to(cb_ref[...], (q, cb_ref.shape[1]))
        for k in range(SSD_CONV):
            acc = acc + cw_ref[k:k + 1, :] * ext[h - pad + k:h - pad + k + q, :]
        xc[pl.ds(r0, q), :] = _silu(acc).astype(xc.dtype)
        return carry
    lax.fori_loop(0, nc, conv_body, 0)

    Sf[...] = s0f_ref[0].T
    Sb[...] = s0b_ref[0].T
    params = (dtg_ref, dtgT_ref, alog_ref, alogT_ref, dtb_ref, dtbT_ref)

    def scan_body(c, carry):
        _ssd_chunk(0, c, xc, *params, Sf, yf)
        _ssd_chunk(1, nc - 1 - c, xc, *params, Sb, yb)
        return carry
    lax.fori_loop(0, nc, scan_body, 0)
    sf_ref[0] = Sf[...].T
    sb_ref[0] = Sb[...].T

    def out_body(c, carry):
        rows = pl.ds(pl.multiple_of(c * q, q), q)
        ytot = yf[rows, :] + yb[rows, :] + dskip_ref[...] * xc[rows, 0:SSD_INNER].astype(F32)
        yg = ytot * _silu(z_ref[rows, :].astype(F32))
        ms = jnp.mean(yg * yg, axis=-1, keepdims=True)
        y_ref[rows, :] = (yg * lax.rsqrt(ms + EPS) * nrm_ref[...]).astype(y_ref.dtype)
        return carry
    lax.fori_loop(0, nc, out_body, 0)


def _ssd_call(n_seq, ln, blk0, proj, small, smallT3, s0f, s0b, params):
    q = SSD_CHUNK
    hp = SSD_HEADS * SSD_HEAD_DIM
    cw = SSD_INNER + 2 * SSD_GROUPS * SSD_STATE
    nc = ln // q
    assert PJ_XBC % cw == 0 and PJ_Z % SSD_INNER == 0
    tok = lambda w, cb: pl.BlockSpec((ln, w), lambda b: (blk0 + b, cb))
    seq3 = pl.BlockSpec((1, hp, SSD_STATE), lambda b: (b, 0, 0))
    full = lambda a: pl.BlockSpec(a.shape, lambda b: (0,) * a.ndim)
    return pl.pallas_call(
        _ssd_seq_kernel, grid=(n_seq,),
        in_specs=[tok(cw, PJ_XBC // cw), tok(SSD_INNER, PJ_Z // SSD_INNER), tok(LANES, 0),
                  pl.BlockSpec((nc, 2 * SSD_HEADS, q), lambda b: (blk0 + b, 0, 0)), seq3, seq3]
        + [full(a) for a in params],
        out_specs=[pl.BlockSpec((ln, hp), lambda b: (b, 0)), seq3, seq3],
        out_shape=[jax.ShapeDtypeStruct((n_seq * ln, hp), BF16),
                   jax.ShapeDtypeStruct((n_seq, hp, SSD_STATE), F32),
                   jax.ShapeDtypeStruct((n_seq, hp, SSD_STATE), F32)],
        scratch_shapes=[pltpu.VMEM((ln, cw), BF16), pltpu.VMEM((q + 2 * CONV_HALO, cw), F32),
                        pltpu.VMEM((SSD_STATE, hp), F32), pltpu.VMEM((SSD_STATE, hp), F32),
                        pltpu.VMEM((ln, hp), F32), pltpu.VMEM((ln, hp), F32)],
        compiler_params=_cp(("arbitrary",)), name="ssd_seq",
    )(proj, proj, small, smallT3, s0f, s0b, *params)


def _gla_block(d, c, q_ref, k_ref, v_ref, glr_ref, w2_ref, gb_ref, S, oout):
    t = GLA_BLOCK
    dk, dv = GLA_KEY_DIM, GLA_VAL_DIM
    rows = pl.ds(pl.multiple_of(c * t, t), t)
    c0 = 2 * SSD_HEADS + GLA_RANK * d
    gp = _dot_hilo(glr_ref[rows, c0:c0 + GLA_RANK], w2_ref[0, d], w2_ref[1, d]) + gb_ref[d:d + 1, :]
    la = -_softplus(-gp) * (1.0 / GLA_TAU)
    row = lax.broadcasted_iota(I32, (t, t), 0)
    col = lax.broadcasted_iota(I32, (t, t), 1)
    mid = t // 2 - 1
    if d == 0:
        e = _dot_sel(col <= row, la)
        tot = e[t - 1:t, :]
        r = e[mid:mid + 1, :]
        fqi, fki = jnp.exp(e - r), jnp.exp(r - e)
        fq, fk = jnp.exp(e), jnp.exp(tot - e)
        mask = row >= col
    else:
        e = _dot_sel(col < row, la)
        tot = e[t - 1:t, :] + la[t - 1:t, :]
        r = e[mid:mid + 1, :]
        fqi, fki = jnp.exp(r - e), jnp.exp(e - r)
        fq, fk = jnp.exp(tot - e), jnp.exp(e)
        mask = col >= row
    dec = jnp.exp(tot)
    qf = q_ref[rows, :].astype(F32) * (dk ** -0.5)
    kf = k_ref[rows, :].astype(F32)
    for h in range(GLA_HEADS):
        sl = slice(dk * h, dk * (h + 1))
        vl = slice(dv * h, dv * (h + 1))
        qh, kh = qf[:, sl], kf[:, sl]
        sc = _dot_nt((qh * fqi[:, sl]).astype(BF16), (kh * fki[:, sl]).astype(BF16))
        sc = jnp.where(mask, sc, 0.0)
        vh = v_ref[rows, vl]
        st = S[h]
        oout[rows, vl] = (_dot(sc.astype(BF16), vh)
                          + _dot_nt((qh * fq[:, sl]).astype(BF16), st.astype(BF16)))
        S[h] = st * dec[:, sl] + _dot_tn(vh, (kh * fk[:, sl]).astype(BF16))


def _gla_seq_kernel(q_ref, k_ref, v_ref, og_ref, glr_ref, s0f_ref, s0b_ref, w2_ref, gb_ref, nrm_ref,
                    o_ref, sf_ref, sb_ref, Sf, Sb, of, ob):
    ln = q_ref.shape[0]
    t = GLA_BLOCK
    nc = ln // t
    dv = GLA_VAL_DIM
    for h in range(GLA_HEADS):
        Sf[h] = s0f_ref[0, h].T
        Sb[h] = s0b_ref[0, h].T
    ins = (q_ref, k_ref, v_ref, glr_ref, w2_ref, gb_ref)

    def scan_body(c, carry):
        _gla_block(0, c, *ins, Sf, of)
        _gla_block(1, nc - 1 - c, *ins, Sb, ob)
        return carry
    lax.fori_loop(0, nc, scan_body, 0)
    for h in range(GLA_HEADS):
        sf_ref[0, h] = Sf[h].T
        sb_ref[0, h] = Sb[h].T

    def out_body(c, carry):
        rows = pl.ds(pl.multiple_of(c * t, t), t)
        for h in range(GLA_HEADS):
            vl = slice(dv * h, dv * (h + 1))
            ot = of[rows, vl] + ob[rows, vl]
            ms = jnp.mean(ot * ot, axis=-1, keepdims=True)
            on = ot * lax.rsqrt(ms + EPS) * nrm_ref[...]
            o_ref[rows, vl] = (on * _silu(og_ref[rows, vl].astype(F32))).astype(o_ref.dtype)
        return carry
    lax.fori_loop(0, nc, out_body, 0)


def _gla_call(n_seq, ln, blk0, proj, small, s0f, s0b, params):
    qk_w = GLA_HEADS * GLA_KEY_DIM
    v_w = GLA_HEADS * GLA_VAL_DIM
    tok = lambda w, cb: pl.BlockSpec((ln, w), lambda b: (blk0 + b, cb))
    seq4 = pl.BlockSpec((1, GLA_HEADS, GLA_KEY_DIM, GLA_VAL_DIM), lambda b: (b, 0, 0, 0))
    full = lambda a: pl.BlockSpec(a.shape, lambda b: (0,) * a.ndim)
    st_shape = jax.ShapeDtypeStruct((n_seq, GLA_HEADS, GLA_KEY_DIM, GLA_VAL_DIM), F32)
    return pl.pallas_call(
        _gla_seq_kernel, grid=(n_seq,),
        in_specs=[tok(qk_w, PJ_Q // qk_w), tok(qk_w, PJ_K // qk_w), tok(v_w, PJ_V // v_w),
                  tok(v_w, PJ_OG // v_w), tok(LANES, 0), seq4, seq4] + [full(a) for a in params],
        out_specs=[pl.BlockSpec((ln, v_w), lambda b: (b, 0)), seq4, seq4],
        out_shape=[jax.ShapeDtypeStruct((n_seq * ln, v_w), BF16), st_shape, st_shape],
        scratch_shapes=[pltpu.VMEM((GLA_HEADS, GLA_VAL_DIM, GLA_KEY_DIM), F32),
                        pltpu.VMEM((GLA_HEADS, GLA_VAL_DIM, GLA_KEY_DIM), F32),
                        pltpu.VMEM((ln, v_w), F32), pltpu.VMEM((ln, v_w), F32)],
        compiler_params=_cp(("arbitrary",)), name="gla_seq",
    )(proj, proj, proj, proj, small, s0f, s0b, *params)


def l0_mixers(lay, proj, small, ssd_f0, ssd_b0, gla_f0, gla_b0, conv_w, conv_b, a_log, dt_bias, d_skip,
              ssd_norm, gate_w2, gate_b, gla_norm):
    q = SSD_CHUNK
    hp = SSD_HEADS * SSD_HEAD_DIM
    n_tok = proj.shape[0]
    smallT3 = small.reshape(n_tok // q, q, LANES).swapaxes(1, 2)
    ssd_p = (conv_w, conv_b.reshape(1, -1), a_log, a_log.T, dt_bias, dt_bias.T,
             jnp.repeat(d_skip, SSD_HEAD_DIM).reshape(1, hp), ssd_norm.reshape(1, hp))
    gla_p = (_hilo(gate_w2), gate_b, gla_norm.reshape(1, -1))
    np_, ns = lay.n_prompt, lay.n_sample
    assert lay.p_tok % lay.sample_len == 0
    groups = [(np_, lay.prompt_len, 0, jnp.zeros((np_, hp, SSD_STATE), F32), jnp.zeros((np_, hp, SSD_STATE), F32),
               jnp.zeros((np_,) + gla_f0.shape[1:], F32), jnp.zeros((np_,) + gla_f0.shape[1:], F32)),
              (ns, lay.sample_len, lay.p_tok // lay.sample_len, ssd_f0.reshape(ns, hp, SSD_STATE),
               ssd_b0.reshape(ns, hp, SSD_STATE), gla_f0, gla_b0)]
    ys, os_, states = [], [], None
    dtT = smallT3[:, :2 * SSD_HEADS, :]
    for n, ln, blk0, sf0, sb0, gf0, gb0 in groups:
        y, sf, sb = _ssd_call(n, ln, blk0, proj, small, dtT, sf0, sb0, ssd_p)
        o, gf, gb = _gla_call(n, ln, blk0, proj, small, gf0, gb0, gla_p)
        ys.append(y)
        os_.append(o)
        if states is None:
            states = (sf, sb, gf, gb)
    return jnp.concatenate(ys, axis=0), jnp.concatenate(os_, axis=0), states


def _res_kernel(*refs, ks):
    n = len(ks)
    a_refs, w_ref, x_ref, gate_ref, o_ref = refs[:n], refs[n], refs[n + 1], refs[n + 2], refs[n + 3]
    acc = None
    off = 0
    for a_ref, k in zip(a_refs, ks):
        part = _dot(a_ref[...], w_ref[off:off + k, :])
        acc = part if acc is None else acc + part
        off += k
    o_ref[...] = x_ref[...] + gate_ref[...] * acc


def proj_residual(lay, acts, w, x, gate, tm=512):
    n_tok = x.shape[0]
    ks = tuple(int(a.shape[1]) for a in acts)
    mrow = lambda i: (lay.mod_row(i * tm), 0, 0)
    return pl.pallas_call(
        functools.partial(_res_kernel, ks=ks), grid=(n_tok // tm,),
        in_specs=[pl.BlockSpec((tm, k), lambda i: (i, 0)) for k in ks]
        + [pl.BlockSpec(w.shape, lambda i: (0, 0)),
           pl.BlockSpec((tm, D), lambda i: (i, 0)),
           pl.BlockSpec((None, 1, D), mrow)],
        out_specs=pl.BlockSpec((tm, D), lambda i: (i, 0)),
        out_shape=jax.ShapeDtypeStruct((n_tok, D), F32),
        compiler_params=_cp(("arbitrary",)), name="proj_residual",
    )(*acts, w, x, gate)


def _router_kernel(x_ref, g_ref, sc_ref, sh_ref, rw_ref, rb_ref,
                   h_ref, idx_ref, gate_ref, pos_ref, posT_ref, cnt_ref):
    tm = x_ref.shape[0]
    h = _modnorm(x_ref[...], g_ref[...], sc_ref[...], sh_ref[...])
    h_hi = h.astype(BF16)
    h_ref[...] = h_hi
    h_lo = (h - h_hi.astype(F32)).astype(BF16)
    lg = (_dot(h_hi, rw_ref[0]) + _dot(h_lo, rw_ref[0]) + _dot(h_hi, rw_ref[1])
          + rb_ref[...])
    lane = lax.broadcasted_iota(I32, (tm, LANES), 1).astype(F32)
    vals, ids = [], []
    for _ in range(TOP_K):
        m = jnp.max(lg, axis=1, keepdims=True)
        i = jnp.min(jnp.where(lg == m, lane, float(LANES)), axis=1, keepdims=True)
        vals.append(m)
        ids.append(i)
        lg = jnp.where(lane == i, -jnp.inf, lg)
    ex = [jnp.exp(v - vals[0]) for v in vals]
    den = ex[0] + ex[1] + ex[2] + ex[3]
    sel = jnp.zeros((tm, LANES), F32)
    for i in ids:
        sel = sel + (lane == i).astype(F32)
    row = lax.broadcasted_iota(I32, (tm, tm), 0)
    col = lax.broadcasted_iota(I32, (tm, tm), 1)
    before = _dot((col < row).astype(BF16), sel.astype(BF16))
    n = jnp.sum(sel, axis=0, keepdims=True)
    er = lax.broadcasted_iota(I32, (LANES, LANES), 0)
    ec = lax.broadcasted_iota(I32, (LANES, LANES), 1)
    n_al = jnp.ceil(n * (1.0 / SEG_ALIGN)) * SEG_ALIGN
    offs = _dot(jnp.broadcast_to(n_al, (8, LANES)).astype(BF16), (er < ec).astype(BF16))[0:1, :]
    slot = before + offs
    idx_o = jnp.zeros((tm, LANES), F32)
    gate_o = jnp.zeros((tm, LANES), F32)
    pos_o = jnp.zeros((tm, LANES), F32)
    for k in range(TOP_K):
        p = jnp.sum(jnp.where(lane == ids[k], slot, 0.0), axis=1, keepdims=True)
        idx_o = jnp.where(lane == k, ids[k], idx_o)
        gate_o = jnp.where(lane == k, ex[k] / den, gate_o)
        pos_o = jnp.where(lane == k, p, pos_o)
    idx_ref[...] = idx_o.astype(I32)
    gate_ref[...] = gate_o
    pos_ref[...] = pos_o.astype(I32)
    posT_ref[...] = pos_o.T[0:8, :]
    cnt_ref[0] = jnp.broadcast_to(n, (8, LANES))


def moe_router(lay, x, g, sc, sh, rw, rb):
    n_tok = x.shape[0]
    tm = TOK_TILE
    nt = n_tok // tm
    mrow = lambda i: (lay.mod_row(i * tm), 0, 0)
    tile = lambda w, dt: (pl.BlockSpec((tm, w), lambda i: (i, 0)), jax.ShapeDtypeStruct((n_tok, w), dt))
    outs = [tile(D, BF16), tile(LANES, I32), tile(LANES, F32), tile(LANES, I32),
            (pl.BlockSpec((8, tm), lambda i: (0, i)), jax.ShapeDtypeStruct((8, n_tok), F32)),
            (pl.BlockSpec((1, 8, LANES), lambda i: (i, 0, 0)), jax.ShapeDtypeStruct((nt, 8, LANES), F32))]
    return pl.pallas_call(
        _router_kernel, grid=(nt,),
        in_specs=[pl.BlockSpec((tm, D), lambda i: (i, 0)),
                  pl.BlockSpec((1, D), lambda i: (0, 0)),
                  pl.BlockSpec((None, 1, D), mrow), pl.BlockSpec((None, 1, D), mrow),
                  pl.BlockSpec((2, D, LANES), lambda i: (0, 0, 0)),
                  pl.BlockSpec((1, LANES), lambda i: (0, 0))],
        out_specs=[o[0] for o in outs], out_shape=[o[1] for o in outs],
        compiler_params=_cp(("arbitrary",)), name="moe_router",
    )(x, g.reshape(1, D), sc, sh, rw, rb)


SEG_ALIGN = 8
SEG_BITS = tuple(range(int(math.log2(TOK_TILE)), int(math.log2(SEG_ALIGN)) - 1, -1))
TILE_ROWS = TOK_TILE * TOP_K + N_EXPERTS * SEG_ALIGN


def _pow2_copies(n, src, dst, make_copy, op, bits):
    for b in bits:
        sz = 1 << b
        done = (n >> (b + 1)) << (b + 1)

        @pl.when((n & sz) != 0)
        def _():
            op(make_copy(pl.multiple_of(src + done, SEG_ALIGN), pl.multiple_of(dst + done, SEG_ALIGN), sz))


def _start_segments(i, n_ref, off_ref, dst_ref, make_copy):
    def body(e, carry):
        k = i * N_EXPERTS + e
        _pow2_copies(n_ref[k], off_ref[k], dst_ref[k], make_copy, lambda c: c.start(), SEG_BITS)
        return carry
    lax.fori_loop(0, N_EXPERTS, body, 0)


TAIL_BITS = tuple(range(int(math.log2(MOE_BLOCK)) - 1, int(math.log2(SEG_ALIGN)) - 1, -1))
TILE_BITS = tuple(range(int(math.log2(TILE_ROWS)), int(math.log2(SEG_ALIGN)) - 1, -1))


def _wait_rows(total, make_copy):
    _pow2_copies(total, 0, 0, make_copy, lambda c: c.wait(), TILE_BITS)


def _dispatch_kernel(n_ref, off_ref, dst_ref, tot_ref, tn_ref, td_ref, posT_ref, h_ref, xout_ref,
                     srt, zbuf, sems):
    i = pl.program_id(0)
    last = pl.num_programs(0) - 1
    slot = i % 2
    tm = h_ref.shape[0]
    r = lax.broadcasted_iota(I32, (TILE_ROWS, tm), 0)
    hit = jnp.zeros((TILE_ROWS, tm), jnp.bool_)
    for k in range(TOP_K):
        hit = hit | (r == posT_ref[k:k + 1, :].astype(I32))
    sel = jnp.where(hit, 1.0, 0.0).astype(BF16)
    srt[slot] = _dot(sel, h_ref[...])

    def copier(s):
        def make_copy(src, dst, sz):
            return pltpu.make_async_copy(srt.at[s, pl.ds(src, sz)], xout_ref.at[pl.ds(dst, sz)], sems.at[s])
        return make_copy

    _start_segments(i, n_ref, off_ref, dst_ref, copier(slot))

    @pl.when(i > 0)
    def _():
        _wait_rows(tot_ref[jnp.maximum(i - 1, 0)], copier(1 - slot))

    @pl.when(i == last)
    def _():
        _wait_rows(tot_ref[i], copier(slot))
        zbuf[...] = jnp.zeros_like(zbuf)
        sem = sems.at[0]

        def zero_copy(src, dst, sz):
            return pltpu.make_async_copy(zbuf.at[pl.ds(src, sz)], xout_ref.at[pl.ds(dst, sz)], sem)

        nb = xout_ref.shape[0] // MOE_BLOCK
        for op in (lambda c: c.start(), lambda c: c.wait()):
            def body(e, carry):
                _pow2_copies(tn_ref[e], 0, td_ref[e], zero_copy, op, TAIL_BITS)
                return carry
            lax.fori_loop(0, N_EXPERTS, body, 0)

            def unused(b, carry):
                op(zero_copy(0, pl.multiple_of(b * MOE_BLOCK, MOE_BLOCK), MOE_BLOCK))
                return carry
            lax.fori_loop(tn_ref[N_EXPERTS], nb, unused, 0)


def moe_dispatch(n_tab, off_tab, dst_tab, tot_tab, tail_n, tail_dst, posT, h2, n_rows):
    n_tok = h2.shape[0]
    tm = TOK_TILE
    grid_spec = pltpu.PrefetchScalarGridSpec(
        num_scalar_prefetch=6, grid=(n_tok // tm,),
        in_specs=[pl.BlockSpec((8, tm), lambda i, *_: (0, i)),
                  pl.BlockSpec((tm, D), lambda i, *_: (i, 0))],
        out_specs=pl.BlockSpec(memory_space=pl.ANY),
        scratch_shapes=[pltpu.VMEM((2, TILE_ROWS, D), F32), pltpu.VMEM((MOE_BLOCK, D), F32),
                        pltpu.SemaphoreType.DMA((2,))])
    return pl.pallas_call(
        _dispatch_kernel, grid_spec=grid_spec,
        out_shape=jax.ShapeDtypeStruct((n_rows, D), F32),
        compiler_params=_cp(("arbitrary",)), name="moe_dispatch",
    )(n_tab, off_tab, dst_tab, tot_tab, tail_n, tail_dst, posT, h2)


def _combine_kernel(n_ref, off_ref, dst_ref, tot_ref, pos_ref, gate_ref, x_ref, g2_ref, y_ref, o_ref,
                    buf, sems):
    i = pl.program_id(0)
    last = pl.num_programs(0) - 1
    slot = i % 2
    tm = x_ref.shape[0]
    na = TILE_ROWS

    def copier(s):
        def make_copy(src, dst, sz):
            return pltpu.make_async_copy(y_ref.at[pl.ds(dst, sz)], buf.at[s, pl.ds(src, sz)], sems.at[s])
        return make_copy

    def fetch(tile, s):
        buf[s, tm * TOP_K:na, :] = jnp.zeros((na - tm * TOP_K, D), F32)
        _start_segments(tile, n_ref, off_ref, dst_ref, copier(s))

    @pl.when(i == 0)
    def _():
        fetch(i, slot)

    @pl.when(i < last)
    def _():
        fetch(i + 1, 1 - slot)

    _wait_rows(tot_ref[i], copier(slot))
    lane = lax.broadcasted_iota(I32, (tm, na), 1)
    pw = jnp.zeros((tm, na), F32)
    for k in range(TOP_K):
        pw = pw + jnp.where(lane == pos_ref[:, k:k + 1], gate_ref[:, k:k + 1], 0.0)
    phi = pw.astype(BF16)
    plo = (pw - phi.astype(F32)).astype(BF16)
    yb = buf[slot].astype(BF16)
    o_ref[...] = x_ref[...] + g2_ref[...] * (_dot(phi, yb) + _dot(plo, yb))


def moe_combine(lay, n_tab, off_tab, dst_tab, tot_tab, pos, gates, x, gate2, y_rows):
    n_tok = x.shape[0]
    tm = TOK_TILE
    mrow = lambda i, *_: (lay.mod_row(i * tm), 0, 0)
    grid_spec = pltpu.PrefetchScalarGridSpec(
        num_scalar_prefetch=4, grid=(n_tok // tm,),
        in_specs=[pl.BlockSpec((tm, LANES), lambda i, *_: (i, 0)),
                  pl.BlockSpec((tm, LANES), lambda i, *_: (i, 0)),
                  pl.BlockSpec((tm, D), lambda i, *_: (i, 0)),
                  pl.BlockSpec((None, 1, D), mrow),
                  pl.BlockSpec(memory_space=pl.ANY)],
        out_specs=pl.BlockSpec((tm, D), lambda i, *_: (i, 0)),
        scratch_shapes=[pltpu.VMEM((2, TILE_ROWS, D), F32), pltpu.SemaphoreType.DMA((2,))])
    return pl.pallas_call(
        _combine_kernel, grid_spec=grid_spec,
        out_shape=jax.ShapeDtypeStruct((n_tok, D), F32),
        compiler_params=_cp(("arbitrary",)), name="moe_combine",
    )(n_tab, off_tab, dst_tab, tot_tab, pos, gates, x, gate2, y_rows)


def _expert_kernel(be_ref, nv_ref, nxt_ref, x_ref, bg_ref, bu_ref, bd_ref, wg_hbm, wu_hbm, wd_hbm, y_ref,
                   wf, wg_s, wu_s, wd_s, sems, *, layer):
    i = pl.program_id(0)
    valid = i < nv_ref[0]
    e = be_ref[i]
    changed = jnp.logical_or(i == 0, e != be_ref[jnp.maximum(i - 1, 0)])

    def weight_copies(ex):
        return [pltpu.make_async_copy(w.at[layer, ex], wf.at[k], sems.at[k])
                for k, w in enumerate((wg_hbm, wu_hbm, wd_hbm))]

    @pl.when(jnp.logical_and(valid, changed))
    def _():
        @pl.when(i == 0)
        def _():
            for c in weight_copies(e):
                c.start()

        for c in weight_copies(e):
            c.wait()
        rc = 128

        def cast(c, carry):
            rows = pl.ds(pl.multiple_of(c * rc, rc), rc)
            wg_s[rows, :] = wf[0, rows, :].astype(BF16)
            wu_s[rows, :] = wf[1, rows, :].astype(BF16)
            wd_s[rows, :] = wf[2, rows, :].astype(BF16)
            return carry
        lax.fori_loop(0, D // rc, cast, 0)
        nxt = nxt_ref[e]

        @pl.when(nxt >= 0)
        def _():
            for c in weight_copies(nxt):
                c.start()

    @pl.when(valid)
    def _():
        x = x_ref[...].astype(BF16)
        gt = jnp.minimum(_dot(x, wg_s[...]) + bg_ref[...], SWIGLU_LIMIT)
        up = jnp.clip(_dot(x, wu_s[...]) + bu_ref[...], -SWIGLU_LIMIT, SWIGLU_LIMIT)
        act = (up + 1.0) * gt * _sigmoid(SWIGLU_ALPHA * gt)
        y_ref[...] = _dot(act.astype(BF16), wd_s[...]) + bd_ref[...]

    @pl.when(jnp.logical_not(valid))
    def _():
        y_ref[...] = jnp.zeros_like(y_ref)


def moe_experts(layer, blk_expert, n_valid, next_expert, x_rows, w_gate, b_gate, w_up, b_up, w_down, b_down):
    n_rows = x_rows.shape[0]
    nb = n_rows // MOE_BLOCK
    depth, ne, _, ff = w_gate.shape
    assert ff == D
    rowblk = lambda i, be, nv, nx: (jnp.maximum(jnp.minimum(i, nv[0] - 1), 0), 0)
    bsel = lambda i, be, nv, nx: (layer, be[i], 0, 0)
    hbm = pl.BlockSpec(memory_space=pl.ANY)
    grid_spec = pltpu.PrefetchScalarGridSpec(
        num_scalar_prefetch=3, grid=(nb,),
        in_specs=[pl.BlockSpec((MOE_BLOCK, D), rowblk),
                  pl.BlockSpec((None, None, 1, ff), bsel), pl.BlockSpec((None, None, 1, ff), bsel),
                  pl.BlockSpec((None, None, 1, D), bsel), hbm, hbm, hbm],
        out_specs=pl.BlockSpec((MOE_BLOCK, D), lambda i, be, nv, nx: (i, 0)),
        scratch_shapes=[pltpu.VMEM((3, D, ff), F32), pltpu.VMEM((D, ff), BF16), pltpu.VMEM((D, ff), BF16),
                        pltpu.VMEM((ff, D), BF16), pltpu.SemaphoreType.DMA((3,))])
    return pl.pallas_call(
        functools.partial(_expert_kernel, layer=layer), grid_spec=grid_spec,
        out_shape=jax.ShapeDtypeStruct((n_rows, D), F32),
        compiler_params=_cp(("arbitrary",)), name="moe_experts",
    )(blk_expert, n_valid, next_expert, x_rows, b_gate.reshape(depth, ne, 1, ff),
      b_up.reshape(depth, ne, 1, ff), b_down.reshape(depth, ne, 1, D), w_gate, w_up, w_down)


def moe_layer(lay, layer, x, g2, sc2, sh2, gate2, router_w, router_b, w_gate, b_gate, w_up, b_up, w_down,
              b_down):
    n_tok = x.shape[0]
    nt = n_tok // TOK_TILE
    rw = jnp.zeros((D, LANES), F32).at[:, :N_EXPERTS].set(router_w)
    rw = _hilo(rw)
    rb = jnp.full((1, LANES), NEG, F32).at[0, :N_EXPERTS].set(router_b)
    h2, _, gates, pos, posT, cnt = moe_router(lay, x, g2, sc2, sh2, rw, rb)
    n_te = cnt[:, 0, :N_EXPERTS].astype(I32)
    n_te = (n_te + SEG_ALIGN - 1) // SEG_ALIGN * SEG_ALIGN
    totals = jnp.sum(n_te, axis=0)
    padded = (totals + MOE_BLOCK - 1) // MOE_BLOCK * MOE_BLOCK
    padded_end = jnp.cumsum(padded)
    pstart = padded_end - padded
    dst = pstart[None, :] + jnp.cumsum(n_te, axis=0) - n_te
    off = jnp.cumsum(n_te, axis=1) - n_te
    n_rows = nt * TILE_ROWS + N_EXPERTS * MOE_BLOCK
    nb = n_rows // MOE_BLOCK
    n_valid = (padded_end[-1] // MOE_BLOCK).astype(I32).reshape(1)
    bstart = jnp.minimum(jnp.arange(nb, dtype=I32), n_valid[0] - 1) * MOE_BLOCK
    blk_expert = jnp.minimum(jnp.sum((bstart[:, None] >= padded_end[None, :]).astype(I32), axis=1),
                             N_EXPERTS - 1).astype(I32)
    tabs = (n_te.reshape(-1).astype(I32), off.reshape(-1).astype(I32), dst.reshape(-1).astype(I32),
            jnp.sum(n_te, axis=1).astype(I32))
    tail_n = jnp.concatenate([(padded - totals).astype(I32), n_valid])
    x_rows = moe_dispatch(*tabs, tail_n, (pstart + totals).astype(I32), posT, h2, n_rows)
    owner = jnp.where(padded > 0, jnp.arange(N_EXPERTS, dtype=I32), N_EXPERTS)
    later = jnp.concatenate([lax.cummin(owner, axis=0, reverse=True)[1:], jnp.full((1,), N_EXPERTS, I32)])
    next_expert = jnp.where(later < N_EXPERTS, later, -1).astype(I32)
    y_rows = moe_experts(layer, blk_expert, n_valid, next_expert, x_rows, w_gate, b_gate, w_up, b_up,
                         w_down, b_down)
    return moe_combine(lay, *tabs, pos, gates, x, gate2, y_rows)


QKV_TN = 256
N_QK_TILES = (ATT_HEADS + ATT_KV) * ATT_HD // QKV_TN


def _qkv_kernel(x_ref, g_ref, sc_ref, sh_ref, w_ref, nw_ref, cos_ref, sin_ref, o_ref, *, p_tok):
    i = pl.program_id(0)
    tm = x_ref.shape[0]
    h = _modnorm(x_ref[...], g_ref[...], sc_ref[...], sh_ref[...]).astype(BF16)
    r = lax.broadcasted_iota(I32, (QKV_TN, QKV_TN), 0) // ATT_HD
    c = lax.broadcasted_iota(I32, (QKV_TN, QKV_TN), 1) // ATT_HD
    head_mean = jnp.where(r == c, 1.0 / ATT_HD, 0.0).astype(BF16)
    lane = lax.broadcasted_iota(I32, (tm, QKV_TN), 1)
    half = ATT_HD // 4
    first = (lane % (2 * half)) < half
    for j in range(w_ref.shape[1] // QKV_TN):
        cols = slice(QKV_TN * j, QKV_TN * (j + 1))
        acc = _dot(h, w_ref[:, cols])
        if j >= N_QK_TILES:
            o_ref[:, cols] = acc
            continue
        ms = _dot((acc * acc).astype(BF16), head_mean)
        qn = acc * lax.rsqrt(ms + EPS) * nw_ref[j]

        @pl.when(i * tm < p_tok)
        def _():
            o_ref[:, cols] = qn

        @pl.when(i * tm >= p_tok)
        def _():
            swapped = jnp.where(first, pltpu.roll(qn, QKV_TN - half, 1), pltpu.roll(qn, half, 1))
            o_ref[:, cols] = qn * cos_ref[...] + swapped * sin_ref[...]


def _rope_tables(sample_len):
    pos = np.arange(sample_len)
    half = ATT_HD // 4
    inv = (ROPE_THETA ** (-np.arange(half, dtype=np.float32) / half)).astype(np.float32)
    ang_r = (pos // GRID_W).astype(np.float32)[:, None] * inv[None, :]
    ang_c = (pos % GRID_W).astype(np.float32)[:, None] * inv[None, :]
    cos = np.concatenate([np.cos(ang_r)] * 2 + [np.cos(ang_c)] * 2, axis=1)
    sin = np.concatenate([-np.sin(ang_r), np.sin(ang_r), -np.sin(ang_c), np.sin(ang_c)], axis=1)
    rep = QKV_TN // ATT_HD
    return (jnp.asarray(np.tile(cos, (1, rep)), F32), jnp.asarray(np.tile(sin, (1, rep)), F32))


def qkv_proj(lay, x, g, sc, sh, w, q_norm, k_norm, tm=256):
    n_tok = x.shape[0]
    n = w.shape[1]
    nq = ATT_HEADS * ATT_HD // QKV_TN
    rep = QKV_TN // ATT_HD
    nw = jnp.concatenate([jnp.tile(jnp.tile(q_norm, rep)[None, :], (nq, 1)),
                          jnp.tile(jnp.tile(k_norm, rep)[None, :], (n // QKV_TN - nq, 1))], axis=0)
    cos, sin = _rope_tables(lay.sample_len)
    mrow = lambda i: (lay.mod_row(i * tm), 0, 0)
    rrow = lambda i: (jnp.where(i * tm < lay.p_tok, 0, ((i * tm - lay.p_tok) % lay.sample_len) // tm), 0)
    nt = n // QKV_TN
    return pl.pallas_call(
        functools.partial(_qkv_kernel, p_tok=lay.p_tok), grid=(n_tok // tm,),
        in_specs=[pl.BlockSpec((tm, D), lambda i: (i, 0)),
                  pl.BlockSpec((1, D), lambda i: (0, 0)),
                  pl.BlockSpec((None, 1, D), mrow), pl.BlockSpec((None, 1, D), mrow),
                  pl.BlockSpec((D, n), lambda i: (0, 0)),
                  pl.BlockSpec((nt, 1, QKV_TN), lambda i: (0, 0, 0)),
                  pl.BlockSpec((tm, QKV_TN), rrow), pl.BlockSpec((tm, QKV_TN), rrow)],
        out_specs=pl.BlockSpec((tm, n), lambda i: (i, 0)),
        out_shape=jax.ShapeDtypeStruct((n_tok, n), F32),
        compiler_params=_cp(("arbitrary",)), name="qkv_proj",
    )(x, g.reshape(1, D), sc, sh, w, nw.reshape(nt, 1, QKV_TN), cos, sin)


def _dup_group(x, g):
    blk = x[:, LANES * (g // 2):LANES * (g // 2 + 1)]
    if g % 2 == 1:
        blk = pltpu.roll(blk, ATT_HD, 1)
    lo = lax.broadcasted_iota(I32, blk.shape, 1) < ATT_HD
    low = jnp.where(lo, blk, 0.0)
    return low + pltpu.roll(low, ATT_HD, 1)


def _attend(q_ref, k_all, v_all, mask, sink_ref, o_ref):
    nq = q_ref.shape[0]
    lo = lax.broadcasted_iota(I32, (nq, LANES), 1) < ATT_HD
    grp = ATT_HEADS // ATT_KV
    for g in range(ATT_KV):
        k2 = _dup_group(k_all, g).astype(BF16)
        v2 = _dup_group(v_all, g).astype(BF16)
        for jp in range(grp // 2):
            j = g * (grp // 2) + jp
            qp = q_ref[:, LANES * j:LANES * (j + 1)] * (ATT_HD ** -0.5)
            outs = []
            for half in range(2):
                qh = jnp.where(lo, qp, 0.0) if half == 0 else jnp.where(lo, 0.0, qp)
                s = _dot_nt(qh.astype(BF16), k2)
                if mask is not None:
                    s = jnp.where(mask, s, NEG)
                sink = sink_ref[2 * j + half]
                m = jnp.maximum(jnp.max(s, axis=1, keepdims=True), sink)
                p = jnp.exp(s - m)
                den = jnp.sum(p, axis=1, keepdims=True) + jnp.exp(sink - m)
                outs.append(_dot(p.astype(BF16), v2) / den)
            o_ref[:, LANES * j:LANES * (j + 1)] = jnp.where(lo, outs[0], outs[1]).astype(o_ref.dtype)


def _attn_ctx_kernel(sink_ref, q_ref, k_ref, v_ref, o_ref):
    _attend(q_ref, k_ref[...], v_ref[...], None, sink_ref, o_ref)


def attn_context(lay, qkv, sinks):
    qw = ATT_HEADS * ATT_HD
    kw = ATT_KV * ATT_HD
    ln = lay.prompt_len
    grid_spec = pltpu.PrefetchScalarGridSpec(
        num_scalar_prefetch=0, grid=(lay.n_prompt,),
        in_specs=[pl.BlockSpec(memory_space=pltpu.SMEM),
                  pl.BlockSpec((ln, qw), lambda b: (b, 0)),
                  pl.BlockSpec((ln, kw), lambda b: (b, qw // kw)),
                  pl.BlockSpec((ln, kw), lambda b: (b, qw // kw + 1))],
        out_specs=pl.BlockSpec((ln, qw), lambda b: (b, 0)))
    return pl.pallas_call(
        _attn_ctx_kernel, grid_spec=grid_spec,
        out_shape=jax.ShapeDtypeStruct((lay.p_tok, qw), BF16),
        compiler_params=_cp(("arbitrary",)), name="attn_context",
    )(sinks, qkv, qkv, qkv)


def _attn_lat_kernel(sink_ref, q_ref, kp_ref, kc_ref, kn_ref, vp_ref, vc_ref, vn_ref, ck_ref, cv_ref, o_ref,
                     *, nblk):
    i = pl.program_id(1)
    bq = ATT_BLOCK
    nctx = ck_ref.shape[1]
    k_all = jnp.concatenate([kp_ref[...], kc_ref[...], kn_ref[...], ck_ref[0]], axis=0)
    v_all = jnp.concatenate([vp_ref[...], vc_ref[...], vn_ref[...], cv_ref[0]], axis=0)
    ns = 3 * bq + nctx
    r = lax.broadcasted_iota(I32, (bq, ns), 0)
    c = lax.broadcasted_iota(I32, (bq, ns), 1)
    rel = c - r
    first_key = jnp.where(i > 0, 0, bq)
    end_key = jnp.where(i < nblk - 1, 3 * bq, 2 * bq)
    band = (rel >= bq - WINDOW) & (rel <= bq + WINDOW) & (c >= first_key) & (c < end_key)
    mask = band | (c >= 3 * bq)
    _attend(q_ref, k_all, v_all, mask, sink_ref, o_ref)


def attn_latent(lay, qkv, cache_k, cache_v, sinks):
    qw = ATT_HEADS * ATT_HD
    kw = ATT_KV * ATT_HD
    bq = ATT_BLOCK
    nblk = lay.sample_len // bq
    b0 = lay.p_tok // bq
    nctx = cache_k.shape[1]
    rb = lambda b, i: b0 + b * nblk + i
    kspec = lambda cb, sh: pl.BlockSpec(
        (bq, kw), lambda b, i: (b0 + b * nblk + jnp.clip(i + sh, 0, nblk - 1), cb))
    kc, vc = qw // kw, qw // kw + 1
    grid_spec = pltpu.PrefetchScalarGridSpec(
        num_scalar_prefetch=0, grid=(lay.n_sample, nblk),
        in_specs=[pl.BlockSpec(memory_space=pltpu.SMEM),
                  pl.BlockSpec((bq, qw), lambda b, i: (rb(b, i), 0)),
                  kspec(kc, -1), kspec(kc, 0), kspec(kc, 1),
                  kspec(vc, -1), kspec(vc, 0), kspec(vc, 1),
                  pl.BlockSpec((1, nctx, kw), lambda b, i: (b, 0, 0)),
                  pl.BlockSpec((1, nctx, kw), lambda b, i: (b, 0, 0))],
        out_specs=pl.BlockSpec((bq, qw), lambda b, i: (b * nblk + i, 0)))
    return pl.pallas_call(
        functools.partial(_attn_lat_kernel, nblk=nblk), grid_spec=grid_spec,
        out_shape=jax.ShapeDtypeStruct((lay.n_sample * lay.sample_len, qw), BF16),
        compiler_params=_cp(("arbitrary", "arbitrary")), name="attn_latent",
    )(sinks, qkv, qkv, qkv, qkv, qkv, qkv, qkv,
      cache_k.reshape(lay.n_sample, nctx, kw), cache_v.reshape(lay.n_sample, nctx, kw))


def _forward(lay, x_prompt, x_sample, state_l0_ssd_fwd, state_l0_ssd_bwd, state_l0_gla_fwd, state_l0_gla_bwd,
             cache_l1_k, cache_l1_v, c, c_ctx, ada_w, ada_b, norm1, norm2,
             l0_w_in, l0_conv_w, l0_conv_b, l0_a_log, l0_dt_bias, l0_d_skip, l0_ssd_norm,
             l0_gate_w2, l0_gate_b, l0_gla_norm, l0_w_out,
             l1_w_qkv, l1_q_norm, l1_k_norm, l1_sinks, l1_w_out,
             router_w, router_b, exp_w_gate, exp_b_gate, exp_w_up, exp_b_up, exp_w_down, exp_b_down):
    np_, ns = lay.n_prompt, lay.n_sample
    x = jnp.concatenate([x_prompt.reshape(-1, D), x_sample.reshape(-1, D)], axis=0)
    cond8 = jnp.zeros((8, D), F32).at[0].set(c_ctx).at[1:1 + ns].set(c)
    mod = ada_table(cond8, ada_w, ada_b)
    mods = [[mod[l, :, p * D:(p + 1) * D].reshape(8, 1, D) for p in range(N_ADA)] for l in range(2)]

    def moe(l, xx):
        return moe_layer(lay, l, xx, norm2[l], mods[l][4], mods[l][3], mods[l][5], router_w[l], router_b[l],
                         exp_w_gate, exp_b_gate, exp_w_up, exp_b_up, exp_w_down, exp_b_down)

    sp = np.cumsum((SSD_INNER, SSD_INNER + 2 * SSD_GROUPS * SSD_STATE, 2 * SSD_HEADS,
                    GLA_HEADS * GLA_KEY_DIM, GLA_HEADS * GLA_KEY_DIM,
                    GLA_HEADS * GLA_VAL_DIM, GLA_HEADS * GLA_VAL_DIM, 2 * GLA_RANK))
    cols = lambda a, b: l0_w_in[:, a:b]
    w_main = jnp.concatenate([cols(0, sp[0]), cols(sp[4], sp[5]), cols(sp[5], sp[6]), cols(sp[0], sp[1]),
                              cols(sp[2], sp[3]), cols(sp[3], sp[4])], axis=1).astype(BF16)
    w_small = jnp.concatenate([cols(sp[1], sp[2]), cols(sp[6], sp[7]),
                               jnp.zeros((D, LANES - 2 * SSD_HEADS - 2 * GLA_RANK), F32)], axis=1)
    proj, small = norm_proj(lay, x, norm1[0], mods[0][1], mods[0][0], w_main, _hilo(w_small), 512, PJ_W // 2,
                             BF16)
    y_n, o_n, (ssd_f, ssd_b, gla_f, gla_b) = l0_mixers(
        lay, proj, small, state_l0_ssd_fwd, state_l0_ssd_bwd, state_l0_gla_fwd, state_l0_gla_bwd,
        l0_conv_w, l0_conv_b, l0_a_log, l0_dt_bias, l0_d_skip, l0_ssd_norm,
        l0_gate_w2, l0_gate_b, l0_gla_norm)
    x = proj_residual(lay, [y_n, o_n], l0_w_out.astype(BF16), x, mods[0][2])
    x = moe(0, x)

    qkv = qkv_proj(lay, x, norm1[1], mods[1][1], mods[1][0], l1_w_qkv.astype(BF16), l1_q_norm, l1_k_norm)
    o_ctx = attn_context(lay, qkv, l1_sinks)
    o_lat = attn_latent(lay, qkv, cache_l1_k, cache_l1_v, l1_sinks)
    o = jnp.concatenate([o_ctx, o_lat], axis=0)
    x = proj_residual(lay, [o], l1_w_out.astype(BF16), x, mods[1][2])
    x = moe(1, x)

    qw = ATT_HEADS * ATT_HD
    kw = ATT_KV * ATT_HD
    return (x[:lay.p_tok].reshape(x_prompt.shape), x[lay.p_tok:].reshape(x_sample.shape),
            ssd_f[:np_].reshape(np_, SSD_HEADS, SSD_HEAD_DIM, SSD_STATE),
            ssd_b[:np_].reshape(np_, SSD_HEADS, SSD_HEAD_DIM, SSD_STATE),
            gla_f[:np_], gla_b[:np_],
            qkv[:lay.p_tok, qw:qw + kw].reshape(np_, lay.prompt_len, ATT_KV, ATT_HD),
            qkv[:lay.p_tok, qw + kw:].reshape(np_, lay.prompt_len, ATT_KV, ATT_HD))


def kernel(x_prompt, x_sample, state_l0_ssd_fwd, state_l0_ssd_bwd, state_l0_gla_fwd, state_l0_gla_bwd, cache_l1_k, cache_l1_v, c, c_ctx, ada_w, ada_b, norm1, norm2, l0_w_in, l0_conv_w, l0_conv_b, l0_a_log, l0_dt_bias, l0_d_skip, l0_ssd_norm, l0_gate_w2, l0_gate_b, l0_gla_norm, l0_w_out, l1_w_qkv, l1_q_norm, l1_k_norm, l1_sinks, l1_w_out, router_w, router_b, exp_w_gate, exp_b_gate, exp_w_up, exp_b_up, exp_w_down, exp_b_down):
    lay = Layout(x_prompt.shape[0], x_prompt.shape[1], x_sample.shape[0], x_sample.shape[1])
    return _forward(lay, x_prompt, x_sample, state_l0_ssd_fwd, state_l0_ssd_bwd, state_l0_gla_fwd,
                    state_l0_gla_bwd, cache_l1_k, cache_l1_v, c, c_ctx, ada_w, ada_b, norm1, norm2,
                    l0_w_in, l0_conv_w, l0_conv_b, l0_a_log, l0_dt_bias, l0_d_skip, l0_ssd_norm,
                    l0_gate_w2, l0_gate_b, l0_gla_norm, l0_w_out,
                    l1_w_qkv, l1_q_norm, l1_k_norm, l1_sinks, l1_w_out,
                    router_w, router_b, exp_w_gate, exp_b_gate, exp_w_up, exp_b_up, exp_w_down, exp_b_down)
```

```python
import functools
import math

import numpy as np
import jax
import jax.numpy as jnp
from jax import lax
from jax.experimental import pallas as pl
from jax.experimental.pallas import tpu as pltpu

F32 = jnp.float32
BF16 = jnp.bfloat16
I32 = jnp.int32
HI = lax.Precision.HIGHEST

D = 1024
EPS = 1e-6
N_ADA = 6
SSD_HEADS = 16
SSD_HEAD_DIM = 64
SSD_INNER = 1024
SSD_STATE = 128
SSD_GROUPS = 2
SSD_CONV = 5
SSD_CHUNK = 128
GLA_HEADS = 4
GLA_KEY_DIM = 128
GLA_VAL_DIM = 256
GLA_RANK = 16
GLA_TAU = 16.0
GLA_BLOCK = 64
ATT_HEADS = 16
ATT_KV = 4
ATT_HD = 64
ATT_BLOCK = 128
WINDOW = 128
GRID_W = 64
ROPE_THETA = 10000.0
N_EXPERTS = 32
TOP_K = 4
EXPERT_FF = 1024
SWIGLU_LIMIT = 7.0
SWIGLU_ALPHA = 1.702
MOE_BLOCK = 256
TOK_TILE = 256
LANES = 128
NEG = -1e30

PJ_Z, PJ_V, PJ_OG, PJ_XBC, PJ_Q, PJ_K = 0, 1024, 2048, 3072, 4608, 5120
PJ_W = 5632
VMEM_LIMIT = 48 * 1024 * 1024


def _cp(sem, vmem=VMEM_LIMIT):
    return pltpu.CompilerParams(dimension_semantics=sem, vmem_limit_bytes=vmem)


class Layout:
    def __init__(self, n_prompt, prompt_len, n_sample, sample_len):
        self.n_prompt, self.prompt_len = n_prompt, prompt_len
        self.n_sample, self.sample_len = n_sample, sample_len
        self.p_tok = n_prompt * prompt_len
        self.n_tok = self.p_tok + n_sample * sample_len
        self.seqs = [(i * prompt_len, prompt_len) for i in range(n_prompt)]
        self.seqs += [(self.p_tok + i * sample_len, sample_len) for i in range(n_sample)]
        self.n_seq = len(self.seqs)

    def mod_row(self, start):
        return jnp.where(start < self.p_tok, 0, 1 + (start - self.p_tok) // self.sample_len)

def _sigmoid(x):
    return 1.0 / (1.0 + jnp.exp(-x))


def _silu(x):
    return x * _sigmoid(x)


def _softplus(x):
    return jnp.maximum(x, 0.0) + jnp.log(1.0 + jnp.exp(-jnp.abs(x)))


def _modnorm(x, g, sc, sh):
    ms = jnp.mean(x * x, axis=-1, keepdims=True)
    return (x * lax.rsqrt(ms + EPS) * g) * (1.0 + sc) + sh


def _dot(a, b, **kw):
    return jnp.dot(a, b, preferred_element_type=F32, **kw)


def _dot_nt(a, b):
    return lax.dot_general(a, b, (((1,), (1,)), ((), ())), preferred_element_type=F32)


def _dot_tn(a, b):
    return lax.dot_general(a, b, (((0,), (0,)), ((), ())), preferred_element_type=F32)


def _split(x, n):
    parts = []
    for _ in range(n):
        p = x.astype(BF16)
        parts.append(p)
        x = x - p.astype(F32)
    return parts


def _dot_sel(sel, x):
    sel = sel.astype(BF16)
    return sum(_dot(sel, p) for p in _split(x, 3))


def _dot_sel_r(x, sel):
    sel = sel.astype(BF16)
    return sum(_dot(p, sel) for p in _split(x, 3))


def _dot_hilo(x, w_hi, w_lo):
    x_hi, x_lo = _split(x, 2)
    return _dot(x_hi, w_hi) + _dot(x_lo, w_hi) + _dot(x_hi, w_lo)


def _hilo(w):
    hi = w.astype(BF16)
    return jnp.stack([hi, (w - hi.astype(F32)).astype(BF16)])


def _ada_kernel(c_ref, w_ref, b_ref, o_ref):
    o_ref[0] = _dot(_silu(c_ref[...]), w_ref[0], precision=HI) + b_ref[0]


def ada_table(cond8, ada_w, ada_b):
    depth, _, n = ada_w.shape
    tn = 1536
    return pl.pallas_call(
        _ada_kernel, grid=(depth, n // tn),
        in_specs=[pl.BlockSpec((8, D), lambda l, j: (0, 0)),
                  pl.BlockSpec((1, D, tn), lambda l, j: (l, 0, j)),
                  pl.BlockSpec((1, 1, tn), lambda l, j: (l, 0, j))],
        out_specs=pl.BlockSpec((1, 8, tn), lambda l, j: (l, 0, j)),
        out_shape=jax.ShapeDtypeStruct((depth, 8, n), F32),
        compiler_params=_cp(("arbitrary", "arbitrary")), name="ada_table",
    )(cond8, ada_w, ada_b.reshape(depth, 1, n))


def _proj_kernel(x_ref, g_ref, sc_ref, sh_ref, w_ref, ws_ref, o_ref, os_ref, h_scr):
    @pl.when(pl.program_id(1) == 0)
    def _():
        h = _modnorm(x_ref[...], g_ref[...], sc_ref[...], sh_ref[...])
        h_scr[...] = h.astype(BF16)
        os_ref[...] = _dot_hilo(h, ws_ref[0], ws_ref[1])

    o_ref[...] = _dot(h_scr[...], w_ref[...]).astype(o_ref.dtype)


def norm_proj(lay, x, g, sc, sh, w, w_small, tm, tn, out_dtype):
    n_tok = x.shape[0]
    n = w.shape[1]
    ns = w_small.shape[-1]
    mrow = lambda i, j: (lay.mod_row(i * tm), 0, 0)
    return pl.pallas_call(
        _proj_kernel, grid=(n_tok // tm, n // tn),
        in_specs=[pl.BlockSpec((tm, D), lambda i, j: (i, 0)),
                  pl.BlockSpec((1, D), lambda i, j: (0, 0)),
                  pl.BlockSpec((None, 1, D), mrow),
                  pl.BlockSpec((None, 1, D), mrow),
                  pl.BlockSpec((D, tn), lambda i, j: (0, j)),
                  pl.BlockSpec((2, D, ns), lambda i, j: (0, 0, 0))],
        out_specs=[pl.BlockSpec((tm, tn), lambda i, j: (i, j)),
                   pl.BlockSpec((tm, ns), lambda i, j: (i, 0))],
        out_shape=[jax.ShapeDtypeStruct((n_tok, n), out_dtype),
                   jax.ShapeDtypeStruct((n_tok, ns), F32)],
        scratch_shapes=[pltpu.VMEM((tm, D), BF16)],
        compiler_params=_cp(("arbitrary", "arbitrary")), name="norm_proj",
    )(x, g.reshape(1, D), sc, sh, w, w_small)


CONV_HALO = 16


def _ssd_load(d, c, xc, dtg_ref, dtgT_ref, S):
    q = SSD_CHUNK
    nh = SSD_HEADS
    rows = pl.ds(pl.multiple_of(c * q, q), q)
    return (rows, xc[rows, 0:SSD_INNER], xc[rows, SSD_INNER:SSD_INNER + 2 * SSD_GROUPS * SSD_STATE],
            dtg_ref[rows, nh * d:nh * d + nh], dtgT_ref[c, nh * d:nh * d + nh, :], S[...])


def _ssd_chunk(d, loaded, alog_ref, alogT_ref, dtb_ref, dtbT_ref):
    q = SSD_CHUNK
    nh = SSD_HEADS
    _, xs, bc, dtg, dtgT, s = loaded
    xs = xs.astype(F32)
    dt = _softplus(dtg + dtb_ref[d:d + 1, :])
    dtT = _softplus(dtgT + dtbT_ref[:, d:d + 1])
    ad = dt * (-jnp.exp(alog_ref[d:d + 1, :]))
    adT = dtT * (-jnp.exp(alogT_ref[:, d:d + 1]))
    row = lax.broadcasted_iota(I32, (q, q), 0)
    col = lax.broadcasted_iota(I32, (q, q), 1)
    if d == 0:
        e = _dot_sel(col <= row, ad)
        eT = _dot_sel_r(adT, row <= col)
        tot = e[q - 1:q, :]
        mask = row >= col
        fq = jnp.exp(e)
        fk = jnp.exp(tot - e)
    else:
        e = _dot_sel(col < row, ad)
        eT = _dot_sel_r(adT, row < col)
        tot = jnp.sum(ad, axis=0, keepdims=True)
        mask = col >= row
        fq = jnp.exp(tot - e)
        fk = jnp.exp(e)
    dec = jnp.exp(tot)
    lo = lax.broadcasted_iota(I32, (q, LANES), 1) < SSD_HEAD_DIM
    lo1 = lax.broadcasted_iota(I32, (1, LANES), 1) < SSD_HEAD_DIM

    def colpat(arr, a):
        return jnp.where(lo, arr[:, a:a + 1], arr[:, a + 1:a + 2])

    rep = (nh // SSD_GROUPS) // 2
    ys, s_new = [], []
    for g in range(SSD_GROUPS):
        bg = bc[:, SSD_STATE * g:SSD_STATE * (g + 1)]
        cg = bc[:, SSD_STATE * (SSD_GROUPS + g):SSD_STATE * (SSD_GROUPS + g + 1)]
        gmat = _dot_nt(cg, bg)
        for j in range(rep * g, rep * (g + 1)):
            a = 2 * j
            sl = slice(LANES * j, LANES * (j + 1))
            parts = []
            for hh in (a, a + 1):
                if d == 0:
                    diff = e[:, hh:hh + 1] - eT[hh:hh + 1, :]
                else:
                    diff = eT[hh:hh + 1, :] - e[:, hh:hh + 1]
                parts.append((gmat * jnp.exp(jnp.where(mask, diff, NEG))).astype(BF16))
            lhs = jnp.concatenate(parts, axis=1)
            xdt = xs[:, sl] * colpat(dt, a)
            rhs = jnp.concatenate([jnp.where(lo, xdt, 0.0), jnp.where(lo, 0.0, xdt)], axis=0)
            sj = s[:, sl]
            ys.append(_dot(lhs, rhs.astype(BF16)) + _dot(cg, sj.astype(BF16)) * colpat(fq, a))
            xk = (xdt * colpat(fk, a)).astype(BF16)
            decp = jnp.where(lo1, dec[:, a:a + 1], dec[:, a + 1:a + 2])
            s_new.append(sj * decp + _dot_tn(bg, xk))
    return jnp.concatenate(ys, axis=1), jnp.concatenate(s_new, axis=1)


def _ssd_seq_kernel(xbc_ref, z_ref, dtg_ref, dtgT_ref, s0f_ref, s0b_ref, cw_ref, cb_ref,
                    alog_ref, alogT_ref, dtb_ref, dtbT_ref, dskip_ref, nrm_ref,
                    y_ref, sf_ref, sb_ref, xc, ext, Sf, Sb, yf, yb):
    ln = xbc_ref.shape[0]
    q = SSD_CHUNK
    nc = ln // q
    h = CONV_HALO
    pad = SSD_CONV // 2

    def conv_body(c, carry):
        r0 = pl.multiple_of(c * q, q)
        prev = xbc_ref[pl.ds(pl.multiple_of(jnp.maximum(r0 - h, 0), h), h), :].astype(F32)
        nxt = xbc_ref[pl.ds(pl.multiple_of(jnp.minimum(r0 + q, ln - h), h), h), :].astype(F32)
        ext[0:h, :] = jnp.where(c > 0, prev, 0.0)
        ext[h:h + q, :] = xbc_ref[pl.ds(r0, q), :].astype(F32)
        ext[h + q:h + q + h, :] = jnp.where(c < nc - 1, nxt, 0.0)
        acc = jnp.broadcast_to(cb_ref[...], (q, cb_ref.shape[1]))
        for k in range(SSD_CONV):
            acc = acc + cw_ref[k:k + 1, :] * ext[h - pad + k:h - pad + k + q, :]
        xc[pl.ds(r0, q), :] = _silu(acc).astype(xc.dtype)
        return carry
    lax.fori_loop(0, nc, conv_body, 0)

    Sf[...] = s0f_ref[0].T
    Sb[...] = s0b_ref[0].T
    params = (alog_ref, alogT_ref, dtb_ref, dtbT_ref)

    def scan_body(c, carry):
        lf = _ssd_load(0, c, xc, dtg_ref, dtgT_ref, Sf)
        lb = _ssd_load(1, nc - 1 - c, xc, dtg_ref, dtgT_ref, Sb)
        y_f, s_f = _ssd_chunk(0, lf, *params)
        y_b, s_b = _ssd_chunk(1, lb, *params)
        yf[lf[0], :] = y_f
        yb[lb[0], :] = y_b
        Sf[...] = s_f
        Sb[...] = s_b
        return carry
    lax.fori_loop(0, nc, scan_body, 0)
    sf_ref[0] = Sf[...].T
    sb_ref[0] = Sb[...].T

    def out_body(c, carry):
        rows = pl.ds(pl.multiple_of(c * q, q), q)
        ytot = yf[rows, :] + yb[rows, :] + dskip_ref[...] * xc[rows, 0:SSD_INNER].astype(F32)
        yg = ytot * _silu(z_ref[rows, :].astype(F32))
        ms = jnp.mean(yg * yg, axis=-1, keepdims=True)
        y_ref[rows, :] = (yg * lax.rsqrt(ms + EPS) * nrm_ref[...]).astype(y_ref.dtype)
        return carry
    lax.fori_loop(0, nc, out_body, 0)


def _ssd_call(n_seq, ln, blk0, proj, small, smallT3, s0f, s0b, params):
    q = SSD_CHUNK
    hp = SSD_HEADS * SSD_HEAD_DIM
    cw = SSD_INNER + 2 * SSD_GROUPS * SSD_STATE
    nc = ln // q
    assert PJ_XBC % cw == 0 and PJ_Z % SSD_INNER == 0
    tok = lambda w, cb: pl.BlockSpec((ln, w), lambda b: (blk0 + b, cb))
    seq3 = pl.BlockSpec((1, hp, SSD_STATE), lambda b: (b, 0, 0))
    full = lambda a: pl.BlockSpec(a.shape, lambda b: (0,) * a.ndim)
    return pl.pallas_call(
        _ssd_seq_kernel, grid=(n_seq,),
        in_specs=[tok(cw, PJ_XBC // cw), tok(SSD_INNER, PJ_Z // SSD_INNER), tok(LANES, 0),
                  pl.BlockSpec((nc, 2 * SSD_HEADS, q), lambda b: (blk0 + b, 0, 0)), seq3, seq3]
        + [full(a) for a in params],
        out_specs=[pl.BlockSpec((ln, hp), lambda b: (b, 0)), seq3, seq3],
        out_shape=[jax.ShapeDtypeStruct((n_seq * ln, hp), BF16),
                   jax.ShapeDtypeStruct((n_seq, hp, SSD_STATE), F32),
                   jax.ShapeDtypeStruct((n_seq, hp, SSD_STATE), F32)],
        scratch_shapes=[pltpu.VMEM((ln, cw), BF16), pltpu.VMEM((q + 2 * CONV_HALO, cw), F32),
                        pltpu.VMEM((SSD_STATE, hp), F32), pltpu.VMEM((SSD_STATE, hp), F32),
                        pltpu.VMEM((ln, hp), F32), pltpu.VMEM((ln, hp), F32)],
        compiler_params=_cp(("arbitrary",)), name="ssd_seq",
    )(proj, proj, small, smallT3, s0f, s0b, *params)


def _gla_load(d, c, q_ref, k_ref, v_ref, glr_ref, S):
    t = GLA_BLOCK
    rows = pl.ds(pl.multiple_of(c * t, t), t)
    c0 = 2 * SSD_HEADS + GLA_RANK * d
    return (rows, q_ref[rows, :], k_ref[rows, :], v_ref[rows, :], glr_ref[rows, c0:c0 + GLA_RANK],
            [S[h] for h in range(GLA_HEADS)])


def _gla_block(d, loaded, w2_ref, gb_ref):
    t = GLA_BLOCK
    dk, dv = GLA_KEY_DIM, GLA_VAL_DIM
    _, q, k, v, glr, states = loaded
    gp = _dot_hilo(glr, w2_ref[0, d], w2_ref[1, d]) + gb_ref[d:d + 1, :]
    la = -_softplus(-gp) * (1.0 / GLA_TAU)
    row = lax.broadcasted_iota(I32, (t, t), 0)
    col = lax.broadcasted_iota(I32, (t, t), 1)
    mid = t // 2 - 1
    if d == 0:
        e = _dot_sel(col <= row, la)
        tot = e[t - 1:t, :]
        r = e[mid:mid + 1, :]
        fqi, fki = jnp.exp(e - r), jnp.exp(r - e)
        fq, fk = jnp.exp(e), jnp.exp(tot - e)
        mask = row >= col
    else:
        e = _dot_sel(col < row, la)
        tot = e[t - 1:t, :] + la[t - 1:t, :]
        r = e[mid:mid + 1, :]
        fqi, fki = jnp.exp(r - e), jnp.exp(e - r)
        fq, fk = jnp.exp(tot - e), jnp.exp(e)
        mask = col >= row
    dec = jnp.exp(tot)
    qf = q.astype(F32) * (dk ** -0.5)
    kf = k.astype(F32)
    outs, new_states = [], []
    for h in range(GLA_HEADS):
        sl = slice(dk * h, dk * (h + 1))
        qh, kh = qf[:, sl], kf[:, sl]
        sc = _dot_nt((qh * fqi[:, sl]).astype(BF16), (kh * fki[:, sl]).astype(BF16))
        sc = jnp.where(mask, sc, 0.0)
        vh = v[:, dv * h:dv * (h + 1)]
        st = states[h]
        outs.append(_dot(sc.astype(BF16), vh) + _dot_nt((qh * fq[:, sl]).astype(BF16), st.astype(BF16)))
        new_states.append(st * dec[:, sl] + _dot_tn(vh, (kh * fk[:, sl]).astype(BF16)))
    return jnp.concatenate(outs, axis=1), new_states


def _gla_seq_kernel(q_ref, k_ref, v_ref, og_ref, glr_ref, s0f_ref, s0b_ref, w2_ref, gb_ref, nrm_ref,
                    o_ref, sf_ref, sb_ref, Sf, Sb, of, ob):
    ln = q_ref.shape[0]
    t = GLA_BLOCK
    nc = ln // t
    dv = GLA_VAL_DIM
    for h in range(GLA_HEADS):
        Sf[h] = s0f_ref[0, h].T
        Sb[h] = s0b_ref[0, h].T
    ins = (q_ref, k_ref, v_ref, glr_ref)

    def scan_body(c, carry):
        lf = _gla_load(0, c, *ins, Sf)
        lb = _gla_load(1, nc - 1 - c, *ins, Sb)
        o_f, s_f = _gla_block(0, lf, w2_ref, gb_ref)
        o_b, s_b = _gla_block(1, lb, w2_ref, gb_ref)
        of[lf[0], :] = o_f
        ob[lb[0], :] = o_b
        for h in range(GLA_HEADS):
            Sf[h] = s_f[h]
            Sb[h] = s_b[h]
        return carry
    lax.fori_loop(0, nc, scan_body, 0, unroll=2)
    for h in range(GLA_HEADS):
        sf_ref[0, h] = Sf[h].T
        sb_ref[0, h] = Sb[h].T

    def out_body(c, carry):
        rows = pl.ds(pl.multiple_of(c * t, t), t)
        for h in range(GLA_HEADS):
            vl = slice(dv * h, dv * (h + 1))
            ot = of[rows, vl] + ob[rows, vl]
            ms = jnp.mean(ot * ot, axis=-1, keepdims=True)
            on = ot * lax.rsqrt(ms + EPS) * nrm_ref[...]
            o_ref[rows, vl] = (on * _silu(og_ref[rows, vl].astype(F32))).astype(o_ref.dtype)
        return carry
    lax.fori_loop(0, nc, out_body, 0)


def _gla_call(n_seq, ln, blk0, proj, small, s0f, s0b, params):
    qk_w = GLA_HEADS * GLA_KEY_DIM
    v_w = GLA_HEADS * GLA_VAL_DIM
    tok = lambda w, cb: pl.BlockSpec((ln, w), lambda b: (blk0 + b, cb))
    seq4 = pl.BlockSpec((1, GLA_HEADS, GLA_KEY_DIM, GLA_VAL_DIM), lambda b: (b, 0, 0, 0))
    full = lambda a: pl.BlockSpec(a.shape, lambda b: (0,) * a.ndim)
    st_shape = jax.ShapeDtypeStruct((n_seq, GLA_HEADS, GLA_KEY_DIM, GLA_VAL_DIM), F32)
    return pl.pallas_call(
        _gla_seq_kernel, grid=(n_seq,),
        in_specs=[tok(qk_w, PJ_Q // qk_w), tok(qk_w, PJ_K // qk_w), tok(v_w, PJ_V // v_w),
                  tok(v_w, PJ_OG // v_w), tok(LANES, 0), seq4, seq4] + [full(a) for a in params],
        out_specs=[pl.BlockSpec((ln, v_w), lambda b: (b, 0)), seq4, seq4],
        out_shape=[jax.ShapeDtypeStruct((n_seq * ln, v_w), BF16), st_shape, st_shape],
        scratch_shapes=[pltpu.VMEM((GLA_HEADS, GLA_VAL_DIM, GLA_KEY_DIM), F32),
                        pltpu.VMEM((GLA_HEADS, GLA_VAL_DIM, GLA_KEY_DIM), F32),
                        pltpu.VMEM((ln, v_w), F32), pltpu.VMEM((ln, v_w), F32)],
        compiler_params=_cp(("arbitrary",)), name="gla_seq",
    )(proj, proj, proj, proj, small, s0f, s0b, *params)


def l0_mixers(lay, proj, small, ssd_f0, ssd_b0, gla_f0, gla_b0, conv_w, conv_b, a_log, dt_bias, d_skip,
              ssd_norm, gate_w2, gate_b, gla_norm):
    q = SSD_CHUNK
    hp = SSD_HEADS * SSD_HEAD_DIM
    n_tok = proj.shape[0]
    smallT3 = small.reshape(n_tok // q, q, LANES).swapaxes(1, 2)
    ssd_p = (conv_w, conv_b.reshape(1, -1), a_log, a_log.T, dt_bias, dt_bias.T,
             jnp.repeat(d_skip, SSD_HEAD_DIM).reshape(1, hp), ssd_norm.reshape(1, hp))
    gla_p = (_hilo(gate_w2), gate_b, gla_norm.reshape(1, -1))
    np_, ns = lay.n_prompt, lay.n_sample
    assert lay.p_tok % lay.sample_len == 0
    groups = [(np_, lay.prompt_len, 0, jnp.zeros((np_, hp, SSD_STATE), F32), jnp.zeros((np_, hp, SSD_STATE), F32),
               jnp.zeros((np_,) + gla_f0.shape[1:], F32), jnp.zeros((np_,) + gla_f0.shape[1:], F32)),
              (ns, lay.sample_len, lay.p_tok // lay.sample_len, ssd_f0.reshape(ns, hp, SSD_STATE),
               ssd_b0.reshape(ns, hp, SSD_STATE), gla_f0, gla_b0)]
    ys, os_, states = [], [], None
    dtT = smallT3[:, :2 * SSD_HEADS, :]
    for n, ln, blk0, sf0, sb0, gf0, gb0 in groups:
        y, sf, sb = _ssd_call(n, ln, blk0, proj, small, dtT, sf0, sb0, ssd_p)
        o, gf, gb = _gla_call(n, ln, blk0, proj, small, gf0, gb0, gla_p)
        ys.append(y)
        os_.append(o)
        if states is None:
            states = (sf, sb, gf, gb)
    return jnp.concatenate(ys, axis=0), jnp.concatenate(os_, axis=0), states


def _res_kernel(*refs, ks):
    n = len(ks)
    a_refs, w_ref, x_ref, gate_ref, o_ref = refs[:n], refs[n], refs[n + 1], refs[n + 2], refs[n + 3]
    acc = None
    off = 0
    for a_ref, k in zip(a_refs, ks):
        part = _dot(a_ref[...], w_ref[off:off + k, :])
        acc = part if acc is None else acc + part
        off += k
    o_ref[...] = x_ref[...] + gate_ref[...] * acc


def proj_residual(lay, acts, w, x, gate, tm=512):
    n_tok = x.shape[0]
    ks = tuple(int(a.shape[1]) for a in acts)
    mrow = lambda i: (lay.mod_row(i * tm), 0, 0)
    return pl.pallas_call(
        functools.partial(_res_kernel, ks=ks), grid=(n_tok // tm,),
        in_specs=[pl.BlockSpec((tm, k), lambda i: (i, 0)) for k in ks]
        + [pl.BlockSpec(w.shape, lambda i: (0, 0)),
           pl.BlockSpec((tm, D), lambda i: (i, 0)),
           pl.BlockSpec((None, 1, D), mrow)],
        out_specs=pl.BlockSpec((tm, D), lambda i: (i, 0)),
        out_shape=jax.ShapeDtypeStruct((n_tok, D), F32),
        compiler_params=_cp(("arbitrary",)), name="proj_residual",
    )(*acts, w, x, gate)


def _router_kernel(x_ref, g_ref, sc_ref, sh_ref, rw_ref, rb_ref,
                   h_ref, idx_ref, gate_ref, pos_ref, posT_ref, cnt_ref):
    tm = x_ref.shape[0]
    h = _modnorm(x_ref[...], g_ref[...], sc_ref[...], sh_ref[...])
    h_hi = h.astype(BF16)
    h_ref[...] = h_hi
    h_lo = (h - h_hi.astype(F32)).astype(BF16)
    lg = (_dot(h_hi, rw_ref[0]) + _dot(h_lo, rw_ref[0]) + _dot(h_hi, rw_ref[1])
          + rb_ref[...])
    lane = lax.broadcasted_iota(I32, (tm, LANES), 1).astype(F32)
    vals, ids = [], []
    for _ in range(TOP_K):
        m = jnp.max(lg, axis=1, keepdims=True)
        i = jnp.min(jnp.where(lg == m, lane, float(LANES)), axis=1, keepdims=True)
        vals.append(m)
        ids.append(i)
        lg = jnp.where(lane == i, -jnp.inf, lg)
    ex = [jnp.exp(v - vals[0]) for v in vals]
    den = ex[0] + ex[1] + ex[2] + ex[3]
    sel = jnp.zeros((tm, LANES), F32)
    for i in ids:
        sel = sel + (lane == i).astype(F32)
    row = lax.broadcasted_iota(I32, (tm, tm), 0)
    col = lax.broadcasted_iota(I32, (tm, tm), 1)
    before = _dot((col < row).astype(BF16), sel.astype(BF16))
    n = jnp.sum(sel, axis=0, keepdims=True)
    er = lax.broadcasted_iota(I32, (LANES, LANES), 0)
    ec = lax.broadcasted_iota(I32, (LANES, LANES), 1)
    n_al = jnp.ceil(n * (1.0 / SEG_ALIGN)) * SEG_ALIGN
    offs = _dot(jnp.broadcast_to(n_al, (8, LANES)).astype(BF16), (er < ec).astype(BF16))[0:1, :]
    slot = before + offs
    idx_o = jnp.zeros((tm, LANES), F32)
    gate_o = jnp.zeros((tm, LANES), F32)
    pos_o = jnp.zeros((tm, LANES), F32)
    for k in range(TOP_K):
        p = jnp.sum(jnp.where(lane == ids[k], slot, 0.0), axis=1, keepdims=True)
        idx_o = jnp.where(lane == k, ids[k], idx_o)
        gate_o = jnp.where(lane == k, ex[k] / den, gate_o)
        pos_o = jnp.where(lane == k, p, pos_o)
    idx_ref[...] = idx_o.astype(I32)
    gate_ref[...] = gate_o
    pos_ref[...] = pos_o.astype(I32)
    posT_ref[...] = pos_o.T[0:8, :]
    cnt_ref[0] = jnp.broadcast_to(n, (8, LANES))


def moe_router(lay, x, g, sc, sh, rw, rb):
    n_tok = x.shape[0]
    tm = TOK_TILE
    nt = n_tok // tm
    mrow = lambda i: (lay.mod_row(i * tm), 0, 0)
    tile = lambda w, dt: (pl.BlockSpec((tm, w), lambda i: (i, 0)), jax.ShapeDtypeStruct((n_tok, w), dt))
    outs = [tile(D, BF16), tile(LANES, I32), tile(LANES, F32), tile(LANES, I32),
            (pl.BlockSpec((8, tm), lambda i: (0, i)), jax.ShapeDtypeStruct((8, n_tok), F32)),
            (pl.BlockSpec((1, 8, LANES), lambda i: (i, 0, 0)), jax.ShapeDtypeStruct((nt, 8, LANES), F32))]
    return pl.pallas_call(
        _router_kernel, grid=(nt,),
        in_specs=[pl.BlockSpec((tm, D), lambda i: (i, 0)),
                  pl.BlockSpec((1, D), lambda i: (0, 0)),
                  pl.BlockSpec((None, 1, D), mrow), pl.BlockSpec((None, 1, D), mrow),
                  pl.BlockSpec((2, D, LANES), lambda i: (0, 0, 0)),
                  pl.BlockSpec((1, LANES), lambda i: (0, 0))],
        out_specs=[o[0] for o in outs], out_shape=[o[1] for o in outs],
        compiler_params=_cp(("arbitrary",)), name="moe_router",
    )(x, g.reshape(1, D), sc, sh, rw, rb)


SEG_ALIGN = 8
SEG_BITS = tuple(range(int(math.log2(TOK_TILE)), int(math.log2(SEG_ALIGN)) - 1, -1))
TILE_ROWS = TOK_TILE * TOP_K + N_EXPERTS * SEG_ALIGN


def _pow2_copies(n, src, dst, make_copy, op, bits):
    for b in bits:
        sz = 1 << b
        done = (n >> (b + 1)) << (b + 1)

        @pl.when((n & sz) != 0)
        def _():
            op(make_copy(pl.multiple_of(src + done, SEG_ALIGN), pl.multiple_of(dst + done, SEG_ALIGN), sz))


def _start_segments(i, n_ref, off_ref, dst_ref, make_copy):
    def body(e, carry):
        k = i * N_EXPERTS + e
        _pow2_copies(n_ref[k], off_ref[k], dst_ref[k], make_copy, lambda c: c.start(), SEG_BITS)
        return carry
    lax.fori_loop(0, N_EXPERTS, body, 0)


TAIL_BITS = tuple(range(int(math.log2(MOE_BLOCK)) - 1, int(math.log2(SEG_ALIGN)) - 1, -1))
TILE_BITS = tuple(range(int(math.log2(TILE_ROWS)), int(math.log2(SEG_ALIGN)) - 1, -1))


def _wait_rows(total, make_copy):
    _pow2_copies(total, 0, 0, make_copy, lambda c: c.wait(), TILE_BITS)


def _dispatch_kernel(n_ref, off_ref, dst_ref, tot_ref, tn_ref, td_ref, posT_ref, h_ref, xout_ref,
                     srt, zbuf, sems):
    i = pl.program_id(0)
    last = pl.num_programs(0) - 1
    slot = i % 2
    tm = h_ref.shape[0]
    r = lax.broadcasted_iota(I32, (TILE_ROWS, tm), 0)
    hit = jnp.zeros((TILE_ROWS, tm), jnp.bool_)
    for k in range(TOP_K):
        hit = hit | (r == posT_ref[k:k + 1, :].astype(I32))
    sel = jnp.where(hit, 1.0, 0.0).astype(BF16)
    srt[slot] = _dot(sel, h_ref[...])

    def copier(s):
        def make_copy(src, dst, sz):
            return pltpu.make_async_copy(srt.at[s, pl.ds(src, sz)], xout_ref.at[pl.ds(dst, sz)], sems.at[s])
        return make_copy

    _start_segments(i, n_ref, off_ref, dst_ref, copier(slot))

    @pl.when(i > 0)
    def _():
        _wait_rows(tot_ref[jnp.maximum(i - 1, 0)], copier(1 - slot))

    @pl.when(i == last)
    def _():
        _wait_rows(tot_ref[i], copier(slot))
        zbuf[...] = jnp.zeros_like(zbuf)
        sem = sems.at[0]

        def zero_copy(src, dst, sz):
            return pltpu.make_async_copy(zbuf.at[pl.ds(src, sz)], xout_ref.at[pl.ds(dst, sz)], sem)

        nb = xout_ref.shape[0] // MOE_BLOCK
        for op in (lambda c: c.start(), lambda c: c.wait()):
            def body(e, carry):
                _pow2_copies(tn_ref[e], 0, td_ref[e], zero_copy, op, TAIL_BITS)
                return carry
            lax.fori_loop(0, N_EXPERTS, body, 0)

            def unused(b, carry):
                op(zero_copy(0, pl.multiple_of(b * MOE_BLOCK, MOE_BLOCK), MOE_BLOCK))
                return carry
            lax.fori_loop(tn_ref[N_EXPERTS], nb, unused, 0)


def moe_dispatch(n_tab, off_tab, dst_tab, tot_tab, tail_n, tail_dst, posT, h2, n_rows):
    n_tok = h2.shape[0]
    tm = TOK_TILE
    grid_spec = pltpu.PrefetchScalarGridSpec(
        num_scalar_prefetch=6, grid=(n_tok // tm,),
        in_specs=[pl.BlockSpec((8, tm), lambda i, *_: (0, i)),
                  pl.BlockSpec((tm, D), lambda i, *_: (i, 0))],
        out_specs=pl.BlockSpec(memory_space=pl.ANY),
        scratch_shapes=[pltpu.VMEM((2, TILE_ROWS, D), F32), pltpu.VMEM((MOE_BLOCK, D), F32),
                        pltpu.SemaphoreType.DMA((2,))])
    return pl.pallas_call(
        _dispatch_kernel, grid_spec=grid_spec,
        out_shape=jax.ShapeDtypeStruct((n_rows, D), F32),
        compiler_params=_cp(("arbitrary",)), name="moe_dispatch",
    )(n_tab, off_tab, dst_tab, tot_tab, tail_n, tail_dst, posT, h2)


def _combine_kernel(n_ref, off_ref, dst_ref, tot_ref, pos_ref, gate_ref, x_ref, g2_ref, y_ref, o_ref,
                    buf, sems):
    i = pl.program_id(0)
    last = pl.num_programs(0) - 1
    slot = i % 2
    tm = x_ref.shape[0]
    na = TILE_ROWS

    def copier(s):
        def make_copy(src, dst, sz):
            return pltpu.make_async_copy(y_ref.at[pl.ds(dst, sz)], buf.at[s, pl.ds(src, sz)], sems.at[s])
        return make_copy

    def fetch(tile, s):
        buf[s, tm * TOP_K:na, :] = jnp.zeros((na - tm * TOP_K, D), F32)
        _start_segments(tile, n_ref, off_ref, dst_ref, copier(s))

    @pl.when(i == 0)
    def _():
        fetch(i, slot)

    @pl.when(i < last)
    def _():
        fetch(i + 1, 1 - slot)

    _wait_rows(tot_ref[i], copier(slot))
    lane = lax.broadcasted_iota(I32, (tm, na), 1)
    pw = jnp.zeros((tm, na), F32)
    for k in range(TOP_K):
        pw = pw + jnp.where(lane == pos_ref[:, k:k + 1], gate_ref[:, k:k + 1], 0.0)
    phi = pw.astype(BF16)
    plo = (pw - phi.astype(F32)).astype(BF16)
    yb = buf[slot].astype(BF16)
    o_ref[...] = x_ref[...] + g2_ref[...] * (_dot(phi, yb) + _dot(plo, yb))


def moe_combine(lay, n_tab, off_tab, dst_tab, tot_tab, pos, gates, x, gate2, y_rows):
    n_tok = x.shape[0]
    tm = TOK_TILE
    mrow = lambda i, *_: (lay.mod_row(i * tm), 0, 0)
    grid_spec = pltpu.PrefetchScalarGridSpec(
        num_scalar_prefetch=4, grid=(n_tok // tm,),
        in_specs=[pl.BlockSpec((tm, LANES), lambda i, *_: (i, 0)),
                  pl.BlockSpec((tm, LANES), lambda i, *_: (i, 0)),
                  pl.BlockSpec((tm, D), lambda i, *_: (i, 0)),
                  pl.BlockSpec((None, 1, D), mrow),
                  pl.BlockSpec(memory_space=pl.ANY)],
        out_specs=pl.BlockSpec((tm, D), lambda i, *_: (i, 0)),
        scratch_shapes=[pltpu.VMEM((2, TILE_ROWS, D), F32), pltpu.SemaphoreType.DMA((2,))])
    return pl.pallas_call(
        _combine_kernel, grid_spec=grid_spec,
        out_shape=jax.ShapeDtypeStruct((n_tok, D), F32),
        compiler_params=_cp(("arbitrary",)), name="moe_combine",
    )(n_tab, off_tab, dst_tab, tot_tab, pos, gates, x, gate2, y_rows)


def _expert_kernel(be_ref, nv_ref, nxt_ref, slot_ref, x_ref, bg_ref, bu_ref, bd_ref, wg_hbm, wu_hbm, wd_hbm,
                   y_ref, wf, sems, *, layer):
    i = pl.program_id(0)
    valid = i < nv_ref[0]
    e = be_ref[i]
    slot = slot_ref[e]
    changed = jnp.logical_or(i == 0, e != be_ref[jnp.maximum(i - 1, 0)])

    def weight_copies(ex, s):
        return [pltpu.make_async_copy(w.at[layer, ex], wf.at[s, k], sems.at[s, k])
                for k, w in enumerate((wg_hbm, wu_hbm, wd_hbm))]

    @pl.when(jnp.logical_and(valid, changed))
    def _():
        @pl.when(i == 0)
        def _():
            for c in weight_copies(e, slot):
                c.start()

        nxt = nxt_ref[e]

        @pl.when(nxt >= 0)
        def _():
            for c in weight_copies(nxt, 1 - slot):
                c.start()

        for c in weight_copies(e, slot):
            c.wait()

    @pl.when(valid)
    def _():
        x = x_ref[...]
        gt = jnp.minimum(_dot(x, wf[slot, 0]) + bg_ref[...], SWIGLU_LIMIT)
        up = jnp.clip(_dot(x, wf[slot, 1]) + bu_ref[...], -SWIGLU_LIMIT, SWIGLU_LIMIT)
        act = (up + 1.0) * gt * _sigmoid(SWIGLU_ALPHA * gt)
        y_ref[...] = _dot(act, wf[slot, 2]) + bd_ref[...]

    @pl.when(jnp.logical_not(valid))
    def _():
        y_ref[...] = jnp.zeros_like(y_ref)


def moe_experts(layer, blk_expert, n_valid, next_expert, slot, x_rows, w_gate, b_gate, w_up, b_up, w_down,
                b_down):
    n_rows = x_rows.shape[0]
    nb = n_rows // MOE_BLOCK
    depth, ne, _, ff = w_gate.shape
    assert ff == D
    rowblk = lambda i, be, nv, *_: (jnp.maximum(jnp.minimum(i, nv[0] - 1), 0), 0)
    bsel = lambda i, be, *_: (layer, be[i], 0, 0)
    hbm = pl.BlockSpec(memory_space=pl.ANY)
    grid_spec = pltpu.PrefetchScalarGridSpec(
        num_scalar_prefetch=4, grid=(nb,),
        in_specs=[pl.BlockSpec((MOE_BLOCK, D), rowblk),
                  pl.BlockSpec((None, None, 1, ff), bsel), pl.BlockSpec((None, None, 1, ff), bsel),
                  pl.BlockSpec((None, None, 1, D), bsel), hbm, hbm, hbm],
        out_specs=pl.BlockSpec((MOE_BLOCK, D), lambda i, *_: (i, 0)),
        scratch_shapes=[pltpu.VMEM((2, 3, D, ff), F32), pltpu.SemaphoreType.DMA((2, 3))])
    return pl.pallas_call(
        functools.partial(_expert_kernel, layer=layer), grid_spec=grid_spec,
        out_shape=jax.ShapeDtypeStruct((n_rows, D), F32),
        compiler_params=_cp(("arbitrary",)), name="moe_experts",
    )(blk_expert, n_valid, next_expert, slot, x_rows, b_gate.reshape(depth, ne, 1, ff),
      b_up.reshape(depth, ne, 1, ff), b_down.reshape(depth, ne, 1, D), w_gate, w_up, w_down)


def moe_layer(lay, layer, x, g2, sc2, sh2, gate2, router_w, router_b, w_gate, b_gate, w_up, b_up, w_down,
              b_down):
    n_tok = x.shape[0]
    nt = n_tok // TOK_TILE
    rw = jnp.zeros((D, LANES), F32).at[:, :N_EXPERTS].set(router_w)
    rw = _hilo(rw)
    rb = jnp.full((1, LANES), NEG, F32).at[0, :N_EXPERTS].set(router_b)
    h2, _, gates, pos, posT, cnt = moe_router(lay, x, g2, sc2, sh2, rw, rb)
    n_te = cnt[:, 0, :N_EXPERTS].astype(I32)
    n_te = (n_te + SEG_ALIGN - 1) // SEG_ALIGN * SEG_ALIGN
    totals = jnp.sum(n_te, axis=0)
    padded = (totals + MOE_BLOCK - 1) // MOE_BLOCK * MOE_BLOCK
    padded_end = jnp.cumsum(padded)
    pstart = padded_end - padded
    dst = pstart[None, :] + jnp.cumsum(n_te, axis=0) - n_te
    off = jnp.cumsum(n_te, axis=1) - n_te
    n_rows = nt * TILE_ROWS + N_EXPERTS * MOE_BLOCK
    nb = n_rows // MOE_BLOCK
    n_valid = (padded_end[-1] // MOE_BLOCK).astype(I32).reshape(1)
    bstart = jnp.minimum(jnp.arange(nb, dtype=I32), n_valid[0] - 1) * MOE_BLOCK
    blk_expert = jnp.minimum(jnp.sum((bstart[:, None] >= padded_end[None, :]).astype(I32), axis=1),
                             N_EXPERTS - 1).astype(I32)
    tabs = (n_te.reshape(-1).astype(I32), off.reshape(-1).astype(I32), dst.reshape(-1).astype(I32),
            jnp.sum(n_te, axis=1).astype(I32))
    tail_n = jnp.concatenate([(padded - totals).astype(I32), n_valid])
    x_rows = moe_dispatch(*tabs, tail_n, (pstart + totals).astype(I32), posT, h2, n_rows)
    owner = jnp.where(padded > 0, jnp.arange(N_EXPERTS, dtype=I32), N_EXPERTS)
    later = jnp.concatenate([lax.cummin(owner, axis=0, reverse=True)[1:], jnp.full((1,), N_EXPERTS, I32)])
    next_expert = jnp.where(later < N_EXPERTS, later, -1).astype(I32)
    slot = ((jnp.cumsum((padded > 0).astype(I32)) - 1) % 2).astype(I32)
    y_rows = moe_experts(layer, blk_expert, n_valid, next_expert, slot, x_rows, w_gate, b_gate, w_up, b_up,
                         w_down, b_down)
    return moe_combine(lay, *tabs, pos, gates, x, gate2, y_rows)


QKV_TN = 256
N_QK_TILES = (ATT_HEADS + ATT_KV) * ATT_HD // QKV_TN


def _qkv_kernel(x_ref, g_ref, sc_ref, sh_ref, w_ref, nw_ref, cos_ref, sin_ref, o_ref, *, p_tok):
    i = pl.program_id(0)
    tm = x_ref.shape[0]
    h = _modnorm(x_ref[...], g_ref[...], sc_ref[...], sh_ref[...]).astype(BF16)
    r = lax.broadcasted_iota(I32, (QKV_TN, QKV_TN), 0) // ATT_HD
    c = lax.broadcasted_iota(I32, (QKV_TN, QKV_TN), 1) // ATT_HD
    head_mean = jnp.where(r == c, 1.0 / ATT_HD, 0.0).astype(BF16)
    lane = lax.broadcasted_iota(I32, (tm, QKV_TN), 1)
    half = ATT_HD // 4
    first = (lane % (2 * half)) < half
    for j in range(w_ref.shape[1] // QKV_TN):
        cols = slice(QKV_TN * j, QKV_TN * (j + 1))
        acc = _dot(h, w_ref[:, cols])
        if j >= N_QK_TILES:
            o_ref[:, cols] = acc
            continue
        ms = _dot((acc * acc).astype(BF16), head_mean)
        qn = acc * lax.rsqrt(ms + EPS) * nw_ref[j]

        @pl.when(i * tm < p_tok)
        def _():
            o_ref[:, cols] = qn

        @pl.when(i * tm >= p_tok)
        def _():
            swapped = jnp.where(first, pltpu.roll(qn, QKV_TN - half, 1), pltpu.roll(qn, half, 1))
            o_ref[:, cols] = qn * cos_ref[...] + swapped * sin_ref[...]


def _rope_tables(sample_len):
    pos = np.arange(sample_len)
    half = ATT_HD // 4
    inv = (ROPE_THETA ** (-np.arange(half, dtype=np.float32) / half)).astype(np.float32)
    ang_r = (pos // GRID_W).astype(np.float32)[:, None] * inv[None, :]
    ang_c = (pos % GRID_W).astype(np.float32)[:, None] * inv[None, :]
    cos = np.concatenate([np.cos(ang_r)] * 2 + [np.cos(ang_c)] * 2, axis=1)
    sin = np.concatenate([-np.sin(ang_r), np.sin(ang_r), -np.sin(ang_c), np.sin(ang_c)], axis=1)
    rep = QKV_TN // ATT_HD
    return (jnp.asarray(np.tile(cos, (1, rep)), F32), jnp.asarray(np.tile(sin, (1, rep)), F32))


def qkv_proj(lay, x, g, sc, sh, w, q_norm, k_norm, tm=256):
    n_tok = x.shape[0]
    n = w.shape[1]
    nq = ATT_HEADS * ATT_HD // QKV_TN
    rep = QKV_TN // ATT_HD
    nw = jnp.concatenate([jnp.tile(jnp.tile(q_norm, rep)[None, :], (nq, 1)),
                          jnp.tile(jnp.tile(k_norm, rep)[None, :], (n // QKV_TN - nq, 1))], axis=0)
    cos, sin = _rope_tables(lay.sample_len)
    mrow = lambda i: (lay.mod_row(i * tm), 0, 0)
    rrow = lambda i: (jnp.where(i * tm < lay.p_tok, 0, ((i * tm - lay.p_tok) % lay.sample_len) // tm), 0)
    nt = n // QKV_TN
    return pl.pallas_call(
        functools.partial(_qkv_kernel, p_tok=lay.p_tok), grid=(n_tok // tm,),
        in_specs=[pl.BlockSpec((tm, D), lambda i: (i, 0)),
                  pl.BlockSpec((1, D), lambda i: (0, 0)),
                  pl.BlockSpec((None, 1, D), mrow), pl.BlockSpec((None, 1, D), mrow),
                  pl.BlockSpec((D, n), lambda i: (0, 0)),
                  pl.BlockSpec((nt, 1, QKV_TN), lambda i: (0, 0, 0)),
                  pl.BlockSpec((tm, QKV_TN), rrow), pl.BlockSpec((tm, QKV_TN), rrow)],
        out_specs=pl.BlockSpec((tm, n), lambda i: (i, 0)),
        out_shape=jax.ShapeDtypeStruct((n_tok, n), F32),
        compiler_params=_cp(("arbitrary",)), name="qkv_proj",
    )(x, g.reshape(1, D), sc, sh, w, nw.reshape(nt, 1, QKV_TN), cos, sin)


def _dup_group(x, g):
    blk = x[:, LANES * (g // 2):LANES * (g // 2 + 1)]
    if g % 2 == 1:
        blk = pltpu.roll(blk, ATT_HD, 1)
    lo = lax.broadcasted_iota(I32, blk.shape, 1) < ATT_HD
    low = jnp.where(lo, blk, 0.0)
    return low + pltpu.roll(low, ATT_HD, 1)


def _attend(q_ref, k_all, v_all, mask, sink_ref, o_ref):
    nq = q_ref.shape[0]
    lo = lax.broadcasted_iota(I32, (nq, LANES), 1) < ATT_HD
    grp = ATT_HEADS // ATT_KV
    for g in range(ATT_KV):
        k2 = _dup_group(k_all, g).astype(BF16)
        v2 = _dup_group(v_all, g).astype(BF16)
        for jp in range(grp // 2):
            j = g * (grp // 2) + jp
            qp = q_ref[:, LANES * j:LANES * (j + 1)] * (ATT_HD ** -0.5)
            outs = []
            for half in range(2):
                qh = jnp.where(lo, qp, 0.0) if half == 0 else jnp.where(lo, 0.0, qp)
                s = _dot_nt(qh.astype(BF16), k2)
                if mask is not None:
                    s = jnp.where(mask, s, NEG)
                sink = sink_ref[2 * j + half]
                m = jnp.maximum(jnp.max(s, axis=1, keepdims=True), sink)
                p = jnp.exp(s - m)
                den = jnp.sum(p, axis=1, keepdims=True) + jnp.exp(sink - m)
                outs.append(_dot(p.astype(BF16), v2) / den)
            o_ref[:, LANES * j:LANES * (j + 1)] = jnp.where(lo, outs[0], outs[1]).astype(o_ref.dtype)


def _attn_ctx_kernel(sink_ref, q_ref, k_ref, v_ref, o_ref):
    _attend(q_ref, k_ref[...], v_ref[...], None, sink_ref, o_ref)


def attn_context(lay, qkv, sinks):
    qw = ATT_HEADS * ATT_HD
    kw = ATT_KV * ATT_HD
    ln = lay.prompt_len
    grid_spec = pltpu.PrefetchScalarGridSpec(
        num_scalar_prefetch=0, grid=(lay.n_prompt,),
        in_specs=[pl.BlockSpec(memory_space=pltpu.SMEM),
                  pl.BlockSpec((ln, qw), lambda b: (b, 0)),
                  pl.BlockSpec((ln, kw), lambda b: (b, qw // kw)),
                  pl.BlockSpec((ln, kw), lambda b: (b, qw // kw + 1))],
        out_specs=pl.BlockSpec((ln, qw), lambda b: (b, 0)))
    return pl.pallas_call(
        _attn_ctx_kernel, grid_spec=grid_spec,
        out_shape=jax.ShapeDtypeStruct((lay.p_tok, qw), BF16),
        compiler_params=_cp(("arbitrary",)), name="attn_context",
    )(sinks, qkv, qkv, qkv)


def _attn_lat_kernel(sink_ref, q_ref, kp_ref, kc_ref, kn_ref, vp_ref, vc_ref, vn_ref, ck_ref, cv_ref, o_ref,
                     *, nblk):
    i = pl.program_id(1)
    bq = ATT_BLOCK
    nctx = ck_ref.shape[1]
    k_all = jnp.concatenate([kp_ref[...], kc_ref[...], kn_ref[...], ck_ref[0]], axis=0)
    v_all = jnp.concatenate([vp_ref[...], vc_ref[...], vn_ref[...], cv_ref[0]], axis=0)
    ns = 3 * bq + nctx
    r = lax.broadcasted_iota(I32, (bq, ns), 0)
    c = lax.broadcasted_iota(I32, (bq, ns), 1)
    rel = c - r
    first_key = jnp.where(i > 0, 0, bq)
    end_key = jnp.where(i < nblk - 1, 3 * bq, 2 * bq)
    band = (rel >= bq - WINDOW) & (rel <= bq + WINDOW) & (c >= first_key) & (c < end_key)
    mask = band | (c >= 3 * bq)
    _attend(q_ref, k_all, v_all, mask, sink_ref, o_ref)


def attn_latent(lay, qkv, cache_k, cache_v, sinks):
    qw = ATT_HEADS * ATT_HD
    kw = ATT_KV * ATT_HD
    bq = ATT_BLOCK
    nblk = lay.sample_len // bq
    b0 = lay.p_tok // bq
    nctx = cache_k.shape[1]
    rb = lambda b, i: b0 + b * nblk + i
    kspec = lambda cb, sh: pl.BlockSpec(
        (bq, kw), lambda b, i: (b0 + b * nblk + jnp.clip(i + sh, 0, nblk - 1), cb))
    kc, vc = qw // kw, qw // kw + 1
    grid_spec = pltpu.PrefetchScalarGridSpec(
        num_scalar_prefetch=0, grid=(lay.n_sample, nblk),
        in_specs=[pl.BlockSpec(memory_space=pltpu.SMEM),
                  pl.BlockSpec((bq, qw), lambda b, i: (rb(b, i), 0)),
                  kspec(kc, -1), kspec(kc, 0), kspec(kc, 1),
                  kspec(vc, -1), kspec(vc, 0), kspec(vc, 1),
                  pl.BlockSpec((1, nctx, kw), lambda b, i: (b, 0, 0)),
                  pl.BlockSpec((1, nctx, kw), lambda b, i: (b, 0, 0))],
        out_specs=pl.BlockSpec((bq, qw), lambda b, i: (b * nblk + i, 0)))
    return pl.pallas_call(
        functools.partial(_attn_lat_kernel, nblk=nblk), grid_spec=grid_spec,
        out_shape=jax.ShapeDtypeStruct((lay.n_sample * lay.sample_len, qw), BF16),
        compiler_params=_cp(("arbitrary", "arbitrary")), name="attn_latent",
    )(sinks, qkv, qkv, qkv, qkv, qkv, qkv, qkv,
      cache_k.reshape(lay.n_sample, nctx, kw), cache_v.reshape(lay.n_sample, nctx, kw))


def _forward(lay, x_prompt, x_sample, state_l0_ssd_fwd, state_l0_ssd_bwd, state_l0_gla_fwd, state_l0_gla_bwd,
             cache_l1_k, cache_l1_v, c, c_ctx, ada_w, ada_b, norm1, norm2,
             l0_w_in, l0_conv_w, l0_conv_b, l0_a_log, l0_dt_bias, l0_d_skip, l0_ssd_norm,
             l0_gate_w2, l0_gate_b, l0_gla_norm, l0_w_out,
             l1_w_qkv, l1_q_norm, l1_k_norm, l1_sinks, l1_w_out,
             router_w, router_b, exp_w_gate, exp_b_gate, exp_w_up, exp_b_up, exp_w_down, exp_b_down):
    np_, ns = lay.n_prompt, lay.n_sample
    x = jnp.concatenate([x_prompt.reshape(-1, D), x_sample.reshape(-1, D)], axis=0)
    cond8 = jnp.zeros((8, D), F32).at[0].set(c_ctx).at[1:1 + ns].set(c)
    mod = ada_table(cond8, ada_w, ada_b)
    mods = [[mod[l, :, p * D:(p + 1) * D].reshape(8, 1, D) for p in range(N_ADA)] for l in range(2)]

    def moe(l, xx):
        return moe_layer(lay, l, xx, norm2[l], mods[l][4], mods[l][3], mods[l][5], router_w[l], router_b[l],
                         exp_w_gate, exp_b_gate, exp_w_up, exp_b_up, exp_w_down, exp_b_down)

    sp = np.cumsum((SSD_INNER, SSD_INNER + 2 * SSD_GROUPS * SSD_STATE, 2 * SSD_HEADS,
                    GLA_HEADS * GLA_KEY_DIM, GLA_HEADS * GLA_KEY_DIM,
                    GLA_HEADS * GLA_VAL_DIM, GLA_HEADS * GLA_VAL_DIM, 2 * GLA_RANK))
    cols = lambda a, b: l0_w_in[:, a:b]
    w_main = jnp.concatenate([cols(0, sp[0]), cols(sp[4], sp[5]), cols(sp[5], sp[6]), cols(sp[0], sp[1]),
                              cols(sp[2], sp[3]), cols(sp[3], sp[4])], axis=1).astype(BF16)
    w_small = jnp.concatenate([cols(sp[1], sp[2]), cols(sp[6], sp[7]),
                               jnp.zeros((D, LANES - 2 * SSD_HEADS - 2 * GLA_RANK), F32)], axis=1)
    proj, small = norm_proj(lay, x, norm1[0], mods[0][1], mods[0][0], w_main, _hilo(w_small), 512, PJ_W // 2,
                             BF16)
    y_n, o_n, (ssd_f, ssd_b, gla_f, gla_b) = l0_mixers(
        lay, proj, small, state_l0_ssd_fwd, state_l0_ssd_bwd, state_l0_gla_fwd, state_l0_gla_bwd,
        l0_conv_w, l0_conv_b, l0_a_log, l0_dt_bias, l0_d_skip, l0_ssd_norm,
        l0_gate_w2, l0_gate_b, l0_gla_norm)
    x = proj_residual(lay, [y_n, o_n], l0_w_out.astype(BF16), x, mods[0][2])
    x = moe(0, x)

    qkv = qkv_proj(lay, x, norm1[1], mods[1][1], mods[1][0], l1_w_qkv.astype(BF16), l1_q_norm, l1_k_norm)
    o_ctx = attn_context(lay, qkv, l1_sinks)
    o_lat = attn_latent(lay, qkv, cache_l1_k, cache_l1_v, l1_sinks)
    o = jnp.concatenate([o_ctx, o_lat], axis=0)
    x = proj_residual(lay, [o], l1_w_out.astype(BF16), x, mods[1][2])
    x = moe(1, x)

    qw = ATT_HEADS * ATT_HD
    kw = ATT_KV * ATT_HD
    return (x[:lay.p_tok].reshape(x_prompt.shape), x[lay.p_tok:].reshape(x_sample.shape),
            ssd_f[:np_].reshape(np_, SSD_HEADS, SSD_HEAD_DIM, SSD_STATE),
            ssd_b[:np_].reshape(np_, SSD_HEADS, SSD_HEAD_DIM, SSD_STATE),
            gla_f[:np_], gla_b[:np_],
            qkv[:lay.p_tok, qw:qw + kw].reshape(np_, lay.prompt_len, ATT_KV, ATT_HD),
            qkv[:lay.p_tok, qw + kw:].reshape(np_, lay.prompt_len, ATT_KV, ATT_HD))


def kernel(x_prompt, x_sample, state_l0_ssd_fwd, state_l0_ssd_bwd, state_l0_gla_fwd, state_l0_gla_bwd, cache_l1_k, cache_l1_v, c, c_ctx, ada_w, ada_b, norm1, norm2, l0_w_in, l0_conv_w, l0_conv_b, l0_a_log, l0_dt_bias, l0_d_skip, l0_ssd_norm, l0_gate_w2, l0_gate_b, l0_gla_norm, l0_w_out, l1_w_qkv, l1_q_norm, l1_k_norm, l1_sinks, l1_w_out, router_w, router_b, exp_w_gate, exp_b_gate, exp_w_up, exp_b_up, exp_w_down, exp_b_down):
    lay = Layout(x_prompt.shape[0], x_prompt.shape[1], x_sample.shape[0], x_sample.shape[1])
    return _forward(lay, x_prompt, x_sample, state_l0_ssd_fwd, state_l0_ssd_bwd, state_l0_gla_fwd,
                    state_l0_gla_bwd, cache_l1_k, cache_l1_v, c, c_ctx, ada_w, ada_b, norm1, norm2,
                    l0_w_in, l0_conv_w, l0_conv_b, l0_a_log, l0_dt_bias, l0_d_skip, l0_ssd_norm,
                    l0_gate_w2, l0_gate_b, l0_gla_norm, l0_w_out,
                    l1_w_qkv, l1_q_norm, l1_k_norm, l1_sinks, l1_w_out,
                    router_w, router_b, exp_w_gate, exp_b_gate, exp_w_up, exp_b_up, exp_w_down, exp_b_down)
```

```python
import functools
import math

import numpy as np
import jax
import jax.numpy as jnp
from jax import lax
from jax.experimental import pallas as pl
from jax.experimental.pallas import tpu as pltpu

F32 = jnp.float32
BF16 = jnp.bfloat16
I32 = jnp.int32
HI = lax.Precision.HIGHEST

D = 1024
EPS = 1e-6
N_ADA = 6
SSD_HEADS = 16
SSD_HEAD_DIM = 64
SSD_INNER = 1024
SSD_STATE = 128
SSD_GROUPS = 2
SSD_CONV = 5
SSD_CHUNK = 128
GLA_HEADS = 4
GLA_KEY_DIM = 128
GLA_VAL_DIM = 256
GLA_RANK = 16
GLA_TAU = 16.0
GLA_BLOCK = 64
ATT_HEADS = 16
ATT_KV = 4
ATT_HD = 64
ATT_BLOCK = 128
WINDOW = 128
GRID_W = 64
ROPE_THETA = 10000.0
N_EXPERTS = 32
TOP_K = 4
EXPERT_FF = 1024
SWIGLU_LIMIT = 7.0
SWIGLU_ALPHA = 1.702
MOE_BLOCK = 256
TOK_TILE = 256
LANES = 128
NEG = -1e30

PJ_Z, PJ_V, PJ_OG, PJ_XBC, PJ_Q, PJ_K = 0, 1024, 2048, 3072, 4608, 5120
PJ_W = 5632
VMEM_LIMIT = 48 * 1024 * 1024


def _cp(sem, vmem=VMEM_LIMIT):
    return pltpu.CompilerParams(dimension_semantics=sem, vmem_limit_bytes=vmem)


class Layout:
    def __init__(self, n_prompt, prompt_len, n_sample, sample_len):
        self.n_prompt, self.prompt_len = n_prompt, prompt_len
        self.n_sample, self.sample_len = n_sample, sample_len
        self.p_tok = n_prompt * prompt_len
        self.n_tok = self.p_tok + n_sample * sample_len
        self.seqs = [(i * prompt_len, prompt_len) for i in range(n_prompt)]
        self.seqs += [(self.p_tok + i * sample_len, sample_len) for i in range(n_sample)]
        self.n_seq = len(self.seqs)

    def mod_row(self, start):
        return jnp.where(start < self.p_tok, 0, 1 + (start - self.p_tok) // self.sample_len)

def _sigmoid(x):
    return 1.0 / (1.0 + jnp.exp(-x))


def _silu(x):
    return x * _sigmoid(x)


def _softplus(x):
    return jnp.maximum(x, 0.0) + jnp.log(1.0 + jnp.exp(-jnp.abs(x)))


def _modnorm(x, g, sc, sh):
    ms = jnp.mean(x * x, axis=-1, keepdims=True)
    return (x * lax.rsqrt(ms + EPS) * g) * (1.0 + sc) + sh


def _dot(a, b, **kw):
    return jnp.dot(a, b, preferred_element_type=F32, **kw)


def _dot_nt(a, b):
    return lax.dot_general(a, b, (((1,), (1,)), ((), ())), preferred_element_type=F32)


def _dot_tn(a, b):
    return lax.dot_general(a, b, (((0,), (0,)), ((), ())), preferred_element_type=F32)


def _split(x, n):
    parts = []
    for _ in range(n):
        p = x.astype(BF16)
        parts.append(p)
        x = x - p.astype(F32)
    return parts


def _dot_sel(sel, x):
    sel = sel.astype(BF16)
    return sum(_dot(sel, p) for p in _split(x, 3))


def _dot_sel_r(x, sel):
    sel = sel.astype(BF16)
    return sum(_dot(p, sel) for p in _split(x, 3))


def _dot_hilo(x, w_hi, w_lo):
    x_hi, x_lo = _split(x, 2)
    return _dot(x_hi, w_hi) + _dot(x_lo, w_hi) + _dot(x_hi, w_lo)


def _hilo(w):
    hi = w.astype(BF16)
    return jnp.stack([hi, (w - hi.astype(F32)).astype(BF16)])


U32 = jnp.uint32
ROW_WORDS = D // 2
_HI_MASK = 0xFFFF0000


def _pack_rows(x):
    lo = lax.bitcast_convert_type(x[:, :ROW_WORDS], U32) >> 16
    hi = lax.bitcast_convert_type(x[:, ROW_WORDS:], U32) & jnp.uint32(_HI_MASK)
    return lo | hi


def _unpack_rows(u):
    lo = lax.bitcast_convert_type(u << 16, F32)
    hi = lax.bitcast_convert_type(u & jnp.uint32(_HI_MASK), F32)
    return jnp.concatenate([lo, hi], axis=1)


def _ada_kernel(c_ref, w_ref, b_ref, o_ref):
    o_ref[0] = _dot(_silu(c_ref[...]), w_ref[0], precision=HI) + b_ref[0]


def ada_table(cond8, ada_w, ada_b):
    depth, _, n = ada_w.shape
    tn = 1536
    return pl.pallas_call(
        _ada_kernel, grid=(depth, n // tn),
        in_specs=[pl.BlockSpec((8, D), lambda l, j: (0, 0)),
                  pl.BlockSpec((1, D, tn), lambda l, j: (l, 0, j)),
                  pl.BlockSpec((1, 1, tn), lambda l, j: (l, 0, j))],
        out_specs=pl.BlockSpec((1, 8, tn), lambda l, j: (l, 0, j)),
        out_shape=jax.ShapeDtypeStruct((depth, 8, n), F32),
        compiler_params=_cp(("arbitrary", "arbitrary")), name="ada_table",
    )(cond8, ada_w, ada_b.reshape(depth, 1, n))


def _proj_kernel(x_ref, g_ref, sc_ref, sh_ref, w_ref, ws_ref, o_ref, os_ref, h_scr):
    @pl.when(pl.program_id(1) == 0)
    def _():
        h = _modnorm(x_ref[...], g_ref[...], sc_ref[...], sh_ref[...])
        h_scr[...] = h.astype(BF16)
        os_ref[...] = _dot_hilo(h, ws_ref[0], ws_ref[1])

    o_ref[...] = _dot(h_scr[...], w_ref[...]).astype(o_ref.dtype)


def norm_proj(lay, x, g, sc, sh, w, w_small, tm, tn, out_dtype):
    n_tok = x.shape[0]
    n = w.shape[1]
    ns = w_small.shape[-1]
    mrow = lambda i, j: (lay.mod_row(i * tm), 0, 0)
    return pl.pallas_call(
        _proj_kernel, grid=(n_tok // tm, n // tn),
        in_specs=[pl.BlockSpec((tm, D), lambda i, j: (i, 0)),
                  pl.BlockSpec((1, D), lambda i, j: (0, 0)),
                  pl.BlockSpec((None, 1, D), mrow),
                  pl.BlockSpec((None, 1, D), mrow),
                  pl.BlockSpec((D, tn), lambda i, j: (0, j)),
                  pl.BlockSpec((2, D, ns), lambda i, j: (0, 0, 0))],
        out_specs=[pl.BlockSpec((tm, tn), lambda i, j: (i, j)),
                   pl.BlockSpec((tm, ns), lambda i, j: (i, 0))],
        out_shape=[jax.ShapeDtypeStruct((n_tok, n), out_dtype),
                   jax.ShapeDtypeStruct((n_tok, ns), F32)],
        scratch_shapes=[pltpu.VMEM((tm, D), BF16)],
        compiler_params=_cp(("arbitrary", "arbitrary")), name="norm_proj",
    )(x, g.reshape(1, D), sc, sh, w, w_small)


CONV_HALO = 16


def _ssd_load(d, c, xc, dtg_ref, dtgT_ref, S):
    q = SSD_CHUNK
    nh = SSD_HEADS
    rows = pl.ds(pl.multiple_of(c * q, q), q)
    return (rows, xc[rows, 0:SSD_INNER], xc[rows, SSD_INNER:SSD_INNER + 2 * SSD_GROUPS * SSD_STATE],
            dtg_ref[rows, nh * d:nh * d + nh], dtgT_ref[c, nh * d:nh * d + nh, :], S[...])


def _ssd_chunk(d, loaded, alog_ref, alogT_ref, dtb_ref, dtbT_ref):
    q = SSD_CHUNK
    nh = SSD_HEADS
    _, xs, bc, dtg, dtgT, s = loaded
    xs = xs.astype(F32)
    dt = _softplus(dtg + dtb_ref[d:d + 1, :])
    dtT = _softplus(dtgT + dtbT_ref[:, d:d + 1])
    ad = dt * (-jnp.exp(alog_ref[d:d + 1, :]))
    adT = dtT * (-jnp.exp(alogT_ref[:, d:d + 1]))
    row = lax.broadcasted_iota(I32, (q, q), 0)
    col = lax.broadcasted_iota(I32, (q, q), 1)
    if d == 0:
        e = _dot_sel(col <= row, ad)
        eT = _dot_sel_r(adT, row <= col)
        tot = e[q - 1:q, :]
        mask = row >= col
        fq = jnp.exp(e)
        fk = jnp.exp(tot - e)
    else:
        e = _dot_sel(col < row, ad)
        eT = _dot_sel_r(adT, row < col)
        tot = jnp.sum(ad, axis=0, keepdims=True)
        mask = col >= row
        fq = jnp.exp(tot - e)
        fk = jnp.exp(e)
    dec = jnp.exp(tot)
    lo = lax.broadcasted_iota(I32, (q, LANES), 1) < SSD_HEAD_DIM
    lo1 = lax.broadcasted_iota(I32, (1, LANES), 1) < SSD_HEAD_DIM

    def colpat(arr, a):
        return jnp.where(lo, arr[:, a:a + 1], arr[:, a + 1:a + 2])

    rep = (nh // SSD_GROUPS) // 2
    ys, s_new = [], []
    for g in range(SSD_GROUPS):
        bg = bc[:, SSD_STATE * g:SSD_STATE * (g + 1)]
        cg = bc[:, SSD_STATE * (SSD_GROUPS + g):SSD_STATE * (SSD_GROUPS + g + 1)]
        gmat = _dot_nt(cg, bg)
        for j in range(rep * g, rep * (g + 1)):
            a = 2 * j
            sl = slice(LANES * j, LANES * (j + 1))
            parts = []
            for hh in (a, a + 1):
                if d == 0:
                    diff = e[:, hh:hh + 1] - eT[hh:hh + 1, :]
                else:
                    diff = eT[hh:hh + 1, :] - e[:, hh:hh + 1]
                parts.append((gmat * jnp.exp(jnp.where(mask, diff, NEG))).astype(BF16))
            lhs = jnp.concatenate(parts, axis=1)
            xdt = xs[:, sl] * colpat(dt, a)
            rhs = jnp.concatenate([jnp.where(lo, xdt, 0.0), jnp.where(lo, 0.0, xdt)], axis=0)
            sj = s[:, sl]
            ys.append(_dot(lhs, rhs.astype(BF16)) + _dot(cg, sj.astype(BF16)) * colpat(fq, a))
            xk = (xdt * colpat(fk, a)).astype(BF16)
            decp = jnp.where(lo1, dec[:, a:a + 1], dec[:, a + 1:a + 2])
            s_new.append(sj * decp + _dot_tn(bg, xk))
    return jnp.concatenate(ys, axis=1), jnp.concatenate(s_new, axis=1)


def _ssd_seq_kernel(xbc_ref, z_ref, dtg_ref, dtgT_ref, s0f_ref, s0b_ref, cw_ref, cb_ref,
                    alog_ref, alogT_ref, dtb_ref, dtbT_ref, dskip_ref, nrm_ref,
                    y_ref, sf_ref, sb_ref, xc, ext, Sf, Sb, yf, yb):
    ln = xbc_ref.shape[0]
    q = SSD_CHUNK
    nc = ln // q
    h = CONV_HALO
    pad = SSD_CONV // 2

    def conv_body(c, carry):
        r0 = pl.multiple_of(c * q, q)
        prev = xbc_ref[pl.ds(pl.multiple_of(jnp.maximum(r0 - h, 0), h), h), :].astype(F32)
        nxt = xbc_ref[pl.ds(pl.multiple_of(jnp.minimum(r0 + q, ln - h), h), h), :].astype(F32)
        ext[0:h, :] = jnp.where(c > 0, prev, 0.0)
        ext[h:h + q, :] = xbc_ref[pl.ds(r0, q), :].astype(F32)
        ext[h + q:h + q + h, :] = jnp.where(c < nc - 1, nxt, 0.0)
        acc = jnp.broadcast_to(cb_ref[...], (q, cb_ref.shape[1]))
        for k in range(SSD_CONV):
            acc = acc + cw_ref[k:k + 1, :] * ext[h - pad + k:h - pad + k + q, :]
        xc[pl.ds(r0, q), :] = _silu(acc).astype(xc.dtype)
        return carry
    lax.fori_loop(0, nc, conv_body, 0)

    Sf[...] = s0f_ref[0].T
    Sb[...] = s0b_ref[0].T
    params = (alog_ref, alogT_ref, dtb_ref, dtbT_ref)

    def scan_body(c, carry):
        lf = _ssd_load(0, c, xc, dtg_ref, dtgT_ref, Sf)
        lb = _ssd_load(1, nc - 1 - c, xc, dtg_ref, dtgT_ref, Sb)
        y_f, s_f = _ssd_chunk(0, lf, *params)
        y_b, s_b = _ssd_chunk(1, lb, *params)
        yf[lf[0], :] = y_f
        yb[lb[0], :] = y_b
        Sf[...] = s_f
        Sb[...] = s_b
        return carry
    lax.fori_loop(0, nc, scan_body, 0)
    sf_ref[0] = Sf[...].T
    sb_ref[0] = Sb[...].T

    def out_body(c, carry):
        rows = pl.ds(pl.multiple_of(c * q, q), q)
        ytot = yf[rows, :] + yb[rows, :] + dskip_ref[...] * xc[rows, 0:SSD_INNER].astype(F32)
        yg = ytot * _silu(z_ref[rows, :].astype(F32))
        ms = jnp.mean(yg * yg, axis=-1, keepdims=True)
        y_ref[rows, :] = (yg * lax.rsqrt(ms + EPS) * nrm_ref[...]).astype(y_ref.dtype)
        return carry
    lax.fori_loop(0, nc, out_body, 0)


def _ssd_call(n_seq, ln, blk0, proj, small, smallT3, s0f, s0b, params):
    q = SSD_CHUNK
    hp = SSD_HEADS * SSD_HEAD_DIM
    cw = SSD_INNER + 2 * SSD_GROUPS * SSD_STATE
    nc = ln // q
    assert PJ_XBC % cw == 0 and PJ_Z % SSD_INNER == 0
    tok = lambda w, cb: pl.BlockSpec((ln, w), lambda b: (blk0 + b, cb))
    seq3 = pl.BlockSpec((1, hp, SSD_STATE), lambda b: (b, 0, 0))
    full = lambda a: pl.BlockSpec(a.shape, lambda b: (0,) * a.ndim)
    return pl.pallas_call(
        _ssd_seq_kernel, grid=(n_seq,),
        in_specs=[tok(cw, PJ_XBC // cw), tok(SSD_INNER, PJ_Z // SSD_INNER), tok(LANES, 0),
                  pl.BlockSpec((nc, 2 * SSD_HEADS, q), lambda b: (blk0 + b, 0, 0)), seq3, seq3]
        + [full(a) for a in params],
        out_specs=[pl.BlockSpec((ln, hp), lambda b: (b, 0)), seq3, seq3],
        out_shape=[jax.ShapeDtypeStruct((n_seq * ln, hp), BF16),
                   jax.ShapeDtypeStruct((n_seq, hp, SSD_STATE), F32),
                   jax.ShapeDtypeStruct((n_seq, hp, SSD_STATE), F32)],
        scratch_shapes=[pltpu.VMEM((ln, cw), BF16), pltpu.VMEM((q + 2 * CONV_HALO, cw), F32),
                        pltpu.VMEM((SSD_STATE, hp), F32), pltpu.VMEM((SSD_STATE, hp), F32),
                        pltpu.VMEM((ln, hp), F32), pltpu.VMEM((ln, hp), F32)],
        compiler_params=_cp(("arbitrary",)), name="ssd_seq",
    )(proj, proj, small, smallT3, s0f, s0b, *params)


def _gla_load(d, c, q_ref, k_ref, v_ref, glr_ref, S):
    t = GLA_BLOCK
    rows = pl.ds(pl.multiple_of(c * t, t), t)
    c0 = 2 * SSD_HEADS + GLA_RANK * d
    return (rows, q_ref[rows, :], k_ref[rows, :], v_ref[rows, :], glr_ref[rows, c0:c0 + GLA_RANK],
            [S[h] for h in range(GLA_HEADS)])


def _gla_block(d, loaded, w2_ref, gb_ref):
    t = GLA_BLOCK
    dk, dv = GLA_KEY_DIM, GLA_VAL_DIM
    _, q, k, v, glr, states = loaded
    gp = _dot_hilo(glr, w2_ref[0, d], w2_ref[1, d]) + gb_ref[d:d + 1, :]
    la = -_softplus(-gp) * (1.0 / GLA_TAU)
    row = lax.broadcasted_iota(I32, (t, t), 0)
    col = lax.broadcasted_iota(I32, (t, t), 1)
    mid = t // 2 - 1
    if d == 0:
        e = _dot_sel(col <= row, la)
        tot = e[t - 1:t, :]
        r = e[mid:mid + 1, :]
        fqi, fki = jnp.exp(e - r), jnp.exp(r - e)
        fq, fk = jnp.exp(e), jnp.exp(tot - e)
        mask = row >= col
    else:
        e = _dot_sel(col < row, la)
        tot = e[t - 1:t, :] + la[t - 1:t, :]
        r = e[mid:mid + 1, :]
        fqi, fki = jnp.exp(r - e), jnp.exp(e - r)
        fq, fk = jnp.exp(tot - e), jnp.exp(e)
        mask = col >= row
    dec = jnp.exp(tot)
    qf = q.astype(F32) * (dk ** -0.5)
    kf = k.astype(F32)
    outs, new_states = [], []
    for h in range(GLA_HEADS):
        sl = slice(dk * h, dk * (h + 1))
        qh, kh = qf[:, sl], kf[:, sl]
        sc = _dot_nt((qh * fqi[:, sl]).astype(BF16), (kh * fki[:, sl]).astype(BF16))
        sc = jnp.where(mask, sc, 0.0)
        vh = v[:, dv * h:dv * (h + 1)]
        st = states[h]
        outs.append(_dot(sc.astype(BF16), vh) + _dot_nt((qh * fq[:, sl]).astype(BF16), st.astype(BF16)))
        new_states.append(st * dec[:, sl] + _dot_tn(vh, (kh * fk[:, sl]).astype(BF16)))
    return jnp.concatenate(outs, axis=1), new_states


def _gla_seq_kernel(q_ref, k_ref, v_ref, og_ref, glr_ref, s0f_ref, s0b_ref, w2_ref, gb_ref, nrm_ref,
                    o_ref, sf_ref, sb_ref, Sf, Sb, of, ob):
    ln = q_ref.shape[0]
    t = GLA_BLOCK
    nc = ln // t
    dv = GLA_VAL_DIM
    for h in range(GLA_HEADS):
        Sf[h] = s0f_ref[0, h].T
        Sb[h] = s0b_ref[0, h].T
    ins = (q_ref, k_ref, v_ref, glr_ref)

    def scan_body(c, carry):
        lf = _gla_load(0, c, *ins, Sf)
        lb = _gla_load(1, nc - 1 - c, *ins, Sb)
        o_f, s_f = _gla_block(0, lf, w2_ref, gb_ref)
        o_b, s_b = _gla_block(1, lb, w2_ref, gb_ref)
        of[lf[0], :] = o_f
        ob[lb[0], :] = o_b
        for h in range(GLA_HEADS):
            Sf[h] = s_f[h]
            Sb[h] = s_b[h]
        return carry
    lax.fori_loop(0, nc, scan_body, 0, unroll=2)
    for h in range(GLA_HEADS):
        sf_ref[0, h] = Sf[h].T
        sb_ref[0, h] = Sb[h].T

    def out_body(c, carry):
        rows = pl.ds(pl.multiple_of(c * t, t), t)
        for h in range(GLA_HEADS):
            vl = slice(dv * h, dv * (h + 1))
            ot = of[rows, vl] + ob[rows, vl]
            ms = jnp.mean(ot * ot, axis=-1, keepdims=True)
            on = ot * lax.rsqrt(ms + EPS) * nrm_ref[...]
            o_ref[rows, vl] = (on * _silu(og_ref[rows, vl].astype(F32))).astype(o_ref.dtype)
        return carry
    lax.fori_loop(0, nc, out_body, 0)


def _gla_call(n_seq, ln, blk0, proj, small, s0f, s0b, params):
    qk_w = GLA_HEADS * GLA_KEY_DIM
    v_w = GLA_HEADS * GLA_VAL_DIM
    tok = lambda w, cb: pl.BlockSpec((ln, w), lambda b: (blk0 + b, cb))
    seq4 = pl.BlockSpec((1, GLA_HEADS, GLA_KEY_DIM, GLA_VAL_DIM), lambda b: (b, 0, 0, 0))
    full = lambda a: pl.BlockSpec(a.shape, lambda b: (0,) * a.ndim)
    st_shape = jax.ShapeDtypeStruct((n_seq, GLA_HEADS, GLA_KEY_DIM, GLA_VAL_DIM), F32)
    return pl.pallas_call(
        _gla_seq_kernel, grid=(n_seq,),
        in_specs=[tok(qk_w, PJ_Q // qk_w), tok(qk_w, PJ_K // qk_w), tok(v_w, PJ_V // v_w),
                  tok(v_w, PJ_OG // v_w), tok(LANES, 0), seq4, seq4] + [full(a) for a in params],
        out_specs=[pl.BlockSpec((ln, v_w), lambda b: (b, 0)), seq4, seq4],
        out_shape=[jax.ShapeDtypeStruct((n_seq * ln, v_w), BF16), st_shape, st_shape],
        scratch_shapes=[pltpu.VMEM((GLA_HEADS, GLA_VAL_DIM, GLA_KEY_DIM), F32),
                        pltpu.VMEM((GLA_HEADS, GLA_VAL_DIM, GLA_KEY_DIM), F32),
                        pltpu.VMEM((ln, v_w), F32), pltpu.VMEM((ln, v_w), F32)],
        compiler_params=_cp(("arbitrary",)), name="gla_seq",
    )(proj, proj, proj, proj, small, s0f, s0b, *params)


def l0_mixers(lay, proj, small, ssd_f0, ssd_b0, gla_f0, gla_b0, conv_w, conv_b, a_log, dt_bias, d_skip,
              ssd_norm, gate_w2, gate_b, gla_norm):
    q = SSD_CHUNK
    hp = SSD_HEADS * SSD_HEAD_DIM
    n_tok = proj.shape[0]
    smallT3 = small.reshape(n_tok // q, q, LANES).swapaxes(1, 2)
    ssd_p = (conv_w, conv_b.reshape(1, -1), a_log, a_log.T, dt_bias, dt_bias.T,
             jnp.repeat(d_skip, SSD_HEAD_DIM).reshape(1, hp), ssd_norm.reshape(1, hp))
    gla_p = (_hilo(gate_w2), gate_b, gla_norm.reshape(1, -1))
    np_, ns = lay.n_prompt, lay.n_sample
    assert lay.p_tok % lay.sample_len == 0
    groups = [(np_, lay.prompt_len, 0, jnp.zeros((np_, hp, SSD_STATE), F32), jnp.zeros((np_, hp, SSD_STATE), F32),
               jnp.zeros((np_,) + gla_f0.shape[1:], F32), jnp.zeros((np_,) + gla_f0.shape[1:], F32)),
              (ns, lay.sample_len, lay.p_tok // lay.sample_len, ssd_f0.reshape(ns, hp, SSD_STATE),
               ssd_b0.reshape(ns, hp, SSD_STATE), gla_f0, gla_b0)]
    ys, os_, states = [], [], None
    dtT = smallT3[:, :2 * SSD_HEADS, :]
    for n, ln, blk0, sf0, sb0, gf0, gb0 in groups:
        y, sf, sb = _ssd_call(n, ln, blk0, proj, small, dtT, sf0, sb0, ssd_p)
        o, gf, gb = _gla_call(n, ln, blk0, proj, small, gf0, gb0, gla_p)
        ys.append(y)
        os_.append(o)
        if states is None:
            states = (sf, sb, gf, gb)
    return jnp.concatenate(ys, axis=0), jnp.concatenate(os_, axis=0), states


def _res_kernel(*refs, ks):
    n = len(ks)
    a_refs, w_ref, x_ref, gate_ref, o_ref = refs[:n], refs[n], refs[n + 1], refs[n + 2], refs[n + 3]
    acc = None
    off = 0
    for a_ref, k in zip(a_refs, ks):
        part = _dot(a_ref[...], w_ref[off:off + k, :])
        acc = part if acc is None else acc + part
        off += k
    o_ref[...] = x_ref[...] + gate_ref[...] * acc


def proj_residual(lay, acts, w, x, gate, tm=512):
    n_tok = x.shape[0]
    ks = tuple(int(a.shape[1]) for a in acts)
    mrow = lambda i: (lay.mod_row(i * tm), 0, 0)
    return pl.pallas_call(
        functools.partial(_res_kernel, ks=ks), grid=(n_tok // tm,),
        in_specs=[pl.BlockSpec((tm, k), lambda i: (i, 0)) for k in ks]
        + [pl.BlockSpec(w.shape, lambda i: (0, 0)),
           pl.BlockSpec((tm, D), lambda i: (i, 0)),
           pl.BlockSpec((None, 1, D), mrow)],
        out_specs=pl.BlockSpec((tm, D), lambda i: (i, 0)),
        out_shape=jax.ShapeDtypeStruct((n_tok, D), F32),
        compiler_params=_cp(("arbitrary",)), name="proj_residual",
    )(*acts, w, x, gate)


def _router_kernel(x_ref, g_ref, sc_ref, sh_ref, rw_ref, rb_ref,
                   h_ref, idx_ref, gate_ref, pos_ref, posT_ref, cnt_ref):
    tm = x_ref.shape[0]
    h = _modnorm(x_ref[...], g_ref[...], sc_ref[...], sh_ref[...])
    h_hi = h.astype(BF16)
    h_ref[...] = h_hi
    h_lo = (h - h_hi.astype(F32)).astype(BF16)
    lg = (_dot(h_hi, rw_ref[0]) + _dot(h_lo, rw_ref[0]) + _dot(h_hi, rw_ref[1])
          + rb_ref[...])
    lane = lax.broadcasted_iota(I32, (tm, LANES), 1).astype(F32)
    vals, ids = [], []
    for _ in range(TOP_K):
        m = jnp.max(lg, axis=1, keepdims=True)
        i = jnp.min(jnp.where(lg == m, lane, float(LANES)), axis=1, keepdims=True)
        vals.append(m)
        ids.append(i)
        lg = jnp.where(lane == i, -jnp.inf, lg)
    ex = [jnp.exp(v - vals[0]) for v in vals]
    den = ex[0] + ex[1] + ex[2] + ex[3]
    sel = jnp.zeros((tm, LANES), F32)
    for i in ids:
        sel = sel + (lane == i).astype(F32)
    row = lax.broadcasted_iota(I32, (tm, tm), 0)
    col = lax.broadcasted_iota(I32, (tm, tm), 1)
    before = _dot((col < row).astype(BF16), sel.astype(BF16))
    n = jnp.sum(sel, axis=0, keepdims=True)
    er = lax.broadcasted_iota(I32, (LANES, LANES), 0)
    ec = lax.broadcasted_iota(I32, (LANES, LANES), 1)
    n_al = jnp.ceil(n * (1.0 / SEG_ALIGN)) * SEG_ALIGN
    offs = _dot(jnp.broadcast_to(n_al, (8, LANES)).astype(BF16), (er < ec).astype(BF16))[0:1, :]
    slot = before + offs
    idx_o = jnp.zeros((tm, LANES), F32)
    gate_o = jnp.zeros((tm, LANES), F32)
    pos_o = jnp.zeros((tm, LANES), F32)
    for k in range(TOP_K):
        p = jnp.sum(jnp.where(lane == ids[k], slot, 0.0), axis=1, keepdims=True)
        idx_o = jnp.where(lane == k, ids[k], idx_o)
        gate_o = jnp.where(lane == k, ex[k] / den, gate_o)
        pos_o = jnp.where(lane == k, p, pos_o)
    idx_ref[...] = idx_o.astype(I32)
    gate_ref[...] = gate_o
    pos_ref[...] = pos_o.astype(I32)
    posT_ref[...] = pos_o.T[0:8, :]
    cnt_ref[0] = jnp.broadcast_to(n, (8, LANES))


def moe_router(lay, x, g, sc, sh, rw, rb):
    n_tok = x.shape[0]
    tm = TOK_TILE
    nt = n_tok // tm
    mrow = lambda i: (lay.mod_row(i * tm), 0, 0)
    tile = lambda w, dt: (pl.BlockSpec((tm, w), lambda i: (i, 0)), jax.ShapeDtypeStruct((n_tok, w), dt))
    outs = [tile(D, BF16), tile(LANES, I32), tile(LANES, F32), tile(LANES, I32),
            (pl.BlockSpec((8, tm), lambda i: (0, i)), jax.ShapeDtypeStruct((8, n_tok), F32)),
            (pl.BlockSpec((1, 8, LANES), lambda i: (i, 0, 0)), jax.ShapeDtypeStruct((nt, 8, LANES), F32))]
    return pl.pallas_call(
        _router_kernel, grid=(nt,),
        in_specs=[pl.BlockSpec((tm, D), lambda i: (i, 0)),
                  pl.BlockSpec((1, D), lambda i: (0, 0)),
                  pl.BlockSpec((None, 1, D), mrow), pl.BlockSpec((None, 1, D), mrow),
                  pl.BlockSpec((2, D, LANES), lambda i: (0, 0, 0)),
                  pl.BlockSpec((1, LANES), lambda i: (0, 0))],
        out_specs=[o[0] for o in outs], out_shape=[o[1] for o in outs],
        compiler_params=_cp(("arbitrary",)), name="moe_router",
    )(x, g.reshape(1, D), sc, sh, rw, rb)


SEG_ALIGN = 8
SEG_BITS = tuple(range(int(math.log2(TOK_TILE)), int(math.log2(SEG_ALIGN)) - 1, -1))
TILE_ROWS = TOK_TILE * TOP_K + N_EXPERTS * SEG_ALIGN


def _pow2_copies(n, src, dst, make_copy, op, bits):
    for b in bits:
        sz = 1 << b
        done = (n >> (b + 1)) << (b + 1)

        @pl.when((n & sz) != 0)
        def _():
            op(make_copy(pl.multiple_of(src + done, SEG_ALIGN), pl.multiple_of(dst + done, SEG_ALIGN), sz))


def _start_segments(i, n_ref, off_ref, dst_ref, make_copy):
    def body(e, carry):
        k = i * N_EXPERTS + e
        _pow2_copies(n_ref[k], off_ref[k], dst_ref[k], make_copy, lambda c: c.start(), SEG_BITS)
        return carry
    lax.fori_loop(0, N_EXPERTS, body, 0)


TAIL_BITS = tuple(range(int(math.log2(MOE_BLOCK)) - 1, int(math.log2(SEG_ALIGN)) - 1, -1))
TILE_BITS = tuple(range(int(math.log2(TILE_ROWS)), int(math.log2(SEG_ALIGN)) - 1, -1))


def _wait_rows(total, make_copy):
    _pow2_copies(total, 0, 0, make_copy, lambda c: c.wait(), TILE_BITS)


def _dispatch_kernel(n_ref, off_ref, dst_ref, tot_ref, tn_ref, td_ref, posT_ref, h_ref, xout_ref,
                     srt, zbuf, sems):
    i = pl.program_id(0)
    last = pl.num_programs(0) - 1
    slot = i % 2
    tm = h_ref.shape[0]
    r = lax.broadcasted_iota(I32, (TILE_ROWS, tm), 0)
    hit = jnp.zeros((TILE_ROWS, tm), jnp.bool_)
    for k in range(TOP_K):
        hit = hit | (r == posT_ref[k:k + 1, :].astype(I32))
    sel = jnp.where(hit, 1.0, 0.0).astype(BF16)
    srt[slot] = _pack_rows(_dot(sel, h_ref[...]))

    def copier(s):
        def make_copy(src, dst, sz):
            return pltpu.make_async_copy(srt.at[s, pl.ds(src, sz)], xout_ref.at[pl.ds(dst, sz)], sems.at[s])
        return make_copy

    _start_segments(i, n_ref, off_ref, dst_ref, copier(slot))

    @pl.when(i > 0)
    def _():
        _wait_rows(tot_ref[jnp.maximum(i - 1, 0)], copier(1 - slot))

    @pl.when(i == last)
    def _():
        _wait_rows(tot_ref[i], copier(slot))
        zbuf[...] = jnp.zeros_like(zbuf)
        sem = sems.at[0]

        def zero_copy(src, dst, sz):
            return pltpu.make_async_copy(zbuf.at[pl.ds(src, sz)], xout_ref.at[pl.ds(dst, sz)], sem)

        nb = xout_ref.shape[0] // MOE_BLOCK
        for op in (lambda c: c.start(), lambda c: c.wait()):
            def body(e, carry):
                _pow2_copies(tn_ref[e], 0, td_ref[e], zero_copy, op, TAIL_BITS)
                return carry
            lax.fori_loop(0, N_EXPERTS, body, 0)

            def unused(b, carry):
                op(zero_copy(0, pl.multiple_of(b * MOE_BLOCK, MOE_BLOCK), MOE_BLOCK))
                return carry
            lax.fori_loop(tn_ref[N_EXPERTS], nb, unused, 0)


def moe_dispatch(n_tab, off_tab, dst_tab, tot_tab, tail_n, tail_dst, posT, h2, n_rows):
    n_tok = h2.shape[0]
    tm = TOK_TILE
    grid_spec = pltpu.PrefetchScalarGridSpec(
        num_scalar_prefetch=6, grid=(n_tok // tm,),
        in_specs=[pl.BlockSpec((8, tm), lambda i, *_: (0, i)),
                  pl.BlockSpec((tm, D), lambda i, *_: (i, 0))],
        out_specs=pl.BlockSpec(memory_space=pl.ANY),
        scratch_shapes=[pltpu.VMEM((2, TILE_ROWS, ROW_WORDS), U32), pltpu.VMEM((MOE_BLOCK, ROW_WORDS), U32),
                        pltpu.SemaphoreType.DMA((2,))])
    return pl.pallas_call(
        _dispatch_kernel, grid_spec=grid_spec,
        out_shape=jax.ShapeDtypeStruct((n_rows, ROW_WORDS), U32),
        compiler_params=_cp(("arbitrary",)), name="moe_dispatch",
    )(n_tab, off_tab, dst_tab, tot_tab, tail_n, tail_dst, posT, h2)


def _combine_kernel(n_ref, off_ref, dst_ref, tot_ref, pos_ref, gate_ref, x_ref, g2_ref, y_ref, o_ref,
                    buf, sems):
    i = pl.program_id(0)
    last = pl.num_programs(0) - 1
    slot = i % 2
    tm = x_ref.shape[0]
    na = TILE_ROWS

    def copier(s):
        def make_copy(src, dst, sz):
            return pltpu.make_async_copy(y_ref.at[pl.ds(dst, sz)], buf.at[s, pl.ds(src, sz)], sems.at[s])
        return make_copy

    def fetch(tile, s):
        buf[s, tm * TOP_K:na, :] = jnp.zeros((na - tm * TOP_K, ROW_WORDS), U32)
        _start_segments(tile, n_ref, off_ref, dst_ref, copier(s))

    @pl.when(i == 0)
    def _():
        fetch(i, slot)

    @pl.when(i < last)
    def _():
        fetch(i + 1, 1 - slot)

    _wait_rows(tot_ref[i], copier(slot))
    lane = lax.broadcasted_iota(I32, (tm, na), 1)
    pw = jnp.zeros((tm, na), F32)
    for k in range(TOP_K):
        pw = pw + jnp.where(lane == pos_ref[:, k:k + 1], gate_ref[:, k:k + 1], 0.0)
    phi = pw.astype(BF16)
    plo = (pw - phi.astype(F32)).astype(BF16)
    yb = _unpack_rows(buf[slot]).astype(BF16)
    o_ref[...] = x_ref[...] + g2_ref[...] * (_dot(phi, yb) + _dot(plo, yb))


def moe_combine(lay, n_tab, off_tab, dst_tab, tot_tab, pos, gates, x, gate2, y_rows):
    n_tok = x.shape[0]
    tm = TOK_TILE
    mrow = lambda i, *_: (lay.mod_row(i * tm), 0, 0)
    grid_spec = pltpu.PrefetchScalarGridSpec(
        num_scalar_prefetch=4, grid=(n_tok // tm,),
        in_specs=[pl.BlockSpec((tm, LANES), lambda i, *_: (i, 0)),
                  pl.BlockSpec((tm, LANES), lambda i, *_: (i, 0)),
                  pl.BlockSpec((tm, D), lambda i, *_: (i, 0)),
                  pl.BlockSpec((None, 1, D), mrow),
                  pl.BlockSpec(memory_space=pl.ANY)],
        out_specs=pl.BlockSpec((tm, D), lambda i, *_: (i, 0)),
        scratch_shapes=[pltpu.VMEM((2, TILE_ROWS, ROW_WORDS), U32), pltpu.SemaphoreType.DMA((2,))])
    return pl.pallas_call(
        _combine_kernel, grid_spec=grid_spec,
        out_shape=jax.ShapeDtypeStruct((n_tok, D), F32),
        compiler_params=_cp(("arbitrary",)), name="moe_combine",
    )(n_tab, off_tab, dst_tab, tot_tab, pos, gates, x, gate2, y_rows)


def _expert_kernel(be_ref, nv_ref, nxt_ref, slot_ref, x_ref, bg_ref, bu_ref, bd_ref, wg_hbm, wu_hbm, wd_hbm,
                   y_ref, wf, sems, *, layer):
    i = pl.program_id(0)
    valid = i < nv_ref[0]
    e = be_ref[i]
    slot = slot_ref[e]
    changed = jnp.logical_or(i == 0, e != be_ref[jnp.maximum(i - 1, 0)])

    def weight_copies(ex, s):
        return [pltpu.make_async_copy(w.at[layer, ex], wf.at[s, k], sems.at[s, k])
                for k, w in enumerate((wg_hbm, wu_hbm, wd_hbm))]

    @pl.when(jnp.logical_and(valid, changed))
    def _():
        @pl.when(i == 0)
        def _():
            for c in weight_copies(e, slot):
                c.start()

        nxt = nxt_ref[e]

        @pl.when(nxt >= 0)
        def _():
            for c in weight_copies(nxt, 1 - slot):
                c.start()

        for c in weight_copies(e, slot):
            c.wait()

    @pl.when(valid)
    def _():
        x = _unpack_rows(x_ref[...])
        gt = jnp.minimum(_dot(x, wf[slot, 0]) + bg_ref[...], SWIGLU_LIMIT)
        up = jnp.clip(_dot(x, wf[slot, 1]) + bu_ref[...], -SWIGLU_LIMIT, SWIGLU_LIMIT)
        act = (up + 1.0) * gt * _sigmoid(SWIGLU_ALPHA * gt)
        y = _dot(act, wf[slot, 2]) + bd_ref[...]
        y_ref[...] = _pack_rows(y.astype(BF16).astype(F32))

    @pl.when(jnp.logical_not(valid))
    def _():
        y_ref[...] = jnp.zeros_like(y_ref)


def moe_experts(layer, blk_expert, n_valid, next_expert, slot, x_rows, w_gate, b_gate, w_up, b_up, w_down,
                b_down):
    n_rows = x_rows.shape[0]
    nb = n_rows // MOE_BLOCK
    depth, ne, _, ff = w_gate.shape
    assert ff == D
    rowblk = lambda i, be, nv, *_: (jnp.maximum(jnp.minimum(i, nv[0] - 1), 0), 0)
    bsel = lambda i, be, *_: (layer, be[i], 0, 0)
    hbm = pl.BlockSpec(memory_space=pl.ANY)
    grid_spec = pltpu.PrefetchScalarGridSpec(
        num_scalar_prefetch=4, grid=(nb,),
        in_specs=[pl.BlockSpec((MOE_BLOCK, ROW_WORDS), rowblk),
                  pl.BlockSpec((None, None, 1, ff), bsel), pl.BlockSpec((None, None, 1, ff), bsel),
                  pl.BlockSpec((None, None, 1, D), bsel), hbm, hbm, hbm],
        out_specs=pl.BlockSpec((MOE_BLOCK, ROW_WORDS), lambda i, *_: (i, 0)),
        scratch_shapes=[pltpu.VMEM((2, 3, D, ff), F32), pltpu.SemaphoreType.DMA((2, 3))])
    return pl.pallas_call(
        functools.partial(_expert_kernel, layer=layer), grid_spec=grid_spec,
        out_shape=jax.ShapeDtypeStruct((n_rows, ROW_WORDS), U32),
        compiler_params=_cp(("arbitrary",)), name="moe_experts",
    )(blk_expert, n_valid, next_expert, slot, x_rows, b_gate.reshape(depth, ne, 1, ff),
      b_up.reshape(depth, ne, 1, ff), b_down.reshape(depth, ne, 1, D), w_gate, w_up, w_down)


def moe_layer(lay, layer, x, g2, sc2, sh2, gate2, router_w, router_b, w_gate, b_gate, w_up, b_up, w_down,
              b_down):
    n_tok = x.shape[0]
    nt = n_tok // TOK_TILE
    rw = jnp.zeros((D, LANES), F32).at[:, :N_EXPERTS].set(router_w)
    rw = _hilo(rw)
    rb = jnp.full((1, LANES), NEG, F32).at[0, :N_EXPERTS].set(router_b)
    h2, _, gates, pos, posT, cnt = moe_router(lay, x, g2, sc2, sh2, rw, rb)
    n_te = cnt[:, 0, :N_EXPERTS].astype(I32)
    n_te = (n_te + SEG_ALIGN - 1) // SEG_ALIGN * SEG_ALIGN
    totals = jnp.sum(n_te, axis=0)
    padded = (totals + MOE_BLOCK - 1) // MOE_BLOCK * MOE_BLOCK
    padded_end = jnp.cumsum(padded)
    pstart = padded_end - padded
    dst = pstart[None, :] + jnp.cumsum(n_te, axis=0) - n_te
    off = jnp.cumsum(n_te, axis=1) - n_te
    n_rows = nt * TILE_ROWS + N_EXPERTS * MOE_BLOCK
    nb = n_rows // MOE_BLOCK
    n_valid = (padded_end[-1] // MOE_BLOCK).astype(I32).reshape(1)
    bstart = jnp.minimum(jnp.arange(nb, dtype=I32), n_valid[0] - 1) * MOE_BLOCK
    blk_expert = jnp.minimum(jnp.sum((bstart[:, None] >= padded_end[None, :]).astype(I32), axis=1),
                             N_EXPERTS - 1).astype(I32)
    tabs = (n_te.reshape(-1).astype(I32), off.reshape(-1).astype(I32), dst.reshape(-1).astype(I32),
            jnp.sum(n_te, axis=1).astype(I32))
    tail_n = jnp.concatenate([(padded - totals).astype(I32), n_valid])
    x_rows = moe_dispatch(*tabs, tail_n, (pstart + totals).astype(I32), posT, h2, n_rows)
    owner = jnp.where(padded > 0, jnp.arange(N_EXPERTS, dtype=I32), N_EXPERTS)
    later = jnp.concatenate([lax.cummin(owner, axis=0, reverse=True)[1:], jnp.full((1,), N_EXPERTS, I32)])
    next_expert = jnp.where(later < N_EXPERTS, later, -1).astype(I32)
    slot = ((jnp.cumsum((padded > 0).astype(I32)) - 1) % 2).astype(I32)
    y_rows = moe_experts(layer, blk_expert, n_valid, next_expert, slot, x_rows, w_gate, b_gate, w_up, b_up,
                         w_down, b_down)
    return moe_combine(lay, *tabs, pos, gates, x, gate2, y_rows)


QKV_TN = 256
N_QK_TILES = (ATT_HEADS + ATT_KV) * ATT_HD // QKV_TN


def _qkv_kernel(x_ref, g_ref, sc_ref, sh_ref, w_ref, nw_ref, cos_ref, sin_ref, o_ref, *, p_tok):
    i = pl.program_id(0)
    tm = x_ref.shape[0]
    h = _modnorm(x_ref[...], g_ref[...], sc_ref[...], sh_ref[...]).astype(BF16)
    r = lax.broadcasted_iota(I32, (QKV_TN, QKV_TN), 0) // ATT_HD
    c = lax.broadcasted_iota(I32, (QKV_TN, QKV_TN), 1) // ATT_HD
    head_mean = jnp.where(r == c, 1.0 / ATT_HD, 0.0).astype(BF16)
    lane = lax.broadcasted_iota(I32, (tm, QKV_TN), 1)
    half = ATT_HD // 4
    first = (lane % (2 * half)) < half
    for j in range(w_ref.shape[1] // QKV_TN):
        cols = slice(QKV_TN * j, QKV_TN * (j + 1))
        acc = _dot(h, w_ref[:, cols])
        if j >= N_QK_TILES:
            o_ref[:, cols] = acc
            continue
        ms = _dot((acc * acc).astype(BF16), head_mean)
        qn = acc * lax.rsqrt(ms + EPS) * nw_ref[j]

        @pl.when(i * tm < p_tok)
        def _():
            o_ref[:, cols] = qn

        @pl.when(i * tm >= p_tok)
        def _():
            swapped = jnp.where(first, pltpu.roll(qn, QKV_TN - half, 1), pltpu.roll(qn, half, 1))
            o_ref[:, cols] = qn * cos_ref[...] + swapped * sin_ref[...]


def _rope_tables(sample_len):
    pos = np.arange(sample_len)
    half = ATT_HD // 4
    inv = (ROPE_THETA ** (-np.arange(half, dtype=np.float32) / half)).astype(np.float32)
    ang_r = (pos // GRID_W).astype(np.float32)[:, None] * inv[None, :]
    ang_c = (pos % GRID_W).astype(np.float32)[:, None] * inv[None, :]
    cos = np.concatenate([np.cos(ang_r)] * 2 + [np.cos(ang_c)] * 2, axis=1)
    sin = np.concatenate([-np.sin(ang_r), np.sin(ang_r), -np.sin(ang_c), np.sin(ang_c)], axis=1)
    rep = QKV_TN // ATT_HD
    return (jnp.asarray(np.tile(cos, (1, rep)), F32), jnp.asarray(np.tile(sin, (1, rep)), F32))


def qkv_proj(lay, x, g, sc, sh, w, q_norm, k_norm, tm=256):
    n_tok = x.shape[0]
    n = w.shape[1]
    nq = ATT_HEADS * ATT_HD // QKV_TN
    rep = QKV_TN // ATT_HD
    nw = jnp.concatenate([jnp.tile(jnp.tile(q_norm, rep)[None, :], (nq, 1)),
                          jnp.tile(jnp.tile(k_norm, rep)[None, :], (n // QKV_TN - nq, 1))], axis=0)
    cos, sin = _rope_tables(lay.sample_len)
    mrow = lambda i: (lay.mod_row(i * tm), 0, 0)
    rrow = lambda i: (jnp.where(i * tm < lay.p_tok, 0, ((i * tm - lay.p_tok) % lay.sample_len) // tm), 0)
    nt = n // QKV_TN
    return pl.pallas_call(
        functools.partial(_qkv_kernel, p_tok=lay.p_tok), grid=(n_tok // tm,),
        in_specs=[pl.BlockSpec((tm, D), lambda i: (i, 0)),
                  pl.BlockSpec((1, D), lambda i: (0, 0)),
                  pl.BlockSpec((None, 1, D), mrow), pl.BlockSpec((None, 1, D), mrow),
                  pl.BlockSpec((D, n), lambda i: (0, 0)),
                  pl.BlockSpec((nt, 1, QKV_TN), lambda i: (0, 0, 0)),
                  pl.BlockSpec((tm, QKV_TN), rrow), pl.BlockSpec((tm, QKV_TN), rrow)],
        out_specs=pl.BlockSpec((tm, n), lambda i: (i, 0)),
        out_shape=jax.ShapeDtypeStruct((n_tok, n), F32),
        compiler_params=_cp(("arbitrary",)), name="qkv_proj",
    )(x, g.reshape(1, D), sc, sh, w, nw.reshape(nt, 1, QKV_TN), cos, sin)


def _dup_group(x, g):
    blk = x[:, LANES * (g // 2):LANES * (g // 2 + 1)]
    if g % 2 == 1:
        blk = pltpu.roll(blk, ATT_HD, 1)
    lo = lax.broadcasted_iota(I32, blk.shape, 1) < ATT_HD
    low = jnp.where(lo, blk, 0.0)
    return low + pltpu.roll(low, ATT_HD, 1)


def _attend(q_ref, k_all, v_all, mask, sink_ref, o_ref):
    nq = q_ref.shape[0]
    lo = lax.broadcasted_iota(I32, (nq, LANES), 1) < ATT_HD
    grp = ATT_HEADS // ATT_KV
    for g in range(ATT_KV):
        k2 = _dup_group(k_all, g).astype(BF16)
        v2 = _dup_group(v_all, g).astype(BF16)
        for jp in range(grp // 2):
            j = g * (grp // 2) + jp
            qp = q_ref[:, LANES * j:LANES * (j + 1)] * (ATT_HD ** -0.5)
            outs = []
            for half in range(2):
                qh = jnp.where(lo, qp, 0.0) if half == 0 else jnp.where(lo, 0.0, qp)
                s = _dot_nt(qh.astype(BF16), k2)
                if mask is not None:
                    s = jnp.where(mask, s, NEG)
                sink = sink_ref[2 * j + half]
                m = jnp.maximum(jnp.max(s, axis=1, keepdims=True), sink)
                p = jnp.exp(s - m)
                den = jnp.sum(p, axis=1, keepdims=True) + jnp.exp(sink - m)
                outs.append(_dot(p.astype(BF16), v2) / den)
            o_ref[:, LANES * j:LANES * (j + 1)] = jnp.where(lo, outs[0], outs[1]).astype(o_ref.dtype)


def _attn_ctx_kernel(sink_ref, q_ref, k_ref, v_ref, o_ref):
    _attend(q_ref, k_ref[...], v_ref[...], None, sink_ref, o_ref)


def attn_context(lay, qkv, sinks):
    qw = ATT_HEADS * ATT_HD
    kw = ATT_KV * ATT_HD
    ln = lay.prompt_len
    grid_spec = pltpu.PrefetchScalarGridSpec(
        num_scalar_prefetch=0, grid=(lay.n_prompt,),
        in_specs=[pl.BlockSpec(memory_space=pltpu.SMEM),
                  pl.BlockSpec((ln, qw), lambda b: (b, 0)),
                  pl.BlockSpec((ln, kw), lambda b: (b, qw // kw)),
                  pl.BlockSpec((ln, kw), lambda b: (b, qw // kw + 1))],
        out_specs=pl.BlockSpec((ln, qw), lambda b: (b, 0)))
    return pl.pallas_call(
        _attn_ctx_kernel, grid_spec=grid_spec,
        out_shape=jax.ShapeDtypeStruct((lay.p_tok, qw), BF16),
        compiler_params=_cp(("arbitrary",)), name="attn_context",
    )(sinks, qkv, qkv, qkv)


def _attn_lat_kernel(sink_ref, q_ref, kp_ref, kc_ref, kn_ref, vp_ref, vc_ref, vn_ref, ck_ref, cv_ref, o_ref,
                     *, nblk):
    i = pl.program_id(1)
    bq = ATT_BLOCK
    nctx = ck_ref.shape[1]
    k_all = jnp.concatenate([kp_ref[...], kc_ref[...], kn_ref[...], ck_ref[0]], axis=0)
    v_all = jnp.concatenate([vp_ref[...], vc_ref[...], vn_ref[...], cv_ref[0]], axis=0)
    ns = 3 * bq + nctx
    r = lax.broadcasted_iota(I32, (bq, ns), 0)
    c = lax.broadcasted_iota(I32, (bq, ns), 1)
    rel = c - r
    first_key = jnp.where(i > 0, 0, bq)
    end_key = jnp.where(i < nblk - 1, 3 * bq, 2 * bq)
    band = (rel >= bq - WINDOW) & (rel <= bq + WINDOW) & (c >= first_key) & (c < end_key)
    mask = band | (c >= 3 * bq)
    _attend(q_ref, k_all, v_all, mask, sink_ref, o_ref)


def attn_latent(lay, qkv, cache_k, cache_v, sinks):
    qw = ATT_HEADS * ATT_HD
    kw = ATT_KV * ATT_HD
    bq = ATT_BLOCK
    nblk = lay.sample_len // bq
    b0 = lay.p_tok // bq
    nctx = cache_k.shape[1]
    rb = lambda b, i: b0 + b * nblk + i
    kspec = lambda cb, sh: pl.BlockSpec(
        (bq, kw), lambda b, i: (b0 + b * nblk + jnp.clip(i + sh, 0, nblk - 1), cb))
    kc, vc = qw // kw, qw // kw + 1
    grid_spec = pltpu.PrefetchScalarGridSpec(
        num_scalar_prefetch=0, grid=(lay.n_sample, nblk),
        in_specs=[pl.BlockSpec(memory_space=pltpu.SMEM),
                  pl.BlockSpec((bq, qw), lambda b, i: (rb(b, i), 0)),
                  kspec(kc, -1), kspec(kc, 0), kspec(kc, 1),
                  kspec(vc, -1), kspec(vc, 0), kspec(vc, 1),
                  pl.BlockSpec((1, nctx, kw), lambda b, i: (b, 0, 0)),
                  pl.BlockSpec((1, nctx, kw), lambda b, i: (b, 0, 0))],
        out_specs=pl.BlockSpec((bq, qw), lambda b, i: (b * nblk + i, 0)))
    return pl.pallas_call(
        functools.partial(_attn_lat_kernel, nblk=nblk), grid_spec=grid_spec,
        out_shape=jax.ShapeDtypeStruct((lay.n_sample * lay.sample_len, qw), BF16),
        compiler_params=_cp(("arbitrary", "arbitrary")), name="attn_latent",
    )(sinks, qkv, qkv, qkv, qkv, qkv, qkv, qkv,
      cache_k.reshape(lay.n_sample, nctx, kw), cache_v.reshape(lay.n_sample, nctx, kw))


def _forward(lay, x_prompt, x_sample, state_l0_ssd_fwd, state_l0_ssd_bwd, state_l0_gla_fwd, state_l0_gla_bwd,
             cache_l1_k, cache_l1_v, c, c_ctx, ada_w, ada_b, norm1, norm2,
             l0_w_in, l0_conv_w, l0_conv_b, l0_a_log, l0_dt_bias, l0_d_skip, l0_ssd_norm,
             l0_gate_w2, l0_gate_b, l0_gla_norm, l0_w_out,
             l1_w_qkv, l1_q_norm, l1_k_norm, l1_sinks, l1_w_out,
             router_w, router_b, exp_w_gate, exp_b_gate, exp_w_up, exp_b_up, exp_w_down, exp_b_down):
    np_, ns = lay.n_prompt, lay.n_sample
    x = jnp.concatenate([x_prompt.reshape(-1, D), x_sample.reshape(-1, D)], axis=0)
    cond8 = jnp.zeros((8, D), F32).at[0].set(c_ctx).at[1:1 + ns].set(c)
    mod = ada_table(cond8, ada_w, ada_b)
    mods = [[mod[l, :, p * D:(p + 1) * D].reshape(8, 1, D) for p in range(N_ADA)] for l in range(2)]

    def moe(l, xx):
        return moe_layer(lay, l, xx, norm2[l], mods[l][4], mods[l][3], mods[l][5], router_w[l], router_b[l],
                         exp_w_gate, exp_b_gate, exp_w_up, exp_b_up, exp_w_down, exp_b_down)

    sp = np.cumsum((SSD_INNER, SSD_INNER + 2 * SSD_GROUPS * SSD_STATE, 2 * SSD_HEADS,
                    GLA_HEADS * GLA_KEY_DIM, GLA_HEADS * GLA_KEY_DIM,
                    GLA_HEADS * GLA_VAL_DIM, GLA_HEADS * GLA_VAL_DIM, 2 * GLA_RANK))
    cols = lambda a, b: l0_w_in[:, a:b]
    w_main = jnp.concatenate([cols(0, sp[0]), cols(sp[4], sp[5]), cols(sp[5], sp[6]), cols(sp[0], sp[1]),
                              cols(sp[2], sp[3]), cols(sp[3], sp[4])], axis=1).astype(BF16)
    w_small = jnp.concatenate([cols(sp[1], sp[2]), cols(sp[6], sp[7]),
                               jnp.zeros((D, LANES - 2 * SSD_HEADS - 2 * GLA_RANK), F32)], axis=1)
    proj, small = norm_proj(lay, x, norm1[0], mods[0][1], mods[0][0], w_main, _hilo(w_small), 512, PJ_W // 2,
                             BF16)
    y_n, o_n, (ssd_f, ssd_b, gla_f, gla_b) = l0_mixers(
        lay, proj, small, state_l0_ssd_fwd, state_l0_ssd_bwd, state_l0_gla_fwd, state_l0_gla_bwd,
        l0_conv_w, l0_conv_b, l0_a_log, l0_dt_bias, l0_d_skip, l0_ssd_norm,
        l0_gate_w2, l0_gate_b, l0_gla_norm)
    x = proj_residual(lay, [y_n, o_n], l0_w_out.astype(BF16), x, mods[0][2])
    x = moe(0, x)

    qkv = qkv_proj(lay, x, norm1[1], mods[1][1], mods[1][0], l1_w_qkv.astype(BF16), l1_q_norm, l1_k_norm)
    o_ctx = attn_context(lay, qkv, l1_sinks)
    o_lat = attn_latent(lay, qkv, cache_l1_k, cache_l1_v, l1_sinks)
    o = jnp.concatenate([o_ctx, o_lat], axis=0)
    x = proj_residual(lay, [o], l1_w_out.astype(BF16), x, mods[1][2])
    x = moe(1, x)

    qw = ATT_HEADS * ATT_HD
    kw = ATT_KV * ATT_HD
    return (x[:lay.p_tok].reshape(x_prompt.shape), x[lay.p_tok:].reshape(x_sample.shape),
            ssd_f[:np_].reshape(np_, SSD_HEADS, SSD_HEAD_DIM, SSD_STATE),
            ssd_b[:np_].reshape(np_, SSD_HEADS, SSD_HEAD_DIM, SSD_STATE),
            gla_f[:np_], gla_b[:np_],
            qkv[:lay.p_tok, qw:qw + kw].reshape(np_, lay.prompt_len, ATT_KV, ATT_HD),
            qkv[:lay.p_tok, qw + kw:].reshape(np_, lay.prompt_len, ATT_KV, ATT_HD))


def kernel(x_prompt, x_sample, state_l0_ssd_fwd, state_l0_ssd_bwd, state_l0_gla_fwd, state_l0_gla_bwd, cache_l1_k, cache_l1_v, c, c_ctx, ada_w, ada_b, norm1, norm2, l0_w_in, l0_conv_w, l0_conv_b, l0_a_log, l0_dt_bias, l0_d_skip, l0_ssd_norm, l0_gate_w2, l0_gate_b, l0_gla_norm, l0_w_out, l1_w_qkv, l1_q_norm, l1_k_norm, l1_sinks, l1_w_out, router_w, router_b, exp_w_gate, exp_b_gate, exp_w_up, exp_b_up, exp_w_down, exp_b_down):
    lay = Layout(x_prompt.shape[0], x_prompt.shape[1], x_sample.shape[0], x_sample.shape[1])
    return _forward(lay, x_prompt, x_sample, state_l0_ssd_fwd, state_l0_ssd_bwd, state_l0_gla_fwd,
                    state_l0_gla_bwd, cache_l1_k, cache_l1_v, c, c_ctx, ada_w, ada_b, norm1, norm2,
                    l0_w_in, l0_conv_w, l0_conv_b, l0_a_log, l0_dt_bias, l0_d_skip, l0_ssd_norm,
                    l0_gate_w2, l0_gate_b, l0_gla_norm, l0_w_out,
                    l1_w_qkv, l1_q_norm, l1_k_norm, l1_sinks, l1_w_out,
                    router_w, router_b, exp_w_gate, exp_b_gate, exp_w_up, exp_b_up, exp_w_down, exp_b_down)
```

```python
import functools
import math

import numpy as np
import jax
import jax.numpy as jnp
from jax import lax
from jax.experimental import pallas as pl
from jax.experimental.pallas import tpu as pltpu

F32 = jnp.float32
BF16 = jnp.bfloat16
I32 = jnp.int32
HI = lax.Precision.HIGHEST

D = 1024
EPS = 1e-6
N_ADA = 6
SSD_HEADS = 16
SSD_HEAD_DIM = 64
SSD_INNER = 1024
SSD_STATE = 128
SSD_GROUPS = 2
SSD_CONV = 5
SSD_CHUNK = 128
GLA_HEADS = 4
GLA_KEY_DIM = 128
GLA_VAL_DIM = 256
GLA_RANK = 16
GLA_TAU = 16.0
GLA_BLOCK = 64
ATT_HEADS = 16
ATT_KV = 4
ATT_HD = 64
ATT_BLOCK = 128
WINDOW = 128
GRID_W = 64
ROPE_THETA = 10000.0
N_EXPERTS = 32
TOP_K = 4
EXPERT_FF = 1024
SWIGLU_LIMIT = 7.0
SWIGLU_ALPHA = 1.702
MOE_BLOCK = 256
TOK_TILE = 256
LANES = 128
NEG = -1e30

PJ_Z, PJ_V, PJ_OG, PJ_XBC, PJ_Q, PJ_K = 0, 1024, 2048, 3072, 4608, 5120
PJ_W = 5632
VMEM_LIMIT = 48 * 1024 * 1024


def _cp(sem, vmem=VMEM_LIMIT):
    return pltpu.CompilerParams(dimension_semantics=sem, vmem_limit_bytes=vmem)


class Layout:
    def __init__(self, n_prompt, prompt_len, n_sample, sample_len):
        self.n_prompt, self.prompt_len = n_prompt, prompt_len
        self.n_sample, self.sample_len = n_sample, sample_len
        self.p_tok = n_prompt * prompt_len
        self.n_tok = self.p_tok + n_sample * sample_len
        self.seqs = [(i * prompt_len, prompt_len) for i in range(n_prompt)]
        self.seqs += [(self.p_tok + i * sample_len, sample_len) for i in range(n_sample)]
        self.n_seq = len(self.seqs)

    def mod_row(self, start):
        return jnp.where(start < self.p_tok, 0, 1 + (start - self.p_tok) // self.sample_len)

def _sigmoid(x):
    return 1.0 / (1.0 + jnp.exp(-x))


def _silu(x):
    return x * _sigmoid(x)


def _softplus(x):
    return jnp.maximum(x, 0.0) + jnp.log(1.0 + jnp.exp(-jnp.abs(x)))


def _modnorm(x, g, sc, sh):
    ms = jnp.mean(x * x, axis=-1, keepdims=True)
    return (x * lax.rsqrt(ms + EPS) * g) * (1.0 + sc) + sh


def _dot(a, b, **kw):
    return jnp.dot(a, b, preferred_element_type=F32, **kw)


def _dot_nt(a, b):
    return lax.dot_general(a, b, (((1,), (1,)), ((), ())), preferred_element_type=F32)


def _dot_tn(a, b):
    return lax.dot_general(a, b, (((0,), (0,)), ((), ())), preferred_element_type=F32)


def _split(x, n):
    parts = []
    for _ in range(n):
        p = x.astype(BF16)
        parts.append(p)
        x = x - p.astype(F32)
    return parts


def _dot_sel(sel, x):
    sel = sel.astype(BF16)
    return sum(_dot(sel, p) for p in _split(x, 3))


def _dot_sel_r(x, sel):
    sel = sel.astype(BF16)
    return sum(_dot(p, sel) for p in _split(x, 3))


def _dot_hilo(x, w_hi, w_lo):
    x_hi, x_lo = _split(x, 2)
    return _dot(x_hi, w_hi) + _dot(x_lo, w_hi) + _dot(x_hi, w_lo)


def _hilo(w):
    hi = w.astype(BF16)
    return jnp.stack([hi, (w - hi.astype(F32)).astype(BF16)])


U32 = jnp.uint32
ROW_WORDS = D // 2
_HI_MASK = 0xFFFF0000


def _pack_rows(x):
    lo = lax.bitcast_convert_type(x[:, :ROW_WORDS], U32) >> 16
    hi = lax.bitcast_convert_type(x[:, ROW_WORDS:], U32) & jnp.uint32(_HI_MASK)
    return lo | hi


def _unpack_rows(u):
    lo = lax.bitcast_convert_type(u << 16, F32)
    hi = lax.bitcast_convert_type(u & jnp.uint32(_HI_MASK), F32)
    return jnp.concatenate([lo, hi], axis=1)


def _ada_kernel(c_ref, w_ref, b_ref, o_ref):
    o_ref[0] = _dot(_silu(c_ref[...]), w_ref[0], precision=HI) + b_ref[0]


def ada_table(cond8, ada_w, ada_b):
    depth, _, n = ada_w.shape
    tn = 1536
    return pl.pallas_call(
        _ada_kernel, grid=(depth, n // tn),
        in_specs=[pl.BlockSpec((8, D), lambda l, j: (0, 0)),
                  pl.BlockSpec((1, D, tn), lambda l, j: (l, 0, j)),
                  pl.BlockSpec((1, 1, tn), lambda l, j: (l, 0, j))],
        out_specs=pl.BlockSpec((1, 8, tn), lambda l, j: (l, 0, j)),
        out_shape=jax.ShapeDtypeStruct((depth, 8, n), F32),
        compiler_params=_cp(("arbitrary", "arbitrary")), name="ada_table",
    )(cond8, ada_w, ada_b.reshape(depth, 1, n))


def _stream_specs(lay, stream, tm):
    if not isinstance(stream, (tuple, list)):
        return [stream], [pl.BlockSpec((tm, stream.shape[1]), lambda i, *_: (i, 0))]
    assert lay.p_tok % tm == 0
    npt = lay.p_tok // tm
    w = stream[0].shape[1]
    return list(stream), [pl.BlockSpec((tm, w), lambda i, *_: (jnp.minimum(i, npt - 1), 0)),
                          pl.BlockSpec((tm, w), lambda i, *_: (jnp.maximum(i - npt, 0), 0))]


def _stream_tile(refs, in_prompt):
    if len(refs) == 1:
        return refs[0][...]
    return jnp.where(in_prompt, refs[0][...], refs[1][...])


def _proj_kernel(*refs, nx, npt):
    x_refs = refs[:nx]
    g_ref, sc_ref, sh_ref, w_ref, ws_ref, o_ref, os_ref, ost_ref, h_scr = refs[nx:]

    @pl.when(pl.program_id(1) == 0)
    def _():
        x = _stream_tile(x_refs, pl.program_id(0) < npt)
        h = _modnorm(x, g_ref[...], sc_ref[...], sh_ref[...])
        h_scr[...] = h.astype(BF16)
        small = _dot_hilo(h, ws_ref[0], ws_ref[1])
        os_ref[...] = small
        q = ost_ref.shape[2]
        for c in range(ost_ref.shape[0]):
            ost_ref[c] = small[q * c:q * (c + 1), :].T

    o_ref[...] = _dot(h_scr[...], w_ref[...]).astype(o_ref.dtype)


def norm_proj(lay, x, g, sc, sh, w, w_small, tm, tn, out_dtype, chunk):
    n_tok = lay.n_tok
    n = w.shape[1]
    ns = w_small.shape[-1]
    mrow = lambda i, j: (lay.mod_row(i * tm), 0, 0)
    xs, x_specs = _stream_specs(lay, x, tm)
    return pl.pallas_call(
        functools.partial(_proj_kernel, nx=len(xs), npt=lay.p_tok // tm), grid=(n_tok // tm, n // tn),
        in_specs=x_specs + [pl.BlockSpec((1, D), lambda i, j: (0, 0)),
                            pl.BlockSpec((None, 1, D), mrow),
                            pl.BlockSpec((None, 1, D), mrow),
                            pl.BlockSpec((D, tn), lambda i, j: (0, j)),
                            pl.BlockSpec((2, D, ns), lambda i, j: (0, 0, 0))],
        out_specs=[pl.BlockSpec((tm, tn), lambda i, j: (i, j)),
                   pl.BlockSpec((tm, ns), lambda i, j: (i, 0)),
                   pl.BlockSpec((tm // chunk, ns, chunk), lambda i, j: (i, 0, 0))],
        out_shape=[jax.ShapeDtypeStruct((n_tok, n), out_dtype),
                   jax.ShapeDtypeStruct((n_tok, ns), F32),
                   jax.ShapeDtypeStruct((n_tok // chunk, ns, chunk), F32)],
        scratch_shapes=[pltpu.VMEM((tm, D), BF16)],
        compiler_params=_cp(("arbitrary", "arbitrary")), name="norm_proj",
    )(*xs, g.reshape(1, D), sc, sh, w, w_small)


CONV_HALO = 16


def _ssd_load(d, c, xc, dtg_ref, dtgT_ref, S):
    q = SSD_CHUNK
    nh = SSD_HEADS
    rows = pl.ds(pl.multiple_of(c * q, q), q)
    return (rows, xc[rows, 0:SSD_INNER], xc[rows, SSD_INNER:SSD_INNER + 2 * SSD_GROUPS * SSD_STATE],
            dtg_ref[rows, nh * d:nh * d + nh], dtgT_ref[c, nh * d:nh * d + nh, :], S[...])


def _ssd_chunk(d, loaded, alog_ref, alogT_ref, dtb_ref, dtbT_ref):
    q = SSD_CHUNK
    nh = SSD_HEADS
    _, xs, bc, dtg, dtgT, s = loaded
    xs = xs.astype(F32)
    dt = _softplus(dtg + dtb_ref[d:d + 1, :])
    dtT = _softplus(dtgT + dtbT_ref[:, d:d + 1])
    ad = dt * (-jnp.exp(alog_ref[d:d + 1, :]))
    adT = dtT * (-jnp.exp(alogT_ref[:, d:d + 1]))
    row = lax.broadcasted_iota(I32, (q, q), 0)
    col = lax.broadcasted_iota(I32, (q, q), 1)
    if d == 0:
        e = _dot_sel(col <= row, ad)
        eT = _dot_sel_r(adT, row <= col)
        tot = e[q - 1:q, :]
        mask = row >= col
        fq = jnp.exp(e)
        fk = jnp.exp(tot - e)
    else:
        e = _dot_sel(col < row, ad)
        eT = _dot_sel_r(adT, row < col)
        tot = jnp.sum(ad, axis=0, keepdims=True)
        mask = col >= row
        fq = jnp.exp(tot - e)
        fk = jnp.exp(e)
    dec = jnp.exp(tot)
    lo = lax.broadcasted_iota(I32, (q, LANES), 1) < SSD_HEAD_DIM
    lo1 = lax.broadcasted_iota(I32, (1, LANES), 1) < SSD_HEAD_DIM

    def colpat(arr, a):
        return jnp.where(lo, arr[:, a:a + 1], arr[:, a + 1:a + 2])

    rep = (nh // SSD_GROUPS) // 2
    ys, s_new = [], []
    for g in range(SSD_GROUPS):
        bg = bc[:, SSD_STATE * g:SSD_STATE * (g + 1)]
        cg = bc[:, SSD_STATE * (SSD_GROUPS + g):SSD_STATE * (SSD_GROUPS + g + 1)]
        gmat = _dot_nt(cg, bg)
        for j in range(rep * g, rep * (g + 1)):
            a = 2 * j
            sl = slice(LANES * j, LANES * (j + 1))
            parts = []
            for hh in (a, a + 1):
                if d == 0:
                    diff = e[:, hh:hh + 1] - eT[hh:hh + 1, :]
                else:
                    diff = eT[hh:hh + 1, :] - e[:, hh:hh + 1]
                parts.append((gmat * jnp.exp(jnp.where(mask, diff, NEG))).astype(BF16))
            lhs = jnp.concatenate(parts, axis=1)
            xdt = xs[:, sl] * colpat(dt, a)
            rhs = jnp.concatenate([jnp.where(lo, xdt, 0.0), jnp.where(lo, 0.0, xdt)], axis=0)
            sj = s[:, sl]
            ys.append(_dot(lhs, rhs.astype(BF16)) + _dot(cg, sj.astype(BF16)) * colpat(fq, a))
            xk = (xdt * colpat(fk, a)).astype(BF16)
            decp = jnp.where(lo1, dec[:, a:a + 1], dec[:, a + 1:a + 2])
            s_new.append(sj * decp + _dot_tn(bg, xk))
    return jnp.concatenate(ys, axis=1), jnp.concatenate(s_new, axis=1)


def _ssd_seq_kernel(*refs, has_init):
    (xbc_ref, z_ref, dtg_ref, dtgT_ref, cw_ref, cb_ref,
     alog_ref, alogT_ref, dtb_ref, dtbT_ref, dskip_ref, nrm_ref) = refs[:12]
    refs = refs[12:]
    if has_init:
        s0f_ref, s0b_ref = refs[:2]
        refs = refs[2:]
    y_ref, sf_ref, sb_ref, xc, ext, Sf, Sb, yf, yb = refs
    ln = xbc_ref.shape[0]
    q = SSD_CHUNK
    nc = ln // q
    h = CONV_HALO
    pad = SSD_CONV // 2

    def conv_body(c, carry):
        r0 = pl.multiple_of(c * q, q)
        prev = xbc_ref[pl.ds(pl.multiple_of(jnp.maximum(r0 - h, 0), h), h), :].astype(F32)
        nxt = xbc_ref[pl.ds(pl.multiple_of(jnp.minimum(r0 + q, ln - h), h), h), :].astype(F32)
        ext[0:h, :] = jnp.where(c > 0, prev, 0.0)
        ext[h:h + q, :] = xbc_ref[pl.ds(r0, q), :].astype(F32)
        ext[h + q:h + q + h, :] = jnp.where(c < nc - 1, nxt, 0.0)
        acc = jnp.broadcast_to(cb_ref[...], (q, cb_ref.shape[1]))
        for k in range(SSD_CONV):
            acc = acc + cw_ref[k:k + 1, :] * ext[h - pad + k:h - pad + k + q, :]
        xc[pl.ds(r0, q), :] = _silu(acc).astype(xc.dtype)
        return carry
    lax.fori_loop(0, nc, conv_body, 0)

    if has_init:
        Sf[...] = s0f_ref[0].T
        Sb[...] = s0b_ref[0].T
    else:
        Sf[...] = jnp.zeros_like(Sf)
        Sb[...] = jnp.zeros_like(Sb)
    params = (alog_ref, alogT_ref, dtb_ref, dtbT_ref)

    def scan_body(c, carry):
        lf = _ssd_load(0, c, xc, dtg_ref, dtgT_ref, Sf)
        lb = _ssd_load(1, nc - 1 - c, xc, dtg_ref, dtgT_ref, Sb)
        y_f, s_f = _ssd_chunk(0, lf, *params)
        y_b, s_b = _ssd_chunk(1, lb, *params)
        yf[lf[0], :] = y_f
        yb[lb[0], :] = y_b
        Sf[...] = s_f
        Sb[...] = s_b
        return carry
    lax.fori_loop(0, nc, scan_body, 0)
    sf_ref[0] = Sf[...].T
    sb_ref[0] = Sb[...].T

    def out_body(c, carry):
        rows = pl.ds(pl.multiple_of(c * q, q), q)
        ytot = yf[rows, :] + yb[rows, :] + dskip_ref[...] * xc[rows, 0:SSD_INNER].astype(F32)
        yg = ytot * _silu(z_ref[rows, :].astype(F32))
        ms = jnp.mean(yg * yg, axis=-1, keepdims=True)
        y_ref[rows, :] = (yg * lax.rsqrt(ms + EPS) * nrm_ref[...]).astype(y_ref.dtype)
        return carry
    lax.fori_loop(0, nc, out_body, 0)


def _ssd_call(n_seq, ln, blk0, proj, small, smallT3, init, params):
    q = SSD_CHUNK
    hp = SSD_HEADS * SSD_HEAD_DIM
    cw = SSD_INNER + 2 * SSD_GROUPS * SSD_STATE
    nc = ln // q
    assert PJ_XBC % cw == 0 and PJ_Z % SSD_INNER == 0
    tok = lambda w, cb: pl.BlockSpec((ln, w), lambda b: (blk0 + b, cb))
    seq3 = pl.BlockSpec((1, hp, SSD_STATE), lambda b: (b, 0, 0))
    full = lambda a: pl.BlockSpec(a.shape, lambda b: (0,) * a.ndim)
    init = () if init is None else tuple(init)
    return pl.pallas_call(
        functools.partial(_ssd_seq_kernel, has_init=bool(init)), grid=(n_seq,),
        in_specs=[tok(cw, PJ_XBC // cw), tok(SSD_INNER, PJ_Z // SSD_INNER), tok(LANES, 0),
                  pl.BlockSpec((nc, 2 * SSD_HEADS, q), lambda b: (blk0 + b, 0, 0))]
        + [full(a) for a in params] + [seq3] * len(init),
        out_specs=[pl.BlockSpec((ln, hp), lambda b: (b, 0)), seq3, seq3],
        out_shape=[jax.ShapeDtypeStruct((n_seq * ln, hp), BF16),
                   jax.ShapeDtypeStruct((n_seq, hp, SSD_STATE), F32),
                   jax.ShapeDtypeStruct((n_seq, hp, SSD_STATE), F32)],
        scratch_shapes=[pltpu.VMEM((ln, cw), BF16), pltpu.VMEM((q + 2 * CONV_HALO, cw), F32),
                        pltpu.VMEM((SSD_STATE, hp), F32), pltpu.VMEM((SSD_STATE, hp), F32),
                        pltpu.VMEM((ln, hp), F32), pltpu.VMEM((ln, hp), F32)],
        compiler_params=_cp(("arbitrary",)), name="ssd_seq",
    )(proj, proj, small, smallT3, *params, *init)


def _gla_load(d, c, q_ref, k_ref, v_ref, glr_ref, S):
    t = GLA_BLOCK
    rows = pl.ds(pl.multiple_of(c * t, t), t)
    c0 = 2 * SSD_HEADS + GLA_RANK * d
    return (rows, q_ref[rows, :], k_ref[rows, :], v_ref[rows, :], glr_ref[rows, c0:c0 + GLA_RANK],
            [S[h] for h in range(GLA_HEADS)])


def _gla_block(d, loaded, w2_ref, gb_ref):
    t = GLA_BLOCK
    dk, dv = GLA_KEY_DIM, GLA_VAL_DIM
    _, q, k, v, glr, states = loaded
    gp = _dot_hilo(glr, w2_ref[0, d], w2_ref[1, d]) + gb_ref[d:d + 1, :]
    la = -_softplus(-gp) * (1.0 / GLA_TAU)
    row = lax.broadcasted_iota(I32, (t, t), 0)
    col = lax.broadcasted_iota(I32, (t, t), 1)
    mid = t // 2 - 1
    if d == 0:
        e = _dot_sel(col <= row, la)
        tot = e[t - 1:t, :]
        r = e[mid:mid + 1, :]
        fqi, fki = jnp.exp(e - r), jnp.exp(r - e)
        fq, fk = jnp.exp(e), jnp.exp(tot - e)
        mask = row >= col
    else:
        e = _dot_sel(col < row, la)
        tot = e[t - 1:t, :] + la[t - 1:t, :]
        r = e[mid:mid + 1, :]
        fqi, fki = jnp.exp(r - e), jnp.exp(e - r)
        fq, fk = jnp.exp(tot - e), jnp.exp(e)
        mask = col >= row
    dec = jnp.exp(tot)
    qf = q.astype(F32) * (dk ** -0.5)
    kf = k.astype(F32)
    outs, new_states = [], []
    for h in range(GLA_HEADS):
        sl = slice(dk * h, dk * (h + 1))
        qh, kh = qf[:, sl], kf[:, sl]
        sc = _dot_nt((qh * fqi[:, sl]).astype(BF16), (kh * fki[:, sl]).astype(BF16))
        sc = jnp.where(mask, sc, 0.0)
        vh = v[:, dv * h:dv * (h + 1)]
        st = states[h]
        outs.append(_dot(sc.astype(BF16), vh) + _dot_nt((qh * fq[:, sl]).astype(BF16), st.astype(BF16)))
        new_states.append(st * dec[:, sl] + _dot_tn(vh, (kh * fk[:, sl]).astype(BF16)))
    return jnp.concatenate(outs, axis=1), new_states


def _gla_seq_kernel(*refs, has_init):
    q_ref, k_ref, v_ref, og_ref, glr_ref, w2_ref, gb_ref, nrm_ref = refs[:8]
    refs = refs[8:]
    if has_init:
        s0f_ref, s0b_ref = refs[:2]
        refs = refs[2:]
    o_ref, sf_ref, sb_ref, Sf, Sb, of, ob = refs
    ln = q_ref.shape[0]
    t = GLA_BLOCK
    nc = ln // t
    dv = GLA_VAL_DIM
    for h in range(GLA_HEADS):
        if has_init:
            Sf[h] = s0f_ref[0, h].T
            Sb[h] = s0b_ref[0, h].T
        else:
            Sf[h] = jnp.zeros(Sf.shape[1:], F32)
            Sb[h] = jnp.zeros(Sb.shape[1:], F32)
    ins = (q_ref, k_ref, v_ref, glr_ref)

    def scan_body(c, carry):
        lf = _gla_load(0, c, *ins, Sf)
        lb = _gla_load(1, nc - 1 - c, *ins, Sb)
        o_f, s_f = _gla_block(0, lf, w2_ref, gb_ref)
        o_b, s_b = _gla_block(1, lb, w2_ref, gb_ref)
        of[lf[0], :] = o_f
        ob[lb[0], :] = o_b
        for h in range(GLA_HEADS):
            Sf[h] = s_f[h]
            Sb[h] = s_b[h]
        return carry
    lax.fori_loop(0, nc, scan_body, 0, unroll=2)
    for h in range(GLA_HEADS):
        sf_ref[0, h] = Sf[h].T
        sb_ref[0, h] = Sb[h].T

    def out_body(c, carry):
        rows = pl.ds(pl.multiple_of(c * t, t), t)
        for h in range(GLA_HEADS):
            vl = slice(dv * h, dv * (h + 1))
            ot = of[rows, vl] + ob[rows, vl]
            ms = jnp.mean(ot * ot, axis=-1, keepdims=True)
            on = ot * lax.rsqrt(ms + EPS) * nrm_ref[...]
            o_ref[rows, vl] = (on * _silu(og_ref[rows, vl].astype(F32))).astype(o_ref.dtype)
        return carry
    lax.fori_loop(0, nc, out_body, 0)


def _gla_call(n_seq, ln, blk0, proj, small, init, params):
    qk_w = GLA_HEADS * GLA_KEY_DIM
    v_w = GLA_HEADS * GLA_VAL_DIM
    tok = lambda w, cb: pl.BlockSpec((ln, w), lambda b: (blk0 + b, cb))
    seq4 = pl.BlockSpec((1, GLA_HEADS, GLA_KEY_DIM, GLA_VAL_DIM), lambda b: (b, 0, 0, 0))
    full = lambda a: pl.BlockSpec(a.shape, lambda b: (0,) * a.ndim)
    st_shape = jax.ShapeDtypeStruct((n_seq, GLA_HEADS, GLA_KEY_DIM, GLA_VAL_DIM), F32)
    init = () if init is None else tuple(init)
    return pl.pallas_call(
        functools.partial(_gla_seq_kernel, has_init=bool(init)), grid=(n_seq,),
        in_specs=[tok(qk_w, PJ_Q // qk_w), tok(qk_w, PJ_K // qk_w), tok(v_w, PJ_V // v_w),
                  tok(v_w, PJ_OG // v_w), tok(LANES, 0)] + [full(a) for a in params] + [seq4] * len(init),
        out_specs=[pl.BlockSpec((ln, v_w), lambda b: (b, 0)), seq4, seq4],
        out_shape=[jax.ShapeDtypeStruct((n_seq * ln, v_w), BF16), st_shape, st_shape],
        scratch_shapes=[pltpu.VMEM((GLA_HEADS, GLA_VAL_DIM, GLA_KEY_DIM), F32),
                        pltpu.VMEM((GLA_HEADS, GLA_VAL_DIM, GLA_KEY_DIM), F32),
                        pltpu.VMEM((ln, v_w), F32), pltpu.VMEM((ln, v_w), F32)],
        compiler_params=_cp(("arbitrary",)), name="gla_seq",
    )(proj, proj, proj, proj, small, *params, *init)


def l0_mixers(lay, proj, small, small_t, ssd_f0, ssd_b0, gla_f0, gla_b0, conv_w, conv_b, a_log, dt_bias, d_skip,
              ssd_norm, gate_w2, gate_b, gla_norm):
    hp = SSD_HEADS * SSD_HEAD_DIM
    ssd_p = (conv_w, conv_b.reshape(1, -1), a_log, a_log.T, dt_bias, dt_bias.T,
             jnp.repeat(d_skip, SSD_HEAD_DIM).reshape(1, hp), ssd_norm.reshape(1, hp))
    gla_p = (_hilo(gate_w2), gate_b, gla_norm.reshape(1, -1))
    np_, ns = lay.n_prompt, lay.n_sample
    assert lay.p_tok % lay.sample_len == 0
    groups = [(np_, lay.prompt_len, 0, None, None),
              (ns, lay.sample_len, lay.p_tok // lay.sample_len,
               (ssd_f0.reshape(ns, hp, SSD_STATE), ssd_b0.reshape(ns, hp, SSD_STATE)), (gla_f0, gla_b0))]
    ys, os_, states = [], [], None
    for n, ln, blk0, ssd_init, gla_init in groups:
        y, sf, sb = _ssd_call(n, ln, blk0, proj, small, small_t, ssd_init, ssd_p)
        o, gf, gb = _gla_call(n, ln, blk0, proj, small, gla_init, gla_p)
        ys.append(y)
        os_.append(o)
        if states is None:
            states = (sf, sb, gf, gb)
    return tuple(ys), tuple(os_), states


def _res_kernel(*refs, counts, ks, npt):
    in_prompt = pl.program_id(0) < npt
    streams, pos = [], 0
    for c in counts:
        streams.append(refs[pos:pos + c])
        pos += c
    w_ref, gate_ref, o_ref = refs[pos:]
    acc = None
    off = 0
    for a_refs, k in zip(streams[:-1], ks):
        part = _dot(_stream_tile(a_refs, in_prompt), w_ref[off:off + k, :])
        acc = part if acc is None else acc + part
        off += k
    o_ref[...] = _stream_tile(streams[-1], in_prompt) + gate_ref[...] * acc


def proj_residual(lay, acts, w, x, gate, tm=512):
    arrays, specs, counts = [], [], []
    for s in list(acts) + [x]:
        a, sp = _stream_specs(lay, s, tm)
        arrays += a
        specs += sp
        counts.append(len(a))
    ks = tuple(int((a[0] if isinstance(a, (tuple, list)) else a).shape[1]) for a in acts)
    mrow = lambda i: (lay.mod_row(i * tm), 0, 0)
    return pl.pallas_call(
        functools.partial(_res_kernel, counts=tuple(counts), ks=ks, npt=lay.p_tok // tm),
        grid=(lay.n_tok // tm,),
        in_specs=specs + [pl.BlockSpec(w.shape, lambda i: (0, 0)), pl.BlockSpec((None, 1, D), mrow)],
        out_specs=pl.BlockSpec((tm, D), lambda i: (i, 0)),
        out_shape=jax.ShapeDtypeStruct((lay.n_tok, D), F32),
        compiler_params=_cp(("arbitrary",)), name="proj_residual",
    )(*arrays, w, gate)


def _router_kernel(x_ref, g_ref, sc_ref, sh_ref, rw_ref, rb_ref,
                   h_ref, idx_ref, gate_ref, pos_ref, posT_ref, cnt_ref):
    tm = x_ref.shape[0]
    h = _modnorm(x_ref[...], g_ref[...], sc_ref[...], sh_ref[...])
    h_hi = h.astype(BF16)
    h_ref[...] = h_hi
    h_lo = (h - h_hi.astype(F32)).astype(BF16)
    lg = (_dot(h_hi, rw_ref[0]) + _dot(h_lo, rw_ref[0]) + _dot(h_hi, rw_ref[1])
          + rb_ref[...])
    lane = lax.broadcasted_iota(I32, (tm, LANES), 1).astype(F32)
    vals, ids = [], []
    for _ in range(TOP_K):
        m = jnp.max(lg, axis=1, keepdims=True)
        i = jnp.min(jnp.where(lg == m, lane, float(LANES)), axis=1, keepdims=True)
        vals.append(m)
        ids.append(i)
        lg = jnp.where(lane == i, -jnp.inf, lg)
    ex = [jnp.exp(v - vals[0]) for v in vals]
    den = ex[0] + ex[1] + ex[2] + ex[3]
    sel = jnp.zeros((tm, LANES), F32)
    for i in ids:
        sel = sel + (lane == i).astype(F32)
    row = lax.broadcasted_iota(I32, (tm, tm), 0)
    col = lax.broadcasted_iota(I32, (tm, tm), 1)
    before = _dot((col < row).astype(BF16), sel.astype(BF16))
    n = jnp.sum(sel, axis=0, keepdims=True)
    er = lax.broadcasted_iota(I32, (LANES, LANES), 0)
    ec = lax.broadcasted_iota(I32, (LANES, LANES), 1)
    n_al = jnp.ceil(n * (1.0 / SEG_ALIGN)) * SEG_ALIGN
    offs = _dot(jnp.broadcast_to(n_al, (8, LANES)).astype(BF16), (er < ec).astype(BF16))[0:1, :]
    slot = before + offs
    idx_o = jnp.zeros((tm, LANES), F32)
    gate_o = jnp.zeros((tm, LANES), F32)
    pos_o = jnp.zeros((tm, LANES), F32)
    for k in range(TOP_K):
        p = jnp.sum(jnp.where(lane == ids[k], slot, 0.0), axis=1, keepdims=True)
        idx_o = jnp.where(lane == k, ids[k], idx_o)
        gate_o = jnp.where(lane == k, ex[k] / den, gate_o)
        pos_o = jnp.where(lane == k, p, pos_o)
    idx_ref[...] = idx_o.astype(I32)
    gate_ref[...] = gate_o
    pos_ref[...] = pos_o.astype(I32)
    posT_ref[...] = pos_o.T[0:8, :]
    cnt_ref[0] = jnp.broadcast_to(n, (8, LANES))


def moe_router(lay, x, g, sc, sh, rw, rb):
    n_tok = x.shape[0]
    tm = TOK_TILE
    nt = n_tok // tm
    mrow = lambda i: (lay.mod_row(i * tm), 0, 0)
    tile = lambda w, dt: (pl.BlockSpec((tm, w), lambda i: (i, 0)), jax.ShapeDtypeStruct((n_tok, w), dt))
    outs = [tile(D, BF16), tile(LANES, I32), tile(LANES, F32), tile(LANES, I32),
            (pl.BlockSpec((8, tm), lambda i: (0, i)), jax.ShapeDtypeStruct((8, n_tok), F32)),
            (pl.BlockSpec((1, 8, LANES), lambda i: (i, 0, 0)), jax.ShapeDtypeStruct((nt, 8, LANES), F32))]
    return pl.pallas_call(
        _router_kernel, grid=(nt,),
        in_specs=[pl.BlockSpec((tm, D), lambda i: (i, 0)),
                  pl.BlockSpec((1, D), lambda i: (0, 0)),
                  pl.BlockSpec((None, 1, D), mrow), pl.BlockSpec((None, 1, D), mrow),
                  pl.BlockSpec((2, D, LANES), lambda i: (0, 0, 0)),
                  pl.BlockSpec((1, LANES), lambda i: (0, 0))],
        out_specs=[o[0] for o in outs], out_shape=[o[1] for o in outs],
        compiler_params=_cp(("arbitrary",)), name="moe_router",
    )(x, g.reshape(1, D), sc, sh, rw, rb)


SEG_ALIGN = 8
SEG_BITS = tuple(range(int(math.log2(TOK_TILE)), int(math.log2(SEG_ALIGN)) - 1, -1))
TILE_ROWS = TOK_TILE * TOP_K + N_EXPERTS * SEG_ALIGN


def _pow2_copies(n, src, dst, make_copy, op, bits):
    for b in bits:
        sz = 1 << b
        done = (n >> (b + 1)) << (b + 1)

        @pl.when((n & sz) != 0)
        def _():
            op(make_copy(pl.multiple_of(src + done, SEG_ALIGN), pl.multiple_of(dst + done, SEG_ALIGN), sz))


def _start_segments(i, n_ref, off_ref, dst_ref, make_copy):
    def body(e, carry):
        k = i * N_EXPERTS + e
        _pow2_copies(n_ref[k], off_ref[k], dst_ref[k], make_copy, lambda c: c.start(), SEG_BITS)
        return carry
    lax.fori_loop(0, N_EXPERTS, body, 0)


TAIL_BITS = tuple(range(int(math.log2(MOE_BLOCK)) - 1, int(math.log2(SEG_ALIGN)) - 1, -1))
TILE_BITS = tuple(range(int(math.log2(TILE_ROWS)), int(math.log2(SEG_ALIGN)) - 1, -1))


def _wait_rows(total, make_copy):
    _pow2_copies(total, 0, 0, make_copy, lambda c: c.wait(), TILE_BITS)


def _dispatch_kernel(n_ref, off_ref, dst_ref, tot_ref, tn_ref, td_ref, posT_ref, h_ref, xout_ref,
                     srt, zbuf, sems):
    i = pl.program_id(0)
    last = pl.num_programs(0) - 1
    slot = i % 2
    tm = h_ref.shape[0]
    r = lax.broadcasted_iota(I32, (TILE_ROWS, tm), 0)
    hit = jnp.zeros((TILE_ROWS, tm), jnp.bool_)
    for k in range(TOP_K):
        hit = hit | (r == posT_ref[k:k + 1, :].astype(I32))
    sel = jnp.where(hit, 1.0, 0.0).astype(BF16)
    srt[slot] = _pack_rows(_dot(sel, h_ref[...]))

    def copier(s):
        def make_copy(src, dst, sz):
            return pltpu.make_async_copy(srt.at[s, pl.ds(src, sz)], xout_ref.at[pl.ds(dst, sz)], sems.at[s])
        return make_copy

    _start_segments(i, n_ref, off_ref, dst_ref, copier(slot))

    @pl.when(i > 0)
    def _():
        _wait_rows(tot_ref[jnp.maximum(i - 1, 0)], copier(1 - slot))

    @pl.when(i == last)
    def _():
        _wait_rows(tot_ref[i], copier(slot))
        zbuf[...] = jnp.zeros_like(zbuf)
        sem = sems.at[0]

        def zero_copy(src, dst, sz):
            return pltpu.make_async_copy(zbuf.at[pl.ds(src, sz)], xout_ref.at[pl.ds(dst, sz)], sem)

        nb = xout_ref.shape[0] // MOE_BLOCK
        for op in (lambda c: c.start(), lambda c: c.wait()):
            def body(e, carry):
                _pow2_copies(tn_ref[e], 0, td_ref[e], zero_copy, op, TAIL_BITS)
                return carry
            lax.fori_loop(0, N_EXPERTS, body, 0)

            def unused(b, carry):
                op(zero_copy(0, pl.multiple_of(b * MOE_BLOCK, MOE_BLOCK), MOE_BLOCK))
                return carry
            lax.fori_loop(tn_ref[N_EXPERTS], nb, unused, 0)


def moe_dispatch(n_tab, off_tab, dst_tab, tot_tab, tail_n, tail_dst, posT, h2, n_rows):
    n_tok = h2.shape[0]
    tm = TOK_TILE
    grid_spec = pltpu.PrefetchScalarGridSpec(
        num_scalar_prefetch=6, grid=(n_tok // tm,),
        in_specs=[pl.BlockSpec((8, tm), lambda i, *_: (0, i)),
                  pl.BlockSpec((tm, D), lambda i, *_: (i, 0))],
        out_specs=pl.BlockSpec(memory_space=pl.ANY),
        scratch_shapes=[pltpu.VMEM((2, TILE_ROWS, ROW_WORDS), U32), pltpu.VMEM((MOE_BLOCK, ROW_WORDS), U32),
                        pltpu.SemaphoreType.DMA((2,))])
    return pl.pallas_call(
        _dispatch_kernel, grid_spec=grid_spec,
        out_shape=jax.ShapeDtypeStruct((n_rows, ROW_WORDS), U32),
        compiler_params=_cp(("arbitrary",)), name="moe_dispatch",
    )(n_tab, off_tab, dst_tab, tot_tab, tail_n, tail_dst, posT, h2)


def _combine_kernel(n_ref, off_ref, dst_ref, tot_ref, pos_ref, gate_ref, x_ref, g2_ref, y_ref, *rest, npt):
    o_refs, (buf, sems) = rest[:-2], rest[-2:]
    i = pl.program_id(0)
    last = pl.num_programs(0) - 1
    slot = i % 2
    tm = x_ref.shape[0]
    na = TILE_ROWS

    def copier(s):
        def make_copy(src, dst, sz):
            return pltpu.make_async_copy(y_ref.at[pl.ds(dst, sz)], buf.at[s, pl.ds(src, sz)], sems.at[s])
        return make_copy

    def fetch(tile, s):
        buf[s, tm * TOP_K:na, :] = jnp.zeros((na - tm * TOP_K, ROW_WORDS), U32)
        _start_segments(tile, n_ref, off_ref, dst_ref, copier(s))

    @pl.when(i == 0)
    def _():
        fetch(i, slot)

    @pl.when(i < last)
    def _():
        fetch(i + 1, 1 - slot)

    _wait_rows(tot_ref[i], copier(slot))
    lane = lax.broadcasted_iota(I32, (tm, na), 1)
    pw = jnp.zeros((tm, na), F32)
    for k in range(TOP_K):
        pw = pw + jnp.where(lane == pos_ref[:, k:k + 1], gate_ref[:, k:k + 1], 0.0)
    phi = pw.astype(BF16)
    plo = (pw - phi.astype(F32)).astype(BF16)
    yb = _unpack_rows(buf[slot]).astype(BF16)
    res = x_ref[...] + g2_ref[...] * (_dot(phi, yb) + _dot(plo, yb))
    if len(o_refs) == 1:
        o_refs[0][...] = res
    else:
        @pl.when(i < npt)
        def _():
            o_refs[0][...] = res

        @pl.when(i >= npt)
        def _():
            o_refs[1][...] = res


def moe_combine(lay, n_tab, off_tab, dst_tab, tot_tab, pos, gates, x, gate2, y_rows, split):
    n_tok = x.shape[0]
    tm = TOK_TILE
    npt = lay.p_tok // tm
    mrow = lambda i, *_: (lay.mod_row(i * tm), 0, 0)
    if split:
        out_specs = [pl.BlockSpec((tm, D), lambda i, *_: (jnp.minimum(i, npt - 1), 0)),
                     pl.BlockSpec((tm, D), lambda i, *_: (jnp.maximum(i - npt, 0), 0))]
        out_shape = [jax.ShapeDtypeStruct((lay.p_tok, D), F32), jax.ShapeDtypeStruct((n_tok - lay.p_tok, D), F32)]
    else:
        out_specs = pl.BlockSpec((tm, D), lambda i, *_: (i, 0))
        out_shape = jax.ShapeDtypeStruct((n_tok, D), F32)
    grid_spec = pltpu.PrefetchScalarGridSpec(
        num_scalar_prefetch=4, grid=(n_tok // tm,),
        in_specs=[pl.BlockSpec((tm, LANES), lambda i, *_: (i, 0)),
                  pl.BlockSpec((tm, LANES), lambda i, *_: (i, 0)),
                  pl.BlockSpec((tm, D), lambda i, *_: (i, 0)),
                  pl.BlockSpec((None, 1, D), mrow),
                  pl.BlockSpec(memory_space=pl.ANY)],
        out_specs=out_specs,
        scratch_shapes=[pltpu.VMEM((2, TILE_ROWS, ROW_WORDS), U32), pltpu.SemaphoreType.DMA((2,))])
    return pl.pallas_call(
        functools.partial(_combine_kernel, npt=npt), grid_spec=grid_spec, out_shape=out_shape,
        compiler_params=_cp(("arbitrary",)), name="moe_combine",
    )(n_tab, off_tab, dst_tab, tot_tab, pos, gates, x, gate2, y_rows)


def _expert_kernel(be_ref, nv_ref, nxt_ref, slot_ref, x_ref, bg_ref, bu_ref, bd_ref, wg_hbm, wu_hbm, wd_hbm,
                   y_ref, wf, sems, *, layer):
    i = pl.program_id(0)
    valid = i < nv_ref[0]
    e = be_ref[i]
    slot = slot_ref[e]
    changed = jnp.logical_or(i == 0, e != be_ref[jnp.maximum(i - 1, 0)])

    def weight_copies(ex, s):
        return [pltpu.make_async_copy(w.at[layer, ex], wf.at[s, k], sems.at[s, k])
                for k, w in enumerate((wg_hbm, wu_hbm, wd_hbm))]

    @pl.when(jnp.logical_and(valid, changed))
    def _():
        @pl.when(i == 0)
        def _():
            for c in weight_copies(e, slot):
                c.start()

        nxt = nxt_ref[e]

        @pl.when(nxt >= 0)
        def _():
            for c in weight_copies(nxt, 1 - slot):
                c.start()

        for c in weight_copies(e, slot):
            c.wait()

    @pl.when(valid)
    def _():
        x = _unpack_rows(x_ref[...])
        gt = jnp.minimum(_dot(x, wf[slot, 0]) + bg_ref[...], SWIGLU_LIMIT)
        up = jnp.clip(_dot(x, wf[slot, 1]) + bu_ref[...], -SWIGLU_LIMIT, SWIGLU_LIMIT)
        act = (up + 1.0) * gt * _sigmoid(SWIGLU_ALPHA * gt)
        y = _dot(act, wf[slot, 2]) + bd_ref[...]
        y_ref[...] = _pack_rows(y.astype(BF16).astype(F32))

    @pl.when(jnp.logical_not(valid))
    def _():
        y_ref[...] = jnp.zeros_like(y_ref)


def moe_experts(layer, blk_expert, n_valid, next_expert, slot, x_rows, w_gate, b_gate, w_up, b_up, w_down,
                b_down):
    n_rows = x_rows.shape[0]
    nb = n_rows // MOE_BLOCK
    depth, ne, _, ff = w_gate.shape
    assert ff == D
    rowblk = lambda i, be, nv, *_: (jnp.maximum(jnp.minimum(i, nv[0] - 1), 0), 0)
    bsel = lambda i, be, *_: (layer, be[i], 0, 0)
    hbm = pl.BlockSpec(memory_space=pl.ANY)
    grid_spec = pltpu.PrefetchScalarGridSpec(
        num_scalar_prefetch=4, grid=(nb,),
        in_specs=[pl.BlockSpec((MOE_BLOCK, ROW_WORDS), rowblk),
                  pl.BlockSpec((None, None, 1, ff), bsel), pl.BlockSpec((None, None, 1, ff), bsel),
                  pl.BlockSpec((None, None, 1, D), bsel), hbm, hbm, hbm],
        out_specs=pl.BlockSpec((MOE_BLOCK, ROW_WORDS), lambda i, *_: (i, 0)),
        scratch_shapes=[pltpu.VMEM((2, 3, D, ff), F32), pltpu.SemaphoreType.DMA((2, 3))])
    return pl.pallas_call(
        functools.partial(_expert_kernel, layer=layer), grid_spec=grid_spec,
        out_shape=jax.ShapeDtypeStruct((n_rows, ROW_WORDS), U32),
        compiler_params=_cp(("arbitrary",)), name="moe_experts",
    )(blk_expert, n_valid, next_expert, slot, x_rows, b_gate.reshape(depth, ne, 1, ff),
      b_up.reshape(depth, ne, 1, ff), b_down.reshape(depth, ne, 1, D), w_gate, w_up, w_down)


def moe_layer(lay, layer, x, g2, sc2, sh2, gate2, router_w, router_b, w_gate, b_gate, w_up, b_up, w_down,
              b_down, split=False):
    n_tok = x.shape[0]
    nt = n_tok // TOK_TILE
    rw = jnp.zeros((D, LANES), F32).at[:, :N_EXPERTS].set(router_w)
    rw = _hilo(rw)
    rb = jnp.full((1, LANES), NEG, F32).at[0, :N_EXPERTS].set(router_b)
    h2, _, gates, pos, posT, cnt = moe_router(lay, x, g2, sc2, sh2, rw, rb)
    n_te = cnt[:, 0, :N_EXPERTS].astype(I32)
    n_te = (n_te + SEG_ALIGN - 1) // SEG_ALIGN * SEG_ALIGN
    totals = jnp.sum(n_te, axis=0)
    padded = (totals + MOE_BLOCK - 1) // MOE_BLOCK * MOE_BLOCK
    padded_end = jnp.cumsum(padded)
    pstart = padded_end - padded
    dst = pstart[None, :] + jnp.cumsum(n_te, axis=0) - n_te
    off = jnp.cumsum(n_te, axis=1) - n_te
    n_rows = nt * TILE_ROWS + N_EXPERTS * MOE_BLOCK
    nb = n_rows // MOE_BLOCK
    n_valid = (padded_end[-1] // MOE_BLOCK).astype(I32).reshape(1)
    bstart = jnp.minimum(jnp.arange(nb, dtype=I32), n_valid[0] - 1) * MOE_BLOCK
    blk_expert = jnp.minimum(jnp.sum((bstart[:, None] >= padded_end[None, :]).astype(I32), axis=1),
                             N_EXPERTS - 1).astype(I32)
    tabs = (n_te.reshape(-1).astype(I32), off.reshape(-1).astype(I32), dst.reshape(-1).astype(I32),
            jnp.sum(n_te, axis=1).astype(I32))
    tail_n = jnp.concatenate([(padded - totals).astype(I32), n_valid])
    x_rows = moe_dispatch(*tabs, tail_n, (pstart + totals).astype(I32), posT, h2, n_rows)
    owner = jnp.where(padded > 0, jnp.arange(N_EXPERTS, dtype=I32), N_EXPERTS)
    later = jnp.concatenate([lax.cummin(owner, axis=0, reverse=True)[1:], jnp.full((1,), N_EXPERTS, I32)])
    next_expert = jnp.where(later < N_EXPERTS, later, -1).astype(I32)
    slot = ((jnp.cumsum((padded > 0).astype(I32)) - 1) % 2).astype(I32)
    y_rows = moe_experts(layer, blk_expert, n_valid, next_expert, slot, x_rows, w_gate, b_gate, w_up, b_up,
                         w_down, b_down)
    return moe_combine(lay, *tabs, pos, gates, x, gate2, y_rows, split)


QKV_TN = 256
N_QK_TILES = (ATT_HEADS + ATT_KV) * ATT_HD // QKV_TN


def _qkv_kernel(x_ref, g_ref, sc_ref, sh_ref, w_ref, nw_ref, cos_ref, sin_ref, o_ref):
    tm = x_ref.shape[0]
    h = _modnorm(x_ref[...], g_ref[...], sc_ref[...], sh_ref[...]).astype(BF16)
    r = lax.broadcasted_iota(I32, (QKV_TN, QKV_TN), 0) // ATT_HD
    c = lax.broadcasted_iota(I32, (QKV_TN, QKV_TN), 1) // ATT_HD
    head_mean = jnp.where(r == c, 1.0 / ATT_HD, 0.0).astype(BF16)
    lane = lax.broadcasted_iota(I32, (tm, QKV_TN), 1)
    half = ATT_HD // 4
    first = (lane % (2 * half)) < half
    for j in range(w_ref.shape[1] // QKV_TN):
        cols = slice(QKV_TN * j, QKV_TN * (j + 1))
        acc = _dot(h, w_ref[:, cols])
        if j >= N_QK_TILES:
            o_ref[:, cols] = acc
            continue
        ms = _dot((acc * acc).astype(BF16), head_mean)
        qn = acc * lax.rsqrt(ms + EPS) * nw_ref[j]
        swapped = jnp.where(first, pltpu.roll(qn, QKV_TN - half, 1), pltpu.roll(qn, half, 1))
        o_ref[:, cols] = qn * cos_ref[...] + swapped * sin_ref[...]


def _rope_tables(sample_len):
    pos = np.arange(sample_len)
    half = ATT_HD // 4
    inv = (ROPE_THETA ** (-np.arange(half, dtype=np.float32) / half)).astype(np.float32)
    ang_r = (pos // GRID_W).astype(np.float32)[:, None] * inv[None, :]
    ang_c = (pos % GRID_W).astype(np.float32)[:, None] * inv[None, :]
    cos = np.concatenate([np.cos(ang_r)] * 2 + [np.cos(ang_c)] * 2, axis=1)
    sin = np.concatenate([-np.sin(ang_r), np.sin(ang_r), -np.sin(ang_c), np.sin(ang_c)], axis=1)
    rep = QKV_TN // ATT_HD
    return (jnp.asarray(np.tile(cos, (1, rep)), F32), jnp.asarray(np.tile(sin, (1, rep)), F32))


def qkv_proj(lay, x, g, sc, sh, w, q_norm, k_norm, tm=512):
    n_tok = x.shape[0]
    n = w.shape[1]
    nq = ATT_HEADS * ATT_HD // QKV_TN
    rep = QKV_TN // ATT_HD
    nw = jnp.concatenate([jnp.tile(jnp.tile(q_norm, rep)[None, :], (nq, 1)),
                          jnp.tile(jnp.tile(k_norm, rep)[None, :], (n // QKV_TN - nq, 1))], axis=0)
    cos, sin = _rope_tables(lay.sample_len)
    cos = jnp.concatenate([jnp.ones((tm, QKV_TN), F32), cos], axis=0)
    sin = jnp.concatenate([jnp.zeros((tm, QKV_TN), F32), sin], axis=0)
    assert lay.p_tok % tm == 0 and lay.sample_len % tm == 0
    mrow = lambda i: (lay.mod_row(i * tm), 0, 0)
    rrow = lambda i: (jnp.where(i * tm < lay.p_tok, 0, 1 + ((i * tm - lay.p_tok) % lay.sample_len) // tm), 0)
    nt = n // QKV_TN
    return pl.pallas_call(
        _qkv_kernel, grid=(n_tok // tm,),
        in_specs=[pl.BlockSpec((tm, D), lambda i: (i, 0)),
                  pl.BlockSpec((1, D), lambda i: (0, 0)),
                  pl.BlockSpec((None, 1, D), mrow), pl.BlockSpec((None, 1, D), mrow),
                  pl.BlockSpec((D, n), lambda i: (0, 0)),
                  pl.BlockSpec((nt, 1, QKV_TN), lambda i: (0, 0, 0)),
                  pl.BlockSpec((tm, QKV_TN), rrow), pl.BlockSpec((tm, QKV_TN), rrow)],
        out_specs=pl.BlockSpec((tm, n), lambda i: (i, 0)),
        out_shape=jax.ShapeDtypeStruct((n_tok, n), F32),
        compiler_params=_cp(("arbitrary",)), name="qkv_proj",
    )(x, g.reshape(1, D), sc, sh, w, nw.reshape(nt, 1, QKV_TN), cos, sin)


def _dup_group(x, g):
    blk = x[:, LANES * (g // 2):LANES * (g // 2 + 1)]
    if g % 2 == 1:
        blk = pltpu.roll(blk, ATT_HD, 1)
    lo = lax.broadcasted_iota(I32, blk.shape, 1) < ATT_HD
    low = jnp.where(lo, blk, 0.0)
    return low + pltpu.roll(low, ATT_HD, 1)


def _attend(q_ref, k_all, v_all, mask, sink_ref, o_ref):
    nq = q_ref.shape[0]
    lo = lax.broadcasted_iota(I32, (nq, LANES), 1) < ATT_HD
    grp = ATT_HEADS // ATT_KV
    for g in range(ATT_KV):
        k2 = _dup_group(k_all, g).astype(BF16)
        v2 = _dup_group(v_all, g).astype(BF16)
        for jp in range(grp // 2):
            j = g * (grp // 2) + jp
            qp = q_ref[:, LANES * j:LANES * (j + 1)] * (ATT_HD ** -0.5)
            outs = []
            for half in range(2):
                qh = jnp.where(lo, qp, 0.0) if half == 0 else jnp.where(lo, 0.0, qp)
                s = _dot_nt(qh.astype(BF16), k2)
                if mask is not None:
                    s = jnp.where(mask, s, NEG)
                sink = sink_ref[2 * j + half]
                m = jnp.maximum(jnp.max(s, axis=1, keepdims=True), sink)
                p = jnp.exp(s - m)
                den = jnp.sum(p, axis=1, keepdims=True) + jnp.exp(sink - m)
                outs.append(_dot(p.astype(BF16), v2) / den)
            o_ref[:, LANES * j:LANES * (j + 1)] = jnp.where(lo, outs[0], outs[1]).astype(o_ref.dtype)


def _attn_ctx_kernel(sink_ref, q_ref, k_ref, v_ref, o_ref):
    _attend(q_ref, k_ref[...], v_ref[...], None, sink_ref, o_ref)


def attn_context(lay, qkv, sinks):
    qw = ATT_HEADS * ATT_HD
    kw = ATT_KV * ATT_HD
    ln = lay.prompt_len
    grid_spec = pltpu.PrefetchScalarGridSpec(
        num_scalar_prefetch=0, grid=(lay.n_prompt,),
        in_specs=[pl.BlockSpec(memory_space=pltpu.SMEM),
                  pl.BlockSpec((ln, qw), lambda b: (b, 0)),
                  pl.BlockSpec((ln, kw), lambda b: (b, qw // kw)),
                  pl.BlockSpec((ln, kw), lambda b: (b, qw // kw + 1))],
        out_specs=pl.BlockSpec((ln, qw), lambda b: (b, 0)))
    return pl.pallas_call(
        _attn_ctx_kernel, grid_spec=grid_spec,
        out_shape=jax.ShapeDtypeStruct((lay.p_tok, qw), BF16),
        compiler_params=_cp(("arbitrary",)), name="attn_context",
    )(sinks, qkv, qkv, qkv)


def _attn_lat_kernel(sink_ref, q_ref, kp_ref, kc_ref, kn_ref, vp_ref, vc_ref, vn_ref, ck_ref, cv_ref, o_ref,
                     *, nblk):
    i = pl.program_id(1)
    bq = ATT_BLOCK
    nctx = ck_ref.shape[1]
    k_all = jnp.concatenate([kp_ref[...], kc_ref[...], kn_ref[...], ck_ref[0]], axis=0)
    v_all = jnp.concatenate([vp_ref[...], vc_ref[...], vn_ref[...], cv_ref[0]], axis=0)
    ns = 3 * bq + nctx
    r = lax.broadcasted_iota(I32, (bq, ns), 0)
    c = lax.broadcasted_iota(I32, (bq, ns), 1)
    rel = c - r
    first_key = jnp.where(i > 0, 0, bq)
    end_key = jnp.where(i < nblk - 1, 3 * bq, 2 * bq)
    band = (rel >= bq - WINDOW) & (rel <= bq + WINDOW) & (c >= first_key) & (c < end_key)
    mask = band | (c >= 3 * bq)
    _attend(q_ref, k_all, v_all, mask, sink_ref, o_ref)


def attn_latent(lay, qkv, cache_k, cache_v, sinks):
    qw = ATT_HEADS * ATT_HD
    kw = ATT_KV * ATT_HD
    bq = ATT_BLOCK
    nblk = lay.sample_len // bq
    b0 = lay.p_tok // bq
    nctx = cache_k.shape[1]
    rb = lambda b, i: b0 + b * nblk + i
    kspec = lambda cb, sh: pl.BlockSpec(
        (bq, kw), lambda b, i: (b0 + b * nblk + jnp.clip(i + sh, 0, nblk - 1), cb))
    kc, vc = qw // kw, qw // kw + 1
    grid_spec = pltpu.PrefetchScalarGridSpec(
        num_scalar_prefetch=0, grid=(lay.n_sample, nblk),
        in_specs=[pl.BlockSpec(memory_space=pltpu.SMEM),
                  pl.BlockSpec((bq, qw), lambda b, i: (rb(b, i), 0)),
                  kspec(kc, -1), kspec(kc, 0), kspec(kc, 1),
                  kspec(vc, -1), kspec(vc, 0), kspec(vc, 1),
                  pl.BlockSpec((1, nctx, kw), lambda b, i: (b, 0, 0)),
                  pl.BlockSpec((1, nctx, kw), lambda b, i: (b, 0, 0))],
        out_specs=pl.BlockSpec((bq, qw), lambda b, i: (b * nblk + i, 0)))
    return pl.pallas_call(
        functools.partial(_attn_lat_kernel, nblk=nblk), grid_spec=grid_spec,
        out_shape=jax.ShapeDtypeStruct((lay.n_sample * lay.sample_len, qw), BF16),
        compiler_params=_cp(("arbitrary", "arbitrary")), name="attn_latent",
    )(sinks, qkv, qkv, qkv, qkv, qkv, qkv, qkv,
      cache_k.reshape(lay.n_sample, nctx, kw), cache_v.reshape(lay.n_sample, nctx, kw))


def _forward(lay, x_prompt, x_sample, state_l0_ssd_fwd, state_l0_ssd_bwd, state_l0_gla_fwd, state_l0_gla_bwd,
             cache_l1_k, cache_l1_v, c, c_ctx, ada_w, ada_b, norm1, norm2,
             l0_w_in, l0_conv_w, l0_conv_b, l0_a_log, l0_dt_bias, l0_d_skip, l0_ssd_norm,
             l0_gate_w2, l0_gate_b, l0_gla_norm, l0_w_out,
             l1_w_qkv, l1_q_norm, l1_k_norm, l1_sinks, l1_w_out,
             router_w, router_b, exp_w_gate, exp_b_gate, exp_w_up, exp_b_up, exp_w_down, exp_b_down):
    np_, ns = lay.n_prompt, lay.n_sample
    x = (x_prompt.reshape(-1, D), x_sample.reshape(-1, D))
    cond8 = jnp.zeros((8, D), F32).at[0].set(c_ctx).at[1:1 + ns].set(c)
    mod = ada_table(cond8, ada_w, ada_b)
    mods = [[mod[l, :, p * D:(p + 1) * D].reshape(8, 1, D) for p in range(N_ADA)] for l in range(2)]

    def moe(l, xx, split=False):
        return moe_layer(lay, l, xx, norm2[l], mods[l][4], mods[l][3], mods[l][5], router_w[l], router_b[l],
                         exp_w_gate, exp_b_gate, exp_w_up, exp_b_up, exp_w_down, exp_b_down, split=split)

    sp = np.cumsum((SSD_INNER, SSD_INNER + 2 * SSD_GROUPS * SSD_STATE, 2 * SSD_HEADS,
                    GLA_HEADS * GLA_KEY_DIM, GLA_HEADS * GLA_KEY_DIM,
                    GLA_HEADS * GLA_VAL_DIM, GLA_HEADS * GLA_VAL_DIM, 2 * GLA_RANK))
    cols = lambda a, b: l0_w_in[:, a:b]
    w_main = jnp.concatenate([cols(0, sp[0]), cols(sp[4], sp[5]), cols(sp[5], sp[6]), cols(sp[0], sp[1]),
                              cols(sp[2], sp[3]), cols(sp[3], sp[4])], axis=1).astype(BF16)
    w_small = jnp.concatenate([cols(sp[1], sp[2]), cols(sp[6], sp[7]),
                               jnp.zeros((D, LANES - 2 * SSD_HEADS - 2 * GLA_RANK), F32)], axis=1)
    proj, small, small_t = norm_proj(lay, x, norm1[0], mods[0][1], mods[0][0], w_main, _hilo(w_small), 512,
                                      PJ_W // 2, BF16, SSD_CHUNK)
    y_n, o_n, (ssd_f, ssd_b, gla_f, gla_b) = l0_mixers(
        lay, proj, small, small_t, state_l0_ssd_fwd, state_l0_ssd_bwd, state_l0_gla_fwd, state_l0_gla_bwd,
        l0_conv_w, l0_conv_b, l0_a_log, l0_dt_bias, l0_d_skip, l0_ssd_norm,
        l0_gate_w2, l0_gate_b, l0_gla_norm)
    x = proj_residual(lay, [y_n, o_n], l0_w_out.astype(BF16), x, mods[0][2])
    x = moe(0, x)

    qkv = qkv_proj(lay, x, norm1[1], mods[1][1], mods[1][0], l1_w_qkv.astype(BF16), l1_q_norm, l1_k_norm)
    o_ctx = attn_context(lay, qkv, l1_sinks)
    o_lat = attn_latent(lay, qkv, cache_l1_k, cache_l1_v, l1_sinks)
    x = proj_residual(lay, [(o_ctx, o_lat)], l1_w_out.astype(BF16), x, mods[1][2])
    xp, xs = moe(1, x, split=True)

    qw = ATT_HEADS * ATT_HD
    kw = ATT_KV * ATT_HD
    return (xp.reshape(x_prompt.shape), xs.reshape(x_sample.shape),
            ssd_f[:np_].reshape(np_, SSD_HEADS, SSD_HEAD_DIM, SSD_STATE),
            ssd_b[:np_].reshape(np_, SSD_HEADS, SSD_HEAD_DIM, SSD_STATE),
            gla_f[:np_], gla_b[:np_],
            qkv[:lay.p_tok, qw:qw + kw].reshape(np_, lay.prompt_len, ATT_KV, ATT_HD),
            qkv[:lay.p_tok, qw + kw:].reshape(np_, lay.prompt_len, ATT_KV, ATT_HD))


def kernel(x_prompt, x_sample, state_l0_ssd_fwd, state_l0_ssd_bwd, state_l0_gla_fwd, state_l0_gla_bwd, cache_l1_k, cache_l1_v, c, c_ctx, ada_w, ada_b, norm1, norm2, l0_w_in, l0_conv_w, l0_conv_b, l0_a_log, l0_dt_bias, l0_d_skip, l0_ssd_norm, l0_gate_w2, l0_gate_b, l0_gla_norm, l0_w_out, l1_w_qkv, l1_q_norm, l1_k_norm, l1_sinks, l1_w_out, router_w, router_b, exp_w_gate, exp_b_gate, exp_w_up, exp_b_up, exp_w_down, exp_b_down):
    lay = Layout(x_prompt.shape[0], x_prompt.shape[1], x_sample.shape[0], x_sample.shape[1])
    return _forward(lay, x_prompt, x_sample, state_l0_ssd_fwd, state_l0_ssd_bwd, state_l0_gla_fwd,
                    state_l0_gla_bwd, cache_l1_k, cache_l1_v, c, c_ctx, ada_w, ada_b, norm1, norm2,
                    l0_w_in, l0_conv_w, l0_conv_b, l0_a_log, l0_dt_bias, l0_d_skip, l0_ssd_norm,
                    l0_gate_w2, l0_gate_b, l0_gla_norm, l0_w_out,
                    l1_w_qkv, l1_q_norm, l1_k_norm, l1_sinks, l1_w_out,
                    router_w, router_b, exp_w_gate, exp_b_gate, exp_w_up, exp_b_up, exp_w_down, exp_b_down)
```

```python
import functools
import math

import numpy as np
import jax
import jax.numpy as jnp
from jax import lax
from jax.experimental import pallas as pl
from jax.experimental.pallas import tpu as pltpu

F32 = jnp.float32
BF16 = jnp.bfloat16
I32 = jnp.int32
HI = lax.Precision.HIGHEST

D = 1024
EPS = 1e-6
N_ADA = 6
SSD_HEADS = 16
SSD_HEAD_DIM = 64
SSD_INNER = 1024
SSD_STATE = 128
SSD_GROUPS = 2
SSD_CONV = 5
SSD_CHUNK = 128
GLA_HEADS = 4
GLA_KEY_DIM = 128
GLA_VAL_DIM = 256
GLA_RANK = 16
GLA_TAU = 16.0
GLA_BLOCK = 64
ATT_HEADS = 16
ATT_KV = 4
ATT_HD = 64
ATT_BLOCK = 128
WINDOW = 128
GRID_W = 64
ROPE_THETA = 10000.0
N_EXPERTS = 32
TOP_K = 4
EXPERT_FF = 1024
SWIGLU_LIMIT = 7.0
SWIGLU_ALPHA = 1.702
MOE_BLOCK = 256
TOK_TILE = 256
LANES = 128
NEG = -1e30

PJ_Z, PJ_V, PJ_OG, PJ_XBC, PJ_Q, PJ_K = 0, 1024, 2048, 3072, 4608, 5120
PJ_W = 5632
VMEM_LIMIT = 48 * 1024 * 1024


def _cp(sem, vmem=VMEM_LIMIT):
    return pltpu.CompilerParams(dimension_semantics=sem, vmem_limit_bytes=vmem)


class Layout:
    def __init__(self, n_prompt, prompt_len, n_sample, sample_len):
        self.n_prompt, self.prompt_len = n_prompt, prompt_len
        self.n_sample, self.sample_len = n_sample, sample_len
        self.p_tok = n_prompt * prompt_len
        self.n_tok = self.p_tok + n_sample * sample_len
        self.seqs = [(i * prompt_len, prompt_len) for i in range(n_prompt)]
        self.seqs += [(self.p_tok + i * sample_len, sample_len) for i in range(n_sample)]
        self.n_seq = len(self.seqs)

    def mod_row(self, start):
        return jnp.where(start < self.p_tok, 0, 1 + (start - self.p_tok) // self.sample_len)

def _sigmoid(x):
    return 1.0 / (1.0 + jnp.exp(-x))


def _silu(x):
    return x * _sigmoid(x)


def _softplus(x):
    return jnp.maximum(x, 0.0) + jnp.log(1.0 + jnp.exp(-jnp.abs(x)))


def _modnorm(x, g, sc, sh):
    ms = jnp.mean(x * x, axis=-1, keepdims=True)
    return (x * lax.rsqrt(ms + EPS) * g) * (1.0 + sc) + sh


def _dot(a, b, **kw):
    return jnp.dot(a, b, preferred_element_type=F32, **kw)


def _dot_nt(a, b):
    return lax.dot_general(a, b, (((1,), (1,)), ((), ())), preferred_element_type=F32)


def _dot_tn(a, b):
    return lax.dot_general(a, b, (((0,), (0,)), ((), ())), preferred_element_type=F32)


def _split(x, n):
    parts = []
    for _ in range(n):
        p = x.astype(BF16)
        parts.append(p)
        x = x - p.astype(F32)
    return parts


def _dot_sel(sel, x):
    sel = sel.astype(BF16)
    return sum(_dot(sel, p) for p in _split(x, 3))


def _dot_sel_r(x, sel):
    sel = sel.astype(BF16)
    return sum(_dot(p, sel) for p in _split(x, 3))


def _dot_hilo(x, w_hi, w_lo):
    x_hi, x_lo = _split(x, 2)
    return _dot(x_hi, w_hi) + _dot(x_lo, w_hi) + _dot(x_hi, w_lo)


def _hilo(w):
    hi = w.astype(BF16)
    return jnp.stack([hi, (w - hi.astype(F32)).astype(BF16)])


U32 = jnp.uint32
ROW_WORDS = D // 2
_HI_MASK = 0xFFFF0000


def _pack_rows(x):
    lo = lax.bitcast_convert_type(x[:, :ROW_WORDS], U32) >> 16
    hi = lax.bitcast_convert_type(x[:, ROW_WORDS:], U32) & jnp.uint32(_HI_MASK)
    return lo | hi


def _unpack_rows(u):
    lo = lax.bitcast_convert_type(u << 16, F32)
    hi = lax.bitcast_convert_type(u & jnp.uint32(_HI_MASK), F32)
    return jnp.concatenate([lo, hi], axis=1)


def _ada_kernel(c_ref, w_ref, b_ref, o_ref):
    o_ref[0] = _dot(_silu(c_ref[...]), w_ref[0], precision=HI) + b_ref[0]


def ada_table(cond8, ada_w, ada_b):
    depth, _, n = ada_w.shape
    tn = 1536
    return pl.pallas_call(
        _ada_kernel, grid=(depth, n // tn),
        in_specs=[pl.BlockSpec((8, D), lambda l, j: (0, 0)),
                  pl.BlockSpec((1, D, tn), lambda l, j: (l, 0, j)),
                  pl.BlockSpec((1, 1, tn), lambda l, j: (l, 0, j))],
        out_specs=pl.BlockSpec((1, 8, tn), lambda l, j: (l, 0, j)),
        out_shape=jax.ShapeDtypeStruct((depth, 8, n), F32),
        compiler_params=_cp(("arbitrary", "arbitrary")), name="ada_table",
    )(cond8, ada_w, ada_b.reshape(depth, 1, n))


def _stream_specs(lay, stream, tm):
    if not isinstance(stream, (tuple, list)):
        return [stream], [pl.BlockSpec((tm, stream.shape[1]), lambda i, *_: (i, 0))]
    assert lay.p_tok % tm == 0
    npt = lay.p_tok // tm
    w = stream[0].shape[1]
    return list(stream), [pl.BlockSpec((tm, w), lambda i, *_: (jnp.minimum(i, npt - 1), 0)),
                          pl.BlockSpec((tm, w), lambda i, *_: (jnp.maximum(i - npt, 0), 0))]


def _stream_tile(refs, in_prompt):
    if len(refs) == 1:
        return refs[0][...]
    return jnp.where(in_prompt, refs[0][...], refs[1][...])


def _proj_kernel(*refs, nx, npt):
    x_refs = refs[:nx]
    g_ref, sc_ref, sh_ref, w_ref, ws_ref, o_ref, os_ref, ost_ref, h_scr = refs[nx:]

    @pl.when(pl.program_id(1) == 0)
    def _():
        x = _stream_tile(x_refs, pl.program_id(0) < npt)
        h = _modnorm(x, g_ref[...], sc_ref[...], sh_ref[...])
        h_scr[...] = h.astype(BF16)
        small = _dot_hilo(h, ws_ref[0], ws_ref[1])
        os_ref[...] = small
        q = ost_ref.shape[2]
        for c in range(ost_ref.shape[0]):
            ost_ref[c] = small[q * c:q * (c + 1), :].T

    o_ref[...] = _dot(h_scr[...], w_ref[...]).astype(o_ref.dtype)


def norm_proj(lay, x, g, sc, sh, w, w_small, tm, tn, out_dtype, chunk):
    n_tok = lay.n_tok
    n = w.shape[1]
    ns = w_small.shape[-1]
    mrow = lambda i, j: (lay.mod_row(i * tm), 0, 0)
    xs, x_specs = _stream_specs(lay, x, tm)
    return pl.pallas_call(
        functools.partial(_proj_kernel, nx=len(xs), npt=lay.p_tok // tm), grid=(n_tok // tm, n // tn),
        in_specs=x_specs + [pl.BlockSpec((1, D), lambda i, j: (0, 0)),
                            pl.BlockSpec((None, 1, D), mrow),
                            pl.BlockSpec((None, 1, D), mrow),
                            pl.BlockSpec((D, tn), lambda i, j: (0, j)),
                            pl.BlockSpec((2, D, ns), lambda i, j: (0, 0, 0))],
        out_specs=[pl.BlockSpec((tm, tn), lambda i, j: (i, j)),
                   pl.BlockSpec((tm, ns), lambda i, j: (i, 0)),
                   pl.BlockSpec((tm // chunk, ns, chunk), lambda i, j: (i, 0, 0))],
        out_shape=[jax.ShapeDtypeStruct((n_tok, n), out_dtype),
                   jax.ShapeDtypeStruct((n_tok, ns), F32),
                   jax.ShapeDtypeStruct((n_tok // chunk, ns, chunk), F32)],
        scratch_shapes=[pltpu.VMEM((tm, D), BF16)],
        compiler_params=_cp(("arbitrary", "arbitrary")), name="norm_proj",
    )(*xs, g.reshape(1, D), sc, sh, w, w_small)


CONV_HALO = 16


def _ssd_load(d, c, xc, dtg_ref, dtgT_ref, S):
    q = SSD_CHUNK
    nh = SSD_HEADS
    rows = pl.ds(pl.multiple_of(c * q, q), q)
    return (rows, xc[rows, 0:SSD_INNER], xc[rows, SSD_INNER:SSD_INNER + 2 * SSD_GROUPS * SSD_STATE],
            dtg_ref[rows, nh * d:nh * d + nh], dtgT_ref[c, nh * d:nh * d + nh, :], S[...])


def _ssd_chunk(d, loaded, alog_ref, alogT_ref, dtb_ref, dtbT_ref):
    q = SSD_CHUNK
    nh = SSD_HEADS
    _, xs, bc, dtg, dtgT, s = loaded
    xs = xs.astype(F32)
    dt = _softplus(dtg + dtb_ref[d:d + 1, :])
    dtT = _softplus(dtgT + dtbT_ref[:, d:d + 1])
    ad = dt * (-jnp.exp(alog_ref[d:d + 1, :]))
    adT = dtT * (-jnp.exp(alogT_ref[:, d:d + 1]))
    row = lax.broadcasted_iota(I32, (q, q), 0)
    col = lax.broadcasted_iota(I32, (q, q), 1)
    if d == 0:
        e = _dot_sel(col <= row, ad)
        eT = _dot_sel_r(adT, row <= col)
        tot = e[q - 1:q, :]
        mask = row >= col
        fq = jnp.exp(e)
        fk = jnp.exp(tot - e)
    else:
        e = _dot_sel(col < row, ad)
        eT = _dot_sel_r(adT, row < col)
        tot = jnp.sum(ad, axis=0, keepdims=True)
        mask = col >= row
        fq = jnp.exp(tot - e)
        fk = jnp.exp(e)
    dec = jnp.exp(tot)
    lo = lax.broadcasted_iota(I32, (q, LANES), 1) < SSD_HEAD_DIM
    hp = nh * SSD_HEAD_DIM
    head_of = lax.broadcasted_iota(I32, (nh, hp), 1) // SSD_HEAD_DIM
    spread = (lax.broadcasted_iota(I32, (nh, hp), 0) == head_of).astype(BF16)

    def per_lane(arr, passes):
        return sum(_dot(p, spread) for p in _split(arr, passes))

    dt_x, fq_x, fk_x = per_lane(dt, 1), per_lane(fq, 1), per_lane(fk, 1)
    dec_x = per_lane(jnp.broadcast_to(dec, (8, nh)), 3)[0:1, :]
    xdt_all = xs * dt_x
    xk_all = (xdt_all * fk_x).astype(BF16)
    rep = (nh // SSD_GROUPS) // 2
    ys, s_new = [], []
    for g in range(SSD_GROUPS):
        bg_t = bc[:, SSD_STATE * g:SSD_STATE * (g + 1)].T
        cg = bc[:, SSD_STATE * (SSD_GROUPS + g):SSD_STATE * (SSD_GROUPS + g + 1)]
        gmat = _dot(cg, bg_t)
        for j in range(rep * g, rep * (g + 1)):
            a = 2 * j
            sl = slice(LANES * j, LANES * (j + 1))
            parts = []
            for hh in (a, a + 1):
                if d == 0:
                    diff = e[:, hh:hh + 1] - eT[hh:hh + 1, :]
                else:
                    diff = eT[hh:hh + 1, :] - e[:, hh:hh + 1]
                parts.append((gmat * jnp.exp(jnp.where(mask, diff, NEG))).astype(BF16))
            lhs = jnp.concatenate(parts, axis=1)
            xdt = xdt_all[:, sl]
            rhs = jnp.concatenate([jnp.where(lo, xdt, 0.0), jnp.where(lo, 0.0, xdt)], axis=0)
            sj = s[:, sl]
            ys.append(_dot(lhs, rhs.astype(BF16)) + _dot(cg, sj.astype(BF16)) * fq_x[:, sl])
            s_new.append(sj * dec_x[:, sl] + _dot(bg_t, xk_all[:, sl]))
    return jnp.concatenate(ys, axis=1), jnp.concatenate(s_new, axis=1)


def _ssd_seq_kernel(*refs, has_init):
    (xbc_ref, z_ref, dtg_ref, dtgT_ref, cw_ref, cb_ref,
     alog_ref, alogT_ref, dtb_ref, dtbT_ref, dskip_ref, nrm_ref) = refs[:12]
    refs = refs[12:]
    if has_init:
        s0f_ref, s0b_ref = refs[:2]
        refs = refs[2:]
    y_ref, sf_ref, sb_ref, xc, ext, Sf, Sb, yf, yb = refs
    ln = xbc_ref.shape[0]
    q = SSD_CHUNK
    nc = ln // q
    h = CONV_HALO
    pad = SSD_CONV // 2

    def conv_body(c, carry):
        r0 = pl.multiple_of(c * q, q)
        prev = xbc_ref[pl.ds(pl.multiple_of(jnp.maximum(r0 - h, 0), h), h), :].astype(F32)
        nxt = xbc_ref[pl.ds(pl.multiple_of(jnp.minimum(r0 + q, ln - h), h), h), :].astype(F32)
        ext[0:h, :] = jnp.where(c > 0, prev, 0.0)
        ext[h:h + q, :] = xbc_ref[pl.ds(r0, q), :].astype(F32)
        ext[h + q:h + q + h, :] = jnp.where(c < nc - 1, nxt, 0.0)
        acc = jnp.broadcast_to(cb_ref[...], (q, cb_ref.shape[1]))
        for k in range(SSD_CONV):
            acc = acc + cw_ref[k:k + 1, :] * ext[h - pad + k:h - pad + k + q, :]
        xc[pl.ds(r0, q), :] = _silu(acc).astype(xc.dtype)
        return carry
    lax.fori_loop(0, nc, conv_body, 0)

    if has_init:
        Sf[...] = s0f_ref[0].T
        Sb[...] = s0b_ref[0].T
    else:
        Sf[...] = jnp.zeros_like(Sf)
        Sb[...] = jnp.zeros_like(Sb)
    params = (alog_ref, alogT_ref, dtb_ref, dtbT_ref)

    def scan_body(c, carry):
        lf = _ssd_load(0, c, xc, dtg_ref, dtgT_ref, Sf)
        lb = _ssd_load(1, nc - 1 - c, xc, dtg_ref, dtgT_ref, Sb)
        y_f, s_f = _ssd_chunk(0, lf, *params)
        y_b, s_b = _ssd_chunk(1, lb, *params)
        yf[lf[0], :] = y_f
        yb[lb[0], :] = y_b
        Sf[...] = s_f
        Sb[...] = s_b
        return carry
    lax.fori_loop(0, nc, scan_body, 0, unroll=2)
    sf_ref[0] = Sf[...].T
    sb_ref[0] = Sb[...].T

    def out_body(c, carry):
        rows = pl.ds(pl.multiple_of(c * q, q), q)
        ytot = yf[rows, :] + yb[rows, :] + dskip_ref[...] * xc[rows, 0:SSD_INNER].astype(F32)
        yg = ytot * _silu(z_ref[rows, :].astype(F32))
        ms = jnp.mean(yg * yg, axis=-1, keepdims=True)
        y_ref[rows, :] = (yg * lax.rsqrt(ms + EPS) * nrm_ref[...]).astype(y_ref.dtype)
        return carry
    lax.fori_loop(0, nc, out_body, 0)


def _ssd_call(n_seq, ln, blk0, proj, small, smallT3, init, params):
    q = SSD_CHUNK
    hp = SSD_HEADS * SSD_HEAD_DIM
    cw = SSD_INNER + 2 * SSD_GROUPS * SSD_STATE
    nc = ln // q
    assert PJ_XBC % cw == 0 and PJ_Z % SSD_INNER == 0
    tok = lambda w, cb: pl.BlockSpec((ln, w), lambda b: (blk0 + b, cb))
    seq3 = pl.BlockSpec((1, hp, SSD_STATE), lambda b: (b, 0, 0))
    full = lambda a: pl.BlockSpec(a.shape, lambda b: (0,) * a.ndim)
    init = () if init is None else tuple(init)
    return pl.pallas_call(
        functools.partial(_ssd_seq_kernel, has_init=bool(init)), grid=(n_seq,),
        in_specs=[tok(cw, PJ_XBC // cw), tok(SSD_INNER, PJ_Z // SSD_INNER), tok(LANES, 0),
                  pl.BlockSpec((nc, 2 * SSD_HEADS, q), lambda b: (blk0 + b, 0, 0))]
        + [full(a) for a in params] + [seq3] * len(init),
        out_specs=[pl.BlockSpec((ln, hp), lambda b: (b, 0)), seq3, seq3],
        out_shape=[jax.ShapeDtypeStruct((n_seq * ln, hp), BF16),
                   jax.ShapeDtypeStruct((n_seq, hp, SSD_STATE), F32),
                   jax.ShapeDtypeStruct((n_seq, hp, SSD_STATE), F32)],
        scratch_shapes=[pltpu.VMEM((ln, cw), BF16), pltpu.VMEM((q + 2 * CONV_HALO, cw), F32),
                        pltpu.VMEM((SSD_STATE, hp), F32), pltpu.VMEM((SSD_STATE, hp), F32),
                        pltpu.VMEM((ln, hp), F32), pltpu.VMEM((ln, hp), F32)],
        compiler_params=_cp(("arbitrary",)), name="ssd_seq",
    )(proj, proj, small, smallT3, *params, *init)


def _gla_load(d, c, q_ref, k_ref, v_ref, glr_ref, S):
    t = GLA_BLOCK
    rows = pl.ds(pl.multiple_of(c * t, t), t)
    c0 = 2 * SSD_HEADS + GLA_RANK * d
    return (rows, q_ref[rows, :], k_ref[rows, :], v_ref[rows, :], glr_ref[rows, c0:c0 + GLA_RANK],
            [S[h] for h in range(GLA_HEADS)])


def _gla_block(d, loaded, w2_ref, gb_ref):
    t = GLA_BLOCK
    dk, dv = GLA_KEY_DIM, GLA_VAL_DIM
    _, q, k, v, glr, states = loaded
    gp = _dot_hilo(glr, w2_ref[0, d], w2_ref[1, d]) + gb_ref[d:d + 1, :]
    la = -_softplus(-gp) * (1.0 / GLA_TAU)
    row = lax.broadcasted_iota(I32, (t, t), 0)
    col = lax.broadcasted_iota(I32, (t, t), 1)
    mid = t // 2 - 1
    if d == 0:
        e = _dot_sel(col <= row, la)
        tot = e[t - 1:t, :]
        r = e[mid:mid + 1, :]
        fqi, fki = jnp.exp(e - r), jnp.exp(r - e)
        fq, fk = jnp.exp(e), jnp.exp(tot - e)
        mask = row >= col
    else:
        e = _dot_sel(col < row, la)
        tot = e[t - 1:t, :] + la[t - 1:t, :]
        r = e[mid:mid + 1, :]
        fqi, fki = jnp.exp(r - e), jnp.exp(e - r)
        fq, fk = jnp.exp(tot - e), jnp.exp(e)
        mask = col >= row
    dec = jnp.exp(tot)
    qf = q.astype(F32) * (dk ** -0.5)
    kf = k.astype(F32)
    outs, new_states = [], []
    for h in range(GLA_HEADS):
        sl = slice(dk * h, dk * (h + 1))
        qh, kh = qf[:, sl], kf[:, sl]
        sc = _dot_nt((qh * fqi[:, sl]).astype(BF16), (kh * fki[:, sl]).astype(BF16))
        sc = jnp.where(mask, sc, 0.0)
        vh = v[:, dv * h:dv * (h + 1)]
        st = states[h]
        outs.append(_dot(sc.astype(BF16), vh) + _dot_nt((qh * fq[:, sl]).astype(BF16), st.astype(BF16)))
        new_states.append(st * dec[:, sl] + _dot_tn(vh, (kh * fk[:, sl]).astype(BF16)))
    return jnp.concatenate(outs, axis=1), new_states


def _gla_seq_kernel(*refs, has_init):
    q_ref, k_ref, v_ref, og_ref, glr_ref, w2_ref, gb_ref, nrm_ref = refs[:8]
    refs = refs[8:]
    if has_init:
        s0f_ref, s0b_ref = refs[:2]
        refs = refs[2:]
    o_ref, sf_ref, sb_ref, Sf, Sb, of, ob = refs
    ln = q_ref.shape[0]
    t = GLA_BLOCK
    nc = ln // t
    dv = GLA_VAL_DIM
    for h in range(GLA_HEADS):
        if has_init:
            Sf[h] = s0f_ref[0, h].T
            Sb[h] = s0b_ref[0, h].T
        else:
            Sf[h] = jnp.zeros(Sf.shape[1:], F32)
            Sb[h] = jnp.zeros(Sb.shape[1:], F32)
    ins = (q_ref, k_ref, v_ref, glr_ref)

    def scan_body(c, carry):
        lf = _gla_load(0, c, *ins, Sf)
        lb = _gla_load(1, nc - 1 - c, *ins, Sb)
        o_f, s_f = _gla_block(0, lf, w2_ref, gb_ref)
        o_b, s_b = _gla_block(1, lb, w2_ref, gb_ref)
        of[lf[0], :] = o_f
        ob[lb[0], :] = o_b
        for h in range(GLA_HEADS):
            Sf[h] = s_f[h]
            Sb[h] = s_b[h]
        return carry
    lax.fori_loop(0, nc, scan_body, 0, unroll=2)
    for h in range(GLA_HEADS):
        sf_ref[0, h] = Sf[h].T
        sb_ref[0, h] = Sb[h].T

    def out_body(c, carry):
        rows = pl.ds(pl.multiple_of(c * t, t), t)
        for h in range(GLA_HEADS):
            vl = slice(dv * h, dv * (h + 1))
            ot = of[rows, vl] + ob[rows, vl]
            ms = jnp.mean(ot * ot, axis=-1, keepdims=True)
            on = ot * lax.rsqrt(ms + EPS) * nrm_ref[...]
            o_ref[rows, vl] = (on * _silu(og_ref[rows, vl].astype(F32))).astype(o_ref.dtype)
        return carry
    lax.fori_loop(0, nc, out_body, 0)


def _gla_call(n_seq, ln, blk0, proj, small, init, params):
    qk_w = GLA_HEADS * GLA_KEY_DIM
    v_w = GLA_HEADS * GLA_VAL_DIM
    tok = lambda w, cb: pl.BlockSpec((ln, w), lambda b: (blk0 + b, cb))
    seq4 = pl.BlockSpec((1, GLA_HEADS, GLA_KEY_DIM, GLA_VAL_DIM), lambda b: (b, 0, 0, 0))
    full = lambda a: pl.BlockSpec(a.shape, lambda b: (0,) * a.ndim)
    st_shape = jax.ShapeDtypeStruct((n_seq, GLA_HEADS, GLA_KEY_DIM, GLA_VAL_DIM), F32)
    init = () if init is None else tuple(init)
    return pl.pallas_call(
        functools.partial(_gla_seq_kernel, has_init=bool(init)), grid=(n_seq,),
        in_specs=[tok(qk_w, PJ_Q // qk_w), tok(qk_w, PJ_K // qk_w), tok(v_w, PJ_V // v_w),
                  tok(v_w, PJ_OG // v_w), tok(LANES, 0)] + [full(a) for a in params] + [seq4] * len(init),
        out_specs=[pl.BlockSpec((ln, v_w), lambda b: (b, 0)), seq4, seq4],
        out_shape=[jax.ShapeDtypeStruct((n_seq * ln, v_w), BF16), st_shape, st_shape],
        scratch_shapes=[pltpu.VMEM((GLA_HEADS, GLA_VAL_DIM, GLA_KEY_DIM), F32),
                        pltpu.VMEM((GLA_HEADS, GLA_VAL_DIM, GLA_KEY_DIM), F32),
                        pltpu.VMEM((ln, v_w), F32), pltpu.VMEM((ln, v_w), F32)],
        compiler_params=_cp(("arbitrary",)), name="gla_seq",
    )(proj, proj, proj, proj, small, *params, *init)


def l0_mixers(lay, proj, small, small_t, ssd_f0, ssd_b0, gla_f0, gla_b0, conv_w, conv_b, a_log, dt_bias, d_skip,
              ssd_norm, gate_w2, gate_b, gla_norm):
    hp = SSD_HEADS * SSD_HEAD_DIM
    ssd_p = (conv_w, conv_b.reshape(1, -1), a_log, a_log.T, dt_bias, dt_bias.T,
             jnp.repeat(d_skip, SSD_HEAD_DIM).reshape(1, hp), ssd_norm.reshape(1, hp))
    gla_p = (_hilo(gate_w2), gate_b, gla_norm.reshape(1, -1))
    np_, ns = lay.n_prompt, lay.n_sample
    assert lay.p_tok % lay.sample_len == 0
    groups = [(np_, lay.prompt_len, 0, None, None),
              (ns, lay.sample_len, lay.p_tok // lay.sample_len,
               (ssd_f0.reshape(ns, hp, SSD_STATE), ssd_b0.reshape(ns, hp, SSD_STATE)), (gla_f0, gla_b0))]
    ys, os_, states = [], [], None
    for n, ln, blk0, ssd_init, gla_init in groups:
        y, sf, sb = _ssd_call(n, ln, blk0, proj, small, small_t, ssd_init, ssd_p)
        o, gf, gb = _gla_call(n, ln, blk0, proj, small, gla_init, gla_p)
        ys.append(y)
        os_.append(o)
        if states is None:
            states = (sf, sb, gf, gb)
    return tuple(ys), tuple(os_), states


def _res_kernel(*refs, counts, ks, npt):
    in_prompt = pl.program_id(0) < npt
    streams, pos = [], 0
    for c in counts:
        streams.append(refs[pos:pos + c])
        pos += c
    w_ref, gate_ref, o_ref = refs[pos:]
    acc = None
    off = 0
    for a_refs, k in zip(streams[:-1], ks):
        part = _dot(_stream_tile(a_refs, in_prompt), w_ref[off:off + k, :])
        acc = part if acc is None else acc + part
        off += k
    o_ref[...] = _stream_tile(streams[-1], in_prompt) + gate_ref[...] * acc


def proj_residual(lay, acts, w, x, gate, tm=512):
    arrays, specs, counts = [], [], []
    for s in list(acts) + [x]:
        a, sp = _stream_specs(lay, s, tm)
        arrays += a
        specs += sp
        counts.append(len(a))
    ks = tuple(int((a[0] if isinstance(a, (tuple, list)) else a).shape[1]) for a in acts)
    mrow = lambda i: (lay.mod_row(i * tm), 0, 0)
    return pl.pallas_call(
        functools.partial(_res_kernel, counts=tuple(counts), ks=ks, npt=lay.p_tok // tm),
        grid=(lay.n_tok // tm,),
        in_specs=specs + [pl.BlockSpec(w.shape, lambda i: (0, 0)), pl.BlockSpec((None, 1, D), mrow)],
        out_specs=pl.BlockSpec((tm, D), lambda i: (i, 0)),
        out_shape=jax.ShapeDtypeStruct((lay.n_tok, D), F32),
        compiler_params=_cp(("arbitrary",)), name="proj_residual",
    )(*arrays, w, gate)


def _router_kernel(x_ref, g_ref, sc_ref, sh_ref, rw_ref, rb_ref,
                   h_ref, idx_ref, gate_ref, pos_ref, posT_ref, cnt_ref):
    tm = x_ref.shape[0]
    h = _modnorm(x_ref[...], g_ref[...], sc_ref[...], sh_ref[...])
    h_hi = h.astype(BF16)
    h_ref[...] = h_hi
    h_lo = (h - h_hi.astype(F32)).astype(BF16)
    lg = (_dot(h_hi, rw_ref[0]) + _dot(h_lo, rw_ref[0]) + _dot(h_hi, rw_ref[1])
          + rb_ref[...])
    lane = lax.broadcasted_iota(I32, (tm, LANES), 1).astype(F32)
    vals, ids = [], []
    for _ in range(TOP_K):
        m = jnp.max(lg, axis=1, keepdims=True)
        i = jnp.min(jnp.where(lg == m, lane, float(LANES)), axis=1, keepdims=True)
        vals.append(m)
        ids.append(i)
        lg = jnp.where(lane == i, -jnp.inf, lg)
    ex = [jnp.exp(v - vals[0]) for v in vals]
    den = ex[0] + ex[1] + ex[2] + ex[3]
    sel = jnp.zeros((tm, LANES), F32)
    for i in ids:
        sel = sel + (lane == i).astype(F32)
    row = lax.broadcasted_iota(I32, (tm, tm), 0)
    col = lax.broadcasted_iota(I32, (tm, tm), 1)
    before = _dot((col < row).astype(BF16), sel.astype(BF16))
    n = jnp.sum(sel, axis=0, keepdims=True)
    er = lax.broadcasted_iota(I32, (LANES, LANES), 0)
    ec = lax.broadcasted_iota(I32, (LANES, LANES), 1)
    n_al = jnp.ceil(n * (1.0 / SEG_ALIGN)) * SEG_ALIGN
    offs = _dot(jnp.broadcast_to(n_al, (8, LANES)).astype(BF16), (er < ec).astype(BF16))[0:1, :]
    slot = before + offs
    idx_o = jnp.zeros((tm, LANES), F32)
    gate_o = jnp.zeros((tm, LANES), F32)
    pos_o = jnp.zeros((tm, LANES), F32)
    for k in range(TOP_K):
        p = jnp.sum(jnp.where(lane == ids[k], slot, 0.0), axis=1, keepdims=True)
        idx_o = jnp.where(lane == k, ids[k], idx_o)
        gate_o = jnp.where(lane == k, ex[k] / den, gate_o)
        pos_o = jnp.where(lane == k, p, pos_o)
    idx_ref[...] = idx_o.astype(I32)
    gate_ref[...] = gate_o
    pos_ref[...] = pos_o.astype(I32)
    posT_ref[...] = pos_o.T[0:8, :]
    cnt_ref[0] = jnp.broadcast_to(n, (8, LANES))


def moe_router(lay, x, g, sc, sh, rw, rb):
    n_tok = x.shape[0]
    tm = TOK_TILE
    nt = n_tok // tm
    mrow = lambda i: (lay.mod_row(i * tm), 0, 0)
    tile = lambda w, dt: (pl.BlockSpec((tm, w), lambda i: (i, 0)), jax.ShapeDtypeStruct((n_tok, w), dt))
    outs = [tile(D, BF16), tile(LANES, I32), tile(LANES, F32), tile(LANES, I32),
            (pl.BlockSpec((8, tm), lambda i: (0, i)), jax.ShapeDtypeStruct((8, n_tok), F32)),
            (pl.BlockSpec((1, 8, LANES), lambda i: (i, 0, 0)), jax.ShapeDtypeStruct((nt, 8, LANES), F32))]
    return pl.pallas_call(
        _router_kernel, grid=(nt,),
        in_specs=[pl.BlockSpec((tm, D), lambda i: (i, 0)),
                  pl.BlockSpec((1, D), lambda i: (0, 0)),
                  pl.BlockSpec((None, 1, D), mrow), pl.BlockSpec((None, 1, D), mrow),
                  pl.BlockSpec((2, D, LANES), lambda i: (0, 0, 0)),
                  pl.BlockSpec((1, LANES), lambda i: (0, 0))],
        out_specs=[o[0] for o in outs], out_shape=[o[1] for o in outs],
        compiler_params=_cp(("arbitrary",)), name="moe_router",
    )(x, g.reshape(1, D), sc, sh, rw, rb)


SEG_ALIGN = 8
SEG_BITS = tuple(range(int(math.log2(TOK_TILE)), int(math.log2(SEG_ALIGN)) - 1, -1))
TILE_ROWS = TOK_TILE * TOP_K + N_EXPERTS * SEG_ALIGN


def _pow2_copies(n, src, dst, make_copy, op, bits):
    for b in bits:
        sz = 1 << b
        done = (n >> (b + 1)) << (b + 1)

        @pl.when((n & sz) != 0)
        def _():
            op(make_copy(pl.multiple_of(src + done, SEG_ALIGN), pl.multiple_of(dst + done, SEG_ALIGN), sz))


def _start_segments(i, n_ref, off_ref, dst_ref, make_copy):
    def body(e, carry):
        k = i * N_EXPERTS + e
        _pow2_copies(n_ref[k], off_ref[k], dst_ref[k], make_copy, lambda c: c.start(), SEG_BITS)
        return carry
    lax.fori_loop(0, N_EXPERTS, body, 0)


TAIL_BITS = tuple(range(int(math.log2(MOE_BLOCK)) - 1, int(math.log2(SEG_ALIGN)) - 1, -1))
TILE_BITS = tuple(range(int(math.log2(TILE_ROWS)), int(math.log2(SEG_ALIGN)) - 1, -1))


def _wait_rows(total, make_copy):
    _pow2_copies(total, 0, 0, make_copy, lambda c: c.wait(), TILE_BITS)


def _dispatch_kernel(n_ref, off_ref, dst_ref, tot_ref, tn_ref, td_ref, posT_ref, h_ref, xout_ref,
                     srt, zbuf, sems):
    i = pl.program_id(0)
    last = pl.num_programs(0) - 1
    slot = i % 2
    tm = h_ref.shape[0]
    r = lax.broadcasted_iota(I32, (TILE_ROWS, tm), 0)
    hit = jnp.zeros((TILE_ROWS, tm), jnp.bool_)
    for k in range(TOP_K):
        hit = hit | (r == posT_ref[k:k + 1, :].astype(I32))
    sel = jnp.where(hit, 1.0, 0.0).astype(BF16)
    srt[slot] = _pack_rows(_dot(sel, h_ref[...]))

    def copier(s):
        def make_copy(src, dst, sz):
            return pltpu.make_async_copy(srt.at[s, pl.ds(src, sz)], xout_ref.at[pl.ds(dst, sz)], sems.at[s])
        return make_copy

    _start_segments(i, n_ref, off_ref, dst_ref, copier(slot))

    @pl.when(i > 0)
    def _():
        _wait_rows(tot_ref[jnp.maximum(i - 1, 0)], copier(1 - slot))

    @pl.when(i == last)
    def _():
        _wait_rows(tot_ref[i], copier(slot))
        zbuf[...] = jnp.zeros_like(zbuf)
        sem = sems.at[0]

        def zero_copy(src, dst, sz):
            return pltpu.make_async_copy(zbuf.at[pl.ds(src, sz)], xout_ref.at[pl.ds(dst, sz)], sem)

        nb = xout_ref.shape[0] // MOE_BLOCK
        for op in (lambda c: c.start(), lambda c: c.wait()):
            def body(e, carry):
                _pow2_copies(tn_ref[e], 0, td_ref[e], zero_copy, op, TAIL_BITS)
                return carry
            lax.fori_loop(0, N_EXPERTS, body, 0)

            def unused(b, carry):
                op(zero_copy(0, pl.multiple_of(b * MOE_BLOCK, MOE_BLOCK), MOE_BLOCK))
                return carry
            lax.fori_loop(tn_ref[N_EXPERTS], nb, unused, 0)


def moe_dispatch(n_tab, off_tab, dst_tab, tot_tab, tail_n, tail_dst, posT, h2, n_rows):
    n_tok = h2.shape[0]
    tm = TOK_TILE
    grid_spec = pltpu.PrefetchScalarGridSpec(
        num_scalar_prefetch=6, grid=(n_tok // tm,),
        in_specs=[pl.BlockSpec((8, tm), lambda i, *_: (0, i)),
                  pl.BlockSpec((tm, D), lambda i, *_: (i, 0))],
        out_specs=pl.BlockSpec(memory_space=pl.ANY),
        scratch_shapes=[pltpu.VMEM((2, TILE_ROWS, ROW_WORDS), U32), pltpu.VMEM((MOE_BLOCK, ROW_WORDS), U32),
                        pltpu.SemaphoreType.DMA((2,))])
    return pl.pallas_call(
        _dispatch_kernel, grid_spec=grid_spec,
        out_shape=jax.ShapeDtypeStruct((n_rows, ROW_WORDS), U32),
        compiler_params=_cp(("arbitrary",)), name="moe_dispatch",
    )(n_tab, off_tab, dst_tab, tot_tab, tail_n, tail_dst, posT, h2)


def _combine_kernel(n_ref, off_ref, dst_ref, tot_ref, pos_ref, gate_ref, x_ref, g2_ref, y_ref, *rest, npt):
    o_refs, (buf, sems) = rest[:-2], rest[-2:]
    i = pl.program_id(0)
    last = pl.num_programs(0) - 1
    slot = i % 2
    tm = x_ref.shape[0]
    na = TILE_ROWS

    def copier(s):
        def make_copy(src, dst, sz):
            return pltpu.make_async_copy(y_ref.at[pl.ds(dst, sz)], buf.at[s, pl.ds(src, sz)], sems.at[s])
        return make_copy

    def fetch(tile, s):
        buf[s, tm * TOP_K:na, :] = jnp.zeros((na - tm * TOP_K, ROW_WORDS), U32)
        _start_segments(tile, n_ref, off_ref, dst_ref, copier(s))

    @pl.when(i == 0)
    def _():
        fetch(i, slot)

    @pl.when(i < last)
    def _():
        fetch(i + 1, 1 - slot)

    _wait_rows(tot_ref[i], copier(slot))
    lane = lax.broadcasted_iota(I32, (tm, na), 1)
    pw = jnp.zeros((tm, na), F32)
    for k in range(TOP_K):
        pw = pw + jnp.where(lane == pos_ref[:, k:k + 1], gate_ref[:, k:k + 1], 0.0)
    phi = pw.astype(BF16)
    plo = (pw - phi.astype(F32)).astype(BF16)
    yb = _unpack_rows(buf[slot]).astype(BF16)
    res = x_ref[...] + g2_ref[...] * (_dot(phi, yb) + _dot(plo, yb))
    if len(o_refs) == 1:
        o_refs[0][...] = res
    else:
        @pl.when(i < npt)
        def _():
            o_refs[0][...] = res

        @pl.when(i >= npt)
        def _():
            o_refs[1][...] = res


def moe_combine(lay, n_tab, off_tab, dst_tab, tot_tab, pos, gates, x, gate2, y_rows, split):
    n_tok = x.shape[0]
    tm = TOK_TILE
    npt = lay.p_tok // tm
    mrow = lambda i, *_: (lay.mod_row(i * tm), 0, 0)
    if split:
        out_specs = [pl.BlockSpec((tm, D), lambda i, *_: (jnp.minimum(i, npt - 1), 0)),
                     pl.BlockSpec((tm, D), lambda i, *_: (jnp.maximum(i - npt, 0), 0))]
        out_shape = [jax.ShapeDtypeStruct((lay.p_tok, D), F32), jax.ShapeDtypeStruct((n_tok - lay.p_tok, D), F32)]
    else:
        out_specs = pl.BlockSpec((tm, D), lambda i, *_: (i, 0))
        out_shape = jax.ShapeDtypeStruct((n_tok, D), F32)
    grid_spec = pltpu.PrefetchScalarGridSpec(
        num_scalar_prefetch=4, grid=(n_tok // tm,),
        in_specs=[pl.BlockSpec((tm, LANES), lambda i, *_: (i, 0)),
                  pl.BlockSpec((tm, LANES), lambda i, *_: (i, 0)),
                  pl.BlockSpec((tm, D), lambda i, *_: (i, 0)),
                  pl.BlockSpec((None, 1, D), mrow),
                  pl.BlockSpec(memory_space=pl.ANY)],
        out_specs=out_specs,
        scratch_shapes=[pltpu.VMEM((2, TILE_ROWS, ROW_WORDS), U32), pltpu.SemaphoreType.DMA((2,))])
    return pl.pallas_call(
        functools.partial(_combine_kernel, npt=npt), grid_spec=grid_spec, out_shape=out_shape,
        compiler_params=_cp(("arbitrary",)), name="moe_combine",
    )(n_tab, off_tab, dst_tab, tot_tab, pos, gates, x, gate2, y_rows)


def _expert_kernel(be_ref, nv_ref, nxt_ref, slot_ref, x_ref, bg_ref, bu_ref, bd_ref, wg_hbm, wu_hbm, wd_hbm,
                   y_ref, wf, sems, *, layer):
    i = pl.program_id(0)
    valid = i < nv_ref[0]
    e = be_ref[i]
    slot = slot_ref[e]
    changed = jnp.logical_or(i == 0, e != be_ref[jnp.maximum(i - 1, 0)])

    def weight_copies(ex, s):
        return [pltpu.make_async_copy(w.at[layer, ex], wf.at[s, k], sems.at[s, k])
                for k, w in enumerate((wg_hbm, wu_hbm, wd_hbm))]

    @pl.when(jnp.logical_and(valid, changed))
    def _():
        @pl.when(i == 0)
        def _():
            for c in weight_copies(e, slot):
                c.start()

        nxt = nxt_ref[e]

        @pl.when(nxt >= 0)
        def _():
            for c in weight_copies(nxt, 1 - slot):
                c.start()

        for c in weight_copies(e, slot):
            c.wait()

    @pl.when(valid)
    def _():
        x = _unpack_rows(x_ref[...])
        gt = jnp.minimum(_dot(x, wf[slot, 0]) + bg_ref[...], SWIGLU_LIMIT)
        up = jnp.clip(_dot(x, wf[slot, 1]) + bu_ref[...], -SWIGLU_LIMIT, SWIGLU_LIMIT)
        act = (up + 1.0) * gt * _sigmoid(SWIGLU_ALPHA * gt)
        y = _dot(act, wf[slot, 2]) + bd_ref[...]
        y_ref[...] = _pack_rows(y.astype(BF16).astype(F32))

    @pl.when(jnp.logical_not(valid))
    def _():
        y_ref[...] = jnp.zeros_like(y_ref)


def moe_experts(layer, blk_expert, n_valid, next_expert, slot, x_rows, w_gate, b_gate, w_up, b_up, w_down,
                b_down):
    n_rows = x_rows.shape[0]
    nb = n_rows // MOE_BLOCK
    depth, ne, _, ff = w_gate.shape
    assert ff == D
    rowblk = lambda i, be, nv, *_: (jnp.maximum(jnp.minimum(i, nv[0] - 1), 0), 0)
    bsel = lambda i, be, *_: (layer, be[i], 0, 0)
    hbm = pl.BlockSpec(memory_space=pl.ANY)
    grid_spec = pltpu.PrefetchScalarGridSpec(
        num_scalar_prefetch=4, grid=(nb,),
        in_specs=[pl.BlockSpec((MOE_BLOCK, ROW_WORDS), rowblk),
                  pl.BlockSpec((None, None, 1, ff), bsel), pl.BlockSpec((None, None, 1, ff), bsel),
                  pl.BlockSpec((None, None, 1, D), bsel), hbm, hbm, hbm],
        out_specs=pl.BlockSpec((MOE_BLOCK, ROW_WORDS), lambda i, *_: (i, 0)),
        scratch_shapes=[pltpu.VMEM((2, 3, D, ff), F32), pltpu.SemaphoreType.DMA((2, 3))])
    return pl.pallas_call(
        functools.partial(_expert_kernel, layer=layer), grid_spec=grid_spec,
        out_shape=jax.ShapeDtypeStruct((n_rows, ROW_WORDS), U32),
        compiler_params=_cp(("arbitrary",)), name="moe_experts",
    )(blk_expert, n_valid, next_expert, slot, x_rows, b_gate.reshape(depth, ne, 1, ff),
      b_up.reshape(depth, ne, 1, ff), b_down.reshape(depth, ne, 1, D), w_gate, w_up, w_down)


def moe_layer(lay, layer, x, g2, sc2, sh2, gate2, router_w, router_b, w_gate, b_gate, w_up, b_up, w_down,
              b_down, split=False):
    n_tok = x.shape[0]
    nt = n_tok // TOK_TILE
    rw = jnp.zeros((D, LANES), F32).at[:, :N_EXPERTS].set(router_w)
    rw = _hilo(rw)
    rb = jnp.full((1, LANES), NEG, F32).at[0, :N_EXPERTS].set(router_b)
    h2, _, gates, pos, posT, cnt = moe_router(lay, x, g2, sc2, sh2, rw, rb)
    n_te = cnt[:, 0, :N_EXPERTS].astype(I32)
    n_te = (n_te + SEG_ALIGN - 1) // SEG_ALIGN * SEG_ALIGN
    totals = jnp.sum(n_te, axis=0)
    padded = (totals + MOE_BLOCK - 1) // MOE_BLOCK * MOE_BLOCK
    padded_end = jnp.cumsum(padded)
    pstart = padded_end - padded
    dst = pstart[None, :] + jnp.cumsum(n_te, axis=0) - n_te
    off = jnp.cumsum(n_te, axis=1) - n_te
    n_rows = nt * TILE_ROWS + N_EXPERTS * MOE_BLOCK
    nb = n_rows // MOE_BLOCK
    n_valid = (padded_end[-1] // MOE_BLOCK).astype(I32).reshape(1)
    bstart = jnp.minimum(jnp.arange(nb, dtype=I32), n_valid[0] - 1) * MOE_BLOCK
    blk_expert = jnp.minimum(jnp.sum((bstart[:, None] >= padded_end[None, :]).astype(I32), axis=1),
                             N_EXPERTS - 1).astype(I32)
    tabs = (n_te.reshape(-1).astype(I32), off.reshape(-1).astype(I32), dst.reshape(-1).astype(I32),
            jnp.sum(n_te, axis=1).astype(I32))
    tail_n = jnp.concatenate([(padded - totals).astype(I32), n_valid])
    x_rows = moe_dispatch(*tabs, tail_n, (pstart + totals).astype(I32), posT, h2, n_rows)
    owner = jnp.where(padded > 0, jnp.arange(N_EXPERTS, dtype=I32), N_EXPERTS)
    later = jnp.concatenate([lax.cummin(owner, axis=0, reverse=True)[1:], jnp.full((1,), N_EXPERTS, I32)])
    next_expert = jnp.where(later < N_EXPERTS, later, -1).astype(I32)
    slot = ((jnp.cumsum((padded > 0).astype(I32)) - 1) % 2).astype(I32)
    y_rows = moe_experts(layer, blk_expert, n_valid, next_expert, slot, x_rows, w_gate, b_gate, w_up, b_up,
                         w_down, b_down)
    return moe_combine(lay, *tabs, pos, gates, x, gate2, y_rows, split)


QKV_TN = 256
N_QK_TILES = (ATT_HEADS + ATT_KV) * ATT_HD // QKV_TN


def _qkv_kernel(x_ref, g_ref, sc_ref, sh_ref, w_ref, nw_ref, cos_ref, sin_ref, o_ref):
    tm = x_ref.shape[0]
    h = _modnorm(x_ref[...], g_ref[...], sc_ref[...], sh_ref[...]).astype(BF16)
    r = lax.broadcasted_iota(I32, (QKV_TN, QKV_TN), 0) // ATT_HD
    c = lax.broadcasted_iota(I32, (QKV_TN, QKV_TN), 1) // ATT_HD
    head_mean = jnp.where(r == c, 1.0 / ATT_HD, 0.0).astype(BF16)
    lane = lax.broadcasted_iota(I32, (tm, QKV_TN), 1)
    half = ATT_HD // 4
    first = (lane % (2 * half)) < half
    for j in range(w_ref.shape[1] // QKV_TN):
        cols = slice(QKV_TN * j, QKV_TN * (j + 1))
        acc = _dot(h, w_ref[:, cols])
        if j >= N_QK_TILES:
            o_ref[:, cols] = acc
            continue
        ms = _dot((acc * acc).astype(BF16), head_mean)
        qn = acc * lax.rsqrt(ms + EPS) * nw_ref[j]
        swapped = jnp.where(first, pltpu.roll(qn, QKV_TN - half, 1), pltpu.roll(qn, half, 1))
        o_ref[:, cols] = qn * cos_ref[...] + swapped * sin_ref[...]


def _rope_tables(sample_len):
    pos = np.arange(sample_len)
    half = ATT_HD // 4
    inv = (ROPE_THETA ** (-np.arange(half, dtype=np.float32) / half)).astype(np.float32)
    ang_r = (pos // GRID_W).astype(np.float32)[:, None] * inv[None, :]
    ang_c = (pos % GRID_W).astype(np.float32)[:, None] * inv[None, :]
    cos = np.concatenate([np.cos(ang_r)] * 2 + [np.cos(ang_c)] * 2, axis=1)
    sin = np.concatenate([-np.sin(ang_r), np.sin(ang_r), -np.sin(ang_c), np.sin(ang_c)], axis=1)
    rep = QKV_TN // ATT_HD
    return (jnp.asarray(np.tile(cos, (1, rep)), F32), jnp.asarray(np.tile(sin, (1, rep)), F32))


def qkv_proj(lay, x, g, sc, sh, w, q_norm, k_norm, tm=512):
    n_tok = x.shape[0]
    n = w.shape[1]
    nq = ATT_HEADS * ATT_HD // QKV_TN
    rep = QKV_TN // ATT_HD
    nw = jnp.concatenate([jnp.tile(jnp.tile(q_norm, rep)[None, :], (nq, 1)),
                          jnp.tile(jnp.tile(k_norm, rep)[None, :], (n // QKV_TN - nq, 1))], axis=0)
    cos, sin = _rope_tables(lay.sample_len)
    cos = jnp.concatenate([jnp.ones((tm, QKV_TN), F32), cos], axis=0)
    sin = jnp.concatenate([jnp.zeros((tm, QKV_TN), F32), sin], axis=0)
    assert lay.p_tok % tm == 0 and lay.sample_len % tm == 0
    mrow = lambda i: (lay.mod_row(i * tm), 0, 0)
    rrow = lambda i: (jnp.where(i * tm < lay.p_tok, 0, 1 + ((i * tm - lay.p_tok) % lay.sample_len) // tm), 0)
    nt = n // QKV_TN
    return pl.pallas_call(
        _qkv_kernel, grid=(n_tok // tm,),
        in_specs=[pl.BlockSpec((tm, D), lambda i: (i, 0)),
                  pl.BlockSpec((1, D), lambda i: (0, 0)),
                  pl.BlockSpec((None, 1, D), mrow), pl.BlockSpec((None, 1, D), mrow),
                  pl.BlockSpec((D, n), lambda i: (0, 0)),
                  pl.BlockSpec((nt, 1, QKV_TN), lambda i: (0, 0, 0)),
                  pl.BlockSpec((tm, QKV_TN), rrow), pl.BlockSpec((tm, QKV_TN), rrow)],
        out_specs=pl.BlockSpec((tm, n), lambda i: (i, 0)),
        out_shape=jax.ShapeDtypeStruct((n_tok, n), F32),
        compiler_params=_cp(("arbitrary",)), name="qkv_proj",
    )(x, g.reshape(1, D), sc, sh, w, nw.reshape(nt, 1, QKV_TN), cos, sin)


def _dup_group(x, g):
    blk = x[:, LANES * (g // 2):LANES * (g // 2 + 1)]
    if g % 2 == 1:
        blk = pltpu.roll(blk, ATT_HD, 1)
    lo = lax.broadcasted_iota(I32, blk.shape, 1) < ATT_HD
    low = jnp.where(lo, blk, 0.0)
    return low + pltpu.roll(low, ATT_HD, 1)


def _attend(q_ref, k_all, v_all, mask, sink_ref, o_ref):
    nq = q_ref.shape[0]
    lo = lax.broadcasted_iota(I32, (nq, LANES), 1) < ATT_HD
    grp = ATT_HEADS // ATT_KV
    for g in range(ATT_KV):
        k2 = _dup_group(k_all, g).astype(BF16)
        v2 = _dup_group(v_all, g).astype(BF16)
        for jp in range(grp // 2):
            j = g * (grp // 2) + jp
            qp = q_ref[:, LANES * j:LANES * (j + 1)] * (ATT_HD ** -0.5)
            outs = []
            for half in range(2):
                qh = jnp.where(lo, qp, 0.0) if half == 0 else jnp.where(lo, 0.0, qp)
                s = _dot_nt(qh.astype(BF16), k2)
                if mask is not None:
                    s = jnp.where(mask, s, NEG)
                sink = sink_ref[2 * j + half]
                m = jnp.maximum(jnp.max(s, axis=1, keepdims=True), sink)
                p = jnp.exp(s - m)
                den = jnp.sum(p, axis=1, keepdims=True) + jnp.exp(sink - m)
                outs.append(_dot(p.astype(BF16), v2) / den)
            o_ref[:, LANES * j:LANES * (j + 1)] = jnp.where(lo, outs[0], outs[1]).astype(o_ref.dtype)


def _attn_ctx_kernel(sink_ref, q_ref, k_ref, v_ref, o_ref):
    _attend(q_ref, k_ref[...], v_ref[...], None, sink_ref, o_ref)


def attn_context(lay, qkv, sinks):
    qw = ATT_HEADS * ATT_HD
    kw = ATT_KV * ATT_HD
    ln = lay.prompt_len
    grid_spec = pltpu.PrefetchScalarGridSpec(
        num_scalar_prefetch=0, grid=(lay.n_prompt,),
        in_specs=[pl.BlockSpec(memory_space=pltpu.SMEM),
                  pl.BlockSpec((ln, qw), lambda b: (b, 0)),
                  pl.BlockSpec((ln, kw), lambda b: (b, qw // kw)),
                  pl.BlockSpec((ln, kw), lambda b: (b, qw // kw + 1))],
        out_specs=pl.BlockSpec((ln, qw), lambda b: (b, 0)))
    return pl.pallas_call(
        _attn_ctx_kernel, grid_spec=grid_spec,
        out_shape=jax.ShapeDtypeStruct((lay.p_tok, qw), BF16),
        compiler_params=_cp(("arbitrary",)), name="attn_context",
    )(sinks, qkv, qkv, qkv)


def _attn_lat_kernel(sink_ref, q_ref, kp_ref, kc_ref, kn_ref, vp_ref, vc_ref, vn_ref, ck_ref, cv_ref, o_ref,
                     *, nblk):
    i = pl.program_id(1)
    bq = ATT_BLOCK
    nctx = ck_ref.shape[1]
    k_all = jnp.concatenate([kp_ref[...], kc_ref[...], kn_ref[...], ck_ref[0]], axis=0)
    v_all = jnp.concatenate([vp_ref[...], vc_ref[...], vn_ref[...], cv_ref[0]], axis=0)
    ns = 3 * bq + nctx
    r = lax.broadcasted_iota(I32, (bq, ns), 0)
    c = lax.broadcasted_iota(I32, (bq, ns), 1)
    rel = c - r
    first_key = jnp.where(i > 0, 0, bq)
    end_key = jnp.where(i < nblk - 1, 3 * bq, 2 * bq)
    band = (rel >= bq - WINDOW) & (rel <= bq + WINDOW) & (c >= first_key) & (c < end_key)
    mask = band | (c >= 3 * bq)
    _attend(q_ref, k_all, v_all, mask, sink_ref, o_ref)


def attn_latent(lay, qkv, cache_k, cache_v, sinks):
    qw = ATT_HEADS * ATT_HD
    kw = ATT_KV * ATT_HD
    bq = ATT_BLOCK
    nblk = lay.sample_len // bq
    b0 = lay.p_tok // bq
    nctx = cache_k.shape[1]
    rb = lambda b, i: b0 + b * nblk + i
    kspec = lambda cb, sh: pl.BlockSpec(
        (bq, kw), lambda b, i: (b0 + b * nblk + jnp.clip(i + sh, 0, nblk - 1), cb))
    kc, vc = qw // kw, qw // kw + 1
    grid_spec = pltpu.PrefetchScalarGridSpec(
        num_scalar_prefetch=0, grid=(lay.n_sample, nblk),
        in_specs=[pl.BlockSpec(memory_space=pltpu.SMEM),
                  pl.BlockSpec((bq, qw), lambda b, i: (rb(b, i), 0)),
                  kspec(kc, -1), kspec(kc, 0), kspec(kc, 1),
                  kspec(vc, -1), kspec(vc, 0), kspec(vc, 1),
                  pl.BlockSpec((1, nctx, kw), lambda b, i: (b, 0, 0)),
                  pl.BlockSpec((1, nctx, kw), lambda b, i: (b, 0, 0))],
        out_specs=pl.BlockSpec((bq, qw), lambda b, i: (b * nblk + i, 0)))
    return pl.pallas_call(
        functools.partial(_attn_lat_kernel, nblk=nblk), grid_spec=grid_spec,
        out_shape=jax.ShapeDtypeStruct((lay.n_sample * lay.sample_len, qw), BF16),
        compiler_params=_cp(("arbitrary", "arbitrary")), name="attn_latent",
    )(sinks, qkv, qkv, qkv, qkv, qkv, qkv, qkv,
      cache_k.reshape(lay.n_sample, nctx, kw), cache_v.reshape(lay.n_sample, nctx, kw))


def _forward(lay, x_prompt, x_sample, state_l0_ssd_fwd, state_l0_ssd_bwd, state_l0_gla_fwd, state_l0_gla_bwd,
             cache_l1_k, cache_l1_v, c, c_ctx, ada_w, ada_b, norm1, norm2,
             l0_w_in, l0_conv_w, l0_conv_b, l0_a_log, l0_dt_bias, l0_d_skip, l0_ssd_norm,
             l0_gate_w2, l0_gate_b, l0_gla_norm, l0_w_out,
             l1_w_qkv, l1_q_norm, l1_k_norm, l1_sinks, l1_w_out,
             router_w, router_b, exp_w_gate, exp_b_gate, exp_w_up, exp_b_up, exp_w_down, exp_b_down):
    np_, ns = lay.n_prompt, lay.n_sample
    x = (x_prompt.reshape(-1, D), x_sample.reshape(-1, D))
    cond8 = jnp.zeros((8, D), F32).at[0].set(c_ctx).at[1:1 + ns].set(c)
    mod = ada_table(cond8, ada_w, ada_b)
    mods = [[mod[l, :, p * D:(p + 1) * D].reshape(8, 1, D) for p in range(N_ADA)] for l in range(2)]

    def moe(l, xx, split=False):
        return moe_layer(lay, l, xx, norm2[l], mods[l][4], mods[l][3], mods[l][5], router_w[l], router_b[l],
                         exp_w_gate, exp_b_gate, exp_w_up, exp_b_up, exp_w_down, exp_b_down, split=split)

    sp = np.cumsum((SSD_INNER, SSD_INNER + 2 * SSD_GROUPS * SSD_STATE, 2 * SSD_HEADS,
                    GLA_HEADS * GLA_KEY_DIM, GLA_HEADS * GLA_KEY_DIM,
                    GLA_HEADS * GLA_VAL_DIM, GLA_HEADS * GLA_VAL_DIM, 2 * GLA_RANK))
    cols = lambda a, b: l0_w_in[:, a:b]
    w_main = jnp.concatenate([cols(0, sp[0]), cols(sp[4], sp[5]), cols(sp[5], sp[6]), cols(sp[0], sp[1]),
                              cols(sp[2], sp[3]), cols(sp[3], sp[4])], axis=1).astype(BF16)
    w_small = jnp.concatenate([cols(sp[1], sp[2]), cols(sp[6], sp[7]),
                               jnp.zeros((D, LANES - 2 * SSD_HEADS - 2 * GLA_RANK), F32)], axis=1)
    proj, small, small_t = norm_proj(lay, x, norm1[0], mods[0][1], mods[0][0], w_main, _hilo(w_small), 512,
                                      PJ_W // 2, BF16, SSD_CHUNK)
    y_n, o_n, (ssd_f, ssd_b, gla_f, gla_b) = l0_mixers(
        lay, proj, small, small_t, state_l0_ssd_fwd, state_l0_ssd_bwd, state_l0_gla_fwd, state_l0_gla_bwd,
        l0_conv_w, l0_conv_b, l0_a_log, l0_dt_bias, l0_d_skip, l0_ssd_norm,
        l0_gate_w2, l0_gate_b, l0_gla_norm)
    x = proj_residual(lay, [y_n, o_n], l0_w_out.astype(BF16), x, mods[0][2])
    x = moe(0, x)

    qkv = qkv_proj(lay, x, norm1[1], mods[1][1], mods[1][0], l1_w_qkv.astype(BF16), l1_q_norm, l1_k_norm)
    o_ctx = attn_context(lay, qkv, l1_sinks)
    o_lat = attn_latent(lay, qkv, cache_l1_k, cache_l1_v, l1_sinks)
    x = proj_residual(lay, [(o_ctx, o_lat)], l1_w_out.astype(BF16), x, mods[1][2])
    xp, xs = moe(1, x, split=True)

    qw = ATT_HEADS * ATT_HD
    kw = ATT_KV * ATT_HD
    return (xp.reshape(x_prompt.shape), xs.reshape(x_sample.shape),
            ssd_f[:np_].reshape(np_, SSD_HEADS, SSD_HEAD_DIM, SSD_STATE),
            ssd_b[:np_].reshape(np_, SSD_HEADS, SSD_HEAD_DIM, SSD_STATE),
            gla_f[:np_], gla_b[:np_],
            qkv[:lay.p_tok, qw:qw + kw].reshape(np_, lay.prompt_len, ATT_KV, ATT_HD),
            qkv[:lay.p_tok, qw + kw:].reshape(np_, lay.prompt_len, ATT_KV, ATT_HD))


def kernel(x_prompt, x_sample, state_l0_ssd_fwd, state_l0_ssd_bwd, state_l0_gla_fwd, state_l0_gla_bwd, cache_l1_k, cache_l1_v, c, c_ctx, ada_w, ada_b, norm1, norm2, l0_w_in, l0_conv_w, l0_conv_b, l0_a_log, l0_dt_bias, l0_d_skip, l0_ssd_norm, l0_gate_w2, l0_gate_b, l0_gla_norm, l0_w_out, l1_w_qkv, l1_q_norm, l1_k_norm, l1_sinks, l1_w_out, router_w, router_b, exp_w_gate, exp_b_gate, exp_w_up, exp_b_up, exp_w_down, exp_b_down):
    lay = Layout(x_prompt.shape[0], x_prompt.shape[1], x_sample.shape[0], x_sample.shape[1])
    return _forward(lay, x_prompt, x_sample, state_l0_ssd_fwd, state_l0_ssd_bwd, state_l0_gla_fwd,
                    state_l0_gla_bwd, cache_l1_k, cache_l1_v, c, c_ctx, ada_w, ada_b, norm1, norm2,
                    l0_w_in, l0_conv_w, l0_conv_b, l0_a_log, l0_dt_bias, l0_d_skip, l0_ssd_norm,
                    l0_gate_w2, l0_gate_b, l0_gla_norm, l0_w_out,
                    l1_w_qkv, l1_q_norm, l1_k_norm, l1_sinks, l1_w_out,
                    router_w, router_b, exp_w_gate, exp_b_gate, exp_w_up, exp_b_up, exp_w_down, exp_b_down)
```

```python
import functools
import math

import numpy as np
import jax
import jax.numpy as jnp
from jax import lax
from jax.experimental import pallas as pl
from jax.experimental.pallas import tpu as pltpu

F32 = jnp.float32
BF16 = jnp.bfloat16
I32 = jnp.int32
HI = lax.Precision.HIGHEST

D = 1024
EPS = 1e-6
N_ADA = 6
SSD_HEADS = 16
SSD_HEAD_DIM = 64
SSD_INNER = 1024
SSD_STATE = 128
SSD_GROUPS = 2
SSD_CONV = 5
SSD_CHUNK = 128
GLA_HEADS = 4
GLA_KEY_DIM = 128
GLA_VAL_DIM = 256
GLA_RANK = 16
GLA_TAU = 16.0
GLA_BLOCK = 64
ATT_HEADS = 16
ATT_KV = 4
ATT_HD = 64
ATT_BLOCK = 128
WINDOW = 128
GRID_W = 64
ROPE_THETA = 10000.0
N_EXPERTS = 32
TOP_K = 4
EXPERT_FF = 1024
SWIGLU_LIMIT = 7.0
SWIGLU_ALPHA = 1.702
MOE_BLOCK = 256
TOK_TILE = 256
LANES = 128
NEG = -1e30

PJ_Z, PJ_V, PJ_OG, PJ_XBC, PJ_Q, PJ_K = 0, 1024, 2048, 3072, 4608, 5120
PJ_W = 5632
VMEM_LIMIT = 48 * 1024 * 1024


def _cp(sem, vmem=VMEM_LIMIT):
    return pltpu.CompilerParams(dimension_semantics=sem, vmem_limit_bytes=vmem)


class Layout:
    def __init__(self, n_prompt, prompt_len, n_sample, sample_len):
        self.n_prompt, self.prompt_len = n_prompt, prompt_len
        self.n_sample, self.sample_len = n_sample, sample_len
        self.p_tok = n_prompt * prompt_len
        self.n_tok = self.p_tok + n_sample * sample_len
        self.seqs = [(i * prompt_len, prompt_len) for i in range(n_prompt)]
        self.seqs += [(self.p_tok + i * sample_len, sample_len) for i in range(n_sample)]
        self.n_seq = len(self.seqs)

    def mod_row(self, start):
        return jnp.where(start < self.p_tok, 0, 1 + (start - self.p_tok) // self.sample_len)

def _sigmoid(x):
    return 1.0 / (1.0 + jnp.exp(-x))


def _silu(x):
    return x * _sigmoid(x)


def _softplus(x):
    return jnp.maximum(x, 0.0) + jnp.log(1.0 + jnp.exp(-jnp.abs(x)))


def _modnorm(x, g, sc, sh):
    ms = jnp.mean(x * x, axis=-1, keepdims=True)
    return (x * lax.rsqrt(ms + EPS) * g) * (1.0 + sc) + sh


def _dot(a, b, **kw):
    return jnp.dot(a, b, preferred_element_type=F32, **kw)


def _dot_nt(a, b):
    return lax.dot_general(a, b, (((1,), (1,)), ((), ())), preferred_element_type=F32)


def _dot_tn(a, b):
    return lax.dot_general(a, b, (((0,), (0,)), ((), ())), preferred_element_type=F32)


def _split(x, n):
    parts = []
    for _ in range(n):
        p = x.astype(BF16)
        parts.append(p)
        x = x - p.astype(F32)
    return parts


def _dot_sel(sel, x):
    sel = sel.astype(BF16)
    return sum(_dot(sel, p) for p in _split(x, 3))


def _dot_sel_r(x, sel):
    sel = sel.astype(BF16)
    return sum(_dot(p, sel) for p in _split(x, 3))


def _dot_hilo(x, w_hi, w_lo):
    x_hi, x_lo = _split(x, 2)
    return _dot(x_hi, w_hi) + _dot(x_lo, w_hi) + _dot(x_hi, w_lo)


def _hilo(w):
    hi = w.astype(BF16)
    return jnp.stack([hi, (w - hi.astype(F32)).astype(BF16)])


U32 = jnp.uint32
ROW_WORDS = D // 2
_HI_MASK = 0xFFFF0000


def _pack_rows(x):
    lo = lax.bitcast_convert_type(x[:, :ROW_WORDS], U32) >> 16
    hi = lax.bitcast_convert_type(x[:, ROW_WORDS:], U32) & jnp.uint32(_HI_MASK)
    return lo | hi


def _unpack_rows(u):
    lo = lax.bitcast_convert_type(u << 16, F32)
    hi = lax.bitcast_convert_type(u & jnp.uint32(_HI_MASK), F32)
    return jnp.concatenate([lo, hi], axis=1)


def _ada_kernel(c_ref, w_ref, b_ref, o_ref):
    o_ref[0] = _dot(_silu(c_ref[...]), w_ref[0], precision=HI) + b_ref[0]


def ada_table(cond8, ada_w, ada_b):
    depth, _, n = ada_w.shape
    tn = 1536
    return pl.pallas_call(
        _ada_kernel, grid=(depth, n // tn),
        in_specs=[pl.BlockSpec((8, D), lambda l, j: (0, 0)),
                  pl.BlockSpec((1, D, tn), lambda l, j: (l, 0, j)),
                  pl.BlockSpec((1, 1, tn), lambda l, j: (l, 0, j))],
        out_specs=pl.BlockSpec((1, 8, tn), lambda l, j: (l, 0, j)),
        out_shape=jax.ShapeDtypeStruct((depth, 8, n), F32),
        compiler_params=_cp(("arbitrary", "arbitrary")), name="ada_table",
    )(cond8, ada_w, ada_b.reshape(depth, 1, n))


def _stream_specs(lay, stream, tm):
    if not isinstance(stream, (tuple, list)):
        return [stream], [pl.BlockSpec((tm, stream.shape[1]), lambda i, *_: (i, 0))]
    assert lay.p_tok % tm == 0
    npt = lay.p_tok // tm
    w = stream[0].shape[1]
    return list(stream), [pl.BlockSpec((tm, w), lambda i, *_: (jnp.minimum(i, npt - 1), 0)),
                          pl.BlockSpec((tm, w), lambda i, *_: (jnp.maximum(i - npt, 0), 0))]


def _stream_tile(refs, in_prompt):
    if len(refs) == 1:
        return refs[0][...]
    return jnp.where(in_prompt, refs[0][...], refs[1][...])


def _proj_kernel(*refs, nx, npt):
    x_refs = refs[:nx]
    g_ref, sc_ref, sh_ref, w_ref, ws_ref, o_ref, os_ref, ost_ref, h_scr = refs[nx:]

    @pl.when(pl.program_id(1) == 0)
    def _():
        x = _stream_tile(x_refs, pl.program_id(0) < npt)
        h = _modnorm(x, g_ref[...], sc_ref[...], sh_ref[...])
        h_scr[...] = h.astype(BF16)
        small = _dot_hilo(h, ws_ref[0], ws_ref[1])
        os_ref[...] = small
        q = ost_ref.shape[2]
        for c in range(ost_ref.shape[0]):
            ost_ref[c] = small[q * c:q * (c + 1), :].T

    o_ref[...] = _dot(h_scr[...], w_ref[...]).astype(o_ref.dtype)


def norm_proj(lay, x, g, sc, sh, w, w_small, tm, tn, out_dtype, chunk):
    n_tok = lay.n_tok
    n = w.shape[1]
    ns = w_small.shape[-1]
    mrow = lambda i, j: (lay.mod_row(i * tm), 0, 0)
    xs, x_specs = _stream_specs(lay, x, tm)
    return pl.pallas_call(
        functools.partial(_proj_kernel, nx=len(xs), npt=lay.p_tok // tm), grid=(n_tok // tm, n // tn),
        in_specs=x_specs + [pl.BlockSpec((1, D), lambda i, j: (0, 0)),
                            pl.BlockSpec((None, 1, D), mrow),
                            pl.BlockSpec((None, 1, D), mrow),
                            pl.BlockSpec((D, tn), lambda i, j: (0, j)),
                            pl.BlockSpec((2, D, ns), lambda i, j: (0, 0, 0))],
        out_specs=[pl.BlockSpec((tm, tn), lambda i, j: (i, j)),
                   pl.BlockSpec((tm, ns), lambda i, j: (i, 0)),
                   pl.BlockSpec((tm // chunk, ns, chunk), lambda i, j: (i, 0, 0))],
        out_shape=[jax.ShapeDtypeStruct((n_tok, n), out_dtype),
                   jax.ShapeDtypeStruct((n_tok, ns), F32),
                   jax.ShapeDtypeStruct((n_tok // chunk, ns, chunk), F32)],
        scratch_shapes=[pltpu.VMEM((tm, D), BF16)],
        compiler_params=_cp(("arbitrary", "arbitrary")), name="norm_proj",
    )(*xs, g.reshape(1, D), sc, sh, w, w_small)


CONV_HALO = 16


def _ssd_load(d, c, xc, dtg_ref, dtgT_ref, S):
    q = SSD_CHUNK
    nh = SSD_HEADS
    rows = pl.ds(pl.multiple_of(c * q, q), q)
    return (rows, xc[rows, 0:SSD_INNER], xc[rows, SSD_INNER:SSD_INNER + 2 * SSD_GROUPS * SSD_STATE],
            dtg_ref[rows, nh * d:nh * d + nh], dtgT_ref[c, nh * d:nh * d + nh, :], S[...])


def _ssd_chunk(d, loaded, alog_ref, alogT_ref, dtb_ref, dtbT_ref):
    q = SSD_CHUNK
    nh = SSD_HEADS
    _, xs, bc, dtg, dtgT, s = loaded
    xs = xs.astype(F32)
    dt = _softplus(dtg + dtb_ref[d:d + 1, :])
    dtT = _softplus(dtgT + dtbT_ref[:, d:d + 1])
    ad = dt * (-jnp.exp(alog_ref[d:d + 1, :]))
    adT = dtT * (-jnp.exp(alogT_ref[:, d:d + 1]))
    row = lax.broadcasted_iota(I32, (q, q), 0)
    col = lax.broadcasted_iota(I32, (q, q), 1)
    if d == 0:
        e = _dot_sel(col <= row, ad)
        eT = _dot_sel_r(adT, row <= col)
        tot = e[q - 1:q, :]
        mask = row >= col
        fq = jnp.exp(e)
        fk = jnp.exp(tot - e)
    else:
        e = _dot_sel(col < row, ad)
        eT = _dot_sel_r(adT, row < col)
        tot = jnp.sum(ad, axis=0, keepdims=True)
        mask = col >= row
        fq = jnp.exp(tot - e)
        fk = jnp.exp(e)
    dec = jnp.exp(tot)
    lo = lax.broadcasted_iota(I32, (q, LANES), 1) < SSD_HEAD_DIM
    hp = nh * SSD_HEAD_DIM
    head_of = lax.broadcasted_iota(I32, (nh, hp), 1) // SSD_HEAD_DIM
    spread = (lax.broadcasted_iota(I32, (nh, hp), 0) == head_of).astype(BF16)

    def per_lane(arr, passes):
        return sum(_dot(p, spread) for p in _split(arr, passes))

    dt_x, fq_x, fk_x = per_lane(dt, 1), per_lane(fq, 1), per_lane(fk, 1)
    dec_x = per_lane(jnp.broadcast_to(dec, (8, nh)), 3)[0:1, :]
    xdt_all = xs * dt_x
    xk_all = (xdt_all * fk_x).astype(BF16)
    rep = (nh // SSD_GROUPS) // 2
    ys, s_new = [], []
    for g in range(SSD_GROUPS):
        bg_t = bc[:, SSD_STATE * g:SSD_STATE * (g + 1)].T
        cg = bc[:, SSD_STATE * (SSD_GROUPS + g):SSD_STATE * (SSD_GROUPS + g + 1)]
        gmat = _dot(cg, bg_t)
        for j in range(rep * g, rep * (g + 1)):
            a = 2 * j
            sl = slice(LANES * j, LANES * (j + 1))
            parts = []
            for hh in (a, a + 1):
                if d == 0:
                    diff = e[:, hh:hh + 1] - eT[hh:hh + 1, :]
                else:
                    diff = eT[hh:hh + 1, :] - e[:, hh:hh + 1]
                parts.append((gmat * jnp.exp(jnp.where(mask, diff, NEG))).astype(BF16))
            lhs = jnp.concatenate(parts, axis=1)
            xdt = xdt_all[:, sl]
            rhs = jnp.concatenate([jnp.where(lo, xdt, 0.0), jnp.where(lo, 0.0, xdt)], axis=0)
            sj = s[:, sl]
            ys.append(_dot(lhs, rhs.astype(BF16)) + _dot(cg, sj.astype(BF16)) * fq_x[:, sl])
            s_new.append(sj * dec_x[:, sl] + _dot(bg_t, xk_all[:, sl]))
    return jnp.concatenate(ys, axis=1), jnp.concatenate(s_new, axis=1)


def _ssd_seq_kernel(*refs, has_init):
    (xbc_ref, z_ref, dtg_ref, dtgT_ref, cw_ref, cb_ref,
     alog_ref, alogT_ref, dtb_ref, dtbT_ref, dskip_ref, nrm_ref) = refs[:12]
    refs = refs[12:]
    if has_init:
        s0f_ref, s0b_ref = refs[:2]
        refs = refs[2:]
    y_ref, sf_ref, sb_ref, xc, ext, Sf, Sb, yf, yb = refs
    ln = xbc_ref.shape[0]
    q = SSD_CHUNK
    nc = ln // q
    h = CONV_HALO
    pad = SSD_CONV // 2

    def conv_body(c, carry):
        r0 = pl.multiple_of(c * q, q)
        prev = xbc_ref[pl.ds(pl.multiple_of(jnp.maximum(r0 - h, 0), h), h), :].astype(F32)
        nxt = xbc_ref[pl.ds(pl.multiple_of(jnp.minimum(r0 + q, ln - h), h), h), :].astype(F32)
        ext[0:h, :] = jnp.where(c > 0, prev, 0.0)
        ext[h:h + q, :] = xbc_ref[pl.ds(r0, q), :].astype(F32)
        ext[h + q:h + q + h, :] = jnp.where(c < nc - 1, nxt, 0.0)
        acc = jnp.broadcast_to(cb_ref[...], (q, cb_ref.shape[1]))
        for k in range(SSD_CONV):
            acc = acc + cw_ref[k:k + 1, :] * ext[h - pad + k:h - pad + k + q, :]
        xc[pl.ds(r0, q), :] = _silu(acc).astype(xc.dtype)
        return carry
    lax.fori_loop(0, nc, conv_body, 0)

    if has_init:
        Sf[...] = s0f_ref[0].T
        Sb[...] = s0b_ref[0].T
    else:
        Sf[...] = jnp.zeros_like(Sf)
        Sb[...] = jnp.zeros_like(Sb)
    params = (alog_ref, alogT_ref, dtb_ref, dtbT_ref)

    def scan_body(c, carry):
        lf = _ssd_load(0, c, xc, dtg_ref, dtgT_ref, Sf)
        lb = _ssd_load(1, nc - 1 - c, xc, dtg_ref, dtgT_ref, Sb)
        y_f, s_f = _ssd_chunk(0, lf, *params)
        y_b, s_b = _ssd_chunk(1, lb, *params)
        yf[lf[0], :] = y_f
        yb[lb[0], :] = y_b
        Sf[...] = s_f
        Sb[...] = s_b
        return carry
    lax.fori_loop(0, nc, scan_body, 0, unroll=2)
    sf_ref[0] = Sf[...].T
    sb_ref[0] = Sb[...].T

    def out_body(c, carry):
        rows = pl.ds(pl.multiple_of(c * q, q), q)
        ytot = yf[rows, :] + yb[rows, :] + dskip_ref[...] * xc[rows, 0:SSD_INNER].astype(F32)
        yg = ytot * _silu(z_ref[rows, :].astype(F32))
        ms = jnp.mean(yg * yg, axis=-1, keepdims=True)
        y_ref[rows, :] = (yg * lax.rsqrt(ms + EPS) * nrm_ref[...]).astype(y_ref.dtype)
        return carry
    lax.fori_loop(0, nc, out_body, 0)


def _ssd_call(n_seq, ln, blk0, proj, small, smallT3, init, params):
    q = SSD_CHUNK
    hp = SSD_HEADS * SSD_HEAD_DIM
    cw = SSD_INNER + 2 * SSD_GROUPS * SSD_STATE
    nc = ln // q
    assert PJ_XBC % cw == 0 and PJ_Z % SSD_INNER == 0
    tok = lambda w, cb: pl.BlockSpec((ln, w), lambda b: (blk0 + b, cb))
    seq3 = pl.BlockSpec((1, hp, SSD_STATE), lambda b: (b, 0, 0))
    full = lambda a: pl.BlockSpec(a.shape, lambda b: (0,) * a.ndim)
    init = () if init is None else tuple(init)
    return pl.pallas_call(
        functools.partial(_ssd_seq_kernel, has_init=bool(init)), grid=(n_seq,),
        in_specs=[tok(cw, PJ_XBC // cw), tok(SSD_INNER, PJ_Z // SSD_INNER), tok(LANES, 0),
                  pl.BlockSpec((nc, 2 * SSD_HEADS, q), lambda b: (blk0 + b, 0, 0))]
        + [full(a) for a in params] + [seq3] * len(init),
        out_specs=[pl.BlockSpec((ln, hp), lambda b: (b, 0)), seq3, seq3],
        out_shape=[jax.ShapeDtypeStruct((n_seq * ln, hp), BF16),
                   jax.ShapeDtypeStruct((n_seq, hp, SSD_STATE), F32),
                   jax.ShapeDtypeStruct((n_seq, hp, SSD_STATE), F32)],
        scratch_shapes=[pltpu.VMEM((ln, cw), BF16), pltpu.VMEM((q + 2 * CONV_HALO, cw), F32),
                        pltpu.VMEM((SSD_STATE, hp), F32), pltpu.VMEM((SSD_STATE, hp), F32),
                        pltpu.VMEM((ln, hp), F32), pltpu.VMEM((ln, hp), F32)],
        compiler_params=_cp(("arbitrary",)), name="ssd_seq",
    )(proj, proj, small, smallT3, *params, *init)


def _gla_gates(d, c, q_ref, k_ref, glr_ref, w2_ref, gb_ref, qi, ki, qo, kk, dec):
    t = GLA_BLOCK
    rows = pl.ds(pl.multiple_of(c * t, t), t)
    c0 = 2 * SSD_HEADS + GLA_RANK * d
    gp = _dot_hilo(glr_ref[rows, c0:c0 + GLA_RANK], w2_ref[0, d], w2_ref[1, d]) + gb_ref[d:d + 1, :]
    la = -_softplus(-gp) * (1.0 / GLA_TAU)
    row = lax.broadcasted_iota(I32, (t, t), 0)
    col = lax.broadcasted_iota(I32, (t, t), 1)
    mid = t // 2 - 1
    if d == 0:
        e = _dot_sel(col <= row, la)
        tot = e[t - 1:t, :]
        r = e[mid:mid + 1, :]
        fqi, fki = jnp.exp(e - r), jnp.exp(r - e)
        fq, fk = jnp.exp(e), jnp.exp(tot - e)
    else:
        e = _dot_sel(col < row, la)
        tot = e[t - 1:t, :] + la[t - 1:t, :]
        r = e[mid:mid + 1, :]
        fqi, fki = jnp.exp(r - e), jnp.exp(e - r)
        fq, fk = jnp.exp(tot - e), jnp.exp(e)
    qf = q_ref[rows, :].astype(F32) * (GLA_KEY_DIM ** -0.5)
    kf = k_ref[rows, :].astype(F32)
    qi[d, rows, :] = (qf * fqi).astype(BF16)
    ki[d, rows, :] = (kf * fki).astype(BF16)
    qo[d, rows, :] = (qf * fq).astype(BF16)
    kk[d, rows, :] = (kf * fk).astype(BF16)
    dec[d, c] = jnp.broadcast_to(jnp.exp(tot), (8, tot.shape[1]))


def _gla_load(d, c, v_ref, qi, ki, qo, kk, dec, S):
    t = GLA_BLOCK
    rows = pl.ds(pl.multiple_of(c * t, t), t)
    return (rows, qi[d, rows, :], ki[d, rows, :], qo[d, rows, :], kk[d, rows, :], dec[d, c][0:1, :],
            v_ref[rows, :], [S[h] for h in range(GLA_HEADS)])


def _gla_block(d, loaded):
    t = GLA_BLOCK
    dk, dv = GLA_KEY_DIM, GLA_VAL_DIM
    _, q_in, k_in, q_st, k_st, dec, v, states = loaded
    row = lax.broadcasted_iota(I32, (t, t), 0)
    col = lax.broadcasted_iota(I32, (t, t), 1)
    mask = row >= col if d == 0 else col >= row
    outs, new_states = [], []
    for h in range(GLA_HEADS):
        sl = slice(dk * h, dk * (h + 1))
        sc = jnp.where(mask, _dot_nt(q_in[:, sl], k_in[:, sl]), 0.0)
        vh = v[:, dv * h:dv * (h + 1)]
        st = states[h]
        outs.append(_dot(sc.astype(BF16), vh) + _dot_nt(q_st[:, sl], st.astype(BF16)))
        new_states.append(st * dec[:, sl] + _dot_tn(vh, k_st[:, sl]))
    return jnp.concatenate(outs, axis=1), new_states


def _gla_seq_kernel(*refs, has_init):
    q_ref, k_ref, v_ref, og_ref, glr_ref, w2_ref, gb_ref, nrm_ref = refs[:8]
    refs = refs[8:]
    if has_init:
        s0f_ref, s0b_ref = refs[:2]
        refs = refs[2:]
    o_ref, sf_ref, sb_ref, Sf, Sb, of, ob, qi, ki, qo, kk, dec = refs
    ln = q_ref.shape[0]
    t = GLA_BLOCK
    nc = ln // t
    dv = GLA_VAL_DIM
    for h in range(GLA_HEADS):
        if has_init:
            Sf[h] = s0f_ref[0, h].T
            Sb[h] = s0b_ref[0, h].T
        else:
            Sf[h] = jnp.zeros(Sf.shape[1:], F32)
            Sb[h] = jnp.zeros(Sb.shape[1:], F32)
    staged = (qi, ki, qo, kk, dec)

    def gate_body(c, carry):
        for d in (0, 1):
            _gla_gates(d, c, q_ref, k_ref, glr_ref, w2_ref, gb_ref, *staged)
        return carry
    lax.fori_loop(0, nc, gate_body, 0, unroll=2)

    def scan_body(c, carry):
        lf = _gla_load(0, c, v_ref, *staged, Sf)
        lb = _gla_load(1, nc - 1 - c, v_ref, *staged, Sb)
        o_f, s_f = _gla_block(0, lf)
        o_b, s_b = _gla_block(1, lb)
        of[lf[0], :] = o_f
        ob[lb[0], :] = o_b
        for h in range(GLA_HEADS):
            Sf[h] = s_f[h]
            Sb[h] = s_b[h]
        return carry
    lax.fori_loop(0, nc, scan_body, 0, unroll=2)
    for h in range(GLA_HEADS):
        sf_ref[0, h] = Sf[h].T
        sb_ref[0, h] = Sb[h].T

    def out_body(c, carry):
        rows = pl.ds(pl.multiple_of(c * t, t), t)
        for h in range(GLA_HEADS):
            vl = slice(dv * h, dv * (h + 1))
            ot = of[rows, vl] + ob[rows, vl]
            ms = jnp.mean(ot * ot, axis=-1, keepdims=True)
            on = ot * lax.rsqrt(ms + EPS) * nrm_ref[...]
            o_ref[rows, vl] = (on * _silu(og_ref[rows, vl].astype(F32))).astype(o_ref.dtype)
        return carry
    lax.fori_loop(0, nc, out_body, 0)


def _gla_call(n_seq, ln, blk0, proj, small, init, params):
    qk_w = GLA_HEADS * GLA_KEY_DIM
    v_w = GLA_HEADS * GLA_VAL_DIM
    tok = lambda w, cb: pl.BlockSpec((ln, w), lambda b: (blk0 + b, cb))
    seq4 = pl.BlockSpec((1, GLA_HEADS, GLA_KEY_DIM, GLA_VAL_DIM), lambda b: (b, 0, 0, 0))
    full = lambda a: pl.BlockSpec(a.shape, lambda b: (0,) * a.ndim)
    st_shape = jax.ShapeDtypeStruct((n_seq, GLA_HEADS, GLA_KEY_DIM, GLA_VAL_DIM), F32)
    init = () if init is None else tuple(init)
    return pl.pallas_call(
        functools.partial(_gla_seq_kernel, has_init=bool(init)), grid=(n_seq,),
        in_specs=[tok(qk_w, PJ_Q // qk_w), tok(qk_w, PJ_K // qk_w), tok(v_w, PJ_V // v_w),
                  tok(v_w, PJ_OG // v_w), tok(LANES, 0)] + [full(a) for a in params] + [seq4] * len(init),
        out_specs=[pl.BlockSpec((ln, v_w), lambda b: (b, 0)), seq4, seq4],
        out_shape=[jax.ShapeDtypeStruct((n_seq * ln, v_w), BF16), st_shape, st_shape],
        scratch_shapes=[pltpu.VMEM((GLA_HEADS, GLA_VAL_DIM, GLA_KEY_DIM), F32),
                        pltpu.VMEM((GLA_HEADS, GLA_VAL_DIM, GLA_KEY_DIM), F32),
                        pltpu.VMEM((ln, v_w), F32), pltpu.VMEM((ln, v_w), F32)]
        + [pltpu.VMEM((2, ln, qk_w), BF16)] * 4 + [pltpu.VMEM((2, ln // GLA_BLOCK, 8, qk_w), F32)],
        compiler_params=_cp(("arbitrary",)), name="gla_seq",
    )(proj, proj, proj, proj, small, *params, *init)


def l0_mixers(lay, proj, small, small_t, ssd_f0, ssd_b0, gla_f0, gla_b0, conv_w, conv_b, a_log, dt_bias, d_skip,
              ssd_norm, gate_w2, gate_b, gla_norm):
    hp = SSD_HEADS * SSD_HEAD_DIM
    ssd_p = (conv_w, conv_b.reshape(1, -1), a_log, a_log.T, dt_bias, dt_bias.T,
             jnp.repeat(d_skip, SSD_HEAD_DIM).reshape(1, hp), ssd_norm.reshape(1, hp))
    gla_p = (_hilo(gate_w2), gate_b, gla_norm.reshape(1, -1))
    np_, ns = lay.n_prompt, lay.n_sample
    assert lay.p_tok % lay.sample_len == 0
    groups = [(np_, lay.prompt_len, 0, None, None),
              (ns, lay.sample_len, lay.p_tok // lay.sample_len,
               (ssd_f0.reshape(ns, hp, SSD_STATE), ssd_b0.reshape(ns, hp, SSD_STATE)), (gla_f0, gla_b0))]
    ys, os_, states = [], [], None
    for n, ln, blk0, ssd_init, gla_init in groups:
        y, sf, sb = _ssd_call(n, ln, blk0, proj, small, small_t, ssd_init, ssd_p)
        o, gf, gb = _gla_call(n, ln, blk0, proj, small, gla_init, gla_p)
        ys.append(y)
        os_.append(o)
        if states is None:
            states = (sf, sb, gf, gb)
    return tuple(ys), tuple(os_), states


def _res_kernel(*refs, counts, ks, npt):
    in_prompt = pl.program_id(0) < npt
    streams, pos = [], 0
    for c in counts:
        streams.append(refs[pos:pos + c])
        pos += c
    w_ref, gate_ref, o_ref = refs[pos:]
    acc = None
    off = 0
    for a_refs, k in zip(streams[:-1], ks):
        part = _dot(_stream_tile(a_refs, in_prompt), w_ref[off:off + k, :])
        acc = part if acc is None else acc + part
        off += k
    o_ref[...] = _stream_tile(streams[-1], in_prompt) + gate_ref[...] * acc


def proj_residual(lay, acts, w, x, gate, tm=512):
    arrays, specs, counts = [], [], []
    for s in list(acts) + [x]:
        a, sp = _stream_specs(lay, s, tm)
        arrays += a
        specs += sp
        counts.append(len(a))
    ks = tuple(int((a[0] if isinstance(a, (tuple, list)) else a).shape[1]) for a in acts)
    mrow = lambda i: (lay.mod_row(i * tm), 0, 0)
    return pl.pallas_call(
        functools.partial(_res_kernel, counts=tuple(counts), ks=ks, npt=lay.p_tok // tm),
        grid=(lay.n_tok // tm,),
        in_specs=specs + [pl.BlockSpec(w.shape, lambda i: (0, 0)), pl.BlockSpec((None, 1, D), mrow)],
        out_specs=pl.BlockSpec((tm, D), lambda i: (i, 0)),
        out_shape=jax.ShapeDtypeStruct((lay.n_tok, D), F32),
        compiler_params=_cp(("arbitrary",)), name="proj_residual",
    )(*arrays, w, gate)


def _router_kernel(x_ref, g_ref, sc_ref, sh_ref, rw_ref, rb_ref,
                   h_ref, idx_ref, gate_ref, pos_ref, posT_ref, cnt_ref):
    tm = x_ref.shape[0]
    h = _modnorm(x_ref[...], g_ref[...], sc_ref[...], sh_ref[...])
    h_hi = h.astype(BF16)
    h_ref[...] = h_hi
    h_lo = (h - h_hi.astype(F32)).astype(BF16)
    lg = (_dot(h_hi, rw_ref[0]) + _dot(h_lo, rw_ref[0]) + _dot(h_hi, rw_ref[1])
          + rb_ref[...])
    lane = lax.broadcasted_iota(I32, (tm, LANES), 1).astype(F32)
    vals, ids = [], []
    for _ in range(TOP_K):
        m = jnp.max(lg, axis=1, keepdims=True)
        i = jnp.min(jnp.where(lg == m, lane, float(LANES)), axis=1, keepdims=True)
        vals.append(m)
        ids.append(i)
        lg = jnp.where(lane == i, -jnp.inf, lg)
    ex = [jnp.exp(v - vals[0]) for v in vals]
    den = ex[0] + ex[1] + ex[2] + ex[3]
    sel = jnp.zeros((tm, LANES), F32)
    for i in ids:
        sel = sel + (lane == i).astype(F32)
    row = lax.broadcasted_iota(I32, (tm, tm), 0)
    col = lax.broadcasted_iota(I32, (tm, tm), 1)
    before = _dot((col < row).astype(BF16), sel.astype(BF16))
    n = jnp.sum(sel, axis=0, keepdims=True)
    er = lax.broadcasted_iota(I32, (LANES, LANES), 0)
    ec = lax.broadcasted_iota(I32, (LANES, LANES), 1)
    n_al = jnp.ceil(n * (1.0 / SEG_ALIGN)) * SEG_ALIGN
    offs = _dot(jnp.broadcast_to(n_al, (8, LANES)).astype(BF16), (er < ec).astype(BF16))[0:1, :]
    slot = before + offs
    idx_o = jnp.zeros((tm, LANES), F32)
    gate_o = jnp.zeros((tm, LANES), F32)
    pos_o = jnp.zeros((tm, LANES), F32)
    for k in range(TOP_K):
        p = jnp.sum(jnp.where(lane == ids[k], slot, 0.0), axis=1, keepdims=True)
        idx_o = jnp.where(lane == k, ids[k], idx_o)
        gate_o = jnp.where(lane == k, ex[k] / den, gate_o)
        pos_o = jnp.where(lane == k, p, pos_o)
    idx_ref[...] = idx_o.astype(I32)
    gate_ref[...] = gate_o
    pos_ref[...] = pos_o.astype(I32)
    posT_ref[...] = pos_o.T[0:8, :]
    cnt_ref[0] = jnp.broadcast_to(n, (8, LANES))


def moe_router(lay, x, g, sc, sh, rw, rb):
    n_tok = x.shape[0]
    tm = TOK_TILE
    nt = n_tok // tm
    mrow = lambda i: (lay.mod_row(i * tm), 0, 0)
    tile = lambda w, dt: (pl.BlockSpec((tm, w), lambda i: (i, 0)), jax.ShapeDtypeStruct((n_tok, w), dt))
    outs = [tile(D, BF16), tile(LANES, I32), tile(LANES, F32), tile(LANES, I32),
            (pl.BlockSpec((8, tm), lambda i: (0, i)), jax.ShapeDtypeStruct((8, n_tok), F32)),
            (pl.BlockSpec((1, 8, LANES), lambda i: (i, 0, 0)), jax.ShapeDtypeStruct((nt, 8, LANES), F32))]
    return pl.pallas_call(
        _router_kernel, grid=(nt,),
        in_specs=[pl.BlockSpec((tm, D), lambda i: (i, 0)),
                  pl.BlockSpec((1, D), lambda i: (0, 0)),
                  pl.BlockSpec((None, 1, D), mrow), pl.BlockSpec((None, 1, D), mrow),
                  pl.BlockSpec((2, D, LANES), lambda i: (0, 0, 0)),
                  pl.BlockSpec((1, LANES), lambda i: (0, 0))],
        out_specs=[o[0] for o in outs], out_shape=[o[1] for o in outs],
        compiler_params=_cp(("arbitrary",)), name="moe_router",
    )(x, g.reshape(1, D), sc, sh, rw, rb)


SEG_ALIGN = 8
SEG_BITS = tuple(range(int(math.log2(TOK_TILE)), int(math.log2(SEG_ALIGN)) - 1, -1))
TILE_ROWS = TOK_TILE * TOP_K + N_EXPERTS * SEG_ALIGN


def _pow2_copies(n, src, dst, make_copy, op, bits):
    for b in bits:
        sz = 1 << b
        done = (n >> (b + 1)) << (b + 1)

        @pl.when((n & sz) != 0)
        def _():
            op(make_copy(pl.multiple_of(src + done, SEG_ALIGN), pl.multiple_of(dst + done, SEG_ALIGN), sz))


def _start_segments(i, n_ref, off_ref, dst_ref, make_copy):
    def body(e, carry):
        k = i * N_EXPERTS + e
        _pow2_copies(n_ref[k], off_ref[k], dst_ref[k], make_copy, lambda c: c.start(), SEG_BITS)
        return carry
    lax.fori_loop(0, N_EXPERTS, body, 0)


TAIL_BITS = tuple(range(int(math.log2(MOE_BLOCK)) - 1, int(math.log2(SEG_ALIGN)) - 1, -1))
TILE_BITS = tuple(range(int(math.log2(TILE_ROWS)), int(math.log2(SEG_ALIGN)) - 1, -1))


def _wait_rows(total, make_copy):
    _pow2_copies(total, 0, 0, make_copy, lambda c: c.wait(), TILE_BITS)


def _dispatch_kernel(n_ref, off_ref, dst_ref, tot_ref, tn_ref, td_ref, posT_ref, h_ref, xout_ref,
                     srt, zbuf, sems):
    i = pl.program_id(0)
    last = pl.num_programs(0) - 1
    slot = i % 2
    tm = h_ref.shape[0]
    r = lax.broadcasted_iota(I32, (TILE_ROWS, tm), 0)
    hit = jnp.zeros((TILE_ROWS, tm), jnp.bool_)
    for k in range(TOP_K):
        hit = hit | (r == posT_ref[k:k + 1, :].astype(I32))
    sel = jnp.where(hit, 1.0, 0.0).astype(BF16)
    srt[slot] = _pack_rows(_dot(sel, h_ref[...]))

    def copier(s):
        def make_copy(src, dst, sz):
            return pltpu.make_async_copy(srt.at[s, pl.ds(src, sz)], xout_ref.at[pl.ds(dst, sz)], sems.at[s])
        return make_copy

    _start_segments(i, n_ref, off_ref, dst_ref, copier(slot))

    @pl.when(i > 0)
    def _():
        _wait_rows(tot_ref[jnp.maximum(i - 1, 0)], copier(1 - slot))

    @pl.when(i == last)
    def _():
        _wait_rows(tot_ref[i], copier(slot))
        zbuf[...] = jnp.zeros_like(zbuf)
        sem = sems.at[0]

        def zero_copy(src, dst, sz):
            return pltpu.make_async_copy(zbuf.at[pl.ds(src, sz)], xout_ref.at[pl.ds(dst, sz)], sem)

        nb = xout_ref.shape[0] // MOE_BLOCK
        for op in (lambda c: c.start(), lambda c: c.wait()):
            def body(e, carry):
                _pow2_copies(tn_ref[e], 0, td_ref[e], zero_copy, op, TAIL_BITS)
                return carry
            lax.fori_loop(0, N_EXPERTS, body, 0)

            def unused(b, carry):
                op(zero_copy(0, pl.multiple_of(b * MOE_BLOCK, MOE_BLOCK), MOE_BLOCK))
                return carry
            lax.fori_loop(tn_ref[N_EXPERTS], nb, unused, 0)


def moe_dispatch(n_tab, off_tab, dst_tab, tot_tab, tail_n, tail_dst, posT, h2, n_rows):
    n_tok = h2.shape[0]
    tm = TOK_TILE
    grid_spec = pltpu.PrefetchScalarGridSpec(
        num_scalar_prefetch=6, grid=(n_tok // tm,),
        in_specs=[pl.BlockSpec((8, tm), lambda i, *_: (0, i)),
                  pl.BlockSpec((tm, D), lambda i, *_: (i, 0))],
        out_specs=pl.BlockSpec(memory_space=pl.ANY),
        scratch_shapes=[pltpu.VMEM((2, TILE_ROWS, ROW_WORDS), U32), pltpu.VMEM((MOE_BLOCK, ROW_WORDS), U32),
                        pltpu.SemaphoreType.DMA((2,))])
    return pl.pallas_call(
        _dispatch_kernel, grid_spec=grid_spec,
        out_shape=jax.ShapeDtypeStruct((n_rows, ROW_WORDS), U32),
        compiler_params=_cp(("arbitrary",)), name="moe_dispatch",
    )(n_tab, off_tab, dst_tab, tot_tab, tail_n, tail_dst, posT, h2)


def _combine_kernel(n_ref, off_ref, dst_ref, tot_ref, pos_ref, gate_ref, x_ref, g2_ref, y_ref, *rest, npt):
    o_refs, (buf, sems) = rest[:-2], rest[-2:]
    i = pl.program_id(0)
    last = pl.num_programs(0) - 1
    slot = i % 2
    tm = x_ref.shape[0]
    na = TILE_ROWS

    def copier(s):
        def make_copy(src, dst, sz):
            return pltpu.make_async_copy(y_ref.at[pl.ds(dst, sz)], buf.at[s, pl.ds(src, sz)], sems.at[s])
        return make_copy

    def fetch(tile, s):
        buf[s, tm * TOP_K:na, :] = jnp.zeros((na - tm * TOP_K, ROW_WORDS), U32)
        _start_segments(tile, n_ref, off_ref, dst_ref, copier(s))

    @pl.when(i == 0)
    def _():
        fetch(i, slot)

    @pl.when(i < last)
    def _():
        fetch(i + 1, 1 - slot)

    _wait_rows(tot_ref[i], copier(slot))
    lane = lax.broadcasted_iota(I32, (tm, na), 1)
    pw = jnp.zeros((tm, na), F32)
    for k in range(TOP_K):
        pw = pw + jnp.where(lane == pos_ref[:, k:k + 1], gate_ref[:, k:k + 1], 0.0)
    phi = pw.astype(BF16)
    plo = (pw - phi.astype(F32)).astype(BF16)
    yb = _unpack_rows(buf[slot]).astype(BF16)
    res = x_ref[...] + g2_ref[...] * (_dot(phi, yb) + _dot(plo, yb))
    if len(o_refs) == 1:
        o_refs[0][...] = res
    else:
        @pl.when(i < npt)
        def _():
            o_refs[0][...] = res

        @pl.when(i >= npt)
        def _():
            o_refs[1][...] = res


def moe_combine(lay, n_tab, off_tab, dst_tab, tot_tab, pos, gates, x, gate2, y_rows, split):
    n_tok = x.shape[0]
    tm = TOK_TILE
    npt = lay.p_tok // tm
    mrow = lambda i, *_: (lay.mod_row(i * tm), 0, 0)
    if split:
        out_specs = [pl.BlockSpec((tm, D), lambda i, *_: (jnp.minimum(i, npt - 1), 0)),
                     pl.BlockSpec((tm, D), lambda i, *_: (jnp.maximum(i - npt, 0), 0))]
        out_shape = [jax.ShapeDtypeStruct((lay.p_tok, D), F32), jax.ShapeDtypeStruct((n_tok - lay.p_tok, D), F32)]
    else:
        out_specs = pl.BlockSpec((tm, D), lambda i, *_: (i, 0))
        out_shape = jax.ShapeDtypeStruct((n_tok, D), F32)
    grid_spec = pltpu.PrefetchScalarGridSpec(
        num_scalar_prefetch=4, grid=(n_tok // tm,),
        in_specs=[pl.BlockSpec((tm, LANES), lambda i, *_: (i, 0)),
                  pl.BlockSpec((tm, LANES), lambda i, *_: (i, 0)),
                  pl.BlockSpec((tm, D), lambda i, *_: (i, 0)),
                  pl.BlockSpec((None, 1, D), mrow),
                  pl.BlockSpec(memory_space=pl.ANY)],
        out_specs=out_specs,
        scratch_shapes=[pltpu.VMEM((2, TILE_ROWS, ROW_WORDS), U32), pltpu.SemaphoreType.DMA((2,))])
    return pl.pallas_call(
        functools.partial(_combine_kernel, npt=npt), grid_spec=grid_spec, out_shape=out_shape,
        compiler_params=_cp(("arbitrary",)), name="moe_combine",
    )(n_tab, off_tab, dst_tab, tot_tab, pos, gates, x, gate2, y_rows)


def _expert_kernel(be_ref, nv_ref, nxt_ref, slot_ref, x_ref, b_ref, wg_hbm, wu_hbm, wd_hbm,
                   y_ref, wf, sems, *, layer):
    i = pl.program_id(0)
    valid = i < nv_ref[0]
    e = be_ref[i]
    slot = slot_ref[e]
    changed = jnp.logical_or(i == 0, e != be_ref[jnp.maximum(i - 1, 0)])

    def weight_copies(ex, s):
        return [pltpu.make_async_copy(w.at[layer, ex], wf.at[s, k], sems.at[s, k])
                for k, w in enumerate((wg_hbm, wu_hbm, wd_hbm))]

    @pl.when(jnp.logical_and(valid, changed))
    def _():
        @pl.when(i == 0)
        def _():
            for c in weight_copies(e, slot):
                c.start()

        nxt = nxt_ref[e]

        @pl.when(nxt >= 0)
        def _():
            for c in weight_copies(nxt, 1 - slot):
                c.start()

        for c in weight_copies(e, slot):
            c.wait()

    @pl.when(valid)
    def _():
        x = _unpack_rows(x_ref[...])
        b = b_ref[e]
        gt = jnp.minimum(_dot(x, wf[slot, 0]) + b[0:1, :], SWIGLU_LIMIT)
        up = jnp.clip(_dot(x, wf[slot, 1]) + b[1:2, :], -SWIGLU_LIMIT, SWIGLU_LIMIT)
        act = (up + 1.0) * gt * _sigmoid(SWIGLU_ALPHA * gt)
        y = _dot(act, wf[slot, 2]) + b[2:3, :]
        y_ref[...] = _pack_rows(y.astype(BF16).astype(F32))

    @pl.when(jnp.logical_not(valid))
    def _():
        y_ref[...] = jnp.zeros_like(y_ref)


def moe_experts(layer, blk_expert, n_valid, next_expert, slot, x_rows, w_gate, b_gate, w_up, b_up, w_down,
                b_down):
    n_rows = x_rows.shape[0]
    nb = n_rows // MOE_BLOCK
    depth, ne, _, ff = w_gate.shape
    assert ff == D
    rowblk = lambda i, be, nv, *_: (jnp.maximum(jnp.minimum(i, nv[0] - 1), 0), 0)
    hbm = pl.BlockSpec(memory_space=pl.ANY)
    biases = jnp.stack([b_gate, b_up, b_down], axis=2)
    grid_spec = pltpu.PrefetchScalarGridSpec(
        num_scalar_prefetch=4, grid=(nb,),
        in_specs=[pl.BlockSpec((MOE_BLOCK, ROW_WORDS), rowblk),
                  pl.BlockSpec((None, ne, 3, D), lambda i, *_: (layer, 0, 0, 0)), hbm, hbm, hbm],
        out_specs=pl.BlockSpec((MOE_BLOCK, ROW_WORDS), lambda i, *_: (i, 0)),
        scratch_shapes=[pltpu.VMEM((2, 3, D, ff), F32), pltpu.SemaphoreType.DMA((2, 3))])
    return pl.pallas_call(
        functools.partial(_expert_kernel, layer=layer), grid_spec=grid_spec,
        out_shape=jax.ShapeDtypeStruct((n_rows, ROW_WORDS), U32),
        compiler_params=_cp(("arbitrary",)), name="moe_experts",
    )(blk_expert, n_valid, next_expert, slot, x_rows, biases, w_gate, w_up, w_down)


def moe_layer(lay, layer, x, g2, sc2, sh2, gate2, router_w, router_b, w_gate, b_gate, w_up, b_up, w_down,
              b_down, split=False):
    n_tok = x.shape[0]
    nt = n_tok // TOK_TILE
    rw = jnp.zeros((D, LANES), F32).at[:, :N_EXPERTS].set(router_w)
    rw = _hilo(rw)
    rb = jnp.full((1, LANES), NEG, F32).at[0, :N_EXPERTS].set(router_b)
    h2, _, gates, pos, posT, cnt = moe_router(lay, x, g2, sc2, sh2, rw, rb)
    n_te = cnt[:, 0, :N_EXPERTS].astype(I32)
    n_te = (n_te + SEG_ALIGN - 1) // SEG_ALIGN * SEG_ALIGN
    totals = jnp.sum(n_te, axis=0)
    padded = (totals + MOE_BLOCK - 1) // MOE_BLOCK * MOE_BLOCK
    padded_end = jnp.cumsum(padded)
    pstart = padded_end - padded
    dst = pstart[None, :] + jnp.cumsum(n_te, axis=0) - n_te
    off = jnp.cumsum(n_te, axis=1) - n_te
    n_rows = nt * TILE_ROWS + N_EXPERTS * MOE_BLOCK
    nb = n_rows // MOE_BLOCK
    n_valid = (padded_end[-1] // MOE_BLOCK).astype(I32).reshape(1)
    bstart = jnp.minimum(jnp.arange(nb, dtype=I32), n_valid[0] - 1) * MOE_BLOCK
    blk_expert = jnp.minimum(jnp.sum((bstart[:, None] >= padded_end[None, :]).astype(I32), axis=1),
                             N_EXPERTS - 1).astype(I32)
    tabs = (n_te.reshape(-1).astype(I32), off.reshape(-1).astype(I32), dst.reshape(-1).astype(I32),
            jnp.sum(n_te, axis=1).astype(I32))
    tail_n = jnp.concatenate([(padded - totals).astype(I32), n_valid])
    x_rows = moe_dispatch(*tabs, tail_n, (pstart + totals).astype(I32), posT, h2, n_rows)
    owner = jnp.where(padded > 0, jnp.arange(N_EXPERTS, dtype=I32), N_EXPERTS)
    later = jnp.concatenate([lax.cummin(owner, axis=0, reverse=True)[1:], jnp.full((1,), N_EXPERTS, I32)])
    next_expert = jnp.where(later < N_EXPERTS, later, -1).astype(I32)
    slot = ((jnp.cumsum((padded > 0).astype(I32)) - 1) % 2).astype(I32)
    y_rows = moe_experts(layer, blk_expert, n_valid, next_expert, slot, x_rows, w_gate, b_gate, w_up, b_up,
                         w_down, b_down)
    return moe_combine(lay, *tabs, pos, gates, x, gate2, y_rows, split)


QKV_TN = 256
N_QK_TILES = (ATT_HEADS + ATT_KV) * ATT_HD // QKV_TN


def _qkv_kernel(x_ref, g_ref, sc_ref, sh_ref, w_ref, nw_ref, cos_ref, sin_ref, o_ref):
    tm = x_ref.shape[0]
    h = _modnorm(x_ref[...], g_ref[...], sc_ref[...], sh_ref[...]).astype(BF16)
    r = lax.broadcasted_iota(I32, (QKV_TN, QKV_TN), 0) // ATT_HD
    c = lax.broadcasted_iota(I32, (QKV_TN, QKV_TN), 1) // ATT_HD
    head_mean = jnp.where(r == c, 1.0 / ATT_HD, 0.0).astype(BF16)
    lane = lax.broadcasted_iota(I32, (tm, QKV_TN), 1)
    half = ATT_HD // 4
    first = (lane % (2 * half)) < half
    for j in range(w_ref.shape[1] // QKV_TN):
        cols = slice(QKV_TN * j, QKV_TN * (j + 1))
        acc = _dot(h, w_ref[:, cols])
        if j >= N_QK_TILES:
            o_ref[:, cols] = acc
            continue
        ms = _dot((acc * acc).astype(BF16), head_mean)
        qn = acc * lax.rsqrt(ms + EPS) * nw_ref[j]
        swapped = jnp.where(first, pltpu.roll(qn, QKV_TN - half, 1), pltpu.roll(qn, half, 1))
        o_ref[:, cols] = qn * cos_ref[...] + swapped * sin_ref[...]


def _rope_tables(sample_len):
    pos = np.arange(sample_len)
    half = ATT_HD // 4
    inv = (ROPE_THETA ** (-np.arange(half, dtype=np.float32) / half)).astype(np.float32)
    ang_r = (pos // GRID_W).astype(np.float32)[:, None] * inv[None, :]
    ang_c = (pos % GRID_W).astype(np.float32)[:, None] * inv[None, :]
    cos = np.concatenate([np.cos(ang_r)] * 2 + [np.cos(ang_c)] * 2, axis=1)
    sin = np.concatenate([-np.sin(ang_r), np.sin(ang_r), -np.sin(ang_c), np.sin(ang_c)], axis=1)
    rep = QKV_TN // ATT_HD
    return (jnp.asarray(np.tile(cos, (1, rep)), F32), jnp.asarray(np.tile(sin, (1, rep)), F32))


def qkv_proj(lay, x, g, sc, sh, w, q_norm, k_norm, tm=512):
    n_tok = x.shape[0]
    n = w.shape[1]
    nq = ATT_HEADS * ATT_HD // QKV_TN
    rep = QKV_TN // ATT_HD
    nw = jnp.concatenate([jnp.tile(jnp.tile(q_norm, rep)[None, :], (nq, 1)),
                          jnp.tile(jnp.tile(k_norm, rep)[None, :], (n // QKV_TN - nq, 1))], axis=0)
    cos, sin = _rope_tables(lay.sample_len)
    cos = jnp.concatenate([jnp.ones((tm, QKV_TN), F32), cos], axis=0)
    sin = jnp.concatenate([jnp.zeros((tm, QKV_TN), F32), sin], axis=0)
    assert lay.p_tok % tm == 0 and lay.sample_len % tm == 0
    mrow = lambda i: (lay.mod_row(i * tm), 0, 0)
    rrow = lambda i: (jnp.where(i * tm < lay.p_tok, 0, 1 + ((i * tm - lay.p_tok) % lay.sample_len) // tm), 0)
    nt = n // QKV_TN
    return pl.pallas_call(
        _qkv_kernel, grid=(n_tok // tm,),
        in_specs=[pl.BlockSpec((tm, D), lambda i: (i, 0)),
                  pl.BlockSpec((1, D), lambda i: (0, 0)),
                  pl.BlockSpec((None, 1, D), mrow), pl.BlockSpec((None, 1, D), mrow),
                  pl.BlockSpec((D, n), lambda i: (0, 0)),
                  pl.BlockSpec((nt, 1, QKV_TN), lambda i: (0, 0, 0)),
                  pl.BlockSpec((tm, QKV_TN), rrow), pl.BlockSpec((tm, QKV_TN), rrow)],
        out_specs=pl.BlockSpec((tm, n), lambda i: (i, 0)),
        out_shape=jax.ShapeDtypeStruct((n_tok, n), F32),
        compiler_params=_cp(("arbitrary",)), name="qkv_proj",
    )(x, g.reshape(1, D), sc, sh, w, nw.reshape(nt, 1, QKV_TN), cos, sin)


def _dup_group(x, g):
    blk = x[:, LANES * (g // 2):LANES * (g // 2 + 1)]
    if g % 2 == 1:
        blk = pltpu.roll(blk, ATT_HD, 1)
    lo = lax.broadcasted_iota(I32, blk.shape, 1) < ATT_HD
    low = jnp.where(lo, blk, 0.0)
    return low + pltpu.roll(low, ATT_HD, 1)


def _attend(q_ref, k_all, v_all, mask, sink_ref, o_ref):
    nq = q_ref.shape[0]
    lo = lax.broadcasted_iota(I32, (nq, LANES), 1) < ATT_HD
    first = lax.broadcasted_iota(I32, (2 * nq, 1), 0) < nq
    if mask is not None:
        mask = jnp.concatenate([mask, mask], axis=0)
    grp = ATT_HEADS // ATT_KV
    for g in range(ATT_KV):
        k2 = _dup_group(k_all, g).astype(BF16)
        v2 = _dup_group(v_all, g).astype(BF16)
        for jp in range(grp // 2):
            j = g * (grp // 2) + jp
            qp = q_ref[:, LANES * j:LANES * (j + 1)] * (ATT_HD ** -0.5)
            qs = jnp.concatenate([jnp.where(lo, qp, 0.0), jnp.where(lo, 0.0, qp)], axis=0)
            s = _dot_nt(qs.astype(BF16), k2)
            if mask is not None:
                s = jnp.where(mask, s, NEG)
            sink = jnp.where(first, sink_ref[2 * j], sink_ref[2 * j + 1])
            m = jnp.maximum(jnp.max(s, axis=1, keepdims=True), sink)
            p = jnp.exp(s - m)
            den = jnp.sum(p, axis=1, keepdims=True) + jnp.exp(sink - m)
            o = _dot(p.astype(BF16), v2) / den
            o_ref[:, LANES * j:LANES * (j + 1)] = jnp.where(lo, o[:nq], o[nq:]).astype(o_ref.dtype)


def _attn_ctx_kernel(sink_ref, q_ref, k_ref, v_ref, o_ref):
    _attend(q_ref, k_ref[...], v_ref[...], None, sink_ref, o_ref)


def attn_context(lay, qkv, sinks):
    qw = ATT_HEADS * ATT_HD
    kw = ATT_KV * ATT_HD
    ln = lay.prompt_len
    grid_spec = pltpu.PrefetchScalarGridSpec(
        num_scalar_prefetch=0, grid=(lay.n_prompt,),
        in_specs=[pl.BlockSpec(memory_space=pltpu.SMEM),
                  pl.BlockSpec((ln, qw), lambda b: (b, 0)),
                  pl.BlockSpec((ln, kw), lambda b: (b, qw // kw)),
                  pl.BlockSpec((ln, kw), lambda b: (b, qw // kw + 1))],
        out_specs=pl.BlockSpec((ln, qw), lambda b: (b, 0)))
    return pl.pallas_call(
        _attn_ctx_kernel, grid_spec=grid_spec,
        out_shape=jax.ShapeDtypeStruct((lay.p_tok, qw), BF16),
        compiler_params=_cp(("arbitrary",)), name="attn_context",
    )(sinks, qkv, qkv, qkv)


def _attn_lat_kernel(sink_ref, q_ref, kp_ref, kc_ref, kn_ref, vp_ref, vc_ref, vn_ref, ck_ref, cv_ref, o_ref,
                     *, nblk):
    i = pl.program_id(1)
    bq = ATT_BLOCK
    nctx = ck_ref.shape[1]
    k_all = jnp.concatenate([kp_ref[...], kc_ref[...], kn_ref[...], ck_ref[0]], axis=0)
    v_all = jnp.concatenate([vp_ref[...], vc_ref[...], vn_ref[...], cv_ref[0]], axis=0)
    ns = 3 * bq + nctx
    r = lax.broadcasted_iota(I32, (bq, ns), 0)
    c = lax.broadcasted_iota(I32, (bq, ns), 1)
    rel = c - r
    first_key = jnp.where(i > 0, 0, bq)
    end_key = jnp.where(i < nblk - 1, 3 * bq, 2 * bq)
    band = (rel >= bq - WINDOW) & (rel <= bq + WINDOW) & (c >= first_key) & (c < end_key)
    mask = band | (c >= 3 * bq)
    _attend(q_ref, k_all, v_all, mask, sink_ref, o_ref)


def attn_latent(lay, qkv, cache_k, cache_v, sinks):
    qw = ATT_HEADS * ATT_HD
    kw = ATT_KV * ATT_HD
    bq = ATT_BLOCK
    nblk = lay.sample_len // bq
    b0 = lay.p_tok // bq
    nctx = cache_k.shape[1]
    rb = lambda b, i: b0 + b * nblk + i
    kspec = lambda cb, sh: pl.BlockSpec(
        (bq, kw), lambda b, i: (b0 + b * nblk + jnp.clip(i + sh, 0, nblk - 1), cb))
    kc, vc = qw // kw, qw // kw + 1
    grid_spec = pltpu.PrefetchScalarGridSpec(
        num_scalar_prefetch=0, grid=(lay.n_sample, nblk),
        in_specs=[pl.BlockSpec(memory_space=pltpu.SMEM),
                  pl.BlockSpec((bq, qw), lambda b, i: (rb(b, i), 0)),
                  kspec(kc, -1), kspec(kc, 0), kspec(kc, 1),
                  kspec(vc, -1), kspec(vc, 0), kspec(vc, 1),
                  pl.BlockSpec((1, nctx, kw), lambda b, i: (b, 0, 0)),
                  pl.BlockSpec((1, nctx, kw), lambda b, i: (b, 0, 0))],
        out_specs=pl.BlockSpec((bq, qw), lambda b, i: (b * nblk + i, 0)))
    return pl.pallas_call(
        functools.partial(_attn_lat_kernel, nblk=nblk), grid_spec=grid_spec,
        out_shape=jax.ShapeDtypeStruct((lay.n_sample * lay.sample_len, qw), BF16),
        compiler_params=_cp(("arbitrary", "arbitrary")), name="attn_latent",
    )(sinks, qkv, qkv, qkv, qkv, qkv, qkv, qkv,
      cache_k.reshape(lay.n_sample, nctx, kw), cache_v.reshape(lay.n_sample, nctx, kw))


def _forward(lay, x_prompt, x_sample, state_l0_ssd_fwd, state_l0_ssd_bwd, state_l0_gla_fwd, state_l0_gla_bwd,
             cache_l1_k, cache_l1_v, c, c_ctx, ada_w, ada_b, norm1, norm2,
             l0_w_in, l0_conv_w, l0_conv_b, l0_a_log, l0_dt_bias, l0_d_skip, l0_ssd_norm,
             l0_gate_w2, l0_gate_b, l0_gla_norm, l0_w_out,
             l1_w_qkv, l1_q_norm, l1_k_norm, l1_sinks, l1_w_out,
             router_w, router_b, exp_w_gate, exp_b_gate, exp_w_up, exp_b_up, exp_w_down, exp_b_down):
    np_, ns = lay.n_prompt, lay.n_sample
    x = (x_prompt.reshape(-1, D), x_sample.reshape(-1, D))
    cond8 = jnp.zeros((8, D), F32).at[0].set(c_ctx).at[1:1 + ns].set(c)
    mod = ada_table(cond8, ada_w, ada_b)
    mods = [[mod[l, :, p * D:(p + 1) * D].reshape(8, 1, D) for p in range(N_ADA)] for l in range(2)]

    def moe(l, xx, split=False):
        return moe_layer(lay, l, xx, norm2[l], mods[l][4], mods[l][3], mods[l][5], router_w[l], router_b[l],
                         exp_w_gate, exp_b_gate, exp_w_up, exp_b_up, exp_w_down, exp_b_down, split=split)

    sp = np.cumsum((SSD_INNER, SSD_INNER + 2 * SSD_GROUPS * SSD_STATE, 2 * SSD_HEADS,
                    GLA_HEADS * GLA_KEY_DIM, GLA_HEADS * GLA_KEY_DIM,
                    GLA_HEADS * GLA_VAL_DIM, GLA_HEADS * GLA_VAL_DIM, 2 * GLA_RANK))
    cols = lambda a, b: l0_w_in[:, a:b]
    w_main = jnp.concatenate([cols(0, sp[0]), cols(sp[4], sp[5]), cols(sp[5], sp[6]), cols(sp[0], sp[1]),
                              cols(sp[2], sp[3]), cols(sp[3], sp[4])], axis=1).astype(BF16)
    w_small = jnp.concatenate([cols(sp[1], sp[2]), cols(sp[6], sp[7]),
                               jnp.zeros((D, LANES - 2 * SSD_HEADS - 2 * GLA_RANK), F32)], axis=1)
    proj, small, small_t = norm_proj(lay, x, norm1[0], mods[0][1], mods[0][0], w_main, _hilo(w_small), 512,
                                      PJ_W // 2, BF16, SSD_CHUNK)
    y_n, o_n, (ssd_f, ssd_b, gla_f, gla_b) = l0_mixers(
        lay, proj, small, small_t, state_l0_ssd_fwd, state_l0_ssd_bwd, state_l0_gla_fwd, state_l0_gla_bwd,
        l0_conv_w, l0_conv_b, l0_a_log, l0_dt_bias, l0_d_skip, l0_ssd_norm,
        l0_gate_w2, l0_gate_b, l0_gla_norm)
    x = proj_residual(lay, [y_n, o_n], l0_w_out.astype(BF16), x, mods[0][2])
    x = moe(0, x)

    qkv = qkv_proj(lay, x, norm1[1], mods[1][1], mods[1][0], l1_w_qkv.astype(BF16), l1_q_norm, l1_k_norm)
    o_ctx = attn_context(lay, qkv, l1_sinks)
    o_lat = attn_latent(lay, qkv, cache_l1_k, cache_l1_v, l1_sinks)
    x = proj_residual(lay, [(o_ctx, o_lat)], l1_w_out.astype(BF16), x, mods[1][2])
    xp, xs = moe(1, x, split=True)

    qw = ATT_HEADS * ATT_HD
    kw = ATT_KV * ATT_HD
    return (xp.reshape(x_prompt.shape), xs.reshape(x_sample.shape),
            ssd_f[:np_].reshape(np_, SSD_HEADS, SSD_HEAD_DIM, SSD_STATE),
            ssd_b[:np_].reshape(np_, SSD_HEADS, SSD_HEAD_DIM, SSD_STATE),
            gla_f[:np_], gla_b[:np_],
            qkv[:lay.p_tok, qw:qw + kw].reshape(np_, lay.prompt_len, ATT_KV, ATT_HD),
            qkv[:lay.p_tok, qw + kw:].reshape(np_, lay.prompt_len, ATT_KV, ATT_HD))


def kernel(x_prompt, x_sample, state_l0_ssd_fwd, state_l0_ssd_bwd, state_l0_gla_fwd, state_l0_gla_bwd, cache_l1_k, cache_l1_v, c, c_ctx, ada_w, ada_b, norm1, norm2, l0_w_in, l0_conv_w, l0_conv_b, l0_a_log, l0_dt_bias, l0_d_skip, l0_ssd_norm, l0_gate_w2, l0_gate_b, l0_gla_norm, l0_w_out, l1_w_qkv, l1_q_norm, l1_k_norm, l1_sinks, l1_w_out, router_w, router_b, exp_w_gate, exp_b_gate, exp_w_up, exp_b_up, exp_w_down, exp_b_down):
    lay = Layout(x_prompt.shape[0], x_prompt.shape[1], x_sample.shape[0], x_sample.shape[1])
    return _forward(lay, x_prompt, x_sample, state_l0_ssd_fwd, state_l0_ssd_bwd, state_l0_gla_fwd,
                    state_l0_gla_bwd, cache_l1_k, cache_l1_v, c, c_ctx, ada_w, ada_b, norm1, norm2,
                    l0_w_in, l0_conv_w, l0_conv_b, l0_a_log, l0_dt_bias, l0_d_skip, l0_ssd_norm,
                    l0_gate_w2, l0_gate_b, l0_gla_norm, l0_w_out,
                    l1_w_qkv, l1_q_norm, l1_k_norm, l1_sinks, l1_w_out,
                    router_w, router_b, exp_w_gate, exp_b_gate, exp_w_up, exp_b_up, exp_w_down, exp_b_down)
```

```python
import functools
import math

import numpy as np
import jax
import jax.numpy as jnp
from jax import lax
from jax.experimental import pallas as pl
from jax.experimental.pallas import tpu as pltpu

F32 = jnp.float32
BF16 = jnp.bfloat16
I32 = jnp.int32
HI = lax.Precision.HIGHEST

D = 1024
EPS = 1e-6
N_ADA = 6
SSD_HEADS = 16
SSD_HEAD_DIM = 64
SSD_INNER = 1024
SSD_STATE = 128
SSD_GROUPS = 2
SSD_CONV = 5
SSD_CHUNK = 128
GLA_HEADS = 4
GLA_KEY_DIM = 128
GLA_VAL_DIM = 256
GLA_RANK = 16
GLA_TAU = 16.0
GLA_BLOCK = 64
ATT_HEADS = 16
ATT_KV = 4
ATT_HD = 64
ATT_BLOCK = 128
WINDOW = 128
GRID_W = 64
ROPE_THETA = 10000.0
N_EXPERTS = 32
TOP_K = 4
EXPERT_FF = 1024
SWIGLU_LIMIT = 7.0
SWIGLU_ALPHA = 1.702
MOE_BLOCK = 256
TOK_TILE = 256
LANES = 128
NEG = -1e30

PJ_Z, PJ_V, PJ_OG, PJ_XBC, PJ_Q, PJ_K = 0, 1024, 2048, 3072, 4608, 5120
PJ_W = 5632
VMEM_LIMIT = 48 * 1024 * 1024


def _cp(sem, vmem=VMEM_LIMIT):
    return pltpu.CompilerParams(dimension_semantics=sem, vmem_limit_bytes=vmem)


class Layout:
    def __init__(self, n_prompt, prompt_len, n_sample, sample_len):
        self.n_prompt, self.prompt_len = n_prompt, prompt_len
        self.n_sample, self.sample_len = n_sample, sample_len
        self.p_tok = n_prompt * prompt_len
        self.n_tok = self.p_tok + n_sample * sample_len
        self.seqs = [(i * prompt_len, prompt_len) for i in range(n_prompt)]
        self.seqs += [(self.p_tok + i * sample_len, sample_len) for i in range(n_sample)]
        self.n_seq = len(self.seqs)

    def mod_row(self, start):
        return jnp.where(start < self.p_tok, 0, 1 + (start - self.p_tok) // self.sample_len)

def _sigmoid(x):
    return 1.0 / (1.0 + jnp.exp(-x))


def _silu(x):
    return x * _sigmoid(x)


def _softplus(x):
    return jnp.maximum(x, 0.0) + jnp.log(1.0 + jnp.exp(-jnp.abs(x)))


def _modnorm(x, g, sc, sh):
    ms = jnp.mean(x * x, axis=-1, keepdims=True)
    return (x * lax.rsqrt(ms + EPS) * g) * (1.0 + sc) + sh


def _dot(a, b, **kw):
    return jnp.dot(a, b, preferred_element_type=F32, **kw)


def _dot_nt(a, b):
    return lax.dot_general(a, b, (((1,), (1,)), ((), ())), preferred_element_type=F32)


def _dot_tn(a, b):
    return lax.dot_general(a, b, (((0,), (0,)), ((), ())), preferred_element_type=F32)


def _split(x, n):
    parts = []
    for _ in range(n):
        p = x.astype(BF16)
        parts.append(p)
        x = x - p.astype(F32)
    return parts


def _dot_sel(sel, x):
    sel = sel.astype(BF16)
    return sum(_dot(sel, p) for p in _split(x, 3))


def _dot_sel_r(x, sel):
    sel = sel.astype(BF16)
    return sum(_dot(p, sel) for p in _split(x, 3))


def _dot_hilo(x, w_hi, w_lo):
    x_hi, x_lo = _split(x, 2)
    return _dot(x_hi, w_hi) + _dot(x_lo, w_hi) + _dot(x_hi, w_lo)


def _hilo(w):
    hi = w.astype(BF16)
    return jnp.stack([hi, (w - hi.astype(F32)).astype(BF16)])


U32 = jnp.uint32
ROW_WORDS = D // 2
_HI_MASK = 0xFFFF0000


def _pack_rows(x):
    lo = lax.bitcast_convert_type(x[:, :ROW_WORDS], U32) >> 16
    hi = lax.bitcast_convert_type(x[:, ROW_WORDS:], U32) & jnp.uint32(_HI_MASK)
    return lo | hi


def _unpack_rows(u):
    lo = lax.bitcast_convert_type(u << 16, F32)
    hi = lax.bitcast_convert_type(u & jnp.uint32(_HI_MASK), F32)
    return jnp.concatenate([lo, hi], axis=1)


def _ada_kernel(c_ref, w_ref, b_ref, o_ref):
    o_ref[0] = _dot(_silu(c_ref[...]), w_ref[0], precision=HI) + b_ref[0]


def ada_table(cond8, ada_w, ada_b):
    depth, _, n = ada_w.shape
    tn = 1536
    return pl.pallas_call(
        _ada_kernel, grid=(depth, n // tn),
        in_specs=[pl.BlockSpec((8, D), lambda l, j: (0, 0)),
                  pl.BlockSpec((1, D, tn), lambda l, j: (l, 0, j)),
                  pl.BlockSpec((1, 1, tn), lambda l, j: (l, 0, j))],
        out_specs=pl.BlockSpec((1, 8, tn), lambda l, j: (l, 0, j)),
        out_shape=jax.ShapeDtypeStruct((depth, 8, n), F32),
        compiler_params=_cp(("arbitrary", "arbitrary")), name="ada_table",
    )(cond8, ada_w, ada_b.reshape(depth, 1, n))


def _stream_specs(lay, stream, tm):
    if not isinstance(stream, (tuple, list)):
        return [stream], [pl.BlockSpec((tm, stream.shape[1]), lambda i, *_: (i, 0))]
    assert lay.p_tok % tm == 0
    npt = lay.p_tok // tm
    w = stream[0].shape[1]
    return list(stream), [pl.BlockSpec((tm, w), lambda i, *_: (jnp.minimum(i, npt - 1), 0)),
                          pl.BlockSpec((tm, w), lambda i, *_: (jnp.maximum(i - npt, 0), 0))]


def _stream_tile(refs, in_prompt):
    if len(refs) == 1:
        return refs[0][...]
    return jnp.where(in_prompt, refs[0][...], refs[1][...])


def _proj_kernel(*refs, nx, npt):
    x_refs = refs[:nx]
    g_ref, sc_ref, sh_ref, w_ref, ws_ref, o_ref, os_ref, ost_ref, h_scr = refs[nx:]

    @pl.when(pl.program_id(1) == 0)
    def _():
        x = _stream_tile(x_refs, pl.program_id(0) < npt)
        h = _modnorm(x, g_ref[...], sc_ref[...], sh_ref[...])
        h_scr[...] = h.astype(BF16)
        small = _dot_hilo(h, ws_ref[0], ws_ref[1])
        os_ref[...] = small
        q = ost_ref.shape[2]
        for c in range(ost_ref.shape[0]):
            ost_ref[c] = small[q * c:q * (c + 1), :].T

    o_ref[...] = _dot(h_scr[...], w_ref[...]).astype(o_ref.dtype)


def norm_proj(lay, x, g, sc, sh, w, w_small, tm, tn, out_dtype, chunk):
    n_tok = lay.n_tok
    n = w.shape[1]
    ns = w_small.shape[-1]
    mrow = lambda i, j: (lay.mod_row(i * tm), 0, 0)
    xs, x_specs = _stream_specs(lay, x, tm)
    return pl.pallas_call(
        functools.partial(_proj_kernel, nx=len(xs), npt=lay.p_tok // tm), grid=(n_tok // tm, n // tn),
        in_specs=x_specs + [pl.BlockSpec((1, D), lambda i, j: (0, 0)),
                            pl.BlockSpec((None, 1, D), mrow),
                            pl.BlockSpec((None, 1, D), mrow),
                            pl.BlockSpec((D, tn), lambda i, j: (0, j)),
                            pl.BlockSpec((2, D, ns), lambda i, j: (0, 0, 0))],
        out_specs=[pl.BlockSpec((tm, tn), lambda i, j: (i, j)),
                   pl.BlockSpec((tm, ns), lambda i, j: (i, 0)),
                   pl.BlockSpec((tm // chunk, ns, chunk), lambda i, j: (i, 0, 0))],
        out_shape=[jax.ShapeDtypeStruct((n_tok, n), out_dtype),
                   jax.ShapeDtypeStruct((n_tok, ns), F32),
                   jax.ShapeDtypeStruct((n_tok // chunk, ns, chunk), F32)],
        scratch_shapes=[pltpu.VMEM((tm, D), BF16)],
        compiler_params=_cp(("arbitrary", "arbitrary")), name="norm_proj",
    )(*xs, g.reshape(1, D), sc, sh, w, w_small)


CONV_HALO = 16


def _ssd_load(d, c, xc, dtg_ref, dtgT_ref, S):
    q = SSD_CHUNK
    nh = SSD_HEADS
    rows = pl.ds(pl.multiple_of(c * q, q), q)
    return (rows, xc[rows, 0:SSD_INNER], xc[rows, SSD_INNER:SSD_INNER + 2 * SSD_GROUPS * SSD_STATE],
            dtg_ref[rows, nh * d:nh * d + nh], dtgT_ref[c, nh * d:nh * d + nh, :], S[...])


def _ssd_chunk(d, loaded, alog_ref, alogT_ref, dtb_ref, dtbT_ref):
    q = SSD_CHUNK
    nh = SSD_HEADS
    _, xs, bc, dtg, dtgT, s = loaded
    xs = xs.astype(F32)
    dt = _softplus(dtg + dtb_ref[d:d + 1, :])
    dtT = _softplus(dtgT + dtbT_ref[:, d:d + 1])
    ad = dt * (-jnp.exp(alog_ref[d:d + 1, :]))
    adT = dtT * (-jnp.exp(alogT_ref[:, d:d + 1]))
    row = lax.broadcasted_iota(I32, (q, q), 0)
    col = lax.broadcasted_iota(I32, (q, q), 1)
    if d == 0:
        e = _dot_sel(col <= row, ad)
        eT = _dot_sel_r(adT, row <= col)
        tot = e[q - 1:q, :]
        mask = row >= col
        fq = jnp.exp(e)
        fk = jnp.exp(tot - e)
    else:
        e = _dot_sel(col < row, ad)
        eT = _dot_sel_r(adT, row < col)
        tot = jnp.sum(ad, axis=0, keepdims=True)
        mask = col >= row
        fq = jnp.exp(tot - e)
        fk = jnp.exp(e)
    dec = jnp.exp(tot)
    lo = lax.broadcasted_iota(I32, (q, LANES), 1) < SSD_HEAD_DIM
    hp = nh * SSD_HEAD_DIM
    head_of = lax.broadcasted_iota(I32, (nh, hp), 1) // SSD_HEAD_DIM
    spread = (lax.broadcasted_iota(I32, (nh, hp), 0) == head_of).astype(BF16)

    def per_lane(arr, passes):
        return sum(_dot(p, spread) for p in _split(arr, passes))

    dt_x, fq_x, fk_x = per_lane(dt, 1), per_lane(fq, 1), per_lane(fk, 1)
    dec_x = per_lane(jnp.broadcast_to(dec, (8, nh)), 3)[0:1, :]
    xdt_all = xs * dt_x
    xk_all = (xdt_all * fk_x).astype(BF16)
    rep = (nh // SSD_GROUPS) // 2
    ys, s_new = [], []
    for g in range(SSD_GROUPS):
        bg_t = bc[:, SSD_STATE * g:SSD_STATE * (g + 1)].T
        cg = bc[:, SSD_STATE * (SSD_GROUPS + g):SSD_STATE * (SSD_GROUPS + g + 1)]
        gmat = _dot(cg, bg_t)
        for j in range(rep * g, rep * (g + 1)):
            a = 2 * j
            sl = slice(LANES * j, LANES * (j + 1))
            parts = []
            for hh in (a, a + 1):
                if d == 0:
                    diff = e[:, hh:hh + 1] - eT[hh:hh + 1, :]
                else:
                    diff = eT[hh:hh + 1, :] - e[:, hh:hh + 1]
                parts.append((gmat * jnp.exp(jnp.where(mask, diff, NEG))).astype(BF16))
            lhs = jnp.concatenate(parts, axis=1)
            xdt = xdt_all[:, sl]
            rhs = jnp.concatenate([jnp.where(lo, xdt, 0.0), jnp.where(lo, 0.0, xdt)], axis=0)
            sj = s[:, sl]
            ys.append(_dot(lhs, rhs.astype(BF16)) + _dot(cg, sj.astype(BF16)) * fq_x[:, sl])
            s_new.append(sj * dec_x[:, sl] + _dot(bg_t, xk_all[:, sl]))
    return jnp.concatenate(ys, axis=1), jnp.concatenate(s_new, axis=1)


def _ssd_seq_kernel(*refs, has_init):
    (xbc_ref, z_ref, dtg_ref, dtgT_ref, cw_ref, cb_ref,
     alog_ref, alogT_ref, dtb_ref, dtbT_ref, dskip_ref, nrm_ref) = refs[:12]
    refs = refs[12:]
    if has_init:
        s0f_ref, s0b_ref = refs[:2]
        refs = refs[2:]
    y_ref, sf_ref, sb_ref, xc, ext, Sf, Sb, yf, yb = refs
    ln = xbc_ref.shape[0]
    q = SSD_CHUNK
    nc = ln // q
    h = CONV_HALO
    pad = SSD_CONV // 2

    def conv_body(c, carry):
        r0 = pl.multiple_of(c * q, q)
        prev = xbc_ref[pl.ds(pl.multiple_of(jnp.maximum(r0 - h, 0), h), h), :].astype(F32)
        nxt = xbc_ref[pl.ds(pl.multiple_of(jnp.minimum(r0 + q, ln - h), h), h), :].astype(F32)
        ext[0:h, :] = jnp.where(c > 0, prev, 0.0)
        ext[h:h + q, :] = xbc_ref[pl.ds(r0, q), :].astype(F32)
        ext[h + q:h + q + h, :] = jnp.where(c < nc - 1, nxt, 0.0)
        acc = jnp.broadcast_to(cb_ref[...], (q, cb_ref.shape[1]))
        for k in range(SSD_CONV):
            acc = acc + cw_ref[k:k + 1, :] * ext[h - pad + k:h - pad + k + q, :]
        xc[pl.ds(r0, q), :] = _silu(acc).astype(xc.dtype)
        return carry
    lax.fori_loop(0, nc, conv_body, 0)

    if has_init:
        Sf[...] = s0f_ref[0].T
        Sb[...] = s0b_ref[0].T
    else:
        Sf[...] = jnp.zeros_like(Sf)
        Sb[...] = jnp.zeros_like(Sb)
    params = (alog_ref, alogT_ref, dtb_ref, dtbT_ref)

    def scan_body(c, carry):
        lf = _ssd_load(0, c, xc, dtg_ref, dtgT_ref, Sf)
        lb = _ssd_load(1, nc - 1 - c, xc, dtg_ref, dtgT_ref, Sb)
        y_f, s_f = _ssd_chunk(0, lf, *params)
        y_b, s_b = _ssd_chunk(1, lb, *params)
        yf[lf[0], :] = y_f
        yb[lb[0], :] = y_b
        Sf[...] = s_f
        Sb[...] = s_b
        return carry
    lax.fori_loop(0, nc, scan_body, 0, unroll=2)
    sf_ref[0] = Sf[...].T
    sb_ref[0] = Sb[...].T

    def out_body(c, carry):
        rows = pl.ds(pl.multiple_of(c * q, q), q)
        ytot = yf[rows, :] + yb[rows, :] + dskip_ref[...] * xc[rows, 0:SSD_INNER].astype(F32)
        yg = ytot * _silu(z_ref[rows, :].astype(F32))
        ms = jnp.mean(yg * yg, axis=-1, keepdims=True)
        y_ref[rows, :] = (yg * lax.rsqrt(ms + EPS) * nrm_ref[...]).astype(y_ref.dtype)
        return carry
    lax.fori_loop(0, nc, out_body, 0)


def _ssd_call(n_seq, ln, blk0, proj, small, smallT3, init, params):
    q = SSD_CHUNK
    hp = SSD_HEADS * SSD_HEAD_DIM
    cw = SSD_INNER + 2 * SSD_GROUPS * SSD_STATE
    nc = ln // q
    assert PJ_XBC % cw == 0 and PJ_Z % SSD_INNER == 0
    tok = lambda w, cb: pl.BlockSpec((ln, w), lambda b: (blk0 + b, cb))
    seq3 = pl.BlockSpec((1, hp, SSD_STATE), lambda b: (b, 0, 0))
    full = lambda a: pl.BlockSpec(a.shape, lambda b: (0,) * a.ndim)
    init = () if init is None else tuple(init)
    return pl.pallas_call(
        functools.partial(_ssd_seq_kernel, has_init=bool(init)), grid=(n_seq,),
        in_specs=[tok(cw, PJ_XBC // cw), tok(SSD_INNER, PJ_Z // SSD_INNER), tok(LANES, 0),
                  pl.BlockSpec((nc, 2 * SSD_HEADS, q), lambda b: (blk0 + b, 0, 0))]
        + [full(a) for a in params] + [seq3] * len(init),
        out_specs=[pl.BlockSpec((ln, hp), lambda b: (b, 0)), seq3, seq3],
        out_shape=[jax.ShapeDtypeStruct((n_seq * ln, hp), BF16),
                   jax.ShapeDtypeStruct((n_seq, hp, SSD_STATE), F32),
                   jax.ShapeDtypeStruct((n_seq, hp, SSD_STATE), F32)],
        scratch_shapes=[pltpu.VMEM((ln, cw), BF16), pltpu.VMEM((q + 2 * CONV_HALO, cw), F32),
                        pltpu.VMEM((SSD_STATE, hp), F32), pltpu.VMEM((SSD_STATE, hp), F32),
                        pltpu.VMEM((ln, hp), F32), pltpu.VMEM((ln, hp), F32)],
        compiler_params=_cp(("arbitrary",)), name="ssd_seq",
    )(proj, proj, small, smallT3, *params, *init)


def _gla_gates(d, c, q_ref, k_ref, glr_ref, w2_ref, gb_ref, qi, ki, qo, kk, dec):
    t = GLA_BLOCK
    rows = pl.ds(pl.multiple_of(c * t, t), t)
    c0 = 2 * SSD_HEADS + GLA_RANK * d
    gp = _dot_hilo(glr_ref[rows, c0:c0 + GLA_RANK], w2_ref[0, d], w2_ref[1, d]) + gb_ref[d:d + 1, :]
    la = -_softplus(-gp) * (1.0 / GLA_TAU)
    row = lax.broadcasted_iota(I32, (t, t), 0)
    col = lax.broadcasted_iota(I32, (t, t), 1)
    mid = t // 2 - 1
    if d == 0:
        e = _dot_sel(col <= row, la)
        tot = e[t - 1:t, :]
        r = e[mid:mid + 1, :]
        fqi, fki = jnp.exp(e - r), jnp.exp(r - e)
        fq, fk = jnp.exp(e), jnp.exp(tot - e)
    else:
        e = _dot_sel(col < row, la)
        tot = e[t - 1:t, :] + la[t - 1:t, :]
        r = e[mid:mid + 1, :]
        fqi, fki = jnp.exp(r - e), jnp.exp(e - r)
        fq, fk = jnp.exp(tot - e), jnp.exp(e)
    qf = q_ref[rows, :].astype(F32) * (GLA_KEY_DIM ** -0.5)
    kf = k_ref[rows, :].astype(F32)
    qi[d, rows, :] = (qf * fqi).astype(BF16)
    ki[d, rows, :] = (kf * fki).astype(BF16)
    qo[d, rows, :] = (qf * fq).astype(BF16)
    kk[d, rows, :] = (kf * fk).astype(BF16)
    dec[d, c] = jnp.broadcast_to(jnp.exp(tot), (8, tot.shape[1]))


def _gla_load(d, c, v_ref, qi, ki, qo, kk, dec, S):
    t = GLA_BLOCK
    rows = pl.ds(pl.multiple_of(c * t, t), t)
    return (rows, qi[d, rows, :], ki[d, rows, :], qo[d, rows, :], kk[d, rows, :], dec[d, c][0:1, :],
            v_ref[rows, :], [S[h] for h in range(GLA_HEADS)])


def _gla_block(d, loaded):
    t = GLA_BLOCK
    dk, dv = GLA_KEY_DIM, GLA_VAL_DIM
    _, q_in, k_in, q_st, k_st, dec, v, states = loaded
    row = lax.broadcasted_iota(I32, (t, t), 0)
    col = lax.broadcasted_iota(I32, (t, t), 1)
    mask = row >= col if d == 0 else col >= row
    outs, new_states = [], []
    for h in range(GLA_HEADS):
        sl = slice(dk * h, dk * (h + 1))
        sc = jnp.where(mask, _dot_nt(q_in[:, sl], k_in[:, sl]), 0.0)
        vh = v[:, dv * h:dv * (h + 1)]
        st = states[h]
        outs.append(_dot(sc.astype(BF16), vh) + _dot_nt(q_st[:, sl], st.astype(BF16)))
        new_states.append(st * dec[:, sl] + _dot_tn(vh, k_st[:, sl]))
    return jnp.concatenate(outs, axis=1), new_states


def _gla_seq_kernel(*refs, has_init):
    q_ref, k_ref, v_ref, og_ref, glr_ref, w2_ref, gb_ref, nrm_ref = refs[:8]
    refs = refs[8:]
    if has_init:
        s0f_ref, s0b_ref = refs[:2]
        refs = refs[2:]
    o_ref, sf_ref, sb_ref, Sf, Sb, of, ob, qi, ki, qo, kk, dec = refs
    ln = q_ref.shape[0]
    t = GLA_BLOCK
    nc = ln // t
    dv = GLA_VAL_DIM
    for h in range(GLA_HEADS):
        if has_init:
            Sf[h] = s0f_ref[0, h].T
            Sb[h] = s0b_ref[0, h].T
        else:
            Sf[h] = jnp.zeros(Sf.shape[1:], F32)
            Sb[h] = jnp.zeros(Sb.shape[1:], F32)
    staged = (qi, ki, qo, kk, dec)

    def gate_body(c, carry):
        for d in (0, 1):
            _gla_gates(d, c, q_ref, k_ref, glr_ref, w2_ref, gb_ref, *staged)
        return carry
    lax.fori_loop(0, nc, gate_body, 0, unroll=2)

    def scan_body(c, carry):
        lf = _gla_load(0, c, v_ref, *staged, Sf)
        lb = _gla_load(1, nc - 1 - c, v_ref, *staged, Sb)
        o_f, s_f = _gla_block(0, lf)
        o_b, s_b = _gla_block(1, lb)
        of[lf[0], :] = o_f
        ob[lb[0], :] = o_b
        for h in range(GLA_HEADS):
            Sf[h] = s_f[h]
            Sb[h] = s_b[h]
        return carry
    lax.fori_loop(0, nc, scan_body, 0, unroll=2)
    for h in range(GLA_HEADS):
        sf_ref[0, h] = Sf[h].T
        sb_ref[0, h] = Sb[h].T

    def out_body(c, carry):
        rows = pl.ds(pl.multiple_of(c * t, t), t)
        for h in range(GLA_HEADS):
            vl = slice(dv * h, dv * (h + 1))
            ot = of[rows, vl] + ob[rows, vl]
            ms = jnp.mean(ot * ot, axis=-1, keepdims=True)
            on = ot * lax.rsqrt(ms + EPS) * nrm_ref[...]
            o_ref[rows, vl] = (on * _silu(og_ref[rows, vl].astype(F32))).astype(o_ref.dtype)
        return carry
    lax.fori_loop(0, nc, out_body, 0)


def _gla_call(n_seq, ln, blk0, proj, small, init, params):
    qk_w = GLA_HEADS * GLA_KEY_DIM
    v_w = GLA_HEADS * GLA_VAL_DIM
    tok = lambda w, cb: pl.BlockSpec((ln, w), lambda b: (blk0 + b, cb))
    seq4 = pl.BlockSpec((1, GLA_HEADS, GLA_KEY_DIM, GLA_VAL_DIM), lambda b: (b, 0, 0, 0))
    full = lambda a: pl.BlockSpec(a.shape, lambda b: (0,) * a.ndim)
    st_shape = jax.ShapeDtypeStruct((n_seq, GLA_HEADS, GLA_KEY_DIM, GLA_VAL_DIM), F32)
    init = () if init is None else tuple(init)
    return pl.pallas_call(
        functools.partial(_gla_seq_kernel, has_init=bool(init)), grid=(n_seq,),
        in_specs=[tok(qk_w, PJ_Q // qk_w), tok(qk_w, PJ_K // qk_w), tok(v_w, PJ_V // v_w),
                  tok(v_w, PJ_OG // v_w), tok(LANES, 0)] + [full(a) for a in params] + [seq4] * len(init),
        out_specs=[pl.BlockSpec((ln, v_w), lambda b: (b, 0)), seq4, seq4],
        out_shape=[jax.ShapeDtypeStruct((n_seq * ln, v_w), BF16), st_shape, st_shape],
        scratch_shapes=[pltpu.VMEM((GLA_HEADS, GLA_VAL_DIM, GLA_KEY_DIM), F32),
                        pltpu.VMEM((GLA_HEADS, GLA_VAL_DIM, GLA_KEY_DIM), F32),
                        pltpu.VMEM((ln, v_w), F32), pltpu.VMEM((ln, v_w), F32)]
        + [pltpu.VMEM((2, ln, qk_w), BF16)] * 4 + [pltpu.VMEM((2, ln // GLA_BLOCK, 8, qk_w), F32)],
        compiler_params=_cp(("arbitrary",)), name="gla_seq",
    )(proj, proj, proj, proj, small, *params, *init)


def l0_mixers(lay, proj, small, small_t, ssd_f0, ssd_b0, gla_f0, gla_b0, conv_w, conv_b, a_log, dt_bias, d_skip,
              ssd_norm, gate_w2, gate_b, gla_norm):
    hp = SSD_HEADS * SSD_HEAD_DIM
    ssd_p = (conv_w, conv_b.reshape(1, -1), a_log, a_log.T, dt_bias, dt_bias.T,
             jnp.repeat(d_skip, SSD_HEAD_DIM).reshape(1, hp), ssd_norm.reshape(1, hp))
    gla_p = (_hilo(gate_w2), gate_b, gla_norm.reshape(1, -1))
    np_, ns = lay.n_prompt, lay.n_sample
    assert lay.p_tok % lay.sample_len == 0
    groups = [(np_, lay.prompt_len, 0, None, None),
              (ns, lay.sample_len, lay.p_tok // lay.sample_len,
               (ssd_f0.reshape(ns, hp, SSD_STATE), ssd_b0.reshape(ns, hp, SSD_STATE)), (gla_f0, gla_b0))]
    ys, os_, states = [], [], None
    for n, ln, blk0, ssd_init, gla_init in groups:
        y, sf, sb = _ssd_call(n, ln, blk0, proj, small, small_t, ssd_init, ssd_p)
        o, gf, gb = _gla_call(n, ln, blk0, proj, small, gla_init, gla_p)
        ys.append(y)
        os_.append(o)
        if states is None:
            states = (sf, sb, gf, gb)
    return tuple(ys), tuple(os_), states


def _res_kernel(*refs, counts, ks, npt):
    in_prompt = pl.program_id(0) < npt
    streams, pos = [], 0
    for c in counts:
        streams.append(refs[pos:pos + c])
        pos += c
    w_ref, gate_ref, o_ref = refs[pos:]
    acc = None
    off = 0
    for a_refs, k in zip(streams[:-1], ks):
        part = _dot(_stream_tile(a_refs, in_prompt), w_ref[off:off + k, :])
        acc = part if acc is None else acc + part
        off += k
    o_ref[...] = _stream_tile(streams[-1], in_prompt) + gate_ref[...] * acc


def proj_residual(lay, acts, w, x, gate, tm=512):
    arrays, specs, counts = [], [], []
    for s in list(acts) + [x]:
        a, sp = _stream_specs(lay, s, tm)
        arrays += a
        specs += sp
        counts.append(len(a))
    ks = tuple(int((a[0] if isinstance(a, (tuple, list)) else a).shape[1]) for a in acts)
    mrow = lambda i: (lay.mod_row(i * tm), 0, 0)
    return pl.pallas_call(
        functools.partial(_res_kernel, counts=tuple(counts), ks=ks, npt=lay.p_tok // tm),
        grid=(lay.n_tok // tm,),
        in_specs=specs + [pl.BlockSpec(w.shape, lambda i: (0, 0)), pl.BlockSpec((None, 1, D), mrow)],
        out_specs=pl.BlockSpec((tm, D), lambda i: (i, 0)),
        out_shape=jax.ShapeDtypeStruct((lay.n_tok, D), F32),
        compiler_params=_cp(("arbitrary",)), name="proj_residual",
    )(*arrays, w, gate)


def _router_kernel(x_ref, g_ref, sc_ref, sh_ref, rw_ref, rb_ref,
                   h_ref, idx_ref, gate_ref, pos_ref, posT_ref, cnt_ref):
    tm = x_ref.shape[0]
    h = _modnorm(x_ref[...], g_ref[...], sc_ref[...], sh_ref[...])
    h_hi = h.astype(BF16)
    h_ref[...] = h_hi
    h_lo = (h - h_hi.astype(F32)).astype(BF16)
    lg = (_dot(h_hi, rw_ref[0]) + _dot(h_lo, rw_ref[0]) + _dot(h_hi, rw_ref[1])
          + rb_ref[...])
    lane = lax.broadcasted_iota(I32, (tm, LANES), 1).astype(F32)
    vals, ids = [], []
    for _ in range(TOP_K):
        m = jnp.max(lg, axis=1, keepdims=True)
        i = jnp.min(jnp.where(lg == m, lane, float(LANES)), axis=1, keepdims=True)
        vals.append(m)
        ids.append(i)
        lg = jnp.where(lane == i, -jnp.inf, lg)
    ex = [jnp.exp(v - vals[0]) for v in vals]
    den = ex[0] + ex[1] + ex[2] + ex[3]
    sel = jnp.zeros((tm, LANES), F32)
    for i in ids:
        sel = sel + (lane == i).astype(F32)
    row = lax.broadcasted_iota(I32, (tm, tm), 0)
    col = lax.broadcasted_iota(I32, (tm, tm), 1)
    before = _dot((col < row).astype(BF16), sel.astype(BF16))
    n = jnp.sum(sel, axis=0, keepdims=True)
    er = lax.broadcasted_iota(I32, (LANES, LANES), 0)
    ec = lax.broadcasted_iota(I32, (LANES, LANES), 1)
    n_al = jnp.ceil(n * (1.0 / SEG_ALIGN)) * SEG_ALIGN
    offs = _dot(jnp.broadcast_to(n_al, (8, LANES)).astype(BF16), (er < ec).astype(BF16))[0:1, :]
    slot = before + offs
    idx_o = jnp.zeros((tm, LANES), F32)
    gate_o = jnp.zeros((tm, LANES), F32)
    pos_o = jnp.zeros((tm, LANES), F32)
    for k in range(TOP_K):
        p = jnp.sum(jnp.where(lane == ids[k], slot, 0.0), axis=1, keepdims=True)
        idx_o = jnp.where(lane == k, ids[k], idx_o)
        gate_o = jnp.where(lane == k, ex[k] / den, gate_o)
        pos_o = jnp.where(lane == k, p, pos_o)
    idx_ref[...] = idx_o.astype(I32)
    gate_ref[...] = gate_o
    pos_ref[...] = pos_o.astype(I32)
    posT_ref[...] = pos_o.T[0:8, :]
    cnt_ref[0] = jnp.broadcast_to(n, (8, LANES))


def moe_router(lay, x, g, sc, sh, rw, rb):
    n_tok = x.shape[0]
    tm = TOK_TILE
    nt = n_tok // tm
    mrow = lambda i: (lay.mod_row(i * tm), 0, 0)
    tile = lambda w, dt: (pl.BlockSpec((tm, w), lambda i: (i, 0)), jax.ShapeDtypeStruct((n_tok, w), dt))
    outs = [tile(D, BF16), tile(LANES, I32), tile(LANES, F32), tile(LANES, I32),
            (pl.BlockSpec((8, tm), lambda i: (0, i)), jax.ShapeDtypeStruct((8, n_tok), F32)),
            (pl.BlockSpec((1, 8, LANES), lambda i: (i, 0, 0)), jax.ShapeDtypeStruct((nt, 8, LANES), F32))]
    return pl.pallas_call(
        _router_kernel, grid=(nt,),
        in_specs=[pl.BlockSpec((tm, D), lambda i: (i, 0)),
                  pl.BlockSpec((1, D), lambda i: (0, 0)),
                  pl.BlockSpec((None, 1, D), mrow), pl.BlockSpec((None, 1, D), mrow),
                  pl.BlockSpec((2, D, LANES), lambda i: (0, 0, 0)),
                  pl.BlockSpec((1, LANES), lambda i: (0, 0))],
        out_specs=[o[0] for o in outs], out_shape=[o[1] for o in outs],
        compiler_params=_cp(("arbitrary",)), name="moe_router",
    )(x, g.reshape(1, D), sc, sh, rw, rb)


SEG_ALIGN = 8
SEG_CHUNK = 16
REST_BITS = tuple(range(int(math.log2(SEG_CHUNK)) - 1, int(math.log2(SEG_ALIGN)) - 1, -1))
TILE_ROWS = TOK_TILE * TOP_K + N_EXPERTS * SEG_ALIGN


def _pow2_copies(n, src, dst, make_copy, op, bits):
    for b in bits:
        sz = 1 << b
        done = (n >> (b + 1)) << (b + 1)

        @pl.when((n & sz) != 0)
        def _():
            op(make_copy(pl.multiple_of(src + done, SEG_ALIGN), pl.multiple_of(dst + done, SEG_ALIGN), sz))


def _start_segments(i, n_ref, off_ref, dst_ref, make_copy):
    def body(e, carry):
        k = i * N_EXPERTS + e
        n, src, dst = n_ref[k], off_ref[k], dst_ref[k]

        def chunk(j, c):
            o = pl.multiple_of(j * SEG_CHUNK, SEG_CHUNK)
            make_copy(pl.multiple_of(src + o, SEG_ALIGN), pl.multiple_of(dst + o, SEG_ALIGN), SEG_CHUNK).start()
            return c
        shift = int(math.log2(SEG_CHUNK))
        full = n >> shift
        lax.fori_loop(0, full, chunk, 0)
        done = full << shift
        _pow2_copies(n - done, src + done, dst + done, make_copy, lambda c: c.start(), REST_BITS)
        return carry
    lax.fori_loop(0, N_EXPERTS, body, 0)


TAIL_BITS = tuple(range(int(math.log2(MOE_BLOCK)) - 1, int(math.log2(SEG_ALIGN)) - 1, -1))
TILE_BITS = tuple(range(int(math.log2(TILE_ROWS)), int(math.log2(SEG_ALIGN)) - 1, -1))


def _wait_rows(total, make_copy):
    _pow2_copies(total, 0, 0, make_copy, lambda c: c.wait(), TILE_BITS)


def _dispatch_kernel(n_ref, off_ref, dst_ref, tot_ref, tn_ref, td_ref, posT_ref, h_ref, xout_ref,
                     srt, zbuf, sems):
    i = pl.program_id(0)
    last = pl.num_programs(0) - 1
    slot = i % 2
    tm = h_ref.shape[0]
    r = lax.broadcasted_iota(I32, (TILE_ROWS, tm), 0)
    hit = jnp.zeros((TILE_ROWS, tm), jnp.bool_)
    for k in range(TOP_K):
        hit = hit | (r == posT_ref[k:k + 1, :].astype(I32))
    sel = jnp.where(hit, 1.0, 0.0).astype(BF16)
    srt[slot] = _pack_rows(_dot(sel, h_ref[...]))

    def copier(s):
        def make_copy(src, dst, sz):
            return pltpu.make_async_copy(srt.at[s, pl.ds(src, sz)], xout_ref.at[pl.ds(dst, sz)], sems.at[s])
        return make_copy

    _start_segments(i, n_ref, off_ref, dst_ref, copier(slot))

    @pl.when(i > 0)
    def _():
        _wait_rows(tot_ref[jnp.maximum(i - 1, 0)], copier(1 - slot))

    @pl.when(i == last)
    def _():
        _wait_rows(tot_ref[i], copier(slot))
        zbuf[...] = jnp.zeros_like(zbuf)
        sem = sems.at[0]

        def zero_copy(src, dst, sz):
            return pltpu.make_async_copy(zbuf.at[pl.ds(src, sz)], xout_ref.at[pl.ds(dst, sz)], sem)

        nb = xout_ref.shape[0] // MOE_BLOCK
        for op in (lambda c: c.start(), lambda c: c.wait()):
            def body(e, carry):
                _pow2_copies(tn_ref[e], 0, td_ref[e], zero_copy, op, TAIL_BITS)
                return carry
            lax.fori_loop(0, N_EXPERTS, body, 0)

            def unused(b, carry):
                op(zero_copy(0, pl.multiple_of(b * MOE_BLOCK, MOE_BLOCK), MOE_BLOCK))
                return carry
            lax.fori_loop(tn_ref[N_EXPERTS], nb, unused, 0)


def moe_dispatch(n_tab, off_tab, dst_tab, tot_tab, tail_n, tail_dst, posT, h2, n_rows):
    n_tok = h2.shape[0]
    tm = TOK_TILE
    grid_spec = pltpu.PrefetchScalarGridSpec(
        num_scalar_prefetch=6, grid=(n_tok // tm,),
        in_specs=[pl.BlockSpec((8, tm), lambda i, *_: (0, i)),
                  pl.BlockSpec((tm, D), lambda i, *_: (i, 0))],
        out_specs=pl.BlockSpec(memory_space=pl.ANY),
        scratch_shapes=[pltpu.VMEM((2, TILE_ROWS, ROW_WORDS), U32), pltpu.VMEM((MOE_BLOCK, ROW_WORDS), U32),
                        pltpu.SemaphoreType.DMA((2,))])
    return pl.pallas_call(
        _dispatch_kernel, grid_spec=grid_spec,
        out_shape=jax.ShapeDtypeStruct((n_rows, ROW_WORDS), U32),
        compiler_params=_cp(("arbitrary",)), name="moe_dispatch",
    )(n_tab, off_tab, dst_tab, tot_tab, tail_n, tail_dst, posT, h2)


def _combine_kernel(n_ref, off_ref, dst_ref, tot_ref, pos_ref, gate_ref, x_ref, g2_ref, y_ref, *rest, npt):
    o_refs, (buf, sems) = rest[:-2], rest[-2:]
    i = pl.program_id(0)
    last = pl.num_programs(0) - 1
    slot = i % 2
    tm = x_ref.shape[0]
    na = TILE_ROWS

    def copier(s):
        def make_copy(src, dst, sz):
            return pltpu.make_async_copy(y_ref.at[pl.ds(dst, sz)], buf.at[s, pl.ds(src, sz)], sems.at[s])
        return make_copy

    def fetch(tile, s):
        buf[s, tm * TOP_K:na, :] = jnp.zeros((na - tm * TOP_K, ROW_WORDS), U32)
        _start_segments(tile, n_ref, off_ref, dst_ref, copier(s))

    @pl.when(i == 0)
    def _():
        fetch(i, slot)

    @pl.when(i < last)
    def _():
        fetch(i + 1, 1 - slot)

    _wait_rows(tot_ref[i], copier(slot))
    lane = lax.broadcasted_iota(I32, (tm, na), 1)
    pw = jnp.zeros((tm, na), F32)
    for k in range(TOP_K):
        pw = pw + jnp.where(lane == pos_ref[:, k:k + 1], gate_ref[:, k:k + 1], 0.0)
    phi = pw.astype(BF16)
    plo = (pw - phi.astype(F32)).astype(BF16)
    yb = _unpack_rows(buf[slot]).astype(BF16)
    res = x_ref[...] + g2_ref[...] * (_dot(phi, yb) + _dot(plo, yb))
    if len(o_refs) == 1:
        o_refs[0][...] = res
    else:
        @pl.when(i < npt)
        def _():
            o_refs[0][...] = res

        @pl.when(i >= npt)
        def _():
            o_refs[1][...] = res


def moe_combine(lay, n_tab, off_tab, dst_tab, tot_tab, pos, gates, x, gate2, y_rows, split):
    n_tok = x.shape[0]
    tm = TOK_TILE
    npt = lay.p_tok // tm
    mrow = lambda i, *_: (lay.mod_row(i * tm), 0, 0)
    if split:
        out_specs = [pl.BlockSpec((tm, D), lambda i, *_: (jnp.minimum(i, npt - 1), 0)),
                     pl.BlockSpec((tm, D), lambda i, *_: (jnp.maximum(i - npt, 0), 0))]
        out_shape = [jax.ShapeDtypeStruct((lay.p_tok, D), F32), jax.ShapeDtypeStruct((n_tok - lay.p_tok, D), F32)]
    else:
        out_specs = pl.BlockSpec((tm, D), lambda i, *_: (i, 0))
        out_shape = jax.ShapeDtypeStruct((n_tok, D), F32)
    grid_spec = pltpu.PrefetchScalarGridSpec(
        num_scalar_prefetch=4, grid=(n_tok // tm,),
        in_specs=[pl.BlockSpec((tm, LANES), lambda i, *_: (i, 0)),
                  pl.BlockSpec((tm, LANES), lambda i, *_: (i, 0)),
                  pl.BlockSpec((tm, D), lambda i, *_: (i, 0)),
                  pl.BlockSpec((None, 1, D), mrow),
                  pl.BlockSpec(memory_space=pl.ANY)],
        out_specs=out_specs,
        scratch_shapes=[pltpu.VMEM((2, TILE_ROWS, ROW_WORDS), U32), pltpu.SemaphoreType.DMA((2,))])
    return pl.pallas_call(
        functools.partial(_combine_kernel, npt=npt), grid_spec=grid_spec, out_shape=out_shape,
        compiler_params=_cp(("arbitrary",)), name="moe_combine",
    )(n_tab, off_tab, dst_tab, tot_tab, pos, gates, x, gate2, y_rows)


def _expert_kernel(be_ref, nv_ref, nxt_ref, slot_ref, x_ref, b_ref, wg_hbm, wu_hbm, wd_hbm,
                   y_ref, wf, sems, *, layer):
    i = pl.program_id(0)
    valid = i < nv_ref[0]
    e = be_ref[i]
    slot = slot_ref[e]
    changed = jnp.logical_or(i == 0, e != be_ref[jnp.maximum(i - 1, 0)])

    def weight_copies(ex, s):
        return [pltpu.make_async_copy(w.at[layer, ex], wf.at[s, k], sems.at[s, k])
                for k, w in enumerate((wg_hbm, wu_hbm, wd_hbm))]

    @pl.when(jnp.logical_and(valid, changed))
    def _():
        @pl.when(i == 0)
        def _():
            for c in weight_copies(e, slot):
                c.start()

        nxt = nxt_ref[e]

        @pl.when(nxt >= 0)
        def _():
            for c in weight_copies(nxt, 1 - slot):
                c.start()

        for c in weight_copies(e, slot):
            c.wait()

    @pl.when(valid)
    def _():
        x = _unpack_rows(x_ref[...])
        b = b_ref[e]
        gt = jnp.minimum(_dot(x, wf[slot, 0]) + b[0:1, :], SWIGLU_LIMIT)
        up = jnp.clip(_dot(x, wf[slot, 1]) + b[1:2, :], -SWIGLU_LIMIT, SWIGLU_LIMIT)
        act = (up + 1.0) * gt * _sigmoid(SWIGLU_ALPHA * gt)
        y = _dot(act, wf[slot, 2]) + b[2:3, :]
        y_ref[...] = _pack_rows(y.astype(BF16).astype(F32))

    @pl.when(jnp.logical_not(valid))
    def _():
        y_ref[...] = jnp.zeros_like(y_ref)


def moe_experts(layer, blk_expert, n_valid, next_expert, slot, x_rows, w_gate, b_gate, w_up, b_up, w_down,
                b_down):
    n_rows = x_rows.shape[0]
    nb = n_rows // MOE_BLOCK
    depth, ne, _, ff = w_gate.shape
    assert ff == D
    rowblk = lambda i, be, nv, *_: (jnp.maximum(jnp.minimum(i, nv[0] - 1), 0), 0)
    hbm = pl.BlockSpec(memory_space=pl.ANY)
    biases = jnp.stack([b_gate, b_up, b_down], axis=2)
    grid_spec = pltpu.PrefetchScalarGridSpec(
        num_scalar_prefetch=4, grid=(nb,),
        in_specs=[pl.BlockSpec((MOE_BLOCK, ROW_WORDS), rowblk),
                  pl.BlockSpec((None, ne, 3, D), lambda i, *_: (layer, 0, 0, 0)), hbm, hbm, hbm],
        out_specs=pl.BlockSpec((MOE_BLOCK, ROW_WORDS), lambda i, *_: (i, 0)),
        scratch_shapes=[pltpu.VMEM((2, 3, D, ff), F32), pltpu.SemaphoreType.DMA((2, 3))])
    return pl.pallas_call(
        functools.partial(_expert_kernel, layer=layer), grid_spec=grid_spec,
        out_shape=jax.ShapeDtypeStruct((n_rows, ROW_WORDS), U32),
        compiler_params=_cp(("arbitrary",)), name="moe_experts",
    )(blk_expert, n_valid, next_expert, slot, x_rows, biases, w_gate, w_up, w_down)


def moe_layer(lay, layer, x, g2, sc2, sh2, gate2, router_w, router_b, w_gate, b_gate, w_up, b_up, w_down,
              b_down, split=False):
    n_tok = x.shape[0]
    nt = n_tok // TOK_TILE
    rw = jnp.zeros((D, LANES), F32).at[:, :N_EXPERTS].set(router_w)
    rw = _hilo(rw)
    rb = jnp.full((1, LANES), NEG, F32).at[0, :N_EXPERTS].set(router_b)
    h2, _, gates, pos, posT, cnt = moe_router(lay, x, g2, sc2, sh2, rw, rb)
    n_te = cnt[:, 0, :N_EXPERTS].astype(I32)
    n_te = (n_te + SEG_ALIGN - 1) // SEG_ALIGN * SEG_ALIGN
    totals = jnp.sum(n_te, axis=0)
    padded = (totals + MOE_BLOCK - 1) // MOE_BLOCK * MOE_BLOCK
    padded_end = jnp.cumsum(padded)
    pstart = padded_end - padded
    dst = pstart[None, :] + jnp.cumsum(n_te, axis=0) - n_te
    off = jnp.cumsum(n_te, axis=1) - n_te
    n_rows = nt * TILE_ROWS + N_EXPERTS * MOE_BLOCK
    nb = n_rows // MOE_BLOCK
    n_valid = (padded_end[-1] // MOE_BLOCK).astype(I32).reshape(1)
    bstart = jnp.minimum(jnp.arange(nb, dtype=I32), n_valid[0] - 1) * MOE_BLOCK
    blk_expert = jnp.minimum(jnp.sum((bstart[:, None] >= padded_end[None, :]).astype(I32), axis=1),
                             N_EXPERTS - 1).astype(I32)
    tabs = (n_te.reshape(-1).astype(I32), off.reshape(-1).astype(I32), dst.reshape(-1).astype(I32),
            jnp.sum(n_te, axis=1).astype(I32))
    tail_n = jnp.concatenate([(padded - totals).astype(I32), n_valid])
    x_rows = moe_dispatch(*tabs, tail_n, (pstart + totals).astype(I32), posT, h2, n_rows)
    owner = jnp.where(padded > 0, jnp.arange(N_EXPERTS, dtype=I32), N_EXPERTS)
    later = jnp.concatenate([lax.cummin(owner, axis=0, reverse=True)[1:], jnp.full((1,), N_EXPERTS, I32)])
    next_expert = jnp.where(later < N_EXPERTS, later, -1).astype(I32)
    slot = ((jnp.cumsum((padded > 0).astype(I32)) - 1) % 2).astype(I32)
    y_rows = moe_experts(layer, blk_expert, n_valid, next_expert, slot, x_rows, w_gate, b_gate, w_up, b_up,
                         w_down, b_down)
    return moe_combine(lay, *tabs, pos, gates, x, gate2, y_rows, split)


QKV_TN = 256
N_QK_TILES = (ATT_HEADS + ATT_KV) * ATT_HD // QKV_TN


def _qkv_kernel(x_ref, g_ref, sc_ref, sh_ref, w_ref, nw_ref, cos_ref, sin_ref, o_ref):
    tm = x_ref.shape[0]
    h = _modnorm(x_ref[...], g_ref[...], sc_ref[...], sh_ref[...]).astype(BF16)
    r = lax.broadcasted_iota(I32, (QKV_TN, QKV_TN), 0) // ATT_HD
    c = lax.broadcasted_iota(I32, (QKV_TN, QKV_TN), 1) // ATT_HD
    head_mean = jnp.where(r == c, 1.0 / ATT_HD, 0.0).astype(BF16)
    lane = lax.broadcasted_iota(I32, (tm, QKV_TN), 1)
    half = ATT_HD // 4
    first = (lane % (2 * half)) < half
    for j in range(w_ref.shape[1] // QKV_TN):
        cols = slice(QKV_TN * j, QKV_TN * (j + 1))
        acc = _dot(h, w_ref[:, cols])
        if j >= N_QK_TILES:
            o_ref[:, cols] = acc
            continue
        ms = _dot((acc * acc).astype(BF16), head_mean)
        qn = acc * lax.rsqrt(ms + EPS) * nw_ref[j]
        swapped = jnp.where(first, pltpu.roll(qn, QKV_TN - half, 1), pltpu.roll(qn, half, 1))
        o_ref[:, cols] = qn * cos_ref[...] + swapped * sin_ref[...]


def _rope_tables(sample_len):
    pos = np.arange(sample_len)
    half = ATT_HD // 4
    inv = (ROPE_THETA ** (-np.arange(half, dtype=np.float32) / half)).astype(np.float32)
    ang_r = (pos // GRID_W).astype(np.float32)[:, None] * inv[None, :]
    ang_c = (pos % GRID_W).astype(np.float32)[:, None] * inv[None, :]
    cos = np.concatenate([np.cos(ang_r)] * 2 + [np.cos(ang_c)] * 2, axis=1)
    sin = np.concatenate([-np.sin(ang_r), np.sin(ang_r), -np.sin(ang_c), np.sin(ang_c)], axis=1)
    rep = QKV_TN // ATT_HD
    return (jnp.asarray(np.tile(cos, (1, rep)), F32), jnp.asarray(np.tile(sin, (1, rep)), F32))


def qkv_proj(lay, x, g, sc, sh, w, q_norm, k_norm, tm=512):
    n_tok = x.shape[0]
    n = w.shape[1]
    nq = ATT_HEADS * ATT_HD // QKV_TN
    rep = QKV_TN // ATT_HD
    nw = jnp.concatenate([jnp.tile(jnp.tile(q_norm, rep)[None, :], (nq, 1)),
                          jnp.tile(jnp.tile(k_norm, rep)[None, :], (n // QKV_TN - nq, 1))], axis=0)
    cos, sin = _rope_tables(lay.sample_len)
    cos = jnp.concatenate([jnp.ones((tm, QKV_TN), F32), cos], axis=0)
    sin = jnp.concatenate([jnp.zeros((tm, QKV_TN), F32), sin], axis=0)
    assert lay.p_tok % tm == 0 and lay.sample_len % tm == 0
    mrow = lambda i: (lay.mod_row(i * tm), 0, 0)
    rrow = lambda i: (jnp.where(i * tm < lay.p_tok, 0, 1 + ((i * tm - lay.p_tok) % lay.sample_len) // tm), 0)
    nt = n // QKV_TN
    return pl.pallas_call(
        _qkv_kernel, grid=(n_tok // tm,),
        in_specs=[pl.BlockSpec((tm, D), lambda i: (i, 0)),
                  pl.BlockSpec((1, D), lambda i: (0, 0)),
                  pl.BlockSpec((None, 1, D), mrow), pl.BlockSpec((None, 1, D), mrow),
                  pl.BlockSpec((D, n), lambda i: (0, 0)),
                  pl.BlockSpec((nt, 1, QKV_TN), lambda i: (0, 0, 0)),
                  pl.BlockSpec((tm, QKV_TN), rrow), pl.BlockSpec((tm, QKV_TN), rrow)],
        out_specs=pl.BlockSpec((tm, n), lambda i: (i, 0)),
        out_shape=jax.ShapeDtypeStruct((n_tok, n), F32),
        compiler_params=_cp(("arbitrary",)), name="qkv_proj",
    )(x, g.reshape(1, D), sc, sh, w, nw.reshape(nt, 1, QKV_TN), cos, sin)


def _dup_group(x, g):
    blk = x[:, LANES * (g // 2):LANES * (g // 2 + 1)]
    if g % 2 == 1:
        blk = pltpu.roll(blk, ATT_HD, 1)
    lo = lax.broadcasted_iota(I32, blk.shape, 1) < ATT_HD
    low = jnp.where(lo, blk, 0.0)
    return low + pltpu.roll(low, ATT_HD, 1)


def _attend(q_ref, k_all, v_all, mask, sink_ref, o_ref):
    nq = q_ref.shape[0]
    lo = lax.broadcasted_iota(I32, (nq, LANES), 1) < ATT_HD
    first = lax.broadcasted_iota(I32, (2 * nq, 1), 0) < nq
    if mask is not None:
        mask = jnp.concatenate([mask, mask], axis=0)
    grp = ATT_HEADS // ATT_KV
    for g in range(ATT_KV):
        k2 = _dup_group(k_all, g).astype(BF16)
        v2 = _dup_group(v_all, g).astype(BF16)
        for jp in range(grp // 2):
            j = g * (grp // 2) + jp
            qp = q_ref[:, LANES * j:LANES * (j + 1)] * (ATT_HD ** -0.5)
            qs = jnp.concatenate([jnp.where(lo, qp, 0.0), jnp.where(lo, 0.0, qp)], axis=0)
            s = _dot_nt(qs.astype(BF16), k2)
            if mask is not None:
                s = jnp.where(mask, s, NEG)
            sink = jnp.where(first, sink_ref[2 * j], sink_ref[2 * j + 1])
            m = jnp.maximum(jnp.max(s, axis=1, keepdims=True), sink)
            p = jnp.exp(s - m)
            den = jnp.sum(p, axis=1, keepdims=True) + jnp.exp(sink - m)
            o = _dot(p.astype(BF16), v2) / den
            o_ref[:, LANES * j:LANES * (j + 1)] = jnp.where(lo, o[:nq], o[nq:]).astype(o_ref.dtype)


def _attn_ctx_kernel(sink_ref, q_ref, k_ref, v_ref, o_ref):
    _attend(q_ref, k_ref[...], v_ref[...], None, sink_ref, o_ref)


def attn_context(lay, qkv, sinks):
    qw = ATT_HEADS * ATT_HD
    kw = ATT_KV * ATT_HD
    ln = lay.prompt_len
    grid_spec = pltpu.PrefetchScalarGridSpec(
        num_scalar_prefetch=0, grid=(lay.n_prompt,),
        in_specs=[pl.BlockSpec(memory_space=pltpu.SMEM),
                  pl.BlockSpec((ln, qw), lambda b: (b, 0)),
                  pl.BlockSpec((ln, kw), lambda b: (b, qw // kw)),
                  pl.BlockSpec((ln, kw), lambda b: (b, qw // kw + 1))],
        out_specs=pl.BlockSpec((ln, qw), lambda b: (b, 0)))
    return pl.pallas_call(
        _attn_ctx_kernel, grid_spec=grid_spec,
        out_shape=jax.ShapeDtypeStruct((lay.p_tok, qw), BF16),
        compiler_params=_cp(("arbitrary",)), name="attn_context",
    )(sinks, qkv, qkv, qkv)


def _attn_lat_kernel(sink_ref, q_ref, kp_ref, kc_ref, kn_ref, vp_ref, vc_ref, vn_ref, ck_ref, cv_ref, o_ref,
                     *, nblk):
    i = pl.program_id(1)
    bq = ATT_BLOCK
    nctx = ck_ref.shape[1]
    k_all = jnp.concatenate([kp_ref[...], kc_ref[...], kn_ref[...], ck_ref[0]], axis=0)
    v_all = jnp.concatenate([vp_ref[...], vc_ref[...], vn_ref[...], cv_ref[0]], axis=0)
    ns = 3 * bq + nctx
    r = lax.broadcasted_iota(I32, (bq, ns), 0)
    c = lax.broadcasted_iota(I32, (bq, ns), 1)
    rel = c - r
    first_key = jnp.where(i > 0, 0, bq)
    end_key = jnp.where(i < nblk - 1, 3 * bq, 2 * bq)
    band = (rel >= bq - WINDOW) & (rel <= bq + WINDOW) & (c >= first_key) & (c < end_key)
    mask = band | (c >= 3 * bq)
    _attend(q_ref, k_all, v_all, mask, sink_ref, o_ref)


def attn_latent(lay, qkv, cache_k, cache_v, sinks):
    qw = ATT_HEADS * ATT_HD
    kw = ATT_KV * ATT_HD
    bq = ATT_BLOCK
    nblk = lay.sample_len // bq
    b0 = lay.p_tok // bq
    nctx = cache_k.shape[1]
    rb = lambda b, i: b0 + b * nblk + i
    kspec = lambda cb, sh: pl.BlockSpec(
        (bq, kw), lambda b, i: (b0 + b * nblk + jnp.clip(i + sh, 0, nblk - 1), cb))
    kc, vc = qw // kw, qw // kw + 1
    grid_spec = pltpu.PrefetchScalarGridSpec(
        num_scalar_prefetch=0, grid=(lay.n_sample, nblk),
        in_specs=[pl.BlockSpec(memory_space=pltpu.SMEM),
                  pl.BlockSpec((bq, qw), lambda b, i: (rb(b, i), 0)),
                  kspec(kc, -1), kspec(kc, 0), kspec(kc, 1),
                  kspec(vc, -1), kspec(vc, 0), kspec(vc, 1),
                  pl.BlockSpec((1, nctx, kw), lambda b, i: (b, 0, 0)),
                  pl.BlockSpec((1, nctx, kw), lambda b, i: (b, 0, 0))],
        out_specs=pl.BlockSpec((bq, qw), lambda b, i: (b * nblk + i, 0)))
    return pl.pallas_call(
        functools.partial(_attn_lat_kernel, nblk=nblk), grid_spec=grid_spec,
        out_shape=jax.ShapeDtypeStruct((lay.n_sample * lay.sample_len, qw), BF16),
        compiler_params=_cp(("arbitrary", "arbitrary")), name="attn_latent",
    )(sinks, qkv, qkv, qkv, qkv, qkv, qkv, qkv,
      cache_k.reshape(lay.n_sample, nctx, kw), cache_v.reshape(lay.n_sample, nctx, kw))


def _forward(lay, x_prompt, x_sample, state_l0_ssd_fwd, state_l0_ssd_bwd, state_l0_gla_fwd, state_l0_gla_bwd,
             cache_l1_k, cache_l1_v, c, c_ctx, ada_w, ada_b, norm1, norm2,
             l0_w_in, l0_conv_w, l0_conv_b, l0_a_log, l0_dt_bias, l0_d_skip, l0_ssd_norm,
             l0_gate_w2, l0_gate_b, l0_gla_norm, l0_w_out,
             l1_w_qkv, l1_q_norm, l1_k_norm, l1_sinks, l1_w_out,
             router_w, router_b, exp_w_gate, exp_b_gate, exp_w_up, exp_b_up, exp_w_down, exp_b_down):
    np_, ns = lay.n_prompt, lay.n_sample
    x = (x_prompt.reshape(-1, D), x_sample.reshape(-1, D))
    cond8 = jnp.zeros((8, D), F32).at[0].set(c_ctx).at[1:1 + ns].set(c)
    mod = ada_table(cond8, ada_w, ada_b)
    mods = [[mod[l, :, p * D:(p + 1) * D].reshape(8, 1, D) for p in range(N_ADA)] for l in range(2)]

    def moe(l, xx, split=False):
        return moe_layer(lay, l, xx, norm2[l], mods[l][4], mods[l][3], mods[l][5], router_w[l], router_b[l],
                         exp_w_gate, exp_b_gate, exp_w_up, exp_b_up, exp_w_down, exp_b_down, split=split)

    sp = np.cumsum((SSD_INNER, SSD_INNER + 2 * SSD_GROUPS * SSD_STATE, 2 * SSD_HEADS,
                    GLA_HEADS * GLA_KEY_DIM, GLA_HEADS * GLA_KEY_DIM,
                    GLA_HEADS * GLA_VAL_DIM, GLA_HEADS * GLA_VAL_DIM, 2 * GLA_RANK))
    cols = lambda a, b: l0_w_in[:, a:b]
    w_main = jnp.concatenate([cols(0, sp[0]), cols(sp[4], sp[5]), cols(sp[5], sp[6]), cols(sp[0], sp[1]),
                              cols(sp[2], sp[3]), cols(sp[3], sp[4])], axis=1).astype(BF16)
    w_small = jnp.concatenate([cols(sp[1], sp[2]), cols(sp[6], sp[7]),
                               jnp.zeros((D, LANES - 2 * SSD_HEADS - 2 * GLA_RANK), F32)], axis=1)
    proj, small, small_t = norm_proj(lay, x, norm1[0], mods[0][1], mods[0][0], w_main, _hilo(w_small), 512,
                                      PJ_W // 2, BF16, SSD_CHUNK)
    y_n, o_n, (ssd_f, ssd_b, gla_f, gla_b) = l0_mixers(
        lay, proj, small, small_t, state_l0_ssd_fwd, state_l0_ssd_bwd, state_l0_gla_fwd, state_l0_gla_bwd,
        l0_conv_w, l0_conv_b, l0_a_log, l0_dt_bias, l0_d_skip, l0_ssd_norm,
        l0_gate_w2, l0_gate_b, l0_gla_norm)
    x = proj_residual(lay, [y_n, o_n], l0_w_out.astype(BF16), x, mods[0][2])
    x = moe(0, x)

    qkv = qkv_proj(lay, x, norm1[1], mods[1][1], mods[1][0], l1_w_qkv.astype(BF16), l1_q_norm, l1_k_norm)
    o_ctx = attn_context(lay, qkv, l1_sinks)
    o_lat = attn_latent(lay, qkv, cache_l1_k, cache_l1_v, l1_sinks)
    x = proj_residual(lay, [(o_ctx, o_lat)], l1_w_out.astype(BF16), x, mods[1][2])
    xp, xs = moe(1, x, split=True)

    qw = ATT_HEADS * ATT_HD
    kw = ATT_KV * ATT_HD
    return (xp.reshape(x_prompt.shape), xs.reshape(x_sample.shape),
            ssd_f[:np_].reshape(np_, SSD_HEADS, SSD_HEAD_DIM, SSD_STATE),
            ssd_b[:np_].reshape(np_, SSD_HEADS, SSD_HEAD_DIM, SSD_STATE),
            gla_f[:np_], gla_b[:np_],
            qkv[:lay.p_tok, qw:qw + kw].reshape(np_, lay.prompt_len, ATT_KV, ATT_HD),
            qkv[:lay.p_tok, qw + kw:].reshape(np_, lay.prompt_len, ATT_KV, ATT_HD))


def kernel(x_prompt, x_sample, state_l0_ssd_fwd, state_l0_ssd_bwd, state_l0_gla_fwd, state_l0_gla_bwd, cache_l1_k, cache_l1_v, c, c_ctx, ada_w, ada_b, norm1, norm2, l0_w_in, l0_conv_w, l0_conv_b, l0_a_log, l0_dt_bias, l0_d_skip, l0_ssd_norm, l0_gate_w2, l0_gate_b, l0_gla_norm, l0_w_out, l1_w_qkv, l1_q_norm, l1_k_norm, l1_sinks, l1_w_out, router_w, router_b, exp_w_gate, exp_b_gate, exp_w_up, exp_b_up, exp_w_down, exp_b_down):
    lay = Layout(x_prompt.shape[0], x_prompt.shape[1], x_sample.shape[0], x_sample.shape[1])
    return _forward(lay, x_prompt, x_sample, state_l0_ssd_fwd, state_l0_ssd_bwd, state_l0_gla_fwd,
                    state_l0_gla_bwd, cache_l1_k, cache_l1_v, c, c_ctx, ada_w, ada_b, norm1, norm2,
                    l0_w_in, l0_conv_w, l0_conv_b, l0_a_log, l0_dt_bias, l0_d_skip, l0_ssd_norm,
                    l0_gate_w2, l0_gate_b, l0_gla_norm, l0_w_out,
                    l1_w_qkv, l1_q_norm, l1_k_norm, l1_sinks, l1_w_out,
                    router_w, router_b, exp_w_gate, exp_b_gate, exp_w_up, exp_b_up, exp_w_down, exp_b_down)
```

```python
import functools
import math

import numpy as np
import jax
import jax.numpy as jnp
from jax import lax
from jax.experimental import pallas as pl
from jax.experimental.pallas import tpu as pltpu

F32 = jnp.float32
BF16 = jnp.bfloat16
I32 = jnp.int32
HI = lax.Precision.HIGHEST

D = 1024
EPS = 1e-6
N_ADA = 6
SSD_HEADS = 16
SSD_HEAD_DIM = 64
SSD_INNER = 1024
SSD_STATE = 128
SSD_GROUPS = 2
SSD_CONV = 5
SSD_CHUNK = 128
GLA_HEADS = 4
GLA_KEY_DIM = 128
GLA_VAL_DIM = 256
GLA_RANK = 16
GLA_TAU = 16.0
GLA_BLOCK = 64
ATT_HEADS = 16
ATT_KV = 4
ATT_HD = 64
ATT_BLOCK = 128
WINDOW = 128
GRID_W = 64
ROPE_THETA = 10000.0
N_EXPERTS = 32
TOP_K = 4
EXPERT_FF = 1024
SWIGLU_LIMIT = 7.0
SWIGLU_ALPHA = 1.702
MOE_BLOCK = 256
TOK_TILE = 256
LANES = 128
NEG = -1e30

PJ_Z, PJ_V, PJ_OG, PJ_XBC, PJ_Q, PJ_K = 0, 1024, 2048, 3072, 4608, 5120
PJ_W = 5632
VMEM_LIMIT = 48 * 1024 * 1024


def _cp(sem, vmem=VMEM_LIMIT):
    return pltpu.CompilerParams(dimension_semantics=sem, vmem_limit_bytes=vmem)


class Layout:
    def __init__(self, n_prompt, prompt_len, n_sample, sample_len):
        self.n_prompt, self.prompt_len = n_prompt, prompt_len
        self.n_sample, self.sample_len = n_sample, sample_len
        self.p_tok = n_prompt * prompt_len
        self.n_tok = self.p_tok + n_sample * sample_len
        self.seqs = [(i * prompt_len, prompt_len) for i in range(n_prompt)]
        self.seqs += [(self.p_tok + i * sample_len, sample_len) for i in range(n_sample)]
        self.n_seq = len(self.seqs)

    def mod_row(self, start):
        return jnp.where(start < self.p_tok, 0, 1 + (start - self.p_tok) // self.sample_len)

def _sigmoid(x):
    return 1.0 / (1.0 + jnp.exp(-x))


def _silu(x):
    return x * _sigmoid(x)


def _softplus(x):
    return jnp.maximum(x, 0.0) + jnp.log(1.0 + jnp.exp(-jnp.abs(x)))


def _modnorm(x, g, sc, sh):
    ms = jnp.mean(x * x, axis=-1, keepdims=True)
    return (x * lax.rsqrt(ms + EPS) * g) * (1.0 + sc) + sh


def _dot(a, b, **kw):
    return jnp.dot(a, b, preferred_element_type=F32, **kw)


def _dot_nt(a, b):
    return lax.dot_general(a, b, (((1,), (1,)), ((), ())), preferred_element_type=F32)


def _dot_tn(a, b):
    return lax.dot_general(a, b, (((0,), (0,)), ((), ())), preferred_element_type=F32)


def _split(x, n):
    parts = []
    for _ in range(n):
        p = x.astype(BF16)
        parts.append(p)
        x = x - p.astype(F32)
    return parts


def _dot_sel(sel, x):
    sel = sel.astype(BF16)
    return sum(_dot(sel, p) for p in _split(x, 3))


def _dot_sel_r(x, sel):
    sel = sel.astype(BF16)
    return sum(_dot(p, sel) for p in _split(x, 3))


def _dot_hilo(x, w_hi, w_lo):
    x_hi, x_lo = _split(x, 2)
    return _dot(x_hi, w_hi) + _dot(x_lo, w_hi) + _dot(x_hi, w_lo)


def _hilo(w):
    hi = w.astype(BF16)
    return jnp.stack([hi, (w - hi.astype(F32)).astype(BF16)])


U32 = jnp.uint32
ROW_WORDS = D // 2
_HI_MASK = 0xFFFF0000


def _pack_rows(x):
    lo = lax.bitcast_convert_type(x[:, :ROW_WORDS], U32) >> 16
    hi = lax.bitcast_convert_type(x[:, ROW_WORDS:], U32) & jnp.uint32(_HI_MASK)
    return lo | hi


def _unpack_rows(u):
    lo = lax.bitcast_convert_type(u << 16, F32)
    hi = lax.bitcast_convert_type(u & jnp.uint32(_HI_MASK), F32)
    return jnp.concatenate([lo, hi], axis=1)


def _ada_kernel(c_ref, w_ref, b_ref, o_ref):
    o_ref[0] = _dot(_silu(c_ref[...]), w_ref[0], precision=HI) + b_ref[0]


def ada_table(cond8, ada_w, ada_b):
    depth, _, n = ada_w.shape
    tn = 1536
    return pl.pallas_call(
        _ada_kernel, grid=(depth, n // tn),
        in_specs=[pl.BlockSpec((8, D), lambda l, j: (0, 0)),
                  pl.BlockSpec((1, D, tn), lambda l, j: (l, 0, j)),
                  pl.BlockSpec((1, 1, tn), lambda l, j: (l, 0, j))],
        out_specs=pl.BlockSpec((1, 8, tn), lambda l, j: (l, 0, j)),
        out_shape=jax.ShapeDtypeStruct((depth, 8, n), F32),
        compiler_params=_cp(("arbitrary", "arbitrary")), name="ada_table",
    )(cond8, ada_w, ada_b.reshape(depth, 1, n))


def _stream_specs(lay, stream, tm):
    if not isinstance(stream, (tuple, list)):
        return [stream], [pl.BlockSpec((tm, stream.shape[1]), lambda i, *_: (i, 0))]
    assert lay.p_tok % tm == 0
    npt = lay.p_tok // tm
    w = stream[0].shape[1]
    return list(stream), [pl.BlockSpec((tm, w), lambda i, *_: (jnp.minimum(i, npt - 1), 0)),
                          pl.BlockSpec((tm, w), lambda i, *_: (jnp.maximum(i - npt, 0), 0))]


def _stream_tile(refs, in_prompt):
    if len(refs) == 1:
        return refs[0][...]
    return jnp.where(in_prompt, refs[0][...], refs[1][...])


def _proj_kernel(*refs, nx, npt):
    x_refs = refs[:nx]
    g_ref, sc_ref, sh_ref, w_ref, ws_ref, o_ref, os_ref, ost_ref, h_scr = refs[nx:]

    @pl.when(pl.program_id(1) == 0)
    def _():
        x = _stream_tile(x_refs, pl.program_id(0) < npt)
        h = _modnorm(x, g_ref[...], sc_ref[...], sh_ref[...])
        h_scr[...] = h.astype(BF16)
        small = _dot_hilo(h, ws_ref[0], ws_ref[1])
        os_ref[...] = small
        q = ost_ref.shape[2]
        for c in range(ost_ref.shape[0]):
            ost_ref[c] = small[q * c:q * (c + 1), :].T

    o_ref[...] = _dot(h_scr[...], w_ref[...]).astype(o_ref.dtype)


def norm_proj(lay, x, g, sc, sh, w, w_small, tm, tn, out_dtype, chunk):
    n_tok = lay.n_tok
    n = w.shape[1]
    ns = w_small.shape[-1]
    mrow = lambda i, j: (lay.mod_row(i * tm), 0, 0)
    xs, x_specs = _stream_specs(lay, x, tm)
    return pl.pallas_call(
        functools.partial(_proj_kernel, nx=len(xs), npt=lay.p_tok // tm), grid=(n_tok // tm, n // tn),
        in_specs=x_specs + [pl.BlockSpec((1, D), lambda i, j: (0, 0)),
                            pl.BlockSpec((None, 1, D), mrow),
                            pl.BlockSpec((None, 1, D), mrow),
                            pl.BlockSpec((D, tn), lambda i, j: (0, j)),
                            pl.BlockSpec((2, D, ns), lambda i, j: (0, 0, 0))],
        out_specs=[pl.BlockSpec((tm, tn), lambda i, j: (i, j)),
                   pl.BlockSpec((tm, ns), lambda i, j: (i, 0)),
                   pl.BlockSpec((tm // chunk, ns, chunk), lambda i, j: (i, 0, 0))],
        out_shape=[jax.ShapeDtypeStruct((n_tok, n), out_dtype),
                   jax.ShapeDtypeStruct((n_tok, ns), F32),
                   jax.ShapeDtypeStruct((n_tok // chunk, ns, chunk), F32)],
        scratch_shapes=[pltpu.VMEM((tm, D), BF16)],
        compiler_params=_cp(("arbitrary", "arbitrary")), name="norm_proj",
    )(*xs, g.reshape(1, D), sc, sh, w, w_small)


CONV_HALO = 16


def _ssd_load(d, c, xc, dtg_ref, dtgT_ref, S):
    q = SSD_CHUNK
    nh = SSD_HEADS
    rows = pl.ds(pl.multiple_of(c * q, q), q)
    return (rows, xc[rows, 0:SSD_INNER], xc[rows, SSD_INNER:SSD_INNER + 2 * SSD_GROUPS * SSD_STATE],
            dtg_ref[rows, nh * d:nh * d + nh], dtgT_ref[c, nh * d:nh * d + nh, :], S[...])


def _ssd_chunk(d, loaded, alog_ref, alogT_ref, dtb_ref, dtbT_ref):
    q = SSD_CHUNK
    nh = SSD_HEADS
    _, xs, bc, dtg, dtgT, s = loaded
    xs = xs.astype(F32)
    dt = _softplus(dtg + dtb_ref[d:d + 1, :])
    dtT = _softplus(dtgT + dtbT_ref[:, d:d + 1])
    ad = dt * (-jnp.exp(alog_ref[d:d + 1, :]))
    adT = dtT * (-jnp.exp(alogT_ref[:, d:d + 1]))
    row = lax.broadcasted_iota(I32, (q, q), 0)
    col = lax.broadcasted_iota(I32, (q, q), 1)
    if d == 0:
        e = _dot_sel(col <= row, ad)
        eT = _dot_sel_r(adT, row <= col)
        tot = e[q - 1:q, :]
        mask = row >= col
        fq = jnp.exp(e)
        fk = jnp.exp(tot - e)
    else:
        e = _dot_sel(col < row, ad)
        eT = _dot_sel_r(adT, row < col)
        tot = jnp.sum(ad, axis=0, keepdims=True)
        mask = col >= row
        fq = jnp.exp(tot - e)
        fk = jnp.exp(e)
    dec = jnp.exp(tot)
    lo = lax.broadcasted_iota(I32, (q, LANES), 1) < SSD_HEAD_DIM
    hp = nh * SSD_HEAD_DIM
    head_of = lax.broadcasted_iota(I32, (nh, hp), 1) // SSD_HEAD_DIM
    spread = (lax.broadcasted_iota(I32, (nh, hp), 0) == head_of).astype(BF16)

    def per_lane(arr, passes):
        return sum(_dot(p, spread) for p in _split(arr, passes))

    dt_x, fq_x, fk_x = per_lane(dt, 1), per_lane(fq, 1), per_lane(fk, 1)
    dec_x = per_lane(jnp.broadcast_to(dec, (8, nh)), 3)[0:1, :]
    xdt_all = xs * dt_x
    xk_all = (xdt_all * fk_x).astype(BF16)
    rep = (nh // SSD_GROUPS) // 2
    ys, s_new = [], []
    for g in range(SSD_GROUPS):
        bg_t = bc[:, SSD_STATE * g:SSD_STATE * (g + 1)].T
        cg = bc[:, SSD_STATE * (SSD_GROUPS + g):SSD_STATE * (SSD_GROUPS + g + 1)]
        gmat = _dot(cg, bg_t)
        for j in range(rep * g, rep * (g + 1)):
            a = 2 * j
            sl = slice(LANES * j, LANES * (j + 1))
            parts = []
            for hh in (a, a + 1):
                if d == 0:
                    diff = e[:, hh:hh + 1] - eT[hh:hh + 1, :]
                else:
                    diff = eT[hh:hh + 1, :] - e[:, hh:hh + 1]
                parts.append((gmat * jnp.exp(jnp.where(mask, diff, NEG))).astype(BF16))
            lhs = jnp.concatenate(parts, axis=1)
            xdt = xdt_all[:, sl]
            rhs = jnp.concatenate([jnp.where(lo, xdt, 0.0), jnp.where(lo, 0.0, xdt)], axis=0)
            sj = s[:, sl]
            ys.append(_dot(lhs, rhs.astype(BF16)) + _dot(cg, sj.astype(BF16)) * fq_x[:, sl])
            s_new.append(sj * dec_x[:, sl] + _dot(bg_t, xk_all[:, sl]))
    return jnp.concatenate(ys, axis=1), jnp.concatenate(s_new, axis=1)


def _ssd_seq_kernel(*refs, has_init):
    (xbc_ref, z_ref, dtg_ref, dtgT_ref, cw_ref, cb_ref,
     alog_ref, alogT_ref, dtb_ref, dtbT_ref, dskip_ref, nrm_ref) = refs[:12]
    refs = refs[12:]
    if has_init:
        s0f_ref, s0b_ref = refs[:2]
        refs = refs[2:]
    y_ref, sf_ref, sb_ref, xc, ext, Sf, Sb, yf, yb = refs
    ln = xbc_ref.shape[0]
    q = SSD_CHUNK
    nc = ln // q
    h = CONV_HALO
    pad = SSD_CONV // 2

    def conv_body(c, carry):
        r0 = pl.multiple_of(c * q, q)
        prev = xbc_ref[pl.ds(pl.multiple_of(jnp.maximum(r0 - h, 0), h), h), :].astype(F32)
        nxt = xbc_ref[pl.ds(pl.multiple_of(jnp.minimum(r0 + q, ln - h), h), h), :].astype(F32)
        ext[0:h, :] = jnp.where(c > 0, prev, 0.0)
        ext[h:h + q, :] = xbc_ref[pl.ds(r0, q), :].astype(F32)
        ext[h + q:h + q + h, :] = jnp.where(c < nc - 1, nxt, 0.0)
        acc = jnp.broadcast_to(cb_ref[...], (q, cb_ref.shape[1]))
        for k in range(SSD_CONV):
            acc = acc + cw_ref[k:k + 1, :] * ext[h - pad + k:h - pad + k + q, :]
        xc[pl.ds(r0, q), :] = _silu(acc).astype(xc.dtype)
        return carry
    lax.fori_loop(0, nc, conv_body, 0)

    if has_init:
        Sf[...] = s0f_ref[0].T
        Sb[...] = s0b_ref[0].T
    else:
        Sf[...] = jnp.zeros_like(Sf)
        Sb[...] = jnp.zeros_like(Sb)
    params = (alog_ref, alogT_ref, dtb_ref, dtbT_ref)

    def scan_body(c, carry):
        lf = _ssd_load(0, c, xc, dtg_ref, dtgT_ref, Sf)
        lb = _ssd_load(1, nc - 1 - c, xc, dtg_ref, dtgT_ref, Sb)
        y_f, s_f = _ssd_chunk(0, lf, *params)
        y_b, s_b = _ssd_chunk(1, lb, *params)
        yf[lf[0], :] = y_f
        yb[lb[0], :] = y_b
        Sf[...] = s_f
        Sb[...] = s_b
        return carry
    lax.fori_loop(0, nc, scan_body, 0, unroll=2)
    sf_ref[0] = Sf[...].T
    sb_ref[0] = Sb[...].T

    def out_body(c, carry):
        rows = pl.ds(pl.multiple_of(c * q, q), q)
        ytot = yf[rows, :] + yb[rows, :] + dskip_ref[...] * xc[rows, 0:SSD_INNER].astype(F32)
        yg = ytot * _silu(z_ref[rows, :].astype(F32))
        ms = jnp.mean(yg * yg, axis=-1, keepdims=True)
        y_ref[rows, :] = (yg * lax.rsqrt(ms + EPS) * nrm_ref[...]).astype(y_ref.dtype)
        return carry
    lax.fori_loop(0, nc, out_body, 0)


def _ssd_call(n_seq, ln, blk0, proj, small, smallT3, init, params):
    q = SSD_CHUNK
    hp = SSD_HEADS * SSD_HEAD_DIM
    cw = SSD_INNER + 2 * SSD_GROUPS * SSD_STATE
    nc = ln // q
    assert PJ_XBC % cw == 0 and PJ_Z % SSD_INNER == 0
    tok = lambda w, cb: pl.BlockSpec((ln, w), lambda b: (blk0 + b, cb))
    seq3 = pl.BlockSpec((1, hp, SSD_STATE), lambda b: (b, 0, 0))
    full = lambda a: pl.BlockSpec(a.shape, lambda b: (0,) * a.ndim)
    init = () if init is None else tuple(init)
    return pl.pallas_call(
        functools.partial(_ssd_seq_kernel, has_init=bool(init)), grid=(n_seq,),
        in_specs=[tok(cw, PJ_XBC // cw), tok(SSD_INNER, PJ_Z // SSD_INNER), tok(LANES, 0),
                  pl.BlockSpec((nc, 2 * SSD_HEADS, q), lambda b: (blk0 + b, 0, 0))]
        + [full(a) for a in params] + [seq3] * len(init),
        out_specs=[pl.BlockSpec((ln, hp), lambda b: (b, 0)), seq3, seq3],
        out_shape=[jax.ShapeDtypeStruct((n_seq * ln, hp), BF16),
                   jax.ShapeDtypeStruct((n_seq, hp, SSD_STATE), F32),
                   jax.ShapeDtypeStruct((n_seq, hp, SSD_STATE), F32)],
        scratch_shapes=[pltpu.VMEM((ln, cw), BF16), pltpu.VMEM((q + 2 * CONV_HALO, cw), F32),
                        pltpu.VMEM((SSD_STATE, hp), F32), pltpu.VMEM((SSD_STATE, hp), F32),
                        pltpu.VMEM((ln, hp), F32), pltpu.VMEM((ln, hp), F32)],
        compiler_params=_cp(("arbitrary",)), name="ssd_seq",
    )(proj, proj, small, smallT3, *params, *init)


def _gla_gates(d, c, q_ref, k_ref, glr_ref, w2_ref, gb_ref, qi, ki, qo, kk, dec):
    t = GLA_BLOCK
    rows = pl.ds(pl.multiple_of(c * t, t), t)
    c0 = 2 * SSD_HEADS + GLA_RANK * d
    gp = _dot_hilo(glr_ref[rows, c0:c0 + GLA_RANK], w2_ref[0, d], w2_ref[1, d]) + gb_ref[d:d + 1, :]
    la = -_softplus(-gp) * (1.0 / GLA_TAU)
    row = lax.broadcasted_iota(I32, (t, t), 0)
    col = lax.broadcasted_iota(I32, (t, t), 1)
    mid = t // 2 - 1
    if d == 0:
        e = _dot_sel(col <= row, la)
        tot = e[t - 1:t, :]
        r = e[mid:mid + 1, :]
        fqi, fki = jnp.exp(e - r), jnp.exp(r - e)
        fq, fk = jnp.exp(e), jnp.exp(tot - e)
    else:
        e = _dot_sel(col < row, la)
        tot = e[t - 1:t, :] + la[t - 1:t, :]
        r = e[mid:mid + 1, :]
        fqi, fki = jnp.exp(r - e), jnp.exp(e - r)
        fq, fk = jnp.exp(tot - e), jnp.exp(e)
    qf = q_ref[rows, :].astype(F32) * (GLA_KEY_DIM ** -0.5)
    kf = k_ref[rows, :].astype(F32)
    qi[d, rows, :] = (qf * fqi).astype(BF16)
    ki[d, rows, :] = (kf * fki).astype(BF16)
    qo[d, rows, :] = (qf * fq).astype(BF16)
    kk[d, rows, :] = (kf * fk).astype(BF16)
    dec[d, c] = jnp.broadcast_to(jnp.exp(tot), (8, tot.shape[1]))


def _gla_load(d, c, v_ref, qi, ki, qo, kk, dec, S):
    t = GLA_BLOCK
    rows = pl.ds(pl.multiple_of(c * t, t), t)
    return (rows, qi[d, rows, :], ki[d, rows, :], qo[d, rows, :], kk[d, rows, :], dec[d, c][0:1, :],
            v_ref[rows, :], [S[h] for h in range(GLA_HEADS)])


def _gla_block(d, loaded):
    t = GLA_BLOCK
    dk, dv = GLA_KEY_DIM, GLA_VAL_DIM
    _, q_in, k_in, q_st, k_st, dec, v, states = loaded
    row = lax.broadcasted_iota(I32, (t, t), 0)
    col = lax.broadcasted_iota(I32, (t, t), 1)
    mask = row >= col if d == 0 else col >= row
    outs, new_states = [], []
    for h in range(GLA_HEADS):
        sl = slice(dk * h, dk * (h + 1))
        sc = jnp.where(mask, _dot_nt(q_in[:, sl], k_in[:, sl]), 0.0)
        vh = v[:, dv * h:dv * (h + 1)]
        st = states[h]
        outs.append(_dot(sc.astype(BF16), vh) + _dot_nt(q_st[:, sl], st.astype(BF16)))
        new_states.append(st * dec[:, sl] + _dot_tn(vh, k_st[:, sl]))
    return jnp.concatenate(outs, axis=1), new_states


def _gla_seq_kernel(*refs, has_init):
    q_ref, k_ref, v_ref, og_ref, glr_ref, w2_ref, gb_ref, nrm_ref = refs[:8]
    refs = refs[8:]
    if has_init:
        s0f_ref, s0b_ref = refs[:2]
        refs = refs[2:]
    o_ref, sf_ref, sb_ref, Sf, Sb, of, ob, qi, ki, qo, kk, dec = refs
    ln = q_ref.shape[0]
    t = GLA_BLOCK
    nc = ln // t
    dv = GLA_VAL_DIM
    for h in range(GLA_HEADS):
        if has_init:
            Sf[h] = s0f_ref[0, h].T
            Sb[h] = s0b_ref[0, h].T
        else:
            Sf[h] = jnp.zeros(Sf.shape[1:], F32)
            Sb[h] = jnp.zeros(Sb.shape[1:], F32)
    staged = (qi, ki, qo, kk, dec)

    def gate_body(c, carry):
        for d in (0, 1):
            _gla_gates(d, c, q_ref, k_ref, glr_ref, w2_ref, gb_ref, *staged)
        return carry
    lax.fori_loop(0, nc, gate_body, 0, unroll=2)

    def scan_body(c, carry):
        lf = _gla_load(0, c, v_ref, *staged, Sf)
        lb = _gla_load(1, nc - 1 - c, v_ref, *staged, Sb)
        o_f, s_f = _gla_block(0, lf)
        o_b, s_b = _gla_block(1, lb)
        of[lf[0], :] = o_f
        ob[lb[0], :] = o_b
        for h in range(GLA_HEADS):
            Sf[h] = s_f[h]
            Sb[h] = s_b[h]
        return carry
    lax.fori_loop(0, nc, scan_body, 0, unroll=2)
    for h in range(GLA_HEADS):
        sf_ref[0, h] = Sf[h].T
        sb_ref[0, h] = Sb[h].T

    def out_body(c, carry):
        rows = pl.ds(pl.multiple_of(c * t, t), t)
        for h in range(GLA_HEADS):
            vl = slice(dv * h, dv * (h + 1))
            ot = of[rows, vl] + ob[rows, vl]
            ms = jnp.mean(ot * ot, axis=-1, keepdims=True)
            on = ot * lax.rsqrt(ms + EPS) * nrm_ref[...]
            o_ref[rows, vl] = (on * _silu(og_ref[rows, vl].astype(F32))).astype(o_ref.dtype)
        return carry
    lax.fori_loop(0, nc, out_body, 0)


def _gla_call(n_seq, ln, blk0, proj, small, init, params):
    qk_w = GLA_HEADS * GLA_KEY_DIM
    v_w = GLA_HEADS * GLA_VAL_DIM
    tok = lambda w, cb: pl.BlockSpec((ln, w), lambda b: (blk0 + b, cb))
    seq4 = pl.BlockSpec((1, GLA_HEADS, GLA_KEY_DIM, GLA_VAL_DIM), lambda b: (b, 0, 0, 0))
    full = lambda a: pl.BlockSpec(a.shape, lambda b: (0,) * a.ndim)
    st_shape = jax.ShapeDtypeStruct((n_seq, GLA_HEADS, GLA_KEY_DIM, GLA_VAL_DIM), F32)
    init = () if init is None else tuple(init)
    return pl.pallas_call(
        functools.partial(_gla_seq_kernel, has_init=bool(init)), grid=(n_seq,),
        in_specs=[tok(qk_w, PJ_Q // qk_w), tok(qk_w, PJ_K // qk_w), tok(v_w, PJ_V // v_w),
                  tok(v_w, PJ_OG // v_w), tok(LANES, 0)] + [full(a) for a in params] + [seq4] * len(init),
        out_specs=[pl.BlockSpec((ln, v_w), lambda b: (b, 0)), seq4, seq4],
        out_shape=[jax.ShapeDtypeStruct((n_seq * ln, v_w), BF16), st_shape, st_shape],
        scratch_shapes=[pltpu.VMEM((GLA_HEADS, GLA_VAL_DIM, GLA_KEY_DIM), F32),
                        pltpu.VMEM((GLA_HEADS, GLA_VAL_DIM, GLA_KEY_DIM), F32),
                        pltpu.VMEM((ln, v_w), F32), pltpu.VMEM((ln, v_w), F32)]
        + [pltpu.VMEM((2, ln, qk_w), BF16)] * 4 + [pltpu.VMEM((2, ln // GLA_BLOCK, 8, qk_w), F32)],
        compiler_params=_cp(("arbitrary",)), name="gla_seq",
    )(proj, proj, proj, proj, small, *params, *init)


def l0_mixers(lay, proj, small, small_t, ssd_f0, ssd_b0, gla_f0, gla_b0, conv_w, conv_b, a_log, dt_bias, d_skip,
              ssd_norm, gate_w2, gate_b, gla_norm):
    hp = SSD_HEADS * SSD_HEAD_DIM
    ssd_p = (conv_w, conv_b.reshape(1, -1), a_log, a_log.T, dt_bias, dt_bias.T,
             jnp.repeat(d_skip, SSD_HEAD_DIM).reshape(1, hp), ssd_norm.reshape(1, hp))
    gla_p = (_hilo(gate_w2), gate_b, gla_norm.reshape(1, -1))
    np_, ns = lay.n_prompt, lay.n_sample
    assert lay.p_tok % lay.sample_len == 0
    groups = [(np_, lay.prompt_len, 0, None, None),
              (ns, lay.sample_len, lay.p_tok // lay.sample_len,
               (ssd_f0.reshape(ns, hp, SSD_STATE), ssd_b0.reshape(ns, hp, SSD_STATE)), (gla_f0, gla_b0))]
    ys, os_, states = [], [], None
    for n, ln, blk0, ssd_init, gla_init in groups:
        y, sf, sb = _ssd_call(n, ln, blk0, proj, small, small_t, ssd_init, ssd_p)
        o, gf, gb = _gla_call(n, ln, blk0, proj, small, gla_init, gla_p)
        ys.append(y)
        os_.append(o)
        if states is None:
            states = (sf, sb, gf, gb)
    return tuple(ys), tuple(os_), states


def _res_kernel(*refs, counts, ks, npt):
    in_prompt = pl.program_id(0) < npt
    streams, pos = [], 0
    for c in counts:
        streams.append(refs[pos:pos + c])
        pos += c
    w_ref, gate_ref, o_ref = refs[pos:]
    acc = None
    off = 0
    for a_refs, k in zip(streams[:-1], ks):
        part = _dot(_stream_tile(a_refs, in_prompt), w_ref[off:off + k, :])
        acc = part if acc is None else acc + part
        off += k
    o_ref[...] = _stream_tile(streams[-1], in_prompt) + gate_ref[...] * acc


def proj_residual(lay, acts, w, x, gate, tm=512):
    arrays, specs, counts = [], [], []
    for s in list(acts) + [x]:
        a, sp = _stream_specs(lay, s, tm)
        arrays += a
        specs += sp
        counts.append(len(a))
    ks = tuple(int((a[0] if isinstance(a, (tuple, list)) else a).shape[1]) for a in acts)
    mrow = lambda i: (lay.mod_row(i * tm), 0, 0)
    return pl.pallas_call(
        functools.partial(_res_kernel, counts=tuple(counts), ks=ks, npt=lay.p_tok // tm),
        grid=(lay.n_tok // tm,),
        in_specs=specs + [pl.BlockSpec(w.shape, lambda i: (0, 0)), pl.BlockSpec((None, 1, D), mrow)],
        out_specs=pl.BlockSpec((tm, D), lambda i: (i, 0)),
        out_shape=jax.ShapeDtypeStruct((lay.n_tok, D), F32),
        compiler_params=_cp(("arbitrary",)), name="proj_residual",
    )(*arrays, w, gate)


def _router_kernel(x_ref, g_ref, sc_ref, sh_ref, rw_ref, rb_ref,
                   h_ref, idx_ref, gate_ref, pos_ref, posT_ref, cnt_ref):
    tm = x_ref.shape[0]
    h = _modnorm(x_ref[...], g_ref[...], sc_ref[...], sh_ref[...])
    h_hi = h.astype(BF16)
    h_ref[...] = h_hi
    h_lo = (h - h_hi.astype(F32)).astype(BF16)
    lg = (_dot(h_hi, rw_ref[0]) + _dot(h_lo, rw_ref[0]) + _dot(h_hi, rw_ref[1])
          + rb_ref[...])
    lane = lax.broadcasted_iota(I32, (tm, LANES), 1).astype(F32)
    vals, ids = [], []
    for _ in range(TOP_K):
        m = jnp.max(lg, axis=1, keepdims=True)
        i = jnp.min(jnp.where(lg == m, lane, float(LANES)), axis=1, keepdims=True)
        vals.append(m)
        ids.append(i)
        lg = jnp.where(lane == i, -jnp.inf, lg)
    ex = [jnp.exp(v - vals[0]) for v in vals]
    den = ex[0] + ex[1] + ex[2] + ex[3]
    sel = jnp.zeros((tm, LANES), F32)
    for i in ids:
        sel = sel + (lane == i).astype(F32)
    row = lax.broadcasted_iota(I32, (tm, tm), 0)
    col = lax.broadcasted_iota(I32, (tm, tm), 1)
    before = _dot((col < row).astype(BF16), sel.astype(BF16))
    n = jnp.sum(sel, axis=0, keepdims=True)
    er = lax.broadcasted_iota(I32, (LANES, LANES), 0)
    ec = lax.broadcasted_iota(I32, (LANES, LANES), 1)
    n_al = jnp.ceil(n * (1.0 / SEG_ALIGN)) * SEG_ALIGN
    offs = _dot(jnp.broadcast_to(n_al, (8, LANES)).astype(BF16), (er < ec).astype(BF16))[0:1, :]
    slot = before + offs
    idx_o = jnp.zeros((tm, LANES), F32)
    gate_o = jnp.zeros((tm, LANES), F32)
    pos_o = jnp.zeros((tm, LANES), F32)
    for k in range(TOP_K):
        p = jnp.sum(jnp.where(lane == ids[k], slot, 0.0), axis=1, keepdims=True)
        idx_o = jnp.where(lane == k, ids[k], idx_o)
        gate_o = jnp.where(lane == k, ex[k] / den, gate_o)
        pos_o = jnp.where(lane == k, p, pos_o)
    idx_ref[...] = idx_o.astype(I32)
    gate_ref[...] = gate_o
    pos_ref[...] = pos_o.astype(I32)
    posT_ref[...] = pos_o.T[0:8, :]
    cnt_ref[0] = jnp.broadcast_to(n, (8, LANES))


def moe_router(lay, x, g, sc, sh, rw, rb):
    n_tok = x.shape[0]
    tm = TOK_TILE
    nt = n_tok // tm
    mrow = lambda i: (lay.mod_row(i * tm), 0, 0)
    tile = lambda w, dt: (pl.BlockSpec((tm, w), lambda i: (i, 0)), jax.ShapeDtypeStruct((n_tok, w), dt))
    outs = [tile(D, BF16), tile(LANES, I32), tile(LANES, F32), tile(LANES, I32),
            (pl.BlockSpec((8, tm), lambda i: (0, i)), jax.ShapeDtypeStruct((8, n_tok), F32)),
            (pl.BlockSpec((1, 8, LANES), lambda i: (i, 0, 0)), jax.ShapeDtypeStruct((nt, 8, LANES), F32))]
    return pl.pallas_call(
        _router_kernel, grid=(nt,),
        in_specs=[pl.BlockSpec((tm, D), lambda i: (i, 0)),
                  pl.BlockSpec((1, D), lambda i: (0, 0)),
                  pl.BlockSpec((None, 1, D), mrow), pl.BlockSpec((None, 1, D), mrow),
                  pl.BlockSpec((2, D, LANES), lambda i: (0, 0, 0)),
                  pl.BlockSpec((1, LANES), lambda i: (0, 0))],
        out_specs=[o[0] for o in outs], out_shape=[o[1] for o in outs],
        compiler_params=_cp(("arbitrary",)), name="moe_router",
    )(x, g.reshape(1, D), sc, sh, rw, rb)


SEG_ALIGN = 8
SEG_CHUNK = 16
REST_BITS = tuple(range(int(math.log2(SEG_CHUNK)) - 1, int(math.log2(SEG_ALIGN)) - 1, -1))
TILE_ROWS = TOK_TILE * TOP_K + N_EXPERTS * SEG_ALIGN


def _pow2_copies(n, src, dst, make_copy, op, bits):
    for b in bits:
        sz = 1 << b
        done = (n >> (b + 1)) << (b + 1)

        @pl.when((n & sz) != 0)
        def _():
            op(make_copy(pl.multiple_of(src + done, SEG_ALIGN), pl.multiple_of(dst + done, SEG_ALIGN), sz))


def _start_segments(i, n_ref, off_ref, dst_ref, make_copy):
    def body(e, carry):
        k = i * N_EXPERTS + e
        n, src, dst = n_ref[k], off_ref[k], dst_ref[k]

        def chunk(j, c):
            o = pl.multiple_of(j * SEG_CHUNK, SEG_CHUNK)
            make_copy(pl.multiple_of(src + o, SEG_ALIGN), pl.multiple_of(dst + o, SEG_ALIGN),
                      SEG_CHUNK).start(priority=1)
            return c
        shift = int(math.log2(SEG_CHUNK))
        full = n >> shift
        lax.fori_loop(0, full, chunk, 0)
        done = full << shift
        _pow2_copies(n - done, src + done, dst + done, make_copy, lambda c: c.start(), REST_BITS)
        return carry
    lax.fori_loop(0, N_EXPERTS, body, 0)


TAIL_BITS = tuple(range(int(math.log2(MOE_BLOCK)) - 1, int(math.log2(SEG_ALIGN)) - 1, -1))
TILE_BITS = tuple(range(int(math.log2(TILE_ROWS)), int(math.log2(SEG_ALIGN)) - 1, -1))


def _wait_rows(total, make_copy):
    _pow2_copies(total, 0, 0, make_copy, lambda c: c.wait(), TILE_BITS)


def _dispatch_kernel(n_ref, off_ref, dst_ref, tot_ref, tn_ref, td_ref, posT_ref, h_ref, xout_ref,
                     srt, zbuf, sems):
    i = pl.program_id(0)
    last = pl.num_programs(0) - 1
    slot = i % 2
    tm = h_ref.shape[0]
    r = lax.broadcasted_iota(I32, (TILE_ROWS, tm), 0)
    hit = jnp.zeros((TILE_ROWS, tm), jnp.bool_)
    for k in range(TOP_K):
        hit = hit | (r == posT_ref[k:k + 1, :].astype(I32))
    sel = jnp.where(hit, 1.0, 0.0).astype(BF16)
    srt[slot] = _pack_rows(_dot(sel, h_ref[...]))

    def copier(s):
        def make_copy(src, dst, sz):
            return pltpu.make_async_copy(srt.at[s, pl.ds(src, sz)], xout_ref.at[pl.ds(dst, sz)], sems.at[s])
        return make_copy

    _start_segments(i, n_ref, off_ref, dst_ref, copier(slot))

    @pl.when(i > 0)
    def _():
        _wait_rows(tot_ref[jnp.maximum(i - 1, 0)], copier(1 - slot))

    @pl.when(i == last)
    def _():
        _wait_rows(tot_ref[i], copier(slot))
        zbuf[...] = jnp.zeros_like(zbuf)
        sem = sems.at[0]

        def zero_copy(src, dst, sz):
            return pltpu.make_async_copy(zbuf.at[pl.ds(src, sz)], xout_ref.at[pl.ds(dst, sz)], sem)

        nb = xout_ref.shape[0] // MOE_BLOCK
        for op in (lambda c: c.start(), lambda c: c.wait()):
            def body(e, carry):
                _pow2_copies(tn_ref[e], 0, td_ref[e], zero_copy, op, TAIL_BITS)
                return carry
            lax.fori_loop(0, N_EXPERTS, body, 0)

            def unused(b, carry):
                op(zero_copy(0, pl.multiple_of(b * MOE_BLOCK, MOE_BLOCK), MOE_BLOCK))
                return carry
            lax.fori_loop(tn_ref[N_EXPERTS], nb, unused, 0)


def moe_dispatch(n_tab, off_tab, dst_tab, tot_tab, tail_n, tail_dst, posT, h2, n_rows):
    n_tok = h2.shape[0]
    tm = TOK_TILE
    grid_spec = pltpu.PrefetchScalarGridSpec(
        num_scalar_prefetch=6, grid=(n_tok // tm,),
        in_specs=[pl.BlockSpec((8, tm), lambda i, *_: (0, i)),
                  pl.BlockSpec((tm, D), lambda i, *_: (i, 0))],
        out_specs=pl.BlockSpec(memory_space=pl.ANY),
        scratch_shapes=[pltpu.VMEM((2, TILE_ROWS, ROW_WORDS), U32), pltpu.VMEM((MOE_BLOCK, ROW_WORDS), U32),
                        pltpu.SemaphoreType.DMA((2,))])
    return pl.pallas_call(
        _dispatch_kernel, grid_spec=grid_spec,
        out_shape=jax.ShapeDtypeStruct((n_rows, ROW_WORDS), U32),
        compiler_params=_cp(("arbitrary",)), name="moe_dispatch",
    )(n_tab, off_tab, dst_tab, tot_tab, tail_n, tail_dst, posT, h2)


def _combine_kernel(n_ref, off_ref, dst_ref, tot_ref, pos_ref, gate_ref, x_ref, g2_ref, y_ref, *rest, npt):
    o_refs, (buf, sems) = rest[:-2], rest[-2:]
    i = pl.program_id(0)
    last = pl.num_programs(0) - 1
    slot = i % 2
    tm = x_ref.shape[0]
    na = TILE_ROWS

    def copier(s):
        def make_copy(src, dst, sz):
            return pltpu.make_async_copy(y_ref.at[pl.ds(dst, sz)], buf.at[s, pl.ds(src, sz)], sems.at[s])
        return make_copy

    def fetch(tile, s):
        buf[s, tm * TOP_K:na, :] = jnp.zeros((na - tm * TOP_K, ROW_WORDS), U32)
        _start_segments(tile, n_ref, off_ref, dst_ref, copier(s))

    @pl.when(i == 0)
    def _():
        fetch(i, slot)

    @pl.when(i < last)
    def _():
        fetch(i + 1, 1 - slot)

    _wait_rows(tot_ref[i], copier(slot))
    lane = lax.broadcasted_iota(I32, (tm, na), 1)
    pw = jnp.zeros((tm, na), F32)
    for k in range(TOP_K):
        pw = pw + jnp.where(lane == pos_ref[:, k:k + 1], gate_ref[:, k:k + 1], 0.0)
    phi = pw.astype(BF16)
    plo = (pw - phi.astype(F32)).astype(BF16)
    yb = _unpack_rows(buf[slot]).astype(BF16)
    res = x_ref[...] + g2_ref[...] * (_dot(phi, yb) + _dot(plo, yb))
    if len(o_refs) == 1:
        o_refs[0][...] = res
    else:
        @pl.when(i < npt)
        def _():
            o_refs[0][...] = res

        @pl.when(i >= npt)
        def _():
            o_refs[1][...] = res


def moe_combine(lay, n_tab, off_tab, dst_tab, tot_tab, pos, gates, x, gate2, y_rows, split):
    n_tok = x.shape[0]
    tm = TOK_TILE
    npt = lay.p_tok // tm
    mrow = lambda i, *_: (lay.mod_row(i * tm), 0, 0)
    if split:
        out_specs = [pl.BlockSpec((tm, D), lambda i, *_: (jnp.minimum(i, npt - 1), 0)),
                     pl.BlockSpec((tm, D), lambda i, *_: (jnp.maximum(i - npt, 0), 0))]
        out_shape = [jax.ShapeDtypeStruct((lay.p_tok, D), F32), jax.ShapeDtypeStruct((n_tok - lay.p_tok, D), F32)]
    else:
        out_specs = pl.BlockSpec((tm, D), lambda i, *_: (i, 0))
        out_shape = jax.ShapeDtypeStruct((n_tok, D), F32)
    grid_spec = pltpu.PrefetchScalarGridSpec(
        num_scalar_prefetch=4, grid=(n_tok // tm,),
        in_specs=[pl.BlockSpec((tm, LANES), lambda i, *_: (i, 0)),
                  pl.BlockSpec((tm, LANES), lambda i, *_: (i, 0)),
                  pl.BlockSpec((tm, D), lambda i, *_: (i, 0)),
                  pl.BlockSpec((None, 1, D), mrow),
                  pl.BlockSpec(memory_space=pl.ANY)],
        out_specs=out_specs,
        scratch_shapes=[pltpu.VMEM((2, TILE_ROWS, ROW_WORDS), U32), pltpu.SemaphoreType.DMA((2,))])
    return pl.pallas_call(
        functools.partial(_combine_kernel, npt=npt), grid_spec=grid_spec, out_shape=out_shape,
        compiler_params=_cp(("arbitrary",)), name="moe_combine",
    )(n_tab, off_tab, dst_tab, tot_tab, pos, gates, x, gate2, y_rows)


def _expert_kernel(be_ref, nv_ref, nxt_ref, slot_ref, x_ref, b_ref, wg_hbm, wu_hbm, wd_hbm,
                   y_ref, wf, sems, *, layer):
    i = pl.program_id(0)
    valid = i < nv_ref[0]
    e = be_ref[i]
    slot = slot_ref[e]
    changed = jnp.logical_or(i == 0, e != be_ref[jnp.maximum(i - 1, 0)])

    def weight_copies(ex, s):
        return [pltpu.make_async_copy(w.at[layer, ex], wf.at[s, k], sems.at[s, k])
                for k, w in enumerate((wg_hbm, wu_hbm, wd_hbm))]

    @pl.when(jnp.logical_and(valid, changed))
    def _():
        @pl.when(i == 0)
        def _():
            for c in weight_copies(e, slot):
                c.start()

        nxt = nxt_ref[e]

        @pl.when(nxt >= 0)
        def _():
            for c in weight_copies(nxt, 1 - slot):
                c.start(priority=1)

        for c in weight_copies(e, slot):
            c.wait()

    @pl.when(valid)
    def _():
        x = _unpack_rows(x_ref[...])
        b = b_ref[e]
        gt = jnp.minimum(_dot(x, wf[slot, 0]) + b[0:1, :], SWIGLU_LIMIT)
        up = jnp.clip(_dot(x, wf[slot, 1]) + b[1:2, :], -SWIGLU_LIMIT, SWIGLU_LIMIT)
        act = (up + 1.0) * gt * _sigmoid(SWIGLU_ALPHA * gt)
        y = _dot(act, wf[slot, 2]) + b[2:3, :]
        y_ref[...] = _pack_rows(y.astype(BF16).astype(F32))

    @pl.when(jnp.logical_not(valid))
    def _():
        y_ref[...] = jnp.zeros_like(y_ref)


def moe_experts(layer, blk_expert, n_valid, next_expert, slot, x_rows, w_gate, b_gate, w_up, b_up, w_down,
                b_down):
    n_rows = x_rows.shape[0]
    nb = n_rows // MOE_BLOCK
    depth, ne, _, ff = w_gate.shape
    assert ff == D
    rowblk = lambda i, be, nv, *_: (jnp.maximum(jnp.minimum(i, nv[0] - 1), 0), 0)
    hbm = pl.BlockSpec(memory_space=pl.ANY)
    biases = jnp.stack([b_gate, b_up, b_down], axis=2)
    grid_spec = pltpu.PrefetchScalarGridSpec(
        num_scalar_prefetch=4, grid=(nb,),
        in_specs=[pl.BlockSpec((MOE_BLOCK, ROW_WORDS), rowblk),
                  pl.BlockSpec((None, ne, 3, D), lambda i, *_: (layer, 0, 0, 0)), hbm, hbm, hbm],
        out_specs=pl.BlockSpec((MOE_BLOCK, ROW_WORDS), lambda i, *_: (i, 0)),
        scratch_shapes=[pltpu.VMEM((2, 3, D, ff), F32), pltpu.SemaphoreType.DMA((2, 3))])
    return pl.pallas_call(
        functools.partial(_expert_kernel, layer=layer), grid_spec=grid_spec,
        out_shape=jax.ShapeDtypeStruct((n_rows, ROW_WORDS), U32),
        compiler_params=_cp(("arbitrary",)), name="moe_experts",
    )(blk_expert, n_valid, next_expert, slot, x_rows, biases, w_gate, w_up, w_down)


def moe_layer(lay, layer, x, g2, sc2, sh2, gate2, router_w, router_b, w_gate, b_gate, w_up, b_up, w_down,
              b_down, split=False):
    n_tok = x.shape[0]
    nt = n_tok // TOK_TILE
    rw = jnp.zeros((D, LANES), F32).at[:, :N_EXPERTS].set(router_w)
    rw = _hilo(rw)
    rb = jnp.full((1, LANES), NEG, F32).at[0, :N_EXPERTS].set(router_b)
    h2, _, gates, pos, posT, cnt = moe_router(lay, x, g2, sc2, sh2, rw, rb)
    n_te = cnt[:, 0, :N_EXPERTS].astype(I32)
    n_te = (n_te + SEG_ALIGN - 1) // SEG_ALIGN * SEG_ALIGN
    totals = jnp.sum(n_te, axis=0)
    padded = (totals + MOE_BLOCK - 1) // MOE_BLOCK * MOE_BLOCK
    padded_end = jnp.cumsum(padded)
    pstart = padded_end - padded
    dst = pstart[None, :] + jnp.cumsum(n_te, axis=0) - n_te
    off = jnp.cumsum(n_te, axis=1) - n_te
    n_rows = nt * TILE_ROWS + N_EXPERTS * MOE_BLOCK
    nb = n_rows // MOE_BLOCK
    n_valid = (padded_end[-1] // MOE_BLOCK).astype(I32).reshape(1)
    bstart = jnp.minimum(jnp.arange(nb, dtype=I32), n_valid[0] - 1) * MOE_BLOCK
    blk_expert = jnp.minimum(jnp.sum((bstart[:, None] >= padded_end[None, :]).astype(I32), axis=1),
                             N_EXPERTS - 1).astype(I32)
    tabs = (n_te.reshape(-1).astype(I32), off.reshape(-1).astype(I32), dst.reshape(-1).astype(I32),
            jnp.sum(n_te, axis=1).astype(I32))
    tail_n = jnp.concatenate([(padded - totals).astype(I32), n_valid])
    x_rows = moe_dispatch(*tabs, tail_n, (pstart + totals).astype(I32), posT, h2, n_rows)
    owner = jnp.where(padded > 0, jnp.arange(N_EXPERTS, dtype=I32), N_EXPERTS)
    later = jnp.concatenate([lax.cummin(owner, axis=0, reverse=True)[1:], jnp.full((1,), N_EXPERTS, I32)])
    next_expert = jnp.where(later < N_EXPERTS, later, -1).astype(I32)
    slot = ((jnp.cumsum((padded > 0).astype(I32)) - 1) % 2).astype(I32)
    y_rows = moe_experts(layer, blk_expert, n_valid, next_expert, slot, x_rows, w_gate, b_gate, w_up, b_up,
                         w_down, b_down)
    return moe_combine(lay, *tabs, pos, gates, x, gate2, y_rows, split)


QKV_TN = 256
N_QK_TILES = (ATT_HEADS + ATT_KV) * ATT_HD // QKV_TN


def _qkv_kernel(x_ref, g_ref, sc_ref, sh_ref, w_ref, nw_ref, cos_ref, sin_ref, o_ref):
    tm = x_ref.shape[0]
    h = _modnorm(x_ref[...], g_ref[...], sc_ref[...], sh_ref[...]).astype(BF16)
    r = lax.broadcasted_iota(I32, (QKV_TN, QKV_TN), 0) // ATT_HD
    c = lax.broadcasted_iota(I32, (QKV_TN, QKV_TN), 1) // ATT_HD
    head_mean = jnp.where(r == c, 1.0 / ATT_HD, 0.0).astype(BF16)
    lane = lax.broadcasted_iota(I32, (tm, QKV_TN), 1)
    half = ATT_HD // 4
    first = (lane % (2 * half)) < half
    for j in range(w_ref.shape[1] // QKV_TN):
        cols = slice(QKV_TN * j, QKV_TN * (j + 1))
        acc = _dot(h, w_ref[:, cols])
        if j >= N_QK_TILES:
            o_ref[:, cols] = acc
            continue
        ms = _dot((acc * acc).astype(BF16), head_mean)
        qn = acc * lax.rsqrt(ms + EPS) * nw_ref[j]
        swapped = jnp.where(first, pltpu.roll(qn, QKV_TN - half, 1), pltpu.roll(qn, half, 1))
        o_ref[:, cols] = qn * cos_ref[...] + swapped * sin_ref[...]


def _rope_tables(sample_len):
    pos = np.arange(sample_len)
    half = ATT_HD // 4
    inv = (ROPE_THETA ** (-np.arange(half, dtype=np.float32) / half)).astype(np.float32)
    ang_r = (pos // GRID_W).astype(np.float32)[:, None] * inv[None, :]
    ang_c = (pos % GRID_W).astype(np.float32)[:, None] * inv[None, :]
    cos = np.concatenate([np.cos(ang_r)] * 2 + [np.cos(ang_c)] * 2, axis=1)
    sin = np.concatenate([-np.sin(ang_r), np.sin(ang_r), -np.sin(ang_c), np.sin(ang_c)], axis=1)
    rep = QKV_TN // ATT_HD
    return (jnp.asarray(np.tile(cos, (1, rep)), F32), jnp.asarray(np.tile(sin, (1, rep)), F32))


def qkv_proj(lay, x, g, sc, sh, w, q_norm, k_norm, tm=512):
    n_tok = x.shape[0]
    n = w.shape[1]
    nq = ATT_HEADS * ATT_HD // QKV_TN
    rep = QKV_TN // ATT_HD
    nw = jnp.concatenate([jnp.tile(jnp.tile(q_norm, rep)[None, :], (nq, 1)),
                          jnp.tile(jnp.tile(k_norm, rep)[None, :], (n // QKV_TN - nq, 1))], axis=0)
    cos, sin = _rope_tables(lay.sample_len)
    cos = jnp.concatenate([jnp.ones((tm, QKV_TN), F32), cos], axis=0)
    sin = jnp.concatenate([jnp.zeros((tm, QKV_TN), F32), sin], axis=0)
    assert lay.p_tok % tm == 0 and lay.sample_len % tm == 0
    mrow = lambda i: (lay.mod_row(i * tm), 0, 0)
    rrow = lambda i: (jnp.where(i * tm < lay.p_tok, 0, 1 + ((i * tm - lay.p_tok) % lay.sample_len) // tm), 0)
    nt = n // QKV_TN
    return pl.pallas_call(
        _qkv_kernel, grid=(n_tok // tm,),
        in_specs=[pl.BlockSpec((tm, D), lambda i: (i, 0)),
                  pl.BlockSpec((1, D), lambda i: (0, 0)),
                  pl.BlockSpec((None, 1, D), mrow), pl.BlockSpec((None, 1, D), mrow),
                  pl.BlockSpec((D, n), lambda i: (0, 0)),
                  pl.BlockSpec((nt, 1, QKV_TN), lambda i: (0, 0, 0)),
                  pl.BlockSpec((tm, QKV_TN), rrow), pl.BlockSpec((tm, QKV_TN), rrow)],
        out_specs=pl.BlockSpec((tm, n), lambda i: (i, 0)),
        out_shape=jax.ShapeDtypeStruct((n_tok, n), F32),
        compiler_params=_cp(("arbitrary",)), name="qkv_proj",
    )(x, g.reshape(1, D), sc, sh, w, nw.reshape(nt, 1, QKV_TN), cos, sin)


def _dup_group(x, g):
    blk = x[:, LANES * (g // 2):LANES * (g // 2 + 1)]
    if g % 2 == 1:
        blk = pltpu.roll(blk, ATT_HD, 1)
    lo = lax.broadcasted_iota(I32, blk.shape, 1) < ATT_HD
    low = jnp.where(lo, blk, 0.0)
    return low + pltpu.roll(low, ATT_HD, 1)


def _attend(q_ref, k_all, v_all, mask, sink_ref, o_ref):
    nq = q_ref.shape[0]
    lo = lax.broadcasted_iota(I32, (nq, LANES), 1) < ATT_HD
    first = lax.broadcasted_iota(I32, (2 * nq, 1), 0) < nq
    if mask is not None:
        mask = jnp.concatenate([mask, mask], axis=0)
    grp = ATT_HEADS // ATT_KV
    for g in range(ATT_KV):
        k2 = _dup_group(k_all, g).astype(BF16)
        v2 = _dup_group(v_all, g).astype(BF16)
        for jp in range(grp // 2):
            j = g * (grp // 2) + jp
            qp = q_ref[:, LANES * j:LANES * (j + 1)] * (ATT_HD ** -0.5)
            qs = jnp.concatenate([jnp.where(lo, qp, 0.0), jnp.where(lo, 0.0, qp)], axis=0)
            s = _dot_nt(qs.astype(BF16), k2)
            if mask is not None:
                s = jnp.where(mask, s, NEG)
            sink = jnp.where(first, sink_ref[2 * j], sink_ref[2 * j + 1])
            m = jnp.maximum(jnp.max(s, axis=1, keepdims=True), sink)
            p = jnp.exp(s - m)
            den = jnp.sum(p, axis=1, keepdims=True) + jnp.exp(sink - m)
            o = _dot(p.astype(BF16), v2) / den
            o_ref[:, LANES * j:LANES * (j + 1)] = jnp.where(lo, o[:nq], o[nq:]).astype(o_ref.dtype)


def _attn_ctx_kernel(sink_ref, q_ref, k_ref, v_ref, o_ref):
    _attend(q_ref, k_ref[...], v_ref[...], None, sink_ref, o_ref)


def attn_context(lay, qkv, sinks):
    qw = ATT_HEADS * ATT_HD
    kw = ATT_KV * ATT_HD
    ln = lay.prompt_len
    grid_spec = pltpu.PrefetchScalarGridSpec(
        num_scalar_prefetch=0, grid=(lay.n_prompt,),
        in_specs=[pl.BlockSpec(memory_space=pltpu.SMEM),
                  pl.BlockSpec((ln, qw), lambda b: (b, 0)),
                  pl.BlockSpec((ln, kw), lambda b: (b, qw // kw)),
                  pl.BlockSpec((ln, kw), lambda b: (b, qw // kw + 1))],
        out_specs=pl.BlockSpec((ln, qw), lambda b: (b, 0)))
    return pl.pallas_call(
        _attn_ctx_kernel, grid_spec=grid_spec,
        out_shape=jax.ShapeDtypeStruct((lay.p_tok, qw), BF16),
        compiler_params=_cp(("arbitrary",)), name="attn_context",
    )(sinks, qkv, qkv, qkv)


def _attn_lat_kernel(sink_ref, q_ref, kp_ref, kc_ref, kn_ref, vp_ref, vc_ref, vn_ref, ck_ref, cv_ref, o_ref,
                     *, nblk):
    i = pl.program_id(1)
    bq = ATT_BLOCK
    nctx = ck_ref.shape[1]
    k_all = jnp.concatenate([kp_ref[...], kc_ref[...], kn_ref[...], ck_ref[0]], axis=0)
    v_all = jnp.concatenate([vp_ref[...], vc_ref[...], vn_ref[...], cv_ref[0]], axis=0)
    ns = 3 * bq + nctx
    r = lax.broadcasted_iota(I32, (bq, ns), 0)
    c = lax.broadcasted_iota(I32, (bq, ns), 1)
    rel = c - r
    first_key = jnp.where(i > 0, 0, bq)
    end_key = jnp.where(i < nblk - 1, 3 * bq, 2 * bq)
    band = (rel >= bq - WINDOW) & (rel <= bq + WINDOW) & (c >= first_key) & (c < end_key)
    mask = band | (c >= 3 * bq)
    _attend(q_ref, k_all, v_all, mask, sink_ref, o_ref)


def attn_latent(lay, qkv, cache_k, cache_v, sinks):
    qw = ATT_HEADS * ATT_HD
    kw = ATT_KV * ATT_HD
    bq = ATT_BLOCK
    nblk = lay.sample_len // bq
    b0 = lay.p_tok // bq
    nctx = cache_k.shape[1]
    rb = lambda b, i: b0 + b * nblk + i
    kspec = lambda cb, sh: pl.BlockSpec(
        (bq, kw), lambda b, i: (b0 + b * nblk + jnp.clip(i + sh, 0, nblk - 1), cb))
    kc, vc = qw // kw, qw // kw + 1
    grid_spec = pltpu.PrefetchScalarGridSpec(
        num_scalar_prefetch=0, grid=(lay.n_sample, nblk),
        in_specs=[pl.BlockSpec(memory_space=pltpu.SMEM),
                  pl.BlockSpec((bq, qw), lambda b, i: (rb(b, i), 0)),
                  kspec(kc, -1), kspec(kc, 0), kspec(kc, 1),
                  kspec(vc, -1), kspec(vc, 0), kspec(vc, 1),
                  pl.BlockSpec((1, nctx, kw), lambda b, i: (b, 0, 0)),
                  pl.BlockSpec((1, nctx, kw), lambda b, i: (b, 0, 0))],
        out_specs=pl.BlockSpec((bq, qw), lambda b, i: (b * nblk + i, 0)))
    return pl.pallas_call(
        functools.partial(_attn_lat_kernel, nblk=nblk), grid_spec=grid_spec,
        out_shape=jax.ShapeDtypeStruct((lay.n_sample * lay.sample_len, qw), BF16),
        compiler_params=_cp(("arbitrary", "arbitrary")), name="attn_latent",
    )(sinks, qkv, qkv, qkv, qkv, qkv, qkv, qkv,
      cache_k.reshape(lay.n_sample, nctx, kw), cache_v.reshape(lay.n_sample, nctx, kw))


def _forward(lay, x_prompt, x_sample, state_l0_ssd_fwd, state_l0_ssd_bwd, state_l0_gla_fwd, state_l0_gla_bwd,
             cache_l1_k, cache_l1_v, c, c_ctx, ada_w, ada_b, norm1, norm2,
             l0_w_in, l0_conv_w, l0_conv_b, l0_a_log, l0_dt_bias, l0_d_skip, l0_ssd_norm,
             l0_gate_w2, l0_gate_b, l0_gla_norm, l0_w_out,
             l1_w_qkv, l1_q_norm, l1_k_norm, l1_sinks, l1_w_out,
             router_w, router_b, exp_w_gate, exp_b_gate, exp_w_up, exp_b_up, exp_w_down, exp_b_down):
    np_, ns = lay.n_prompt, lay.n_sample
    x = (x_prompt.reshape(-1, D), x_sample.reshape(-1, D))
    cond8 = jnp.zeros((8, D), F32).at[0].set(c_ctx).at[1:1 + ns].set(c)
    mod = ada_table(cond8, ada_w, ada_b)
    mods = [[mod[l, :, p * D:(p + 1) * D].reshape(8, 1, D) for p in range(N_ADA)] for l in range(2)]

    def moe(l, xx, split=False):
        return moe_layer(lay, l, xx, norm2[l], mods[l][4], mods[l][3], mods[l][5], router_w[l], router_b[l],
                         exp_w_gate, exp_b_gate, exp_w_up, exp_b_up, exp_w_down, exp_b_down, split=split)

    sp = np.cumsum((SSD_INNER, SSD_INNER + 2 * SSD_GROUPS * SSD_STATE, 2 * SSD_HEADS,
                    GLA_HEADS * GLA_KEY_DIM, GLA_HEADS * GLA_KEY_DIM,
                    GLA_HEADS * GLA_VAL_DIM, GLA_HEADS * GLA_VAL_DIM, 2 * GLA_RANK))
    cols = lambda a, b: l0_w_in[:, a:b]
    w_main = jnp.concatenate([cols(0, sp[0]), cols(sp[4], sp[5]), cols(sp[5], sp[6]), cols(sp[0], sp[1]),
                              cols(sp[2], sp[3]), cols(sp[3], sp[4])], axis=1).astype(BF16)
    w_small = jnp.concatenate([cols(sp[1], sp[2]), cols(sp[6], sp[7]),
                               jnp.zeros((D, LANES - 2 * SSD_HEADS - 2 * GLA_RANK), F32)], axis=1)
    proj, small, small_t = norm_proj(lay, x, norm1[0], mods[0][1], mods[0][0], w_main, _hilo(w_small), 512,
                                      PJ_W // 2, BF16, SSD_CHUNK)
    y_n, o_n, (ssd_f, ssd_b, gla_f, gla_b) = l0_mixers(
        lay, proj, small, small_t, state_l0_ssd_fwd, state_l0_ssd_bwd, state_l0_gla_fwd, state_l0_gla_bwd,
        l0_conv_w, l0_conv_b, l0_a_log, l0_dt_bias, l0_d_skip, l0_ssd_norm,
        l0_gate_w2, l0_gate_b, l0_gla_norm)
    x = proj_residual(lay, [y_n, o_n], l0_w_out.astype(BF16), x, mods[0][2])
    x = moe(0, x)

    qkv = qkv_proj(lay, x, norm1[1], mods[1][1], mods[1][0], l1_w_qkv.astype(BF16), l1_q_norm, l1_k_norm)
    o_ctx = attn_context(lay, qkv, l1_sinks)
    o_lat = attn_latent(lay, qkv, cache_l1_k, cache_l1_v, l1_sinks)
    x = proj_residual(lay, [(o_ctx, o_lat)], l1_w_out.astype(BF16), x, mods[1][2])
    xp, xs = moe(1, x, split=True)

    qw = ATT_HEADS * ATT_HD
    kw = ATT_KV * ATT_HD
    return (xp.reshape(x_prompt.shape), xs.reshape(x_sample.shape),
            ssd_f[:np_].reshape(np_, SSD_HEADS, SSD_HEAD_DIM, SSD_STATE),
            ssd_b[:np_].reshape(np_, SSD_HEADS, SSD_HEAD_DIM, SSD_STATE),
            gla_f[:np_], gla_b[:np_],
            qkv[:lay.p_tok, qw:qw + kw].reshape(np_, lay.prompt_len, ATT_KV, ATT_HD),
            qkv[:lay.p_tok, qw + kw:].reshape(np_, lay.prompt_len, ATT_KV, ATT_HD))


def kernel(x_prompt, x_sample, state_l0_ssd_fwd, state_l0_ssd_bwd, state_l0_gla_fwd, state_l0_gla_bwd, cache_l1_k, cache_l1_v, c, c_ctx, ada_w, ada_b, norm1, norm2, l0_w_in, l0_conv_w, l0_conv_b, l0_a_log, l0_dt_bias, l0_d_skip, l0_ssd_norm, l0_gate_w2, l0_gate_b, l0_gla_norm, l0_w_out, l1_w_qkv, l1_q_norm, l1_k_norm, l1_sinks, l1_w_out, router_w, router_b, exp_w_gate, exp_b_gate, exp_w_up, exp_b_up, exp_w_down, exp_b_down):
    lay = Layout(x_prompt.shape[0], x_prompt.shape[1], x_sample.shape[0], x_sample.shape[1])
    return _forward(lay, x_prompt, x_sample, state_l0_ssd_fwd, state_l0_ssd_bwd, state_l0_gla_fwd,
                    state_l0_gla_bwd, cache_l1_k, cache_l1_v, c, c_ctx, ada_w, ada_b, norm1, norm2,
                    l0_w_in, l0_conv_w, l0_conv_b, l0_a_log, l0_dt_bias, l0_d_skip, l0_ssd_norm,
                    l0_gate_w2, l0_gate_b, l0_gla_norm, l0_w_out,
                    l1_w_qkv, l1_q_norm, l1_k_norm, l1_sinks, l1_w_out,
                    router_w, router_b, exp_w_gate, exp_b_gate, exp_w_up, exp_b_up, exp_w_down, exp_b_down)
```

```python
import functools
import math

import numpy as np
import jax
import jax.numpy as jnp
from jax import lax
from jax.experimental import pallas as pl
from jax.experimental.pallas import tpu as pltpu

F32 = jnp.float32
BF16 = jnp.bfloat16
I32 = jnp.int32
HI = lax.Precision.HIGHEST

D = 1024
EPS = 1e-6
N_ADA = 6
SSD_HEADS = 16
SSD_HEAD_DIM = 64
SSD_INNER = 1024
SSD_STATE = 128
SSD_GROUPS = 2
SSD_CONV = 5
SSD_CHUNK = 128
GLA_HEADS = 4
GLA_KEY_DIM = 128
GLA_VAL_DIM = 256
GLA_RANK = 16
GLA_TAU = 16.0
GLA_BLOCK = 64
ATT_HEADS = 16
ATT_KV = 4
ATT_HD = 64
ATT_BLOCK = 128
WINDOW = 128
GRID_W = 64
ROPE_THETA = 10000.0
N_EXPERTS = 32
TOP_K = 4
EXPERT_FF = 1024
SWIGLU_LIMIT = 7.0
SWIGLU_ALPHA = 1.702
MOE_BLOCK = 256
TOK_TILE = 256
LANES = 128
NEG = -1e30

PJ_Z, PJ_V, PJ_OG, PJ_XBC, PJ_Q, PJ_K = 0, 1024, 2048, 3072, 4608, 5120
PJ_W = 5632
VMEM_LIMIT = 48 * 1024 * 1024


def _cp(sem, vmem=VMEM_LIMIT):
    return pltpu.CompilerParams(dimension_semantics=sem, vmem_limit_bytes=vmem)


class Layout:
    def __init__(self, n_prompt, prompt_len, n_sample, sample_len):
        self.n_prompt, self.prompt_len = n_prompt, prompt_len
        self.n_sample, self.sample_len = n_sample, sample_len
        self.p_tok = n_prompt * prompt_len
        self.n_tok = self.p_tok + n_sample * sample_len
        self.seqs = [(i * prompt_len, prompt_len) for i in range(n_prompt)]
        self.seqs += [(self.p_tok + i * sample_len, sample_len) for i in range(n_sample)]
        self.n_seq = len(self.seqs)

    def mod_row(self, start):
        return jnp.where(start < self.p_tok, 0, 1 + (start - self.p_tok) // self.sample_len)

def _sigmoid(x):
    return 1.0 / (1.0 + jnp.exp(-x))


def _silu(x):
    return x * _sigmoid(x)


def _softplus(x):
    return jnp.maximum(x, 0.0) + jnp.log(1.0 + jnp.exp(-jnp.abs(x)))


def _modnorm(x, g, sc, sh):
    ms = jnp.mean(x * x, axis=-1, keepdims=True)
    return (x * lax.rsqrt(ms + EPS) * g) * (1.0 + sc) + sh


def _dot(a, b, **kw):
    return jnp.dot(a, b, preferred_element_type=F32, **kw)


def _dot_nt(a, b):
    return lax.dot_general(a, b, (((1,), (1,)), ((), ())), preferred_element_type=F32)


def _dot_tn(a, b):
    return lax.dot_general(a, b, (((0,), (0,)), ((), ())), preferred_element_type=F32)


def _split(x, n):
    parts = []
    for _ in range(n):
        p = x.astype(BF16)
        parts.append(p)
        x = x - p.astype(F32)
    return parts


def _dot_sel(sel, x):
    sel = sel.astype(BF16)
    return sum(_dot(sel, p) for p in _split(x, 3))


def _dot_sel_r(x, sel):
    sel = sel.astype(BF16)
    return sum(_dot(p, sel) for p in _split(x, 3))


def _dot_hilo(x, w_hi, w_lo):
    x_hi, x_lo = _split(x, 2)
    return _dot(x_hi, w_hi) + _dot(x_lo, w_hi) + _dot(x_hi, w_lo)


def _hilo(w):
    hi = w.astype(BF16)
    return jnp.stack([hi, (w - hi.astype(F32)).astype(BF16)])


U32 = jnp.uint32
ROW_WORDS = D // 2
_HI_MASK = 0xFFFF0000


def _pack_rows(x):
    lo = lax.bitcast_convert_type(x[:, :ROW_WORDS], U32) >> 16
    hi = lax.bitcast_convert_type(x[:, ROW_WORDS:], U32) & jnp.uint32(_HI_MASK)
    return lo | hi


def _unpack_rows(u):
    lo = lax.bitcast_convert_type(u << 16, F32)
    hi = lax.bitcast_convert_type(u & jnp.uint32(_HI_MASK), F32)
    return jnp.concatenate([lo, hi], axis=1)


def _ada_kernel(c_ref, w_ref, b_ref, o_ref):
    o_ref[0] = _dot(_silu(c_ref[...]), w_ref[0], precision=HI) + b_ref[0]


def ada_table(cond8, ada_w, ada_b):
    depth, _, n = ada_w.shape
    tn = 1536
    return pl.pallas_call(
        _ada_kernel, grid=(depth, n // tn),
        in_specs=[pl.BlockSpec((8, D), lambda l, j: (0, 0)),
                  pl.BlockSpec((1, D, tn), lambda l, j: (l, 0, j)),
                  pl.BlockSpec((1, 1, tn), lambda l, j: (l, 0, j))],
        out_specs=pl.BlockSpec((1, 8, tn), lambda l, j: (l, 0, j)),
        out_shape=jax.ShapeDtypeStruct((depth, 8, n), F32),
        compiler_params=_cp(("arbitrary", "arbitrary")), name="ada_table",
    )(cond8, ada_w, ada_b.reshape(depth, 1, n))


def _stream_specs(lay, stream, tm):
    if not isinstance(stream, (tuple, list)):
        return [stream], [pl.BlockSpec((tm, stream.shape[1]), lambda i, *_: (i, 0))]
    assert lay.p_tok % tm == 0
    npt = lay.p_tok // tm
    w = stream[0].shape[1]
    return list(stream), [pl.BlockSpec((tm, w), lambda i, *_: (jnp.minimum(i, npt - 1), 0)),
                          pl.BlockSpec((tm, w), lambda i, *_: (jnp.maximum(i - npt, 0), 0))]


def _stream_tile(refs, in_prompt):
    if len(refs) == 1:
        return refs[0][...]
    return jnp.where(in_prompt, refs[0][...], refs[1][...])


def _proj_kernel(*refs, nx, npt):
    x_refs = refs[:nx]
    g_ref, sc_ref, sh_ref, w_ref, ws_ref, o_ref, os_ref, ost_ref, h_scr = refs[nx:]

    @pl.when(pl.program_id(1) == 0)
    def _():
        x = _stream_tile(x_refs, pl.program_id(0) < npt)
        h = _modnorm(x, g_ref[...], sc_ref[...], sh_ref[...])
        h_scr[...] = h.astype(BF16)
        small = _dot_hilo(h, ws_ref[0], ws_ref[1])
        os_ref[...] = small
        q = ost_ref.shape[2]
        for c in range(ost_ref.shape[0]):
            ost_ref[c] = small[q * c:q * (c + 1), :].T

    o_ref[...] = _dot(h_scr[...], w_ref[...]).astype(o_ref.dtype)


def norm_proj(lay, x, g, sc, sh, w, w_small, tm, tn, out_dtype, chunk):
    n_tok = lay.n_tok
    n = w.shape[1]
    ns = w_small.shape[-1]
    mrow = lambda i, j: (lay.mod_row(i * tm), 0, 0)
    xs, x_specs = _stream_specs(lay, x, tm)
    return pl.pallas_call(
        functools.partial(_proj_kernel, nx=len(xs), npt=lay.p_tok // tm), grid=(n_tok // tm, n // tn),
        in_specs=x_specs + [pl.BlockSpec((1, D), lambda i, j: (0, 0)),
                            pl.BlockSpec((None, 1, D), mrow),
                            pl.BlockSpec((None, 1, D), mrow),
                            pl.BlockSpec((D, tn), lambda i, j: (0, j)),
                            pl.BlockSpec((2, D, ns), lambda i, j: (0, 0, 0))],
        out_specs=[pl.BlockSpec((tm, tn), lambda i, j: (i, j)),
                   pl.BlockSpec((tm, ns), lambda i, j: (i, 0)),
                   pl.BlockSpec((tm // chunk, ns, chunk), lambda i, j: (i, 0, 0))],
        out_shape=[jax.ShapeDtypeStruct((n_tok, n), out_dtype),
                   jax.ShapeDtypeStruct((n_tok, ns), F32),
                   jax.ShapeDtypeStruct((n_tok // chunk, ns, chunk), F32)],
        scratch_shapes=[pltpu.VMEM((tm, D), BF16)],
        compiler_params=_cp(("arbitrary", "arbitrary")), name="norm_proj",
    )(*xs, g.reshape(1, D), sc, sh, w, w_small)


CONV_HALO = 16


def _ssd_load(d, c, xc, dtg_ref, dtgT_ref, S):
    q = SSD_CHUNK
    nh = SSD_HEADS
    rows = pl.ds(pl.multiple_of(c * q, q), q)
    return (rows, xc[rows, 0:SSD_INNER], xc[rows, SSD_INNER:SSD_INNER + 2 * SSD_GROUPS * SSD_STATE],
            dtg_ref[rows, nh * d:nh * d + nh], dtgT_ref[c, nh * d:nh * d + nh, :], S[...])


def _ssd_chunks(loaded, alog_ref, alogT_ref, dtb_ref, dtbT_ref):
    q = SSD_CHUNK
    nh = SSD_HEADS
    hp = nh * SSD_HEAD_DIM
    row = lax.broadcasted_iota(I32, (q, q), 0)
    col = lax.broadcasted_iota(I32, (q, q), 1)
    lo = lax.broadcasted_iota(I32, (q, LANES), 1) < SSD_HEAD_DIM
    head_of = lax.broadcasted_iota(I32, (nh, hp), 1) // SSD_HEAD_DIM
    spread = (lax.broadcasted_iota(I32, (nh, hp), 0) == head_of).astype(BF16)
    rep = (nh // SSD_GROUPS) // 2
    pairs = [(g, j) for g in range(SSD_GROUPS) for j in range(rep * g, rep * (g + 1))]
    lane = lambda j: slice(LANES * j, LANES * (j + 1))

    def per_lane(arr, passes):
        return sum(_dot(p, spread) for p in _split(arr, passes))

    stage1 = []
    for d, (_, _, _, dtg, dtgT, _) in enumerate(loaded):
        dt = _softplus(dtg + dtb_ref[d:d + 1, :])
        dtT = _softplus(dtgT + dtbT_ref[:, d:d + 1])
        ad = dt * (-jnp.exp(alog_ref[d:d + 1, :]))
        adT = dtT * (-jnp.exp(alogT_ref[:, d:d + 1]))
        if d == 0:
            e, eT = _dot_sel(col <= row, ad), _dot_sel_r(adT, row <= col)
        else:
            e, eT = _dot_sel(col < row, ad), _dot_sel_r(adT, row < col)
        stage1.append((dt, ad, e, eT))
    stage2 = []
    for d, ((_, xs, bc, _, _, s), (dt, ad, e, eT)) in enumerate(zip(loaded, stage1)):
        if d == 0:
            tot = e[q - 1:q, :]
            fq, fk = jnp.exp(e), jnp.exp(tot - e)
        else:
            tot = jnp.sum(ad, axis=0, keepdims=True)
            fq, fk = jnp.exp(tot - e), jnp.exp(e)
        dt_x, fq_x, fk_x = per_lane(dt, 1), per_lane(fq, 1), per_lane(fk, 1)
        dec_x = per_lane(jnp.broadcast_to(jnp.exp(tot), (8, nh)), 3)[0:1, :]
        xdt_all = xs.astype(F32) * dt_x
        xk_all = (xdt_all * fk_x).astype(BF16)
        bg_t = [bc[:, SSD_STATE * g:SSD_STATE * (g + 1)].T for g in range(SSD_GROUPS)]
        cg = [bc[:, SSD_STATE * (SSD_GROUPS + g):SSD_STATE * (SSD_GROUPS + g + 1)] for g in range(SSD_GROUPS)]
        gmat = [_dot(cg[g], bg_t[g]) for g in range(SSD_GROUPS)]
        carried = [_dot(cg[g], s[:, lane(j)].astype(BF16)) * fq_x[:, lane(j)] for g, j in pairs]
        s_new = [s[:, lane(j)] * dec_x[:, lane(j)] + _dot(bg_t[g], xk_all[:, lane(j)]) for g, j in pairs]
        stage2.append((xdt_all, gmat, carried, s_new))
    results = []
    for d, ((_, _, e, eT), (xdt_all, gmat, carried, s_new)) in enumerate(zip(stage1, stage2)):
        mask = row >= col if d == 0 else col >= row
        ys = []
        for (g, j), off in zip(pairs, carried):
            parts = []
            for hh in (2 * j, 2 * j + 1):
                diff = e[:, hh:hh + 1] - eT[hh:hh + 1, :] if d == 0 else eT[hh:hh + 1, :] - e[:, hh:hh + 1]
                parts.append((gmat[g] * jnp.exp(jnp.where(mask, diff, NEG))).astype(BF16))
            lhs = jnp.concatenate(parts, axis=1)
            xdt = xdt_all[:, lane(j)]
            rhs = jnp.concatenate([jnp.where(lo, xdt, 0.0), jnp.where(lo, 0.0, xdt)], axis=0)
            ys.append(_dot(lhs, rhs.astype(BF16)) + off)
        results.append((jnp.concatenate(ys, axis=1), jnp.concatenate(s_new, axis=1)))
    return results


def _ssd_seq_kernel(*refs, has_init):
    (xbc_ref, z_ref, dtg_ref, dtgT_ref, cw_ref, cb_ref,
     alog_ref, alogT_ref, dtb_ref, dtbT_ref, dskip_ref, nrm_ref) = refs[:12]
    refs = refs[12:]
    if has_init:
        s0f_ref, s0b_ref = refs[:2]
        refs = refs[2:]
    y_ref, sf_ref, sb_ref, xc, ext, Sf, Sb, yf, yb = refs
    ln = xbc_ref.shape[0]
    q = SSD_CHUNK
    nc = ln // q
    h = CONV_HALO
    pad = SSD_CONV // 2

    def conv_body(c, carry):
        r0 = pl.multiple_of(c * q, q)
        prev = xbc_ref[pl.ds(pl.multiple_of(jnp.maximum(r0 - h, 0), h), h), :].astype(F32)
        nxt = xbc_ref[pl.ds(pl.multiple_of(jnp.minimum(r0 + q, ln - h), h), h), :].astype(F32)
        ext[0:h, :] = jnp.where(c > 0, prev, 0.0)
        ext[h:h + q, :] = xbc_ref[pl.ds(r0, q), :].astype(F32)
        ext[h + q:h + q + h, :] = jnp.where(c < nc - 1, nxt, 0.0)
        acc = jnp.broadcast_to(cb_ref[...], (q, cb_ref.shape[1]))
        for k in range(SSD_CONV):
            acc = acc + cw_ref[k:k + 1, :] * ext[h - pad + k:h - pad + k + q, :]
        xc[pl.ds(r0, q), :] = _silu(acc).astype(xc.dtype)
        return carry
    lax.fori_loop(0, nc, conv_body, 0)

    if has_init:
        Sf[...] = s0f_ref[0].T
        Sb[...] = s0b_ref[0].T
    else:
        Sf[...] = jnp.zeros_like(Sf)
        Sb[...] = jnp.zeros_like(Sb)
    params = (alog_ref, alogT_ref, dtb_ref, dtbT_ref)

    def scan_body(c, carry):
        lf = _ssd_load(0, c, xc, dtg_ref, dtgT_ref, Sf)
        lb = _ssd_load(1, nc - 1 - c, xc, dtg_ref, dtgT_ref, Sb)
        (y_f, s_f), (y_b, s_b) = _ssd_chunks((lf, lb), *params)
        yf[lf[0], :] = y_f
        yb[lb[0], :] = y_b
        Sf[...] = s_f
        Sb[...] = s_b
        return carry
    lax.fori_loop(0, nc, scan_body, 0, unroll=2)
    sf_ref[0] = Sf[...].T
    sb_ref[0] = Sb[...].T

    def out_body(c, carry):
        rows = pl.ds(pl.multiple_of(c * q, q), q)
        ytot = yf[rows, :] + yb[rows, :] + dskip_ref[...] * xc[rows, 0:SSD_INNER].astype(F32)
        yg = ytot * _silu(z_ref[rows, :].astype(F32))
        ms = jnp.mean(yg * yg, axis=-1, keepdims=True)
        y_ref[rows, :] = (yg * lax.rsqrt(ms + EPS) * nrm_ref[...]).astype(y_ref.dtype)
        return carry
    lax.fori_loop(0, nc, out_body, 0)


def _ssd_call(n_seq, ln, blk0, proj, small, smallT3, init, params):
    q = SSD_CHUNK
    hp = SSD_HEADS * SSD_HEAD_DIM
    cw = SSD_INNER + 2 * SSD_GROUPS * SSD_STATE
    nc = ln // q
    assert PJ_XBC % cw == 0 and PJ_Z % SSD_INNER == 0
    tok = lambda w, cb: pl.BlockSpec((ln, w), lambda b: (blk0 + b, cb))
    seq3 = pl.BlockSpec((1, hp, SSD_STATE), lambda b: (b, 0, 0))
    full = lambda a: pl.BlockSpec(a.shape, lambda b: (0,) * a.ndim)
    init = () if init is None else tuple(init)
    return pl.pallas_call(
        functools.partial(_ssd_seq_kernel, has_init=bool(init)), grid=(n_seq,),
        in_specs=[tok(cw, PJ_XBC // cw), tok(SSD_INNER, PJ_Z // SSD_INNER), tok(LANES, 0),
                  pl.BlockSpec((nc, 2 * SSD_HEADS, q), lambda b: (blk0 + b, 0, 0))]
        + [full(a) for a in params] + [seq3] * len(init),
        out_specs=[pl.BlockSpec((ln, hp), lambda b: (b, 0)), seq3, seq3],
        out_shape=[jax.ShapeDtypeStruct((n_seq * ln, hp), BF16),
                   jax.ShapeDtypeStruct((n_seq, hp, SSD_STATE), F32),
                   jax.ShapeDtypeStruct((n_seq, hp, SSD_STATE), F32)],
        scratch_shapes=[pltpu.VMEM((ln, cw), BF16), pltpu.VMEM((q + 2 * CONV_HALO, cw), F32),
                        pltpu.VMEM((SSD_STATE, hp), F32), pltpu.VMEM((SSD_STATE, hp), F32),
                        pltpu.VMEM((ln, hp), F32), pltpu.VMEM((ln, hp), F32)],
        compiler_params=_cp(("arbitrary",)), name="ssd_seq",
    )(proj, proj, small, smallT3, *params, *init)


def _gla_gates(c, q_ref, k_ref, glr_ref, w2_ref, gb_ref, qi, ki, qo, kk, dec):
    t = GLA_BLOCK
    rows = pl.ds(pl.multiple_of(c * t, t), t)
    c0 = 2 * SSD_HEADS
    gps = [_dot_hilo(glr_ref[rows, c0 + GLA_RANK * d:c0 + GLA_RANK * (d + 1)], w2_ref[0, d], w2_ref[1, d])
           + gb_ref[d:d + 1, :] for d in (0, 1)]
    las = [-_softplus(-gp) * (1.0 / GLA_TAU) for gp in gps]
    row = lax.broadcasted_iota(I32, (t, t), 0)
    col = lax.broadcasted_iota(I32, (t, t), 1)
    es = [_dot_sel(col <= row, las[0]), _dot_sel(col < row, las[1])]
    mid = t // 2 - 1
    qf = q_ref[rows, :].astype(F32) * (GLA_KEY_DIM ** -0.5)
    kf = k_ref[rows, :].astype(F32)
    for d in (0, 1):
        e = es[d]
        r = e[mid:mid + 1, :]
        if d == 0:
            tot = e[t - 1:t, :]
            fqi, fki = jnp.exp(e - r), jnp.exp(r - e)
            fq, fk = jnp.exp(e), jnp.exp(tot - e)
        else:
            tot = e[t - 1:t, :] + las[1][t - 1:t, :]
            fqi, fki = jnp.exp(r - e), jnp.exp(e - r)
            fq, fk = jnp.exp(tot - e), jnp.exp(e)
        qi[d, rows, :] = (qf * fqi).astype(BF16)
        ki[d, rows, :] = (kf * fki).astype(BF16)
        qo[d, rows, :] = (qf * fq).astype(BF16)
        kk[d, rows, :] = (kf * fk).astype(BF16)
        dec[d, c] = jnp.broadcast_to(jnp.exp(tot), (8, tot.shape[1]))


def _gla_load(d, c, v_ref, qi, ki, qo, kk, dec, S):
    t = GLA_BLOCK
    rows = pl.ds(pl.multiple_of(c * t, t), t)
    return (rows, qi[d, rows, :], ki[d, rows, :], qo[d, rows, :], kk[d, rows, :], dec[d, c][0:1, :],
            v_ref[rows, :], [S[h] for h in range(GLA_HEADS)])


def _gla_blocks(loaded):
    t = GLA_BLOCK
    dk, dv = GLA_KEY_DIM, GLA_VAL_DIM
    row = lax.broadcasted_iota(I32, (t, t), 0)
    col = lax.broadcasted_iota(I32, (t, t), 1)
    heads = range(GLA_HEADS)
    ks = [slice(dk * h, dk * (h + 1)) for h in heads]
    first = []
    for _, q_in, k_in, q_st, k_st, dec, v, states in loaded:
        vs = [v[:, dv * h:dv * (h + 1)] for h in heads]
        scores = [_dot_nt(q_in[:, ks[h]], k_in[:, ks[h]]) for h in heads]
        carried = [_dot_nt(q_st[:, ks[h]], states[h].astype(BF16)) for h in heads]
        grown = [_dot_tn(vs[h], k_st[:, ks[h]]) for h in heads]
        first.append((vs, scores, carried, grown))
    results = []
    for d, ((_, _, _, _, _, dec, _, states), (vs, scores, carried, grown)) in enumerate(zip(loaded, first)):
        mask = row >= col if d == 0 else col >= row
        outs = [_dot(jnp.where(mask, scores[h], 0.0).astype(BF16), vs[h]) + carried[h] for h in heads]
        new_states = [states[h] * dec[:, ks[h]] + grown[h] for h in heads]
        results.append((jnp.concatenate(outs, axis=1), new_states))
    return results


def _gla_seq_kernel(*refs, has_init):
    q_ref, k_ref, v_ref, og_ref, glr_ref, w2_ref, gb_ref, nrm_ref = refs[:8]
    refs = refs[8:]
    if has_init:
        s0f_ref, s0b_ref = refs[:2]
        refs = refs[2:]
    o_ref, sf_ref, sb_ref, Sf, Sb, of, ob, qi, ki, qo, kk, dec = refs
    ln = q_ref.shape[0]
    t = GLA_BLOCK
    nc = ln // t
    dv = GLA_VAL_DIM
    for h in range(GLA_HEADS):
        if has_init:
            Sf[h] = s0f_ref[0, h].T
            Sb[h] = s0b_ref[0, h].T
        else:
            Sf[h] = jnp.zeros(Sf.shape[1:], F32)
            Sb[h] = jnp.zeros(Sb.shape[1:], F32)
    staged = (qi, ki, qo, kk, dec)

    def gate_body(c, carry):
        _gla_gates(c, q_ref, k_ref, glr_ref, w2_ref, gb_ref, *staged)
        return carry
    lax.fori_loop(0, nc, gate_body, 0, unroll=2)

    def scan_body(c, carry):
        lf = _gla_load(0, c, v_ref, *staged, Sf)
        lb = _gla_load(1, nc - 1 - c, v_ref, *staged, Sb)
        (o_f, s_f), (o_b, s_b) = _gla_blocks((lf, lb))
        of[lf[0], :] = o_f
        ob[lb[0], :] = o_b
        for h in range(GLA_HEADS):
            Sf[h] = s_f[h]
            Sb[h] = s_b[h]
        return carry
    lax.fori_loop(0, nc, scan_body, 0, unroll=2)
    for h in range(GLA_HEADS):
        sf_ref[0, h] = Sf[h].T
        sb_ref[0, h] = Sb[h].T

    def out_body(c, carry):
        rows = pl.ds(pl.multiple_of(c * t, t), t)
        for h in range(GLA_HEADS):
            vl = slice(dv * h, dv * (h + 1))
            ot = of[rows, vl] + ob[rows, vl]
            ms = jnp.mean(ot * ot, axis=-1, keepdims=True)
            on = ot * lax.rsqrt(ms + EPS) * nrm_ref[...]
            o_ref[rows, vl] = (on * _silu(og_ref[rows, vl].astype(F32))).astype(o_ref.dtype)
        return carry
    lax.fori_loop(0, nc, out_body, 0)


def _gla_call(n_seq, ln, blk0, proj, small, init, params):
    qk_w = GLA_HEADS * GLA_KEY_DIM
    v_w = GLA_HEADS * GLA_VAL_DIM
    tok = lambda w, cb: pl.BlockSpec((ln, w), lambda b: (blk0 + b, cb))
    seq4 = pl.BlockSpec((1, GLA_HEADS, GLA_KEY_DIM, GLA_VAL_DIM), lambda b: (b, 0, 0, 0))
    full = lambda a: pl.BlockSpec(a.shape, lambda b: (0,) * a.ndim)
    st_shape = jax.ShapeDtypeStruct((n_seq, GLA_HEADS, GLA_KEY_DIM, GLA_VAL_DIM), F32)
    init = () if init is None else tuple(init)
    return pl.pallas_call(
        functools.partial(_gla_seq_kernel, has_init=bool(init)), grid=(n_seq,),
        in_specs=[tok(qk_w, PJ_Q // qk_w), tok(qk_w, PJ_K // qk_w), tok(v_w, PJ_V // v_w),
                  tok(v_w, PJ_OG // v_w), tok(LANES, 0)] + [full(a) for a in params] + [seq4] * len(init),
        out_specs=[pl.BlockSpec((ln, v_w), lambda b: (b, 0)), seq4, seq4],
        out_shape=[jax.ShapeDtypeStruct((n_seq * ln, v_w), BF16), st_shape, st_shape],
        scratch_shapes=[pltpu.VMEM((GLA_HEADS, GLA_VAL_DIM, GLA_KEY_DIM), F32),
                        pltpu.VMEM((GLA_HEADS, GLA_VAL_DIM, GLA_KEY_DIM), F32),
                        pltpu.VMEM((ln, v_w), F32), pltpu.VMEM((ln, v_w), F32)]
        + [pltpu.VMEM((2, ln, qk_w), BF16)] * 4 + [pltpu.VMEM((2, ln // GLA_BLOCK, 8, qk_w), F32)],
        compiler_params=_cp(("arbitrary",)), name="gla_seq",
    )(proj, proj, proj, proj, small, *params, *init)


def l0_mixers(lay, proj, small, small_t, ssd_f0, ssd_b0, gla_f0, gla_b0, conv_w, conv_b, a_log, dt_bias, d_skip,
              ssd_norm, gate_w2, gate_b, gla_norm):
    hp = SSD_HEADS * SSD_HEAD_DIM
    ssd_p = (conv_w, conv_b.reshape(1, -1), a_log, a_log.T, dt_bias, dt_bias.T,
             jnp.repeat(d_skip, SSD_HEAD_DIM).reshape(1, hp), ssd_norm.reshape(1, hp))
    gla_p = (_hilo(gate_w2), gate_b, gla_norm.reshape(1, -1))
    np_, ns = lay.n_prompt, lay.n_sample
    assert lay.p_tok % lay.sample_len == 0
    groups = [(np_, lay.prompt_len, 0, None, None),
              (ns, lay.sample_len, lay.p_tok // lay.sample_len,
               (ssd_f0.reshape(ns, hp, SSD_STATE), ssd_b0.reshape(ns, hp, SSD_STATE)), (gla_f0, gla_b0))]
    ys, os_, states = [], [], None
    for n, ln, blk0, ssd_init, gla_init in groups:
        y, sf, sb = _ssd_call(n, ln, blk0, proj, small, small_t, ssd_init, ssd_p)
        o, gf, gb = _gla_call(n, ln, blk0, proj, small, gla_init, gla_p)
        ys.append(y)
        os_.append(o)
        if states is None:
            states = (sf, sb, gf, gb)
    return tuple(ys), tuple(os_), states


def _res_kernel(*refs, counts, ks, npt):
    in_prompt = pl.program_id(0) < npt
    streams, pos = [], 0
    for c in counts:
        streams.append(refs[pos:pos + c])
        pos += c
    w_ref, gate_ref, o_ref = refs[pos:]
    acc = None
    off = 0
    for a_refs, k in zip(streams[:-1], ks):
        part = _dot(_stream_tile(a_refs, in_prompt), w_ref[off:off + k, :])
        acc = part if acc is None else acc + part
        off += k
    o_ref[...] = _stream_tile(streams[-1], in_prompt) + gate_ref[...] * acc


def proj_residual(lay, acts, w, x, gate, tm=512):
    arrays, specs, counts = [], [], []
    for s in list(acts) + [x]:
        a, sp = _stream_specs(lay, s, tm)
        arrays += a
        specs += sp
        counts.append(len(a))
    ks = tuple(int((a[0] if isinstance(a, (tuple, list)) else a).shape[1]) for a in acts)
    mrow = lambda i: (lay.mod_row(i * tm), 0, 0)
    return pl.pallas_call(
        functools.partial(_res_kernel, counts=tuple(counts), ks=ks, npt=lay.p_tok // tm),
        grid=(lay.n_tok // tm,),
        in_specs=specs + [pl.BlockSpec(w.shape, lambda i: (0, 0)), pl.BlockSpec((None, 1, D), mrow)],
        out_specs=pl.BlockSpec((tm, D), lambda i: (i, 0)),
        out_shape=jax.ShapeDtypeStruct((lay.n_tok, D), F32),
        compiler_params=_cp(("arbitrary",)), name="proj_residual",
    )(*arrays, w, gate)


def _router_kernel(x_ref, g_ref, sc_ref, sh_ref, rw_ref, rb_ref,
                   h_ref, idx_ref, gate_ref, pos_ref, posT_ref, cnt_ref):
    tm = x_ref.shape[0]
    h = _modnorm(x_ref[...], g_ref[...], sc_ref[...], sh_ref[...])
    h_hi = h.astype(BF16)
    h_ref[...] = h_hi
    h_lo = (h - h_hi.astype(F32)).astype(BF16)
    lg = (_dot(h_hi, rw_ref[0]) + _dot(h_lo, rw_ref[0]) + _dot(h_hi, rw_ref[1])
          + rb_ref[...])
    lane = lax.broadcasted_iota(I32, (tm, LANES), 1).astype(F32)
    vals, ids = [], []
    for _ in range(TOP_K):
        m = jnp.max(lg, axis=1, keepdims=True)
        i = jnp.min(jnp.where(lg == m, lane, float(LANES)), axis=1, keepdims=True)
        vals.append(m)
        ids.append(i)
        lg = jnp.where(lane == i, -jnp.inf, lg)
    ex = [jnp.exp(v - vals[0]) for v in vals]
    den = ex[0] + ex[1] + ex[2] + ex[3]
    sel = jnp.zeros((tm, LANES), F32)
    for i in ids:
        sel = sel + (lane == i).astype(F32)
    row = lax.broadcasted_iota(I32, (tm, tm), 0)
    col = lax.broadcasted_iota(I32, (tm, tm), 1)
    before = _dot((col < row).astype(BF16), sel.astype(BF16))
    n = jnp.sum(sel, axis=0, keepdims=True)
    er = lax.broadcasted_iota(I32, (LANES, LANES), 0)
    ec = lax.broadcasted_iota(I32, (LANES, LANES), 1)
    n_al = jnp.ceil(n * (1.0 / SEG_ALIGN)) * SEG_ALIGN
    offs = _dot(jnp.broadcast_to(n_al, (8, LANES)).astype(BF16), (er < ec).astype(BF16))[0:1, :]
    slot = before + offs
    idx_o = jnp.zeros((tm, LANES), F32)
    gate_o = jnp.zeros((tm, LANES), F32)
    pos_o = jnp.zeros((tm, LANES), F32)
    for k in range(TOP_K):
        p = jnp.sum(jnp.where(lane == ids[k], slot, 0.0), axis=1, keepdims=True)
        idx_o = jnp.where(lane == k, ids[k], idx_o)
        gate_o = jnp.where(lane == k, ex[k] / den, gate_o)
        pos_o = jnp.where(lane == k, p, pos_o)
    idx_ref[...] = idx_o.astype(I32)
    gate_ref[...] = gate_o
    pos_ref[...] = pos_o.astype(I32)
    posT_ref[...] = pos_o.T[0:8, :]
    cnt_ref[0] = jnp.broadcast_to(n, (8, LANES))


def moe_router(lay, x, g, sc, sh, rw, rb):
    n_tok = x.shape[0]
    tm = TOK_TILE
    nt = n_tok // tm
    mrow = lambda i: (lay.mod_row(i * tm), 0, 0)
    tile = lambda w, dt: (pl.BlockSpec((tm, w), lambda i: (i, 0)), jax.ShapeDtypeStruct((n_tok, w), dt))
    outs = [tile(D, BF16), tile(LANES, I32), tile(LANES, F32), tile(LANES, I32),
            (pl.BlockSpec((8, tm), lambda i: (0, i)), jax.ShapeDtypeStruct((8, n_tok), F32)),
            (pl.BlockSpec((1, 8, LANES), lambda i: (i, 0, 0)), jax.ShapeDtypeStruct((nt, 8, LANES), F32))]
    return pl.pallas_call(
        _router_kernel, grid=(nt,),
        in_specs=[pl.BlockSpec((tm, D), lambda i: (i, 0)),
                  pl.BlockSpec((1, D), lambda i: (0, 0)),
                  pl.BlockSpec((None, 1, D), mrow), pl.BlockSpec((None, 1, D), mrow),
                  pl.BlockSpec((2, D, LANES), lambda i: (0, 0, 0)),
                  pl.BlockSpec((1, LANES), lambda i: (0, 0))],
        out_specs=[o[0] for o in outs], out_shape=[o[1] for o in outs],
        compiler_params=_cp(("arbitrary",)), name="moe_router",
    )(x, g.reshape(1, D), sc, sh, rw, rb)


SEG_ALIGN = 8
SEG_CHUNK = 16
REST_BITS = tuple(range(int(math.log2(SEG_CHUNK)) - 1, int(math.log2(SEG_ALIGN)) - 1, -1))
TILE_ROWS = TOK_TILE * TOP_K + N_EXPERTS * SEG_ALIGN


def _pow2_copies(n, src, dst, make_copy, op, bits):
    for b in bits:
        sz = 1 << b
        done = (n >> (b + 1)) << (b + 1)

        @pl.when((n & sz) != 0)
        def _():
            op(make_copy(pl.multiple_of(src + done, SEG_ALIGN), pl.multiple_of(dst + done, SEG_ALIGN), sz))


def _start_segments(i, n_ref, off_ref, dst_ref, make_copy):
    def body(e, carry):
        k = i * N_EXPERTS + e
        n, src, dst = n_ref[k], off_ref[k], dst_ref[k]

        def chunk(j, c):
            o = pl.multiple_of(j * SEG_CHUNK, SEG_CHUNK)
            make_copy(pl.multiple_of(src + o, SEG_ALIGN), pl.multiple_of(dst + o, SEG_ALIGN),
                      SEG_CHUNK).start(priority=1)
            return c
        shift = int(math.log2(SEG_CHUNK))
        full = n >> shift
        lax.fori_loop(0, full, chunk, 0)
        done = full << shift
        _pow2_copies(n - done, src + done, dst + done, make_copy, lambda c: c.start(), REST_BITS)
        return carry
    lax.fori_loop(0, N_EXPERTS, body, 0)


TAIL_BITS = tuple(range(int(math.log2(MOE_BLOCK)) - 1, int(math.log2(SEG_ALIGN)) - 1, -1))
TILE_BITS = tuple(range(int(math.log2(TILE_ROWS)), int(math.log2(SEG_ALIGN)) - 1, -1))


def _wait_rows(total, make_copy):
    _pow2_copies(total, 0, 0, make_copy, lambda c: c.wait(), TILE_BITS)


def _dispatch_kernel(n_ref, off_ref, dst_ref, tot_ref, tn_ref, td_ref, posT_ref, h_ref, xout_ref,
                     srt, zbuf, sems):
    i = pl.program_id(0)
    last = pl.num_programs(0) - 1
    slot = i % 2
    tm = h_ref.shape[0]
    r = lax.broadcasted_iota(I32, (TILE_ROWS, tm), 0)
    hit = jnp.zeros((TILE_ROWS, tm), jnp.bool_)
    for k in range(TOP_K):
        hit = hit | (r == posT_ref[k:k + 1, :].astype(I32))
    sel = jnp.where(hit, 1.0, 0.0).astype(BF16)
    srt[slot] = _pack_rows(_dot(sel, h_ref[...]))

    def copier(s):
        def make_copy(src, dst, sz):
            return pltpu.make_async_copy(srt.at[s, pl.ds(src, sz)], xout_ref.at[pl.ds(dst, sz)], sems.at[s])
        return make_copy

    _start_segments(i, n_ref, off_ref, dst_ref, copier(slot))

    @pl.when(i > 0)
    def _():
        _wait_rows(tot_ref[jnp.maximum(i - 1, 0)], copier(1 - slot))

    @pl.when(i == last)
    def _():
        _wait_rows(tot_ref[i], copier(slot))
        zbuf[...] = jnp.zeros_like(zbuf)
        sem = sems.at[0]

        def zero_copy(src, dst, sz):
            return pltpu.make_async_copy(zbuf.at[pl.ds(src, sz)], xout_ref.at[pl.ds(dst, sz)], sem)

        nb = xout_ref.shape[0] // MOE_BLOCK
        for op in (lambda c: c.start(), lambda c: c.wait()):
            def body(e, carry):
                _pow2_copies(tn_ref[e], 0, td_ref[e], zero_copy, op, TAIL_BITS)
                return carry
            lax.fori_loop(0, N_EXPERTS, body, 0)

            def unused(b, carry):
                op(zero_copy(0, pl.multiple_of(b * MOE_BLOCK, MOE_BLOCK), MOE_BLOCK))
                return carry
            lax.fori_loop(tn_ref[N_EXPERTS], nb, unused, 0)


def moe_dispatch(n_tab, off_tab, dst_tab, tot_tab, tail_n, tail_dst, posT, h2, n_rows):
    n_tok = h2.shape[0]
    tm = TOK_TILE
    grid_spec = pltpu.PrefetchScalarGridSpec(
        num_scalar_prefetch=6, grid=(n_tok // tm,),
        in_specs=[pl.BlockSpec((8, tm), lambda i, *_: (0, i)),
                  pl.BlockSpec((tm, D), lambda i, *_: (i, 0))],
        out_specs=pl.BlockSpec(memory_space=pl.ANY),
        scratch_shapes=[pltpu.VMEM((2, TILE_ROWS, ROW_WORDS), U32), pltpu.VMEM((MOE_BLOCK, ROW_WORDS), U32),
                        pltpu.SemaphoreType.DMA((2,))])
    return pl.pallas_call(
        _dispatch_kernel, grid_spec=grid_spec,
        out_shape=jax.ShapeDtypeStruct((n_rows, ROW_WORDS), U32),
        compiler_params=_cp(("arbitrary",)), name="moe_dispatch",
    )(n_tab, off_tab, dst_tab, tot_tab, tail_n, tail_dst, posT, h2)


def _combine_kernel(n_ref, off_ref, dst_ref, tot_ref, pos_ref, gate_ref, x_ref, g2_ref, y_ref, *rest, npt):
    o_refs, (buf, sems) = rest[:-2], rest[-2:]
    i = pl.program_id(0)
    last = pl.num_programs(0) - 1
    slot = i % 2
    tm = x_ref.shape[0]
    na = TILE_ROWS

    def copier(s):
        def make_copy(src, dst, sz):
            return pltpu.make_async_copy(y_ref.at[pl.ds(dst, sz)], buf.at[s, pl.ds(src, sz)], sems.at[s])
        return make_copy

    def fetch(tile, s):
        buf[s, tm * TOP_K:na, :] = jnp.zeros((na - tm * TOP_K, ROW_WORDS), U32)
        _start_segments(tile, n_ref, off_ref, dst_ref, copier(s))

    @pl.when(i == 0)
    def _():
        fetch(i, slot)

    @pl.when(i < last)
    def _():
        fetch(i + 1, 1 - slot)

    _wait_rows(tot_ref[i], copier(slot))
    lane = lax.broadcasted_iota(I32, (tm, na), 1)
    pw = jnp.zeros((tm, na), F32)
    for k in range(TOP_K):
        pw = pw + jnp.where(lane == pos_ref[:, k:k + 1], gate_ref[:, k:k + 1], 0.0)
    phi = pw.astype(BF16)
    plo = (pw - phi.astype(F32)).astype(BF16)
    yb = _unpack_rows(buf[slot]).astype(BF16)
    res = x_ref[...] + g2_ref[...] * (_dot(phi, yb) + _dot(plo, yb))
    if len(o_refs) == 1:
        o_refs[0][...] = res
    else:
        @pl.when(i < npt)
        def _():
            o_refs[0][...] = res

        @pl.when(i >= npt)
        def _():
            o_refs[1][...] = res


def moe_combine(lay, n_tab, off_tab, dst_tab, tot_tab, pos, gates, x, gate2, y_rows, split):
    n_tok = x.shape[0]
    tm = TOK_TILE
    npt = lay.p_tok // tm
    mrow = lambda i, *_: (lay.mod_row(i * tm), 0, 0)
    if split:
        out_specs = [pl.BlockSpec((tm, D), lambda i, *_: (jnp.minimum(i, npt - 1), 0)),
                     pl.BlockSpec((tm, D), lambda i, *_: (jnp.maximum(i - npt, 0), 0))]
        out_shape = [jax.ShapeDtypeStruct((lay.p_tok, D), F32), jax.ShapeDtypeStruct((n_tok - lay.p_tok, D), F32)]
    else:
        out_specs = pl.BlockSpec((tm, D), lambda i, *_: (i, 0))
        out_shape = jax.ShapeDtypeStruct((n_tok, D), F32)
    grid_spec = pltpu.PrefetchScalarGridSpec(
        num_scalar_prefetch=4, grid=(n_tok // tm,),
        in_specs=[pl.BlockSpec((tm, LANES), lambda i, *_: (i, 0)),
                  pl.BlockSpec((tm, LANES), lambda i, *_: (i, 0)),
                  pl.BlockSpec((tm, D), lambda i, *_: (i, 0)),
                  pl.BlockSpec((None, 1, D), mrow),
                  pl.BlockSpec(memory_space=pl.ANY)],
        out_specs=out_specs,
        scratch_shapes=[pltpu.VMEM((2, TILE_ROWS, ROW_WORDS), U32), pltpu.SemaphoreType.DMA((2,))])
    return pl.pallas_call(
        functools.partial(_combine_kernel, npt=npt), grid_spec=grid_spec, out_shape=out_shape,
        compiler_params=_cp(("arbitrary",)), name="moe_combine",
    )(n_tab, off_tab, dst_tab, tot_tab, pos, gates, x, gate2, y_rows)


def _expert_kernel(be_ref, nv_ref, nxt_ref, slot_ref, x_ref, b_ref, wg_hbm, wu_hbm, wd_hbm,
                   y_ref, wf, sems, *, layer):
    i = pl.program_id(0)
    valid = i < nv_ref[0]
    e = be_ref[i]
    slot = slot_ref[e]
    changed = jnp.logical_or(i == 0, e != be_ref[jnp.maximum(i - 1, 0)])

    def weight_copies(ex, s):
        return [pltpu.make_async_copy(w.at[layer, ex], wf.at[s, k], sems.at[s, k])
                for k, w in enumerate((wg_hbm, wu_hbm, wd_hbm))]

    @pl.when(jnp.logical_and(valid, changed))
    def _():
        @pl.when(i == 0)
        def _():
            for c in weight_copies(e, slot):
                c.start()

        nxt = nxt_ref[e]

        @pl.when(nxt >= 0)
        def _():
            for c in weight_copies(nxt, 1 - slot):
                c.start(priority=1)

        for c in weight_copies(e, slot):
            c.wait()

    @pl.when(valid)
    def _():
        x = _unpack_rows(x_ref[...])
        b = b_ref[e]
        gt = jnp.minimum(_dot(x, wf[slot, 0]) + b[0:1, :], SWIGLU_LIMIT)
        up = jnp.clip(_dot(x, wf[slot, 1]) + b[1:2, :], -SWIGLU_LIMIT, SWIGLU_LIMIT)
        act = (up + 1.0) * gt * _sigmoid(SWIGLU_ALPHA * gt)
        y = _dot(act, wf[slot, 2]) + b[2:3, :]
        y_ref[...] = _pack_rows(y.astype(BF16).astype(F32))

    @pl.when(jnp.logical_not(valid))
    def _():
        y_ref[...] = jnp.zeros_like(y_ref)


def moe_experts(layer, blk_expert, n_valid, next_expert, slot, x_rows, w_gate, b_gate, w_up, b_up, w_down,
                b_down):
    n_rows = x_rows.shape[0]
    nb = n_rows // MOE_BLOCK
    depth, ne, _, ff = w_gate.shape
    assert ff == D
    rowblk = lambda i, be, nv, *_: (jnp.maximum(jnp.minimum(i, nv[0] - 1), 0), 0)
    hbm = pl.BlockSpec(memory_space=pl.ANY)
    biases = jnp.stack([b_gate, b_up, b_down], axis=2)
    grid_spec = pltpu.PrefetchScalarGridSpec(
        num_scalar_prefetch=4, grid=(nb,),
        in_specs=[pl.BlockSpec((MOE_BLOCK, ROW_WORDS), rowblk),
                  pl.BlockSpec((None, ne, 3, D), lambda i, *_: (layer, 0, 0, 0)), hbm, hbm, hbm],
        out_specs=pl.BlockSpec((MOE_BLOCK, ROW_WORDS), lambda i, *_: (i, 0)),
        scratch_shapes=[pltpu.VMEM((2, 3, D, ff), F32), pltpu.SemaphoreType.DMA((2, 3))])
    return pl.pallas_call(
        functools.partial(_expert_kernel, layer=layer), grid_spec=grid_spec,
        out_shape=jax.ShapeDtypeStruct((n_rows, ROW_WORDS), U32),
        compiler_params=_cp(("arbitrary",)), name="moe_experts",
    )(blk_expert, n_valid, next_expert, slot, x_rows, biases, w_gate, w_up, w_down)


def moe_layer(lay, layer, x, g2, sc2, sh2, gate2, router_w, router_b, w_gate, b_gate, w_up, b_up, w_down,
              b_down, split=False):
    n_tok = x.shape[0]
    nt = n_tok // TOK_TILE
    rw = jnp.zeros((D, LANES), F32).at[:, :N_EXPERTS].set(router_w)
    rw = _hilo(rw)
    rb = jnp.full((1, LANES), NEG, F32).at[0, :N_EXPERTS].set(router_b)
    h2, _, gates, pos, posT, cnt = moe_router(lay, x, g2, sc2, sh2, rw, rb)
    n_te = cnt[:, 0, :N_EXPERTS].astype(I32)
    n_te = (n_te + SEG_ALIGN - 1) // SEG_ALIGN * SEG_ALIGN
    totals = jnp.sum(n_te, axis=0)
    padded = (totals + MOE_BLOCK - 1) // MOE_BLOCK * MOE_BLOCK
    padded_end = jnp.cumsum(padded)
    pstart = padded_end - padded
    dst = pstart[None, :] + jnp.cumsum(n_te, axis=0) - n_te
    off = jnp.cumsum(n_te, axis=1) - n_te
    n_rows = nt * TILE_ROWS + N_EXPERTS * MOE_BLOCK
    nb = n_rows // MOE_BLOCK
    n_valid = (padded_end[-1] // MOE_BLOCK).astype(I32).reshape(1)
    bstart = jnp.minimum(jnp.arange(nb, dtype=I32), n_valid[0] - 1) * MOE_BLOCK
    blk_expert = jnp.minimum(jnp.sum((bstart[:, None] >= padded_end[None, :]).astype(I32), axis=1),
                             N_EXPERTS - 1).astype(I32)
    tabs = (n_te.reshape(-1).astype(I32), off.reshape(-1).astype(I32), dst.reshape(-1).astype(I32),
            jnp.sum(n_te, axis=1).astype(I32))
    tail_n = jnp.concatenate([(padded - totals).astype(I32), n_valid])
    x_rows = moe_dispatch(*tabs, tail_n, (pstart + totals).astype(I32), posT, h2, n_rows)
    owner = jnp.where(padded > 0, jnp.arange(N_EXPERTS, dtype=I32), N_EXPERTS)
    later = jnp.concatenate([lax.cummin(owner, axis=0, reverse=True)[1:], jnp.full((1,), N_EXPERTS, I32)])
    next_expert = jnp.where(later < N_EXPERTS, later, -1).astype(I32)
    slot = ((jnp.cumsum((padded > 0).astype(I32)) - 1) % 2).astype(I32)
    y_rows = moe_experts(layer, blk_expert, n_valid, next_expert, slot, x_rows, w_gate, b_gate, w_up, b_up,
                         w_down, b_down)
    return moe_combine(lay, *tabs, pos, gates, x, gate2, y_rows, split)


QKV_TN = 256
N_QK_TILES = (ATT_HEADS + ATT_KV) * ATT_HD // QKV_TN


def _qkv_kernel(x_ref, g_ref, sc_ref, sh_ref, w_ref, nw_ref, cos_ref, sin_ref, o_ref):
    tm = x_ref.shape[0]
    h = _modnorm(x_ref[...], g_ref[...], sc_ref[...], sh_ref[...]).astype(BF16)
    r = lax.broadcasted_iota(I32, (QKV_TN, QKV_TN), 0) // ATT_HD
    c = lax.broadcasted_iota(I32, (QKV_TN, QKV_TN), 1) // ATT_HD
    head_mean = jnp.where(r == c, 1.0 / ATT_HD, 0.0).astype(BF16)
    lane = lax.broadcasted_iota(I32, (tm, QKV_TN), 1)
    half = ATT_HD // 4
    first = (lane % (2 * half)) < half
    acc_all = _dot(h, w_ref[...])
    tiles = [acc_all[:, QKV_TN * j:QKV_TN * (j + 1)] for j in range(w_ref.shape[1] // QKV_TN)]
    sq = jnp.concatenate([(t * t).astype(BF16) for t in tiles[:N_QK_TILES]], axis=0)
    ms_all = _dot(sq, head_mean)
    for j, acc in enumerate(tiles):
        cols = slice(QKV_TN * j, QKV_TN * (j + 1))
        if j >= N_QK_TILES:
            o_ref[:, cols] = acc
            continue
        qn = acc * lax.rsqrt(ms_all[tm * j:tm * (j + 1)] + EPS) * nw_ref[j]
        swapped = jnp.where(first, pltpu.roll(qn, QKV_TN - half, 1), pltpu.roll(qn, half, 1))
        o_ref[:, cols] = qn * cos_ref[...] + swapped * sin_ref[...]


def _rope_tables(sample_len):
    pos = np.arange(sample_len)
    half = ATT_HD // 4
    inv = (ROPE_THETA ** (-np.arange(half, dtype=np.float32) / half)).astype(np.float32)
    ang_r = (pos // GRID_W).astype(np.float32)[:, None] * inv[None, :]
    ang_c = (pos % GRID_W).astype(np.float32)[:, None] * inv[None, :]
    cos = np.concatenate([np.cos(ang_r)] * 2 + [np.cos(ang_c)] * 2, axis=1)
    sin = np.concatenate([-np.sin(ang_r), np.sin(ang_r), -np.sin(ang_c), np.sin(ang_c)], axis=1)
    rep = QKV_TN // ATT_HD
    return (jnp.asarray(np.tile(cos, (1, rep)), F32), jnp.asarray(np.tile(sin, (1, rep)), F32))


def qkv_proj(lay, x, g, sc, sh, w, q_norm, k_norm, tm=512):
    n_tok = x.shape[0]
    n = w.shape[1]
    nq = ATT_HEADS * ATT_HD // QKV_TN
    rep = QKV_TN // ATT_HD
    nw = jnp.concatenate([jnp.tile(jnp.tile(q_norm, rep)[None, :], (nq, 1)),
                          jnp.tile(jnp.tile(k_norm, rep)[None, :], (n // QKV_TN - nq, 1))], axis=0)
    cos, sin = _rope_tables(lay.sample_len)
    cos = jnp.concatenate([jnp.ones((tm, QKV_TN), F32), cos], axis=0)
    sin = jnp.concatenate([jnp.zeros((tm, QKV_TN), F32), sin], axis=0)
    assert lay.p_tok % tm == 0 and lay.sample_len % tm == 0
    mrow = lambda i: (lay.mod_row(i * tm), 0, 0)
    rrow = lambda i: (jnp.where(i * tm < lay.p_tok, 0, 1 + ((i * tm - lay.p_tok) % lay.sample_len) // tm), 0)
    nt = n // QKV_TN
    return pl.pallas_call(
        _qkv_kernel, grid=(n_tok // tm,),
        in_specs=[pl.BlockSpec((tm, D), lambda i: (i, 0)),
                  pl.BlockSpec((1, D), lambda i: (0, 0)),
                  pl.BlockSpec((None, 1, D), mrow), pl.BlockSpec((None, 1, D), mrow),
                  pl.BlockSpec((D, n), lambda i: (0, 0)),
                  pl.BlockSpec((nt, 1, QKV_TN), lambda i: (0, 0, 0)),
                  pl.BlockSpec((tm, QKV_TN), rrow), pl.BlockSpec((tm, QKV_TN), rrow)],
        out_specs=pl.BlockSpec((tm, n), lambda i: (i, 0)),
        out_shape=jax.ShapeDtypeStruct((n_tok, n), F32),
        compiler_params=_cp(("arbitrary",)), name="qkv_proj",
    )(x, g.reshape(1, D), sc, sh, w, nw.reshape(nt, 1, QKV_TN), cos, sin)


def _dup_group(x, g):
    blk = x[:, LANES * (g // 2):LANES * (g // 2 + 1)]
    if g % 2 == 1:
        blk = pltpu.roll(blk, ATT_HD, 1)
    lo = lax.broadcasted_iota(I32, blk.shape, 1) < ATT_HD
    low = jnp.where(lo, blk, 0.0)
    return low + pltpu.roll(low, ATT_HD, 1)


def _attend(q_ref, k_all, v_all, mask, sink_ref, o_ref):
    nq = q_ref.shape[0]
    lo = lax.broadcasted_iota(I32, (nq, LANES), 1) < ATT_HD
    first = lax.broadcasted_iota(I32, (2 * nq, 1), 0) < nq
    if mask is not None:
        mask = jnp.concatenate([mask, mask], axis=0)
    pairs_per_group = ATT_HEADS // ATT_KV // 2
    n_pairs = ATT_HEADS // 2
    kv = {}

    def group_kv(g):
        if g not in kv:
            kv[g] = (_dup_group(k_all, g).astype(BF16), _dup_group(v_all, g).astype(BF16))
        return kv[g]

    def scores(j):
        qp = q_ref[:, LANES * j:LANES * (j + 1)] * (ATT_HD ** -0.5)
        qs = jnp.concatenate([jnp.where(lo, qp, 0.0), jnp.where(lo, 0.0, qp)], axis=0)
        return _dot_nt(qs.astype(BF16), group_kv(j // pairs_per_group)[0])

    s_next = scores(0)
    for j in range(n_pairs):
        s = s_next
        if j + 1 < n_pairs:
            s_next = scores(j + 1)
        if mask is not None:
            s = jnp.where(mask, s, NEG)
        sink = jnp.where(first, sink_ref[2 * j], sink_ref[2 * j + 1])
        m = jnp.maximum(jnp.max(s, axis=1, keepdims=True), sink)
        p = jnp.exp(s - m)
        den = jnp.sum(p, axis=1, keepdims=True) + jnp.exp(sink - m)
        o = _dot(p.astype(BF16), group_kv(j // pairs_per_group)[1]) / den
        o_ref[:, LANES * j:LANES * (j + 1)] = jnp.where(lo, o[:nq], o[nq:]).astype(o_ref.dtype)


def _attn_ctx_kernel(sink_ref, q_ref, k_ref, v_ref, o_ref):
    _attend(q_ref, k_ref[...], v_ref[...], None, sink_ref, o_ref)


def attn_context(lay, qkv, sinks):
    qw = ATT_HEADS * ATT_HD
    kw = ATT_KV * ATT_HD
    ln = lay.prompt_len
    grid_spec = pltpu.PrefetchScalarGridSpec(
        num_scalar_prefetch=0, grid=(lay.n_prompt,),
        in_specs=[pl.BlockSpec(memory_space=pltpu.SMEM),
                  pl.BlockSpec((ln, qw), lambda b: (b, 0)),
                  pl.BlockSpec((ln, kw), lambda b: (b, qw // kw)),
                  pl.BlockSpec((ln, kw), lambda b: (b, qw // kw + 1))],
        out_specs=pl.BlockSpec((ln, qw), lambda b: (b, 0)))
    return pl.pallas_call(
        _attn_ctx_kernel, grid_spec=grid_spec,
        out_shape=jax.ShapeDtypeStruct((lay.p_tok, qw), BF16),
        compiler_params=_cp(("arbitrary",)), name="attn_context",
    )(sinks, qkv, qkv, qkv)


def _attn_lat_kernel(sink_ref, q_ref, kp_ref, kc_ref, kn_ref, vp_ref, vc_ref, vn_ref, ck_ref, cv_ref, o_ref,
                     *, nblk):
    i = pl.program_id(1)
    bq = ATT_BLOCK
    nctx = ck_ref.shape[1]
    k_all = jnp.concatenate([kp_ref[...], kc_ref[...], kn_ref[...], ck_ref[0]], axis=0)
    v_all = jnp.concatenate([vp_ref[...], vc_ref[...], vn_ref[...], cv_ref[0]], axis=0)
    ns = 3 * bq + nctx
    r = lax.broadcasted_iota(I32, (bq, ns), 0)
    c = lax.broadcasted_iota(I32, (bq, ns), 1)
    rel = c - r
    first_key = jnp.where(i > 0, 0, bq)
    end_key = jnp.where(i < nblk - 1, 3 * bq, 2 * bq)
    band = (rel >= bq - WINDOW) & (rel <= bq + WINDOW) & (c >= first_key) & (c < end_key)
    mask = band | (c >= 3 * bq)
    _attend(q_ref, k_all, v_all, mask, sink_ref, o_ref)


def attn_latent(lay, qkv, cache_k, cache_v, sinks):
    qw = ATT_HEADS * ATT_HD
    kw = ATT_KV * ATT_HD
    bq = ATT_BLOCK
    nblk = lay.sample_len // bq
    b0 = lay.p_tok // bq
    nctx = cache_k.shape[1]
    rb = lambda b, i: b0 + b * nblk + i
    kspec = lambda cb, sh: pl.BlockSpec(
        (bq, kw), lambda b, i: (b0 + b * nblk + jnp.clip(i + sh, 0, nblk - 1), cb))
    kc, vc = qw // kw, qw // kw + 1
    grid_spec = pltpu.PrefetchScalarGridSpec(
        num_scalar_prefetch=0, grid=(lay.n_sample, nblk),
        in_specs=[pl.BlockSpec(memory_space=pltpu.SMEM),
                  pl.BlockSpec((bq, qw), lambda b, i: (rb(b, i), 0)),
                  kspec(kc, -1), kspec(kc, 0), kspec(kc, 1),
                  kspec(vc, -1), kspec(vc, 0), kspec(vc, 1),
                  pl.BlockSpec((1, nctx, kw), lambda b, i: (b, 0, 0)),
                  pl.BlockSpec((1, nctx, kw), lambda b, i: (b, 0, 0))],
        out_specs=pl.BlockSpec((bq, qw), lambda b, i: (b * nblk + i, 0)))
    return pl.pallas_call(
        functools.partial(_attn_lat_kernel, nblk=nblk), grid_spec=grid_spec,
        out_shape=jax.ShapeDtypeStruct((lay.n_sample * lay.sample_len, qw), BF16),
        compiler_params=_cp(("arbitrary", "arbitrary")), name="attn_latent",
    )(sinks, qkv, qkv, qkv, qkv, qkv, qkv, qkv,
      cache_k.reshape(lay.n_sample, nctx, kw), cache_v.reshape(lay.n_sample, nctx, kw))


def _forward(lay, x_prompt, x_sample, state_l0_ssd_fwd, state_l0_ssd_bwd, state_l0_gla_fwd, state_l0_gla_bwd,
             cache_l1_k, cache_l1_v, c, c_ctx, ada_w, ada_b, norm1, norm2,
             l0_w_in, l0_conv_w, l0_conv_b, l0_a_log, l0_dt_bias, l0_d_skip, l0_ssd_norm,
             l0_gate_w2, l0_gate_b, l0_gla_norm, l0_w_out,
             l1_w_qkv, l1_q_norm, l1_k_norm, l1_sinks, l1_w_out,
             router_w, router_b, exp_w_gate, exp_b_gate, exp_w_up, exp_b_up, exp_w_down, exp_b_down):
    np_, ns = lay.n_prompt, lay.n_sample
    x = (x_prompt.reshape(-1, D), x_sample.reshape(-1, D))
    cond8 = jnp.zeros((8, D), F32).at[0].set(c_ctx).at[1:1 + ns].set(c)
    mod = ada_table(cond8, ada_w, ada_b)
    mods = [[mod[l, :, p * D:(p + 1) * D].reshape(8, 1, D) for p in range(N_ADA)] for l in range(2)]

    def moe(l, xx, split=False):
        return moe_layer(lay, l, xx, norm2[l], mods[l][4], mods[l][3], mods[l][5], router_w[l], router_b[l],
                         exp_w_gate, exp_b_gate, exp_w_up, exp_b_up, exp_w_down, exp_b_down, split=split)

    sp = np.cumsum((SSD_INNER, SSD_INNER + 2 * SSD_GROUPS * SSD_STATE, 2 * SSD_HEADS,
                    GLA_HEADS * GLA_KEY_DIM, GLA_HEADS * GLA_KEY_DIM,
                    GLA_HEADS * GLA_VAL_DIM, GLA_HEADS * GLA_VAL_DIM, 2 * GLA_RANK))
    cols = lambda a, b: l0_w_in[:, a:b]
    w_main = jnp.concatenate([cols(0, sp[0]), cols(sp[4], sp[5]), cols(sp[5], sp[6]), cols(sp[0], sp[1]),
                              cols(sp[2], sp[3]), cols(sp[3], sp[4])], axis=1).astype(BF16)
    w_small = jnp.concatenate([cols(sp[1], sp[2]), cols(sp[6], sp[7]),
                               jnp.zeros((D, LANES - 2 * SSD_HEADS - 2 * GLA_RANK), F32)], axis=1)
    proj, small, small_t = norm_proj(lay, x, norm1[0], mods[0][1], mods[0][0], w_main, _hilo(w_small), 512,
                                      PJ_W // 2, BF16, SSD_CHUNK)
    y_n, o_n, (ssd_f, ssd_b, gla_f, gla_b) = l0_mixers(
        lay, proj, small, small_t, state_l0_ssd_fwd, state_l0_ssd_bwd, state_l0_gla_fwd, state_l0_gla_bwd,
        l0_conv_w, l0_conv_b, l0_a_log, l0_dt_bias, l0_d_skip, l0_ssd_norm,
        l0_gate_w2, l0_gate_b, l0_gla_norm)
    x = proj_residual(lay, [y_n, o_n], l0_w_out.astype(BF16), x, mods[0][2])
    x = moe(0, x)

    qkv = qkv_proj(lay, x, norm1[1], mods[1][1], mods[1][0], l1_w_qkv.astype(BF16), l1_q_norm, l1_k_norm)
    o_ctx = attn_context(lay, qkv, l1_sinks)
    o_lat = attn_latent(lay, qkv, cache_l1_k, cache_l1_v, l1_sinks)
    x = proj_residual(lay, [(o_ctx, o_lat)], l1_w_out.astype(BF16), x, mods[1][2])
    xp, xs = moe(1, x, split=True)

    qw = ATT_HEADS * ATT_HD
    kw = ATT_KV * ATT_HD
    return (xp.reshape(x_prompt.shape), xs.reshape(x_sample.shape),
            ssd_f[:np_].reshape(np_, SSD_HEADS, SSD_HEAD_DIM, SSD_STATE),
            ssd_b[:np_].reshape(np_, SSD_HEADS, SSD_HEAD_DIM, SSD_STATE),
            gla_f[:np_], gla_b[:np_],
            qkv[:lay.p_tok, qw:qw + kw].reshape(np_, lay.prompt_len, ATT_KV, ATT_HD),
            qkv[:lay.p_tok, qw + kw:].reshape(np_, lay.prompt_len, ATT_KV, ATT_HD))


def kernel(x_prompt, x_sample, state_l0_ssd_fwd, state_l0_ssd_bwd, state_l0_gla_fwd, state_l0_gla_bwd, cache_l1_k, cache_l1_v, c, c_ctx, ada_w, ada_b, norm1, norm2, l0_w_in, l0_conv_w, l0_conv_b, l0_a_log, l0_dt_bias, l0_d_skip, l0_ssd_norm, l0_gate_w2, l0_gate_b, l0_gla_norm, l0_w_out, l1_w_qkv, l1_q_norm, l1_k_norm, l1_sinks, l1_w_out, router_w, router_b, exp_w_gate, exp_b_gate, exp_w_up, exp_b_up, exp_w_down, exp_b_down):
    lay = Layout(x_prompt.shape[0], x_prompt.shape[1], x_sample.shape[0], x_sample.shape[1])
    return _forward(lay, x_prompt, x_sample, state_l0_ssd_fwd, state_l0_ssd_bwd, state_l0_gla_fwd,
                    state_l0_gla_bwd, cache_l1_k, cache_l1_v, c, c_ctx, ada_w, ada_b, norm1, norm2,
                    l0_w_in, l0_conv_w, l0_conv_b, l0_a_log, l0_dt_bias, l0_d_skip, l0_ssd_norm,
                    l0_gate_w2, l0_gate_b, l0_gla_norm, l0_w_out,
                    l1_w_qkv, l1_q_norm, l1_k_norm, l1_sinks, l1_w_out,
                    router_w, router_b, exp_w_gate, exp_b_gate, exp_w_up, exp_b_up, exp_w_down, exp_b_down)
```

```python
import functools
import math

import numpy as np
import jax
import jax.numpy as jnp
from jax import lax
from jax.experimental import pallas as pl
from jax.experimental.pallas import tpu as pltpu

F32 = jnp.float32
BF16 = jnp.bfloat16
I32 = jnp.int32
HI = lax.Precision.HIGHEST

D = 1024
EPS = 1e-6
N_ADA = 6
SSD_HEADS = 16
SSD_HEAD_DIM = 64
SSD_INNER = 1024
SSD_STATE = 128
SSD_GROUPS = 2
SSD_CONV = 5
SSD_CHUNK = 128
GLA_HEADS = 4
GLA_KEY_DIM = 128
GLA_VAL_DIM = 256
GLA_RANK = 16
GLA_TAU = 16.0
GLA_BLOCK = 64
ATT_HEADS = 16
ATT_KV = 4
ATT_HD = 64
ATT_BLOCK = 128
WINDOW = 128
GRID_W = 64
ROPE_THETA = 10000.0
N_EXPERTS = 32
TOP_K = 4
EXPERT_FF = 1024
SWIGLU_LIMIT = 7.0
SWIGLU_ALPHA = 1.702
MOE_BLOCK = 256
TOK_TILE = 256
LANES = 128
NEG = -1e30

PJ_Z, PJ_V, PJ_OG, PJ_XBC, PJ_Q, PJ_K = 0, 1024, 2048, 3072, 4608, 5120
PJ_W = 5632
VMEM_LIMIT = 48 * 1024 * 1024


def _cp(sem, vmem=VMEM_LIMIT):
    return pltpu.CompilerParams(dimension_semantics=sem, vmem_limit_bytes=vmem)


class Layout:
    def __init__(self, n_prompt, prompt_len, n_sample, sample_len):
        self.n_prompt, self.prompt_len = n_prompt, prompt_len
        self.n_sample, self.sample_len = n_sample, sample_len
        self.p_tok = n_prompt * prompt_len
        self.n_tok = self.p_tok + n_sample * sample_len
        self.seqs = [(i * prompt_len, prompt_len) for i in range(n_prompt)]
        self.seqs += [(self.p_tok + i * sample_len, sample_len) for i in range(n_sample)]
        self.n_seq = len(self.seqs)

    def mod_row(self, start):
        return jnp.where(start < self.p_tok, 0, 1 + (start - self.p_tok) // self.sample_len)

def _sigmoid(x):
    return 1.0 / (1.0 + jnp.exp(-x))


def _silu(x):
    return x * _sigmoid(x)


def _softplus(x):
    return jnp.maximum(x, 0.0) + jnp.log(1.0 + jnp.exp(-jnp.abs(x)))


def _modnorm(x, g, sc, sh):
    ms = jnp.mean(x * x, axis=-1, keepdims=True)
    return (x * lax.rsqrt(ms + EPS) * g) * (1.0 + sc) + sh


def _dot(a, b, **kw):
    return jnp.dot(a, b, preferred_element_type=F32, **kw)


def _dot_nt(a, b):
    return lax.dot_general(a, b, (((1,), (1,)), ((), ())), preferred_element_type=F32)


def _dot_tn(a, b):
    return lax.dot_general(a, b, (((0,), (0,)), ((), ())), preferred_element_type=F32)


def _split(x, n):
    parts = []
    for _ in range(n):
        p = x.astype(BF16)
        parts.append(p)
        x = x - p.astype(F32)
    return parts


def _dot_sel(sel, x):
    sel = sel.astype(BF16)
    return sum(_dot(sel, p) for p in _split(x, 3))


def _dot_sel_r(x, sel):
    sel = sel.astype(BF16)
    return sum(_dot(p, sel) for p in _split(x, 3))


def _dot_hilo(x, w_hi, w_lo):
    x_hi, x_lo = _split(x, 2)
    return _dot(x_hi, w_hi) + _dot(x_lo, w_hi) + _dot(x_hi, w_lo)


def _hilo(w):
    hi = w.astype(BF16)
    return jnp.stack([hi, (w - hi.astype(F32)).astype(BF16)])


U32 = jnp.uint32
ROW_WORDS = D // 2
_HI_MASK = 0xFFFF0000


def _pack_rows(x):
    lo = lax.bitcast_convert_type(x[:, :ROW_WORDS], U32) >> 16
    hi = lax.bitcast_convert_type(x[:, ROW_WORDS:], U32) & jnp.uint32(_HI_MASK)
    return lo | hi


def _unpack_rows(u):
    lo = lax.bitcast_convert_type(u << 16, F32)
    hi = lax.bitcast_convert_type(u & jnp.uint32(_HI_MASK), F32)
    return jnp.concatenate([lo, hi], axis=1)


def _ada_kernel(c_ref, w_ref, b_ref, o_ref):
    o_ref[0] = _dot(_silu(c_ref[...]), w_ref[0], precision=HI) + b_ref[0]


def ada_table(cond8, ada_w, ada_b):
    depth, _, n = ada_w.shape
    tn = 1536
    return pl.pallas_call(
        _ada_kernel, grid=(depth, n // tn),
        in_specs=[pl.BlockSpec((8, D), lambda l, j: (0, 0)),
                  pl.BlockSpec((1, D, tn), lambda l, j: (l, 0, j)),
                  pl.BlockSpec((1, 1, tn), lambda l, j: (l, 0, j))],
        out_specs=pl.BlockSpec((1, 8, tn), lambda l, j: (l, 0, j)),
        out_shape=jax.ShapeDtypeStruct((depth, 8, n), F32),
        compiler_params=_cp(("arbitrary", "arbitrary")), name="ada_table",
    )(cond8, ada_w, ada_b.reshape(depth, 1, n))


def _stream_specs(lay, stream, tm):
    if not isinstance(stream, (tuple, list)):
        return [stream], [pl.BlockSpec((tm, stream.shape[1]), lambda i, *_: (i, 0))]
    assert lay.p_tok % tm == 0
    npt = lay.p_tok // tm
    w = stream[0].shape[1]
    return list(stream), [pl.BlockSpec((tm, w), lambda i, *_: (jnp.minimum(i, npt - 1), 0)),
                          pl.BlockSpec((tm, w), lambda i, *_: (jnp.maximum(i - npt, 0), 0))]


def _stream_tile(refs, in_prompt):
    if len(refs) == 1:
        return refs[0][...]
    return jnp.where(in_prompt, refs[0][...], refs[1][...])


def _proj_kernel(*refs, nx, npt):
    x_refs = refs[:nx]
    g_ref, sc_ref, sh_ref, w_ref, ws_ref, o_ref, os_ref, ost_ref, h_scr = refs[nx:]

    @pl.when(pl.program_id(1) == 0)
    def _():
        x = _stream_tile(x_refs, pl.program_id(0) < npt)
        h = _modnorm(x, g_ref[...], sc_ref[...], sh_ref[...])
        h_scr[...] = h.astype(BF16)
        small = _dot_hilo(h, ws_ref[0], ws_ref[1])
        os_ref[...] = small
        q = ost_ref.shape[2]
        for c in range(ost_ref.shape[0]):
            ost_ref[c] = small[q * c:q * (c + 1), :].T

    o_ref[...] = _dot(h_scr[...], w_ref[...]).astype(o_ref.dtype)


def norm_proj(lay, x, g, sc, sh, w, w_small, tm, tn, out_dtype, chunk):
    n_tok = lay.n_tok
    n = w.shape[1]
    ns = w_small.shape[-1]
    mrow = lambda i, j: (lay.mod_row(i * tm), 0, 0)
    xs, x_specs = _stream_specs(lay, x, tm)
    return pl.pallas_call(
        functools.partial(_proj_kernel, nx=len(xs), npt=lay.p_tok // tm), grid=(n_tok // tm, n // tn),
        in_specs=x_specs + [pl.BlockSpec((1, D), lambda i, j: (0, 0)),
                            pl.BlockSpec((None, 1, D), mrow),
                            pl.BlockSpec((None, 1, D), mrow),
                            pl.BlockSpec((D, tn), lambda i, j: (0, j)),
                            pl.BlockSpec((2, D, ns), lambda i, j: (0, 0, 0))],
        out_specs=[pl.BlockSpec((tm, tn), lambda i, j: (i, j)),
                   pl.BlockSpec((tm, ns), lambda i, j: (i, 0)),
                   pl.BlockSpec((tm // chunk, ns, chunk), lambda i, j: (i, 0, 0))],
        out_shape=[jax.ShapeDtypeStruct((n_tok, n), out_dtype),
                   jax.ShapeDtypeStruct((n_tok, ns), F32),
                   jax.ShapeDtypeStruct((n_tok // chunk, ns, chunk), F32)],
        scratch_shapes=[pltpu.VMEM((tm, D), BF16)],
        compiler_params=_cp(("arbitrary", "arbitrary")), name="norm_proj",
    )(*xs, g.reshape(1, D), sc, sh, w, w_small)


CONV_HALO = 16


def _ssd_load(d, c, xc, dtg_ref, dtgT_ref, S):
    q = SSD_CHUNK
    nh = SSD_HEADS
    rows = pl.ds(pl.multiple_of(c * q, q), q)
    return (rows, xc[rows, 0:SSD_INNER], xc[rows, SSD_INNER:SSD_INNER + 2 * SSD_GROUPS * SSD_STATE],
            dtg_ref[rows, nh * d:nh * d + nh], dtgT_ref[c, nh * d:nh * d + nh, :], S[...])


def _ssd_chunks(loaded, alog_ref, alogT_ref, dtb_ref, dtbT_ref):
    q = SSD_CHUNK
    nh = SSD_HEADS
    hp = nh * SSD_HEAD_DIM
    row = lax.broadcasted_iota(I32, (q, q), 0)
    col = lax.broadcasted_iota(I32, (q, q), 1)
    lo = lax.broadcasted_iota(I32, (q, LANES), 1) < SSD_HEAD_DIM
    head_of = lax.broadcasted_iota(I32, (nh, hp), 1) // SSD_HEAD_DIM
    spread = (lax.broadcasted_iota(I32, (nh, hp), 0) == head_of).astype(BF16)
    rep = (nh // SSD_GROUPS) // 2
    pairs = [(g, j) for g in range(SSD_GROUPS) for j in range(rep * g, rep * (g + 1))]
    lane = lambda j: slice(LANES * j, LANES * (j + 1))

    def per_lane(arr, passes):
        return sum(_dot(p, spread) for p in _split(arr, passes))

    stage1 = []
    for d, (_, _, _, dtg, dtgT, _) in enumerate(loaded):
        dt = _softplus(dtg + dtb_ref[d:d + 1, :])
        dtT = _softplus(dtgT + dtbT_ref[:, d:d + 1])
        ad = dt * (-jnp.exp(alog_ref[d:d + 1, :]))
        adT = dtT * (-jnp.exp(alogT_ref[:, d:d + 1]))
        if d == 0:
            e, eT = _dot_sel(col <= row, ad), _dot_sel_r(adT, row <= col)
        else:
            e, eT = _dot_sel(col < row, ad), _dot_sel_r(adT, row < col)
        stage1.append((dt, ad, e, eT))
    stage2 = []
    for d, ((_, xs, bc, _, _, s), (dt, ad, e, eT)) in enumerate(zip(loaded, stage1)):
        if d == 0:
            tot = e[q - 1:q, :]
            fq, fk = jnp.exp(e), jnp.exp(tot - e)
        else:
            tot = jnp.sum(ad, axis=0, keepdims=True)
            fq, fk = jnp.exp(tot - e), jnp.exp(e)
        dt_x, fq_x, fk_x = per_lane(dt, 1), per_lane(fq, 1), per_lane(fk, 1)
        dec_x = per_lane(jnp.broadcast_to(jnp.exp(tot), (8, nh)), 3)[0:1, :]
        xdt_all = xs.astype(F32) * dt_x
        xk_all = (xdt_all * fk_x).astype(BF16)
        bg_t = [bc[:, SSD_STATE * g:SSD_STATE * (g + 1)].T for g in range(SSD_GROUPS)]
        cg = [bc[:, SSD_STATE * (SSD_GROUPS + g):SSD_STATE * (SSD_GROUPS + g + 1)] for g in range(SSD_GROUPS)]
        gmat = [_dot(cg[g], bg_t[g]) for g in range(SSD_GROUPS)]
        carried = [_dot(cg[g], s[:, lane(j)].astype(BF16)) * fq_x[:, lane(j)] for g, j in pairs]
        s_new = [s[:, lane(j)] * dec_x[:, lane(j)] + _dot(bg_t[g], xk_all[:, lane(j)]) for g, j in pairs]
        stage2.append((xdt_all, gmat, carried, s_new))
    results = []
    for d, ((_, _, e, eT), (xdt_all, gmat, carried, s_new)) in enumerate(zip(stage1, stage2)):
        mask = row >= col if d == 0 else col >= row
        ys = []
        for (g, j), off in zip(pairs, carried):
            parts = []
            for hh in (2 * j, 2 * j + 1):
                diff = e[:, hh:hh + 1] - eT[hh:hh + 1, :] if d == 0 else eT[hh:hh + 1, :] - e[:, hh:hh + 1]
                parts.append((gmat[g] * jnp.exp(jnp.where(mask, diff, NEG))).astype(BF16))
            lhs = jnp.concatenate(parts, axis=1)
            xdt = xdt_all[:, lane(j)]
            rhs = jnp.concatenate([jnp.where(lo, xdt, 0.0), jnp.where(lo, 0.0, xdt)], axis=0)
            ys.append(_dot(lhs, rhs.astype(BF16)) + off)
        results.append((jnp.concatenate(ys, axis=1), jnp.concatenate(s_new, axis=1)))
    return results


def _ssd_seq_kernel(*refs, has_init):
    (xbc_ref, z_ref, dtg_ref, dtgT_ref, cw_ref, cb_ref,
     alog_ref, alogT_ref, dtb_ref, dtbT_ref, dskip_ref, nrm_ref) = refs[:12]
    refs = refs[12:]
    if has_init:
        s0f_ref, s0b_ref = refs[:2]
        refs = refs[2:]
    y_ref, sf_ref, sb_ref, xc, ext, Sf, Sb, yf, yb = refs
    ln = xbc_ref.shape[0]
    q = SSD_CHUNK
    nc = ln // q
    h = CONV_HALO
    pad = SSD_CONV // 2

    def conv_body(c, carry):
        r0 = pl.multiple_of(c * q, q)
        prev = xbc_ref[pl.ds(pl.multiple_of(jnp.maximum(r0 - h, 0), h), h), :].astype(F32)
        nxt = xbc_ref[pl.ds(pl.multiple_of(jnp.minimum(r0 + q, ln - h), h), h), :].astype(F32)
        ext[0:h, :] = jnp.where(c > 0, prev, 0.0)
        ext[h:h + q, :] = xbc_ref[pl.ds(r0, q), :].astype(F32)
        ext[h + q:h + q + h, :] = jnp.where(c < nc - 1, nxt, 0.0)
        acc = jnp.broadcast_to(cb_ref[...], (q, cb_ref.shape[1]))
        for k in range(SSD_CONV):
            acc = acc + cw_ref[k:k + 1, :] * ext[h - pad + k:h - pad + k + q, :]
        xc[pl.ds(r0, q), :] = _silu(acc).astype(xc.dtype)
        return carry
    lax.fori_loop(0, nc, conv_body, 0)

    if has_init:
        Sf[...] = s0f_ref[0].T
        Sb[...] = s0b_ref[0].T
    else:
        Sf[...] = jnp.zeros_like(Sf)
        Sb[...] = jnp.zeros_like(Sb)
    params = (alog_ref, alogT_ref, dtb_ref, dtbT_ref)

    def scan_body(c, carry):
        lf = _ssd_load(0, c, xc, dtg_ref, dtgT_ref, Sf)
        lb = _ssd_load(1, nc - 1 - c, xc, dtg_ref, dtgT_ref, Sb)
        (y_f, s_f), (y_b, s_b) = _ssd_chunks((lf, lb), *params)
        yf[lf[0], :] = y_f
        yb[lb[0], :] = y_b
        Sf[...] = s_f
        Sb[...] = s_b
        return carry
    lax.fori_loop(0, nc, scan_body, 0, unroll=2)
    sf_ref[0] = Sf[...].T
    sb_ref[0] = Sb[...].T

    def out_body(c, carry):
        rows = pl.ds(pl.multiple_of(c * q, q), q)
        ytot = yf[rows, :] + yb[rows, :] + dskip_ref[...] * xc[rows, 0:SSD_INNER].astype(F32)
        yg = ytot * _silu(z_ref[rows, :].astype(F32))
        ms = jnp.mean(yg * yg, axis=-1, keepdims=True)
        y_ref[rows, :] = (yg * lax.rsqrt(ms + EPS) * nrm_ref[...]).astype(y_ref.dtype)
        return carry
    lax.fori_loop(0, nc, out_body, 0)


def _ssd_call(n_seq, ln, blk0, proj, small, smallT3, init, params):
    q = SSD_CHUNK
    hp = SSD_HEADS * SSD_HEAD_DIM
    cw = SSD_INNER + 2 * SSD_GROUPS * SSD_STATE
    nc = ln // q
    assert PJ_XBC % cw == 0 and PJ_Z % SSD_INNER == 0
    tok = lambda w, cb: pl.BlockSpec((ln, w), lambda b: (blk0 + b, cb))
    seq3 = pl.BlockSpec((1, hp, SSD_STATE), lambda b: (b, 0, 0))
    full = lambda a: pl.BlockSpec(a.shape, lambda b: (0,) * a.ndim)
    init = () if init is None else tuple(init)
    return pl.pallas_call(
        functools.partial(_ssd_seq_kernel, has_init=bool(init)), grid=(n_seq,),
        in_specs=[tok(cw, PJ_XBC // cw), tok(SSD_INNER, PJ_Z // SSD_INNER), tok(LANES, 0),
                  pl.BlockSpec((nc, 2 * SSD_HEADS, q), lambda b: (blk0 + b, 0, 0))]
        + [full(a) for a in params] + [seq3] * len(init),
        out_specs=[pl.BlockSpec((ln, hp), lambda b: (b, 0)), seq3, seq3],
        out_shape=[jax.ShapeDtypeStruct((n_seq * ln, hp), BF16),
                   jax.ShapeDtypeStruct((n_seq, hp, SSD_STATE), F32),
                   jax.ShapeDtypeStruct((n_seq, hp, SSD_STATE), F32)],
        scratch_shapes=[pltpu.VMEM((ln, cw), BF16), pltpu.VMEM((q + 2 * CONV_HALO, cw), F32),
                        pltpu.VMEM((SSD_STATE, hp), F32), pltpu.VMEM((SSD_STATE, hp), F32),
                        pltpu.VMEM((ln, hp), F32), pltpu.VMEM((ln, hp), F32)],
        compiler_params=_cp(("arbitrary",)), name="ssd_seq",
    )(proj, proj, small, smallT3, *params, *init)


def _gla_gates(c, q_ref, k_ref, glr_ref, w2_ref, gb_ref, qi, ki, qo, kk, dec):
    t = GLA_BLOCK
    rows = pl.ds(pl.multiple_of(c * t, t), t)
    c0 = 2 * SSD_HEADS
    gps = [_dot_hilo(glr_ref[rows, c0 + GLA_RANK * d:c0 + GLA_RANK * (d + 1)], w2_ref[0, d], w2_ref[1, d])
           + gb_ref[d:d + 1, :] for d in (0, 1)]
    las = [-_softplus(-gp) * (1.0 / GLA_TAU) for gp in gps]
    row = lax.broadcasted_iota(I32, (t, t), 0)
    col = lax.broadcasted_iota(I32, (t, t), 1)
    es = [_dot_sel(col <= row, las[0]), _dot_sel(col < row, las[1])]
    mid = t // 2 - 1
    qf = q_ref[rows, :].astype(F32) * (GLA_KEY_DIM ** -0.5)
    kf = k_ref[rows, :].astype(F32)
    for d in (0, 1):
        e = es[d]
        r = e[mid:mid + 1, :]
        if d == 0:
            tot = e[t - 1:t, :]
            fqi, fki = jnp.exp(e - r), jnp.exp(r - e)
            fq, fk = jnp.exp(e), jnp.exp(tot - e)
        else:
            tot = e[t - 1:t, :] + las[1][t - 1:t, :]
            fqi, fki = jnp.exp(r - e), jnp.exp(e - r)
            fq, fk = jnp.exp(tot - e), jnp.exp(e)
        qi[d, rows, :] = (qf * fqi).astype(BF16)
        ki[d, rows, :] = (kf * fki).astype(BF16)
        qo[d, rows, :] = (qf * fq).astype(BF16)
        kk[d, rows, :] = (kf * fk).astype(BF16)
        dec[d, c] = jnp.broadcast_to(jnp.exp(tot), (8, tot.shape[1]))


def _gla_load(d, c, v_ref, qi, ki, qo, kk, dec, S):
    t = GLA_BLOCK
    rows = pl.ds(pl.multiple_of(c * t, t), t)
    return (rows, qi[d, rows, :], ki[d, rows, :], qo[d, rows, :], kk[d, rows, :], dec[d, c][0:1, :],
            v_ref[rows, :], [S[h] for h in range(GLA_HEADS)])


def _gla_blocks(loaded):
    t = GLA_BLOCK
    dk, dv = GLA_KEY_DIM, GLA_VAL_DIM
    row = lax.broadcasted_iota(I32, (t, t), 0)
    col = lax.broadcasted_iota(I32, (t, t), 1)
    heads = range(GLA_HEADS)
    ks = [slice(dk * h, dk * (h + 1)) for h in heads]
    first = []
    for _, q_in, k_in, q_st, k_st, dec, v, states in loaded:
        vs = [v[:, dv * h:dv * (h + 1)] for h in heads]
        scores = [_dot_nt(q_in[:, ks[h]], k_in[:, ks[h]]) for h in heads]
        carried = [_dot_nt(q_st[:, ks[h]], states[h].astype(BF16)) for h in heads]
        grown = [_dot_tn(vs[h], k_st[:, ks[h]]) for h in heads]
        first.append((vs, scores, carried, grown))
    results = []
    for d, ((_, _, _, _, _, dec, _, states), (vs, scores, carried, grown)) in enumerate(zip(loaded, first)):
        mask = row >= col if d == 0 else col >= row
        outs = [_dot(jnp.where(mask, scores[h], 0.0).astype(BF16), vs[h]) + carried[h] for h in heads]
        new_states = [states[h] * dec[:, ks[h]] + grown[h] for h in heads]
        results.append((jnp.concatenate(outs, axis=1), new_states))
    return results


def _gla_seq_kernel(*refs, has_init):
    q_ref, k_ref, v_ref, og_ref, glr_ref, w2_ref, gb_ref, nrm_ref = refs[:8]
    refs = refs[8:]
    if has_init:
        s0f_ref, s0b_ref = refs[:2]
        refs = refs[2:]
    o_ref, sf_ref, sb_ref, Sf, Sb, of, ob, qi, ki, qo, kk, dec = refs
    ln = q_ref.shape[0]
    t = GLA_BLOCK
    nc = ln // t
    dv = GLA_VAL_DIM
    for h in range(GLA_HEADS):
        if has_init:
            Sf[h] = s0f_ref[0, h].T
            Sb[h] = s0b_ref[0, h].T
        else:
            Sf[h] = jnp.zeros(Sf.shape[1:], F32)
            Sb[h] = jnp.zeros(Sb.shape[1:], F32)
    staged = (qi, ki, qo, kk, dec)

    def gate_body(c, carry):
        _gla_gates(c, q_ref, k_ref, glr_ref, w2_ref, gb_ref, *staged)
        return carry
    lax.fori_loop(0, nc, gate_body, 0, unroll=2)

    def scan_body(c, carry):
        lf = _gla_load(0, c, v_ref, *staged, Sf)
        lb = _gla_load(1, nc - 1 - c, v_ref, *staged, Sb)
        (o_f, s_f), (o_b, s_b) = _gla_blocks((lf, lb))
        of[lf[0], :] = o_f
        ob[lb[0], :] = o_b
        for h in range(GLA_HEADS):
            Sf[h] = s_f[h]
            Sb[h] = s_b[h]
        return carry
    lax.fori_loop(0, nc, scan_body, 0, unroll=2)
    for h in range(GLA_HEADS):
        sf_ref[0, h] = Sf[h].T
        sb_ref[0, h] = Sb[h].T

    def out_body(c, carry):
        rows = pl.ds(pl.multiple_of(c * t, t), t)
        for h in range(GLA_HEADS):
            vl = slice(dv * h, dv * (h + 1))
            ot = of[rows, vl] + ob[rows, vl]
            ms = jnp.mean(ot * ot, axis=-1, keepdims=True)
            on = ot * lax.rsqrt(ms + EPS) * nrm_ref[...]
            o_ref[rows, vl] = (on * _silu(og_ref[rows, vl].astype(F32))).astype(o_ref.dtype)
        return carry
    lax.fori_loop(0, nc, out_body, 0)


def _gla_call(n_seq, ln, blk0, proj, small, init, params):
    qk_w = GLA_HEADS * GLA_KEY_DIM
    v_w = GLA_HEADS * GLA_VAL_DIM
    tok = lambda w, cb: pl.BlockSpec((ln, w), lambda b: (blk0 + b, cb))
    seq4 = pl.BlockSpec((1, GLA_HEADS, GLA_KEY_DIM, GLA_VAL_DIM), lambda b: (b, 0, 0, 0))
    full = lambda a: pl.BlockSpec(a.shape, lambda b: (0,) * a.ndim)
    st_shape = jax.ShapeDtypeStruct((n_seq, GLA_HEADS, GLA_KEY_DIM, GLA_VAL_DIM), F32)
    init = () if init is None else tuple(init)
    return pl.pallas_call(
        functools.partial(_gla_seq_kernel, has_init=bool(init)), grid=(n_seq,),
        in_specs=[tok(qk_w, PJ_Q // qk_w), tok(qk_w, PJ_K // qk_w), tok(v_w, PJ_V // v_w),
                  tok(v_w, PJ_OG // v_w), tok(LANES, 0)] + [full(a) for a in params] + [seq4] * len(init),
        out_specs=[pl.BlockSpec((ln, v_w), lambda b: (b, 0)), seq4, seq4],
        out_shape=[jax.ShapeDtypeStruct((n_seq * ln, v_w), BF16), st_shape, st_shape],
        scratch_shapes=[pltpu.VMEM((GLA_HEADS, GLA_VAL_DIM, GLA_KEY_DIM), F32),
                        pltpu.VMEM((GLA_HEADS, GLA_VAL_DIM, GLA_KEY_DIM), F32),
                        pltpu.VMEM((ln, v_w), F32), pltpu.VMEM((ln, v_w), F32)]
        + [pltpu.VMEM((2, ln, qk_w), BF16)] * 4 + [pltpu.VMEM((2, ln // GLA_BLOCK, 8, qk_w), F32)],
        compiler_params=_cp(("arbitrary",)), name="gla_seq",
    )(proj, proj, proj, proj, small, *params, *init)


def l0_mixers(lay, proj, small, small_t, ssd_f0, ssd_b0, gla_f0, gla_b0, conv_w, conv_b, a_log, dt_bias, d_skip,
              ssd_norm, gate_w2, gate_b, gla_norm):
    hp = SSD_HEADS * SSD_HEAD_DIM
    ssd_p = (conv_w, conv_b.reshape(1, -1), a_log, a_log.T, dt_bias, dt_bias.T,
             jnp.repeat(d_skip, SSD_HEAD_DIM).reshape(1, hp), ssd_norm.reshape(1, hp))
    gla_p = (_hilo(gate_w2), gate_b, gla_norm.reshape(1, -1))
    np_, ns = lay.n_prompt, lay.n_sample
    assert lay.p_tok % lay.sample_len == 0
    groups = [(np_, lay.prompt_len, 0, None, None),
              (ns, lay.sample_len, lay.p_tok // lay.sample_len,
               (ssd_f0.reshape(ns, hp, SSD_STATE), ssd_b0.reshape(ns, hp, SSD_STATE)), (gla_f0, gla_b0))]
    ys, os_, states = [], [], None
    for n, ln, blk0, ssd_init, gla_init in groups:
        y, sf, sb = _ssd_call(n, ln, blk0, proj, small, small_t, ssd_init, ssd_p)
        o, gf, gb = _gla_call(n, ln, blk0, proj, small, gla_init, gla_p)
        ys.append(y)
        os_.append(o)
        if states is None:
            states = (sf, sb, gf, gb)
    return tuple(ys), tuple(os_), states


def _res_kernel(*refs, counts, ks, npt):
    in_prompt = pl.program_id(0) < npt
    streams, pos = [], 0
    for c in counts:
        streams.append(refs[pos:pos + c])
        pos += c
    w_ref, gate_ref, o_ref = refs[pos:]
    acc = None
    off = 0
    for a_refs, k in zip(streams[:-1], ks):
        part = _dot(_stream_tile(a_refs, in_prompt), w_ref[off:off + k, :])
        acc = part if acc is None else acc + part
        off += k
    o_ref[...] = _stream_tile(streams[-1], in_prompt) + gate_ref[...] * acc


def proj_residual(lay, acts, w, x, gate, tm=512):
    arrays, specs, counts = [], [], []
    for s in list(acts) + [x]:
        a, sp = _stream_specs(lay, s, tm)
        arrays += a
        specs += sp
        counts.append(len(a))
    ks = tuple(int((a[0] if isinstance(a, (tuple, list)) else a).shape[1]) for a in acts)
    mrow = lambda i: (lay.mod_row(i * tm), 0, 0)
    return pl.pallas_call(
        functools.partial(_res_kernel, counts=tuple(counts), ks=ks, npt=lay.p_tok // tm),
        grid=(lay.n_tok // tm,),
        in_specs=specs + [pl.BlockSpec(w.shape, lambda i: (0, 0)), pl.BlockSpec((None, 1, D), mrow)],
        out_specs=pl.BlockSpec((tm, D), lambda i: (i, 0)),
        out_shape=jax.ShapeDtypeStruct((lay.n_tok, D), F32),
        compiler_params=_cp(("arbitrary",)), name="proj_residual",
    )(*arrays, w, gate)


def _router_kernel(x_ref, g_ref, sc_ref, sh_ref, rw_ref, rb_ref,
                   h_ref, idx_ref, gate_ref, pos_ref, posT_ref, cnt_ref):
    tm = x_ref.shape[0]
    h = _modnorm(x_ref[...], g_ref[...], sc_ref[...], sh_ref[...])
    h_hi = h.astype(BF16)
    h_ref[...] = h_hi
    h_lo = (h - h_hi.astype(F32)).astype(BF16)
    lg = (_dot(h_hi, rw_ref[0]) + _dot(h_lo, rw_ref[0]) + _dot(h_hi, rw_ref[1])
          + rb_ref[...])
    lane = lax.broadcasted_iota(I32, (tm, LANES), 1).astype(F32)
    vals, ids = [], []
    for _ in range(TOP_K):
        m = jnp.max(lg, axis=1, keepdims=True)
        i = jnp.min(jnp.where(lg == m, lane, float(LANES)), axis=1, keepdims=True)
        vals.append(m)
        ids.append(i)
        lg = jnp.where(lane == i, -jnp.inf, lg)
    ex = [jnp.exp(v - vals[0]) for v in vals]
    den = ex[0] + ex[1] + ex[2] + ex[3]
    sel = jnp.zeros((tm, LANES), F32)
    for i in ids:
        sel = sel + (lane == i).astype(F32)
    row = lax.broadcasted_iota(I32, (tm, tm), 0)
    col = lax.broadcasted_iota(I32, (tm, tm), 1)
    before = _dot((col < row).astype(BF16), sel.astype(BF16))
    n = jnp.sum(sel, axis=0, keepdims=True)
    er = lax.broadcasted_iota(I32, (LANES, LANES), 0)
    ec = lax.broadcasted_iota(I32, (LANES, LANES), 1)
    n_al = jnp.ceil(n * (1.0 / SEG_ALIGN)) * SEG_ALIGN
    offs = _dot(jnp.broadcast_to(n_al, (8, LANES)).astype(BF16), (er < ec).astype(BF16))[0:1, :]
    slot = before + offs
    idx_o = jnp.zeros((tm, LANES), F32)
    gate_o = jnp.zeros((tm, LANES), F32)
    pos_o = jnp.zeros((tm, LANES), F32)
    for k in range(TOP_K):
        p = jnp.sum(jnp.where(lane == ids[k], slot, 0.0), axis=1, keepdims=True)
        idx_o = jnp.where(lane == k, ids[k], idx_o)
        gate_o = jnp.where(lane == k, ex[k] / den, gate_o)
        pos_o = jnp.where(lane == k, p, pos_o)
    idx_ref[...] = idx_o.astype(I32)
    gate_ref[...] = gate_o
    pos_ref[...] = pos_o.astype(I32)
    posT_ref[...] = pos_o.T[0:8, :]
    cnt_ref[0] = jnp.broadcast_to(n, (8, LANES))


def moe_router(lay, x, g, sc, sh, rw, rb):
    n_tok = x.shape[0]
    tm = TOK_TILE
    nt = n_tok // tm
    mrow = lambda i: (lay.mod_row(i * tm), 0, 0)
    tile = lambda w, dt: (pl.BlockSpec((tm, w), lambda i: (i, 0)), jax.ShapeDtypeStruct((n_tok, w), dt))
    outs = [tile(D, BF16), tile(LANES, I32), tile(LANES, F32), tile(LANES, I32),
            (pl.BlockSpec((8, tm), lambda i: (0, i)), jax.ShapeDtypeStruct((8, n_tok), F32)),
            (pl.BlockSpec((1, 8, LANES), lambda i: (i, 0, 0)), jax.ShapeDtypeStruct((nt, 8, LANES), F32))]
    return pl.pallas_call(
        _router_kernel, grid=(nt,),
        in_specs=[pl.BlockSpec((tm, D), lambda i: (i, 0)),
                  pl.BlockSpec((1, D), lambda i: (0, 0)),
                  pl.BlockSpec((None, 1, D), mrow), pl.BlockSpec((None, 1, D), mrow),
                  pl.BlockSpec((2, D, LANES), lambda i: (0, 0, 0)),
                  pl.BlockSpec((1, LANES), lambda i: (0, 0))],
        out_specs=[o[0] for o in outs], out_shape=[o[1] for o in outs],
        compiler_params=_cp(("arbitrary",)), name="moe_router",
    )(x, g.reshape(1, D), sc, sh, rw, rb)


SEG_ALIGN = 8
SEG_CHUNK = 16
REST_BITS = tuple(range(int(math.log2(SEG_CHUNK)) - 1, int(math.log2(SEG_ALIGN)) - 1, -1))
TILE_ROWS = TOK_TILE * TOP_K + N_EXPERTS * SEG_ALIGN


def _pow2_copies(n, src, dst, make_copy, op, bits):
    for b in bits:
        sz = 1 << b
        done = (n >> (b + 1)) << (b + 1)

        @pl.when((n & sz) != 0)
        def _():
            op(make_copy(pl.multiple_of(src + done, SEG_ALIGN), pl.multiple_of(dst + done, SEG_ALIGN), sz))


def _start_segments(i, n_ref, off_ref, dst_ref, make_copy):
    def body(e, carry):
        k = i * N_EXPERTS + e
        n, src, dst = n_ref[k], off_ref[k], dst_ref[k]

        def chunk(j, c):
            o = pl.multiple_of(j * SEG_CHUNK, SEG_CHUNK)
            make_copy(pl.multiple_of(src + o, SEG_ALIGN), pl.multiple_of(dst + o, SEG_ALIGN),
                      SEG_CHUNK).start(priority=1)
            return c
        shift = int(math.log2(SEG_CHUNK))
        full = n >> shift
        lax.fori_loop(0, full, chunk, 0)
        done = full << shift
        _pow2_copies(n - done, src + done, dst + done, make_copy, lambda c: c.start(), REST_BITS)
        return carry
    lax.fori_loop(0, N_EXPERTS, body, 0)


TAIL_BITS = tuple(range(int(math.log2(MOE_BLOCK)) - 1, int(math.log2(SEG_ALIGN)) - 1, -1))
TILE_BITS = tuple(range(int(math.log2(TILE_ROWS)), int(math.log2(SEG_ALIGN)) - 1, -1))


def _wait_rows(total, make_copy):
    _pow2_copies(total, 0, 0, make_copy, lambda c: c.wait(), TILE_BITS)


def _dispatch_kernel(n_ref, off_ref, dst_ref, tot_ref, tn_ref, td_ref, posT_ref, h_ref, xout_ref,
                     srt, zbuf, sems):
    i = pl.program_id(0)
    last = pl.num_programs(0) - 1
    slot = i % 2
    tm = h_ref.shape[0]
    r = lax.broadcasted_iota(I32, (TILE_ROWS, tm), 0)
    hit = jnp.zeros((TILE_ROWS, tm), jnp.bool_)
    for k in range(TOP_K):
        hit = hit | (r == posT_ref[k:k + 1, :].astype(I32))
    sel = jnp.where(hit, 1.0, 0.0).astype(BF16)
    srt[slot] = _pack_rows(_dot(sel, h_ref[...]))

    def copier(s):
        def make_copy(src, dst, sz):
            return pltpu.make_async_copy(srt.at[s, pl.ds(src, sz)], xout_ref.at[pl.ds(dst, sz)], sems.at[s])
        return make_copy

    _start_segments(i, n_ref, off_ref, dst_ref, copier(slot))

    @pl.when(i > 0)
    def _():
        _wait_rows(tot_ref[jnp.maximum(i - 1, 0)], copier(1 - slot))

    @pl.when(i == last)
    def _():
        _wait_rows(tot_ref[i], copier(slot))
        zbuf[...] = jnp.zeros_like(zbuf)
        sem = sems.at[0]

        def zero_copy(src, dst, sz):
            return pltpu.make_async_copy(zbuf.at[pl.ds(src, sz)], xout_ref.at[pl.ds(dst, sz)], sem)

        nb = xout_ref.shape[0] // MOE_BLOCK
        for op in (lambda c: c.start(), lambda c: c.wait()):
            def body(e, carry):
                _pow2_copies(tn_ref[e], 0, td_ref[e], zero_copy, op, TAIL_BITS)
                return carry
            lax.fori_loop(0, N_EXPERTS, body, 0)

            def unused(b, carry):
                op(zero_copy(0, pl.multiple_of(b * MOE_BLOCK, MOE_BLOCK), MOE_BLOCK))
                return carry
            lax.fori_loop(tn_ref[N_EXPERTS], nb, unused, 0)


def moe_dispatch(n_tab, off_tab, dst_tab, tot_tab, tail_n, tail_dst, posT, h2, n_rows):
    n_tok = h2.shape[0]
    tm = TOK_TILE
    grid_spec = pltpu.PrefetchScalarGridSpec(
        num_scalar_prefetch=6, grid=(n_tok // tm,),
        in_specs=[pl.BlockSpec((8, tm), lambda i, *_: (0, i)),
                  pl.BlockSpec((tm, D), lambda i, *_: (i, 0))],
        out_specs=pl.BlockSpec(memory_space=pl.ANY),
        scratch_shapes=[pltpu.VMEM((2, TILE_ROWS, ROW_WORDS), U32), pltpu.VMEM((MOE_BLOCK, ROW_WORDS), U32),
                        pltpu.SemaphoreType.DMA((2,))])
    return pl.pallas_call(
        _dispatch_kernel, grid_spec=grid_spec,
        out_shape=jax.ShapeDtypeStruct((n_rows, ROW_WORDS), U32),
        compiler_params=_cp(("arbitrary",)), name="moe_dispatch",
    )(n_tab, off_tab, dst_tab, tot_tab, tail_n, tail_dst, posT, h2)


def _combine_kernel(n_ref, off_ref, dst_ref, tot_ref, pos_ref, gate_ref, x_ref, g2_ref, y_ref, *rest, npt):
    o_refs, (buf, sems) = rest[:-2], rest[-2:]
    i = pl.program_id(0)
    last = pl.num_programs(0) - 1
    slot = i % 2
    tm = x_ref.shape[0]
    na = TILE_ROWS

    def copier(s):
        def make_copy(src, dst, sz):
            return pltpu.make_async_copy(y_ref.at[pl.ds(dst, sz)], buf.at[s, pl.ds(src, sz)], sems.at[s])
        return make_copy

    def fetch(tile, s):
        buf[s, tm * TOP_K:na, :] = jnp.zeros((na - tm * TOP_K, ROW_WORDS), U32)
        _start_segments(tile, n_ref, off_ref, dst_ref, copier(s))

    @pl.when(i == 0)
    def _():
        fetch(i, slot)

    @pl.when(i < last)
    def _():
        fetch(i + 1, 1 - slot)

    _wait_rows(tot_ref[i], copier(slot))
    lane = lax.broadcasted_iota(I32, (tm, na), 1)
    pw = jnp.zeros((tm, na), F32)
    for k in range(TOP_K):
        pw = pw + jnp.where(lane == pos_ref[:, k:k + 1], gate_ref[:, k:k + 1], 0.0)
    phi = pw.astype(BF16)
    plo = (pw - phi.astype(F32)).astype(BF16)
    yb = _unpack_rows(buf[slot]).astype(BF16)
    res = x_ref[...] + g2_ref[...] * (_dot(phi, yb) + _dot(plo, yb))
    if len(o_refs) == 1:
        o_refs[0][...] = res
    else:
        @pl.when(i < npt)
        def _():
            o_refs[0][...] = res

        @pl.when(i >= npt)
        def _():
            o_refs[1][...] = res


def moe_combine(lay, n_tab, off_tab, dst_tab, tot_tab, pos, gates, x, gate2, y_rows, split):
    n_tok = x.shape[0]
    tm = TOK_TILE
    npt = lay.p_tok // tm
    mrow = lambda i, *_: (lay.mod_row(i * tm), 0, 0)
    if split:
        out_specs = [pl.BlockSpec((tm, D), lambda i, *_: (jnp.minimum(i, npt - 1), 0)),
                     pl.BlockSpec((tm, D), lambda i, *_: (jnp.maximum(i - npt, 0), 0))]
        out_shape = [jax.ShapeDtypeStruct((lay.p_tok, D), F32), jax.ShapeDtypeStruct((n_tok - lay.p_tok, D), F32)]
    else:
        out_specs = pl.BlockSpec((tm, D), lambda i, *_: (i, 0))
        out_shape = jax.ShapeDtypeStruct((n_tok, D), F32)
    grid_spec = pltpu.PrefetchScalarGridSpec(
        num_scalar_prefetch=4, grid=(n_tok // tm,),
        in_specs=[pl.BlockSpec((tm, LANES), lambda i, *_: (i, 0)),
                  pl.BlockSpec((tm, LANES), lambda i, *_: (i, 0)),
                  pl.BlockSpec((tm, D), lambda i, *_: (i, 0)),
                  pl.BlockSpec((None, 1, D), mrow),
                  pl.BlockSpec(memory_space=pl.ANY)],
        out_specs=out_specs,
        scratch_shapes=[pltpu.VMEM((2, TILE_ROWS, ROW_WORDS), U32), pltpu.SemaphoreType.DMA((2,))])
    return pl.pallas_call(
        functools.partial(_combine_kernel, npt=npt), grid_spec=grid_spec, out_shape=out_shape,
        compiler_params=_cp(("arbitrary",)), name="moe_combine",
    )(n_tab, off_tab, dst_tab, tot_tab, pos, gates, x, gate2, y_rows)


def _expert_kernel(be_ref, nv_ref, nxt_ref, slot_ref, x_ref, b_ref, wg_hbm, wu_hbm, wd_hbm,
                   y_ref, wf, sems, *, layer):
    i = pl.program_id(0)
    valid = i < nv_ref[0]
    e = be_ref[i]
    slot = slot_ref[e]
    changed = jnp.logical_or(i == 0, e != be_ref[jnp.maximum(i - 1, 0)])

    def weight_copies(ex, s):
        return [pltpu.make_async_copy(w.at[layer, ex], wf.at[s, k], sems.at[s, k])
                for k, w in enumerate((wg_hbm, wu_hbm, wd_hbm))]

    @pl.when(jnp.logical_and(valid, changed))
    def _():
        @pl.when(i == 0)
        def _():
            for c in weight_copies(e, slot):
                c.start()

        nxt = nxt_ref[e]

        @pl.when(nxt >= 0)
        def _():
            for c in weight_copies(nxt, 1 - slot):
                c.start(priority=1)

        for c in weight_copies(e, slot):
            c.wait()

    @pl.when(valid)
    def _():
        x = _unpack_rows(x_ref[...])
        b = b_ref[e]
        gt = jnp.minimum(_dot(x, wf[slot, 0]) + b[0:1, :], SWIGLU_LIMIT)
        up = jnp.clip(_dot(x, wf[slot, 1]) + b[1:2, :], -SWIGLU_LIMIT, SWIGLU_LIMIT)
        act = (up + 1.0) * gt * _sigmoid(SWIGLU_ALPHA * gt)
        y = _dot(act, wf[slot, 2]) + b[2:3, :]
        y_ref[...] = _pack_rows(y.astype(BF16).astype(F32))

    @pl.when(jnp.logical_not(valid))
    def _():
        y_ref[...] = jnp.zeros_like(y_ref)


def moe_experts(layer, blk_expert, n_valid, next_expert, slot, x_rows, w_gate, b_gate, w_up, b_up, w_down,
                b_down):
    n_rows = x_rows.shape[0]
    nb = n_rows // MOE_BLOCK
    depth, ne, _, ff = w_gate.shape
    assert ff == D
    rowblk = lambda i, be, nv, *_: (jnp.maximum(jnp.minimum(i, nv[0] - 1), 0), 0)
    hbm = pl.BlockSpec(memory_space=pl.ANY)
    biases = jnp.stack([b_gate, b_up, b_down], axis=2)
    grid_spec = pltpu.PrefetchScalarGridSpec(
        num_scalar_prefetch=4, grid=(nb,),
        in_specs=[pl.BlockSpec((MOE_BLOCK, ROW_WORDS), rowblk),
                  pl.BlockSpec((None, ne, 3, D), lambda i, *_: (layer, 0, 0, 0)), hbm, hbm, hbm],
        out_specs=pl.BlockSpec((MOE_BLOCK, ROW_WORDS), lambda i, *_: (i, 0)),
        scratch_shapes=[pltpu.VMEM((2, 3, D, ff), F32), pltpu.SemaphoreType.DMA((2, 3))])
    return pl.pallas_call(
        functools.partial(_expert_kernel, layer=layer), grid_spec=grid_spec,
        out_shape=jax.ShapeDtypeStruct((n_rows, ROW_WORDS), U32),
        compiler_params=_cp(("arbitrary",)), name="moe_experts",
    )(blk_expert, n_valid, next_expert, slot, x_rows, biases, w_gate, w_up, w_down)


def moe_layer(lay, layer, x, g2, sc2, sh2, gate2, router_w, router_b, w_gate, b_gate, w_up, b_up, w_down,
              b_down, split=False):
    n_tok = x.shape[0]
    nt = n_tok // TOK_TILE
    rw = jnp.zeros((D, LANES), F32).at[:, :N_EXPERTS].set(router_w)
    rw = _hilo(rw)
    rb = jnp.full((1, LANES), NEG, F32).at[0, :N_EXPERTS].set(router_b)
    h2, _, gates, pos, posT, cnt = moe_router(lay, x, g2, sc2, sh2, rw, rb)
    n_te = cnt[:, 0, :N_EXPERTS].astype(I32)
    n_te = (n_te + SEG_ALIGN - 1) // SEG_ALIGN * SEG_ALIGN
    totals = jnp.sum(n_te, axis=0)
    padded = (totals + MOE_BLOCK - 1) // MOE_BLOCK * MOE_BLOCK
    padded_end = jnp.cumsum(padded)
    pstart = padded_end - padded
    dst = pstart[None, :] + jnp.cumsum(n_te, axis=0) - n_te
    off = jnp.cumsum(n_te, axis=1) - n_te
    n_rows = nt * TILE_ROWS + N_EXPERTS * MOE_BLOCK
    nb = n_rows // MOE_BLOCK
    n_valid = (padded_end[-1] // MOE_BLOCK).astype(I32).reshape(1)
    bstart = jnp.minimum(jnp.arange(nb, dtype=I32), n_valid[0] - 1) * MOE_BLOCK
    blk_expert = jnp.minimum(jnp.sum((bstart[:, None] >= padded_end[None, :]).astype(I32), axis=1),
                             N_EXPERTS - 1).astype(I32)
    tabs = (n_te.reshape(-1).astype(I32), off.reshape(-1).astype(I32), dst.reshape(-1).astype(I32),
            jnp.sum(n_te, axis=1).astype(I32))
    tail_n = jnp.concatenate([(padded - totals).astype(I32), n_valid])
    x_rows = moe_dispatch(*tabs, tail_n, (pstart + totals).astype(I32), posT, h2, n_rows)
    owner = jnp.where(padded > 0, jnp.arange(N_EXPERTS, dtype=I32), N_EXPERTS)
    later = jnp.concatenate([lax.cummin(owner, axis=0, reverse=True)[1:], jnp.full((1,), N_EXPERTS, I32)])
    next_expert = jnp.where(later < N_EXPERTS, later, -1).astype(I32)
    slot = ((jnp.cumsum((padded > 0).astype(I32)) - 1) % 2).astype(I32)
    y_rows = moe_experts(layer, blk_expert, n_valid, next_expert, slot, x_rows, w_gate, b_gate, w_up, b_up,
                         w_down, b_down)
    return moe_combine(lay, *tabs, pos, gates, x, gate2, y_rows, split)


QKV_TN = 256
N_QK_TILES = (ATT_HEADS + ATT_KV) * ATT_HD // QKV_TN


def _qkv_kernel(x_ref, g_ref, sc_ref, sh_ref, w_ref, nw_ref, cos_ref, sin_ref, o_ref):
    tm = x_ref.shape[0]
    h = _modnorm(x_ref[...], g_ref[...], sc_ref[...], sh_ref[...]).astype(BF16)
    r = lax.broadcasted_iota(I32, (QKV_TN, QKV_TN), 0) // ATT_HD
    c = lax.broadcasted_iota(I32, (QKV_TN, QKV_TN), 1) // ATT_HD
    head_mean = jnp.where(r == c, 1.0 / ATT_HD, 0.0).astype(BF16)
    lane = lax.broadcasted_iota(I32, (tm, QKV_TN), 1)
    half = ATT_HD // 4
    first = (lane % (2 * half)) < half
    acc_all = _dot(h, w_ref[...])
    tiles = [acc_all[:, QKV_TN * j:QKV_TN * (j + 1)] for j in range(w_ref.shape[1] // QKV_TN)]
    sq = jnp.concatenate([(t * t).astype(BF16) for t in tiles[:N_QK_TILES]], axis=0)
    ms_all = _dot(sq, head_mean)
    for j, acc in enumerate(tiles):
        cols = slice(QKV_TN * j, QKV_TN * (j + 1))
        if j >= N_QK_TILES:
            o_ref[:, cols] = acc
            continue
        qn = acc * lax.rsqrt(ms_all[tm * j:tm * (j + 1)] + EPS) * nw_ref[j]
        swapped = jnp.where(first, pltpu.roll(qn, QKV_TN - half, 1), pltpu.roll(qn, half, 1))
        o_ref[:, cols] = qn * cos_ref[...] + swapped * sin_ref[...]


def _rope_tables(sample_len):
    pos = np.arange(sample_len)
    half = ATT_HD // 4
    inv = (ROPE_THETA ** (-np.arange(half, dtype=np.float32) / half)).astype(np.float32)
    ang_r = (pos // GRID_W).astype(np.float32)[:, None] * inv[None, :]
    ang_c = (pos % GRID_W).astype(np.float32)[:, None] * inv[None, :]
    cos = np.concatenate([np.cos(ang_r)] * 2 + [np.cos(ang_c)] * 2, axis=1)
    sin = np.concatenate([-np.sin(ang_r), np.sin(ang_r), -np.sin(ang_c), np.sin(ang_c)], axis=1)
    rep = QKV_TN // ATT_HD
    return (jnp.asarray(np.tile(cos, (1, rep)), F32), jnp.asarray(np.tile(sin, (1, rep)), F32))


def qkv_proj(lay, x, g, sc, sh, w, q_norm, k_norm, tm=512):
    n_tok = x.shape[0]
    n = w.shape[1]
    nq = ATT_HEADS * ATT_HD // QKV_TN
    rep = QKV_TN // ATT_HD
    nw = jnp.concatenate([jnp.tile(jnp.tile(q_norm, rep)[None, :], (nq, 1)),
                          jnp.tile(jnp.tile(k_norm, rep)[None, :], (n // QKV_TN - nq, 1))], axis=0)
    cos, sin = _rope_tables(lay.sample_len)
    cos = jnp.concatenate([jnp.ones((tm, QKV_TN), F32), cos], axis=0)
    sin = jnp.concatenate([jnp.zeros((tm, QKV_TN), F32), sin], axis=0)
    assert lay.p_tok % tm == 0 and lay.sample_len % tm == 0
    mrow = lambda i: (lay.mod_row(i * tm), 0, 0)
    rrow = lambda i: (jnp.where(i * tm < lay.p_tok, 0, 1 + ((i * tm - lay.p_tok) % lay.sample_len) // tm), 0)
    nt = n // QKV_TN
    return pl.pallas_call(
        _qkv_kernel, grid=(n_tok // tm,),
        in_specs=[pl.BlockSpec((tm, D), lambda i: (i, 0)),
                  pl.BlockSpec((1, D), lambda i: (0, 0)),
                  pl.BlockSpec((None, 1, D), mrow), pl.BlockSpec((None, 1, D), mrow),
                  pl.BlockSpec((D, n), lambda i: (0, 0)),
                  pl.BlockSpec((nt, 1, QKV_TN), lambda i: (0, 0, 0)),
                  pl.BlockSpec((tm, QKV_TN), rrow), pl.BlockSpec((tm, QKV_TN), rrow)],
        out_specs=pl.BlockSpec((tm, n), lambda i: (i, 0)),
        out_shape=jax.ShapeDtypeStruct((n_tok, n), F32),
        compiler_params=_cp(("arbitrary",)), name="qkv_proj",
    )(x, g.reshape(1, D), sc, sh, w, nw.reshape(nt, 1, QKV_TN), cos, sin)


def _dup_group(x, g):
    blk = x[:, LANES * (g // 2):LANES * (g // 2 + 1)]
    if g % 2 == 1:
        blk = pltpu.roll(blk, ATT_HD, 1)
    lo = lax.broadcasted_iota(I32, blk.shape, 1) < ATT_HD
    low = jnp.where(lo, blk, 0.0)
    return low + pltpu.roll(low, ATT_HD, 1)


def _attend(q_ref, k_all, v_all, mask, sink_ref, o_ref):
    nq = q_ref.shape[0]
    lo = lax.broadcasted_iota(I32, (nq, LANES), 1) < ATT_HD
    first = lax.broadcasted_iota(I32, (2 * nq, 1), 0) < nq
    if mask is not None:
        mask = jnp.concatenate([mask, mask], axis=0)
    pairs_per_group = ATT_HEADS // ATT_KV // 2
    n_pairs = ATT_HEADS // 2
    kv = {}

    def group_kv(g):
        if g not in kv:
            kv[g] = (_dup_group(k_all, g).astype(BF16), _dup_group(v_all, g).astype(BF16))
        return kv[g]

    def scores(j):
        qp = q_ref[:, LANES * j:LANES * (j + 1)] * (ATT_HD ** -0.5)
        qs = jnp.concatenate([jnp.where(lo, qp, 0.0), jnp.where(lo, 0.0, qp)], axis=0)
        return _dot_nt(qs.astype(BF16), group_kv(j // pairs_per_group)[0])

    ahead = 2
    queue = [scores(j) for j in range(min(ahead, n_pairs))]
    for j in range(n_pairs):
        s = queue.pop(0)
        if j + ahead < n_pairs:
            queue.append(scores(j + ahead))
        if mask is not None:
            s = jnp.where(mask, s, NEG)
        sink = jnp.where(first, sink_ref[2 * j], sink_ref[2 * j + 1])
        m = jnp.maximum(jnp.max(s, axis=1, keepdims=True), sink)
        p = jnp.exp(s - m)
        den = jnp.sum(p, axis=1, keepdims=True) + jnp.exp(sink - m)
        o = _dot(p.astype(BF16), group_kv(j // pairs_per_group)[1]) / den
        o_ref[:, LANES * j:LANES * (j + 1)] = jnp.where(lo, o[:nq], o[nq:]).astype(o_ref.dtype)


def _attn_ctx_kernel(sink_ref, q_ref, k_ref, v_ref, o_ref):
    _attend(q_ref, k_ref[...], v_ref[...], None, sink_ref, o_ref)


def attn_context(lay, qkv, sinks):
    qw = ATT_HEADS * ATT_HD
    kw = ATT_KV * ATT_HD
    ln = lay.prompt_len
    grid_spec = pltpu.PrefetchScalarGridSpec(
        num_scalar_prefetch=0, grid=(lay.n_prompt,),
        in_specs=[pl.BlockSpec(memory_space=pltpu.SMEM),
                  pl.BlockSpec((ln, qw), lambda b: (b, 0)),
                  pl.BlockSpec((ln, kw), lambda b: (b, qw // kw)),
                  pl.BlockSpec((ln, kw), lambda b: (b, qw // kw + 1))],
        out_specs=pl.BlockSpec((ln, qw), lambda b: (b, 0)))
    return pl.pallas_call(
        _attn_ctx_kernel, grid_spec=grid_spec,
        out_shape=jax.ShapeDtypeStruct((lay.p_tok, qw), BF16),
        compiler_params=_cp(("arbitrary",)), name="attn_context",
    )(sinks, qkv, qkv, qkv)


def _attn_lat_kernel(sink_ref, q_ref, kp_ref, kc_ref, kn_ref, vp_ref, vc_ref, vn_ref, ck_ref, cv_ref, o_ref,
                     *, nblk):
    i = pl.program_id(1)
    bq = ATT_BLOCK
    nctx = ck_ref.shape[1]
    k_all = jnp.concatenate([kp_ref[...], kc_ref[...], kn_ref[...], ck_ref[0]], axis=0)
    v_all = jnp.concatenate([vp_ref[...], vc_ref[...], vn_ref[...], cv_ref[0]], axis=0)
    ns = 3 * bq + nctx
    r = lax.broadcasted_iota(I32, (bq, ns), 0)
    c = lax.broadcasted_iota(I32, (bq, ns), 1)
    rel = c - r
    first_key = jnp.where(i > 0, 0, bq)
    end_key = jnp.where(i < nblk - 1, 3 * bq, 2 * bq)
    band = (rel >= bq - WINDOW) & (rel <= bq + WINDOW) & (c >= first_key) & (c < end_key)
    mask = band | (c >= 3 * bq)
    _attend(q_ref, k_all, v_all, mask, sink_ref, o_ref)


def attn_latent(lay, qkv, cache_k, cache_v, sinks):
    qw = ATT_HEADS * ATT_HD
    kw = ATT_KV * ATT_HD
    bq = ATT_BLOCK
    nblk = lay.sample_len // bq
    b0 = lay.p_tok // bq
    nctx = cache_k.shape[1]
    rb = lambda b, i: b0 + b * nblk + i
    kspec = lambda cb, sh: pl.BlockSpec(
        (bq, kw), lambda b, i: (b0 + b * nblk + jnp.clip(i + sh, 0, nblk - 1), cb))
    kc, vc = qw // kw, qw // kw + 1
    grid_spec = pltpu.PrefetchScalarGridSpec(
        num_scalar_prefetch=0, grid=(lay.n_sample, nblk),
        in_specs=[pl.BlockSpec(memory_space=pltpu.SMEM),
                  pl.BlockSpec((bq, qw), lambda b, i: (rb(b, i), 0)),
                  kspec(kc, -1), kspec(kc, 0), kspec(kc, 1),
                  kspec(vc, -1), kspec(vc, 0), kspec(vc, 1),
                  pl.BlockSpec((1, nctx, kw), lambda b, i: (b, 0, 0)),
                  pl.BlockSpec((1, nctx, kw), lambda b, i: (b, 0, 0))],
        out_specs=pl.BlockSpec((bq, qw), lambda b, i: (b * nblk + i, 0)))
    return pl.pallas_call(
        functools.partial(_attn_lat_kernel, nblk=nblk), grid_spec=grid_spec,
        out_shape=jax.ShapeDtypeStruct((lay.n_sample * lay.sample_len, qw), BF16),
        compiler_params=_cp(("arbitrary", "arbitrary")), name="attn_latent",
    )(sinks, qkv, qkv, qkv, qkv, qkv, qkv, qkv,
      cache_k.reshape(lay.n_sample, nctx, kw), cache_v.reshape(lay.n_sample, nctx, kw))


def _forward(lay, x_prompt, x_sample, state_l0_ssd_fwd, state_l0_ssd_bwd, state_l0_gla_fwd, state_l0_gla_bwd,
             cache_l1_k, cache_l1_v, c, c_ctx, ada_w, ada_b, norm1, norm2,
             l0_w_in, l0_conv_w, l0_conv_b, l0_a_log, l0_dt_bias, l0_d_skip, l0_ssd_norm,
             l0_gate_w2, l0_gate_b, l0_gla_norm, l0_w_out,
             l1_w_qkv, l1_q_norm, l1_k_norm, l1_sinks, l1_w_out,
             router_w, router_b, exp_w_gate, exp_b_gate, exp_w_up, exp_b_up, exp_w_down, exp_b_down):
    np_, ns = lay.n_prompt, lay.n_sample
    x = (x_prompt.reshape(-1, D), x_sample.reshape(-1, D))
    cond8 = jnp.zeros((8, D), F32).at[0].set(c_ctx).at[1:1 + ns].set(c)
    mod = ada_table(cond8, ada_w, ada_b)
    mods = [[mod[l, :, p * D:(p + 1) * D].reshape(8, 1, D) for p in range(N_ADA)] for l in range(2)]

    def moe(l, xx, split=False):
        return moe_layer(lay, l, xx, norm2[l], mods[l][4], mods[l][3], mods[l][5], router_w[l], router_b[l],
                         exp_w_gate, exp_b_gate, exp_w_up, exp_b_up, exp_w_down, exp_b_down, split=split)

    sp = np.cumsum((SSD_INNER, SSD_INNER + 2 * SSD_GROUPS * SSD_STATE, 2 * SSD_HEADS,
                    GLA_HEADS * GLA_KEY_DIM, GLA_HEADS * GLA_KEY_DIM,
                    GLA_HEADS * GLA_VAL_DIM, GLA_HEADS * GLA_VAL_DIM, 2 * GLA_RANK))
    cols = lambda a, b: l0_w_in[:, a:b]
    w_main = jnp.concatenate([cols(0, sp[0]), cols(sp[4], sp[5]), cols(sp[5], sp[6]), cols(sp[0], sp[1]),
                              cols(sp[2], sp[3]), cols(sp[3], sp[4])], axis=1).astype(BF16)
    w_small = jnp.concatenate([cols(sp[1], sp[2]), cols(sp[6], sp[7]),
                               jnp.zeros((D, LANES - 2 * SSD_HEADS - 2 * GLA_RANK), F32)], axis=1)
    proj, small, small_t = norm_proj(lay, x, norm1[0], mods[0][1], mods[0][0], w_main, _hilo(w_small), 512,
                                      PJ_W // 2, BF16, SSD_CHUNK)
    y_n, o_n, (ssd_f, ssd_b, gla_f, gla_b) = l0_mixers(
        lay, proj, small, small_t, state_l0_ssd_fwd, state_l0_ssd_bwd, state_l0_gla_fwd, state_l0_gla_bwd,
        l0_conv_w, l0_conv_b, l0_a_log, l0_dt_bias, l0_d_skip, l0_ssd_norm,
        l0_gate_w2, l0_gate_b, l0_gla_norm)
    x = proj_residual(lay, [y_n, o_n], l0_w_out.astype(BF16), x, mods[0][2])
    x = moe(0, x)

    qkv = qkv_proj(lay, x, norm1[1], mods[1][1], mods[1][0], l1_w_qkv.astype(BF16), l1_q_norm, l1_k_norm)
    o_ctx = attn_context(lay, qkv, l1_sinks)
    o_lat = attn_latent(lay, qkv, cache_l1_k, cache_l1_v, l1_sinks)
    x = proj_residual(lay, [(o_ctx, o_lat)], l1_w_out.astype(BF16), x, mods[1][2])
    xp, xs = moe(1, x, split=True)

    qw = ATT_HEADS * ATT_HD
    kw = ATT_KV * ATT_HD
    return (xp.reshape(x_prompt.shape), xs.reshape(x_sample.shape),
            ssd_f[:np_].reshape(np_, SSD_HEADS, SSD_HEAD_DIM, SSD_STATE),
            ssd_b[:np_].reshape(np_, SSD_HEADS, SSD_HEAD_DIM, SSD_STATE),
            gla_f[:np_], gla_b[:np_],
            qkv[:lay.p_tok, qw:qw + kw].reshape(np_, lay.prompt_len, ATT_KV, ATT_HD),
            qkv[:lay.p_tok, qw + kw:].reshape(np_, lay.prompt_len, ATT_KV, ATT_HD))


def kernel(x_prompt, x_sample, state_l0_ssd_fwd, state_l0_ssd_bwd, state_l0_gla_fwd, state_l0_gla_bwd, cache_l1_k, cache_l1_v, c, c_ctx, ada_w, ada_b, norm1, norm2, l0_w_in, l0_conv_w, l0_conv_b, l0_a_log, l0_dt_bias, l0_d_skip, l0_ssd_norm, l0_gate_w2, l0_gate_b, l0_gla_norm, l0_w_out, l1_w_qkv, l1_q_norm, l1_k_norm, l1_sinks, l1_w_out, router_w, router_b, exp_w_gate, exp_b_gate, exp_w_up, exp_b_up, exp_w_down, exp_b_down):
    lay = Layout(x_prompt.shape[0], x_prompt.shape[1], x_sample.shape[0], x_sample.shape[1])
    return _forward(lay, x_prompt, x_sample, state_l0_ssd_fwd, state_l0_ssd_bwd, state_l0_gla_fwd,
                    state_l0_gla_bwd, cache_l1_k, cache_l1_v, c, c_ctx, ada_w, ada_b, norm1, norm2,
                    l0_w_in, l0_conv_w, l0_conv_b, l0_a_log, l0_dt_bias, l0_d_skip, l0_ssd_norm,
                    l0_gate_w2, l0_gate_b, l0_gla_norm, l0_w_out,
                    l1_w_qkv, l1_q_norm, l1_k_norm, l1_sinks, l1_w_out,
                    router_w, router_b, exp_w_gate, exp_b_gate, exp_w_up, exp_b_up, exp_w_down, exp_b_down)
```

```python
import functools
import math

import numpy as np
import jax
import jax.numpy as jnp
from jax import lax
from jax.experimental import pallas as pl
from jax.experimental.pallas import tpu as pltpu

F32 = jnp.float32
BF16 = jnp.bfloat16
I32 = jnp.int32
HI = lax.Precision.HIGHEST

D = 1024
EPS = 1e-6
N_ADA = 6
SSD_HEADS = 16
SSD_HEAD_DIM = 64
SSD_INNER = 1024
SSD_STATE = 128
SSD_GROUPS = 2
SSD_CONV = 5
SSD_CHUNK = 128
GLA_HEADS = 4
GLA_KEY_DIM = 128
GLA_VAL_DIM = 256
GLA_RANK = 16
GLA_TAU = 16.0
GLA_BLOCK = 64
ATT_HEADS = 16
ATT_KV = 4
ATT_HD = 64
ATT_BLOCK = 128
WINDOW = 128
GRID_W = 64
ROPE_THETA = 10000.0
N_EXPERTS = 32
TOP_K = 4
EXPERT_FF = 1024
SWIGLU_LIMIT = 7.0
SWIGLU_ALPHA = 1.702
MOE_BLOCK = 256
TOK_TILE = 256
LANES = 128
NEG = -1e30

PJ_Z, PJ_V, PJ_OG, PJ_XBC, PJ_Q, PJ_K = 0, 1024, 2048, 3072, 4608, 5120
PJ_W = 5632
VMEM_LIMIT = 48 * 1024 * 1024


def _cp(sem, vmem=VMEM_LIMIT):
    return pltpu.CompilerParams(dimension_semantics=sem, vmem_limit_bytes=vmem)


class Layout:
    def __init__(self, n_prompt, prompt_len, n_sample, sample_len):
        self.n_prompt, self.prompt_len = n_prompt, prompt_len
        self.n_sample, self.sample_len = n_sample, sample_len
        self.p_tok = n_prompt * prompt_len
        self.n_tok = self.p_tok + n_sample * sample_len
        self.seqs = [(i * prompt_len, prompt_len) for i in range(n_prompt)]
        self.seqs += [(self.p_tok + i * sample_len, sample_len) for i in range(n_sample)]
        self.n_seq = len(self.seqs)

    def mod_row(self, start):
        return jnp.where(start < self.p_tok, 0, 1 + (start - self.p_tok) // self.sample_len)

def _sigmoid(x):
    return 1.0 / (1.0 + jnp.exp(-x))


def _silu(x):
    return x * _sigmoid(x)


def _softplus(x):
    return jnp.maximum(x, 0.0) + jnp.log(1.0 + jnp.exp(-jnp.abs(x)))


def _modnorm(x, g, sc, sh):
    ms = jnp.mean(x * x, axis=-1, keepdims=True)
    return (x * lax.rsqrt(ms + EPS) * g) * (1.0 + sc) + sh


def _dot(a, b, **kw):
    return jnp.dot(a, b, preferred_element_type=F32, **kw)


def _dot_nt(a, b):
    return lax.dot_general(a, b, (((1,), (1,)), ((), ())), preferred_element_type=F32)


def _dot_tn(a, b):
    return lax.dot_general(a, b, (((0,), (0,)), ((), ())), preferred_element_type=F32)


def _split(x, n):
    parts = []
    for _ in range(n):
        p = x.astype(BF16)
        parts.append(p)
        x = x - p.astype(F32)
    return parts


def _dot_sel(sel, x):
    sel = sel.astype(BF16)
    return sum(_dot(sel, p) for p in _split(x, 3))


def _dot_sel_r(x, sel):
    sel = sel.astype(BF16)
    return sum(_dot(p, sel) for p in _split(x, 3))


def _dot_hilo(x, w_hi, w_lo):
    x_hi, x_lo = _split(x, 2)
    return _dot(x_hi, w_hi) + _dot(x_lo, w_hi) + _dot(x_hi, w_lo)


def _hilo(w):
    hi = w.astype(BF16)
    return jnp.stack([hi, (w - hi.astype(F32)).astype(BF16)])


U32 = jnp.uint32
ROW_WORDS = D // 2
_HI_MASK = 0xFFFF0000


def _pack_rows(x):
    lo = lax.bitcast_convert_type(x[:, :ROW_WORDS], U32) >> 16
    hi = lax.bitcast_convert_type(x[:, ROW_WORDS:], U32) & jnp.uint32(_HI_MASK)
    return lo | hi


def _unpack_rows(u):
    lo = lax.bitcast_convert_type(u << 16, F32)
    hi = lax.bitcast_convert_type(u & jnp.uint32(_HI_MASK), F32)
    return jnp.concatenate([lo, hi], axis=1)


def _ada_kernel(c_ref, w_ref, b_ref, o_ref):
    o_ref[0] = _dot(_silu(c_ref[...]), w_ref[0], precision=HI) + b_ref[0]


def ada_table(cond8, ada_w, ada_b):
    depth, _, n = ada_w.shape
    tn = 1536
    return pl.pallas_call(
        _ada_kernel, grid=(depth, n // tn),
        in_specs=[pl.BlockSpec((8, D), lambda l, j: (0, 0)),
                  pl.BlockSpec((1, D, tn), lambda l, j: (l, 0, j)),
                  pl.BlockSpec((1, 1, tn), lambda l, j: (l, 0, j))],
        out_specs=pl.BlockSpec((1, 8, tn), lambda l, j: (l, 0, j)),
        out_shape=jax.ShapeDtypeStruct((depth, 8, n), F32),
        compiler_params=_cp(("arbitrary", "arbitrary")), name="ada_table",
    )(cond8, ada_w, ada_b.reshape(depth, 1, n))


def _stream_specs(lay, stream, tm):
    if not isinstance(stream, (tuple, list)):
        return [stream], [pl.BlockSpec((tm, stream.shape[1]), lambda i, *_: (i, 0))]
    assert lay.p_tok % tm == 0
    npt = lay.p_tok // tm
    w = stream[0].shape[1]
    return list(stream), [pl.BlockSpec((tm, w), lambda i, *_: (jnp.minimum(i, npt - 1), 0)),
                          pl.BlockSpec((tm, w), lambda i, *_: (jnp.maximum(i - npt, 0), 0))]


def _stream_tile(refs, in_prompt):
    if len(refs) == 1:
        return refs[0][...]
    return jnp.where(in_prompt, refs[0][...], refs[1][...])


def _proj_kernel(*refs, nx, npt):
    x_refs = refs[:nx]
    g_ref, sc_ref, sh_ref, w_ref, ws_ref, o_ref, os_ref, ost_ref, h_scr = refs[nx:]

    @pl.when(pl.program_id(1) == 0)
    def _():
        x = _stream_tile(x_refs, pl.program_id(0) < npt)
        h = _modnorm(x, g_ref[...], sc_ref[...], sh_ref[...])
        h_scr[...] = h.astype(BF16)
        small = _dot_hilo(h, ws_ref[0], ws_ref[1])
        os_ref[...] = small
        q = ost_ref.shape[2]
        for c in range(ost_ref.shape[0]):
            ost_ref[c] = small[q * c:q * (c + 1), :].T

    o_ref[...] = _dot(h_scr[...], w_ref[...]).astype(o_ref.dtype)


def norm_proj(lay, x, g, sc, sh, w, w_small, tm, tn, out_dtype, chunk):
    n_tok = lay.n_tok
    n = w.shape[1]
    ns = w_small.shape[-1]
    mrow = lambda i, j: (lay.mod_row(i * tm), 0, 0)
    xs, x_specs = _stream_specs(lay, x, tm)
    return pl.pallas_call(
        functools.partial(_proj_kernel, nx=len(xs), npt=lay.p_tok // tm), grid=(n_tok // tm, n // tn),
        in_specs=x_specs + [pl.BlockSpec((1, D), lambda i, j: (0, 0)),
                            pl.BlockSpec((None, 1, D), mrow),
                            pl.BlockSpec((None, 1, D), mrow),
                            pl.BlockSpec((D, tn), lambda i, j: (0, j)),
                            pl.BlockSpec((2, D, ns), lambda i, j: (0, 0, 0))],
        out_specs=[pl.BlockSpec((tm, tn), lambda i, j: (i, j)),
                   pl.BlockSpec((tm, ns), lambda i, j: (i, 0)),
                   pl.BlockSpec((tm // chunk, ns, chunk), lambda i, j: (i, 0, 0))],
        out_shape=[jax.ShapeDtypeStruct((n_tok, n), out_dtype),
                   jax.ShapeDtypeStruct((n_tok, ns), F32),
                   jax.ShapeDtypeStruct((n_tok // chunk, ns, chunk), F32)],
        scratch_shapes=[pltpu.VMEM((tm, D), BF16)],
        compiler_params=_cp(("arbitrary", "arbitrary")), name="norm_proj",
    )(*xs, g.reshape(1, D), sc, sh, w, w_small)


CONV_HALO = 16


def _ssd_load(d, c, xc, dtg_ref, dtgT_ref, S):
    q = SSD_CHUNK
    nh = SSD_HEADS
    rows = pl.ds(pl.multiple_of(c * q, q), q)
    return (rows, xc[rows, 0:SSD_INNER], xc[rows, SSD_INNER:SSD_INNER + 2 * SSD_GROUPS * SSD_STATE],
            dtg_ref[rows, nh * d:nh * d + nh], dtgT_ref[c, nh * d:nh * d + nh, :], S[...])


def _ssd_chunks(loaded, alog_ref, alogT_ref, dtb_ref, dtbT_ref):
    q = SSD_CHUNK
    nh = SSD_HEADS
    hp = nh * SSD_HEAD_DIM
    row = lax.broadcasted_iota(I32, (q, q), 0)
    col = lax.broadcasted_iota(I32, (q, q), 1)
    lo = lax.broadcasted_iota(I32, (q, LANES), 1) < SSD_HEAD_DIM
    head_of = lax.broadcasted_iota(I32, (nh, hp), 1) // SSD_HEAD_DIM
    spread = (lax.broadcasted_iota(I32, (nh, hp), 0) == head_of).astype(BF16)
    rep = (nh // SSD_GROUPS) // 2
    pairs = [(g, j) for g in range(SSD_GROUPS) for j in range(rep * g, rep * (g + 1))]
    lane = lambda j: slice(LANES * j, LANES * (j + 1))

    def per_lane(arr, passes):
        return sum(_dot(p, spread) for p in _split(arr, passes))

    stage1 = []
    for d, (_, _, _, dtg, dtgT, _) in enumerate(loaded):
        dt = _softplus(dtg + dtb_ref[d:d + 1, :])
        dtT = _softplus(dtgT + dtbT_ref[:, d:d + 1])
        ad = dt * (-jnp.exp(alog_ref[d:d + 1, :]))
        adT = dtT * (-jnp.exp(alogT_ref[:, d:d + 1]))
        if d == 0:
            e, eT = _dot_sel(col <= row, ad), _dot_sel_r(adT, row <= col)
        else:
            e, eT = _dot_sel(col < row, ad), _dot_sel_r(adT, row < col)
        stage1.append((dt, ad, e, eT))
    stage2 = []
    for d, ((_, xs, bc, _, _, s), (dt, ad, e, eT)) in enumerate(zip(loaded, stage1)):
        if d == 0:
            tot = e[q - 1:q, :]
            fq, fk = jnp.exp(e), jnp.exp(tot - e)
        else:
            tot = jnp.sum(ad, axis=0, keepdims=True)
            fq, fk = jnp.exp(tot - e), jnp.exp(e)
        dt_x, fq_x, fk_x = per_lane(dt, 1), per_lane(fq, 1), per_lane(fk, 1)
        dec_x = per_lane(jnp.broadcast_to(jnp.exp(tot), (8, nh)), 3)[0:1, :]
        xdt_all = xs.astype(F32) * dt_x
        xk_all = (xdt_all * fk_x).astype(BF16)
        bg_t = [bc[:, SSD_STATE * g:SSD_STATE * (g + 1)].T for g in range(SSD_GROUPS)]
        cg = [bc[:, SSD_STATE * (SSD_GROUPS + g):SSD_STATE * (SSD_GROUPS + g + 1)] for g in range(SSD_GROUPS)]
        gmat = [_dot(cg[g], bg_t[g]) for g in range(SSD_GROUPS)]
        carried = [_dot(cg[g], s[:, lane(j)].astype(BF16)) * fq_x[:, lane(j)] for g, j in pairs]
        s_new = [s[:, lane(j)] * dec_x[:, lane(j)] + _dot(bg_t[g], xk_all[:, lane(j)]) for g, j in pairs]
        stage2.append((xdt_all, gmat, carried, s_new))
    results = []
    for d, ((_, _, e, eT), (xdt_all, gmat, carried, s_new)) in enumerate(zip(stage1, stage2)):
        mask = row >= col if d == 0 else col >= row
        ys = []
        for (g, j), off in zip(pairs, carried):
            parts = []
            for hh in (2 * j, 2 * j + 1):
                diff = e[:, hh:hh + 1] - eT[hh:hh + 1, :] if d == 0 else eT[hh:hh + 1, :] - e[:, hh:hh + 1]
                parts.append((gmat[g] * jnp.exp(jnp.where(mask, diff, NEG))).astype(BF16))
            lhs = jnp.concatenate(parts, axis=1)
            xdt = xdt_all[:, lane(j)]
            rhs = jnp.concatenate([jnp.where(lo, xdt, 0.0), jnp.where(lo, 0.0, xdt)], axis=0)
            ys.append(_dot(lhs, rhs.astype(BF16)) + off)
        results.append((jnp.concatenate(ys, axis=1), jnp.concatenate(s_new, axis=1)))
    return results


def _ssd_seq_kernel(*refs, has_init):
    (xbc_ref, z_ref, dtg_ref, dtgT_ref, cw_ref, cb_ref,
     alog_ref, alogT_ref, dtb_ref, dtbT_ref, dskip_ref, nrm_ref) = refs[:12]
    refs = refs[12:]
    if has_init:
        s0f_ref, s0b_ref = refs[:2]
        refs = refs[2:]
    y_ref, sf_ref, sb_ref, xc, ext, Sf, Sb, yf, yb = refs
    ln = xbc_ref.shape[0]
    q = SSD_CHUNK
    nc = ln // q
    h = CONV_HALO
    pad = SSD_CONV // 2

    def conv_body(c, carry):
        r0 = pl.multiple_of(c * q, q)
        prev = xbc_ref[pl.ds(pl.multiple_of(jnp.maximum(r0 - h, 0), h), h), :].astype(F32)
        nxt = xbc_ref[pl.ds(pl.multiple_of(jnp.minimum(r0 + q, ln - h), h), h), :].astype(F32)
        ext[0:h, :] = jnp.where(c > 0, prev, 0.0)
        ext[h:h + q, :] = xbc_ref[pl.ds(r0, q), :].astype(F32)
        ext[h + q:h + q + h, :] = jnp.where(c < nc - 1, nxt, 0.0)
        acc = jnp.broadcast_to(cb_ref[...], (q, cb_ref.shape[1]))
        for k in range(SSD_CONV):
            acc = acc + cw_ref[k:k + 1, :] * ext[h - pad + k:h - pad + k + q, :]
        xc[pl.ds(r0, q), :] = _silu(acc).astype(xc.dtype)
        return carry
    lax.fori_loop(0, nc, conv_body, 0)

    if has_init:
        Sf[...] = s0f_ref[0].T
        Sb[...] = s0b_ref[0].T
    else:
        Sf[...] = jnp.zeros_like(Sf)
        Sb[...] = jnp.zeros_like(Sb)
    params = (alog_ref, alogT_ref, dtb_ref, dtbT_ref)

    def scan_body(c, carry):
        lf = _ssd_load(0, c, xc, dtg_ref, dtgT_ref, Sf)
        lb = _ssd_load(1, nc - 1 - c, xc, dtg_ref, dtgT_ref, Sb)
        (y_f, s_f), (y_b, s_b) = _ssd_chunks((lf, lb), *params)
        yf[lf[0], :] = y_f
        yb[lb[0], :] = y_b
        Sf[...] = s_f
        Sb[...] = s_b
        return carry
    lax.fori_loop(0, nc, scan_body, 0, unroll=2)
    sf_ref[0] = Sf[...].T
    sb_ref[0] = Sb[...].T

    def out_body(c, carry):
        rows = pl.ds(pl.multiple_of(c * q, q), q)
        ytot = yf[rows, :] + yb[rows, :] + dskip_ref[...] * xc[rows, 0:SSD_INNER].astype(F32)
        yg = ytot * _silu(z_ref[rows, :].astype(F32))
        ms = jnp.mean(yg * yg, axis=-1, keepdims=True)
        y_ref[rows, :] = (yg * lax.rsqrt(ms + EPS) * nrm_ref[...]).astype(y_ref.dtype)
        return carry
    lax.fori_loop(0, nc, out_body, 0)


def _ssd_call(n_seq, ln, blk0, proj, small, smallT3, init, params):
    q = SSD_CHUNK
    hp = SSD_HEADS * SSD_HEAD_DIM
    cw = SSD_INNER + 2 * SSD_GROUPS * SSD_STATE
    nc = ln // q
    assert PJ_XBC % cw == 0 and PJ_Z % SSD_INNER == 0
    tok = lambda w, cb: pl.BlockSpec((ln, w), lambda b: (blk0 + b, cb))
    seq3 = pl.BlockSpec((1, hp, SSD_STATE), lambda b: (b, 0, 0))
    full = lambda a: pl.BlockSpec(a.shape, lambda b: (0,) * a.ndim)
    init = () if init is None else tuple(init)
    return pl.pallas_call(
        functools.partial(_ssd_seq_kernel, has_init=bool(init)), grid=(n_seq,),
        in_specs=[tok(cw, PJ_XBC // cw), tok(SSD_INNER, PJ_Z // SSD_INNER), tok(LANES, 0),
                  pl.BlockSpec((nc, 2 * SSD_HEADS, q), lambda b: (blk0 + b, 0, 0))]
        + [full(a) for a in params] + [seq3] * len(init),
        out_specs=[pl.BlockSpec((ln, hp), lambda b: (b, 0)), seq3, seq3],
        out_shape=[jax.ShapeDtypeStruct((n_seq * ln, hp), BF16),
                   jax.ShapeDtypeStruct((n_seq, hp, SSD_STATE), F32),
                   jax.ShapeDtypeStruct((n_seq, hp, SSD_STATE), F32)],
        scratch_shapes=[pltpu.VMEM((ln, cw), BF16), pltpu.VMEM((q + 2 * CONV_HALO, cw), F32),
                        pltpu.VMEM((SSD_STATE, hp), F32), pltpu.VMEM((SSD_STATE, hp), F32),
                        pltpu.VMEM((ln, hp), F32), pltpu.VMEM((ln, hp), F32)],
        compiler_params=_cp(("arbitrary",)), name="ssd_seq",
    )(proj, proj, small, smallT3, *params, *init)


def _gla_gates(c, q_ref, k_ref, glr_ref, w2_ref, gb_ref, qi, ki, qo, kk, dec):
    t = GLA_BLOCK
    rows = pl.ds(pl.multiple_of(c * t, t), t)
    c0 = 2 * SSD_HEADS
    gps = [_dot_hilo(glr_ref[rows, c0 + GLA_RANK * d:c0 + GLA_RANK * (d + 1)], w2_ref[0, d], w2_ref[1, d])
           + gb_ref[d:d + 1, :] for d in (0, 1)]
    las = [-_softplus(-gp) * (1.0 / GLA_TAU) for gp in gps]
    row = lax.broadcasted_iota(I32, (t, t), 0)
    col = lax.broadcasted_iota(I32, (t, t), 1)
    es = [_dot_sel(col <= row, las[0]), _dot_sel(col < row, las[1])]
    mid = t // 2 - 1
    qf = q_ref[rows, :].astype(F32) * (GLA_KEY_DIM ** -0.5)
    kf = k_ref[rows, :].astype(F32)
    for d in (0, 1):
        e = es[d]
        r = e[mid:mid + 1, :]
        if d == 0:
            tot = e[t - 1:t, :]
            fqi, fki = jnp.exp(e - r), jnp.exp(r - e)
            fq, fk = jnp.exp(e), jnp.exp(tot - e)
        else:
            tot = e[t - 1:t, :] + las[1][t - 1:t, :]
            fqi, fki = jnp.exp(r - e), jnp.exp(e - r)
            fq, fk = jnp.exp(tot - e), jnp.exp(e)
        qi[d, rows, :] = (qf * fqi).astype(BF16)
        ki[d, rows, :] = (kf * fki).astype(BF16)
        qo[d, rows, :] = (qf * fq).astype(BF16)
        kk[d, rows, :] = (kf * fk).astype(BF16)
        dec[d, c] = jnp.broadcast_to(jnp.exp(tot), (8, tot.shape[1]))


def _gla_load(d, c, v_ref, qi, ki, qo, kk, dec, S):
    t = GLA_BLOCK
    rows = pl.ds(pl.multiple_of(c * t, t), t)
    return (rows, qi[d, rows, :], ki[d, rows, :], qo[d, rows, :], kk[d, rows, :], dec[d, c][0:1, :],
            v_ref[rows, :], [S[h] for h in range(GLA_HEADS)])


def _gla_blocks(loaded):
    t = GLA_BLOCK
    dk, dv = GLA_KEY_DIM, GLA_VAL_DIM
    row = lax.broadcasted_iota(I32, (t, t), 0)
    col = lax.broadcasted_iota(I32, (t, t), 1)
    heads = range(GLA_HEADS)
    ks = [slice(dk * h, dk * (h + 1)) for h in heads]
    first = []
    for _, q_in, k_in, q_st, k_st, dec, v, states in loaded:
        vs = [v[:, dv * h:dv * (h + 1)] for h in heads]
        scores = [_dot_nt(q_in[:, ks[h]], k_in[:, ks[h]]) for h in heads]
        carried = [_dot_nt(q_st[:, ks[h]], states[h].astype(BF16)) for h in heads]
        grown = [_dot_tn(vs[h], k_st[:, ks[h]]) for h in heads]
        first.append((vs, scores, carried, grown))
    results = []
    for d, ((_, _, _, _, _, dec, _, states), (vs, scores, carried, grown)) in enumerate(zip(loaded, first)):
        mask = row >= col if d == 0 else col >= row
        outs = [_dot(jnp.where(mask, scores[h], 0.0).astype(BF16), vs[h]) + carried[h] for h in heads]
        new_states = [states[h] * dec[:, ks[h]] + grown[h] for h in heads]
        results.append((jnp.concatenate(outs, axis=1), new_states))
    return results


def _gla_seq_kernel(*refs, has_init):
    q_ref, k_ref, v_ref, og_ref, glr_ref, w2_ref, gb_ref, nrm_ref = refs[:8]
    refs = refs[8:]
    if has_init:
        s0f_ref, s0b_ref = refs[:2]
        refs = refs[2:]
    o_ref, sf_ref, sb_ref, Sf, Sb, of, ob, qi, ki, qo, kk, dec = refs
    ln = q_ref.shape[0]
    t = GLA_BLOCK
    nc = ln // t
    dv = GLA_VAL_DIM
    for h in range(GLA_HEADS):
        if has_init:
            Sf[h] = s0f_ref[0, h].T
            Sb[h] = s0b_ref[0, h].T
        else:
            Sf[h] = jnp.zeros(Sf.shape[1:], F32)
            Sb[h] = jnp.zeros(Sb.shape[1:], F32)
    staged = (qi, ki, qo, kk, dec)

    def gate_body(c, carry):
        _gla_gates(c, q_ref, k_ref, glr_ref, w2_ref, gb_ref, *staged)
        return carry
    lax.fori_loop(0, nc, gate_body, 0, unroll=2)

    def scan_body(c, carry):
        lf = _gla_load(0, c, v_ref, *staged, Sf)
        lb = _gla_load(1, nc - 1 - c, v_ref, *staged, Sb)
        (o_f, s_f), (o_b, s_b) = _gla_blocks((lf, lb))
        of[lf[0], :] = o_f
        ob[lb[0], :] = o_b
        for h in range(GLA_HEADS):
            Sf[h] = s_f[h]
            Sb[h] = s_b[h]
        return carry
    lax.fori_loop(0, nc, scan_body, 0, unroll=2)
    for h in range(GLA_HEADS):
        sf_ref[0, h] = Sf[h].T
        sb_ref[0, h] = Sb[h].T

    def out_body(c, carry):
        rows = pl.ds(pl.multiple_of(c * t, t), t)
        for h in range(GLA_HEADS):
            vl = slice(dv * h, dv * (h + 1))
            ot = of[rows, vl] + ob[rows, vl]
            ms = jnp.mean(ot * ot, axis=-1, keepdims=True)
            on = ot * lax.rsqrt(ms + EPS) * nrm_ref[...]
            o_ref[rows, vl] = (on * _silu(og_ref[rows, vl].astype(F32))).astype(o_ref.dtype)
        return carry
    lax.fori_loop(0, nc, out_body, 0)


def _gla_call(n_seq, ln, blk0, proj, small, init, params):
    qk_w = GLA_HEADS * GLA_KEY_DIM
    v_w = GLA_HEADS * GLA_VAL_DIM
    tok = lambda w, cb: pl.BlockSpec((ln, w), lambda b: (blk0 + b, cb))
    seq4 = pl.BlockSpec((1, GLA_HEADS, GLA_KEY_DIM, GLA_VAL_DIM), lambda b: (b, 0, 0, 0))
    full = lambda a: pl.BlockSpec(a.shape, lambda b: (0,) * a.ndim)
    st_shape = jax.ShapeDtypeStruct((n_seq, GLA_HEADS, GLA_KEY_DIM, GLA_VAL_DIM), F32)
    init = () if init is None else tuple(init)
    return pl.pallas_call(
        functools.partial(_gla_seq_kernel, has_init=bool(init)), grid=(n_seq,),
        in_specs=[tok(qk_w, PJ_Q // qk_w), tok(qk_w, PJ_K // qk_w), tok(v_w, PJ_V // v_w),
                  tok(v_w, PJ_OG // v_w), tok(LANES, 0)] + [full(a) for a in params] + [seq4] * len(init),
        out_specs=[pl.BlockSpec((ln, v_w), lambda b: (b, 0)), seq4, seq4],
        out_shape=[jax.ShapeDtypeStruct((n_seq * ln, v_w), BF16), st_shape, st_shape],
        scratch_shapes=[pltpu.VMEM((GLA_HEADS, GLA_VAL_DIM, GLA_KEY_DIM), F32),
                        pltpu.VMEM((GLA_HEADS, GLA_VAL_DIM, GLA_KEY_DIM), F32),
                        pltpu.VMEM((ln, v_w), F32), pltpu.VMEM((ln, v_w), F32)]
        + [pltpu.VMEM((2, ln, qk_w), BF16)] * 4 + [pltpu.VMEM((2, ln // GLA_BLOCK, 8, qk_w), F32)],
        compiler_params=_cp(("arbitrary",)), name="gla_seq",
    )(proj, proj, proj, proj, small, *params, *init)


def l0_mixers(lay, proj, small, small_t, ssd_f0, ssd_b0, gla_f0, gla_b0, conv_w, conv_b, a_log, dt_bias, d_skip,
              ssd_norm, gate_w2, gate_b, gla_norm):
    hp = SSD_HEADS * SSD_HEAD_DIM
    ssd_p = (conv_w, conv_b.reshape(1, -1), a_log, a_log.T, dt_bias, dt_bias.T,
             jnp.repeat(d_skip, SSD_HEAD_DIM).reshape(1, hp), ssd_norm.reshape(1, hp))
    gla_p = (_hilo(gate_w2), gate_b, gla_norm.reshape(1, -1))
    np_, ns = lay.n_prompt, lay.n_sample
    assert lay.p_tok % lay.sample_len == 0
    groups = [(np_, lay.prompt_len, 0, None, None),
              (ns, lay.sample_len, lay.p_tok // lay.sample_len,
               (ssd_f0.reshape(ns, hp, SSD_STATE), ssd_b0.reshape(ns, hp, SSD_STATE)), (gla_f0, gla_b0))]
    ys, os_, states = [], [], None
    for n, ln, blk0, ssd_init, gla_init in groups:
        y, sf, sb = _ssd_call(n, ln, blk0, proj, small, small_t, ssd_init, ssd_p)
        o, gf, gb = _gla_call(n, ln, blk0, proj, small, gla_init, gla_p)
        ys.append(y)
        os_.append(o)
        if states is None:
            states = (sf, sb, gf, gb)
    return tuple(ys), tuple(os_), states


def _res_kernel(*refs, counts, ks, npt):
    in_prompt = pl.program_id(0) < npt
    streams, pos = [], 0
    for c in counts:
        streams.append(refs[pos:pos + c])
        pos += c
    w_ref, gate_ref, o_ref = refs[pos:]
    acc = None
    off = 0
    for a_refs, k in zip(streams[:-1], ks):
        part = _dot(_stream_tile(a_refs, in_prompt), w_ref[off:off + k, :])
        acc = part if acc is None else acc + part
        off += k
    o_ref[...] = _stream_tile(streams[-1], in_prompt) + gate_ref[...] * acc


def proj_residual(lay, acts, w, x, gate, tm=512):
    arrays, specs, counts = [], [], []
    for s in list(acts) + [x]:
        a, sp = _stream_specs(lay, s, tm)
        arrays += a
        specs += sp
        counts.append(len(a))
    ks = tuple(int((a[0] if isinstance(a, (tuple, list)) else a).shape[1]) for a in acts)
    mrow = lambda i: (lay.mod_row(i * tm), 0, 0)
    return pl.pallas_call(
        functools.partial(_res_kernel, counts=tuple(counts), ks=ks, npt=lay.p_tok // tm),
        grid=(lay.n_tok // tm,),
        in_specs=specs + [pl.BlockSpec(w.shape, lambda i: (0, 0)), pl.BlockSpec((None, 1, D), mrow)],
        out_specs=pl.BlockSpec((tm, D), lambda i: (i, 0)),
        out_shape=jax.ShapeDtypeStruct((lay.n_tok, D), F32),
        compiler_params=_cp(("arbitrary",)), name="proj_residual",
    )(*arrays, w, gate)


def _router_kernel(x_ref, g_ref, sc_ref, sh_ref, rw_ref, rb_ref,
                   h_ref, idx_ref, gate_ref, pos_ref, posT_ref, cnt_ref):
    tm = x_ref.shape[0]
    h = _modnorm(x_ref[...], g_ref[...], sc_ref[...], sh_ref[...])
    h_hi = h.astype(BF16)
    h_ref[...] = h_hi
    h_lo = (h - h_hi.astype(F32)).astype(BF16)
    lg = (_dot(h_hi, rw_ref[0]) + _dot(h_lo, rw_ref[0]) + _dot(h_hi, rw_ref[1])
          + rb_ref[...])
    lane = lax.broadcasted_iota(I32, (tm, LANES), 1).astype(F32)
    vals, ids = [], []
    for _ in range(TOP_K):
        m = jnp.max(lg, axis=1, keepdims=True)
        i = jnp.min(jnp.where(lg == m, lane, float(LANES)), axis=1, keepdims=True)
        vals.append(m)
        ids.append(i)
        lg = jnp.where(lane == i, -jnp.inf, lg)
    ex = [jnp.exp(v - vals[0]) for v in vals]
    den = ex[0] + ex[1] + ex[2] + ex[3]
    sel = jnp.zeros((tm, LANES), F32)
    for i in ids:
        sel = sel + (lane == i).astype(F32)
    row = lax.broadcasted_iota(I32, (tm, tm), 0)
    col = lax.broadcasted_iota(I32, (tm, tm), 1)
    before = _dot((col < row).astype(BF16), sel.astype(BF16))
    n = jnp.sum(sel, axis=0, keepdims=True)
    er = lax.broadcasted_iota(I32, (LANES, LANES), 0)
    ec = lax.broadcasted_iota(I32, (LANES, LANES), 1)
    n_al = jnp.ceil(n * (1.0 / SEG_ALIGN)) * SEG_ALIGN
    offs = _dot(jnp.broadcast_to(n_al, (8, LANES)).astype(BF16), (er < ec).astype(BF16))[0:1, :]
    slot = before + offs
    idx_o = jnp.zeros((tm, LANES), F32)
    gate_o = jnp.zeros((tm, LANES), F32)
    pos_o = jnp.zeros((tm, LANES), F32)
    for k in range(TOP_K):
        p = jnp.sum(jnp.where(lane == ids[k], slot, 0.0), axis=1, keepdims=True)
        idx_o = jnp.where(lane == k, ids[k], idx_o)
        gate_o = jnp.where(lane == k, ex[k] / den, gate_o)
        pos_o = jnp.where(lane == k, p, pos_o)
    idx_ref[...] = idx_o.astype(I32)
    gate_ref[...] = gate_o
    pos_ref[...] = pos_o.astype(I32)
    posT_ref[...] = pos_o.T[0:8, :]
    cnt_ref[0] = jnp.broadcast_to(n, (8, LANES))


def moe_router(lay, x, g, sc, sh, rw, rb):
    n_tok = x.shape[0]
    tm = TOK_TILE
    nt = n_tok // tm
    mrow = lambda i: (lay.mod_row(i * tm), 0, 0)
    tile = lambda w, dt: (pl.BlockSpec((tm, w), lambda i: (i, 0)), jax.ShapeDtypeStruct((n_tok, w), dt))
    outs = [tile(D, BF16), tile(LANES, I32), tile(LANES, F32), tile(LANES, I32),
            (pl.BlockSpec((8, tm), lambda i: (0, i)), jax.ShapeDtypeStruct((8, n_tok), F32)),
            (pl.BlockSpec((1, 8, LANES), lambda i: (i, 0, 0)), jax.ShapeDtypeStruct((nt, 8, LANES), F32))]
    return pl.pallas_call(
        _router_kernel, grid=(nt,),
        in_specs=[pl.BlockSpec((tm, D), lambda i: (i, 0)),
                  pl.BlockSpec((1, D), lambda i: (0, 0)),
                  pl.BlockSpec((None, 1, D), mrow), pl.BlockSpec((None, 1, D), mrow),
                  pl.BlockSpec((2, D, LANES), lambda i: (0, 0, 0)),
                  pl.BlockSpec((1, LANES), lambda i: (0, 0))],
        out_specs=[o[0] for o in outs], out_shape=[o[1] for o in outs],
        compiler_params=_cp(("arbitrary",)), name="moe_router",
    )(x, g.reshape(1, D), sc, sh, rw, rb)


SEG_ALIGN = 8
SEG_CHUNK = 16
REST_BITS = tuple(range(int(math.log2(SEG_CHUNK)) - 1, int(math.log2(SEG_ALIGN)) - 1, -1))
TILE_ROWS = TOK_TILE * TOP_K + N_EXPERTS * SEG_ALIGN


def _pow2_copies(n, src, dst, make_copy, op, bits):
    for b in bits:
        sz = 1 << b
        done = (n >> (b + 1)) << (b + 1)

        @pl.when((n & sz) != 0)
        def _():
            op(make_copy(pl.multiple_of(src + done, SEG_ALIGN), pl.multiple_of(dst + done, SEG_ALIGN), sz))


def _start_segments(i, n_ref, off_ref, dst_ref, make_copy):
    def body(e, carry):
        k = i * N_EXPERTS + e
        n, src, dst = n_ref[k], off_ref[k], dst_ref[k]

        def chunk(j, c):
            o = pl.multiple_of(j * SEG_CHUNK, SEG_CHUNK)
            make_copy(pl.multiple_of(src + o, SEG_ALIGN), pl.multiple_of(dst + o, SEG_ALIGN),
                      SEG_CHUNK).start(priority=1)
            return c
        shift = int(math.log2(SEG_CHUNK))
        full = n >> shift
        lax.fori_loop(0, full, chunk, 0)
        done = full << shift
        _pow2_copies(n - done, src + done, dst + done, make_copy, lambda c: c.start(), REST_BITS)
        return carry
    lax.fori_loop(0, N_EXPERTS, body, 0)


TAIL_BITS = tuple(range(int(math.log2(MOE_BLOCK)) - 1, int(math.log2(SEG_ALIGN)) - 1, -1))
TILE_BITS = tuple(range(int(math.log2(TILE_ROWS)), int(math.log2(SEG_ALIGN)) - 1, -1))


def _wait_rows(total, make_copy):
    _pow2_copies(total, 0, 0, make_copy, lambda c: c.wait(), TILE_BITS)


def _dispatch_kernel(n_ref, off_ref, dst_ref, tot_ref, tn_ref, td_ref, posT_ref, h_ref, xout_ref,
                     srt, zbuf, sems):
    i = pl.program_id(0)
    last = pl.num_programs(0) - 1
    slot = i % 2
    tm = h_ref.shape[0]
    rc = 256
    slots = [posT_ref[k:k + 1, :].astype(I32) for k in range(TOP_K)]
    for r0 in range(0, TILE_ROWS, rc):
        r = lax.broadcasted_iota(I32, (rc, tm), 0) + r0
        hit = jnp.zeros((rc, tm), jnp.bool_)
        for k in range(TOP_K):
            hit = hit | (r == slots[k])
        sel = jnp.where(hit, 1.0, 0.0).astype(BF16)
        srt[slot, r0:r0 + rc, :] = _pack_rows(_dot(sel, h_ref[...]))

    def copier(s):
        def make_copy(src, dst, sz):
            return pltpu.make_async_copy(srt.at[s, pl.ds(src, sz)], xout_ref.at[pl.ds(dst, sz)], sems.at[s])
        return make_copy

    _start_segments(i, n_ref, off_ref, dst_ref, copier(slot))

    @pl.when(i > 0)
    def _():
        _wait_rows(tot_ref[jnp.maximum(i - 1, 0)], copier(1 - slot))

    @pl.when(i == last)
    def _():
        _wait_rows(tot_ref[i], copier(slot))
        zbuf[...] = jnp.zeros_like(zbuf)
        sem = sems.at[0]

        def zero_copy(src, dst, sz):
            return pltpu.make_async_copy(zbuf.at[pl.ds(src, sz)], xout_ref.at[pl.ds(dst, sz)], sem)

        nb = xout_ref.shape[0] // MOE_BLOCK
        for op in (lambda c: c.start(), lambda c: c.wait()):
            def body(e, carry):
                _pow2_copies(tn_ref[e], 0, td_ref[e], zero_copy, op, TAIL_BITS)
                return carry
            lax.fori_loop(0, N_EXPERTS, body, 0)

            def unused(b, carry):
                op(zero_copy(0, pl.multiple_of(b * MOE_BLOCK, MOE_BLOCK), MOE_BLOCK))
                return carry
            lax.fori_loop(tn_ref[N_EXPERTS], nb, unused, 0)


def moe_dispatch(n_tab, off_tab, dst_tab, tot_tab, tail_n, tail_dst, posT, h2, n_rows):
    n_tok = h2.shape[0]
    tm = TOK_TILE
    grid_spec = pltpu.PrefetchScalarGridSpec(
        num_scalar_prefetch=6, grid=(n_tok // tm,),
        in_specs=[pl.BlockSpec((8, tm), lambda i, *_: (0, i)),
                  pl.BlockSpec((tm, D), lambda i, *_: (i, 0))],
        out_specs=pl.BlockSpec(memory_space=pl.ANY),
        scratch_shapes=[pltpu.VMEM((2, TILE_ROWS, ROW_WORDS), U32), pltpu.VMEM((MOE_BLOCK, ROW_WORDS), U32),
                        pltpu.SemaphoreType.DMA((2,))])
    return pl.pallas_call(
        _dispatch_kernel, grid_spec=grid_spec,
        out_shape=jax.ShapeDtypeStruct((n_rows, ROW_WORDS), U32),
        compiler_params=_cp(("arbitrary",)), name="moe_dispatch",
    )(n_tab, off_tab, dst_tab, tot_tab, tail_n, tail_dst, posT, h2)


def _combine_kernel(n_ref, off_ref, dst_ref, tot_ref, pos_ref, gate_ref, x_ref, g2_ref, y_ref, *rest, npt):
    o_refs, (buf, sems) = rest[:-2], rest[-2:]
    i = pl.program_id(0)
    last = pl.num_programs(0) - 1
    slot = i % 2
    tm = x_ref.shape[0]
    na = TILE_ROWS

    def copier(s):
        def make_copy(src, dst, sz):
            return pltpu.make_async_copy(y_ref.at[pl.ds(dst, sz)], buf.at[s, pl.ds(src, sz)], sems.at[s])
        return make_copy

    def fetch(tile, s):
        buf[s, tm * TOP_K:na, :] = jnp.zeros((na - tm * TOP_K, ROW_WORDS), U32)
        _start_segments(tile, n_ref, off_ref, dst_ref, copier(s))

    @pl.when(i == 0)
    def _():
        fetch(i, slot)

    @pl.when(i < last)
    def _():
        fetch(i + 1, 1 - slot)

    _wait_rows(tot_ref[i], copier(slot))
    kc = 256
    mixed = None
    for k0 in range(0, na, kc):
        lane = lax.broadcasted_iota(I32, (tm, kc), 1) + k0
        pw = jnp.zeros((tm, kc), F32)
        for k in range(TOP_K):
            pw = pw + jnp.where(lane == pos_ref[:, k:k + 1], gate_ref[:, k:k + 1], 0.0)
        phi = pw.astype(BF16)
        plo = (pw - phi.astype(F32)).astype(BF16)
        yb = _unpack_rows(buf[slot, k0:k0 + kc, :]).astype(BF16)
        part = _dot(phi, yb) + _dot(plo, yb)
        mixed = part if mixed is None else mixed + part
    res = x_ref[...] + g2_ref[...] * mixed
    if len(o_refs) == 1:
        o_refs[0][...] = res
    else:
        @pl.when(i < npt)
        def _():
            o_refs[0][...] = res

        @pl.when(i >= npt)
        def _():
            o_refs[1][...] = res


def moe_combine(lay, n_tab, off_tab, dst_tab, tot_tab, pos, gates, x, gate2, y_rows, split):
    n_tok = x.shape[0]
    tm = TOK_TILE
    npt = lay.p_tok // tm
    mrow = lambda i, *_: (lay.mod_row(i * tm), 0, 0)
    if split:
        out_specs = [pl.BlockSpec((tm, D), lambda i, *_: (jnp.minimum(i, npt - 1), 0)),
                     pl.BlockSpec((tm, D), lambda i, *_: (jnp.maximum(i - npt, 0), 0))]
        out_shape = [jax.ShapeDtypeStruct((lay.p_tok, D), F32), jax.ShapeDtypeStruct((n_tok - lay.p_tok, D), F32)]
    else:
        out_specs = pl.BlockSpec((tm, D), lambda i, *_: (i, 0))
        out_shape = jax.ShapeDtypeStruct((n_tok, D), F32)
    grid_spec = pltpu.PrefetchScalarGridSpec(
        num_scalar_prefetch=4, grid=(n_tok // tm,),
        in_specs=[pl.BlockSpec((tm, LANES), lambda i, *_: (i, 0)),
                  pl.BlockSpec((tm, LANES), lambda i, *_: (i, 0)),
                  pl.BlockSpec((tm, D), lambda i, *_: (i, 0)),
                  pl.BlockSpec((None, 1, D), mrow),
                  pl.BlockSpec(memory_space=pl.ANY)],
        out_specs=out_specs,
        scratch_shapes=[pltpu.VMEM((2, TILE_ROWS, ROW_WORDS), U32), pltpu.SemaphoreType.DMA((2,))])
    return pl.pallas_call(
        functools.partial(_combine_kernel, npt=npt), grid_spec=grid_spec, out_shape=out_shape,
        compiler_params=_cp(("arbitrary",)), name="moe_combine",
    )(n_tab, off_tab, dst_tab, tot_tab, pos, gates, x, gate2, y_rows)


def _expert_kernel(be_ref, nv_ref, nxt_ref, slot_ref, x_ref, b_ref, wg_hbm, wu_hbm, wd_hbm,
                   y_ref, wf, sems, *, layer):
    i = pl.program_id(0)
    valid = i < nv_ref[0]
    e = be_ref[i]
    slot = slot_ref[e]
    changed = jnp.logical_or(i == 0, e != be_ref[jnp.maximum(i - 1, 0)])

    def weight_copies(ex, s):
        return [pltpu.make_async_copy(w.at[layer, ex], wf.at[s, k], sems.at[s, k])
                for k, w in enumerate((wg_hbm, wu_hbm, wd_hbm))]

    @pl.when(jnp.logical_and(valid, changed))
    def _():
        @pl.when(i == 0)
        def _():
            for c in weight_copies(e, slot):
                c.start()

        nxt = nxt_ref[e]

        @pl.when(nxt >= 0)
        def _():
            for c in weight_copies(nxt, 1 - slot):
                c.start(priority=1)

        for c in weight_copies(e, slot):
            c.wait()

    @pl.when(valid)
    def _():
        x = _unpack_rows(x_ref[...])
        b = b_ref[e]
        gt = jnp.minimum(_dot(x, wf[slot, 0]) + b[0:1, :], SWIGLU_LIMIT)
        up = jnp.clip(_dot(x, wf[slot, 1]) + b[1:2, :], -SWIGLU_LIMIT, SWIGLU_LIMIT)
        act = (up + 1.0) * gt * _sigmoid(SWIGLU_ALPHA * gt)
        y = _dot(act, wf[slot, 2]) + b[2:3, :]
        y_ref[...] = _pack_rows(y.astype(BF16).astype(F32))

    @pl.when(jnp.logical_not(valid))
    def _():
        y_ref[...] = jnp.zeros_like(y_ref)


def moe_experts(layer, blk_expert, n_valid, next_expert, slot, x_rows, w_gate, b_gate, w_up, b_up, w_down,
                b_down):
    n_rows = x_rows.shape[0]
    nb = n_rows // MOE_BLOCK
    depth, ne, _, ff = w_gate.shape
    assert ff == D
    rowblk = lambda i, be, nv, *_: (jnp.maximum(jnp.minimum(i, nv[0] - 1), 0), 0)
    hbm = pl.BlockSpec(memory_space=pl.ANY)
    biases = jnp.stack([b_gate, b_up, b_down], axis=2)
    grid_spec = pltpu.PrefetchScalarGridSpec(
        num_scalar_prefetch=4, grid=(nb,),
        in_specs=[pl.BlockSpec((MOE_BLOCK, ROW_WORDS), rowblk),
                  pl.BlockSpec((None, ne, 3, D), lambda i, *_: (layer, 0, 0, 0)), hbm, hbm, hbm],
        out_specs=pl.BlockSpec((MOE_BLOCK, ROW_WORDS), lambda i, *_: (i, 0)),
        scratch_shapes=[pltpu.VMEM((2, 3, D, ff), F32), pltpu.SemaphoreType.DMA((2, 3))])
    return pl.pallas_call(
        functools.partial(_expert_kernel, layer=layer), grid_spec=grid_spec,
        out_shape=jax.ShapeDtypeStruct((n_rows, ROW_WORDS), U32),
        compiler_params=_cp(("arbitrary",)), name="moe_experts",
    )(blk_expert, n_valid, next_expert, slot, x_rows, biases, w_gate, w_up, w_down)


def moe_layer(lay, layer, x, g2, sc2, sh2, gate2, router_w, router_b, w_gate, b_gate, w_up, b_up, w_down,
              b_down, split=False):
    n_tok = x.shape[0]
    nt = n_tok // TOK_TILE
    rw = jnp.zeros((D, LANES), F32).at[:, :N_EXPERTS].set(router_w)
    rw = _hilo(rw)
    rb = jnp.full((1, LANES), NEG, F32).at[0, :N_EXPERTS].set(router_b)
    h2, _, gates, pos, posT, cnt = moe_router(lay, x, g2, sc2, sh2, rw, rb)
    n_te = cnt[:, 0, :N_EXPERTS].astype(I32)
    n_te = (n_te + SEG_ALIGN - 1) // SEG_ALIGN * SEG_ALIGN
    totals = jnp.sum(n_te, axis=0)
    padded = (totals + MOE_BLOCK - 1) // MOE_BLOCK * MOE_BLOCK
    padded_end = jnp.cumsum(padded)
    pstart = padded_end - padded
    dst = pstart[None, :] + jnp.cumsum(n_te, axis=0) - n_te
    off = jnp.cumsum(n_te, axis=1) - n_te
    n_rows = nt * TILE_ROWS + N_EXPERTS * MOE_BLOCK
    nb = n_rows // MOE_BLOCK
    n_valid = (padded_end[-1] // MOE_BLOCK).astype(I32).reshape(1)
    bstart = jnp.minimum(jnp.arange(nb, dtype=I32), n_valid[0] - 1) * MOE_BLOCK
    blk_expert = jnp.minimum(jnp.sum((bstart[:, None] >= padded_end[None, :]).astype(I32), axis=1),
                             N_EXPERTS - 1).astype(I32)
    tabs = (n_te.reshape(-1).astype(I32), off.reshape(-1).astype(I32), dst.reshape(-1).astype(I32),
            jnp.sum(n_te, axis=1).astype(I32))
    tail_n = jnp.concatenate([(padded - totals).astype(I32), n_valid])
    x_rows = moe_dispatch(*tabs, tail_n, (pstart + totals).astype(I32), posT, h2, n_rows)
    owner = jnp.where(padded > 0, jnp.arange(N_EXPERTS, dtype=I32), N_EXPERTS)
    later = jnp.concatenate([lax.cummin(owner, axis=0, reverse=True)[1:], jnp.full((1,), N_EXPERTS, I32)])
    next_expert = jnp.where(later < N_EXPERTS, later, -1).astype(I32)
    slot = ((jnp.cumsum((padded > 0).astype(I32)) - 1) % 2).astype(I32)
    y_rows = moe_experts(layer, blk_expert, n_valid, next_expert, slot, x_rows, w_gate, b_gate, w_up, b_up,
                         w_down, b_down)
    return moe_combine(lay, *tabs, pos, gates, x, gate2, y_rows, split)


QKV_TN = 256
N_QK_TILES = (ATT_HEADS + ATT_KV) * ATT_HD // QKV_TN


def _qkv_kernel(x_ref, g_ref, sc_ref, sh_ref, w_ref, nw_ref, cos_ref, sin_ref, o_ref):
    tm = x_ref.shape[0]
    h = _modnorm(x_ref[...], g_ref[...], sc_ref[...], sh_ref[...]).astype(BF16)
    r = lax.broadcasted_iota(I32, (QKV_TN, QKV_TN), 0) // ATT_HD
    c = lax.broadcasted_iota(I32, (QKV_TN, QKV_TN), 1) // ATT_HD
    head_mean = jnp.where(r == c, 1.0 / ATT_HD, 0.0).astype(BF16)
    lane = lax.broadcasted_iota(I32, (tm, QKV_TN), 1)
    half = ATT_HD // 4
    first = (lane % (2 * half)) < half
    acc_all = _dot(h, w_ref[...])
    tiles = [acc_all[:, QKV_TN * j:QKV_TN * (j + 1)] for j in range(w_ref.shape[1] // QKV_TN)]
    sq = jnp.concatenate([(t * t).astype(BF16) for t in tiles[:N_QK_TILES]], axis=0)
    ms_all = _dot(sq, head_mean)
    for j, acc in enumerate(tiles):
        cols = slice(QKV_TN * j, QKV_TN * (j + 1))
        if j >= N_QK_TILES:
            o_ref[:, cols] = acc
            continue
        qn = acc * lax.rsqrt(ms_all[tm * j:tm * (j + 1)] + EPS) * nw_ref[j]
        swapped = jnp.where(first, pltpu.roll(qn, QKV_TN - half, 1), pltpu.roll(qn, half, 1))
        o_ref[:, cols] = qn * cos_ref[...] + swapped * sin_ref[...]


def _rope_tables(sample_len):
    pos = np.arange(sample_len)
    half = ATT_HD // 4
    inv = (ROPE_THETA ** (-np.arange(half, dtype=np.float32) / half)).astype(np.float32)
    ang_r = (pos // GRID_W).astype(np.float32)[:, None] * inv[None, :]
    ang_c = (pos % GRID_W).astype(np.float32)[:, None] * inv[None, :]
    cos = np.concatenate([np.cos(ang_r)] * 2 + [np.cos(ang_c)] * 2, axis=1)
    sin = np.concatenate([-np.sin(ang_r), np.sin(ang_r), -np.sin(ang_c), np.sin(ang_c)], axis=1)
    rep = QKV_TN // ATT_HD
    return (jnp.asarray(np.tile(cos, (1, rep)), F32), jnp.asarray(np.tile(sin, (1, rep)), F32))


def qkv_proj(lay, x, g, sc, sh, w, q_norm, k_norm, tm=512):
    n_tok = x.shape[0]
    n = w.shape[1]
    nq = ATT_HEADS * ATT_HD // QKV_TN
    rep = QKV_TN // ATT_HD
    nw = jnp.concatenate([jnp.tile(jnp.tile(q_norm, rep)[None, :], (nq, 1)),
                          jnp.tile(jnp.tile(k_norm, rep)[None, :], (n // QKV_TN - nq, 1))], axis=0)
    cos, sin = _rope_tables(lay.sample_len)
    cos = jnp.concatenate([jnp.ones((tm, QKV_TN), F32), cos], axis=0)
    sin = jnp.concatenate([jnp.zeros((tm, QKV_TN), F32), sin], axis=0)
    assert lay.p_tok % tm == 0 and lay.sample_len % tm == 0
    mrow = lambda i: (lay.mod_row(i * tm), 0, 0)
    rrow = lambda i: (jnp.where(i * tm < lay.p_tok, 0, 1 + ((i * tm - lay.p_tok) % lay.sample_len) // tm), 0)
    nt = n // QKV_TN
    return pl.pallas_call(
        _qkv_kernel, grid=(n_tok // tm,),
        in_specs=[pl.BlockSpec((tm, D), lambda i: (i, 0)),
                  pl.BlockSpec((1, D), lambda i: (0, 0)),
                  pl.BlockSpec((None, 1, D), mrow), pl.BlockSpec((None, 1, D), mrow),
                  pl.BlockSpec((D, n), lambda i: (0, 0)),
                  pl.BlockSpec((nt, 1, QKV_TN), lambda i: (0, 0, 0)),
                  pl.BlockSpec((tm, QKV_TN), rrow), pl.BlockSpec((tm, QKV_TN), rrow)],
        out_specs=pl.BlockSpec((tm, n), lambda i: (i, 0)),
        out_shape=jax.ShapeDtypeStruct((n_tok, n), F32),
        compiler_params=_cp(("arbitrary",)), name="qkv_proj",
    )(x, g.reshape(1, D), sc, sh, w, nw.reshape(nt, 1, QKV_TN), cos, sin)


def _dup_group(x, g):
    blk = x[:, LANES * (g // 2):LANES * (g // 2 + 1)]
    if g % 2 == 1:
        blk = pltpu.roll(blk, ATT_HD, 1)
    lo = lax.broadcasted_iota(I32, blk.shape, 1) < ATT_HD
    low = jnp.where(lo, blk, 0.0)
    return low + pltpu.roll(low, ATT_HD, 1)


def _attend(q_ref, k_all, v_all, mask, sink_ref, o_ref):
    nq = q_ref.shape[0]
    lo = lax.broadcasted_iota(I32, (nq, LANES), 1) < ATT_HD
    first = lax.broadcasted_iota(I32, (2 * nq, 1), 0) < nq
    if mask is not None:
        mask = jnp.concatenate([mask, mask], axis=0)
    pairs_per_group = ATT_HEADS // ATT_KV // 2
    n_pairs = ATT_HEADS // 2
    kv = {}

    def group_kv(g):
        if g not in kv:
            kv[g] = (_dup_group(k_all, g).astype(BF16), _dup_group(v_all, g).astype(BF16))
        return kv[g]

    def scores(j):
        qp = q_ref[:, LANES * j:LANES * (j + 1)] * (ATT_HD ** -0.5)
        qs = jnp.concatenate([jnp.where(lo, qp, 0.0), jnp.where(lo, 0.0, qp)], axis=0)
        return _dot_nt(qs.astype(BF16), group_kv(j // pairs_per_group)[0])

    ahead = 2
    queue = [scores(j) for j in range(min(ahead, n_pairs))]
    for j in range(n_pairs):
        s = queue.pop(0)
        if j + ahead < n_pairs:
            queue.append(scores(j + ahead))
        if mask is not None:
            s = jnp.where(mask, s, NEG)
        sink = jnp.where(first, sink_ref[2 * j], sink_ref[2 * j + 1])
        m = jnp.maximum(jnp.max(s, axis=1, keepdims=True), sink)
        p = jnp.exp(s - m)
        den = jnp.sum(p, axis=1, keepdims=True) + jnp.exp(sink - m)
        o = _dot(p.astype(BF16), group_kv(j // pairs_per_group)[1]) / den
        o_ref[:, LANES * j:LANES * (j + 1)] = jnp.where(lo, o[:nq], o[nq:]).astype(o_ref.dtype)


def _attn_ctx_kernel(sink_ref, q_ref, k_ref, v_ref, o_ref):
    _attend(q_ref, k_ref[...], v_ref[...], None, sink_ref, o_ref)


def attn_context(lay, qkv, sinks):
    qw = ATT_HEADS * ATT_HD
    kw = ATT_KV * ATT_HD
    ln = lay.prompt_len
    grid_spec = pltpu.PrefetchScalarGridSpec(
        num_scalar_prefetch=0, grid=(lay.n_prompt,),
        in_specs=[pl.BlockSpec(memory_space=pltpu.SMEM),
                  pl.BlockSpec((ln, qw), lambda b: (b, 0)),
                  pl.BlockSpec((ln, kw), lambda b: (b, qw // kw)),
                  pl.BlockSpec((ln, kw), lambda b: (b, qw // kw + 1))],
        out_specs=pl.BlockSpec((ln, qw), lambda b: (b, 0)))
    return pl.pallas_call(
        _attn_ctx_kernel, grid_spec=grid_spec,
        out_shape=jax.ShapeDtypeStruct((lay.p_tok, qw), BF16),
        compiler_params=_cp(("arbitrary",)), name="attn_context",
    )(sinks, qkv, qkv, qkv)


def _attn_lat_kernel(sink_ref, q_ref, kp_ref, kc_ref, kn_ref, vp_ref, vc_ref, vn_ref, ck_ref, cv_ref, o_ref,
                     *, nblk):
    i = pl.program_id(1)
    bq = ATT_BLOCK
    nctx = ck_ref.shape[1]
    k_all = jnp.concatenate([kp_ref[...], kc_ref[...], kn_ref[...], ck_ref[0]], axis=0)
    v_all = jnp.concatenate([vp_ref[...], vc_ref[...], vn_ref[...], cv_ref[0]], axis=0)
    ns = 3 * bq + nctx
    r = lax.broadcasted_iota(I32, (bq, ns), 0)
    c = lax.broadcasted_iota(I32, (bq, ns), 1)
    rel = c - r
    first_key = jnp.where(i > 0, 0, bq)
    end_key = jnp.where(i < nblk - 1, 3 * bq, 2 * bq)
    band = (rel >= bq - WINDOW) & (rel <= bq + WINDOW) & (c >= first_key) & (c < end_key)
    mask = band | (c >= 3 * bq)
    _attend(q_ref, k_all, v_all, mask, sink_ref, o_ref)


def attn_latent(lay, qkv, cache_k, cache_v, sinks):
    qw = ATT_HEADS * ATT_HD
    kw = ATT_KV * ATT_HD
    bq = ATT_BLOCK
    nblk = lay.sample_len // bq
    b0 = lay.p_tok // bq
    nctx = cache_k.shape[1]
    rb = lambda b, i: b0 + b * nblk + i
    kspec = lambda cb, sh: pl.BlockSpec(
        (bq, kw), lambda b, i: (b0 + b * nblk + jnp.clip(i + sh, 0, nblk - 1), cb))
    kc, vc = qw // kw, qw // kw + 1
    grid_spec = pltpu.PrefetchScalarGridSpec(
        num_scalar_prefetch=0, grid=(lay.n_sample, nblk),
        in_specs=[pl.BlockSpec(memory_space=pltpu.SMEM),
                  pl.BlockSpec((bq, qw), lambda b, i: (rb(b, i), 0)),
                  kspec(kc, -1), kspec(kc, 0), kspec(kc, 1),
                  kspec(vc, -1), kspec(vc, 0), kspec(vc, 1),
                  pl.BlockSpec((1, nctx, kw), lambda b, i: (b, 0, 0)),
                  pl.BlockSpec((1, nctx, kw), lambda b, i: (b, 0, 0))],
        out_specs=pl.BlockSpec((bq, qw), lambda b, i: (b * nblk + i, 0)))
    return pl.pallas_call(
        functools.partial(_attn_lat_kernel, nblk=nblk), grid_spec=grid_spec,
        out_shape=jax.ShapeDtypeStruct((lay.n_sample * lay.sample_len, qw), BF16),
        compiler_params=_cp(("arbitrary", "arbitrary")), name="attn_latent",
    )(sinks, qkv, qkv, qkv, qkv, qkv, qkv, qkv,
      cache_k.reshape(lay.n_sample, nctx, kw), cache_v.reshape(lay.n_sample, nctx, kw))


def _forward(lay, x_prompt, x_sample, state_l0_ssd_fwd, state_l0_ssd_bwd, state_l0_gla_fwd, state_l0_gla_bwd,
             cache_l1_k, cache_l1_v, c, c_ctx, ada_w, ada_b, norm1, norm2,
             l0_w_in, l0_conv_w, l0_conv_b, l0_a_log, l0_dt_bias, l0_d_skip, l0_ssd_norm,
             l0_gate_w2, l0_gate_b, l0_gla_norm, l0_w_out,
             l1_w_qkv, l1_q_norm, l1_k_norm, l1_sinks, l1_w_out,
             router_w, router_b, exp_w_gate, exp_b_gate, exp_w_up, exp_b_up, exp_w_down, exp_b_down):
    np_, ns = lay.n_prompt, lay.n_sample
    x = (x_prompt.reshape(-1, D), x_sample.reshape(-1, D))
    cond8 = jnp.zeros((8, D), F32).at[0].set(c_ctx).at[1:1 + ns].set(c)
    mod = ada_table(cond8, ada_w, ada_b)
    mods = [[mod[l, :, p * D:(p + 1) * D].reshape(8, 1, D) for p in range(N_ADA)] for l in range(2)]

    def moe(l, xx, split=False):
        return moe_layer(lay, l, xx, norm2[l], mods[l][4], mods[l][3], mods[l][5], router_w[l], router_b[l],
                         exp_w_gate, exp_b_gate, exp_w_up, exp_b_up, exp_w_down, exp_b_down, split=split)

    sp = np.cumsum((SSD_INNER, SSD_INNER + 2 * SSD_GROUPS * SSD_STATE, 2 * SSD_HEADS,
                    GLA_HEADS * GLA_KEY_DIM, GLA_HEADS * GLA_KEY_DIM,
                    GLA_HEADS * GLA_VAL_DIM, GLA_HEADS * GLA_VAL_DIM, 2 * GLA_RANK))
    cols = lambda a, b: l0_w_in[:, a:b]
    w_main = jnp.concatenate([cols(0, sp[0]), cols(sp[4], sp[5]), cols(sp[5], sp[6]), cols(sp[0], sp[1]),
                              cols(sp[2], sp[3]), cols(sp[3], sp[4])], axis=1).astype(BF16)
    w_small = jnp.concatenate([cols(sp[1], sp[2]), cols(sp[6], sp[7]),
                               jnp.zeros((D, LANES - 2 * SSD_HEADS - 2 * GLA_RANK), F32)], axis=1)
    proj, small, small_t = norm_proj(lay, x, norm1[0], mods[0][1], mods[0][0], w_main, _hilo(w_small), 512,
                                      PJ_W // 2, BF16, SSD_CHUNK)
    y_n, o_n, (ssd_f, ssd_b, gla_f, gla_b) = l0_mixers(
        lay, proj, small, small_t, state_l0_ssd_fwd, state_l0_ssd_bwd, state_l0_gla_fwd, state_l0_gla_bwd,
        l0_conv_w, l0_conv_b, l0_a_log, l0_dt_bias, l0_d_skip, l0_ssd_norm,
        l0_gate_w2, l0_gate_b, l0_gla_norm)
    x = proj_residual(lay, [y_n, o_n], l0_w_out.astype(BF16), x, mods[0][2])
    x = moe(0, x)

    qkv = qkv_proj(lay, x, norm1[1], mods[1][1], mods[1][0], l1_w_qkv.astype(BF16), l1_q_norm, l1_k_norm)
    o_ctx = attn_context(lay, qkv, l1_sinks)
    o_lat = attn_latent(lay, qkv, cache_l1_k, cache_l1_v, l1_sinks)
    x = proj_residual(lay, [(o_ctx, o_lat)], l1_w_out.astype(BF16), x, mods[1][2])
    xp, xs = moe(1, x, split=True)

    qw = ATT_HEADS * ATT_HD
    kw = ATT_KV * ATT_HD
    return (xp.reshape(x_prompt.shape), xs.reshape(x_sample.shape),
            ssd_f[:np_].reshape(np_, SSD_HEADS, SSD_HEAD_DIM, SSD_STATE),
            ssd_b[:np_].reshape(np_, SSD_HEADS, SSD_HEAD_DIM, SSD_STATE),
            gla_f[:np_], gla_b[:np_],
            qkv[:lay.p_tok, qw:qw + kw].reshape(np_, lay.prompt_len, ATT_KV, ATT_HD),
            qkv[:lay.p_tok, qw + kw:].reshape(np_, lay.prompt_len, ATT_KV, ATT_HD))


def kernel(x_prompt, x_sample, state_l0_ssd_fwd, state_l0_ssd_bwd, state_l0_gla_fwd, state_l0_gla_bwd, cache_l1_k, cache_l1_v, c, c_ctx, ada_w, ada_b, norm1, norm2, l0_w_in, l0_conv_w, l0_conv_b, l0_a_log, l0_dt_bias, l0_d_skip, l0_ssd_norm, l0_gate_w2, l0_gate_b, l0_gla_norm, l0_w_out, l1_w_qkv, l1_q_norm, l1_k_norm, l1_sinks, l1_w_out, router_w, router_b, exp_w_gate, exp_b_gate, exp_w_up, exp_b_up, exp_w_down, exp_b_down):
    lay = Layout(x_prompt.shape[0], x_prompt.shape[1], x_sample.shape[0], x_sample.shape[1])
    return _forward(lay, x_prompt, x_sample, state_l0_ssd_fwd, state_l0_ssd_bwd, state_l0_gla_fwd,
                    state_l0_gla_bwd, cache_l1_k, cache_l1_v, c, c_ctx, ada_w, ada_b, norm1, norm2,
                    l0_w_in, l0_conv_w, l0_conv_b, l0_a_log, l0_dt_bias, l0_d_skip, l0_ssd_norm,
                    l0_gate_w2, l0_gate_b, l0_gla_norm, l0_w_out,
                    l1_w_qkv, l1_q_norm, l1_k_norm, l1_sinks, l1_w_out,
                    router_w, router_b, exp_w_gate, exp_b_gate, exp_w_up, exp_b_up, exp_w_down, exp_b_down)
```

```python
import functools
import math

import numpy as np
import jax
import jax.numpy as jnp
from jax import lax
from jax.experimental import pallas as pl
from jax.experimental.pallas import tpu as pltpu

F32 = jnp.float32
BF16 = jnp.bfloat16
I32 = jnp.int32
HI = lax.Precision.HIGHEST

D = 1024
EPS = 1e-6
N_ADA = 6
SSD_HEADS = 16
SSD_HEAD_DIM = 64
SSD_INNER = 1024
SSD_STATE = 128
SSD_GROUPS = 2
SSD_CONV = 5
SSD_CHUNK = 128
GLA_HEADS = 4
GLA_KEY_DIM = 128
GLA_VAL_DIM = 256
GLA_RANK = 16
GLA_TAU = 16.0
GLA_BLOCK = 64
ATT_HEADS = 16
ATT_KV = 4
ATT_HD = 64
ATT_BLOCK = 128
WINDOW = 128
GRID_W = 64
ROPE_THETA = 10000.0
N_EXPERTS = 32
TOP_K = 4
EXPERT_FF = 1024
SWIGLU_LIMIT = 7.0
SWIGLU_ALPHA = 1.702
MOE_BLOCK = 256
TOK_TILE = 256
LANES = 128
NEG = -1e30

PJ_Z, PJ_V, PJ_OG, PJ_XBC, PJ_Q, PJ_K = 0, 1024, 2048, 3072, 4608, 5120
PJ_W = 5632
VMEM_LIMIT = 48 * 1024 * 1024


def _cp(sem, vmem=VMEM_LIMIT):
    return pltpu.CompilerParams(dimension_semantics=sem, vmem_limit_bytes=vmem)


class Layout:
    def __init__(self, n_prompt, prompt_len, n_sample, sample_len):
        self.n_prompt, self.prompt_len = n_prompt, prompt_len
        self.n_sample, self.sample_len = n_sample, sample_len
        self.p_tok = n_prompt * prompt_len
        self.n_tok = self.p_tok + n_sample * sample_len
        self.seqs = [(i * prompt_len, prompt_len) for i in range(n_prompt)]
        self.seqs += [(self.p_tok + i * sample_len, sample_len) for i in range(n_sample)]
        self.n_seq = len(self.seqs)

    def mod_row(self, start):
        return jnp.where(start < self.p_tok, 0, 1 + (start - self.p_tok) // self.sample_len)

def _sigmoid(x):
    return 1.0 / (1.0 + jnp.exp(-x))


def _silu(x):
    return x * _sigmoid(x)


def _softplus(x):
    return jnp.maximum(x, 0.0) + jnp.log(1.0 + jnp.exp(-jnp.abs(x)))


def _modnorm(x, g, sc, sh):
    ms = jnp.mean(x * x, axis=-1, keepdims=True)
    return (x * lax.rsqrt(ms + EPS) * g) * (1.0 + sc) + sh


def _dot(a, b, **kw):
    return jnp.dot(a, b, preferred_element_type=F32, **kw)


def _dot_nt(a, b):
    return lax.dot_general(a, b, (((1,), (1,)), ((), ())), preferred_element_type=F32)


def _dot_tn(a, b):
    return lax.dot_general(a, b, (((0,), (0,)), ((), ())), preferred_element_type=F32)


def _split(x, n):
    parts = []
    for _ in range(n):
        p = x.astype(BF16)
        parts.append(p)
        x = x - p.astype(F32)
    return parts


def _dot_sel(sel, x):
    sel = sel.astype(BF16)
    return sum(_dot(sel, p) for p in _split(x, 3))


def _dot_sel_r(x, sel):
    sel = sel.astype(BF16)
    return sum(_dot(p, sel) for p in _split(x, 3))


def _dot_hilo(x, w_hi, w_lo):
    x_hi, x_lo = _split(x, 2)
    return _dot(x_hi, w_hi) + _dot(x_lo, w_hi) + _dot(x_hi, w_lo)


def _hilo(w):
    hi = w.astype(BF16)
    return jnp.stack([hi, (w - hi.astype(F32)).astype(BF16)])


U32 = jnp.uint32
ROW_WORDS = D // 2
_HI_MASK = 0xFFFF0000


def _pack_rows(x):
    lo = lax.bitcast_convert_type(x[:, :ROW_WORDS], U32) >> 16
    hi = lax.bitcast_convert_type(x[:, ROW_WORDS:], U32) & jnp.uint32(_HI_MASK)
    return lo | hi


def _unpack_rows(u):
    lo = lax.bitcast_convert_type(u << 16, F32)
    hi = lax.bitcast_convert_type(u & jnp.uint32(_HI_MASK), F32)
    return jnp.concatenate([lo, hi], axis=1)


def _ada_kernel(c_ref, w_ref, b_ref, o_ref):
    o_ref[0] = _dot(_silu(c_ref[...]), w_ref[0], precision=HI) + b_ref[0]


def ada_table(cond8, ada_w, ada_b):
    depth, _, n = ada_w.shape
    tn = 1536
    return pl.pallas_call(
        _ada_kernel, grid=(depth, n // tn),
        in_specs=[pl.BlockSpec((8, D), lambda l, j: (0, 0)),
                  pl.BlockSpec((1, D, tn), lambda l, j: (l, 0, j)),
                  pl.BlockSpec((1, 1, tn), lambda l, j: (l, 0, j))],
        out_specs=pl.BlockSpec((1, 8, tn), lambda l, j: (l, 0, j)),
        out_shape=jax.ShapeDtypeStruct((depth, 8, n), F32),
        compiler_params=_cp(("arbitrary", "arbitrary")), name="ada_table",
    )(cond8, ada_w, ada_b.reshape(depth, 1, n))


def _stream_specs(lay, stream, tm):
    if not isinstance(stream, (tuple, list)):
        return [stream], [pl.BlockSpec((tm, stream.shape[1]), lambda i, *_: (i, 0))]
    assert lay.p_tok % tm == 0
    npt = lay.p_tok // tm
    w = stream[0].shape[1]
    return list(stream), [pl.BlockSpec((tm, w), lambda i, *_: (jnp.minimum(i, npt - 1), 0)),
                          pl.BlockSpec((tm, w), lambda i, *_: (jnp.maximum(i - npt, 0), 0))]


def _stream_tile(refs, in_prompt):
    if len(refs) == 1:
        return refs[0][...]
    return jnp.where(in_prompt, refs[0][...], refs[1][...])


def _proj_kernel(*refs, nx, npt):
    x_refs = refs[:nx]
    g_ref, sc_ref, sh_ref, w_ref, ws_ref, o_ref, os_ref, ost_ref, h_scr = refs[nx:]

    @pl.when(pl.program_id(1) == 0)
    def _():
        x = _stream_tile(x_refs, pl.program_id(0) < npt)
        h = _modnorm(x, g_ref[...], sc_ref[...], sh_ref[...])
        h_scr[...] = h.astype(BF16)
        small = _dot_hilo(h, ws_ref[0], ws_ref[1])
        os_ref[...] = small
        q = ost_ref.shape[2]
        for c in range(ost_ref.shape[0]):
            ost_ref[c] = small[q * c:q * (c + 1), :].T

    o_ref[...] = _dot(h_scr[...], w_ref[...]).astype(o_ref.dtype)


def norm_proj(lay, x, g, sc, sh, w, w_small, tm, tn, out_dtype, chunk):
    n_tok = lay.n_tok
    n = w.shape[1]
    ns = w_small.shape[-1]
    mrow = lambda i, j: (lay.mod_row(i * tm), 0, 0)
    xs, x_specs = _stream_specs(lay, x, tm)
    return pl.pallas_call(
        functools.partial(_proj_kernel, nx=len(xs), npt=lay.p_tok // tm), grid=(n_tok // tm, n // tn),
        in_specs=x_specs + [pl.BlockSpec((1, D), lambda i, j: (0, 0)),
                            pl.BlockSpec((None, 1, D), mrow),
                            pl.BlockSpec((None, 1, D), mrow),
                            pl.BlockSpec((D, tn), lambda i, j: (0, j)),
                            pl.BlockSpec((2, D, ns), lambda i, j: (0, 0, 0))],
        out_specs=[pl.BlockSpec((tm, tn), lambda i, j: (i, j)),
                   pl.BlockSpec((tm, ns), lambda i, j: (i, 0)),
                   pl.BlockSpec((tm // chunk, ns, chunk), lambda i, j: (i, 0, 0))],
        out_shape=[jax.ShapeDtypeStruct((n_tok, n), out_dtype),
                   jax.ShapeDtypeStruct((n_tok, ns), F32),
                   jax.ShapeDtypeStruct((n_tok // chunk, ns, chunk), F32)],
        scratch_shapes=[pltpu.VMEM((tm, D), BF16)],
        compiler_params=_cp(("arbitrary", "arbitrary")), name="norm_proj",
    )(*xs, g.reshape(1, D), sc, sh, w, w_small)


CONV_HALO = 16


def _ssd_load(d, c, xc, dtg_ref, dtgT_ref, S):
    q = SSD_CHUNK
    nh = SSD_HEADS
    rows = pl.ds(pl.multiple_of(c * q, q), q)
    return (rows, xc[rows, 0:SSD_INNER], xc[rows, SSD_INNER:SSD_INNER + 2 * SSD_GROUPS * SSD_STATE],
            dtg_ref[rows, nh * d:nh * d + nh], dtgT_ref[c, nh * d:nh * d + nh, :], S[...])


def _ssd_chunks(loaded, alog_ref, alogT_ref, dtb_ref, dtbT_ref):
    q = SSD_CHUNK
    nh = SSD_HEADS
    hp = nh * SSD_HEAD_DIM
    row = lax.broadcasted_iota(I32, (q, q), 0)
    col = lax.broadcasted_iota(I32, (q, q), 1)
    lo = lax.broadcasted_iota(I32, (q, LANES), 1) < SSD_HEAD_DIM
    head_of = lax.broadcasted_iota(I32, (nh, hp), 1) // SSD_HEAD_DIM
    spread = (lax.broadcasted_iota(I32, (nh, hp), 0) == head_of).astype(BF16)
    rep = (nh // SSD_GROUPS) // 2
    pairs = [(g, j) for g in range(SSD_GROUPS) for j in range(rep * g, rep * (g + 1))]
    lane = lambda j: slice(LANES * j, LANES * (j + 1))

    def per_lane(arr, passes):
        return sum(_dot(p, spread) for p in _split(arr, passes))

    stage1 = []
    for d, (_, _, _, dtg, dtgT, _) in enumerate(loaded):
        dt = _softplus(dtg + dtb_ref[d:d + 1, :])
        dtT = _softplus(dtgT + dtbT_ref[:, d:d + 1])
        ad = dt * (-jnp.exp(alog_ref[d:d + 1, :]))
        adT = dtT * (-jnp.exp(alogT_ref[:, d:d + 1]))
        if d == 0:
            e, eT = _dot_sel(col <= row, ad), _dot_sel_r(adT, row <= col)
        else:
            e, eT = _dot_sel(col < row, ad), _dot_sel_r(adT, row < col)
        stage1.append((dt, ad, e, eT))
    stage2 = []
    for d, ((_, xs, bc, _, _, s), (dt, ad, e, eT)) in enumerate(zip(loaded, stage1)):
        if d == 0:
            tot = e[q - 1:q, :]
            fq, fk = jnp.exp(e), jnp.exp(tot - e)
        else:
            tot = jnp.sum(ad, axis=0, keepdims=True)
            fq, fk = jnp.exp(tot - e), jnp.exp(e)
        dt_x, fq_x, fk_x = per_lane(dt, 1), per_lane(fq, 1), per_lane(fk, 1)
        dec_x = per_lane(jnp.broadcast_to(jnp.exp(tot), (8, nh)), 3)[0:1, :]
        xdt_all = xs.astype(F32) * dt_x
        xk_all = (xdt_all * fk_x).astype(BF16)
        bg_t = [bc[:, SSD_STATE * g:SSD_STATE * (g + 1)].T for g in range(SSD_GROUPS)]
        cg = [bc[:, SSD_STATE * (SSD_GROUPS + g):SSD_STATE * (SSD_GROUPS + g + 1)] for g in range(SSD_GROUPS)]
        gmat = [_dot(cg[g], bg_t[g]) for g in range(SSD_GROUPS)]
        carried = [_dot(cg[g], s[:, lane(j)].astype(BF16)) * fq_x[:, lane(j)] for g, j in pairs]
        s_new = [s[:, lane(j)] * dec_x[:, lane(j)] + _dot(bg_t[g], xk_all[:, lane(j)]) for g, j in pairs]
        stage2.append((xdt_all, gmat, carried, s_new))
    results = []
    for d, ((_, _, e, eT), (xdt_all, gmat, carried, s_new)) in enumerate(zip(stage1, stage2)):
        mask = row >= col if d == 0 else col >= row
        ys = []
        for (g, j), off in zip(pairs, carried):
            parts = []
            for hh in (2 * j, 2 * j + 1):
                diff = e[:, hh:hh + 1] - eT[hh:hh + 1, :] if d == 0 else eT[hh:hh + 1, :] - e[:, hh:hh + 1]
                parts.append((gmat[g] * jnp.exp(jnp.where(mask, diff, NEG))).astype(BF16))
            lhs = jnp.concatenate(parts, axis=1)
            xdt = xdt_all[:, lane(j)]
            rhs = jnp.concatenate([jnp.where(lo, xdt, 0.0), jnp.where(lo, 0.0, xdt)], axis=0)
            ys.append(_dot(lhs, rhs.astype(BF16)) + off)
        results.append((jnp.concatenate(ys, axis=1), jnp.concatenate(s_new, axis=1)))
    return results


def _ssd_seq_kernel(*refs, has_init):
    (xbc_ref, z_ref, dtg_ref, dtgT_ref, cw_ref, cb_ref,
     alog_ref, alogT_ref, dtb_ref, dtbT_ref, dskip_ref, nrm_ref) = refs[:12]
    refs = refs[12:]
    if has_init:
        s0f_ref, s0b_ref = refs[:2]
        refs = refs[2:]
    y_ref, sf_ref, sb_ref, xc, Sf, Sb, yf, yb = refs
    ln = xbc_ref.shape[0]
    q = SSD_CHUNK
    nc = ln // q
    h = CONV_HALO
    pad = SSD_CONV // 2

    def conv_body(c, carry):
        r0 = pl.multiple_of(c * q, q)
        prev = xbc_ref[pl.ds(pl.multiple_of(jnp.maximum(r0 - h, 0), h), h), :]
        nxt = xbc_ref[pl.ds(pl.multiple_of(jnp.minimum(r0 + q, ln - h), h), h), :]
        ext = jnp.concatenate([jnp.where(c > 0, prev, jnp.zeros_like(prev)), xbc_ref[pl.ds(r0, q), :],
                               jnp.where(c < nc - 1, nxt, jnp.zeros_like(nxt))], axis=0)
        src = lax.broadcasted_iota(I32, (q, q + 2 * h), 1) - lax.broadcasted_iota(I32, (q, q + 2 * h), 0)
        taps = [_dot((src == h - pad + k).astype(BF16), ext) for k in range(SSD_CONV)]
        acc = jnp.broadcast_to(cb_ref[...], (q, cb_ref.shape[1]))
        for k in range(SSD_CONV):
            acc = acc + cw_ref[k:k + 1, :] * taps[k]
        xc[pl.ds(r0, q), :] = _silu(acc).astype(xc.dtype)
        return carry
    lax.fori_loop(0, nc, conv_body, 0)

    if has_init:
        Sf[...] = s0f_ref[0].T
        Sb[...] = s0b_ref[0].T
    else:
        Sf[...] = jnp.zeros_like(Sf)
        Sb[...] = jnp.zeros_like(Sb)
    params = (alog_ref, alogT_ref, dtb_ref, dtbT_ref)

    def scan_body(c, carry):
        lf = _ssd_load(0, c, xc, dtg_ref, dtgT_ref, Sf)
        lb = _ssd_load(1, nc - 1 - c, xc, dtg_ref, dtgT_ref, Sb)
        (y_f, s_f), (y_b, s_b) = _ssd_chunks((lf, lb), *params)
        yf[lf[0], :] = y_f
        yb[lb[0], :] = y_b
        Sf[...] = s_f
        Sb[...] = s_b
        return carry
    lax.fori_loop(0, nc, scan_body, 0, unroll=2)
    sf_ref[0] = Sf[...].T
    sb_ref[0] = Sb[...].T

    def out_body(c, carry):
        rows = pl.ds(pl.multiple_of(c * q, q), q)
        ytot = yf[rows, :] + yb[rows, :] + dskip_ref[...] * xc[rows, 0:SSD_INNER].astype(F32)
        yg = ytot * _silu(z_ref[rows, :].astype(F32))
        ms = jnp.mean(yg * yg, axis=-1, keepdims=True)
        y_ref[rows, :] = (yg * lax.rsqrt(ms + EPS) * nrm_ref[...]).astype(y_ref.dtype)
        return carry
    lax.fori_loop(0, nc, out_body, 0)


def _ssd_call(n_seq, ln, blk0, proj, small, smallT3, init, params):
    q = SSD_CHUNK
    hp = SSD_HEADS * SSD_HEAD_DIM
    cw = SSD_INNER + 2 * SSD_GROUPS * SSD_STATE
    nc = ln // q
    assert PJ_XBC % cw == 0 and PJ_Z % SSD_INNER == 0
    tok = lambda w, cb: pl.BlockSpec((ln, w), lambda b: (blk0 + b, cb))
    seq3 = pl.BlockSpec((1, hp, SSD_STATE), lambda b: (b, 0, 0))
    full = lambda a: pl.BlockSpec(a.shape, lambda b: (0,) * a.ndim)
    init = () if init is None else tuple(init)
    return pl.pallas_call(
        functools.partial(_ssd_seq_kernel, has_init=bool(init)), grid=(n_seq,),
        in_specs=[tok(cw, PJ_XBC // cw), tok(SSD_INNER, PJ_Z // SSD_INNER), tok(LANES, 0),
                  pl.BlockSpec((nc, 2 * SSD_HEADS, q), lambda b: (blk0 + b, 0, 0))]
        + [full(a) for a in params] + [seq3] * len(init),
        out_specs=[pl.BlockSpec((ln, hp), lambda b: (b, 0)), seq3, seq3],
        out_shape=[jax.ShapeDtypeStruct((n_seq * ln, hp), BF16),
                   jax.ShapeDtypeStruct((n_seq, hp, SSD_STATE), F32),
                   jax.ShapeDtypeStruct((n_seq, hp, SSD_STATE), F32)],
        scratch_shapes=[pltpu.VMEM((ln, cw), BF16),
                        pltpu.VMEM((SSD_STATE, hp), F32), pltpu.VMEM((SSD_STATE, hp), F32),
                        pltpu.VMEM((ln, hp), F32), pltpu.VMEM((ln, hp), F32)],
        compiler_params=_cp(("arbitrary",)), name="ssd_seq",
    )(proj, proj, small, smallT3, *params, *init)


def _gla_gates(c, q_ref, k_ref, glr_ref, w2_ref, gb_ref, qi, ki, qo, kk, dec):
    t = GLA_BLOCK
    rows = pl.ds(pl.multiple_of(c * t, t), t)
    c0 = 2 * SSD_HEADS
    gps = [_dot_hilo(glr_ref[rows, c0 + GLA_RANK * d:c0 + GLA_RANK * (d + 1)], w2_ref[0, d], w2_ref[1, d])
           + gb_ref[d:d + 1, :] for d in (0, 1)]
    las = [-_softplus(-gp) * (1.0 / GLA_TAU) for gp in gps]
    row = lax.broadcasted_iota(I32, (t, t), 0)
    col = lax.broadcasted_iota(I32, (t, t), 1)
    es = [_dot_sel(col <= row, las[0]), _dot_sel(col < row, las[1])]
    mid = t // 2 - 1
    qf = q_ref[rows, :].astype(F32) * (GLA_KEY_DIM ** -0.5)
    kf = k_ref[rows, :].astype(F32)
    for d in (0, 1):
        e = es[d]
        r = e[mid:mid + 1, :]
        if d == 0:
            tot = e[t - 1:t, :]
            fqi, fki = jnp.exp(e - r), jnp.exp(r - e)
            fq, fk = jnp.exp(e), jnp.exp(tot - e)
        else:
            tot = e[t - 1:t, :] + las[1][t - 1:t, :]
            fqi, fki = jnp.exp(r - e), jnp.exp(e - r)
            fq, fk = jnp.exp(tot - e), jnp.exp(e)
        qi[d, rows, :] = (qf * fqi).astype(BF16)
        ki[d, rows, :] = (kf * fki).astype(BF16)
        qo[d, rows, :] = (qf * fq).astype(BF16)
        kk[d, rows, :] = (kf * fk).astype(BF16)
        dec[d, c] = jnp.broadcast_to(jnp.exp(tot), (8, tot.shape[1]))


def _gla_load(d, c, v_ref, qi, ki, qo, kk, dec, S):
    t = GLA_BLOCK
    rows = pl.ds(pl.multiple_of(c * t, t), t)
    return (rows, qi[d, rows, :], ki[d, rows, :], qo[d, rows, :], kk[d, rows, :], dec[d, c][0:1, :],
            v_ref[rows, :], [S[h] for h in range(GLA_HEADS)])


def _gla_blocks(loaded):
    t = GLA_BLOCK
    dk, dv = GLA_KEY_DIM, GLA_VAL_DIM
    row = lax.broadcasted_iota(I32, (t, t), 0)
    col = lax.broadcasted_iota(I32, (t, t), 1)
    heads = range(GLA_HEADS)
    ks = [slice(dk * h, dk * (h + 1)) for h in heads]
    first = []
    for _, q_in, k_in, q_st, k_st, dec, v, states in loaded:
        vs = [v[:, dv * h:dv * (h + 1)] for h in heads]
        scores = [_dot_nt(q_in[:, ks[h]], k_in[:, ks[h]]) for h in heads]
        carried = [_dot_nt(q_st[:, ks[h]], states[h].astype(BF16)) for h in heads]
        grown = [_dot_tn(vs[h], k_st[:, ks[h]]) for h in heads]
        first.append((vs, scores, carried, grown))
    results = []
    for d, ((_, _, _, _, _, dec, _, states), (vs, scores, carried, grown)) in enumerate(zip(loaded, first)):
        mask = row >= col if d == 0 else col >= row
        outs = [_dot(jnp.where(mask, scores[h], 0.0).astype(BF16), vs[h]) + carried[h] for h in heads]
        new_states = [states[h] * dec[:, ks[h]] + grown[h] for h in heads]
        results.append((jnp.concatenate(outs, axis=1), new_states))
    return results


def _gla_seq_kernel(*refs, has_init):
    q_ref, k_ref, v_ref, og_ref, glr_ref, w2_ref, gb_ref, nrm_ref = refs[:8]
    refs = refs[8:]
    if has_init:
        s0f_ref, s0b_ref = refs[:2]
        refs = refs[2:]
    o_ref, sf_ref, sb_ref, Sf, Sb, of, ob, qi, ki, qo, kk, dec = refs
    ln = q_ref.shape[0]
    t = GLA_BLOCK
    nc = ln // t
    dv = GLA_VAL_DIM
    for h in range(GLA_HEADS):
        if has_init:
            Sf[h] = s0f_ref[0, h].T
            Sb[h] = s0b_ref[0, h].T
        else:
            Sf[h] = jnp.zeros(Sf.shape[1:], F32)
            Sb[h] = jnp.zeros(Sb.shape[1:], F32)
    staged = (qi, ki, qo, kk, dec)

    def gate_body(c, carry):
        _gla_gates(c, q_ref, k_ref, glr_ref, w2_ref, gb_ref, *staged)
        return carry
    lax.fori_loop(0, nc, gate_body, 0, unroll=2)

    def scan_body(c, carry):
        lf = _gla_load(0, c, v_ref, *staged, Sf)
        lb = _gla_load(1, nc - 1 - c, v_ref, *staged, Sb)
        (o_f, s_f), (o_b, s_b) = _gla_blocks((lf, lb))
        of[lf[0], :] = o_f
        ob[lb[0], :] = o_b
        for h in range(GLA_HEADS):
            Sf[h] = s_f[h]
            Sb[h] = s_b[h]
        return carry
    lax.fori_loop(0, nc, scan_body, 0, unroll=2)
    for h in range(GLA_HEADS):
        sf_ref[0, h] = Sf[h].T
        sb_ref[0, h] = Sb[h].T

    def out_body(c, carry):
        rows = pl.ds(pl.multiple_of(c * t, t), t)
        for h in range(GLA_HEADS):
            vl = slice(dv * h, dv * (h + 1))
            ot = of[rows, vl] + ob[rows, vl]
            ms = jnp.mean(ot * ot, axis=-1, keepdims=True)
            on = ot * lax.rsqrt(ms + EPS) * nrm_ref[...]
            o_ref[rows, vl] = (on * _silu(og_ref[rows, vl].astype(F32))).astype(o_ref.dtype)
        return carry
    lax.fori_loop(0, nc, out_body, 0)


def _gla_call(n_seq, ln, blk0, proj, small, init, params):
    qk_w = GLA_HEADS * GLA_KEY_DIM
    v_w = GLA_HEADS * GLA_VAL_DIM
    tok = lambda w, cb: pl.BlockSpec((ln, w), lambda b: (blk0 + b, cb))
    seq4 = pl.BlockSpec((1, GLA_HEADS, GLA_KEY_DIM, GLA_VAL_DIM), lambda b: (b, 0, 0, 0))
    full = lambda a: pl.BlockSpec(a.shape, lambda b: (0,) * a.ndim)
    st_shape = jax.ShapeDtypeStruct((n_seq, GLA_HEADS, GLA_KEY_DIM, GLA_VAL_DIM), F32)
    init = () if init is None else tuple(init)
    return pl.pallas_call(
        functools.partial(_gla_seq_kernel, has_init=bool(init)), grid=(n_seq,),
        in_specs=[tok(qk_w, PJ_Q // qk_w), tok(qk_w, PJ_K // qk_w), tok(v_w, PJ_V // v_w),
                  tok(v_w, PJ_OG // v_w), tok(LANES, 0)] + [full(a) for a in params] + [seq4] * len(init),
        out_specs=[pl.BlockSpec((ln, v_w), lambda b: (b, 0)), seq4, seq4],
        out_shape=[jax.ShapeDtypeStruct((n_seq * ln, v_w), BF16), st_shape, st_shape],
        scratch_shapes=[pltpu.VMEM((GLA_HEADS, GLA_VAL_DIM, GLA_KEY_DIM), F32),
                        pltpu.VMEM((GLA_HEADS, GLA_VAL_DIM, GLA_KEY_DIM), F32),
                        pltpu.VMEM((ln, v_w), F32), pltpu.VMEM((ln, v_w), F32)]
        + [pltpu.VMEM((2, ln, qk_w), BF16)] * 4 + [pltpu.VMEM((2, ln // GLA_BLOCK, 8, qk_w), F32)],
        compiler_params=_cp(("arbitrary",)), name="gla_seq",
    )(proj, proj, proj, proj, small, *params, *init)


def l0_mixers(lay, proj, small, small_t, ssd_f0, ssd_b0, gla_f0, gla_b0, conv_w, conv_b, a_log, dt_bias, d_skip,
              ssd_norm, gate_w2, gate_b, gla_norm):
    hp = SSD_HEADS * SSD_HEAD_DIM
    ssd_p = (conv_w, conv_b.reshape(1, -1), a_log, a_log.T, dt_bias, dt_bias.T,
             jnp.repeat(d_skip, SSD_HEAD_DIM).reshape(1, hp), ssd_norm.reshape(1, hp))
    gla_p = (_hilo(gate_w2), gate_b, gla_norm.reshape(1, -1))
    np_, ns = lay.n_prompt, lay.n_sample
    assert lay.p_tok % lay.sample_len == 0
    groups = [(np_, lay.prompt_len, 0, None, None),
              (ns, lay.sample_len, lay.p_tok // lay.sample_len,
               (ssd_f0.reshape(ns, hp, SSD_STATE), ssd_b0.reshape(ns, hp, SSD_STATE)), (gla_f0, gla_b0))]
    ys, os_, states = [], [], None
    for n, ln, blk0, ssd_init, gla_init in groups:
        y, sf, sb = _ssd_call(n, ln, blk0, proj, small, small_t, ssd_init, ssd_p)
        o, gf, gb = _gla_call(n, ln, blk0, proj, small, gla_init, gla_p)
        ys.append(y)
        os_.append(o)
        if states is None:
            states = (sf, sb, gf, gb)
    return tuple(ys), tuple(os_), states


def _res_kernel(*refs, counts, ks, npt):
    in_prompt = pl.program_id(0) < npt
    streams, pos = [], 0
    for c in counts:
        streams.append(refs[pos:pos + c])
        pos += c
    w_ref, gate_ref, o_ref = refs[pos:]
    acc = None
    off = 0
    for a_refs, k in zip(streams[:-1], ks):
        part = _dot(_stream_tile(a_refs, in_prompt), w_ref[off:off + k, :])
        acc = part if acc is None else acc + part
        off += k
    o_ref[...] = _stream_tile(streams[-1], in_prompt) + gate_ref[...] * acc


def proj_residual(lay, acts, w, x, gate, tm=512):
    arrays, specs, counts = [], [], []
    for s in list(acts) + [x]:
        a, sp = _stream_specs(lay, s, tm)
        arrays += a
        specs += sp
        counts.append(len(a))
    ks = tuple(int((a[0] if isinstance(a, (tuple, list)) else a).shape[1]) for a in acts)
    mrow = lambda i: (lay.mod_row(i * tm), 0, 0)
    return pl.pallas_call(
        functools.partial(_res_kernel, counts=tuple(counts), ks=ks, npt=lay.p_tok // tm),
        grid=(lay.n_tok // tm,),
        in_specs=specs + [pl.BlockSpec(w.shape, lambda i: (0, 0)), pl.BlockSpec((None, 1, D), mrow)],
        out_specs=pl.BlockSpec((tm, D), lambda i: (i, 0)),
        out_shape=jax.ShapeDtypeStruct((lay.n_tok, D), F32),
        compiler_params=_cp(("arbitrary",)), name="proj_residual",
    )(*arrays, w, gate)


def _router_kernel(x_ref, g_ref, sc_ref, sh_ref, rw_ref, rb_ref,
                   h_ref, idx_ref, gate_ref, pos_ref, posT_ref, cnt_ref):
    tm = x_ref.shape[0]
    h = _modnorm(x_ref[...], g_ref[...], sc_ref[...], sh_ref[...])
    h_hi = h.astype(BF16)
    h_ref[...] = h_hi
    h_lo = (h - h_hi.astype(F32)).astype(BF16)
    lg = (_dot(h_hi, rw_ref[0]) + _dot(h_lo, rw_ref[0]) + _dot(h_hi, rw_ref[1])
          + rb_ref[...])
    lane = lax.broadcasted_iota(I32, (tm, LANES), 1).astype(F32)
    vals, ids = [], []
    for _ in range(TOP_K):
        m = jnp.max(lg, axis=1, keepdims=True)
        i = jnp.min(jnp.where(lg == m, lane, float(LANES)), axis=1, keepdims=True)
        vals.append(m)
        ids.append(i)
        lg = jnp.where(lane == i, -jnp.inf, lg)
    ex = [jnp.exp(v - vals[0]) for v in vals]
    den = ex[0] + ex[1] + ex[2] + ex[3]
    sel = jnp.zeros((tm, LANES), F32)
    for i in ids:
        sel = sel + (lane == i).astype(F32)
    row = lax.broadcasted_iota(I32, (tm, tm), 0)
    col = lax.broadcasted_iota(I32, (tm, tm), 1)
    before = _dot((col < row).astype(BF16), sel.astype(BF16))
    n = jnp.sum(sel, axis=0, keepdims=True)
    er = lax.broadcasted_iota(I32, (LANES, LANES), 0)
    ec = lax.broadcasted_iota(I32, (LANES, LANES), 1)
    n_al = jnp.ceil(n * (1.0 / SEG_ALIGN)) * SEG_ALIGN
    offs = _dot(jnp.broadcast_to(n_al, (8, LANES)).astype(BF16), (er < ec).astype(BF16))[0:1, :]
    slot = before + offs
    idx_o = jnp.zeros((tm, LANES), F32)
    gate_o = jnp.zeros((tm, LANES), F32)
    pos_o = jnp.zeros((tm, LANES), F32)
    for k in range(TOP_K):
        p = jnp.sum(jnp.where(lane == ids[k], slot, 0.0), axis=1, keepdims=True)
        idx_o = jnp.where(lane == k, ids[k], idx_o)
        gate_o = jnp.where(lane == k, ex[k] / den, gate_o)
        pos_o = jnp.where(lane == k, p, pos_o)
    idx_ref[...] = idx_o.astype(I32)
    gate_ref[...] = gate_o
    pos_ref[...] = pos_o.astype(I32)
    posT_ref[...] = pos_o.T[0:8, :]
    cnt_ref[0] = jnp.broadcast_to(n, (8, LANES))


def moe_router(lay, x, g, sc, sh, rw, rb):
    n_tok = x.shape[0]
    tm = TOK_TILE
    nt = n_tok // tm
    mrow = lambda i: (lay.mod_row(i * tm), 0, 0)
    tile = lambda w, dt: (pl.BlockSpec((tm, w), lambda i: (i, 0)), jax.ShapeDtypeStruct((n_tok, w), dt))
    outs = [tile(D, BF16), tile(LANES, I32), tile(LANES, F32), tile(LANES, I32),
            (pl.BlockSpec((8, tm), lambda i: (0, i)), jax.ShapeDtypeStruct((8, n_tok), F32)),
            (pl.BlockSpec((1, 8, LANES), lambda i: (i, 0, 0)), jax.ShapeDtypeStruct((nt, 8, LANES), F32))]
    return pl.pallas_call(
        _router_kernel, grid=(nt,),
        in_specs=[pl.BlockSpec((tm, D), lambda i: (i, 0)),
                  pl.BlockSpec((1, D), lambda i: (0, 0)),
                  pl.BlockSpec((None, 1, D), mrow), pl.BlockSpec((None, 1, D), mrow),
                  pl.BlockSpec((2, D, LANES), lambda i: (0, 0, 0)),
                  pl.BlockSpec((1, LANES), lambda i: (0, 0))],
        out_specs=[o[0] for o in outs], out_shape=[o[1] for o in outs],
        compiler_params=_cp(("arbitrary",)), name="moe_router",
    )(x, g.reshape(1, D), sc, sh, rw, rb)


SEG_ALIGN = 8
SEG_CHUNK = 16
REST_BITS = tuple(range(int(math.log2(SEG_CHUNK)) - 1, int(math.log2(SEG_ALIGN)) - 1, -1))
TILE_ROWS = TOK_TILE * TOP_K + N_EXPERTS * SEG_ALIGN


def _pow2_copies(n, src, dst, make_copy, op, bits):
    for b in bits:
        sz = 1 << b
        done = (n >> (b + 1)) << (b + 1)

        @pl.when((n & sz) != 0)
        def _():
            op(make_copy(pl.multiple_of(src + done, SEG_ALIGN), pl.multiple_of(dst + done, SEG_ALIGN), sz))


def _start_segments(i, n_ref, off_ref, dst_ref, make_copy):
    def body(e, carry):
        k = i * N_EXPERTS + e
        n, src, dst = n_ref[k], off_ref[k], dst_ref[k]

        def chunk(j, c):
            o = pl.multiple_of(j * SEG_CHUNK, SEG_CHUNK)
            make_copy(pl.multiple_of(src + o, SEG_ALIGN), pl.multiple_of(dst + o, SEG_ALIGN),
                      SEG_CHUNK).start(priority=1)
            return c
        shift = int(math.log2(SEG_CHUNK))
        full = n >> shift
        lax.fori_loop(0, full, chunk, 0)
        done = full << shift
        _pow2_copies(n - done, src + done, dst + done, make_copy, lambda c: c.start(), REST_BITS)
        return carry
    lax.fori_loop(0, N_EXPERTS, body, 0)


TAIL_BITS = tuple(range(int(math.log2(MOE_BLOCK)) - 1, int(math.log2(SEG_ALIGN)) - 1, -1))
TILE_BITS = tuple(range(int(math.log2(TILE_ROWS)), int(math.log2(SEG_ALIGN)) - 1, -1))


def _wait_rows(total, make_copy):
    _pow2_copies(total, 0, 0, make_copy, lambda c: c.wait(), TILE_BITS)


def _dispatch_kernel(n_ref, off_ref, dst_ref, tot_ref, tn_ref, td_ref, posT_ref, h_ref, xout_ref,
                     srt, zbuf, sems):
    i = pl.program_id(0)
    last = pl.num_programs(0) - 1
    slot = i % 2
    tm = h_ref.shape[0]
    r = lax.broadcasted_iota(I32, (TILE_ROWS, tm), 0)
    hit = jnp.zeros((TILE_ROWS, tm), jnp.bool_)
    for k in range(TOP_K):
        hit = hit | (r == posT_ref[k:k + 1, :].astype(I32))
    sel = jnp.where(hit, 1.0, 0.0).astype(BF16)
    srt[slot] = _pack_rows(_dot(sel, h_ref[...]))

    def copier(s):
        def make_copy(src, dst, sz):
            return pltpu.make_async_copy(srt.at[s, pl.ds(src, sz)], xout_ref.at[pl.ds(dst, sz)], sems.at[s])
        return make_copy

    _start_segments(i, n_ref, off_ref, dst_ref, copier(slot))

    @pl.when(i > 0)
    def _():
        _wait_rows(tot_ref[jnp.maximum(i - 1, 0)], copier(1 - slot))

    @pl.when(i == last)
    def _():
        _wait_rows(tot_ref[i], copier(slot))
        zbuf[...] = jnp.zeros_like(zbuf)
        sem = sems.at[0]

        def zero_copy(src, dst, sz):
            return pltpu.make_async_copy(zbuf.at[pl.ds(src, sz)], xout_ref.at[pl.ds(dst, sz)], sem)

        nb = xout_ref.shape[0] // MOE_BLOCK
        for op in (lambda c: c.start(), lambda c: c.wait()):
            def body(e, carry):
                _pow2_copies(tn_ref[e], 0, td_ref[e], zero_copy, op, TAIL_BITS)
                return carry
            lax.fori_loop(0, N_EXPERTS, body, 0)

            def unused(b, carry):
                op(zero_copy(0, pl.multiple_of(b * MOE_BLOCK, MOE_BLOCK), MOE_BLOCK))
                return carry
            lax.fori_loop(tn_ref[N_EXPERTS], nb, unused, 0)


def moe_dispatch(n_tab, off_tab, dst_tab, tot_tab, tail_n, tail_dst, posT, h2, n_rows):
    n_tok = h2.shape[0]
    tm = TOK_TILE
    grid_spec = pltpu.PrefetchScalarGridSpec(
        num_scalar_prefetch=6, grid=(n_tok // tm,),
        in_specs=[pl.BlockSpec((8, tm), lambda i, *_: (0, i)),
                  pl.BlockSpec((tm, D), lambda i, *_: (i, 0))],
        out_specs=pl.BlockSpec(memory_space=pl.ANY),
        scratch_shapes=[pltpu.VMEM((2, TILE_ROWS, ROW_WORDS), U32), pltpu.VMEM((MOE_BLOCK, ROW_WORDS), U32),
                        pltpu.SemaphoreType.DMA((2,))])
    return pl.pallas_call(
        _dispatch_kernel, grid_spec=grid_spec,
        out_shape=jax.ShapeDtypeStruct((n_rows, ROW_WORDS), U32),
        compiler_params=_cp(("arbitrary",)), name="moe_dispatch",
    )(n_tab, off_tab, dst_tab, tot_tab, tail_n, tail_dst, posT, h2)


def _combine_kernel(n_ref, off_ref, dst_ref, tot_ref, pos_ref, gate_ref, x_ref, g2_ref, y_ref, *rest, npt):
    o_refs, (buf, sems) = rest[:-2], rest[-2:]
    i = pl.program_id(0)
    last = pl.num_programs(0) - 1
    slot = i % 2
    tm = x_ref.shape[0]
    na = TILE_ROWS

    def copier(s):
        def make_copy(src, dst, sz):
            return pltpu.make_async_copy(y_ref.at[pl.ds(dst, sz)], buf.at[s, pl.ds(src, sz)], sems.at[s])
        return make_copy

    def fetch(tile, s):
        buf[s, tm * TOP_K:na, :] = jnp.zeros((na - tm * TOP_K, ROW_WORDS), U32)
        _start_segments(tile, n_ref, off_ref, dst_ref, copier(s))

    @pl.when(i == 0)
    def _():
        fetch(i, slot)

    @pl.when(i < last)
    def _():
        fetch(i + 1, 1 - slot)

    _wait_rows(tot_ref[i], copier(slot))
    kc = 256
    mixed = None
    for k0 in range(0, na, kc):
        lane = lax.broadcasted_iota(I32, (tm, kc), 1) + k0
        pw = jnp.zeros((tm, kc), F32)
        for k in range(TOP_K):
            pw = pw + jnp.where(lane == pos_ref[:, k:k + 1], gate_ref[:, k:k + 1], 0.0)
        phi = pw.astype(BF16)
        plo = (pw - phi.astype(F32)).astype(BF16)
        yb = _unpack_rows(buf[slot, k0:k0 + kc, :]).astype(BF16)
        part = _dot(phi, yb) + _dot(plo, yb)
        mixed = part if mixed is None else mixed + part
    res = x_ref[...] + g2_ref[...] * mixed
    if len(o_refs) == 1:
        o_refs[0][...] = res
    else:
        @pl.when(i < npt)
        def _():
            o_refs[0][...] = res

        @pl.when(i >= npt)
        def _():
            o_refs[1][...] = res


def moe_combine(lay, n_tab, off_tab, dst_tab, tot_tab, pos, gates, x, gate2, y_rows, split):
    n_tok = x.shape[0]
    tm = TOK_TILE
    npt = lay.p_tok // tm
    mrow = lambda i, *_: (lay.mod_row(i * tm), 0, 0)
    if split:
        out_specs = [pl.BlockSpec((tm, D), lambda i, *_: (jnp.minimum(i, npt - 1), 0)),
                     pl.BlockSpec((tm, D), lambda i, *_: (jnp.maximum(i - npt, 0), 0))]
        out_shape = [jax.ShapeDtypeStruct((lay.p_tok, D), F32), jax.ShapeDtypeStruct((n_tok - lay.p_tok, D), F32)]
    else:
        out_specs = pl.BlockSpec((tm, D), lambda i, *_: (i, 0))
        out_shape = jax.ShapeDtypeStruct((n_tok, D), F32)
    grid_spec = pltpu.PrefetchScalarGridSpec(
        num_scalar_prefetch=4, grid=(n_tok // tm,),
        in_specs=[pl.BlockSpec((tm, LANES), lambda i, *_: (i, 0)),
                  pl.BlockSpec((tm, LANES), lambda i, *_: (i, 0)),
                  pl.BlockSpec((tm, D), lambda i, *_: (i, 0)),
                  pl.BlockSpec((None, 1, D), mrow),
                  pl.BlockSpec(memory_space=pl.ANY)],
        out_specs=out_specs,
        scratch_shapes=[pltpu.VMEM((2, TILE_ROWS, ROW_WORDS), U32), pltpu.SemaphoreType.DMA((2,))])
    return pl.pallas_call(
        functools.partial(_combine_kernel, npt=npt), grid_spec=grid_spec, out_shape=out_shape,
        compiler_params=_cp(("arbitrary",)), name="moe_combine",
    )(n_tab, off_tab, dst_tab, tot_tab, pos, gates, x, gate2, y_rows)


def _expert_kernel(be_ref, nv_ref, nxt_ref, slot_ref, x_ref, b_ref, wg_hbm, wu_hbm, wd_hbm,
                   y_ref, wf, sems, *, layer):
    i = pl.program_id(0)
    valid = i < nv_ref[0]
    e = be_ref[i]
    slot = slot_ref[e]
    changed = jnp.logical_or(i == 0, e != be_ref[jnp.maximum(i - 1, 0)])

    def weight_copies(ex, s):
        return [pltpu.make_async_copy(w.at[layer, ex], wf.at[s, k], sems.at[s, k])
                for k, w in enumerate((wg_hbm, wu_hbm, wd_hbm))]

    @pl.when(jnp.logical_and(valid, changed))
    def _():
        @pl.when(i == 0)
        def _():
            for c in weight_copies(e, slot):
                c.start()

        nxt = nxt_ref[e]

        @pl.when(nxt >= 0)
        def _():
            for c in weight_copies(nxt, 1 - slot):
                c.start(priority=1)

        for c in weight_copies(e, slot):
            c.wait()

    @pl.when(valid)
    def _():
        x = _unpack_rows(x_ref[...])
        b = b_ref[e]
        gt = jnp.minimum(_dot(x, wf[slot, 0]) + b[0:1, :], SWIGLU_LIMIT)
        up = jnp.clip(_dot(x, wf[slot, 1]) + b[1:2, :], -SWIGLU_LIMIT, SWIGLU_LIMIT)
        act = (up + 1.0) * gt * _sigmoid(SWIGLU_ALPHA * gt)
        y = _dot(act, wf[slot, 2]) + b[2:3, :]
        y_ref[...] = _pack_rows(y.astype(BF16).astype(F32))

    @pl.when(jnp.logical_not(valid))
    def _():
        y_ref[...] = jnp.zeros_like(y_ref)


def moe_experts(layer, blk_expert, n_valid, next_expert, slot, x_rows, w_gate, b_gate, w_up, b_up, w_down,
                b_down):
    n_rows = x_rows.shape[0]
    nb = n_rows // MOE_BLOCK
    depth, ne, _, ff = w_gate.shape
    assert ff == D
    rowblk = lambda i, be, nv, *_: (jnp.maximum(jnp.minimum(i, nv[0] - 1), 0), 0)
    hbm = pl.BlockSpec(memory_space=pl.ANY)
    biases = jnp.stack([b_gate, b_up, b_down], axis=2)
    grid_spec = pltpu.PrefetchScalarGridSpec(
        num_scalar_prefetch=4, grid=(nb,),
        in_specs=[pl.BlockSpec((MOE_BLOCK, ROW_WORDS), rowblk),
                  pl.BlockSpec((None, ne, 3, D), lambda i, *_: (layer, 0, 0, 0)), hbm, hbm, hbm],
        out_specs=pl.BlockSpec((MOE_BLOCK, ROW_WORDS), lambda i, *_: (i, 0)),
        scratch_shapes=[pltpu.VMEM((2, 3, D, ff), F32), pltpu.SemaphoreType.DMA((2, 3))])
    return pl.pallas_call(
        functools.partial(_expert_kernel, layer=layer), grid_spec=grid_spec,
        out_shape=jax.ShapeDtypeStruct((n_rows, ROW_WORDS), U32),
        compiler_params=_cp(("arbitrary",)), name="moe_experts",
    )(blk_expert, n_valid, next_expert, slot, x_rows, biases, w_gate, w_up, w_down)


def moe_layer(lay, layer, x, g2, sc2, sh2, gate2, router_w, router_b, w_gate, b_gate, w_up, b_up, w_down,
              b_down, split=False):
    n_tok = x.shape[0]
    nt = n_tok // TOK_TILE
    rw = jnp.zeros((D, LANES), F32).at[:, :N_EXPERTS].set(router_w)
    rw = _hilo(rw)
    rb = jnp.full((1, LANES), NEG, F32).at[0, :N_EXPERTS].set(router_b)
    h2, _, gates, pos, posT, cnt = moe_router(lay, x, g2, sc2, sh2, rw, rb)
    n_te = cnt[:, 0, :N_EXPERTS].astype(I32)
    n_te = (n_te + SEG_ALIGN - 1) // SEG_ALIGN * SEG_ALIGN
    totals = jnp.sum(n_te, axis=0)
    padded = (totals + MOE_BLOCK - 1) // MOE_BLOCK * MOE_BLOCK
    padded_end = jnp.cumsum(padded)
    pstart = padded_end - padded
    dst = pstart[None, :] + jnp.cumsum(n_te, axis=0) - n_te
    off = jnp.cumsum(n_te, axis=1) - n_te
    n_rows = nt * TILE_ROWS + N_EXPERTS * MOE_BLOCK
    nb = n_rows // MOE_BLOCK
    n_valid = (padded_end[-1] // MOE_BLOCK).astype(I32).reshape(1)
    bstart = jnp.minimum(jnp.arange(nb, dtype=I32), n_valid[0] - 1) * MOE_BLOCK
    blk_expert = jnp.minimum(jnp.sum((bstart[:, None] >= padded_end[None, :]).astype(I32), axis=1),
                             N_EXPERTS - 1).astype(I32)
    tabs = (n_te.reshape(-1).astype(I32), off.reshape(-1).astype(I32), dst.reshape(-1).astype(I32),
            jnp.sum(n_te, axis=1).astype(I32))
    tail_n = jnp.concatenate([(padded - totals).astype(I32), n_valid])
    x_rows = moe_dispatch(*tabs, tail_n, (pstart + totals).astype(I32), posT, h2, n_rows)
    owner = jnp.where(padded > 0, jnp.arange(N_EXPERTS, dtype=I32), N_EXPERTS)
    later = jnp.concatenate([lax.cummin(owner, axis=0, reverse=True)[1:], jnp.full((1,), N_EXPERTS, I32)])
    next_expert = jnp.where(later < N_EXPERTS, later, -1).astype(I32)
    slot = ((jnp.cumsum((padded > 0).astype(I32)) - 1) % 2).astype(I32)
    y_rows = moe_experts(layer, blk_expert, n_valid, next_expert, slot, x_rows, w_gate, b_gate, w_up, b_up,
                         w_down, b_down)
    return moe_combine(lay, *tabs, pos, gates, x, gate2, y_rows, split)


QKV_TN = 256
N_QK_TILES = (ATT_HEADS + ATT_KV) * ATT_HD // QKV_TN


def _qkv_kernel(x_ref, g_ref, sc_ref, sh_ref, w_ref, nw_ref, cos_ref, sin_ref, o_ref):
    tm = x_ref.shape[0]
    h = _modnorm(x_ref[...], g_ref[...], sc_ref[...], sh_ref[...]).astype(BF16)
    r = lax.broadcasted_iota(I32, (QKV_TN, QKV_TN), 0) // ATT_HD
    c = lax.broadcasted_iota(I32, (QKV_TN, QKV_TN), 1) // ATT_HD
    head_mean = jnp.where(r == c, 1.0 / ATT_HD, 0.0).astype(BF16)
    lane = lax.broadcasted_iota(I32, (tm, QKV_TN), 1)
    half = ATT_HD // 4
    first = (lane % (2 * half)) < half
    acc_all = _dot(h, w_ref[...])
    tiles = [acc_all[:, QKV_TN * j:QKV_TN * (j + 1)] for j in range(w_ref.shape[1] // QKV_TN)]
    sq = jnp.concatenate([(t * t).astype(BF16) for t in tiles[:N_QK_TILES]], axis=0)
    ms_all = _dot(sq, head_mean)
    for j, acc in enumerate(tiles):
        cols = slice(QKV_TN * j, QKV_TN * (j + 1))
        if j >= N_QK_TILES:
            o_ref[:, cols] = acc
            continue
        qn = acc * lax.rsqrt(ms_all[tm * j:tm * (j + 1)] + EPS) * nw_ref[j]
        swapped = jnp.where(first, pltpu.roll(qn, QKV_TN - half, 1), pltpu.roll(qn, half, 1))
        o_ref[:, cols] = qn * cos_ref[...] + swapped * sin_ref[...]


def _rope_tables(sample_len):
    pos = np.arange(sample_len)
    half = ATT_HD // 4
    inv = (ROPE_THETA ** (-np.arange(half, dtype=np.float32) / half)).astype(np.float32)
    ang_r = (pos // GRID_W).astype(np.float32)[:, None] * inv[None, :]
    ang_c = (pos % GRID_W).astype(np.float32)[:, None] * inv[None, :]
    cos = np.concatenate([np.cos(ang_r)] * 2 + [np.cos(ang_c)] * 2, axis=1)
    sin = np.concatenate([-np.sin(ang_r), np.sin(ang_r), -np.sin(ang_c), np.sin(ang_c)], axis=1)
    rep = QKV_TN // ATT_HD
    return (jnp.asarray(np.tile(cos, (1, rep)), F32), jnp.asarray(np.tile(sin, (1, rep)), F32))


def qkv_proj(lay, x, g, sc, sh, w, q_norm, k_norm, tm=512):
    n_tok = x.shape[0]
    n = w.shape[1]
    nq = ATT_HEADS * ATT_HD // QKV_TN
    rep = QKV_TN // ATT_HD
    nw = jnp.concatenate([jnp.tile(jnp.tile(q_norm, rep)[None, :], (nq, 1)),
                          jnp.tile(jnp.tile(k_norm, rep)[None, :], (n // QKV_TN - nq, 1))], axis=0)
    cos, sin = _rope_tables(lay.sample_len)
    cos = jnp.concatenate([jnp.ones((tm, QKV_TN), F32), cos], axis=0)
    sin = jnp.concatenate([jnp.zeros((tm, QKV_TN), F32), sin], axis=0)
    assert lay.p_tok % tm == 0 and lay.sample_len % tm == 0
    mrow = lambda i: (lay.mod_row(i * tm), 0, 0)
    rrow = lambda i: (jnp.where(i * tm < lay.p_tok, 0, 1 + ((i * tm - lay.p_tok) % lay.sample_len) // tm), 0)
    nt = n // QKV_TN
    return pl.pallas_call(
        _qkv_kernel, grid=(n_tok // tm,),
        in_specs=[pl.BlockSpec((tm, D), lambda i: (i, 0)),
                  pl.BlockSpec((1, D), lambda i: (0, 0)),
                  pl.BlockSpec((None, 1, D), mrow), pl.BlockSpec((None, 1, D), mrow),
                  pl.BlockSpec((D, n), lambda i: (0, 0)),
                  pl.BlockSpec((nt, 1, QKV_TN), lambda i: (0, 0, 0)),
                  pl.BlockSpec((tm, QKV_TN), rrow), pl.BlockSpec((tm, QKV_TN), rrow)],
        out_specs=pl.BlockSpec((tm, n), lambda i: (i, 0)),
        out_shape=jax.ShapeDtypeStruct((n_tok, n), F32),
        compiler_params=_cp(("arbitrary",)), name="qkv_proj",
    )(x, g.reshape(1, D), sc, sh, w, nw.reshape(nt, 1, QKV_TN), cos, sin)


def _dup_group(x, g):
    blk = x[:, LANES * (g // 2):LANES * (g // 2 + 1)]
    if g % 2 == 1:
        blk = pltpu.roll(blk, ATT_HD, 1)
    lo = lax.broadcasted_iota(I32, blk.shape, 1) < ATT_HD
    low = jnp.where(lo, blk, 0.0)
    return low + pltpu.roll(low, ATT_HD, 1)


def _attend(q_ref, k_all, v_all, mask, sink_ref, o_ref):
    nq = q_ref.shape[0]
    lo = lax.broadcasted_iota(I32, (nq, LANES), 1) < ATT_HD
    first = lax.broadcasted_iota(I32, (2 * nq, 1), 0) < nq
    if mask is not None:
        mask = jnp.concatenate([mask, mask], axis=0)
    pairs_per_group = ATT_HEADS // ATT_KV // 2
    n_pairs = ATT_HEADS // 2
    kv = {}

    def group_kv(g):
        if g not in kv:
            kv[g] = (_dup_group(k_all, g).astype(BF16), _dup_group(v_all, g).astype(BF16))
        return kv[g]

    def scores(j):
        qp = q_ref[:, LANES * j:LANES * (j + 1)] * (ATT_HD ** -0.5)
        qs = jnp.concatenate([jnp.where(lo, qp, 0.0), jnp.where(lo, 0.0, qp)], axis=0)
        return _dot_nt(qs.astype(BF16), group_kv(j // pairs_per_group)[0])

    ahead = 2
    queue = [scores(j) for j in range(min(ahead, n_pairs))]
    for j in range(n_pairs):
        s = queue.pop(0)
        if j + ahead < n_pairs:
            queue.append(scores(j + ahead))
        if mask is not None:
            s = jnp.where(mask, s, NEG)
        sink = jnp.where(first, sink_ref[2 * j], sink_ref[2 * j + 1])
        m = jnp.maximum(jnp.max(s, axis=1, keepdims=True), sink)
        p = jnp.exp(s - m)
        den = jnp.sum(p, axis=1, keepdims=True) + jnp.exp(sink - m)
        o = _dot(p.astype(BF16), group_kv(j // pairs_per_group)[1]) / den
        o_ref[:, LANES * j:LANES * (j + 1)] = jnp.where(lo, o[:nq], o[nq:]).astype(o_ref.dtype)


def _attn_ctx_kernel(sink_ref, q_ref, k_ref, v_ref, o_ref):
    _attend(q_ref, k_ref[...], v_ref[...], None, sink_ref, o_ref)


def attn_context(lay, qkv, sinks):
    qw = ATT_HEADS * ATT_HD
    kw = ATT_KV * ATT_HD
    ln = lay.prompt_len
    grid_spec = pltpu.PrefetchScalarGridSpec(
        num_scalar_prefetch=0, grid=(lay.n_prompt,),
        in_specs=[pl.BlockSpec(memory_space=pltpu.SMEM),
                  pl.BlockSpec((ln, qw), lambda b: (b, 0)),
                  pl.BlockSpec((ln, kw), lambda b: (b, qw // kw)),
                  pl.BlockSpec((ln, kw), lambda b: (b, qw // kw + 1))],
        out_specs=pl.BlockSpec((ln, qw), lambda b: (b, 0)))
    return pl.pallas_call(
        _attn_ctx_kernel, grid_spec=grid_spec,
        out_shape=jax.ShapeDtypeStruct((lay.p_tok, qw), BF16),
        compiler_params=_cp(("arbitrary",)), name="attn_context",
    )(sinks, qkv, qkv, qkv)


def _attn_lat_kernel(sink_ref, q_ref, kp_ref, kc_ref, kn_ref, vp_ref, vc_ref, vn_ref, ck_ref, cv_ref, o_ref,
                     *, nblk):
    i = pl.program_id(1)
    bq = ATT_BLOCK
    nctx = ck_ref.shape[1]
    k_all = jnp.concatenate([kp_ref[...], kc_ref[...], kn_ref[...], ck_ref[0]], axis=0)
    v_all = jnp.concatenate([vp_ref[...], vc_ref[...], vn_ref[...], cv_ref[0]], axis=0)
    ns = 3 * bq + nctx
    r = lax.broadcasted_iota(I32, (bq, ns), 0)
    c = lax.broadcasted_iota(I32, (bq, ns), 1)
    rel = c - r
    first_key = jnp.where(i > 0, 0, bq)
    end_key = jnp.where(i < nblk - 1, 3 * bq, 2 * bq)
    band = (rel >= bq - WINDOW) & (rel <= bq + WINDOW) & (c >= first_key) & (c < end_key)
    mask = band | (c >= 3 * bq)
    _attend(q_ref, k_all, v_all, mask, sink_ref, o_ref)


def attn_latent(lay, qkv, cache_k, cache_v, sinks):
    qw = ATT_HEADS * ATT_HD
    kw = ATT_KV * ATT_HD
    bq = ATT_BLOCK
    nblk = lay.sample_len // bq
    b0 = lay.p_tok // bq
    nctx = cache_k.shape[1]
    rb = lambda b, i: b0 + b * nblk + i
    kspec = lambda cb, sh: pl.BlockSpec(
        (bq, kw), lambda b, i: (b0 + b * nblk + jnp.clip(i + sh, 0, nblk - 1), cb))
    kc, vc = qw // kw, qw // kw + 1
    grid_spec = pltpu.PrefetchScalarGridSpec(
        num_scalar_prefetch=0, grid=(lay.n_sample, nblk),
        in_specs=[pl.BlockSpec(memory_space=pltpu.SMEM),
                  pl.BlockSpec((bq, qw), lambda b, i: (rb(b, i), 0)),
                  kspec(kc, -1), kspec(kc, 0), kspec(kc, 1),
                  kspec(vc, -1), kspec(vc, 0), kspec(vc, 1),
                  pl.BlockSpec((1, nctx, kw), lambda b, i: (b, 0, 0)),
                  pl.BlockSpec((1, nctx, kw), lambda b, i: (b, 0, 0))],
        out_specs=pl.BlockSpec((bq, qw), lambda b, i: (b * nblk + i, 0)))
    return pl.pallas_call(
        functools.partial(_attn_lat_kernel, nblk=nblk), grid_spec=grid_spec,
        out_shape=jax.ShapeDtypeStruct((lay.n_sample * lay.sample_len, qw), BF16),
        compiler_params=_cp(("arbitrary", "arbitrary")), name="attn_latent",
    )(sinks, qkv, qkv, qkv, qkv, qkv, qkv, qkv,
      cache_k.reshape(lay.n_sample, nctx, kw), cache_v.reshape(lay.n_sample, nctx, kw))


def _forward(lay, x_prompt, x_sample, state_l0_ssd_fwd, state_l0_ssd_bwd, state_l0_gla_fwd, state_l0_gla_bwd,
             cache_l1_k, cache_l1_v, c, c_ctx, ada_w, ada_b, norm1, norm2,
             l0_w_in, l0_conv_w, l0_conv_b, l0_a_log, l0_dt_bias, l0_d_skip, l0_ssd_norm,
             l0_gate_w2, l0_gate_b, l0_gla_norm, l0_w_out,
             l1_w_qkv, l1_q_norm, l1_k_norm, l1_sinks, l1_w_out,
             router_w, router_b, exp_w_gate, exp_b_gate, exp_w_up, exp_b_up, exp_w_down, exp_b_down):
    np_, ns = lay.n_prompt, lay.n_sample
    x = (x_prompt.reshape(-1, D), x_sample.reshape(-1, D))
    cond8 = jnp.zeros((8, D), F32).at[0].set(c_ctx).at[1:1 + ns].set(c)
    mod = ada_table(cond8, ada_w, ada_b)
    mods = [[mod[l, :, p * D:(p + 1) * D].reshape(8, 1, D) for p in range(N_ADA)] for l in range(2)]

    def moe(l, xx, split=False):
        return moe_layer(lay, l, xx, norm2[l], mods[l][4], mods[l][3], mods[l][5], router_w[l], router_b[l],
                         exp_w_gate, exp_b_gate, exp_w_up, exp_b_up, exp_w_down, exp_b_down, split=split)

    sp = np.cumsum((SSD_INNER, SSD_INNER + 2 * SSD_GROUPS * SSD_STATE, 2 * SSD_HEADS,
                    GLA_HEADS * GLA_KEY_DIM, GLA_HEADS * GLA_KEY_DIM,
                    GLA_HEADS * GLA_VAL_DIM, GLA_HEADS * GLA_VAL_DIM, 2 * GLA_RANK))
    cols = lambda a, b: l0_w_in[:, a:b]
    w_main = jnp.concatenate([cols(0, sp[0]), cols(sp[4], sp[5]), cols(sp[5], sp[6]), cols(sp[0], sp[1]),
                              cols(sp[2], sp[3]), cols(sp[3], sp[4])], axis=1).astype(BF16)
    w_small = jnp.concatenate([cols(sp[1], sp[2]), cols(sp[6], sp[7]),
                               jnp.zeros((D, LANES - 2 * SSD_HEADS - 2 * GLA_RANK), F32)], axis=1)
    proj, small, small_t = norm_proj(lay, x, norm1[0], mods[0][1], mods[0][0], w_main, _hilo(w_small), 512,
                                      PJ_W // 2, BF16, SSD_CHUNK)
    y_n, o_n, (ssd_f, ssd_b, gla_f, gla_b) = l0_mixers(
        lay, proj, small, small_t, state_l0_ssd_fwd, state_l0_ssd_bwd, state_l0_gla_fwd, state_l0_gla_bwd,
        l0_conv_w, l0_conv_b, l0_a_log, l0_dt_bias, l0_d_skip, l0_ssd_norm,
        l0_gate_w2, l0_gate_b, l0_gla_norm)
    x = proj_residual(lay, [y_n, o_n], l0_w_out.astype(BF16), x, mods[0][2])
    x = moe(0, x)

    qkv = qkv_proj(lay, x, norm1[1], mods[1][1], mods[1][0], l1_w_qkv.astype(BF16), l1_q_norm, l1_k_norm)
    o_ctx = attn_context(lay, qkv, l1_sinks)
    o_lat = attn_latent(lay, qkv, cache_l1_k, cache_l1_v, l1_sinks)
    x = proj_residual(lay, [(o_ctx, o_lat)], l1_w_out.astype(BF16), x, mods[1][2])
    xp, xs = moe(1, x, split=True)

    qw = ATT_HEADS * ATT_HD
    kw = ATT_KV * ATT_HD
    return (xp.reshape(x_prompt.shape), xs.reshape(x_sample.shape),
            ssd_f[:np_].reshape(np_, SSD_HEADS, SSD_HEAD_DIM, SSD_STATE),
            ssd_b[:np_].reshape(np_, SSD_HEADS, SSD_HEAD_DIM, SSD_STATE),
            gla_f[:np_], gla_b[:np_],
            qkv[:lay.p_tok, qw:qw + kw].reshape(np_, lay.prompt_len, ATT_KV, ATT_HD),
            qkv[:lay.p_tok, qw + kw:].reshape(np_, lay.prompt_len, ATT_KV, ATT_HD))


def kernel(x_prompt, x_sample, state_l0_ssd_fwd, state_l0_ssd_bwd, state_l0_gla_fwd, state_l0_gla_bwd, cache_l1_k, cache_l1_v, c, c_ctx, ada_w, ada_b, norm1, norm2, l0_w_in, l0_conv_w, l0_conv_b, l0_a_log, l0_dt_bias, l0_d_skip, l0_ssd_norm, l0_gate_w2, l0_gate_b, l0_gla_norm, l0_w_out, l1_w_qkv, l1_q_norm, l1_k_norm, l1_sinks, l1_w_out, router_w, router_b, exp_w_gate, exp_b_gate, exp_w_up, exp_b_up, exp_w_down, exp_b_down):
    lay = Layout(x_prompt.shape[0], x_prompt.shape[1], x_sample.shape[0], x_sample.shape[1])
    return _forward(lay, x_prompt, x_sample, state_l0_ssd_fwd, state_l0_ssd_bwd, state_l0_gla_fwd,
                    state_l0_gla_bwd, cache_l1_k, cache_l1_v, c, c_ctx, ada_w, ada_b, norm1, norm2,
                    l0_w_in, l0_conv_w, l0_conv_b, l0_a_log, l0_dt_bias, l0_d_skip, l0_ssd_norm,
                    l0_gate_w2, l0_gate_b, l0_gla_norm, l0_w_out,
                    l1_w_qkv, l1_q_norm, l1_k_norm, l1_sinks, l1_w_out,
                    router_w, router_b, exp_w_gate, exp_b_gate, exp_w_up, exp_b_up, exp_w_down, exp_b_down)
```

```python
import functools
import math

import numpy as np
import jax
import jax.numpy as jnp
from jax import lax
from jax.experimental import pallas as pl
from jax.experimental.pallas import tpu as pltpu

F32 = jnp.float32
BF16 = jnp.bfloat16
I32 = jnp.int32
HI = lax.Precision.HIGHEST

D = 1024
EPS = 1e-6
N_ADA = 6
SSD_HEADS = 16
SSD_HEAD_DIM = 64
SSD_INNER = 1024
SSD_STATE = 128
SSD_GROUPS = 2
SSD_CONV = 5
SSD_CHUNK = 128
GLA_HEADS = 4
GLA_KEY_DIM = 128
GLA_VAL_DIM = 256
GLA_RANK = 16
GLA_TAU = 16.0
GLA_BLOCK = 64
ATT_HEADS = 16
ATT_KV = 4
ATT_HD = 64
ATT_BLOCK = 128
WINDOW = 128
GRID_W = 64
ROPE_THETA = 10000.0
N_EXPERTS = 32
TOP_K = 4
EXPERT_FF = 1024
SWIGLU_LIMIT = 7.0
SWIGLU_ALPHA = 1.702
MOE_BLOCK = 256
TOK_TILE = 256
LANES = 128
NEG = -1e30

PJ_Z, PJ_V, PJ_OG, PJ_XBC, PJ_Q, PJ_K = 0, 1024, 2048, 3072, 4608, 5120
PJ_W = 5632
VMEM_LIMIT = 48 * 1024 * 1024


def _cp(sem, vmem=VMEM_LIMIT):
    return pltpu.CompilerParams(dimension_semantics=sem, vmem_limit_bytes=vmem)


class Layout:
    def __init__(self, n_prompt, prompt_len, n_sample, sample_len):
        self.n_prompt, self.prompt_len = n_prompt, prompt_len
        self.n_sample, self.sample_len = n_sample, sample_len
        self.p_tok = n_prompt * prompt_len
        self.n_tok = self.p_tok + n_sample * sample_len
        self.seqs = [(i * prompt_len, prompt_len) for i in range(n_prompt)]
        self.seqs += [(self.p_tok + i * sample_len, sample_len) for i in range(n_sample)]
        self.n_seq = len(self.seqs)

    def mod_row(self, start):
        return jnp.where(start < self.p_tok, 0, 1 + (start - self.p_tok) // self.sample_len)

def _sigmoid(x):
    return 1.0 / (1.0 + jnp.exp(-x))


def _silu(x):
    return x * _sigmoid(x)


def _softplus(x):
    return jnp.maximum(x, 0.0) + jnp.log(1.0 + jnp.exp(-jnp.abs(x)))


def _modnorm(x, g, sc, sh):
    ms = jnp.mean(x * x, axis=-1, keepdims=True)
    return (x * lax.rsqrt(ms + EPS) * g) * (1.0 + sc) + sh


def _dot(a, b, **kw):
    return jnp.dot(a, b, preferred_element_type=F32, **kw)


def _dot_nt(a, b):
    return lax.dot_general(a, b, (((1,), (1,)), ((), ())), preferred_element_type=F32)


def _dot_tn(a, b):
    return lax.dot_general(a, b, (((0,), (0,)), ((), ())), preferred_element_type=F32)


def _split(x, n):
    parts = []
    for _ in range(n):
        p = x.astype(BF16)
        parts.append(p)
        x = x - p.astype(F32)
    return parts


def _dot_sel(sel, x):
    sel = sel.astype(BF16)
    return sum(_dot(sel, p) for p in _split(x, 3))


def _dot_sel_r(x, sel):
    sel = sel.astype(BF16)
    return sum(_dot(p, sel) for p in _split(x, 3))


def _dot_hilo(x, w_hi, w_lo):
    x_hi, x_lo = _split(x, 2)
    return _dot(x_hi, w_hi) + _dot(x_lo, w_hi) + _dot(x_hi, w_lo)


def _hilo(w):
    hi = w.astype(BF16)
    return jnp.stack([hi, (w - hi.astype(F32)).astype(BF16)])


U32 = jnp.uint32
ROW_WORDS = D // 2
_HI_MASK = 0xFFFF0000


def _pack_rows(x):
    lo = lax.bitcast_convert_type(x[:, :ROW_WORDS], U32) >> 16
    hi = lax.bitcast_convert_type(x[:, ROW_WORDS:], U32) & jnp.uint32(_HI_MASK)
    return lo | hi


def _unpack_rows(u):
    lo = lax.bitcast_convert_type(u << 16, F32)
    hi = lax.bitcast_convert_type(u & jnp.uint32(_HI_MASK), F32)
    return jnp.concatenate([lo, hi], axis=1)


def _ada_kernel(c_ref, w_ref, b_ref, o_ref):
    o_ref[0] = _dot(_silu(c_ref[...]), w_ref[0], precision=HI) + b_ref[0]


def ada_table(cond8, ada_w, ada_b):
    depth, _, n = ada_w.shape
    tn = 1536
    return pl.pallas_call(
        _ada_kernel, grid=(depth, n // tn),
        in_specs=[pl.BlockSpec((8, D), lambda l, j: (0, 0)),
                  pl.BlockSpec((1, D, tn), lambda l, j: (l, 0, j)),
                  pl.BlockSpec((1, 1, tn), lambda l, j: (l, 0, j))],
        out_specs=pl.BlockSpec((1, 8, tn), lambda l, j: (l, 0, j)),
        out_shape=jax.ShapeDtypeStruct((depth, 8, n), F32),
        compiler_params=_cp(("arbitrary", "arbitrary")), name="ada_table",
    )(cond8, ada_w, ada_b.reshape(depth, 1, n))


def _stream_specs(lay, stream, tm):
    if not isinstance(stream, (tuple, list)):
        return [stream], [pl.BlockSpec((tm, stream.shape[1]), lambda i, *_: (i, 0))]
    assert lay.p_tok % tm == 0
    npt = lay.p_tok // tm
    w = stream[0].shape[1]
    return list(stream), [pl.BlockSpec((tm, w), lambda i, *_: (jnp.minimum(i, npt - 1), 0)),
                          pl.BlockSpec((tm, w), lambda i, *_: (jnp.maximum(i - npt, 0), 0))]


def _stream_tile(refs, in_prompt):
    if len(refs) == 1:
        return refs[0][...]
    return jnp.where(in_prompt, refs[0][...], refs[1][...])


def _proj_kernel(*refs, nx, npt):
    x_refs = refs[:nx]
    g_ref, sc_ref, sh_ref, w_ref, ws_ref, o_ref, os_ref, ost_ref, h_scr = refs[nx:]

    @pl.when(pl.program_id(1) == 0)
    def _():
        x = _stream_tile(x_refs, pl.program_id(0) < npt)
        q = ost_ref.shape[2]
        half = x.shape[0] // 2
        for r0 in (0, half):
            rows = slice(r0, r0 + half)
            h = _modnorm(x[rows], g_ref[...], sc_ref[...], sh_ref[...])
            hb = h.astype(BF16)
            h_scr[rows, :] = hb
            o_ref[rows, :] = _dot(hb, w_ref[...]).astype(o_ref.dtype)
            small = _dot_hilo(h, ws_ref[0], ws_ref[1])
            os_ref[rows, :] = small
            for c in range(half // q):
                ost_ref[r0 // q + c] = small[q * c:q * (c + 1), :].T

    @pl.when(pl.program_id(1) > 0)
    def _():
        o_ref[...] = _dot(h_scr[...], w_ref[...]).astype(o_ref.dtype)


def norm_proj(lay, x, g, sc, sh, w, w_small, tm, tn, out_dtype, chunk):
    n_tok = lay.n_tok
    n = w.shape[1]
    ns = w_small.shape[-1]
    mrow = lambda i, j: (lay.mod_row(i * tm), 0, 0)
    xs, x_specs = _stream_specs(lay, x, tm)
    return pl.pallas_call(
        functools.partial(_proj_kernel, nx=len(xs), npt=lay.p_tok // tm), grid=(n_tok // tm, n // tn),
        in_specs=x_specs + [pl.BlockSpec((1, D), lambda i, j: (0, 0)),
                            pl.BlockSpec((None, 1, D), mrow),
                            pl.BlockSpec((None, 1, D), mrow),
                            pl.BlockSpec((D, tn), lambda i, j: (0, j)),
                            pl.BlockSpec((2, D, ns), lambda i, j: (0, 0, 0))],
        out_specs=[pl.BlockSpec((tm, tn), lambda i, j: (i, j)),
                   pl.BlockSpec((tm, ns), lambda i, j: (i, 0)),
                   pl.BlockSpec((tm // chunk, ns, chunk), lambda i, j: (i, 0, 0))],
        out_shape=[jax.ShapeDtypeStruct((n_tok, n), out_dtype),
                   jax.ShapeDtypeStruct((n_tok, ns), F32),
                   jax.ShapeDtypeStruct((n_tok // chunk, ns, chunk), F32)],
        scratch_shapes=[pltpu.VMEM((tm, D), BF16)],
        compiler_params=_cp(("arbitrary", "arbitrary")), name="norm_proj",
    )(*xs, g.reshape(1, D), sc, sh, w, w_small)


CONV_HALO = 16


def _ssd_load(d, c, xc, dtg_ref, dtgT_ref, S):
    q = SSD_CHUNK
    nh = SSD_HEADS
    rows = pl.ds(pl.multiple_of(c * q, q), q)
    return (rows, xc[rows, 0:SSD_INNER], xc[rows, SSD_INNER:SSD_INNER + 2 * SSD_GROUPS * SSD_STATE],
            dtg_ref[rows, nh * d:nh * d + nh], dtgT_ref[c, nh * d:nh * d + nh, :], S[...])


def _ssd_chunks(loaded, alog_ref, alogT_ref, dtb_ref, dtbT_ref):
    q = SSD_CHUNK
    nh = SSD_HEADS
    hp = nh * SSD_HEAD_DIM
    row = lax.broadcasted_iota(I32, (q, q), 0)
    col = lax.broadcasted_iota(I32, (q, q), 1)
    lo = lax.broadcasted_iota(I32, (q, LANES), 1) < SSD_HEAD_DIM
    head_of = lax.broadcasted_iota(I32, (nh, hp), 1) // SSD_HEAD_DIM
    spread = (lax.broadcasted_iota(I32, (nh, hp), 0) == head_of).astype(BF16)
    rep = (nh // SSD_GROUPS) // 2
    pairs = [(g, j) for g in range(SSD_GROUPS) for j in range(rep * g, rep * (g + 1))]
    lane = lambda j: slice(LANES * j, LANES * (j + 1))

    def per_lane(arr, passes):
        return sum(_dot(p, spread) for p in _split(arr, passes))

    stage1 = []
    for d, (_, _, _, dtg, dtgT, _) in enumerate(loaded):
        dt = _softplus(dtg + dtb_ref[d:d + 1, :])
        dtT = _softplus(dtgT + dtbT_ref[:, d:d + 1])
        ad = dt * (-jnp.exp(alog_ref[d:d + 1, :]))
        adT = dtT * (-jnp.exp(alogT_ref[:, d:d + 1]))
        if d == 0:
            e, eT = _dot_sel(col <= row, ad), _dot_sel_r(adT, row <= col)
        else:
            e, eT = _dot_sel(col < row, ad), _dot_sel_r(adT, row < col)
        stage1.append((dt, ad, e, eT))
    stage2 = []
    for d, ((_, xs, bc, _, _, s), (dt, ad, e, eT)) in enumerate(zip(loaded, stage1)):
        if d == 0:
            tot = e[q - 1:q, :]
            fq, fk = jnp.exp(e), jnp.exp(tot - e)
        else:
            tot = jnp.sum(ad, axis=0, keepdims=True)
            fq, fk = jnp.exp(tot - e), jnp.exp(e)
        dt_x, fq_x, fk_x = per_lane(dt, 1), per_lane(fq, 1), per_lane(fk, 1)
        dec_x = per_lane(jnp.broadcast_to(jnp.exp(tot), (8, nh)), 3)[0:1, :]
        xdt_all = xs.astype(F32) * dt_x
        xk_all = (xdt_all * fk_x).astype(BF16)
        bg_t = [bc[:, SSD_STATE * g:SSD_STATE * (g + 1)].T for g in range(SSD_GROUPS)]
        cg = [bc[:, SSD_STATE * (SSD_GROUPS + g):SSD_STATE * (SSD_GROUPS + g + 1)] for g in range(SSD_GROUPS)]
        gmat = [_dot(cg[g], bg_t[g]) for g in range(SSD_GROUPS)]
        carried = [_dot(cg[g], s[:, lane(j)].astype(BF16)) * fq_x[:, lane(j)] for g, j in pairs]
        s_new = [s[:, lane(j)] * dec_x[:, lane(j)] + _dot(bg_t[g], xk_all[:, lane(j)]) for g, j in pairs]
        stage2.append((xdt_all, gmat, carried, s_new))
    results = []
    for d, ((_, _, e, eT), (xdt_all, gmat, carried, s_new)) in enumerate(zip(stage1, stage2)):
        mask = row >= col if d == 0 else col >= row
        ys = []
        for (g, j), off in zip(pairs, carried):
            parts = []
            for hh in (2 * j, 2 * j + 1):
                diff = e[:, hh:hh + 1] - eT[hh:hh + 1, :] if d == 0 else eT[hh:hh + 1, :] - e[:, hh:hh + 1]
                parts.append((gmat[g] * jnp.exp(jnp.where(mask, diff, NEG))).astype(BF16))
            lhs = jnp.concatenate(parts, axis=1)
            xdt = xdt_all[:, lane(j)]
            rhs = jnp.concatenate([jnp.where(lo, xdt, 0.0), jnp.where(lo, 0.0, xdt)], axis=0)
            ys.append(_dot(lhs, rhs.astype(BF16)) + off)
        results.append((jnp.concatenate(ys, axis=1), jnp.concatenate(s_new, axis=1)))
    return results


def _ssd_seq_kernel(*refs, has_init):
    (xbc_ref, z_ref, dtg_ref, dtgT_ref, cw_ref, cb_ref,
     alog_ref, alogT_ref, dtb_ref, dtbT_ref, dskip_ref, nrm_ref) = refs[:12]
    refs = refs[12:]
    if has_init:
        s0f_ref, s0b_ref = refs[:2]
        refs = refs[2:]
    y_ref, sf_ref, sb_ref, xc, Sf, Sb, yf, yb = refs
    ln = xbc_ref.shape[0]
    q = SSD_CHUNK
    nc = ln // q
    h = CONV_HALO
    pad = SSD_CONV // 2

    def conv_body(c, carry):
        r0 = pl.multiple_of(c * q, q)
        prev = xbc_ref[pl.ds(pl.multiple_of(jnp.maximum(r0 - h, 0), h), h), :]
        nxt = xbc_ref[pl.ds(pl.multiple_of(jnp.minimum(r0 + q, ln - h), h), h), :]
        ext = jnp.concatenate([jnp.where(c > 0, prev, jnp.zeros_like(prev)), xbc_ref[pl.ds(r0, q), :],
                               jnp.where(c < nc - 1, nxt, jnp.zeros_like(nxt))], axis=0)
        src = lax.broadcasted_iota(I32, (q, q + 2 * h), 1) - lax.broadcasted_iota(I32, (q, q + 2 * h), 0)
        taps = [_dot((src == h - pad + k).astype(BF16), ext) for k in range(SSD_CONV)]
        acc = jnp.broadcast_to(cb_ref[...], (q, cb_ref.shape[1]))
        for k in range(SSD_CONV):
            acc = acc + cw_ref[k:k + 1, :] * taps[k]
        xc[pl.ds(r0, q), :] = _silu(acc).astype(xc.dtype)
        return carry
    lax.fori_loop(0, nc, conv_body, 0)

    if has_init:
        Sf[...] = s0f_ref[0].T
        Sb[...] = s0b_ref[0].T
    else:
        Sf[...] = jnp.zeros_like(Sf)
        Sb[...] = jnp.zeros_like(Sb)
    params = (alog_ref, alogT_ref, dtb_ref, dtbT_ref)

    def scan_body(c, carry):
        lf = _ssd_load(0, c, xc, dtg_ref, dtgT_ref, Sf)
        lb = _ssd_load(1, nc - 1 - c, xc, dtg_ref, dtgT_ref, Sb)
        (y_f, s_f), (y_b, s_b) = _ssd_chunks((lf, lb), *params)
        yf[lf[0], :] = y_f
        yb[lb[0], :] = y_b
        Sf[...] = s_f
        Sb[...] = s_b
        return carry
    lax.fori_loop(0, nc, scan_body, 0, unroll=2)
    sf_ref[0] = Sf[...].T
    sb_ref[0] = Sb[...].T

    def out_body(c, carry):
        rows = pl.ds(pl.multiple_of(c * q, q), q)
        ytot = yf[rows, :] + yb[rows, :] + dskip_ref[...] * xc[rows, 0:SSD_INNER].astype(F32)
        yg = ytot * _silu(z_ref[rows, :].astype(F32))
        ms = jnp.mean(yg * yg, axis=-1, keepdims=True)
        y_ref[rows, :] = (yg * lax.rsqrt(ms + EPS) * nrm_ref[...]).astype(y_ref.dtype)
        return carry
    lax.fori_loop(0, nc, out_body, 0)


def _ssd_call(n_seq, ln, blk0, proj, small, smallT3, init, params):
    q = SSD_CHUNK
    hp = SSD_HEADS * SSD_HEAD_DIM
    cw = SSD_INNER + 2 * SSD_GROUPS * SSD_STATE
    nc = ln // q
    assert PJ_XBC % cw == 0 and PJ_Z % SSD_INNER == 0
    tok = lambda w, cb: pl.BlockSpec((ln, w), lambda b: (blk0 + b, cb))
    seq3 = pl.BlockSpec((1, hp, SSD_STATE), lambda b: (b, 0, 0))
    full = lambda a: pl.BlockSpec(a.shape, lambda b: (0,) * a.ndim)
    init = () if init is None else tuple(init)
    return pl.pallas_call(
        functools.partial(_ssd_seq_kernel, has_init=bool(init)), grid=(n_seq,),
        in_specs=[tok(cw, PJ_XBC // cw), tok(SSD_INNER, PJ_Z // SSD_INNER), tok(LANES, 0),
                  pl.BlockSpec((nc, 2 * SSD_HEADS, q), lambda b: (blk0 + b, 0, 0))]
        + [full(a) for a in params] + [seq3] * len(init),
        out_specs=[pl.BlockSpec((ln, hp), lambda b: (b, 0)), seq3, seq3],
        out_shape=[jax.ShapeDtypeStruct((n_seq * ln, hp), BF16),
                   jax.ShapeDtypeStruct((n_seq, hp, SSD_STATE), F32),
                   jax.ShapeDtypeStruct((n_seq, hp, SSD_STATE), F32)],
        scratch_shapes=[pltpu.VMEM((ln, cw), BF16),
                        pltpu.VMEM((SSD_STATE, hp), F32), pltpu.VMEM((SSD_STATE, hp), F32),
                        pltpu.VMEM((ln, hp), F32), pltpu.VMEM((ln, hp), F32)],
        compiler_params=_cp(("arbitrary",)), name="ssd_seq",
    )(proj, proj, small, smallT3, *params, *init)


def _gla_gates(c, q_ref, k_ref, glr_ref, w2_ref, gb_ref, qi, ki, qo, kk, dec):
    t = GLA_BLOCK
    rows = pl.ds(pl.multiple_of(c * t, t), t)
    c0 = 2 * SSD_HEADS
    gps = [_dot_hilo(glr_ref[rows, c0 + GLA_RANK * d:c0 + GLA_RANK * (d + 1)], w2_ref[0, d], w2_ref[1, d])
           + gb_ref[d:d + 1, :] for d in (0, 1)]
    las = [-_softplus(-gp) * (1.0 / GLA_TAU) for gp in gps]
    row = lax.broadcasted_iota(I32, (t, t), 0)
    col = lax.broadcasted_iota(I32, (t, t), 1)
    es = [_dot_sel(col <= row, las[0]), _dot_sel(col < row, las[1])]
    mid = t // 2 - 1
    qf = q_ref[rows, :].astype(F32) * (GLA_KEY_DIM ** -0.5)
    kf = k_ref[rows, :].astype(F32)
    for d in (0, 1):
        e = es[d]
        r = e[mid:mid + 1, :]
        if d == 0:
            tot = e[t - 1:t, :]
            fqi, fki = jnp.exp(e - r), jnp.exp(r - e)
            fq, fk = jnp.exp(e), jnp.exp(tot - e)
        else:
            tot = e[t - 1:t, :] + las[1][t - 1:t, :]
            fqi, fki = jnp.exp(r - e), jnp.exp(e - r)
            fq, fk = jnp.exp(tot - e), jnp.exp(e)
        qi[d, rows, :] = (qf * fqi).astype(BF16)
        ki[d, rows, :] = (kf * fki).astype(BF16)
        qo[d, rows, :] = (qf * fq).astype(BF16)
        kk[d, rows, :] = (kf * fk).astype(BF16)
        dec[d, c] = jnp.broadcast_to(jnp.exp(tot), (8, tot.shape[1]))


def _gla_load(d, c, v_ref, qi, ki, qo, kk, dec, S):
    t = GLA_BLOCK
    rows = pl.ds(pl.multiple_of(c * t, t), t)
    return (rows, qi[d, rows, :], ki[d, rows, :], qo[d, rows, :], kk[d, rows, :], dec[d, c][0:1, :],
            v_ref[rows, :], [S[h] for h in range(GLA_HEADS)])


def _gla_blocks(loaded):
    t = GLA_BLOCK
    dk, dv = GLA_KEY_DIM, GLA_VAL_DIM
    row = lax.broadcasted_iota(I32, (t, t), 0)
    col = lax.broadcasted_iota(I32, (t, t), 1)
    heads = range(GLA_HEADS)
    ks = [slice(dk * h, dk * (h + 1)) for h in heads]
    first = []
    for _, q_in, k_in, q_st, k_st, dec, v, states in loaded:
        vs = [v[:, dv * h:dv * (h + 1)] for h in heads]
        scores = [_dot_nt(q_in[:, ks[h]], k_in[:, ks[h]]) for h in heads]
        carried = [_dot_nt(q_st[:, ks[h]], states[h].astype(BF16)) for h in heads]
        grown = [_dot_tn(vs[h], k_st[:, ks[h]]) for h in heads]
        first.append((vs, scores, carried, grown))
    results = []
    for d, ((_, _, _, _, _, dec, _, states), (vs, scores, carried, grown)) in enumerate(zip(loaded, first)):
        mask = row >= col if d == 0 else col >= row
        outs = [_dot(jnp.where(mask, scores[h], 0.0).astype(BF16), vs[h]) + carried[h] for h in heads]
        new_states = [states[h] * dec[:, ks[h]] + grown[h] for h in heads]
        results.append((jnp.concatenate(outs, axis=1), new_states))
    return results


def _gla_seq_kernel(*refs, has_init):
    q_ref, k_ref, v_ref, og_ref, glr_ref, w2_ref, gb_ref, nrm_ref = refs[:8]
    refs = refs[8:]
    if has_init:
        s0f_ref, s0b_ref = refs[:2]
        refs = refs[2:]
    o_ref, sf_ref, sb_ref, Sf, Sb, of, ob, qi, ki, qo, kk, dec = refs
    ln = q_ref.shape[0]
    t = GLA_BLOCK
    nc = ln // t
    dv = GLA_VAL_DIM
    for h in range(GLA_HEADS):
        if has_init:
            Sf[h] = s0f_ref[0, h].T
            Sb[h] = s0b_ref[0, h].T
        else:
            Sf[h] = jnp.zeros(Sf.shape[1:], F32)
            Sb[h] = jnp.zeros(Sb.shape[1:], F32)
    staged = (qi, ki, qo, kk, dec)

    def gate_body(c, carry):
        _gla_gates(c, q_ref, k_ref, glr_ref, w2_ref, gb_ref, *staged)
        return carry
    lax.fori_loop(0, nc, gate_body, 0, unroll=2)

    def scan_body(c, carry):
        lf = _gla_load(0, c, v_ref, *staged, Sf)
        lb = _gla_load(1, nc - 1 - c, v_ref, *staged, Sb)
        (o_f, s_f), (o_b, s_b) = _gla_blocks((lf, lb))
        of[lf[0], :] = o_f
        ob[lb[0], :] = o_b
        for h in range(GLA_HEADS):
            Sf[h] = s_f[h]
            Sb[h] = s_b[h]
        return carry
    lax.fori_loop(0, nc, scan_body, 0, unroll=2)
    for h in range(GLA_HEADS):
        sf_ref[0, h] = Sf[h].T
        sb_ref[0, h] = Sb[h].T

    def out_body(c, carry):
        rows = pl.ds(pl.multiple_of(c * t, t), t)
        for h in range(GLA_HEADS):
            vl = slice(dv * h, dv * (h + 1))
            ot = of[rows, vl] + ob[rows, vl]
            ms = jnp.mean(ot * ot, axis=-1, keepdims=True)
            on = ot * lax.rsqrt(ms + EPS) * nrm_ref[...]
            o_ref[rows, vl] = (on * _silu(og_ref[rows, vl].astype(F32))).astype(o_ref.dtype)
        return carry
    lax.fori_loop(0, nc, out_body, 0)


def _gla_call(n_seq, ln, blk0, proj, small, init, params):
    qk_w = GLA_HEADS * GLA_KEY_DIM
    v_w = GLA_HEADS * GLA_VAL_DIM
    tok = lambda w, cb: pl.BlockSpec((ln, w), lambda b: (blk0 + b, cb))
    seq4 = pl.BlockSpec((1, GLA_HEADS, GLA_KEY_DIM, GLA_VAL_DIM), lambda b: (b, 0, 0, 0))
    full = lambda a: pl.BlockSpec(a.shape, lambda b: (0,) * a.ndim)
    st_shape = jax.ShapeDtypeStruct((n_seq, GLA_HEADS, GLA_KEY_DIM, GLA_VAL_DIM), F32)
    init = () if init is None else tuple(init)
    return pl.pallas_call(
        functools.partial(_gla_seq_kernel, has_init=bool(init)), grid=(n_seq,),
        in_specs=[tok(qk_w, PJ_Q // qk_w), tok(qk_w, PJ_K // qk_w), tok(v_w, PJ_V // v_w),
                  tok(v_w, PJ_OG // v_w), tok(LANES, 0)] + [full(a) for a in params] + [seq4] * len(init),
        out_specs=[pl.BlockSpec((ln, v_w), lambda b: (b, 0)), seq4, seq4],
        out_shape=[jax.ShapeDtypeStruct((n_seq * ln, v_w), BF16), st_shape, st_shape],
        scratch_shapes=[pltpu.VMEM((GLA_HEADS, GLA_VAL_DIM, GLA_KEY_DIM), F32),
                        pltpu.VMEM((GLA_HEADS, GLA_VAL_DIM, GLA_KEY_DIM), F32),
                        pltpu.VMEM((ln, v_w), F32), pltpu.VMEM((ln, v_w), F32)]
        + [pltpu.VMEM((2, ln, qk_w), BF16)] * 4 + [pltpu.VMEM((2, ln // GLA_BLOCK, 8, qk_w), F32)],
        compiler_params=_cp(("arbitrary",)), name="gla_seq",
    )(proj, proj, proj, proj, small, *params, *init)


def l0_mixers(lay, proj, small, small_t, ssd_f0, ssd_b0, gla_f0, gla_b0, conv_w, conv_b, a_log, dt_bias, d_skip,
              ssd_norm, gate_w2, gate_b, gla_norm):
    hp = SSD_HEADS * SSD_HEAD_DIM
    ssd_p = (conv_w, conv_b.reshape(1, -1), a_log, a_log.T, dt_bias, dt_bias.T,
             jnp.repeat(d_skip, SSD_HEAD_DIM).reshape(1, hp), ssd_norm.reshape(1, hp))
    gla_p = (_hilo(gate_w2), gate_b, gla_norm.reshape(1, -1))
    np_, ns = lay.n_prompt, lay.n_sample
    assert lay.p_tok % lay.sample_len == 0
    groups = [(np_, lay.prompt_len, 0, None, None),
              (ns, lay.sample_len, lay.p_tok // lay.sample_len,
               (ssd_f0.reshape(ns, hp, SSD_STATE), ssd_b0.reshape(ns, hp, SSD_STATE)), (gla_f0, gla_b0))]
    ys, os_, states = [], [], None
    for n, ln, blk0, ssd_init, gla_init in groups:
        y, sf, sb = _ssd_call(n, ln, blk0, proj, small, small_t, ssd_init, ssd_p)
        o, gf, gb = _gla_call(n, ln, blk0, proj, small, gla_init, gla_p)
        ys.append(y)
        os_.append(o)
        if states is None:
            states = (sf, sb, gf, gb)
    return tuple(ys), tuple(os_), states


def _res_kernel(*refs, counts, ks, npt):
    in_prompt = pl.program_id(0) < npt
    streams, pos = [], 0
    for c in counts:
        streams.append(refs[pos:pos + c])
        pos += c
    w_ref, gate_ref, o_ref = refs[pos:]
    acc = None
    off = 0
    for a_refs, k in zip(streams[:-1], ks):
        part = _dot(_stream_tile(a_refs, in_prompt), w_ref[off:off + k, :])
        acc = part if acc is None else acc + part
        off += k
    o_ref[...] = _stream_tile(streams[-1], in_prompt) + gate_ref[...] * acc


def proj_residual(lay, acts, w, x, gate, tm=512):
    arrays, specs, counts = [], [], []
    for s in list(acts) + [x]:
        a, sp = _stream_specs(lay, s, tm)
        arrays += a
        specs += sp
        counts.append(len(a))
    ks = tuple(int((a[0] if isinstance(a, (tuple, list)) else a).shape[1]) for a in acts)
    mrow = lambda i: (lay.mod_row(i * tm), 0, 0)
    return pl.pallas_call(
        functools.partial(_res_kernel, counts=tuple(counts), ks=ks, npt=lay.p_tok // tm),
        grid=(lay.n_tok // tm,),
        in_specs=specs + [pl.BlockSpec(w.shape, lambda i: (0, 0)), pl.BlockSpec((None, 1, D), mrow)],
        out_specs=pl.BlockSpec((tm, D), lambda i: (i, 0)),
        out_shape=jax.ShapeDtypeStruct((lay.n_tok, D), F32),
        compiler_params=_cp(("arbitrary",)), name="proj_residual",
    )(*arrays, w, gate)


def _router_kernel(x_ref, g_ref, sc_ref, sh_ref, rw_ref, rb_ref,
                   h_ref, idx_ref, gate_ref, pos_ref, posT_ref, cnt_ref):
    tm = x_ref.shape[0]
    h = _modnorm(x_ref[...], g_ref[...], sc_ref[...], sh_ref[...])
    h_hi = h.astype(BF16)
    h_ref[...] = h_hi
    h_lo = (h - h_hi.astype(F32)).astype(BF16)
    lg = (_dot(h_hi, rw_ref[0]) + _dot(h_lo, rw_ref[0]) + _dot(h_hi, rw_ref[1])
          + rb_ref[...])
    lane = lax.broadcasted_iota(I32, (tm, LANES), 1).astype(F32)
    vals, ids = [], []
    for _ in range(TOP_K):
        m = jnp.max(lg, axis=1, keepdims=True)
        i = jnp.min(jnp.where(lg == m, lane, float(LANES)), axis=1, keepdims=True)
        vals.append(m)
        ids.append(i)
        lg = jnp.where(lane == i, -jnp.inf, lg)
    ex = [jnp.exp(v - vals[0]) for v in vals]
    den = ex[0] + ex[1] + ex[2] + ex[3]
    sel = jnp.zeros((tm, LANES), F32)
    for i in ids:
        sel = sel + (lane == i).astype(F32)
    row = lax.broadcasted_iota(I32, (tm, tm), 0)
    col = lax.broadcasted_iota(I32, (tm, tm), 1)
    before = _dot((col < row).astype(BF16), sel.astype(BF16))
    n = jnp.sum(sel, axis=0, keepdims=True)
    er = lax.broadcasted_iota(I32, (LANES, LANES), 0)
    ec = lax.broadcasted_iota(I32, (LANES, LANES), 1)
    n_al = jnp.ceil(n * (1.0 / SEG_ALIGN)) * SEG_ALIGN
    offs = _dot(jnp.broadcast_to(n_al, (8, LANES)).astype(BF16), (er < ec).astype(BF16))[0:1, :]
    slot = before + offs
    idx_o = jnp.zeros((tm, LANES), F32)
    gate_o = jnp.zeros((tm, LANES), F32)
    pos_o = jnp.zeros((tm, LANES), F32)
    for k in range(TOP_K):
        p = jnp.sum(jnp.where(lane == ids[k], slot, 0.0), axis=1, keepdims=True)
        idx_o = jnp.where(lane == k, ids[k], idx_o)
        gate_o = jnp.where(lane == k, ex[k] / den, gate_o)
        pos_o = jnp.where(lane == k, p, pos_o)
    idx_ref[...] = idx_o.astype(I32)
    gate_ref[...] = gate_o
    pos_ref[...] = pos_o.astype(I32)
    posT_ref[...] = pos_o.T[0:8, :]
    cnt_ref[0] = jnp.broadcast_to(n, (8, LANES))


def moe_router(lay, x, g, sc, sh, rw, rb):
    n_tok = x.shape[0]
    tm = TOK_TILE
    nt = n_tok // tm
    mrow = lambda i: (lay.mod_row(i * tm), 0, 0)
    tile = lambda w, dt: (pl.BlockSpec((tm, w), lambda i: (i, 0)), jax.ShapeDtypeStruct((n_tok, w), dt))
    outs = [tile(D, BF16), tile(LANES, I32), tile(LANES, F32), tile(LANES, I32),
            (pl.BlockSpec((8, tm), lambda i: (0, i)), jax.ShapeDtypeStruct((8, n_tok), F32)),
            (pl.BlockSpec((1, 8, LANES), lambda i: (i, 0, 0)), jax.ShapeDtypeStruct((nt, 8, LANES), F32))]
    return pl.pallas_call(
        _router_kernel, grid=(nt,),
        in_specs=[pl.BlockSpec((tm, D), lambda i: (i, 0)),
                  pl.BlockSpec((1, D), lambda i: (0, 0)),
                  pl.BlockSpec((None, 1, D), mrow), pl.BlockSpec((None, 1, D), mrow),
                  pl.BlockSpec((2, D, LANES), lambda i: (0, 0, 0)),
                  pl.BlockSpec((1, LANES), lambda i: (0, 0))],
        out_specs=[o[0] for o in outs], out_shape=[o[1] for o in outs],
        compiler_params=_cp(("arbitrary",)), name="moe_router",
    )(x, g.reshape(1, D), sc, sh, rw, rb)


SEG_ALIGN = 8
SEG_CHUNK = 16
REST_BITS = tuple(range(int(math.log2(SEG_CHUNK)) - 1, int(math.log2(SEG_ALIGN)) - 1, -1))
TILE_ROWS = TOK_TILE * TOP_K + N_EXPERTS * SEG_ALIGN


def _pow2_copies(n, src, dst, make_copy, op, bits):
    for b in bits:
        sz = 1 << b
        done = (n >> (b + 1)) << (b + 1)

        @pl.when((n & sz) != 0)
        def _():
            op(make_copy(pl.multiple_of(src + done, SEG_ALIGN), pl.multiple_of(dst + done, SEG_ALIGN), sz))


def _start_segments(i, n_ref, off_ref, dst_ref, make_copy):
    def body(e, carry):
        k = i * N_EXPERTS + e
        n, src, dst = n_ref[k], off_ref[k], dst_ref[k]

        def chunk(j, c):
            o = pl.multiple_of(j * SEG_CHUNK, SEG_CHUNK)
            make_copy(pl.multiple_of(src + o, SEG_ALIGN), pl.multiple_of(dst + o, SEG_ALIGN),
                      SEG_CHUNK).start(priority=1)
            return c
        shift = int(math.log2(SEG_CHUNK))
        full = n >> shift
        lax.fori_loop(0, full, chunk, 0)
        done = full << shift
        _pow2_copies(n - done, src + done, dst + done, make_copy, lambda c: c.start(), REST_BITS)
        return carry
    lax.fori_loop(0, N_EXPERTS, body, 0)


TAIL_BITS = tuple(range(int(math.log2(MOE_BLOCK)) - 1, int(math.log2(SEG_ALIGN)) - 1, -1))
TILE_BITS = tuple(range(int(math.log2(TILE_ROWS)), int(math.log2(SEG_ALIGN)) - 1, -1))


def _wait_rows(total, make_copy):
    _pow2_copies(total, 0, 0, make_copy, lambda c: c.wait(), TILE_BITS)


def _dispatch_kernel(n_ref, off_ref, dst_ref, tot_ref, tn_ref, td_ref, posT_ref, h_ref, xout_ref,
                     srt, zbuf, sems):
    i = pl.program_id(0)
    last = pl.num_programs(0) - 1
    slot = i % 2
    tm = h_ref.shape[0]
    r = lax.broadcasted_iota(I32, (TILE_ROWS, tm), 0)
    hit = jnp.zeros((TILE_ROWS, tm), jnp.bool_)
    for k in range(TOP_K):
        hit = hit | (r == posT_ref[k:k + 1, :].astype(I32))
    sel = jnp.where(hit, 1.0, 0.0).astype(BF16)
    srt[slot] = _pack_rows(_dot(sel, h_ref[...]))

    def copier(s):
        def make_copy(src, dst, sz):
            return pltpu.make_async_copy(srt.at[s, pl.ds(src, sz)], xout_ref.at[pl.ds(dst, sz)], sems.at[s])
        return make_copy

    _start_segments(i, n_ref, off_ref, dst_ref, copier(slot))

    @pl.when(i > 0)
    def _():
        _wait_rows(tot_ref[jnp.maximum(i - 1, 0)], copier(1 - slot))

    @pl.when(i == last)
    def _():
        _wait_rows(tot_ref[i], copier(slot))
        zbuf[...] = jnp.zeros_like(zbuf)
        sem = sems.at[0]

        def zero_copy(src, dst, sz):
            return pltpu.make_async_copy(zbuf.at[pl.ds(src, sz)], xout_ref.at[pl.ds(dst, sz)], sem)

        nb = xout_ref.shape[0] // MOE_BLOCK
        for op in (lambda c: c.start(), lambda c: c.wait()):
            def body(e, carry):
                _pow2_copies(tn_ref[e], 0, td_ref[e], zero_copy, op, TAIL_BITS)
                return carry
            lax.fori_loop(0, N_EXPERTS, body, 0)

            def unused(b, carry):
                op(zero_copy(0, pl.multiple_of(b * MOE_BLOCK, MOE_BLOCK), MOE_BLOCK))
                return carry
            lax.fori_loop(tn_ref[N_EXPERTS], nb, unused, 0)


def moe_dispatch(n_tab, off_tab, dst_tab, tot_tab, tail_n, tail_dst, posT, h2, n_rows):
    n_tok = h2.shape[0]
    tm = TOK_TILE
    grid_spec = pltpu.PrefetchScalarGridSpec(
        num_scalar_prefetch=6, grid=(n_tok // tm,),
        in_specs=[pl.BlockSpec((8, tm), lambda i, *_: (0, i)),
                  pl.BlockSpec((tm, D), lambda i, *_: (i, 0))],
        out_specs=pl.BlockSpec(memory_space=pl.ANY),
        scratch_shapes=[pltpu.VMEM((2, TILE_ROWS, ROW_WORDS), U32), pltpu.VMEM((MOE_BLOCK, ROW_WORDS), U32),
                        pltpu.SemaphoreType.DMA((2,))])
    return pl.pallas_call(
        _dispatch_kernel, grid_spec=grid_spec,
        out_shape=jax.ShapeDtypeStruct((n_rows, ROW_WORDS), U32),
        compiler_params=_cp(("arbitrary",)), name="moe_dispatch",
    )(n_tab, off_tab, dst_tab, tot_tab, tail_n, tail_dst, posT, h2)


def _combine_kernel(n_ref, off_ref, dst_ref, tot_ref, pos_ref, gate_ref, x_ref, g2_ref, y_ref, *rest, npt):
    o_refs, (buf, sems) = rest[:-2], rest[-2:]
    i = pl.program_id(0)
    last = pl.num_programs(0) - 1
    slot = i % 2
    tm = x_ref.shape[0]
    na = TILE_ROWS

    def copier(s):
        def make_copy(src, dst, sz):
            return pltpu.make_async_copy(y_ref.at[pl.ds(dst, sz)], buf.at[s, pl.ds(src, sz)], sems.at[s])
        return make_copy

    def fetch(tile, s):
        buf[s, tm * TOP_K:na, :] = jnp.zeros((na - tm * TOP_K, ROW_WORDS), U32)
        _start_segments(tile, n_ref, off_ref, dst_ref, copier(s))

    @pl.when(i == 0)
    def _():
        fetch(i, slot)

    @pl.when(i < last)
    def _():
        fetch(i + 1, 1 - slot)

    _wait_rows(tot_ref[i], copier(slot))
    kc = 256
    mixed = None
    for k0 in range(0, na, kc):
        lane = lax.broadcasted_iota(I32, (tm, kc), 1) + k0
        pw = jnp.zeros((tm, kc), F32)
        for k in range(TOP_K):
            pw = pw + jnp.where(lane == pos_ref[:, k:k + 1], gate_ref[:, k:k + 1], 0.0)
        phi = pw.astype(BF16)
        plo = (pw - phi.astype(F32)).astype(BF16)
        yb = _unpack_rows(buf[slot, k0:k0 + kc, :]).astype(BF16)
        part = _dot(phi, yb) + _dot(plo, yb)
        mixed = part if mixed is None else mixed + part
    res = x_ref[...] + g2_ref[...] * mixed
    if len(o_refs) == 1:
        o_refs[0][...] = res
    else:
        @pl.when(i < npt)
        def _():
            o_refs[0][...] = res

        @pl.when(i >= npt)
        def _():
            o_refs[1][...] = res


def moe_combine(lay, n_tab, off_tab, dst_tab, tot_tab, pos, gates, x, gate2, y_rows, split):
    n_tok = x.shape[0]
    tm = TOK_TILE
    npt = lay.p_tok // tm
    mrow = lambda i, *_: (lay.mod_row(i * tm), 0, 0)
    if split:
        out_specs = [pl.BlockSpec((tm, D), lambda i, *_: (jnp.minimum(i, npt - 1), 0)),
                     pl.BlockSpec((tm, D), lambda i, *_: (jnp.maximum(i - npt, 0), 0))]
        out_shape = [jax.ShapeDtypeStruct((lay.p_tok, D), F32), jax.ShapeDtypeStruct((n_tok - lay.p_tok, D), F32)]
    else:
        out_specs = pl.BlockSpec((tm, D), lambda i, *_: (i, 0))
        out_shape = jax.ShapeDtypeStruct((n_tok, D), F32)
    grid_spec = pltpu.PrefetchScalarGridSpec(
        num_scalar_prefetch=4, grid=(n_tok // tm,),
        in_specs=[pl.BlockSpec((tm, LANES), lambda i, *_: (i, 0)),
                  pl.BlockSpec((tm, LANES), lambda i, *_: (i, 0)),
                  pl.BlockSpec((tm, D), lambda i, *_: (i, 0)),
                  pl.BlockSpec((None, 1, D), mrow),
                  pl.BlockSpec(memory_space=pl.ANY)],
        out_specs=out_specs,
        scratch_shapes=[pltpu.VMEM((2, TILE_ROWS, ROW_WORDS), U32), pltpu.SemaphoreType.DMA((2,))])
    return pl.pallas_call(
        functools.partial(_combine_kernel, npt=npt), grid_spec=grid_spec, out_shape=out_shape,
        compiler_params=_cp(("arbitrary",)), name="moe_combine",
    )(n_tab, off_tab, dst_tab, tot_tab, pos, gates, x, gate2, y_rows)


def _expert_kernel(be_ref, nv_ref, nxt_ref, slot_ref, x_ref, b_ref, wg_hbm, wu_hbm, wd_hbm,
                   y_ref, wf, sems, *, layer):
    i = pl.program_id(0)
    valid = i < nv_ref[0]
    e = be_ref[i]
    slot = slot_ref[e]
    changed = jnp.logical_or(i == 0, e != be_ref[jnp.maximum(i - 1, 0)])

    def weight_copies(ex, s):
        return [pltpu.make_async_copy(w.at[layer, ex], wf.at[s, k], sems.at[s, k])
                for k, w in enumerate((wg_hbm, wu_hbm, wd_hbm))]

    @pl.when(jnp.logical_and(valid, changed))
    def _():
        @pl.when(i == 0)
        def _():
            for c in weight_copies(e, slot):
                c.start()

        nxt = nxt_ref[e]

        @pl.when(nxt >= 0)
        def _():
            for c in weight_copies(nxt, 1 - slot):
                c.start(priority=1)

        for c in weight_copies(e, slot):
            c.wait()

    @pl.when(valid)
    def _():
        x = _unpack_rows(x_ref[...])
        b = b_ref[e]
        gt = jnp.minimum(_dot(x, wf[slot, 0]) + b[0:1, :], SWIGLU_LIMIT)
        up = jnp.clip(_dot(x, wf[slot, 1]) + b[1:2, :], -SWIGLU_LIMIT, SWIGLU_LIMIT)
        act = (up + 1.0) * gt * _sigmoid(SWIGLU_ALPHA * gt)
        y = _dot(act, wf[slot, 2]) + b[2:3, :]
        y_ref[...] = _pack_rows(y.astype(BF16).astype(F32))

    @pl.when(jnp.logical_not(valid))
    def _():
        y_ref[...] = jnp.zeros_like(y_ref)


def moe_experts(layer, blk_expert, n_valid, next_expert, slot, x_rows, w_gate, b_gate, w_up, b_up, w_down,
                b_down):
    n_rows = x_rows.shape[0]
    nb = n_rows // MOE_BLOCK
    depth, ne, _, ff = w_gate.shape
    assert ff == D
    rowblk = lambda i, be, nv, *_: (jnp.maximum(jnp.minimum(i, nv[0] - 1), 0), 0)
    hbm = pl.BlockSpec(memory_space=pl.ANY)
    biases = jnp.stack([b_gate, b_up, b_down], axis=2)
    grid_spec = pltpu.PrefetchScalarGridSpec(
        num_scalar_prefetch=4, grid=(nb,),
        in_specs=[pl.BlockSpec((MOE_BLOCK, ROW_WORDS), rowblk),
                  pl.BlockSpec((None, ne, 3, D), lambda i, *_: (layer, 0, 0, 0)), hbm, hbm, hbm],
        out_specs=pl.BlockSpec((MOE_BLOCK, ROW_WORDS), lambda i, *_: (i, 0)),
        scratch_shapes=[pltpu.VMEM((2, 3, D, ff), F32), pltpu.SemaphoreType.DMA((2, 3))])
    return pl.pallas_call(
        functools.partial(_expert_kernel, layer=layer), grid_spec=grid_spec,
        out_shape=jax.ShapeDtypeStruct((n_rows, ROW_WORDS), U32),
        compiler_params=_cp(("arbitrary",)), name="moe_experts",
    )(blk_expert, n_valid, next_expert, slot, x_rows, biases, w_gate, w_up, w_down)


def moe_layer(lay, layer, x, g2, sc2, sh2, gate2, router_w, router_b, w_gate, b_gate, w_up, b_up, w_down,
              b_down, split=False):
    n_tok = x.shape[0]
    nt = n_tok // TOK_TILE
    rw = jnp.zeros((D, LANES), F32).at[:, :N_EXPERTS].set(router_w)
    rw = _hilo(rw)
    rb = jnp.full((1, LANES), NEG, F32).at[0, :N_EXPERTS].set(router_b)
    h2, _, gates, pos, posT, cnt = moe_router(lay, x, g2, sc2, sh2, rw, rb)
    n_te = cnt[:, 0, :N_EXPERTS].astype(I32)
    n_te = (n_te + SEG_ALIGN - 1) // SEG_ALIGN * SEG_ALIGN
    totals = jnp.sum(n_te, axis=0)
    padded = (totals + MOE_BLOCK - 1) // MOE_BLOCK * MOE_BLOCK
    padded_end = jnp.cumsum(padded)
    pstart = padded_end - padded
    dst = pstart[None, :] + jnp.cumsum(n_te, axis=0) - n_te
    off = jnp.cumsum(n_te, axis=1) - n_te
    n_rows = nt * TILE_ROWS + N_EXPERTS * MOE_BLOCK
    nb = n_rows // MOE_BLOCK
    n_valid = (padded_end[-1] // MOE_BLOCK).astype(I32).reshape(1)
    bstart = jnp.minimum(jnp.arange(nb, dtype=I32), n_valid[0] - 1) * MOE_BLOCK
    blk_expert = jnp.minimum(jnp.sum((bstart[:, None] >= padded_end[None, :]).astype(I32), axis=1),
                             N_EXPERTS - 1).astype(I32)
    tabs = (n_te.reshape(-1).astype(I32), off.reshape(-1).astype(I32), dst.reshape(-1).astype(I32),
            jnp.sum(n_te, axis=1).astype(I32))
    tail_n = jnp.concatenate([(padded - totals).astype(I32), n_valid])
    x_rows = moe_dispatch(*tabs, tail_n, (pstart + totals).astype(I32), posT, h2, n_rows)
    owner = jnp.where(padded > 0, jnp.arange(N_EXPERTS, dtype=I32), N_EXPERTS)
    later = jnp.concatenate([lax.cummin(owner, axis=0, reverse=True)[1:], jnp.full((1,), N_EXPERTS, I32)])
    next_expert = jnp.where(later < N_EXPERTS, later, -1).astype(I32)
    slot = ((jnp.cumsum((padded > 0).astype(I32)) - 1) % 2).astype(I32)
    y_rows = moe_experts(layer, blk_expert, n_valid, next_expert, slot, x_rows, w_gate, b_gate, w_up, b_up,
                         w_down, b_down)
    return moe_combine(lay, *tabs, pos, gates, x, gate2, y_rows, split)


QKV_TN = 256
N_QK_TILES = (ATT_HEADS + ATT_KV) * ATT_HD // QKV_TN


def _qkv_kernel(x_ref, g_ref, sc_ref, sh_ref, w_ref, nw_ref, cos_ref, sin_ref, o_ref):
    tm = x_ref.shape[0]
    h = _modnorm(x_ref[...], g_ref[...], sc_ref[...], sh_ref[...]).astype(BF16)
    r = lax.broadcasted_iota(I32, (QKV_TN, QKV_TN), 0) // ATT_HD
    c = lax.broadcasted_iota(I32, (QKV_TN, QKV_TN), 1) // ATT_HD
    head_mean = jnp.where(r == c, 1.0 / ATT_HD, 0.0).astype(BF16)
    lane = lax.broadcasted_iota(I32, (tm, QKV_TN), 1)
    half = ATT_HD // 4
    first = (lane % (2 * half)) < half
    acc_all = _dot(h, w_ref[...])
    tiles = [acc_all[:, QKV_TN * j:QKV_TN * (j + 1)] for j in range(w_ref.shape[1] // QKV_TN)]
    sq = jnp.concatenate([(t * t).astype(BF16) for t in tiles[:N_QK_TILES]], axis=0)
    ms_all = _dot(sq, head_mean)
    for j, acc in enumerate(tiles):
        cols = slice(QKV_TN * j, QKV_TN * (j + 1))
        if j >= N_QK_TILES:
            o_ref[:, cols] = acc
            continue
        qn = acc * lax.rsqrt(ms_all[tm * j:tm * (j + 1)] + EPS) * nw_ref[j]
        swapped = jnp.where(first, pltpu.roll(qn, QKV_TN - half, 1), pltpu.roll(qn, half, 1))
        o_ref[:, cols] = qn * cos_ref[...] + swapped * sin_ref[...]


def _rope_tables(sample_len):
    pos = np.arange(sample_len)
    half = ATT_HD // 4
    inv = (ROPE_THETA ** (-np.arange(half, dtype=np.float32) / half)).astype(np.float32)
    ang_r = (pos // GRID_W).astype(np.float32)[:, None] * inv[None, :]
    ang_c = (pos % GRID_W).astype(np.float32)[:, None] * inv[None, :]
    cos = np.concatenate([np.cos(ang_r)] * 2 + [np.cos(ang_c)] * 2, axis=1)
    sin = np.concatenate([-np.sin(ang_r), np.sin(ang_r), -np.sin(ang_c), np.sin(ang_c)], axis=1)
    rep = QKV_TN // ATT_HD
    return (jnp.asarray(np.tile(cos, (1, rep)), F32), jnp.asarray(np.tile(sin, (1, rep)), F32))


def qkv_proj(lay, x, g, sc, sh, w, q_norm, k_norm, tm=512):
    n_tok = x.shape[0]
    n = w.shape[1]
    nq = ATT_HEADS * ATT_HD // QKV_TN
    rep = QKV_TN // ATT_HD
    nw = jnp.concatenate([jnp.tile(jnp.tile(q_norm, rep)[None, :], (nq, 1)),
                          jnp.tile(jnp.tile(k_norm, rep)[None, :], (n // QKV_TN - nq, 1))], axis=0)
    cos, sin = _rope_tables(lay.sample_len)
    cos = jnp.concatenate([jnp.ones((tm, QKV_TN), F32), cos], axis=0)
    sin = jnp.concatenate([jnp.zeros((tm, QKV_TN), F32), sin], axis=0)
    assert lay.p_tok % tm == 0 and lay.sample_len % tm == 0
    mrow = lambda i: (lay.mod_row(i * tm), 0, 0)
    rrow = lambda i: (jnp.where(i * tm < lay.p_tok, 0, 1 + ((i * tm - lay.p_tok) % lay.sample_len) // tm), 0)
    nt = n // QKV_TN
    return pl.pallas_call(
        _qkv_kernel, grid=(n_tok // tm,),
        in_specs=[pl.BlockSpec((tm, D), lambda i: (i, 0)),
                  pl.BlockSpec((1, D), lambda i: (0, 0)),
                  pl.BlockSpec((None, 1, D), mrow), pl.BlockSpec((None, 1, D), mrow),
                  pl.BlockSpec((D, n), lambda i: (0, 0)),
                  pl.BlockSpec((nt, 1, QKV_TN), lambda i: (0, 0, 0)),
                  pl.BlockSpec((tm, QKV_TN), rrow), pl.BlockSpec((tm, QKV_TN), rrow)],
        out_specs=pl.BlockSpec((tm, n), lambda i: (i, 0)),
        out_shape=jax.ShapeDtypeStruct((n_tok, n), F32),
        compiler_params=_cp(("arbitrary",)), name="qkv_proj",
    )(x, g.reshape(1, D), sc, sh, w, nw.reshape(nt, 1, QKV_TN), cos, sin)


def _dup_group(x, g):
    blk = x[:, LANES * (g // 2):LANES * (g // 2 + 1)]
    if g % 2 == 1:
        blk = pltpu.roll(blk, ATT_HD, 1)
    lo = lax.broadcasted_iota(I32, blk.shape, 1) < ATT_HD
    low = jnp.where(lo, blk, 0.0)
    return low + pltpu.roll(low, ATT_HD, 1)


def _attend(q_ref, k_all, v_all, mask, sink_ref, o_ref):
    nq = q_ref.shape[0]
    lo = lax.broadcasted_iota(I32, (nq, LANES), 1) < ATT_HD
    first = lax.broadcasted_iota(I32, (2 * nq, 1), 0) < nq
    if mask is not None:
        mask = jnp.concatenate([mask, mask], axis=0)
    pairs_per_group = ATT_HEADS // ATT_KV // 2
    n_pairs = ATT_HEADS // 2
    kv = {}

    def group_kv(g):
        if g not in kv:
            kv[g] = (_dup_group(k_all, g).astype(BF16), _dup_group(v_all, g).astype(BF16))
        return kv[g]

    def scores(j):
        qp = q_ref[:, LANES * j:LANES * (j + 1)] * (ATT_HD ** -0.5)
        qs = jnp.concatenate([jnp.where(lo, qp, 0.0), jnp.where(lo, 0.0, qp)], axis=0)
        return _dot_nt(qs.astype(BF16), group_kv(j // pairs_per_group)[0])

    ahead = 2
    queue = [scores(j) for j in range(min(ahead, n_pairs))]
    for j in range(n_pairs):
        s = queue.pop(0)
        if j + ahead < n_pairs:
            queue.append(scores(j + ahead))
        if mask is not None:
            s = jnp.where(mask, s, NEG)
        sink = jnp.where(first, sink_ref[2 * j], sink_ref[2 * j + 1])
        m = jnp.maximum(jnp.max(s, axis=1, keepdims=True), sink)
        p = jnp.exp(s - m)
        den = jnp.sum(p, axis=1, keepdims=True) + jnp.exp(sink - m)
        o = _dot(p.astype(BF16), group_kv(j // pairs_per_group)[1]) / den
        o_ref[:, LANES * j:LANES * (j + 1)] = jnp.where(lo, o[:nq], o[nq:]).astype(o_ref.dtype)


def _attn_ctx_kernel(sink_ref, q_ref, k_ref, v_ref, o_ref):
    _attend(q_ref, k_ref[...], v_ref[...], None, sink_ref, o_ref)


def attn_context(lay, qkv, sinks):
    qw = ATT_HEADS * ATT_HD
    kw = ATT_KV * ATT_HD
    ln = lay.prompt_len
    grid_spec = pltpu.PrefetchScalarGridSpec(
        num_scalar_prefetch=0, grid=(lay.n_prompt,),
        in_specs=[pl.BlockSpec(memory_space=pltpu.SMEM),
                  pl.BlockSpec((ln, qw), lambda b: (b, 0)),
                  pl.BlockSpec((ln, kw), lambda b: (b, qw // kw)),
                  pl.BlockSpec((ln, kw), lambda b: (b, qw // kw + 1))],
        out_specs=pl.BlockSpec((ln, qw), lambda b: (b, 0)))
    return pl.pallas_call(
        _attn_ctx_kernel, grid_spec=grid_spec,
        out_shape=jax.ShapeDtypeStruct((lay.p_tok, qw), BF16),
        compiler_params=_cp(("arbitrary",)), name="attn_context",
    )(sinks, qkv, qkv, qkv)


def _attn_lat_kernel(sink_ref, q_ref, kp_ref, kc_ref, kn_ref, vp_ref, vc_ref, vn_ref, ck_ref, cv_ref, o_ref,
                     *, nblk):
    i = pl.program_id(1)
    bq = ATT_BLOCK
    nctx = ck_ref.shape[1]
    k_all = jnp.concatenate([kp_ref[...], kc_ref[...], kn_ref[...], ck_ref[0]], axis=0)
    v_all = jnp.concatenate([vp_ref[...], vc_ref[...], vn_ref[...], cv_ref[0]], axis=0)
    ns = 3 * bq + nctx
    r = lax.broadcasted_iota(I32, (bq, ns), 0)
    c = lax.broadcasted_iota(I32, (bq, ns), 1)
    rel = c - r
    first_key = jnp.where(i > 0, 0, bq)
    end_key = jnp.where(i < nblk - 1, 3 * bq, 2 * bq)
    band = (rel >= bq - WINDOW) & (rel <= bq + WINDOW) & (c >= first_key) & (c < end_key)
    mask = band | (c >= 3 * bq)
    _attend(q_ref, k_all, v_all, mask, sink_ref, o_ref)


def attn_latent(lay, qkv, cache_k, cache_v, sinks):
    qw = ATT_HEADS * ATT_HD
    kw = ATT_KV * ATT_HD
    bq = ATT_BLOCK
    nblk = lay.sample_len // bq
    b0 = lay.p_tok // bq
    nctx = cache_k.shape[1]
    rb = lambda b, i: b0 + b * nblk + i
    kspec = lambda cb, sh: pl.BlockSpec(
        (bq, kw), lambda b, i: (b0 + b * nblk + jnp.clip(i + sh, 0, nblk - 1), cb))
    kc, vc = qw // kw, qw // kw + 1
    grid_spec = pltpu.PrefetchScalarGridSpec(
        num_scalar_prefetch=0, grid=(lay.n_sample, nblk),
        in_specs=[pl.BlockSpec(memory_space=pltpu.SMEM),
                  pl.BlockSpec((bq, qw), lambda b, i: (rb(b, i), 0)),
                  kspec(kc, -1), kspec(kc, 0), kspec(kc, 1),
                  kspec(vc, -1), kspec(vc, 0), kspec(vc, 1),
                  pl.BlockSpec((1, nctx, kw), lambda b, i: (b, 0, 0)),
                  pl.BlockSpec((1, nctx, kw), lambda b, i: (b, 0, 0))],
        out_specs=pl.BlockSpec((bq, qw), lambda b, i: (b * nblk + i, 0)))
    return pl.pallas_call(
        functools.partial(_attn_lat_kernel, nblk=nblk), grid_spec=grid_spec,
        out_shape=jax.ShapeDtypeStruct((lay.n_sample * lay.sample_len, qw), BF16),
        compiler_params=_cp(("arbitrary", "arbitrary")), name="attn_latent",
    )(sinks, qkv, qkv, qkv, qkv, qkv, qkv, qkv,
      cache_k.reshape(lay.n_sample, nctx, kw), cache_v.reshape(lay.n_sample, nctx, kw))


def _forward(lay, x_prompt, x_sample, state_l0_ssd_fwd, state_l0_ssd_bwd, state_l0_gla_fwd, state_l0_gla_bwd,
             cache_l1_k, cache_l1_v, c, c_ctx, ada_w, ada_b, norm1, norm2,
             l0_w_in, l0_conv_w, l0_conv_b, l0_a_log, l0_dt_bias, l0_d_skip, l0_ssd_norm,
             l0_gate_w2, l0_gate_b, l0_gla_norm, l0_w_out,
             l1_w_qkv, l1_q_norm, l1_k_norm, l1_sinks, l1_w_out,
             router_w, router_b, exp_w_gate, exp_b_gate, exp_w_up, exp_b_up, exp_w_down, exp_b_down):
    np_, ns = lay.n_prompt, lay.n_sample
    x = (x_prompt.reshape(-1, D), x_sample.reshape(-1, D))
    cond8 = jnp.zeros((8, D), F32).at[0].set(c_ctx).at[1:1 + ns].set(c)
    mod = ada_table(cond8, ada_w, ada_b)
    mods = [[mod[l, :, p * D:(p + 1) * D].reshape(8, 1, D) for p in range(N_ADA)] for l in range(2)]

    def moe(l, xx, split=False):
        return moe_layer(lay, l, xx, norm2[l], mods[l][4], mods[l][3], mods[l][5], router_w[l], router_b[l],
                         exp_w_gate, exp_b_gate, exp_w_up, exp_b_up, exp_w_down, exp_b_down, split=split)

    sp = np.cumsum((SSD_INNER, SSD_INNER + 2 * SSD_GROUPS * SSD_STATE, 2 * SSD_HEADS,
                    GLA_HEADS * GLA_KEY_DIM, GLA_HEADS * GLA_KEY_DIM,
                    GLA_HEADS * GLA_VAL_DIM, GLA_HEADS * GLA_VAL_DIM, 2 * GLA_RANK))
    cols = lambda a, b: l0_w_in[:, a:b]
    w_main = jnp.concatenate([cols(0, sp[0]), cols(sp[4], sp[5]), cols(sp[5], sp[6]), cols(sp[0], sp[1]),
                              cols(sp[2], sp[3]), cols(sp[3], sp[4])], axis=1).astype(BF16)
    w_small = jnp.concatenate([cols(sp[1], sp[2]), cols(sp[6], sp[7]),
                               jnp.zeros((D, LANES - 2 * SSD_HEADS - 2 * GLA_RANK), F32)], axis=1)
    proj, small, small_t = norm_proj(lay, x, norm1[0], mods[0][1], mods[0][0], w_main, _hilo(w_small), 512,
                                      PJ_W // 2, BF16, SSD_CHUNK)
    y_n, o_n, (ssd_f, ssd_b, gla_f, gla_b) = l0_mixers(
        lay, proj, small, small_t, state_l0_ssd_fwd, state_l0_ssd_bwd, state_l0_gla_fwd, state_l0_gla_bwd,
        l0_conv_w, l0_conv_b, l0_a_log, l0_dt_bias, l0_d_skip, l0_ssd_norm,
        l0_gate_w2, l0_gate_b, l0_gla_norm)
    x = proj_residual(lay, [y_n, o_n], l0_w_out.astype(BF16), x, mods[0][2])
    x = moe(0, x)

    qkv = qkv_proj(lay, x, norm1[1], mods[1][1], mods[1][0], l1_w_qkv.astype(BF16), l1_q_norm, l1_k_norm)
    o_ctx = attn_context(lay, qkv, l1_sinks)
    o_lat = attn_latent(lay, qkv, cache_l1_k, cache_l1_v, l1_sinks)
    x = proj_residual(lay, [(o_ctx, o_lat)], l1_w_out.astype(BF16), x, mods[1][2])
    xp, xs = moe(1, x, split=True)

    qw = ATT_HEADS * ATT_HD
    kw = ATT_KV * ATT_HD
    return (xp.reshape(x_prompt.shape), xs.reshape(x_sample.shape),
            ssd_f[:np_].reshape(np_, SSD_HEADS, SSD_HEAD_DIM, SSD_STATE),
            ssd_b[:np_].reshape(np_, SSD_HEADS, SSD_HEAD_DIM, SSD_STATE),
            gla_f[:np_], gla_b[:np_],
            qkv[:lay.p_tok, qw:qw + kw].reshape(np_, lay.prompt_len, ATT_KV, ATT_HD),
            qkv[:lay.p_tok, qw + kw:].reshape(np_, lay.prompt_len, ATT_KV, ATT_HD))


def kernel(x_prompt, x_sample, state_l0_ssd_fwd, state_l0_ssd_bwd, state_l0_gla_fwd, state_l0_gla_bwd, cache_l1_k, cache_l1_v, c, c_ctx, ada_w, ada_b, norm1, norm2, l0_w_in, l0_conv_w, l0_conv_b, l0_a_log, l0_dt_bias, l0_d_skip, l0_ssd_norm, l0_gate_w2, l0_gate_b, l0_gla_norm, l0_w_out, l1_w_qkv, l1_q_norm, l1_k_norm, l1_sinks, l1_w_out, router_w, router_b, exp_w_gate, exp_b_gate, exp_w_up, exp_b_up, exp_w_down, exp_b_down):
    lay = Layout(x_prompt.shape[0], x_prompt.shape[1], x_sample.shape[0], x_sample.shape[1])
    return _forward(lay, x_prompt, x_sample, state_l0_ssd_fwd, state_l0_ssd_bwd, state_l0_gla_fwd,
                    state_l0_gla_bwd, cache_l1_k, cache_l1_v, c, c_ctx, ada_w, ada_b, norm1, norm2,
                    l0_w_in, l0_conv_w, l0_conv_b, l0_a_log, l0_dt_bias, l0_d_skip, l0_ssd_norm,
                    l0_gate_w2, l0_gate_b, l0_gla_norm, l0_w_out,
                    l1_w_qkv, l1_q_norm, l1_k_norm, l1_sinks, l1_w_out,
                    router_w, router_b, exp_w_gate, exp_b_gate, exp_w_up, exp_b_up, exp_w_down, exp_b_down)
```

```python
import functools
import math

import numpy as np
import jax
import jax.numpy as jnp
from jax import lax
from jax.experimental import pallas as pl
from jax.experimental.pallas import tpu as pltpu

F32 = jnp.float32
BF16 = jnp.bfloat16
I32 = jnp.int32
HI = lax.Precision.HIGHEST

D = 1024
EPS = 1e-6
N_ADA = 6
SSD_HEADS = 16
SSD_HEAD_DIM = 64
SSD_INNER = 1024
SSD_STATE = 128
SSD_GROUPS = 2
SSD_CONV = 5
SSD_CHUNK = 128
GLA_HEADS = 4
GLA_KEY_DIM = 128
GLA_VAL_DIM = 256
GLA_RANK = 16
GLA_TAU = 16.0
GLA_BLOCK = 64
ATT_HEADS = 16
ATT_KV = 4
ATT_HD = 64
ATT_BLOCK = 128
WINDOW = 128
GRID_W = 64
ROPE_THETA = 10000.0
N_EXPERTS = 32
TOP_K = 4
EXPERT_FF = 1024
SWIGLU_LIMIT = 7.0
SWIGLU_ALPHA = 1.702
MOE_BLOCK = 256
TOK_TILE = 256
LANES = 128
NEG = -1e30

PJ_Z, PJ_V, PJ_OG, PJ_XBC, PJ_Q, PJ_K = 0, 1024, 2048, 3072, 4608, 5120
PJ_W = 5632
VMEM_LIMIT = 48 * 1024 * 1024


def _cp(sem, vmem=VMEM_LIMIT):
    return pltpu.CompilerParams(dimension_semantics=sem, vmem_limit_bytes=vmem)


class Layout:
    def __init__(self, n_prompt, prompt_len, n_sample, sample_len):
        self.n_prompt, self.prompt_len = n_prompt, prompt_len
        self.n_sample, self.sample_len = n_sample, sample_len
        self.p_tok = n_prompt * prompt_len
        self.n_tok = self.p_tok + n_sample * sample_len
        self.seqs = [(i * prompt_len, prompt_len) for i in range(n_prompt)]
        self.seqs += [(self.p_tok + i * sample_len, sample_len) for i in range(n_sample)]
        self.n_seq = len(self.seqs)

    def mod_row(self, start):
        return jnp.where(start < self.p_tok, 0, 1 + (start - self.p_tok) // self.sample_len)

def _sigmoid(x):
    return 1.0 / (1.0 + jnp.exp(-x))


def _silu(x):
    return x * _sigmoid(x)


def _softplus(x):
    return jnp.maximum(x, 0.0) + jnp.log(1.0 + jnp.exp(-jnp.abs(x)))


def _modnorm(x, g, sc, sh):
    ms = jnp.mean(x * x, axis=-1, keepdims=True)
    return (x * lax.rsqrt(ms + EPS) * g) * (1.0 + sc) + sh


def _dot(a, b, **kw):
    return jnp.dot(a, b, preferred_element_type=F32, **kw)


def _dot_nt(a, b):
    return lax.dot_general(a, b, (((1,), (1,)), ((), ())), preferred_element_type=F32)


def _dot_tn(a, b):
    return lax.dot_general(a, b, (((0,), (0,)), ((), ())), preferred_element_type=F32)


def _split(x, n):
    parts = []
    for _ in range(n):
        p = x.astype(BF16)
        parts.append(p)
        x = x - p.astype(F32)
    return parts


def _dot_sel(sel, x):
    sel = sel.astype(BF16)
    return sum(_dot(sel, p) for p in _split(x, 3))


def _dot_sel_r(x, sel):
    sel = sel.astype(BF16)
    return sum(_dot(p, sel) for p in _split(x, 3))


def _dot_hilo(x, w_hi, w_lo):
    x_hi, x_lo = _split(x, 2)
    return _dot(x_hi, w_hi) + _dot(x_lo, w_hi) + _dot(x_hi, w_lo)


def _hilo(w):
    hi = w.astype(BF16)
    return jnp.stack([hi, (w - hi.astype(F32)).astype(BF16)])


U32 = jnp.uint32
ROW_WORDS = D // 2
_HI_MASK = 0xFFFF0000


def _pack_rows(x):
    lo = lax.bitcast_convert_type(x[:, :ROW_WORDS], U32) >> 16
    hi = lax.bitcast_convert_type(x[:, ROW_WORDS:], U32) & jnp.uint32(_HI_MASK)
    return lo | hi


def _unpack_rows(u):
    lo = lax.bitcast_convert_type(u << 16, F32)
    hi = lax.bitcast_convert_type(u & jnp.uint32(_HI_MASK), F32)
    return jnp.concatenate([lo, hi], axis=1)


def _ada_kernel(c_ref, w_ref, b_ref, o_ref):
    o_ref[0] = _dot(_silu(c_ref[...]), w_ref[0], precision=HI) + b_ref[0]


def ada_table(cond8, ada_w, ada_b):
    depth, _, n = ada_w.shape
    tn = 1536
    return pl.pallas_call(
        _ada_kernel, grid=(depth, n // tn),
        in_specs=[pl.BlockSpec((8, D), lambda l, j: (0, 0)),
                  pl.BlockSpec((1, D, tn), lambda l, j: (l, 0, j)),
                  pl.BlockSpec((1, 1, tn), lambda l, j: (l, 0, j))],
        out_specs=pl.BlockSpec((1, 8, tn), lambda l, j: (l, 0, j)),
        out_shape=jax.ShapeDtypeStruct((depth, 8, n), F32),
        compiler_params=_cp(("arbitrary", "arbitrary")), name="ada_table",
    )(cond8, ada_w, ada_b.reshape(depth, 1, n))


def _stream_specs(lay, stream, tm):
    if not isinstance(stream, (tuple, list)):
        return [stream], [pl.BlockSpec((tm, stream.shape[1]), lambda i, *_: (i, 0))]
    assert lay.p_tok % tm == 0
    npt = lay.p_tok // tm
    w = stream[0].shape[1]
    return list(stream), [pl.BlockSpec((tm, w), lambda i, *_: (jnp.minimum(i, npt - 1), 0)),
                          pl.BlockSpec((tm, w), lambda i, *_: (jnp.maximum(i - npt, 0), 0))]


def _stream_tile(refs, in_prompt):
    if len(refs) == 1:
        return refs[0][...]
    return jnp.where(in_prompt, refs[0][...], refs[1][...])


def _proj_kernel(*refs, nx, npt):
    x_refs = refs[:nx]
    g_ref, sc_ref, sh_ref, w_ref, ws_ref, o_ref, os_ref, ost_ref, h_scr = refs[nx:]

    @pl.when(pl.program_id(1) == 0)
    def _():
        x = _stream_tile(x_refs, pl.program_id(0) < npt)
        q = ost_ref.shape[2]
        half = x.shape[0] // 2
        for r0 in (0, half):
            rows = slice(r0, r0 + half)
            h = _modnorm(x[rows], g_ref[...], sc_ref[...], sh_ref[...])
            hb = h.astype(BF16)
            h_scr[rows, :] = hb
            o_ref[rows, :] = _dot(hb, w_ref[...]).astype(o_ref.dtype)
            small = _dot_hilo(h, ws_ref[0], ws_ref[1])
            os_ref[rows, :] = small
            for c in range(half // q):
                ost_ref[r0 // q + c] = small[q * c:q * (c + 1), :].T

    @pl.when(pl.program_id(1) > 0)
    def _():
        o_ref[...] = _dot(h_scr[...], w_ref[...]).astype(o_ref.dtype)


def norm_proj(lay, x, g, sc, sh, w, w_small, tm, tn, out_dtype, chunk):
    n_tok = lay.n_tok
    n = w.shape[1]
    ns = w_small.shape[-1]
    mrow = lambda i, j: (lay.mod_row(i * tm), 0, 0)
    xs, x_specs = _stream_specs(lay, x, tm)
    return pl.pallas_call(
        functools.partial(_proj_kernel, nx=len(xs), npt=lay.p_tok // tm), grid=(n_tok // tm, n // tn),
        in_specs=x_specs + [pl.BlockSpec((1, D), lambda i, j: (0, 0)),
                            pl.BlockSpec((None, 1, D), mrow),
                            pl.BlockSpec((None, 1, D), mrow),
                            pl.BlockSpec((D, tn), lambda i, j: (0, j)),
                            pl.BlockSpec((2, D, ns), lambda i, j: (0, 0, 0))],
        out_specs=[pl.BlockSpec((tm, tn), lambda i, j: (i, j)),
                   pl.BlockSpec((tm, ns), lambda i, j: (i, 0)),
                   pl.BlockSpec((tm // chunk, ns, chunk), lambda i, j: (i, 0, 0))],
        out_shape=[jax.ShapeDtypeStruct((n_tok, n), out_dtype),
                   jax.ShapeDtypeStruct((n_tok, ns), F32),
                   jax.ShapeDtypeStruct((n_tok // chunk, ns, chunk), F32)],
        scratch_shapes=[pltpu.VMEM((tm, D), BF16)],
        compiler_params=_cp(("arbitrary", "arbitrary")), name="norm_proj",
    )(*xs, g.reshape(1, D), sc, sh, w, w_small)


CONV_HALO = 16


def _ssd_load(d, c, xc, dtg_ref, dtgT_ref, S):
    q = SSD_CHUNK
    nh = SSD_HEADS
    rows = pl.ds(pl.multiple_of(c * q, q), q)
    return (rows, xc[rows, 0:SSD_INNER], xc[rows, SSD_INNER:SSD_INNER + 2 * SSD_GROUPS * SSD_STATE],
            dtg_ref[rows, nh * d:nh * d + nh], dtgT_ref[c, nh * d:nh * d + nh, :], S[...])


def _ssd_chunks(loaded, alog_ref, alogT_ref, dtb_ref, dtbT_ref):
    q = SSD_CHUNK
    nh = SSD_HEADS
    hp = nh * SSD_HEAD_DIM
    row = lax.broadcasted_iota(I32, (q, q), 0)
    col = lax.broadcasted_iota(I32, (q, q), 1)
    lo = lax.broadcasted_iota(I32, (q, LANES), 1) < SSD_HEAD_DIM
    head_of = lax.broadcasted_iota(I32, (nh, hp), 1) // SSD_HEAD_DIM
    spread = (lax.broadcasted_iota(I32, (nh, hp), 0) == head_of).astype(BF16)
    rep = (nh // SSD_GROUPS) // 2
    pairs = [(g, j) for g in range(SSD_GROUPS) for j in range(rep * g, rep * (g + 1))]
    lane = lambda j: slice(LANES * j, LANES * (j + 1))

    def per_lane(arr, passes):
        return sum(_dot(p, spread) for p in _split(arr, passes))

    stage1 = []
    for d, (_, _, _, dtg, dtgT, _) in enumerate(loaded):
        dt = _softplus(dtg + dtb_ref[d:d + 1, :])
        dtT = _softplus(dtgT + dtbT_ref[:, d:d + 1])
        ad = dt * (-jnp.exp(alog_ref[d:d + 1, :]))
        adT = dtT * (-jnp.exp(alogT_ref[:, d:d + 1]))
        if d == 0:
            e, eT = _dot_sel(col <= row, ad), _dot_sel_r(adT, row <= col)
        else:
            e, eT = _dot_sel(col < row, ad), _dot_sel_r(adT, row < col)
        stage1.append((dt, ad, e, eT))
    stage2 = []
    for d, ((_, xs, bc, _, _, s), (dt, ad, e, eT)) in enumerate(zip(loaded, stage1)):
        if d == 0:
            tot = e[q - 1:q, :]
            fq, fk = jnp.exp(e), jnp.exp(tot - e)
        else:
            tot = jnp.sum(ad, axis=0, keepdims=True)
            fq, fk = jnp.exp(tot - e), jnp.exp(e)
        dt_x, fq_x, fk_x = per_lane(dt, 1), per_lane(fq, 1), per_lane(fk, 1)
        dec_x = per_lane(jnp.broadcast_to(jnp.exp(tot), (8, nh)), 3)[0:1, :]
        xdt_all = xs.astype(F32) * dt_x
        xk_all = (xdt_all * fk_x).astype(BF16)
        bg_t = [bc[:, SSD_STATE * g:SSD_STATE * (g + 1)].T for g in range(SSD_GROUPS)]
        cg = [bc[:, SSD_STATE * (SSD_GROUPS + g):SSD_STATE * (SSD_GROUPS + g + 1)] for g in range(SSD_GROUPS)]
        gmat = [_dot(cg[g], bg_t[g]) for g in range(SSD_GROUPS)]
        carried = [_dot(cg[g], s[:, lane(j)].astype(BF16)) * fq_x[:, lane(j)] for g, j in pairs]
        s_new = [s[:, lane(j)] * dec_x[:, lane(j)] + _dot(bg_t[g], xk_all[:, lane(j)]) for g, j in pairs]
        stage2.append((xdt_all, gmat, carried, s_new))
    results = []
    for d, ((_, _, e, eT), (xdt_all, gmat, carried, s_new)) in enumerate(zip(stage1, stage2)):
        mask = row >= col if d == 0 else col >= row
        ys = []
        for (g, j), off in zip(pairs, carried):
            parts = []
            for hh in (2 * j, 2 * j + 1):
                diff = e[:, hh:hh + 1] - eT[hh:hh + 1, :] if d == 0 else eT[hh:hh + 1, :] - e[:, hh:hh + 1]
                parts.append((gmat[g] * jnp.exp(jnp.where(mask, diff, NEG))).astype(BF16))
            lhs = jnp.concatenate(parts, axis=1)
            xdt = xdt_all[:, lane(j)]
            rhs = jnp.concatenate([jnp.where(lo, xdt, 0.0), jnp.where(lo, 0.0, xdt)], axis=0)
            ys.append(_dot(lhs, rhs.astype(BF16)) + off)
        results.append((jnp.concatenate(ys, axis=1), jnp.concatenate(s_new, axis=1)))
    return results


def _ssd_seq_kernel(*refs, has_init):
    (xbc_ref, z_ref, dtg_ref, dtgT_ref, cw_ref, cb_ref,
     alog_ref, alogT_ref, dtb_ref, dtbT_ref, dskip_ref, nrm_ref) = refs[:12]
    refs = refs[12:]
    if has_init:
        s0f_ref, s0b_ref = refs[:2]
        refs = refs[2:]
    y_ref, sf_ref, sb_ref, xc, Sf, Sb, yf, yb = refs
    ln = xbc_ref.shape[0]
    q = SSD_CHUNK
    nc = ln // q
    h = CONV_HALO
    pad = SSD_CONV // 2

    def conv_body(c, carry):
        r0 = pl.multiple_of(c * q, q)
        prev = xbc_ref[pl.ds(pl.multiple_of(jnp.maximum(r0 - h, 0), h), h), :]
        nxt = xbc_ref[pl.ds(pl.multiple_of(jnp.minimum(r0 + q, ln - h), h), h), :]
        ext = jnp.concatenate([jnp.where(c > 0, prev, jnp.zeros_like(prev)), xbc_ref[pl.ds(r0, q), :],
                               jnp.where(c < nc - 1, nxt, jnp.zeros_like(nxt))], axis=0)
        src = lax.broadcasted_iota(I32, (q, q + 2 * h), 1) - lax.broadcasted_iota(I32, (q, q + 2 * h), 0)
        taps = [_dot((src == h - pad + k).astype(BF16), ext) for k in range(SSD_CONV)]
        acc = jnp.broadcast_to(cb_ref[...], (q, cb_ref.shape[1]))
        for k in range(SSD_CONV):
            acc = acc + cw_ref[k:k + 1, :] * taps[k]
        xc[pl.ds(r0, q), :] = _silu(acc).astype(xc.dtype)
        return carry
    lax.fori_loop(0, nc, conv_body, 0)

    if has_init:
        Sf[...] = s0f_ref[0].T
        Sb[...] = s0b_ref[0].T
    else:
        Sf[...] = jnp.zeros_like(Sf)
        Sb[...] = jnp.zeros_like(Sb)
    params = (alog_ref, alogT_ref, dtb_ref, dtbT_ref)

    def scan_body(c, carry):
        lf = _ssd_load(0, c, xc, dtg_ref, dtgT_ref, Sf)
        lb = _ssd_load(1, nc - 1 - c, xc, dtg_ref, dtgT_ref, Sb)
        (y_f, s_f), (y_b, s_b) = _ssd_chunks((lf, lb), *params)
        yf[lf[0], :] = y_f
        yb[lb[0], :] = y_b
        Sf[...] = s_f
        Sb[...] = s_b
        return carry
    lax.fori_loop(0, nc, scan_body, 0, unroll=2)
    sf_ref[0] = Sf[...].T
    sb_ref[0] = Sb[...].T

    def out_body(c, carry):
        rows = pl.ds(pl.multiple_of(c * q, q), q)
        ytot = yf[rows, :] + yb[rows, :] + dskip_ref[...] * xc[rows, 0:SSD_INNER].astype(F32)
        yg = ytot * _silu(z_ref[rows, :].astype(F32))
        ms = jnp.mean(yg * yg, axis=-1, keepdims=True)
        y_ref[rows, :] = (yg * lax.rsqrt(ms + EPS) * nrm_ref[...]).astype(y_ref.dtype)
        return carry
    lax.fori_loop(0, nc, out_body, 0)


def _ssd_call(n_seq, ln, blk0, proj, small, smallT3, init, params):
    q = SSD_CHUNK
    hp = SSD_HEADS * SSD_HEAD_DIM
    cw = SSD_INNER + 2 * SSD_GROUPS * SSD_STATE
    nc = ln // q
    assert PJ_XBC % cw == 0 and PJ_Z % SSD_INNER == 0
    tok = lambda w, cb: pl.BlockSpec((ln, w), lambda b: (blk0 + b, cb))
    seq3 = pl.BlockSpec((1, hp, SSD_STATE), lambda b: (b, 0, 0))
    full = lambda a: pl.BlockSpec(a.shape, lambda b: (0,) * a.ndim)
    init = () if init is None else tuple(init)
    return pl.pallas_call(
        functools.partial(_ssd_seq_kernel, has_init=bool(init)), grid=(n_seq,),
        in_specs=[tok(cw, PJ_XBC // cw), tok(SSD_INNER, PJ_Z // SSD_INNER), tok(LANES, 0),
                  pl.BlockSpec((nc, 2 * SSD_HEADS, q), lambda b: (blk0 + b, 0, 0))]
        + [full(a) for a in params] + [seq3] * len(init),
        out_specs=[pl.BlockSpec((ln, hp), lambda b: (b, 0)), seq3, seq3],
        out_shape=[jax.ShapeDtypeStruct((n_seq * ln, hp), BF16),
                   jax.ShapeDtypeStruct((n_seq, hp, SSD_STATE), F32),
                   jax.ShapeDtypeStruct((n_seq, hp, SSD_STATE), F32)],
        scratch_shapes=[pltpu.VMEM((ln, cw), BF16),
                        pltpu.VMEM((SSD_STATE, hp), F32), pltpu.VMEM((SSD_STATE, hp), F32),
                        pltpu.VMEM((ln, hp), F32), pltpu.VMEM((ln, hp), F32)],
        compiler_params=_cp(("arbitrary",)), name="ssd_seq",
    )(proj, proj, small, smallT3, *params, *init)


def _gla_gates(c, q_ref, k_ref, glr_ref, w2_ref, gb_ref, qi, ki, qo, kk, dec):
    t = GLA_BLOCK
    rows = pl.ds(pl.multiple_of(c * t, t), t)
    c0 = 2 * SSD_HEADS
    gps = [_dot_hilo(glr_ref[rows, c0 + GLA_RANK * d:c0 + GLA_RANK * (d + 1)], w2_ref[0, d], w2_ref[1, d])
           + gb_ref[d:d + 1, :] for d in (0, 1)]
    las = [-_softplus(-gp) * (1.0 / GLA_TAU) for gp in gps]
    row = lax.broadcasted_iota(I32, (t, t), 0)
    col = lax.broadcasted_iota(I32, (t, t), 1)
    es = [_dot_sel(col <= row, las[0]), _dot_sel(col < row, las[1])]
    mid = t // 2 - 1
    qf = q_ref[rows, :].astype(F32) * (GLA_KEY_DIM ** -0.5)
    kf = k_ref[rows, :].astype(F32)
    for d in (0, 1):
        e = es[d]
        r = e[mid:mid + 1, :]
        if d == 0:
            tot = e[t - 1:t, :]
            fqi, fki = jnp.exp(e - r), jnp.exp(r - e)
            fq, fk = jnp.exp(e), jnp.exp(tot - e)
        else:
            tot = e[t - 1:t, :] + las[1][t - 1:t, :]
            fqi, fki = jnp.exp(r - e), jnp.exp(e - r)
            fq, fk = jnp.exp(tot - e), jnp.exp(e)
        qi[d, rows, :] = (qf * fqi).astype(BF16)
        ki[d, rows, :] = (kf * fki).astype(BF16)
        qo[d, rows, :] = (qf * fq).astype(BF16)
        kk[d, rows, :] = (kf * fk).astype(BF16)
        dec[d, c] = jnp.broadcast_to(jnp.exp(tot), (8, tot.shape[1]))


def _gla_load(d, c, v_ref, qi, ki, qo, kk, dec, S):
    t = GLA_BLOCK
    rows = pl.ds(pl.multiple_of(c * t, t), t)
    return (rows, qi[d, rows, :], ki[d, rows, :], qo[d, rows, :], kk[d, rows, :], dec[d, c][0:1, :],
            v_ref[rows, :], [S[h] for h in range(GLA_HEADS)])


def _gla_blocks(loaded):
    t = GLA_BLOCK
    dk, dv = GLA_KEY_DIM, GLA_VAL_DIM
    row = lax.broadcasted_iota(I32, (t, t), 0)
    col = lax.broadcasted_iota(I32, (t, t), 1)
    heads = range(GLA_HEADS)
    ks = [slice(dk * h, dk * (h + 1)) for h in heads]
    first = []
    for _, q_in, k_in, q_st, k_st, dec, v, states in loaded:
        vs = [v[:, dv * h:dv * (h + 1)] for h in heads]
        scores = [_dot_nt(q_in[:, ks[h]], k_in[:, ks[h]]) for h in heads]
        carried = [_dot_nt(q_st[:, ks[h]], states[h].astype(BF16)) for h in heads]
        grown = [_dot_tn(vs[h], k_st[:, ks[h]]) for h in heads]
        first.append((vs, scores, carried, grown))
    results = []
    for d, ((_, _, _, _, _, dec, _, states), (vs, scores, carried, grown)) in enumerate(zip(loaded, first)):
        mask = row >= col if d == 0 else col >= row
        outs = [_dot(jnp.where(mask, scores[h], 0.0).astype(BF16), vs[h]) + carried[h] for h in heads]
        new_states = [states[h] * dec[:, ks[h]] + grown[h] for h in heads]
        results.append((jnp.concatenate(outs, axis=1), new_states))
    return results


def _gla_seq_kernel(*refs, has_init):
    q_ref, k_ref, v_ref, og_ref, glr_ref, w2_ref, gb_ref, nrm_ref = refs[:8]
    refs = refs[8:]
    if has_init:
        s0f_ref, s0b_ref = refs[:2]
        refs = refs[2:]
    o_ref, sf_ref, sb_ref, Sf, Sb, of, ob, qi, ki, qo, kk, dec = refs
    ln = q_ref.shape[0]
    t = GLA_BLOCK
    nc = ln // t
    dv = GLA_VAL_DIM
    for h in range(GLA_HEADS):
        if has_init:
            Sf[h] = s0f_ref[0, h].T
            Sb[h] = s0b_ref[0, h].T
        else:
            Sf[h] = jnp.zeros(Sf.shape[1:], F32)
            Sb[h] = jnp.zeros(Sb.shape[1:], F32)
    staged = (qi, ki, qo, kk, dec)

    def gate_body(c, carry):
        _gla_gates(c, q_ref, k_ref, glr_ref, w2_ref, gb_ref, *staged)
        return carry
    lax.fori_loop(0, nc, gate_body, 0, unroll=2)

    def scan_body(c, carry):
        lf = _gla_load(0, c, v_ref, *staged, Sf)
        lb = _gla_load(1, nc - 1 - c, v_ref, *staged, Sb)
        (o_f, s_f), (o_b, s_b) = _gla_blocks((lf, lb))
        of[lf[0], :] = o_f
        ob[lb[0], :] = o_b
        for h in range(GLA_HEADS):
            Sf[h] = s_f[h]
            Sb[h] = s_b[h]
        return carry
    lax.fori_loop(0, nc, scan_body, 0, unroll=2)
    for h in range(GLA_HEADS):
        sf_ref[0, h] = Sf[h].T
        sb_ref[0, h] = Sb[h].T

    def out_body(c, carry):
        rows = pl.ds(pl.multiple_of(c * t, t), t)
        for h in range(GLA_HEADS):
            vl = slice(dv * h, dv * (h + 1))
            ot = of[rows, vl] + ob[rows, vl]
            ms = jnp.mean(ot * ot, axis=-1, keepdims=True)
            on = ot * lax.rsqrt(ms + EPS) * nrm_ref[...]
            o_ref[rows, vl] = (on * _silu(og_ref[rows, vl].astype(F32))).astype(o_ref.dtype)
        return carry
    lax.fori_loop(0, nc, out_body, 0)


def _gla_call(n_seq, ln, blk0, proj, small, init, params):
    qk_w = GLA_HEADS * GLA_KEY_DIM
    v_w = GLA_HEADS * GLA_VAL_DIM
    tok = lambda w, cb: pl.BlockSpec((ln, w), lambda b: (blk0 + b, cb))
    seq4 = pl.BlockSpec((1, GLA_HEADS, GLA_KEY_DIM, GLA_VAL_DIM), lambda b: (b, 0, 0, 0))
    full = lambda a: pl.BlockSpec(a.shape, lambda b: (0,) * a.ndim)
    st_shape = jax.ShapeDtypeStruct((n_seq, GLA_HEADS, GLA_KEY_DIM, GLA_VAL_DIM), F32)
    init = () if init is None else tuple(init)
    return pl.pallas_call(
        functools.partial(_gla_seq_kernel, has_init=bool(init)), grid=(n_seq,),
        in_specs=[tok(qk_w, PJ_Q // qk_w), tok(qk_w, PJ_K // qk_w), tok(v_w, PJ_V // v_w),
                  tok(v_w, PJ_OG // v_w), tok(LANES, 0)] + [full(a) for a in params] + [seq4] * len(init),
        out_specs=[pl.BlockSpec((ln, v_w), lambda b: (b, 0)), seq4, seq4],
        out_shape=[jax.ShapeDtypeStruct((n_seq * ln, v_w), BF16), st_shape, st_shape],
        scratch_shapes=[pltpu.VMEM((GLA_HEADS, GLA_VAL_DIM, GLA_KEY_DIM), F32),
                        pltpu.VMEM((GLA_HEADS, GLA_VAL_DIM, GLA_KEY_DIM), F32),
                        pltpu.VMEM((ln, v_w), F32), pltpu.VMEM((ln, v_w), F32)]
        + [pltpu.VMEM((2, ln, qk_w), BF16)] * 4 + [pltpu.VMEM((2, ln // GLA_BLOCK, 8, qk_w), F32)],
        compiler_params=_cp(("arbitrary",)), name="gla_seq",
    )(proj, proj, proj, proj, small, *params, *init)


def l0_mixers(lay, proj, small, small_t, ssd_f0, ssd_b0, gla_f0, gla_b0, conv_w, conv_b, a_log, dt_bias, d_skip,
              ssd_norm, gate_w2, gate_b, gla_norm):
    hp = SSD_HEADS * SSD_HEAD_DIM
    ssd_p = (conv_w, conv_b.reshape(1, -1), a_log, a_log.T, dt_bias, dt_bias.T,
             jnp.repeat(d_skip, SSD_HEAD_DIM).reshape(1, hp), ssd_norm.reshape(1, hp))
    gla_p = (_hilo(gate_w2), gate_b, gla_norm.reshape(1, -1))
    np_, ns = lay.n_prompt, lay.n_sample
    assert lay.p_tok % lay.sample_len == 0
    groups = [(np_, lay.prompt_len, 0, None, None),
              (ns, lay.sample_len, lay.p_tok // lay.sample_len,
               (ssd_f0.reshape(ns, hp, SSD_STATE), ssd_b0.reshape(ns, hp, SSD_STATE)), (gla_f0, gla_b0))]
    ys, os_, states = [], [], None
    for n, ln, blk0, ssd_init, gla_init in groups:
        y, sf, sb = _ssd_call(n, ln, blk0, proj, small, small_t, ssd_init, ssd_p)
        o, gf, gb = _gla_call(n, ln, blk0, proj, small, gla_init, gla_p)
        ys.append(y)
        os_.append(o)
        if states is None:
            states = (sf, sb, gf, gb)
    return tuple(ys), tuple(os_), states


def _res_kernel(*refs, counts, ks, npt):
    in_prompt = pl.program_id(0) < npt
    streams, pos = [], 0
    for c in counts:
        streams.append(refs[pos:pos + c])
        pos += c
    w_ref, gate_ref, o_ref = refs[pos:]
    acc = None
    off = 0
    for a_refs, k in zip(streams[:-1], ks):
        part = _dot(_stream_tile(a_refs, in_prompt), w_ref[off:off + k, :])
        acc = part if acc is None else acc + part
        off += k
    o_ref[...] = _stream_tile(streams[-1], in_prompt) + gate_ref[...] * acc


def proj_residual(lay, acts, w, x, gate, tm=512):
    arrays, specs, counts = [], [], []
    for s in list(acts) + [x]:
        a, sp = _stream_specs(lay, s, tm)
        arrays += a
        specs += sp
        counts.append(len(a))
    ks = tuple(int((a[0] if isinstance(a, (tuple, list)) else a).shape[1]) for a in acts)
    mrow = lambda i: (lay.mod_row(i * tm), 0, 0)
    return pl.pallas_call(
        functools.partial(_res_kernel, counts=tuple(counts), ks=ks, npt=lay.p_tok // tm),
        grid=(lay.n_tok // tm,),
        in_specs=specs + [pl.BlockSpec(w.shape, lambda i: (0, 0)), pl.BlockSpec((None, 1, D), mrow)],
        out_specs=pl.BlockSpec((tm, D), lambda i: (i, 0)),
        out_shape=jax.ShapeDtypeStruct((lay.n_tok, D), F32),
        compiler_params=_cp(("arbitrary",)), name="proj_residual",
    )(*arrays, w, gate)


def _router_kernel(x_ref, g_ref, sc_ref, sh_ref, rw_ref, rb_ref,
                   h_ref, idx_ref, gate_ref, pos_ref, posT_ref, cnt_ref):
    tm = TOK_TILE
    for s in range(x_ref.shape[0] // tm):
        _route_tile(s, slice(tm * s, tm * (s + 1)), x_ref, g_ref, sc_ref, sh_ref, rw_ref, rb_ref,
                    h_ref, idx_ref, gate_ref, pos_ref, posT_ref, cnt_ref)


def _route_tile(s, rows, x_ref, g_ref, sc_ref, sh_ref, rw_ref, rb_ref,
                h_ref, idx_ref, gate_ref, pos_ref, posT_ref, cnt_ref):
    tm = TOK_TILE
    h = _modnorm(x_ref[rows, :], g_ref[...], sc_ref[...], sh_ref[...])
    h_hi = h.astype(BF16)
    h_ref[rows, :] = h_hi
    h_lo = (h - h_hi.astype(F32)).astype(BF16)
    lg = (_dot(h_hi, rw_ref[0]) + _dot(h_lo, rw_ref[0]) + _dot(h_hi, rw_ref[1])
          + rb_ref[...])
    lane = lax.broadcasted_iota(I32, (tm, LANES), 1).astype(F32)
    vals, ids = [], []
    for _ in range(TOP_K):
        m = jnp.max(lg, axis=1, keepdims=True)
        i = jnp.min(jnp.where(lg == m, lane, float(LANES)), axis=1, keepdims=True)
        vals.append(m)
        ids.append(i)
        lg = jnp.where(lane == i, -jnp.inf, lg)
    ex = [jnp.exp(v - vals[0]) for v in vals]
    den = ex[0] + ex[1] + ex[2] + ex[3]
    sel = jnp.zeros((tm, LANES), F32)
    for i in ids:
        sel = sel + (lane == i).astype(F32)
    row = lax.broadcasted_iota(I32, (tm, tm), 0)
    col = lax.broadcasted_iota(I32, (tm, tm), 1)
    before = _dot((col < row).astype(BF16), sel.astype(BF16))
    n = jnp.sum(sel, axis=0, keepdims=True)
    er = lax.broadcasted_iota(I32, (LANES, LANES), 0)
    ec = lax.broadcasted_iota(I32, (LANES, LANES), 1)
    n_al = jnp.ceil(n * (1.0 / SEG_ALIGN)) * SEG_ALIGN
    offs = _dot(jnp.broadcast_to(n_al, (8, LANES)).astype(BF16), (er < ec).astype(BF16))[0:1, :]
    slot = before + offs
    idx_o = jnp.zeros((tm, LANES), F32)
    gate_o = jnp.zeros((tm, LANES), F32)
    pos_o = jnp.zeros((tm, LANES), F32)
    for k in range(TOP_K):
        p = jnp.sum(jnp.where(lane == ids[k], slot, 0.0), axis=1, keepdims=True)
        idx_o = jnp.where(lane == k, ids[k], idx_o)
        gate_o = jnp.where(lane == k, ex[k] / den, gate_o)
        pos_o = jnp.where(lane == k, p, pos_o)
    idx_ref[rows, :] = idx_o.astype(I32)
    gate_ref[rows, :] = gate_o
    pos_ref[rows, :] = pos_o.astype(I32)
    posT_ref[:, rows] = pos_o.T[0:8, :]
    cnt_ref[s] = jnp.broadcast_to(n, (8, LANES))


ROUTER_TILES = 2


def moe_router(lay, x, g, sc, sh, rw, rb):
    n_tok = x.shape[0]
    tm = TOK_TILE * ROUTER_TILES
    nt = n_tok // TOK_TILE
    assert lay.p_tok % tm == 0 and lay.sample_len % tm == 0
    mrow = lambda i: (lay.mod_row(i * tm), 0, 0)
    tile = lambda w, dt: (pl.BlockSpec((tm, w), lambda i: (i, 0)), jax.ShapeDtypeStruct((n_tok, w), dt))
    outs = [tile(D, BF16), tile(LANES, I32), tile(LANES, F32), tile(LANES, I32),
            (pl.BlockSpec((8, tm), lambda i: (0, i)), jax.ShapeDtypeStruct((8, n_tok), F32)),
            (pl.BlockSpec((ROUTER_TILES, 8, LANES), lambda i: (i, 0, 0)),
             jax.ShapeDtypeStruct((nt, 8, LANES), F32))]
    return pl.pallas_call(
        _router_kernel, grid=(n_tok // tm,),
        in_specs=[pl.BlockSpec((tm, D), lambda i: (i, 0)),
                  pl.BlockSpec((1, D), lambda i: (0, 0)),
                  pl.BlockSpec((None, 1, D), mrow), pl.BlockSpec((None, 1, D), mrow),
                  pl.BlockSpec((2, D, LANES), lambda i: (0, 0, 0)),
                  pl.BlockSpec((1, LANES), lambda i: (0, 0))],
        out_specs=[o[0] for o in outs], out_shape=[o[1] for o in outs],
        compiler_params=_cp(("arbitrary",)), name="moe_router",
    )(x, g.reshape(1, D), sc, sh, rw, rb)


SEG_ALIGN = 8
SEG_CHUNK = 16
REST_BITS = tuple(range(int(math.log2(SEG_CHUNK)) - 1, int(math.log2(SEG_ALIGN)) - 1, -1))
TILE_ROWS = TOK_TILE * TOP_K + N_EXPERTS * SEG_ALIGN


def _pow2_copies(n, src, dst, make_copy, op, bits):
    for b in bits:
        sz = 1 << b
        done = (n >> (b + 1)) << (b + 1)

        @pl.when((n & sz) != 0)
        def _():
            op(make_copy(pl.multiple_of(src + done, SEG_ALIGN), pl.multiple_of(dst + done, SEG_ALIGN), sz))


def _start_segments(i, n_ref, off_ref, dst_ref, make_copy):
    def body(e, carry):
        k = i * N_EXPERTS + e
        n, src, dst = n_ref[k], off_ref[k], dst_ref[k]

        def chunk(j, c):
            o = pl.multiple_of(j * SEG_CHUNK, SEG_CHUNK)
            make_copy(pl.multiple_of(src + o, SEG_ALIGN), pl.multiple_of(dst + o, SEG_ALIGN),
                      SEG_CHUNK).start(priority=1)
            return c
        shift = int(math.log2(SEG_CHUNK))
        full = n >> shift
        lax.fori_loop(0, full, chunk, 0)
        done = full << shift
        _pow2_copies(n - done, src + done, dst + done, make_copy, lambda c: c.start(), REST_BITS)
        return carry
    lax.fori_loop(0, N_EXPERTS, body, 0)


TAIL_BITS = tuple(range(int(math.log2(MOE_BLOCK)) - 1, int(math.log2(SEG_ALIGN)) - 1, -1))
TILE_BITS = tuple(range(int(math.log2(TILE_ROWS)), int(math.log2(SEG_ALIGN)) - 1, -1))


def _wait_rows(total, make_copy):
    _pow2_copies(total, 0, 0, make_copy, lambda c: c.wait(), TILE_BITS)


def _dispatch_kernel(n_ref, off_ref, dst_ref, tot_ref, tn_ref, td_ref, posT_ref, h_ref, xout_ref,
                     srt, zbuf, sems):
    i = pl.program_id(0)
    last = pl.num_programs(0) - 1
    slot = i % 2
    tm = h_ref.shape[0]
    r = lax.broadcasted_iota(I32, (TILE_ROWS, tm), 0)
    hit = jnp.zeros((TILE_ROWS, tm), jnp.bool_)
    for k in range(TOP_K):
        hit = hit | (r == posT_ref[k:k + 1, :].astype(I32))
    sel = jnp.where(hit, 1.0, 0.0).astype(BF16)
    srt[slot] = _pack_rows(_dot(sel, h_ref[...]))

    def copier(s):
        def make_copy(src, dst, sz):
            return pltpu.make_async_copy(srt.at[s, pl.ds(src, sz)], xout_ref.at[pl.ds(dst, sz)], sems.at[s])
        return make_copy

    _start_segments(i, n_ref, off_ref, dst_ref, copier(slot))

    @pl.when(i > 0)
    def _():
        _wait_rows(tot_ref[jnp.maximum(i - 1, 0)], copier(1 - slot))

    @pl.when(i == last)
    def _():
        _wait_rows(tot_ref[i], copier(slot))
        zbuf[...] = jnp.zeros_like(zbuf)
        sem = sems.at[0]

        def zero_copy(src, dst, sz):
            return pltpu.make_async_copy(zbuf.at[pl.ds(src, sz)], xout_ref.at[pl.ds(dst, sz)], sem)

        nb = xout_ref.shape[0] // MOE_BLOCK
        for op in (lambda c: c.start(), lambda c: c.wait()):
            def body(e, carry):
                _pow2_copies(tn_ref[e], 0, td_ref[e], zero_copy, op, TAIL_BITS)
                return carry
            lax.fori_loop(0, N_EXPERTS, body, 0)

            def unused(b, carry):
                op(zero_copy(0, pl.multiple_of(b * MOE_BLOCK, MOE_BLOCK), MOE_BLOCK))
                return carry
            lax.fori_loop(tn_ref[N_EXPERTS], nb, unused, 0)


def moe_dispatch(n_tab, off_tab, dst_tab, tot_tab, tail_n, tail_dst, posT, h2, n_rows):
    n_tok = h2.shape[0]
    tm = TOK_TILE
    grid_spec = pltpu.PrefetchScalarGridSpec(
        num_scalar_prefetch=6, grid=(n_tok // tm,),
        in_specs=[pl.BlockSpec((8, tm), lambda i, *_: (0, i)),
                  pl.BlockSpec((tm, D), lambda i, *_: (i, 0))],
        out_specs=pl.BlockSpec(memory_space=pl.ANY),
        scratch_shapes=[pltpu.VMEM((2, TILE_ROWS, ROW_WORDS), U32), pltpu.VMEM((MOE_BLOCK, ROW_WORDS), U32),
                        pltpu.SemaphoreType.DMA((2,))])
    return pl.pallas_call(
        _dispatch_kernel, grid_spec=grid_spec,
        out_shape=jax.ShapeDtypeStruct((n_rows, ROW_WORDS), U32),
        compiler_params=_cp(("arbitrary",)), name="moe_dispatch",
    )(n_tab, off_tab, dst_tab, tot_tab, tail_n, tail_dst, posT, h2)


def _combine_kernel(n_ref, off_ref, dst_ref, tot_ref, pos_ref, gate_ref, x_ref, g2_ref, y_ref, *rest, npt):
    o_refs, (buf, sems) = rest[:-2], rest[-2:]
    i = pl.program_id(0)
    last = pl.num_programs(0) - 1
    slot = i % 2
    tm = x_ref.shape[0]
    na = TILE_ROWS

    def copier(s):
        def make_copy(src, dst, sz):
            return pltpu.make_async_copy(y_ref.at[pl.ds(dst, sz)], buf.at[s, pl.ds(src, sz)], sems.at[s])
        return make_copy

    def fetch(tile, s):
        buf[s, tm * TOP_K:na, :] = jnp.zeros((na - tm * TOP_K, ROW_WORDS), U32)
        _start_segments(tile, n_ref, off_ref, dst_ref, copier(s))

    @pl.when(i == 0)
    def _():
        fetch(i, slot)

    @pl.when(i < last)
    def _():
        fetch(i + 1, 1 - slot)

    _wait_rows(tot_ref[i], copier(slot))
    kc = 256
    mixed = None
    for k0 in range(0, na, kc):
        lane = lax.broadcasted_iota(I32, (tm, kc), 1) + k0
        pw = jnp.zeros((tm, kc), F32)
        for k in range(TOP_K):
            pw = pw + jnp.where(lane == pos_ref[:, k:k + 1], gate_ref[:, k:k + 1], 0.0)
        phi = pw.astype(BF16)
        plo = (pw - phi.astype(F32)).astype(BF16)
        yb = _unpack_rows(buf[slot, k0:k0 + kc, :]).astype(BF16)
        part = _dot(phi, yb) + _dot(plo, yb)
        mixed = part if mixed is None else mixed + part
    res = x_ref[...] + g2_ref[...] * mixed
    if len(o_refs) == 1:
        o_refs[0][...] = res
    else:
        @pl.when(i < npt)
        def _():
            o_refs[0][...] = res

        @pl.when(i >= npt)
        def _():
            o_refs[1][...] = res


def moe_combine(lay, n_tab, off_tab, dst_tab, tot_tab, pos, gates, x, gate2, y_rows, split):
    n_tok = x.shape[0]
    tm = TOK_TILE
    npt = lay.p_tok // tm
    mrow = lambda i, *_: (lay.mod_row(i * tm), 0, 0)
    if split:
        out_specs = [pl.BlockSpec((tm, D), lambda i, *_: (jnp.minimum(i, npt - 1), 0)),
                     pl.BlockSpec((tm, D), lambda i, *_: (jnp.maximum(i - npt, 0), 0))]
        out_shape = [jax.ShapeDtypeStruct((lay.p_tok, D), F32), jax.ShapeDtypeStruct((n_tok - lay.p_tok, D), F32)]
    else:
        out_specs = pl.BlockSpec((tm, D), lambda i, *_: (i, 0))
        out_shape = jax.ShapeDtypeStruct((n_tok, D), F32)
    grid_spec = pltpu.PrefetchScalarGridSpec(
        num_scalar_prefetch=4, grid=(n_tok // tm,),
        in_specs=[pl.BlockSpec((tm, LANES), lambda i, *_: (i, 0)),
                  pl.BlockSpec((tm, LANES), lambda i, *_: (i, 0)),
                  pl.BlockSpec((tm, D), lambda i, *_: (i, 0)),
                  pl.BlockSpec((None, 1, D), mrow),
                  pl.BlockSpec(memory_space=pl.ANY)],
        out_specs=out_specs,
        scratch_shapes=[pltpu.VMEM((2, TILE_ROWS, ROW_WORDS), U32), pltpu.SemaphoreType.DMA((2,))])
    return pl.pallas_call(
        functools.partial(_combine_kernel, npt=npt), grid_spec=grid_spec, out_shape=out_shape,
        compiler_params=_cp(("arbitrary",)), name="moe_combine",
    )(n_tab, off_tab, dst_tab, tot_tab, pos, gates, x, gate2, y_rows)


def _expert_kernel(be_ref, nv_ref, nxt_ref, slot_ref, x_ref, b_ref, wg_hbm, wu_hbm, wd_hbm,
                   y_ref, wf, sems, *, layer):
    i = pl.program_id(0)
    valid = i < nv_ref[0]
    e = be_ref[i]
    slot = slot_ref[e]
    changed = jnp.logical_or(i == 0, e != be_ref[jnp.maximum(i - 1, 0)])

    def weight_copies(ex, s):
        return [pltpu.make_async_copy(w.at[layer, ex], wf.at[s, k], sems.at[s, k])
                for k, w in enumerate((wg_hbm, wu_hbm, wd_hbm))]

    @pl.when(jnp.logical_and(valid, changed))
    def _():
        @pl.when(i == 0)
        def _():
            for c in weight_copies(e, slot):
                c.start()

        nxt = nxt_ref[e]

        @pl.when(nxt >= 0)
        def _():
            for c in weight_copies(nxt, 1 - slot):
                c.start(priority=1)

        for c in weight_copies(e, slot):
            c.wait()

    @pl.when(valid)
    def _():
        x = _unpack_rows(x_ref[...])
        b = b_ref[e]
        gt = jnp.minimum(_dot(x, wf[slot, 0]) + b[0:1, :], SWIGLU_LIMIT)
        up = jnp.clip(_dot(x, wf[slot, 1]) + b[1:2, :], -SWIGLU_LIMIT, SWIGLU_LIMIT)
        act = (up + 1.0) * gt * _sigmoid(SWIGLU_ALPHA * gt)
        y = _dot(act, wf[slot, 2]) + b[2:3, :]
        y_ref[...] = _pack_rows(y.astype(BF16).astype(F32))

    @pl.when(jnp.logical_not(valid))
    def _():
        y_ref[...] = jnp.zeros_like(y_ref)


def moe_experts(layer, blk_expert, n_valid, next_expert, slot, x_rows, w_gate, b_gate, w_up, b_up, w_down,
                b_down):
    n_rows = x_rows.shape[0]
    nb = n_rows // MOE_BLOCK
    depth, ne, _, ff = w_gate.shape
    assert ff == D
    rowblk = lambda i, be, nv, *_: (jnp.maximum(jnp.minimum(i, nv[0] - 1), 0), 0)
    hbm = pl.BlockSpec(memory_space=pl.ANY)
    biases = jnp.stack([b_gate, b_up, b_down], axis=2)
    grid_spec = pltpu.PrefetchScalarGridSpec(
        num_scalar_prefetch=4, grid=(nb,),
        in_specs=[pl.BlockSpec((MOE_BLOCK, ROW_WORDS), rowblk),
                  pl.BlockSpec((None, ne, 3, D), lambda i, *_: (layer, 0, 0, 0)), hbm, hbm, hbm],
        out_specs=pl.BlockSpec((MOE_BLOCK, ROW_WORDS), lambda i, *_: (i, 0)),
        scratch_shapes=[pltpu.VMEM((2, 3, D, ff), F32), pltpu.SemaphoreType.DMA((2, 3))])
    return pl.pallas_call(
        functools.partial(_expert_kernel, layer=layer), grid_spec=grid_spec,
        out_shape=jax.ShapeDtypeStruct((n_rows, ROW_WORDS), U32),
        compiler_params=_cp(("arbitrary",)), name="moe_experts",
    )(blk_expert, n_valid, next_expert, slot, x_rows, biases, w_gate, w_up, w_down)


def moe_layer(lay, layer, x, g2, sc2, sh2, gate2, router_w, router_b, w_gate, b_gate, w_up, b_up, w_down,
              b_down, split=False):
    n_tok = x.shape[0]
    nt = n_tok // TOK_TILE
    rw = jnp.zeros((D, LANES), F32).at[:, :N_EXPERTS].set(router_w)
    rw = _hilo(rw)
    rb = jnp.full((1, LANES), NEG, F32).at[0, :N_EXPERTS].set(router_b)
    h2, _, gates, pos, posT, cnt = moe_router(lay, x, g2, sc2, sh2, rw, rb)
    n_te = cnt[:, 0, :N_EXPERTS].astype(I32)
    n_te = (n_te + SEG_ALIGN - 1) // SEG_ALIGN * SEG_ALIGN
    totals = jnp.sum(n_te, axis=0)
    padded = (totals + MOE_BLOCK - 1) // MOE_BLOCK * MOE_BLOCK
    padded_end = jnp.cumsum(padded)
    pstart = padded_end - padded
    dst = pstart[None, :] + jnp.cumsum(n_te, axis=0) - n_te
    off = jnp.cumsum(n_te, axis=1) - n_te
    n_rows = nt * TILE_ROWS + N_EXPERTS * MOE_BLOCK
    nb = n_rows // MOE_BLOCK
    n_valid = (padded_end[-1] // MOE_BLOCK).astype(I32).reshape(1)
    bstart = jnp.minimum(jnp.arange(nb, dtype=I32), n_valid[0] - 1) * MOE_BLOCK
    blk_expert = jnp.minimum(jnp.sum((bstart[:, None] >= padded_end[None, :]).astype(I32), axis=1),
                             N_EXPERTS - 1).astype(I32)
    tabs = (n_te.reshape(-1).astype(I32), off.reshape(-1).astype(I32), dst.reshape(-1).astype(I32),
            jnp.sum(n_te, axis=1).astype(I32))
    tail_n = jnp.concatenate([(padded - totals).astype(I32), n_valid])
    x_rows = moe_dispatch(*tabs, tail_n, (pstart + totals).astype(I32), posT, h2, n_rows)
    owner = jnp.where(padded > 0, jnp.arange(N_EXPERTS, dtype=I32), N_EXPERTS)
    later = jnp.concatenate([lax.cummin(owner, axis=0, reverse=True)[1:], jnp.full((1,), N_EXPERTS, I32)])
    next_expert = jnp.where(later < N_EXPERTS, later, -1).astype(I32)
    slot = ((jnp.cumsum((padded > 0).astype(I32)) - 1) % 2).astype(I32)
    y_rows = moe_experts(layer, blk_expert, n_valid, next_expert, slot, x_rows, w_gate, b_gate, w_up, b_up,
                         w_down, b_down)
    return moe_combine(lay, *tabs, pos, gates, x, gate2, y_rows, split)


QKV_TN = 256
N_QK_TILES = (ATT_HEADS + ATT_KV) * ATT_HD // QKV_TN


def _qkv_kernel(x_ref, g_ref, sc_ref, sh_ref, w_ref, nw_ref, cos_ref, sin_ref, o_ref):
    tm = x_ref.shape[0]
    h = _modnorm(x_ref[...], g_ref[...], sc_ref[...], sh_ref[...]).astype(BF16)
    r = lax.broadcasted_iota(I32, (QKV_TN, QKV_TN), 0) // ATT_HD
    c = lax.broadcasted_iota(I32, (QKV_TN, QKV_TN), 1) // ATT_HD
    head_mean = jnp.where(r == c, 1.0 / ATT_HD, 0.0).astype(BF16)
    lane = lax.broadcasted_iota(I32, (tm, QKV_TN), 1)
    half = ATT_HD // 4
    first = (lane % (2 * half)) < half
    acc_all = _dot(h, w_ref[...])
    tiles = [acc_all[:, QKV_TN * j:QKV_TN * (j + 1)] for j in range(w_ref.shape[1] // QKV_TN)]
    sq = jnp.concatenate([(t * t).astype(BF16) for t in tiles[:N_QK_TILES]], axis=0)
    ms_all = _dot(sq, head_mean)
    for j, acc in enumerate(tiles):
        cols = slice(QKV_TN * j, QKV_TN * (j + 1))
        if j >= N_QK_TILES:
            o_ref[:, cols] = acc
            continue
        qn = acc * lax.rsqrt(ms_all[tm * j:tm * (j + 1)] + EPS) * nw_ref[j]
        swapped = jnp.where(first, pltpu.roll(qn, QKV_TN - half, 1), pltpu.roll(qn, half, 1))
        o_ref[:, cols] = qn * cos_ref[...] + swapped * sin_ref[...]


def _rope_tables(sample_len):
    pos = np.arange(sample_len)
    half = ATT_HD // 4
    inv = (ROPE_THETA ** (-np.arange(half, dtype=np.float32) / half)).astype(np.float32)
    ang_r = (pos // GRID_W).astype(np.float32)[:, None] * inv[None, :]
    ang_c = (pos % GRID_W).astype(np.float32)[:, None] * inv[None, :]
    cos = np.concatenate([np.cos(ang_r)] * 2 + [np.cos(ang_c)] * 2, axis=1)
    sin = np.concatenate([-np.sin(ang_r), np.sin(ang_r), -np.sin(ang_c), np.sin(ang_c)], axis=1)
    rep = QKV_TN // ATT_HD
    return (jnp.asarray(np.tile(cos, (1, rep)), F32), jnp.asarray(np.tile(sin, (1, rep)), F32))


def qkv_proj(lay, x, g, sc, sh, w, q_norm, k_norm, tm=512):
    n_tok = x.shape[0]
    n = w.shape[1]
    nq = ATT_HEADS * ATT_HD // QKV_TN
    rep = QKV_TN // ATT_HD
    nw = jnp.concatenate([jnp.tile(jnp.tile(q_norm, rep)[None, :], (nq, 1)),
                          jnp.tile(jnp.tile(k_norm, rep)[None, :], (n // QKV_TN - nq, 1))], axis=0)
    cos, sin = _rope_tables(lay.sample_len)
    cos = jnp.concatenate([jnp.ones((tm, QKV_TN), F32), cos], axis=0)
    sin = jnp.concatenate([jnp.zeros((tm, QKV_TN), F32), sin], axis=0)
    assert lay.p_tok % tm == 0 and lay.sample_len % tm == 0
    mrow = lambda i: (lay.mod_row(i * tm), 0, 0)
    rrow = lambda i: (jnp.where(i * tm < lay.p_tok, 0, 1 + ((i * tm - lay.p_tok) % lay.sample_len) // tm), 0)
    nt = n // QKV_TN
    return pl.pallas_call(
        _qkv_kernel, grid=(n_tok // tm,),
        in_specs=[pl.BlockSpec((tm, D), lambda i: (i, 0)),
                  pl.BlockSpec((1, D), lambda i: (0, 0)),
                  pl.BlockSpec((None, 1, D), mrow), pl.BlockSpec((None, 1, D), mrow),
                  pl.BlockSpec((D, n), lambda i: (0, 0)),
                  pl.BlockSpec((nt, 1, QKV_TN), lambda i: (0, 0, 0)),
                  pl.BlockSpec((tm, QKV_TN), rrow), pl.BlockSpec((tm, QKV_TN), rrow)],
        out_specs=pl.BlockSpec((tm, n), lambda i: (i, 0)),
        out_shape=jax.ShapeDtypeStruct((n_tok, n), F32),
        compiler_params=_cp(("arbitrary",)), name="qkv_proj",
    )(x, g.reshape(1, D), sc, sh, w, nw.reshape(nt, 1, QKV_TN), cos, sin)


def _dup_group(x, g):
    blk = x[:, LANES * (g // 2):LANES * (g // 2 + 1)]
    if g % 2 == 1:
        blk = pltpu.roll(blk, ATT_HD, 1)
    lo = lax.broadcasted_iota(I32, blk.shape, 1) < ATT_HD
    low = jnp.where(lo, blk, 0.0)
    return low + pltpu.roll(low, ATT_HD, 1)


def _attend(q_ref, k_all, v_all, mask, sink_ref, o_ref):
    nq = q_ref.shape[0]
    lo = lax.broadcasted_iota(I32, (nq, LANES), 1) < ATT_HD
    first = lax.broadcasted_iota(I32, (2 * nq, 1), 0) < nq
    if mask is not None:
        mask = jnp.concatenate([mask, mask], axis=0)
    pairs_per_group = ATT_HEADS // ATT_KV // 2
    n_pairs = ATT_HEADS // 2
    kv = {}

    def group_kv(g):
        if g not in kv:
            kv[g] = (_dup_group(k_all, g).astype(BF16), _dup_group(v_all, g).astype(BF16))
        return kv[g]

    def scores(j):
        qp = q_ref[:, LANES * j:LANES * (j + 1)] * (ATT_HD ** -0.5)
        qs = jnp.concatenate([jnp.where(lo, qp, 0.0), jnp.where(lo, 0.0, qp)], axis=0)
        return _dot_nt(qs.astype(BF16), group_kv(j // pairs_per_group)[0])

    ahead = 2
    queue = [scores(j) for j in range(min(ahead, n_pairs))]
    for j in range(n_pairs):
        s = queue.pop(0)
        if j + ahead < n_pairs:
            queue.append(scores(j + ahead))
        if mask is not None:
            s = jnp.where(mask, s, NEG)
        sink = jnp.where(first, sink_ref[2 * j], sink_ref[2 * j + 1])
        m = jnp.maximum(jnp.max(s, axis=1, keepdims=True), sink)
        p = jnp.exp(s - m)
        den = jnp.sum(p, axis=1, keepdims=True) + jnp.exp(sink - m)
        o = _dot(p.astype(BF16), group_kv(j // pairs_per_group)[1]) / den
        o_ref[:, LANES * j:LANES * (j + 1)] = jnp.where(lo, o[:nq], o[nq:]).astype(o_ref.dtype)


def _attn_ctx_kernel(sink_ref, q_ref, k_ref, v_ref, o_ref):
    _attend(q_ref, k_ref[...], v_ref[...], None, sink_ref, o_ref)


def attn_context(lay, qkv, sinks):
    qw = ATT_HEADS * ATT_HD
    kw = ATT_KV * ATT_HD
    ln = lay.prompt_len
    grid_spec = pltpu.PrefetchScalarGridSpec(
        num_scalar_prefetch=0, grid=(lay.n_prompt,),
        in_specs=[pl.BlockSpec(memory_space=pltpu.SMEM),
                  pl.BlockSpec((ln, qw), lambda b: (b, 0)),
                  pl.BlockSpec((ln, kw), lambda b: (b, qw // kw)),
                  pl.BlockSpec((ln, kw), lambda b: (b, qw // kw + 1))],
        out_specs=pl.BlockSpec((ln, qw), lambda b: (b, 0)))
    return pl.pallas_call(
        _attn_ctx_kernel, grid_spec=grid_spec,
        out_shape=jax.ShapeDtypeStruct((lay.p_tok, qw), BF16),
        compiler_params=_cp(("arbitrary",)), name="attn_context",
    )(sinks, qkv, qkv, qkv)


def _attn_lat_kernel(sink_ref, q_ref, kp_ref, kc_ref, kn_ref, vp_ref, vc_ref, vn_ref, ck_ref, cv_ref, o_ref,
                     *, nblk):
    i = pl.program_id(1)
    bq = ATT_BLOCK
    nctx = ck_ref.shape[1]
    k_all = jnp.concatenate([kp_ref[...], kc_ref[...], kn_ref[...], ck_ref[0]], axis=0)
    v_all = jnp.concatenate([vp_ref[...], vc_ref[...], vn_ref[...], cv_ref[0]], axis=0)
    ns = 3 * bq + nctx
    r = lax.broadcasted_iota(I32, (bq, ns), 0)
    c = lax.broadcasted_iota(I32, (bq, ns), 1)
    rel = c - r
    first_key = jnp.where(i > 0, 0, bq)
    end_key = jnp.where(i < nblk - 1, 3 * bq, 2 * bq)
    band = (rel >= bq - WINDOW) & (rel <= bq + WINDOW) & (c >= first_key) & (c < end_key)
    mask = band | (c >= 3 * bq)
    _attend(q_ref, k_all, v_all, mask, sink_ref, o_ref)


def attn_latent(lay, qkv, cache_k, cache_v, sinks):
    qw = ATT_HEADS * ATT_HD
    kw = ATT_KV * ATT_HD
    bq = ATT_BLOCK
    nblk = lay.sample_len // bq
    b0 = lay.p_tok // bq
    nctx = cache_k.shape[1]
    rb = lambda b, i: b0 + b * nblk + i
    kspec = lambda cb, sh: pl.BlockSpec(
        (bq, kw), lambda b, i: (b0 + b * nblk + jnp.clip(i + sh, 0, nblk - 1), cb))
    kc, vc = qw // kw, qw // kw + 1
    grid_spec = pltpu.PrefetchScalarGridSpec(
        num_scalar_prefetch=0, grid=(lay.n_sample, nblk),
        in_specs=[pl.BlockSpec(memory_space=pltpu.SMEM),
                  pl.BlockSpec((bq, qw), lambda b, i: (rb(b, i), 0)),
                  kspec(kc, -1), kspec(kc, 0), kspec(kc, 1),
                  kspec(vc, -1), kspec(vc, 0), kspec(vc, 1),
                  pl.BlockSpec((1, nctx, kw), lambda b, i: (b, 0, 0)),
                  pl.BlockSpec((1, nctx, kw), lambda b, i: (b, 0, 0))],
        out_specs=pl.BlockSpec((bq, qw), lambda b, i: (b * nblk + i, 0)))
    return pl.pallas_call(
        functools.partial(_attn_lat_kernel, nblk=nblk), grid_spec=grid_spec,
        out_shape=jax.ShapeDtypeStruct((lay.n_sample * lay.sample_len, qw), BF16),
        compiler_params=_cp(("arbitrary", "arbitrary")), name="attn_latent",
    )(sinks, qkv, qkv, qkv, qkv, qkv, qkv, qkv,
      cache_k.reshape(lay.n_sample, nctx, kw), cache_v.reshape(lay.n_sample, nctx, kw))


def _forward(lay, x_prompt, x_sample, state_l0_ssd_fwd, state_l0_ssd_bwd, state_l0_gla_fwd, state_l0_gla_bwd,
             cache_l1_k, cache_l1_v, c, c_ctx, ada_w, ada_b, norm1, norm2,
             l0_w_in, l0_conv_w, l0_conv_b, l0_a_log, l0_dt_bias, l0_d_skip, l0_ssd_norm,
             l0_gate_w2, l0_gate_b, l0_gla_norm, l0_w_out,
             l1_w_qkv, l1_q_norm, l1_k_norm, l1_sinks, l1_w_out,
             router_w, router_b, exp_w_gate, exp_b_gate, exp_w_up, exp_b_up, exp_w_down, exp_b_down):
    np_, ns = lay.n_prompt, lay.n_sample
    x = (x_prompt.reshape(-1, D), x_sample.reshape(-1, D))
    cond8 = jnp.zeros((8, D), F32).at[0].set(c_ctx).at[1:1 + ns].set(c)
    mod = ada_table(cond8, ada_w, ada_b)
    mods = [[mod[l, :, p * D:(p + 1) * D].reshape(8, 1, D) for p in range(N_ADA)] for l in range(2)]

    def moe(l, xx, split=False):
        return moe_layer(lay, l, xx, norm2[l], mods[l][4], mods[l][3], mods[l][5], router_w[l], router_b[l],
                         exp_w_gate, exp_b_gate, exp_w_up, exp_b_up, exp_w_down, exp_b_down, split=split)

    sp = np.cumsum((SSD_INNER, SSD_INNER + 2 * SSD_GROUPS * SSD_STATE, 2 * SSD_HEADS,
                    GLA_HEADS * GLA_KEY_DIM, GLA_HEADS * GLA_KEY_DIM,
                    GLA_HEADS * GLA_VAL_DIM, GLA_HEADS * GLA_VAL_DIM, 2 * GLA_RANK))
    cols = lambda a, b: l0_w_in[:, a:b]
    w_main = jnp.concatenate([cols(0, sp[0]), cols(sp[4], sp[5]), cols(sp[5], sp[6]), cols(sp[0], sp[1]),
                              cols(sp[2], sp[3]), cols(sp[3], sp[4])], axis=1).astype(BF16)
    w_small = jnp.concatenate([cols(sp[1], sp[2]), cols(sp[6], sp[7]),
                               jnp.zeros((D, LANES - 2 * SSD_HEADS - 2 * GLA_RANK), F32)], axis=1)
    proj, small, small_t = norm_proj(lay, x, norm1[0], mods[0][1], mods[0][0], w_main, _hilo(w_small), 512,
                                      PJ_W // 2, BF16, SSD_CHUNK)
    y_n, o_n, (ssd_f, ssd_b, gla_f, gla_b) = l0_mixers(
        lay, proj, small, small_t, state_l0_ssd_fwd, state_l0_ssd_bwd, state_l0_gla_fwd, state_l0_gla_bwd,
        l0_conv_w, l0_conv_b, l0_a_log, l0_dt_bias, l0_d_skip, l0_ssd_norm,
        l0_gate_w2, l0_gate_b, l0_gla_norm)
    x = proj_residual(lay, [y_n, o_n], l0_w_out.astype(BF16), x, mods[0][2])
    x = moe(0, x)

    qkv = qkv_proj(lay, x, norm1[1], mods[1][1], mods[1][0], l1_w_qkv.astype(BF16), l1_q_norm, l1_k_norm)
    o_ctx = attn_context(lay, qkv, l1_sinks)
    o_lat = attn_latent(lay, qkv, cache_l1_k, cache_l1_v, l1_sinks)
    x = proj_residual(lay, [(o_ctx, o_lat)], l1_w_out.astype(BF16), x, mods[1][2])
    xp, xs = moe(1, x, split=True)

    qw = ATT_HEADS * ATT_HD
    kw = ATT_KV * ATT_HD
    return (xp.reshape(x_prompt.shape), xs.reshape(x_sample.shape),
            ssd_f[:np_].reshape(np_, SSD_HEADS, SSD_HEAD_DIM, SSD_STATE),
            ssd_b[:np_].reshape(np_, SSD_HEADS, SSD_HEAD_DIM, SSD_STATE),
            gla_f[:np_], gla_b[:np_],
            qkv[:lay.p_tok, qw:qw + kw].reshape(np_, lay.prompt_len, ATT_KV, ATT_HD),
            qkv[:lay.p_tok, qw + kw:].reshape(np_, lay.prompt_len, ATT_KV, ATT_HD))


def kernel(x_prompt, x_sample, state_l0_ssd_fwd, state_l0_ssd_bwd, state_l0_gla_fwd, state_l0_gla_bwd, cache_l1_k, cache_l1_v, c, c_ctx, ada_w, ada_b, norm1, norm2, l0_w_in, l0_conv_w, l0_conv_b, l0_a_log, l0_dt_bias, l0_d_skip, l0_ssd_norm, l0_gate_w2, l0_gate_b, l0_gla_norm, l0_w_out, l1_w_qkv, l1_q_norm, l1_k_norm, l1_sinks, l1_w_out, router_w, router_b, exp_w_gate, exp_b_gate, exp_w_up, exp_b_up, exp_w_down, exp_b_down):
    lay = Layout(x_prompt.shape[0], x_prompt.shape[1], x_sample.shape[0], x_sample.shape[1])
    return _forward(lay, x_prompt, x_sample, state_l0_ssd_fwd, state_l0_ssd_bwd, state_l0_gla_fwd,
                    state_l0_gla_bwd, cache_l1_k, cache_l1_v, c, c_ctx, ada_w, ada_b, norm1, norm2,
                    l0_w_in, l0_conv_w, l0_conv_b, l0_a_log, l0_dt_bias, l0_d_skip, l0_ssd_norm,
                    l0_gate_w2, l0_gate_b, l0_gla_norm, l0_w_out,
                    l1_w_qkv, l1_q_norm, l1_k_norm, l1_sinks, l1_w_out,
                    router_w, router_b, exp_w_gate, exp_b_gate, exp_w_up, exp_b_up, exp_w_down, exp_b_down)
```
